```python
import math
import jax, jax.numpy as jnp
from jax import lax
import numpy as np

D_MODEL = 2048
BATCH = 8
SEQ = 2048
DEPTH = 4

N_MIXERS = 2
N_SSD_LAYERS = (DEPTH + 1) // 2
N_SB_LAYERS = DEPTH // 2
PLE_DIM = 256
SSD_EXPAND = 2
SSD_D_INNER = SSD_EXPAND * D_MODEL
SSD_HEAD_DIM = 64
SSD_N_HEADS = SSD_D_INNER // SSD_HEAD_DIM
SSD_N_GROUPS = 8
SSD_HEADS_PER_GROUP = SSD_N_HEADS // SSD_N_GROUPS
SSD_D_STATE = 128
SSD_D_CONV = 4
SSD_CHUNK = 128
SSD_CONV_DIM = SSD_D_INNER + 2 * SSD_N_GROUPS * SSD_D_STATE
SSD_IN_DIM = SSD_D_INNER + SSD_CONV_DIM + SSD_N_HEADS
SB_HEAD_DIM = 128
SB_N_HEADS = D_MODEL // SB_HEAD_DIM
SB_WIDTH = SB_N_HEADS * SB_HEAD_DIM
SB_QBLOCK = 128
NORM_EPS = 1e-6
GATED_NORM_EPS = 1e-5

kernel_name = "ssd_stickbreaking_interleaved_ple"


def rms_norm(x, w, eps=NORM_EPS):
    xf = x.astype(jnp.float32)
    y = xf * lax.rsqrt(jnp.mean(xf * xf, axis=-1, keepdims=True) + eps)
    return (y * w.astype(jnp.float32)).astype(x.dtype)


def causal_depthwise_conv(x, w, b):
    y = lax.conv_general_dilated(
        x, w[:, None, :], window_strides=(1,), padding=[(SSD_D_CONV - 1, 0)],
        dimension_numbers=("NWC", "WIO", "NWC"), feature_group_count=x.shape[-1])
    return y + b


def ssd_chunked_scan(xh, dt, a, bm, cm):
    b, s = xh.shape[0], xh.shape[1]
    nc = s // SSD_CHUNK
    L = SSD_CHUNK
    xdt = (xh * dt[..., None]).reshape(b, nc, L, SSD_N_GROUPS, SSD_HEADS_PER_GROUP, SSD_HEAD_DIM)
    adt = (dt * a).reshape(b, nc, L, SSD_N_GROUPS, SSD_HEADS_PER_GROUP)
    bm = bm.reshape(b, nc, L, SSD_N_GROUPS, SSD_D_STATE)
    cm = cm.reshape(b, nc, L, SSD_N_GROUPS, SSD_D_STATE)
    acum = jnp.cumsum(adt, axis=2)
    seg = acum[:, :, :, None] - acum[:, :, None, :]
    causal = jnp.tril(jnp.ones((L, L), dtype=bool))[None, None, :, :, None, None]
    decay = jnp.exp(jnp.where(causal, seg, -jnp.inf))
    scores = jnp.einsum("bclgn,bcsgn->bclsg", cm, bm)
    y_diag = jnp.einsum("bclsg,bclsgr,bcsgrp->bclgrp", scores, decay, xdt)
    decay_to_end = jnp.exp(acum[:, :, -1:] - acum)
    chunk_states = jnp.einsum("bclgn,bclgr,bclgrp->bcgrpn", bm, decay_to_end, xdt)
    chunk_decay = jnp.exp(acum[:, :, -1])

    def step(state, inp):
        cs, cd = inp
        return state * cd[..., None, None] + cs, state

    init = jnp.zeros((b, SSD_N_GROUPS, SSD_HEADS_PER_GROUP, SSD_HEAD_DIM, SSD_D_STATE), jnp.float32)
    _, prev_states = lax.scan(step, init, (jnp.moveaxis(chunk_states, 1, 0), jnp.moveaxis(chunk_decay, 1, 0)))
    prev_states = jnp.moveaxis(prev_states, 0, 1)
    y_off = jnp.einsum("bclgn,bcgrpn,bclgr->bclgrp", cm, prev_states, jnp.exp(acum))
    return (y_diag + y_off).reshape(b, s, SSD_N_GROUPS, SSD_HEADS_PER_GROUP, SSD_HEAD_DIM)


def ssd_branch(u, in_w, conv_w, conv_b, dt_bias, a_log, d_skip, gnorm_w, out_w):
    b, s, _ = u.shape
    proj = u @ in_w
    z = proj[..., :SSD_D_INNER]
    xbc = proj[..., SSD_D_INNER:SSD_D_INNER + SSD_CONV_DIM]
    dt_raw = proj[..., SSD_D_INNER + SSD_CONV_DIM:]
    xbc = jax.nn.silu(causal_depthwise_conv(xbc, conv_w, conv_b))
    nbc = SSD_N_GROUPS * SSD_D_STATE
    xs = xbc[..., :SSD_D_INNER].astype(jnp.float32).reshape(b, s, SSD_N_GROUPS, SSD_HEADS_PER_GROUP, SSD_HEAD_DIM)
    bm = xbc[..., SSD_D_INNER:SSD_D_INNER + nbc].astype(jnp.float32).reshape(b, s, SSD_N_GROUPS, SSD_D_STATE)
    cm = xbc[..., SSD_D_INNER + nbc:].astype(jnp.float32).reshape(b, s, SSD_N_GROUPS, SSD_D_STATE)
    dt = jax.nn.softplus(dt_raw.astype(jnp.float32) + dt_bias.astype(jnp.float32))
    dt = dt.reshape(b, s, SSD_N_GROUPS, SSD_HEADS_PER_GROUP)
    a = (-jnp.exp(a_log.astype(jnp.float32))).reshape(SSD_N_GROUPS, SSD_HEADS_PER_GROUP)
    y = ssd_chunked_scan(xs, dt, a, bm, cm)
    y = y + d_skip.astype(jnp.float32).reshape(SSD_N_GROUPS, SSD_HEADS_PER_GROUP)[..., None] * xs
    y = y.reshape(b, s, SSD_D_INNER) * jax.nn.silu(z.astype(jnp.float32))
    yg = y.reshape(b, s, SSD_N_GROUPS, SSD_D_INNER // SSD_N_GROUPS)
    yg = yg * lax.rsqrt(jnp.mean(yg * yg, axis=-1, keepdims=True) + GATED_NORM_EPS)
    y = (yg.reshape(b, s, SSD_D_INNER) * gnorm_w.astype(jnp.float32)).astype(u.dtype)
    return y @ out_w


def stick_breaking_attention(q, k, v):
    s = q.shape[2]
    scale = 1.0 / math.sqrt(SB_HEAD_DIM)
    outs = []
    for blk in range(s // SB_QBLOCK):
        t0 = blk * SB_QBLOCK
        kend = t0 + SB_QBLOCK
        z = jnp.einsum("bhtd,bhsd->bhts", q[:, :, t0:kend], k[:, :, :kend]) * scale
        t_idx = t0 + jnp.arange(SB_QBLOCK)[:, None]
        s_idx = jnp.arange(kend)[None, :]
        strict = s_idx < t_idx
        log_beta = jax.nn.log_sigmoid(z)
        log_1m_beta = jnp.where(strict, jax.nn.log_sigmoid(-z), 0.0)
        rest = lax.cumsum(log_1m_beta, axis=3, reverse=True) - log_1m_beta
        att = jnp.where(strict, jnp.exp(log_beta + rest), 0.0)
        outs.append(jnp.einsum("bhts,bhsd->bhtd", att, v[:, :, :kend]))
    return jnp.concatenate(outs, axis=2)


def sb_branch(u, in_w, qn_w, kn_w, out_w):
    b, s, _ = u.shape
    proj = u @ in_w
    q, k, v, g = jnp.split(proj, 4, axis=-1)
    def heads(t):
        return t.reshape(b, s, SB_N_HEADS, SB_HEAD_DIM)
    q = rms_norm(heads(q), qn_w).astype(jnp.float32).transpose(0, 2, 1, 3)
    k = rms_norm(heads(k), kn_w).astype(jnp.float32).transpose(0, 2, 1, 3)
    v = heads(v).astype(jnp.float32).transpose(0, 2, 1, 3)
    o = stick_breaking_attention(q, k, v).transpose(0, 2, 1, 3).reshape(b, s, SB_WIDTH)
    o = (o * jax.nn.silu(g.astype(jnp.float32))).astype(u.dtype)
    return o @ out_w


def _fwd_setup_inputs(seed: int = 0) -> dict:
    key = jax.random.key(seed)
    ks = jax.random.split(key, 24)
    f32 = jnp.float32
    nrm = lambda k, shape, sc: jax.random.normal(k, shape, f32) * sc
    dt0 = jnp.exp(jax.random.uniform(ks[7], (N_SSD_LAYERS, SSD_N_HEADS), f32)
                  * (math.log(0.1) - math.log(0.001)) + math.log(0.001))
    return {
        "x": nrm(ks[0], (BATCH, SEQ, D_MODEL), 1.0),
        "p": nrm(ks[1], (DEPTH, BATCH, SEQ, PLE_DIM), 1.0),
        "norm_w": 1.0 + nrm(ks[2], (DEPTH, D_MODEL), 0.02),
        "ssd_in_w": nrm(ks[3], (N_SSD_LAYERS, D_MODEL, SSD_IN_DIM), D_MODEL ** -0.5),
        "ssd_conv_w": nrm(ks[4], (N_SSD_LAYERS, SSD_D_CONV, SSD_CONV_DIM), SSD_D_CONV ** -0.5),
        "ssd_conv_b": nrm(ks[5], (N_SSD_LAYERS, SSD_CONV_DIM), 0.02),
        "ssd_dt_bias": dt0 + jnp.log(-jnp.expm1(-dt0)),
        "ssd_a_log": jnp.log(jax.random.uniform(ks[8], (N_SSD_LAYERS, SSD_N_HEADS), f32, 1.0, 16.0)),
        "ssd_d": 1.0 + nrm(ks[9], (N_SSD_LAYERS, SSD_N_HEADS), 0.02),
        "ssd_gnorm_w": 1.0 + nrm(ks[10], (N_SSD_LAYERS, SSD_D_INNER), 0.02),
        "ssd_out_w": nrm(ks[11], (N_SSD_LAYERS, SSD_D_INNER, D_MODEL), SSD_D_INNER ** -0.5),
        "sb_in_w": nrm(ks[12], (N_SB_LAYERS, D_MODEL, 4 * SB_WIDTH), D_MODEL ** -0.5),
        "sb_qn_w": 1.0 + nrm(ks[13], (N_SB_LAYERS, SB_HEAD_DIM), 0.02),
        "sb_kn_w": 1.0 + nrm(ks[14], (N_SB_LAYERS, SB_HEAD_DIM), 0.02),
        "sb_out_w": nrm(ks[15], (N_SB_LAYERS, SB_WIDTH, D_MODEL), SB_WIDTH ** -0.5),
        "ple_norm_w": 1.0 + nrm(ks[16], (DEPTH, D_MODEL), 0.02),
        "ple_gate_w": nrm(ks[17], (DEPTH, D_MODEL, D_MODEL), D_MODEL ** -0.5),
        "ple_proj_w": nrm(ks[18], (DEPTH, PLE_DIM, D_MODEL), 0.5 * PLE_DIM ** -0.5),
    }


def _fwd_reference(x, p, norm_w, ssd_in_w, ssd_conv_w, ssd_conv_b, ssd_dt_bias, ssd_a_log, ssd_d,
              ssd_gnorm_w, ssd_out_w, sb_in_w, sb_qn_w, sb_kn_w, sb_out_w,
              ple_norm_w, ple_gate_w, ple_proj_w):
    h = x
    for i in range(DEPTH):
        u = rms_norm(h, norm_w[i])
        j = i // N_MIXERS
        if i % N_MIXERS == 0:
            mix = ssd_branch(u, ssd_in_w[j], ssd_conv_w[j], ssd_conv_b[j], ssd_dt_bias[j],
                             ssd_a_log[j], ssd_d[j], ssd_gnorm_w[j], ssd_out_w[j])
        else:
            mix = sb_branch(u, sb_in_w[j], sb_qn_w[j], sb_kn_w[j], sb_out_w[j])
        h = h + mix
        gate = jax.nn.sigmoid((rms_norm(h, ple_norm_w[i]) @ ple_gate_w[i]).astype(jnp.float32))
        h = h + ((p[i] @ ple_proj_w[i]).astype(jnp.float32) * gate).astype(h.dtype)
    return h


import jax as _jax
import jax.numpy as _jnp

TWIN_FORMAT = 'train_step'
FWD_PARAMS = ['x', 'p', 'norm_w', 'ssd_in_w', 'ssd_conv_w', 'ssd_conv_b', 'ssd_dt_bias', 'ssd_a_log', 'ssd_d', 'ssd_gnorm_w', 'ssd_out_w', 'sb_in_w', 'sb_qn_w', 'sb_kn_w', 'sb_out_w', 'ple_norm_w', 'ple_gate_w', 'ple_proj_w']
TWIN_WEIGHTS = ['norm_w', 'ssd_in_w', 'ssd_conv_w', 'ssd_conv_b', 'ssd_dt_bias', 'ssd_a_log', 'ssd_d', 'ssd_gnorm_w', 'ssd_out_w', 'sb_in_w', 'sb_qn_w', 'sb_kn_w', 'sb_out_w', 'ple_norm_w', 'ple_gate_w', 'ple_proj_w']
TWIN_DIFF_INPUT = 'x'
TWIN_INPUTS = ['x', 'p', 'norm_w', 'ssd_in_w', 'ssd_conv_w', 'ssd_conv_b', 'ssd_dt_bias', 'ssd_a_log', 'ssd_d', 'ssd_gnorm_w', 'ssd_out_w', 'sb_in_w', 'sb_qn_w', 'sb_kn_w', 'sb_out_w', 'ple_norm_w', 'ple_gate_w', 'ple_proj_w', 'loss_target', 'm_norm_w', 'm_ssd_in_w', 'm_ssd_conv_w', 'm_ssd_conv_b', 'm_ssd_dt_bias', 'm_ssd_a_log', 'm_ssd_d', 'm_ssd_gnorm_w', 'm_ssd_out_w', 'm_sb_in_w', 'm_sb_qn_w', 'm_sb_kn_w', 'm_sb_out_w', 'm_ple_norm_w', 'm_ple_gate_w', 'm_ple_proj_w', 'v_norm_w', 'v_ssd_in_w', 'v_ssd_conv_w', 'v_ssd_conv_b', 'v_ssd_dt_bias', 'v_ssd_a_log', 'v_ssd_d', 'v_ssd_gnorm_w', 'v_ssd_out_w', 'v_sb_in_w', 'v_sb_qn_w', 'v_sb_kn_w', 'v_sb_out_w', 'v_ple_norm_w', 'v_ple_gate_w', 'v_ple_proj_w']
TWIN_OUTPUTS = ['loss', 'grad_x', 'grad_norm_w', 'grad_ssd_in_w', 'grad_ssd_conv_w', 'grad_ssd_conv_b', 'grad_ssd_dt_bias', 'grad_ssd_a_log', 'grad_ssd_d', 'grad_ssd_gnorm_w', 'grad_ssd_out_w', 'grad_sb_in_w', 'grad_sb_qn_w', 'grad_sb_kn_w', 'grad_sb_out_w', 'grad_ple_norm_w', 'grad_ple_gate_w', 'grad_ple_proj_w', 'delta_norm_w', 'delta_ssd_in_w', 'delta_ssd_conv_w', 'delta_ssd_conv_b', 'delta_ssd_dt_bias', 'delta_ssd_a_log', 'delta_ssd_d', 'delta_ssd_gnorm_w', 'delta_ssd_out_w', 'delta_sb_in_w', 'delta_sb_qn_w', 'delta_sb_kn_w', 'delta_sb_out_w', 'delta_ple_norm_w', 'delta_ple_gate_w', 'delta_ple_proj_w', 'new_m_norm_w', 'new_m_ssd_in_w', 'new_m_ssd_conv_w', 'new_m_ssd_conv_b', 'new_m_ssd_dt_bias', 'new_m_ssd_a_log', 'new_m_ssd_d', 'new_m_ssd_gnorm_w', 'new_m_ssd_out_w', 'new_m_sb_in_w', 'new_m_sb_qn_w', 'new_m_sb_kn_w', 'new_m_sb_out_w', 'new_m_ple_norm_w', 'new_m_ple_gate_w', 'new_m_ple_proj_w', 'new_v_norm_w', 'new_v_ssd_in_w', 'new_v_ssd_conv_w', 'new_v_ssd_conv_b', 'new_v_ssd_dt_bias', 'new_v_ssd_a_log', 'new_v_ssd_d', 'new_v_ssd_gnorm_w', 'new_v_ssd_out_w', 'new_v_sb_in_w', 'new_v_sb_qn_w', 'new_v_sb_kn_w', 'new_v_sb_out_w', 'new_v_ple_norm_w', 'new_v_ple_gate_w', 'new_v_ple_proj_w']
TWIN_LEAF_KINDS = {'loss': 'loss', 'grad_x': 'grad_x', 'grad_norm_w': 'grad_w', 'grad_ssd_in_w': 'grad_w', 'grad_ssd_conv_w': 'grad_w', 'grad_ssd_conv_b': 'grad_w', 'grad_ssd_dt_bias': 'grad_w', 'grad_ssd_a_log': 'grad_w', 'grad_ssd_d': 'grad_w', 'grad_ssd_gnorm_w': 'grad_w', 'grad_ssd_out_w': 'grad_w', 'grad_sb_in_w': 'grad_w', 'grad_sb_qn_w': 'grad_w', 'grad_sb_kn_w': 'grad_w', 'grad_sb_out_w': 'grad_w', 'grad_ple_norm_w': 'grad_w', 'grad_ple_gate_w': 'grad_w', 'grad_ple_proj_w': 'grad_w', 'delta_norm_w': 'delta_w', 'delta_ssd_in_w': 'delta_w', 'delta_ssd_conv_w': 'delta_w', 'delta_ssd_conv_b': 'delta_w', 'delta_ssd_dt_bias': 'delta_w', 'delta_ssd_a_log': 'delta_w', 'delta_ssd_d': 'delta_w', 'delta_ssd_gnorm_w': 'delta_w', 'delta_ssd_out_w': 'delta_w', 'delta_sb_in_w': 'delta_w', 'delta_sb_qn_w': 'delta_w', 'delta_sb_kn_w': 'delta_w', 'delta_sb_out_w': 'delta_w', 'delta_ple_norm_w': 'delta_w', 'delta_ple_gate_w': 'delta_w', 'delta_ple_proj_w': 'delta_w', 'new_m_norm_w': 'new_m', 'new_m_ssd_in_w': 'new_m', 'new_m_ssd_conv_w': 'new_m', 'new_m_ssd_conv_b': 'new_m', 'new_m_ssd_dt_bias': 'new_m', 'new_m_ssd_a_log': 'new_m', 'new_m_ssd_d': 'new_m', 'new_m_ssd_gnorm_w': 'new_m', 'new_m_ssd_out_w': 'new_m', 'new_m_sb_in_w': 'new_m', 'new_m_sb_qn_w': 'new_m', 'new_m_sb_kn_w': 'new_m', 'new_m_sb_out_w': 'new_m', 'new_m_ple_norm_w': 'new_m', 'new_m_ple_gate_w': 'new_m', 'new_m_ple_proj_w': 'new_m', 'new_v_norm_w': 'new_v', 'new_v_ssd_in_w': 'new_v', 'new_v_ssd_conv_w': 'new_v', 'new_v_ssd_conv_b': 'new_v', 'new_v_ssd_dt_bias': 'new_v', 'new_v_ssd_a_log': 'new_v', 'new_v_ssd_d': 'new_v', 'new_v_ssd_gnorm_w': 'new_v', 'new_v_ssd_out_w': 'new_v', 'new_v_sb_in_w': 'new_v', 'new_v_sb_qn_w': 'new_v', 'new_v_sb_kn_w': 'new_v', 'new_v_sb_out_w': 'new_v', 'new_v_ple_norm_w': 'new_v', 'new_v_ple_gate_w': 'new_v', 'new_v_ple_proj_w': 'new_v'}


def _forward(args):
    return _fwd_reference(*[args[k] for k in FWD_PARAMS])


def _output_shape():
    out = _jax.eval_shape(lambda: _forward(_fwd_setup_inputs(0)))
    return out.shape, out.dtype

N_MICROBATCH = 1
ADAM_LR = 0.001
ADAM_B1 = 0.9
ADAM_B2 = 0.999
ADAM_EPS = 1e-08
ADAM_WD = 0.01
ADAM_STEP = 10
PER_EXAMPLE_BATCH_AXIS = {'x': 0, 'p': 1, 'loss_target': 0}
SHARED_INPUTS = []
_WEIGHT_DTYPES = {'norm_w': _jnp.float32, 'ssd_in_w': _jnp.float32, 'ssd_conv_w': _jnp.float32, 'ssd_conv_b': _jnp.float32, 'ssd_dt_bias': _jnp.float32, 'ssd_a_log': _jnp.float32, 'ssd_d': _jnp.float32, 'ssd_gnorm_w': _jnp.float32, 'ssd_out_w': _jnp.float32, 'sb_in_w': _jnp.float32, 'sb_qn_w': _jnp.float32, 'sb_kn_w': _jnp.float32, 'sb_out_w': _jnp.float32, 'ple_norm_w': _jnp.float32, 'ple_gate_w': _jnp.float32, 'ple_proj_w': _jnp.float32}
MOMENT_SCALE = {'norm_w': 1.827880e+00, 'ssd_in_w': 1.407808e-01, 'ssd_conv_w': 2.310879e-01, 'ssd_conv_b': 7.333069e-01, 'ssd_dt_bias': 4.320980e-01, 'ssd_a_log': 1.766143e+00, 'ssd_d': 1.354068e+00, 'ssd_gnorm_w': 5.699953e+00, 'ssd_out_w': 5.547429e-01, 'sb_in_w': 9.658584e-02, 'sb_qn_w': 2.704609e+00, 'sb_kn_w': 2.703018e+00, 'sb_out_w': 9.425447e-02, 'ple_norm_w': 6.146442e-02, 'ple_gate_w': 2.890214e-02, 'ple_proj_w': 1.105456e-01}


def _to_microbatches(a, axis):
    t = _jnp.moveaxis(a, axis, 0)
    t = t.reshape((N_MICROBATCH, t.shape[0] // N_MICROBATCH) + t.shape[1:])
    return _jnp.moveaxis(t, 1, axis + 1)


def setup_inputs(seed: int = 0) -> dict:
    inp = _fwd_setup_inputs(seed)
    key = _jax.random.fold_in(_jax.random.key(seed), 7919)
    shape, _ = _output_shape()
    out = dict(inp)
    out["loss_target"] = _jax.random.normal(_jax.random.fold_in(key, 0), shape, _jnp.float32)
    for i, name in enumerate(TWIN_WEIGHTS):
        w = inp[name].astype(_jnp.float32)
        if MOMENT_SCALE is None:
            s = _jnp.sqrt(_jnp.mean(_jnp.square(w)) + 1e-30)
        else:
            s = MOMENT_SCALE[name]
        km, kv = _jax.random.split(_jax.random.fold_in(key, i + 1))
        out[name] = w
        out["m_" + name] = s * _jax.random.normal(km, w.shape, _jnp.float32)
        out["v_" + name] = (s * s) * _jax.random.uniform(kv, w.shape, _jnp.float32, 0.5, 1.5)
    if N_MICROBATCH > 1:
        for name, axis in PER_EXAMPLE_BATCH_AXIS.items():
            out[name] = _to_microbatches(out[name], axis)
    return {'x': out['x'], 'p': out['p'], 'norm_w': out['norm_w'], 'ssd_in_w': out['ssd_in_w'], 'ssd_conv_w': out['ssd_conv_w'], 'ssd_conv_b': out['ssd_conv_b'], 'ssd_dt_bias': out['ssd_dt_bias'], 'ssd_a_log': out['ssd_a_log'], 'ssd_d': out['ssd_d'], 'ssd_gnorm_w': out['ssd_gnorm_w'], 'ssd_out_w': out['ssd_out_w'], 'sb_in_w': out['sb_in_w'], 'sb_qn_w': out['sb_qn_w'], 'sb_kn_w': out['sb_kn_w'], 'sb_out_w': out['sb_out_w'], 'ple_norm_w': out['ple_norm_w'], 'ple_gate_w': out['ple_gate_w'], 'ple_proj_w': out['ple_proj_w'], 'loss_target': out['loss_target'], 'm_norm_w': out['m_norm_w'], 'm_ssd_in_w': out['m_ssd_in_w'], 'm_ssd_conv_w': out['m_ssd_conv_w'], 'm_ssd_conv_b': out['m_ssd_conv_b'], 'm_ssd_dt_bias': out['m_ssd_dt_bias'], 'm_ssd_a_log': out['m_ssd_a_log'], 'm_ssd_d': out['m_ssd_d'], 'm_ssd_gnorm_w': out['m_ssd_gnorm_w'], 'm_ssd_out_w': out['m_ssd_out_w'], 'm_sb_in_w': out['m_sb_in_w'], 'm_sb_qn_w': out['m_sb_qn_w'], 'm_sb_kn_w': out['m_sb_kn_w'], 'm_sb_out_w': out['m_sb_out_w'], 'm_ple_norm_w': out['m_ple_norm_w'], 'm_ple_gate_w': out['m_ple_gate_w'], 'm_ple_proj_w': out['m_ple_proj_w'], 'v_norm_w': out['v_norm_w'], 'v_ssd_in_w': out['v_ssd_in_w'], 'v_ssd_conv_w': out['v_ssd_conv_w'], 'v_ssd_conv_b': out['v_ssd_conv_b'], 'v_ssd_dt_bias': out['v_ssd_dt_bias'], 'v_ssd_a_log': out['v_ssd_a_log'], 'v_ssd_d': out['v_ssd_d'], 'v_ssd_gnorm_w': out['v_ssd_gnorm_w'], 'v_ssd_out_w': out['v_ssd_out_w'], 'v_sb_in_w': out['v_sb_in_w'], 'v_sb_qn_w': out['v_sb_qn_w'], 'v_sb_kn_w': out['v_sb_kn_w'], 'v_sb_out_w': out['v_sb_out_w'], 'v_ple_norm_w': out['v_ple_norm_w'], 'v_ple_gate_w': out['v_ple_gate_w'], 'v_ple_proj_w': out['v_ple_proj_w']}


def _loss(weights, diff, rest, loss_target):
    with _jax.named_scope("forward"):
        args = {**rest, TWIN_DIFF_INPUT: diff, **{k: w.astype(_WEIGHT_DTYPES[k]) for k, w in weights.items()}}
        y = _forward(args)
    with _jax.named_scope("loss_head"):
        err = _jnp.square(y.astype(_jnp.float32) - loss_target)
        return 0.5 * _jnp.sum(_jnp.mean(err, axis=-1)) if err.ndim else 0.5 * err


def _adamw(w, g, m, v):
    m = ADAM_B1 * m + (1.0 - ADAM_B1) * g
    v = ADAM_B2 * v + (1.0 - ADAM_B2) * _jnp.square(g)
    m_hat = m / (1.0 - ADAM_B1 ** ADAM_STEP)
    v_hat = v / (1.0 - ADAM_B2 ** ADAM_STEP)
    delta = -ADAM_LR * (m_hat / (_jnp.sqrt(v_hat) + ADAM_EPS) + ADAM_WD * w)
    return delta, m, v


def reference(x, p, norm_w, ssd_in_w, ssd_conv_w, ssd_conv_b, ssd_dt_bias, ssd_a_log, ssd_d, ssd_gnorm_w, ssd_out_w, sb_in_w, sb_qn_w, sb_kn_w, sb_out_w, ple_norm_w, ple_gate_w, ple_proj_w, loss_target, m_norm_w, m_ssd_in_w, m_ssd_conv_w, m_ssd_conv_b, m_ssd_dt_bias, m_ssd_a_log, m_ssd_d, m_ssd_gnorm_w, m_ssd_out_w, m_sb_in_w, m_sb_qn_w, m_sb_kn_w, m_sb_out_w, m_ple_norm_w, m_ple_gate_w, m_ple_proj_w, v_norm_w, v_ssd_in_w, v_ssd_conv_w, v_ssd_conv_b, v_ssd_dt_bias, v_ssd_a_log, v_ssd_d, v_ssd_gnorm_w, v_ssd_out_w, v_sb_in_w, v_sb_qn_w, v_sb_kn_w, v_sb_out_w, v_ple_norm_w, v_ple_gate_w, v_ple_proj_w):
    given = dict(x=x, p=p, norm_w=norm_w, ssd_in_w=ssd_in_w, ssd_conv_w=ssd_conv_w, ssd_conv_b=ssd_conv_b, ssd_dt_bias=ssd_dt_bias, ssd_a_log=ssd_a_log, ssd_d=ssd_d, ssd_gnorm_w=ssd_gnorm_w, ssd_out_w=ssd_out_w, sb_in_w=sb_in_w, sb_qn_w=sb_qn_w, sb_kn_w=sb_kn_w, sb_out_w=sb_out_w, ple_norm_w=ple_norm_w, ple_gate_w=ple_gate_w, ple_proj_w=ple_proj_w, loss_target=loss_target, m_norm_w=m_norm_w, m_ssd_in_w=m_ssd_in_w, m_ssd_conv_w=m_ssd_conv_w, m_ssd_conv_b=m_ssd_conv_b, m_ssd_dt_bias=m_ssd_dt_bias, m_ssd_a_log=m_ssd_a_log, m_ssd_d=m_ssd_d, m_ssd_gnorm_w=m_ssd_gnorm_w, m_ssd_out_w=m_ssd_out_w, m_sb_in_w=m_sb_in_w, m_sb_qn_w=m_sb_qn_w, m_sb_kn_w=m_sb_kn_w, m_sb_out_w=m_sb_out_w, m_ple_norm_w=m_ple_norm_w, m_ple_gate_w=m_ple_gate_w, m_ple_proj_w=m_ple_proj_w, v_norm_w=v_norm_w, v_ssd_in_w=v_ssd_in_w, v_ssd_conv_w=v_ssd_conv_w, v_ssd_conv_b=v_ssd_conv_b, v_ssd_dt_bias=v_ssd_dt_bias, v_ssd_a_log=v_ssd_a_log, v_ssd_d=v_ssd_d, v_ssd_gnorm_w=v_ssd_gnorm_w, v_ssd_out_w=v_ssd_out_w, v_sb_in_w=v_sb_in_w, v_sb_qn_w=v_sb_qn_w, v_sb_kn_w=v_sb_kn_w, v_sb_out_w=v_sb_out_w, v_ple_norm_w=v_ple_norm_w, v_ple_gate_w=v_ple_gate_w, v_ple_proj_w=v_ple_proj_w)
    weights = {n: given[n] for n in TWIN_WEIGHTS}
    shared = {n: given[n] for n in SHARED_INPUTS}
    per_example = {n: given[n] for n in ['x', 'p']}
    grad_fn = _jax.value_and_grad(_loss, argnums=(0, 1))

    def one_microbatch(ex, loss_target):
        ex = dict(ex)
        diff = ex.pop(TWIN_DIFF_INPUT)
        return grad_fn(weights, diff, {**shared, **ex}, loss_target)

    if N_MICROBATCH == 1:
        loss, (grad_w, grad_x) = one_microbatch(per_example, given["loss_target"])
    else:
        def body(carry, xs):
            loss_sum, grad_sum = carry
            l_k, (gw_k, gx_k) = one_microbatch(xs[0], xs[1])
            with _jax.named_scope("update"):
                return (loss_sum + l_k, _jax.tree.map(_jnp.add, grad_sum, gw_k)), gx_k

        init = (_jnp.zeros((), _jnp.float32), _jax.tree.map(_jnp.zeros_like, weights))
        (loss, grad_w), grad_x = _jax.lax.scan(body, init, (per_example, given["loss_target"]))
    with _jax.named_scope("update"):
        delta_w, new_m, new_v = {}, {}, {}
        for n in TWIN_WEIGHTS:
            delta_w[n], new_m[n], new_v[n] = _adamw(weights[n], grad_w[n], given["m_" + n], given["v_" + n])
    return (loss, grad_x, *[grad_w[n] for n in TWIN_WEIGHTS], *[delta_w[n] for n in TWIN_WEIGHTS],
            *[new_m[n] for n in TWIN_WEIGHTS], *[new_v[n] for n in TWIN_WEIGHTS])
```

```python
import functools
import math

import jax
import jax.numpy as jnp
from jax import lax
from jax.experimental import pallas as pl
from jax.experimental.pallas import tpu as pltpu

F32 = jnp.float32
BF16 = jnp.bfloat16
MESH = pl.DeviceIdType.MESH

N_DEV = 8
N_CHIP = 4
LANES = 128
VMEM_LIMIT_BYTES = 48 * 1024 * 1024

NORM_EPS = 1e-6
GATED_NORM_EPS = 1e-5
SSD_HEAD_DIM = 64
SSD_N_GROUPS = 8
SSD_D_STATE = 128
SSD_D_CONV = 4
SSD_CHUNK = 128
SB_HEAD_DIM = 128
PLE_DIM = 256

ADAM_LR = 0.001
ADAM_B1 = 0.9
ADAM_B2 = 0.999
ADAM_EPS = 1e-08
ADAM_WD = 0.01
ADAM_STEP = 10


def _cparams(sem=None, **kw):
    return pltpu.CompilerParams(dimension_semantics=sem, vmem_limit_bytes=VMEM_LIMIT_BYTES, **kw)


def _pick(dim, prefs):
    for t in prefs:
        if dim % t == 0:
            return t
    return dim


def _sigmoid(x):
    return 1.0 / (1.0 + jnp.exp(-x))


def _silu(x):
    return x * _sigmoid(x)


def _silu_grad(x):
    s = _sigmoid(x)
    return s * (1.0 + x * (1.0 - s))


def matmul(a, b, *, mode="nn", out_dtype=F32, res=None, out_blocks=None, name):
    b_blocked = b.ndim == 3
    if mode == "nn":
        m, kc = a.shape
        n = b.shape[-1] * (N_DEV if b_blocked else 1)
    elif mode == "nt":
        m, kc = a.shape
        n = b.shape[-2]
    else:
        kc, m = a.shape
        n = b.shape[-1]
    nb = b.shape[-1] if b_blocked else None
    tm = _pick(m, (1024, 512, 256, 128))
    tn = _pick(n if not out_blocks else out_blocks, (512, 256, 128))
    tk = _pick(kc, (512, 256, 128))
    if b_blocked:
        if mode == "nn":
            tn = _pick(nb, (512, 256, 128))
        else:
            tk = _pick(nb, (512, 256, 128))
    nk = kc // tk
    grid = (m // tm, n // tn, nk)

    if mode == "tn":
        a_spec = pl.BlockSpec((tk, tm), lambda i, j, k: (k, i))
        dims = (((0,), (0,)), ((), ()))
    else:
        a_spec = pl.BlockSpec((tm, tk), lambda i, j, k: (i, k))
        dims = (((1,), (0,)), ((), ())) if mode == "nn" else (((1,), (1,)), ((), ()))
    if mode == "nt":
        if b_blocked:
            per = nb // tk
            b_spec = pl.BlockSpec((None, tn, tk), lambda i, j, k: (k // per, j, k % per))
        else:
            b_spec = pl.BlockSpec((tn, tk), lambda i, j, k: (j, k))
    else:
        if b_blocked:
            per = nb // tn
            b_spec = pl.BlockSpec((None, tk, tn), lambda i, j, k: (j // per, k, j % per))
        else:
            b_spec = pl.BlockSpec((tk, tn), lambda i, j, k: (k, j))
    if out_blocks:
        per_o = out_blocks // tn
        out_shape = jax.ShapeDtypeStruct((n // out_blocks, m, out_blocks), out_dtype)
        out_spec = pl.BlockSpec((None, tm, tn), lambda i, j, k: (j // per_o, i, j % per_o))
    else:
        out_shape = jax.ShapeDtypeStruct((m, n), out_dtype)
        out_spec = pl.BlockSpec((tm, tn), lambda i, j, k: (i, j))
    in_specs = [a_spec, b_spec]
    args = [a, b]
    if res is not None:
        in_specs.append(pl.BlockSpec((tm, tn), lambda i, j, k: (i, j)))
        args.append(res)

    def body(*refs):
        if res is not None:
            a_ref, b_ref, r_ref, o_ref, acc_ref = refs
        else:
            a_ref, b_ref, o_ref, acc_ref = refs
        k = pl.program_id(2)

        @pl.when(k == 0)
        def _():
            acc_ref[...] = jnp.zeros_like(acc_ref)

        acc_ref[...] += lax.dot_general(a_ref[...].astype(BF16), b_ref[...].astype(BF16), dims,
                                        preferred_element_type=F32)

        @pl.when(k == nk - 1)
        def _():
            r = acc_ref[...]
            if res is not None:
                r = r + r_ref[...].astype(F32)
            o_ref[...] = r.astype(out_dtype)

    return pl.pallas_call(
        body, out_shape=out_shape, grid=grid, in_specs=in_specs, out_specs=out_spec,
        scratch_shapes=[pltpu.VMEM((tm, tn), F32)], name=name,
        compiler_params=_cparams(("parallel", "parallel", "arbitrary")),
    )(*args)


def _dot(a, b, dims, precision=None):
    return lax.dot_general(a, b, (dims, ((), ())), preferred_element_type=F32, precision=precision)


_NN = ((1,), (0,))
_NT = ((1,), (1,))
_TN = ((0,), (0,))
_EXACT = lax.Precision.HIGHEST


def _chunk_decay_terms(dt, a):
    ln = dt.shape[0]
    row = lax.broadcasted_iota(jnp.int32, (ln, ln), 0)
    col = lax.broadcasted_iota(jnp.int32, (ln, ln), 1)
    tri = (row >= col).astype(F32)
    a_col = _dot(tri, dt * a, _NN, _EXACT)
    return a_col, a_col.T, row >= col


def ssd_scan_fwd(xs, bm, cm, dtp, a_g, d_g, *, heads_per_group, name):
    s, di = xs.shape
    g_n = SSD_N_GROUPS
    r_n, p_n, n_n, ln = heads_per_group, SSD_HEAD_DIM, SSD_D_STATE, SSD_CHUNK
    nc = s // ln

    def body(xs_ref, bm_ref, cm_ref, dt_ref, a_ref, d_ref, y_ref, st_ref, state):
        c = pl.program_id(1)

        @pl.when(c == 0)
        def _():
            state[...] = jnp.zeros_like(state)

        dt = dt_ref[...]
        a_col_all, a_row_all, causal = _chunk_decay_terms(dt, a_ref[...])
        bmb = bm_ref[...].astype(BF16)
        cmb = cm_ref[...].astype(BF16)
        scores = _dot(cmb, bmb, _NT)
        d_all = d_ref[...]
        for r in range(r_n):
            a_col = a_col_all[:, r:r + 1]
            a_row = a_row_all[r:r + 1, :]
            decay = jnp.exp(jnp.where(causal, a_col - a_row, -jnp.inf))
            x_r = xs_ref[:, r * p_n:(r + 1) * p_n]
            xdt = x_r * dt[:, r:r + 1]
            s_r = state[r]
            st_ref[r] = s_r
            y = _dot((scores * decay).astype(BF16), xdt.astype(BF16), _NN)
            y = y + jnp.exp(a_col) * _dot(cmb, s_r.astype(BF16), _NT)
            y_ref[:, r * p_n:(r + 1) * p_n] = y + d_all[:, r:r + 1] * x_r
            a_last = a_col[ln - 1:ln, :]
            to_end = jnp.exp(a_last - a_col)
            state[r] = s_r * jnp.exp(a_last) + _dot((xdt * to_end).astype(BF16), bmb, _TN)

    return pl.pallas_call(
        body,
        out_shape=(jax.ShapeDtypeStruct((s, di), F32),
                   jax.ShapeDtypeStruct((nc, g_n * r_n, p_n, n_n), F32)),
        grid=(g_n, nc),
        in_specs=[pl.BlockSpec((ln, r_n * p_n), lambda g, c: (c, g)),
                  pl.BlockSpec((ln, n_n), lambda g, c: (c, g)),
                  pl.BlockSpec((ln, n_n), lambda g, c: (c, g)),
                  pl.BlockSpec((None, ln, LANES), lambda g, c: (g, c, 0)),
                  pl.BlockSpec((None, 1, LANES), lambda g, c: (g, 0, 0)),
                  pl.BlockSpec((None, 1, LANES), lambda g, c: (g, 0, 0))],
        out_specs=(pl.BlockSpec((ln, r_n * p_n), lambda g, c: (c, g)),
                   pl.BlockSpec((None, r_n, p_n, n_n), lambda g, c: (c, g, 0, 0))),
        scratch_shapes=[pltpu.VMEM((r_n, p_n, n_n), F32)],
        name=name, compiler_params=_cparams(("parallel", "arbitrary")),
    )(xs, bm, cm, dtp, a_g, d_g)


def ssd_scan_bwd(xs, bm, cm, dtp, a_g, d_g, states, dy, *, heads_per_group, name):
    s, di = xs.shape
    g_n = SSD_N_GROUPS
    r_n, p_n, n_n, ln = heads_per_group, SSD_HEAD_DIM, SSD_D_STATE, SSD_CHUNK
    nc = s // ln

    def body(xs_ref, bm_ref, cm_ref, dt_ref, a_ref, d_ref, st_ref, dy_ref,
             dxs_ref, dbm_ref, dcm_ref, ddt_ref, dadt_ref, dd_ref, dstate):
        c = pl.program_id(1)

        @pl.when(c == 0)
        def _():
            dstate[...] = jnp.zeros_like(dstate)
            dd_ref[...] = jnp.zeros_like(dd_ref)

        dt = dt_ref[...]
        a_col_all, a_row_all, causal = _chunk_decay_terms(dt, a_ref[...])
        bm_f = bm_ref[...]
        cm_f = cm_ref[...]
        bmb = bm_f.astype(BF16)
        cmb = cm_f.astype(BF16)
        scores = _dot(cmb, bmb, _NT)
        d_all = d_ref[...]
        lane = lax.broadcasted_iota(jnp.int32, (1, LANES), 1)
        sub = lax.broadcasted_iota(jnp.int32, (LANES, 1), 0)
        dscores = jnp.zeros((ln, ln), F32)
        dcm = jnp.zeros((ln, n_n), F32)
        dbm = jnp.zeros((ln, n_n), F32)
        da_cols = jnp.zeros((ln, LANES), F32)
        da_rows = jnp.zeros((LANES, ln), F32)
        da_last = jnp.zeros((1, LANES), F32)
        ddt = jnp.zeros((ln, LANES), F32)
        dd = jnp.zeros((1, LANES), F32)
        for r in range(r_n):
            pick_l = (lane == r).astype(F32)
            pick_s = (sub == r).astype(F32)
            a_col = a_col_all[:, r:r + 1]
            a_row = a_row_all[r:r + 1, :]
            decay = jnp.exp(jnp.where(causal, a_col - a_row, -jnp.inf))
            x_r = xs_ref[:, r * p_n:(r + 1) * p_n]
            dt_r = dt[:, r:r + 1]
            xdt = x_r * dt_r
            xdtb = xdt.astype(BF16)
            dy_r = dy_ref[:, r * p_n:(r + 1) * p_n]
            dyb = dy_r.astype(BF16)
            s_in = st_ref[r]
            ds_out = dstate[r]
            ds_outb = ds_out.astype(BF16)
            m_mat = scores * decay
            a_last = a_col[ln - 1:ln, :]
            e_last = jnp.exp(a_last)
            to_end = jnp.exp(a_last - a_col)
            e_col = jnp.exp(a_col)
            dm = _dot(dyb, xdtb, _NT)
            dscores = dscores + dm * decay
            e_mat = dm * m_mat
            da_cols = da_cols + jnp.sum(e_mat, axis=1, keepdims=True) * pick_l
            da_rows = da_rows - pick_s * jnp.sum(e_mat, axis=0, keepdims=True)
            dxdt = _dot(m_mat.astype(BF16), dyb, _TN)
            y_off = e_col * _dot(cmb, s_in.astype(BF16), _NT)
            dy_e = (dy_r * e_col).astype(BF16)
            dcm = dcm + _dot(dy_e, s_in.astype(BF16), _NN)
            ds_in = _dot(dy_e, cmb, _TN)
            da_cols = da_cols + jnp.sum(dy_r * y_off, axis=1, keepdims=True) * pick_l
            bds = _dot(bmb, ds_outb, _NT)
            dxdt = dxdt + to_end * bds
            xdt_e = xdt * to_end
            dbm = dbm + _dot(xdt_e.astype(BF16), ds_outb, _NN)
            w_col = jnp.sum(xdt_e * bds, axis=1, keepdims=True)
            da_cols = da_cols - w_col * pick_l
            last = jnp.sum(w_col, axis=0, keepdims=True) + e_last * jnp.sum(
                jnp.sum(ds_out * s_in, axis=1, keepdims=True), axis=0, keepdims=True)
            da_last = da_last + last * pick_l
            dstate[r] = ds_out * e_last + ds_in
            dxs_ref[:, r * p_n:(r + 1) * p_n] = dxdt * dt_r + d_all[:, r:r + 1] * dy_r
            ddt = ddt + jnp.sum(dxdt * x_r, axis=1, keepdims=True) * pick_l
            dd = dd + jnp.sum(jnp.sum(dy_r * x_r, axis=1, keepdims=True), axis=0, keepdims=True) * pick_l
        dsb = dscores.astype(BF16)
        dcm_ref[...] = dcm + _dot(dsb, bmb, _NN)
        dbm_ref[...] = dbm + _dot(dsb, cmb, _TN)
        da_total = da_cols + da_rows.T
        row = lax.broadcasted_iota(jnp.int32, (ln, ln), 0)
        col = lax.broadcasted_iota(jnp.int32, (ln, ln), 1)
        upper = (col >= row).astype(F32)
        dadt_ref[...] = _dot(upper, da_total, _NN, _EXACT) + da_last
        ddt_ref[...] = ddt
        dd_ref[...] += dd

    last_c = nc - 1
    return pl.pallas_call(
        body,
        out_shape=(jax.ShapeDtypeStruct((s, di), F32),
                   jax.ShapeDtypeStruct(bm.shape, F32),
                   jax.ShapeDtypeStruct(cm.shape, F32),
                   jax.ShapeDtypeStruct(dtp.shape, F32),
                   jax.ShapeDtypeStruct(dtp.shape, F32),
                   jax.ShapeDtypeStruct(d_g.shape, F32)),
        grid=(g_n, nc),
        in_specs=[pl.BlockSpec((ln, r_n * p_n), lambda g, c: (last_c - c, g)),
                  pl.BlockSpec((ln, n_n), lambda g, c: (last_c - c, g)),
                  pl.BlockSpec((ln, n_n), lambda g, c: (last_c - c, g)),
                  pl.BlockSpec((None, ln, LANES), lambda g, c: (g, last_c - c, 0)),
                  pl.BlockSpec((None, 1, LANES), lambda g, c: (g, 0, 0)),
                  pl.BlockSpec((None, 1, LANES), lambda g, c: (g, 0, 0)),
                  pl.BlockSpec((None, r_n, p_n, n_n), lambda g, c: (last_c - c, g, 0, 0)),
                  pl.BlockSpec((ln, r_n * p_n), lambda g, c: (last_c - c, g))],
        out_specs=(pl.BlockSpec((ln, r_n * p_n), lambda g, c: (last_c - c, g)),
                   pl.BlockSpec((ln, n_n), lambda g, c: (last_c - c, g)),
                   pl.BlockSpec((ln, n_n), lambda g, c: (last_c - c, g)),
                   pl.BlockSpec((None, ln, LANES), lambda g, c: (g, last_c - c, 0)),
                   pl.BlockSpec((None, ln, LANES), lambda g, c: (g, last_c - c, 0)),
                   pl.BlockSpec((None, 1, LANES), lambda g, c: (g, 0, 0))),
        scratch_shapes=[pltpu.VMEM((r_n, p_n, n_n), F32)],
        name=name, compiler_params=_cparams(("parallel", "arbitrary")),
    )(xs, bm, cm, dtp, a_g, d_g, states, dy)


SB_TK = 128


def _tri_sum(x, tri):
    hi = x.astype(BF16)
    r1 = x - hi.astype(F32)
    mid = r1.astype(BF16)
    lo = (r1 - mid.astype(F32)).astype(BF16)
    return _dot(hi, tri, _NN) + _dot(mid, tri, _NN) + _dot(lo, tri, _NN)


def _sb_logits(q, k_j, t0, s0, scale):
    z = _dot(q, k_j, _NT) * scale
    tq, tk = z.shape
    t_idx = t0 + lax.broadcasted_iota(jnp.int32, (tq, tk), 0)
    s_idx = s0 + lax.broadcasted_iota(jnp.int32, (tq, tk), 1)
    strict = s_idx < t_idx
    sp = jnp.log(1.0 + jnp.exp(-jnp.abs(z)))
    log_b = jnp.minimum(z, 0.0) - sp
    log_1mb = jnp.where(strict, -(jnp.maximum(z, 0.0) + sp), 0.0)
    return log_b, log_1mb, strict


def sb_attn_fwd(qn, kn, v, *, v_off=0, name):
    s, w = qn.shape
    dh = SB_HEAD_DIM
    n_h = w // dh
    tq = _pick(s, (256, 128))
    tk = SB_TK
    scale = 1.0 / math.sqrt(dh)

    def body(q_ref, k_ref, v_ref, o_ref, tot_ref):
        i = pl.program_id(1)
        q = q_ref[...]
        row = lax.broadcasted_iota(jnp.int32, (tk, tk), 0)
        col = lax.broadcasted_iota(jnp.int32, (tk, tk), 1)
        later = (row > col).astype(BF16)
        n_kb = (i + 1) * (tq // tk)

        def step(jj, carry):
            acc, run = carry
            j = n_kb - 1 - jj
            s0 = pl.multiple_of(j * tk, tk)
            k_j = k_ref[pl.ds(s0, tk), :]
            v_j = v_ref[pl.ds(s0, tk), :].astype(BF16)
            log_b, log_1mb, strict = _sb_logits(q, k_j, i * tq, s0, scale)
            rest = _tri_sum(log_1mb, later) + run
            att = jnp.where(strict, jnp.exp(log_b + rest), 0.0)
            acc = acc + _dot(att.astype(BF16), v_j, _NN)
            run = run + jnp.sum(log_1mb, axis=1, keepdims=True)
            return acc, run

        acc, run = lax.fori_loop(0, n_kb, step, (jnp.zeros((tq, dh), F32), jnp.zeros((tq, 1), F32)))
        o_ref[...] = acc
        tot_ref[...] = jnp.broadcast_to(run, (tq, dh))

    return pl.pallas_call(
        body,
        out_shape=(jax.ShapeDtypeStruct((s, w), F32), jax.ShapeDtypeStruct((s, w), F32)),
        grid=(n_h, s // tq),
        in_specs=[pl.BlockSpec((tq, dh), lambda h, i: (i, h)),
                  pl.BlockSpec((s, dh), lambda h, i: (0, h)),
                  pl.BlockSpec((s, dh), lambda h, i: (0, v_off + h))],
        out_specs=(pl.BlockSpec((tq, dh), lambda h, i: (i, h)),
                   pl.BlockSpec((tq, dh), lambda h, i: (i, h))),
        name=name, compiler_params=_cparams(("parallel", "parallel")),
    )(qn, kn, v)


def sb_attn_bwd(qn, kn, v, tot, do, *, v_off=0, name):
    s, w = qn.shape
    dh = SB_HEAD_DIM
    n_h = w // dh
    tq = _pick(s, (256, 128))
    tk = SB_TK
    scale = 1.0 / math.sqrt(dh)

    def body(q_ref, k_ref, v_ref, tot_ref, do_ref, dq_ref, dk_ref, dv_ref):
        dk_ref[...] = jnp.zeros_like(dk_ref)
        dv_ref[...] = jnp.zeros_like(dv_ref)
        row = lax.broadcasted_iota(jnp.int32, (tk, tk), 0)
        col = lax.broadcasted_iota(jnp.int32, (tk, tk), 1)
        upto = (row <= col).astype(BF16)
        before = (row < col).astype(BF16)

        def q_block(i, _):
            t0 = pl.multiple_of(i * tq, tq)
            q = q_ref[pl.ds(t0, tq), :]
            do_i = do_ref[pl.ds(t0, tq), :].astype(BF16)
            total = tot_ref[pl.ds(t0, tq), :]
            n_kb = (i + 1) * (tq // tk)

            def step(j, carry):
                dq, run_l, run_g = carry
                s0 = pl.multiple_of(j * tk, tk)
                k_j = k_ref[pl.ds(s0, tk), :]
                v_j = v_ref[pl.ds(s0, tk), :].astype(BF16)
                log_b, log_1mb, strict = _sb_logits(q, k_j, t0, s0, scale)
                rest = total - (_tri_sum(log_1mb, upto) + run_l)
                att = jnp.where(strict, jnp.exp(log_b + rest), 0.0)
                g = att * _dot(do_i, v_j, _NT)
                c = _tri_sum(g, before) + run_g
                beta = jnp.exp(log_b)
                dz = (jnp.where(strict, g * (1.0 - beta) - c * beta, 0.0) * scale).astype(BF16)
                dq = dq + _dot(dz, k_j, _NN)
                dk_ref[pl.ds(s0, tk), :] += _dot(dz, q, _TN)
                dv_ref[pl.ds(s0, tk), :] += _dot(att.astype(BF16), do_i, _TN)
                run_l = run_l + jnp.sum(log_1mb, axis=1, keepdims=True)
                run_g = run_g + jnp.sum(g, axis=1, keepdims=True)
                return dq, run_l, run_g

            zero = jnp.zeros((tq, 1), F32)
            dq, _, _ = lax.fori_loop(0, n_kb, step, (jnp.zeros((tq, dh), F32), zero, zero))
            dq_ref[pl.ds(t0, tq), :] = dq
            return 0

        lax.fori_loop(0, s // tq, q_block, 0)

    head = pl.BlockSpec((s, dh), lambda h: (0, h))
    return pl.pallas_call(
        body,
        out_shape=tuple(jax.ShapeDtypeStruct((s, w), F32) for _ in range(3)),
        grid=(n_h,),
        in_specs=[head, head, pl.BlockSpec((s, dh), lambda h: (0, v_off + h)), head, head],
        out_specs=(head, head, head),
        name=name, compiler_params=_cparams(("parallel",)),
    )(qn, kn, v, tot, do)


ROW_TILE = 256
WIDE_ROW_TILE = 64


def _rows(width, col=0, tm=ROW_TILE):
    return pl.BlockSpec((tm, width), lambda i: (i, col))


_wide_rows = functools.partial(_rows, tm=WIDE_ROW_TILE)


def _whole(shape):
    return pl.BlockSpec(shape, lambda i: (0,) * len(shape))


def _ew_call(body, out_shape, in_specs, out_specs, args, n_rows, name, carried=False):
    return pl.pallas_call(
        body, out_shape=out_shape, grid=(n_rows // in_specs[0].block_shape[0],), in_specs=in_specs, out_specs=out_specs,
        name=name, compiler_params=_cparams(("arbitrary",) if carried else ("parallel",)),
    )(*args)


def _first_step(*refs):
    @pl.when(pl.program_id(0) == 0)
    def _():
        for r in refs:
            r[...] = jnp.zeros_like(r)


def rmsnorm_fwd(x, w, *, name):
    s, d = x.shape

    def body(x_ref, w_ref, o_ref):
        xv = x_ref[...]
        r = lax.rsqrt(jnp.mean(xv * xv, axis=-1, keepdims=True) + NORM_EPS)
        o_ref[...] = (xv * r * w_ref[...]).astype(BF16)

    return _ew_call(body, jax.ShapeDtypeStruct((s, d), BF16), [_rows(d), _whole((1, d))], _rows(d),
                    (x, w.reshape(1, d)), s, name)


def rmsnorm_bwd(x, w, dy, dres, *, name):
    s, d = x.shape

    def body(x_ref, w_ref, dy_ref, dr_ref, dx_ref, dw_ref):
        _first_step(dw_ref)
        xv = x_ref[...]
        r = lax.rsqrt(jnp.mean(xv * xv, axis=-1, keepdims=True) + NORM_EPS)
        xhat = xv * r
        dyv = dy_ref[...].astype(F32)
        dw_ref[...] += jnp.sum(dyv * xhat, axis=0, keepdims=True)
        g = dyv * w_ref[...]
        dx_ref[...] = dr_ref[...] + r * (g - xhat * jnp.mean(g * xhat, axis=-1, keepdims=True))

    return _ew_call(body, (jax.ShapeDtypeStruct((s, d), F32), jax.ShapeDtypeStruct((1, d), F32)),
                    [_rows(d), _whole((1, d)), _rows(d), _rows(d)], (_rows(d), _whole((1, d))),
                    (x, w.reshape(1, d), dy, dres), s, name, carried=True)


def ple_fwd(h1, gate_pre, pp, *, name):
    s, d = h1.shape

    def body(h_ref, g_ref, p_ref, o_ref):
        o_ref[...] = h_ref[...] + p_ref[...] * _sigmoid(g_ref[...])

    return _ew_call(body, jax.ShapeDtypeStruct((s, d), F32), [_rows(d)] * 3, _rows(d), (h1, gate_pre, pp), s, name)


def ple_bwd(dh2, gate_pre, pp, *, name):
    s, d = dh2.shape

    def body(dh_ref, g_ref, p_ref, dp_ref, dg_ref):
        gate = _sigmoid(g_ref[...])
        dh = dh_ref[...]
        dp_ref[...] = (dh * gate).astype(BF16)
        dg_ref[...] = (dh * p_ref[...] * gate * (1.0 - gate)).astype(BF16)

    shp = jax.ShapeDtypeStruct((s, d), BF16)
    return _ew_call(body, (shp, shp), [_rows(d)] * 3, (_rows(d), _rows(d)), (dh2, gate_pre, pp), s, name)


def loss_head(y, target, *, name):
    s, d = y.shape

    def body(y_ref, t_ref, l_ref, dy_ref):
        _first_step(l_ref)
        err = y_ref[...] - t_ref[...]
        per_tok = jnp.mean(err * err, axis=-1, keepdims=True)
        l_ref[...] += 0.5 * jnp.sum(per_tok, axis=0, keepdims=True)
        dy_ref[...] = err * (1.0 / d)

    return _ew_call(body, (jax.ShapeDtypeStruct((1, 1), F32), jax.ShapeDtypeStruct((s, d), F32)),
                    [_rows(d), _rows(d)], (_whole((1, 1)), _rows(d)), (y, target), s, name, carried=True)


CONV_COL_TILE = 256


def _conv_taps(x, w_ref):
    row = lax.broadcasted_iota(jnp.int32, (x.shape[0], 1), 0)
    acc = x * w_ref[SSD_D_CONV - 1:SSD_D_CONV, :]
    shifted = []
    for d in range(1, SSD_D_CONV):
        xs = jnp.where(row >= d, pltpu.roll(x, d, 0), 0.0)
        shifted.append(xs)
        acc = acc + xs * w_ref[SSD_D_CONV - 1 - d:SSD_D_CONV - d, :]
    return acc, shifted


def ssd_conv_fwd(x, w, b, *, name):
    s, c = x.shape
    tc = _pick(c, (CONV_COL_TILE, LANES))

    def body(x_ref, w_ref, b_ref, o_ref):
        pre, _ = _conv_taps(x_ref[...], w_ref)
        o_ref[...] = _silu(pre + b_ref[...])

    col = pl.BlockSpec((s, tc), lambda j: (0, j))
    return pl.pallas_call(
        body, out_shape=jax.ShapeDtypeStruct((s, c), F32), grid=(c // tc,),
        in_specs=[col, pl.BlockSpec((SSD_D_CONV, tc), lambda j: (0, j)), pl.BlockSpec((1, tc), lambda j: (0, j))],
        out_specs=col, name=name, compiler_params=_cparams(("parallel",)),
    )(x, w, b)


def ssd_conv_bwd(x, w, b, dact, *, name):
    s, c = x.shape
    tc = _pick(c, (CONV_COL_TILE, LANES))

    def body(x_ref, w_ref, b_ref, da_ref, dx_ref, dw_ref, db_ref):
        xv = x_ref[...]
        pre, shifted = _conv_taps(xv, w_ref)
        dpre = da_ref[...] * _silu_grad(pre + b_ref[...])
        db_ref[...] = jnp.sum(dpre, axis=0, keepdims=True)
        row = lax.broadcasted_iota(jnp.int32, (s, 1), 0)
        dx = dpre * w_ref[SSD_D_CONV - 1:SSD_D_CONV, :]
        dw_ref[SSD_D_CONV - 1:SSD_D_CONV, :] = jnp.sum(dpre * xv, axis=0, keepdims=True)
        for d in range(1, SSD_D_CONV):
            k = SSD_D_CONV - 1 - d
            dw_ref[k:k + 1, :] = jnp.sum(dpre * shifted[d - 1], axis=0, keepdims=True)
            up = jnp.where(row < s - d, pltpu.roll(dpre, s - d, 0), 0.0)
            dx = dx + up * w_ref[k:k + 1, :]
        dx_ref[...] = dx.astype(BF16)

    col = pl.BlockSpec((s, tc), lambda j: (0, j))
    wspec = pl.BlockSpec((SSD_D_CONV, tc), lambda j: (0, j))
    bspec = pl.BlockSpec((1, tc), lambda j: (0, j))
    return pl.pallas_call(
        body,
        out_shape=(jax.ShapeDtypeStruct((s, c), BF16), jax.ShapeDtypeStruct((SSD_D_CONV, c), F32),
                   jax.ShapeDtypeStruct((1, c), F32)),
        grid=(c // tc,), in_specs=[col, wspec, bspec, col], out_specs=(col, wspec, bspec),
        name=name, compiler_params=_cparams(("parallel",)),
    )(x, w, b, dact)


def ssd_dt_fwd(dt_raw, bias, a_log, *, name):
    s, h = dt_raw.shape

    def body(r_ref, b_ref, al_ref, dt_ref, a_ref):
        zv = r_ref[...] + b_ref[...]
        dt_ref[...] = jnp.maximum(zv, 0.0) + jnp.log(1.0 + jnp.exp(-jnp.abs(zv)))
        a_ref[...] = -jnp.exp(al_ref[...])

    full = pl.BlockSpec((s, h), lambda: (0, 0))
    vec = pl.BlockSpec((1, h), lambda: (0, 0))
    return pl.pallas_call(
        body, out_shape=(jax.ShapeDtypeStruct((s, h), F32), jax.ShapeDtypeStruct((1, h), F32)),
        in_specs=[full, vec, vec], out_specs=(full, vec), name=name, compiler_params=_cparams(),
    )(dt_raw, bias.reshape(1, h), a_log.reshape(1, h))


def ssd_dt_bwd(dt_raw, bias, a_log, dt, ddt, dadt, *, name):
    s, h = dt_raw.shape

    def body(r_ref, b_ref, al_ref, dt_ref, ddt_ref, dadt_ref, dr_ref, db_ref, dal_ref):
        a = -jnp.exp(al_ref[...])
        dadt_v = dadt_ref[...]
        d_dt = ddt_ref[...] + a * dadt_v
        d_raw = d_dt * _sigmoid(r_ref[...] + b_ref[...])
        dr_ref[...] = d_raw
        db_ref[...] = jnp.sum(d_raw, axis=0, keepdims=True)
        dal_ref[...] = jnp.sum(dadt_v * dt_ref[...], axis=0, keepdims=True) * a

    full = pl.BlockSpec((s, h), lambda: (0, 0))
    vec = pl.BlockSpec((1, h), lambda: (0, 0))
    return pl.pallas_call(
        body, out_shape=(jax.ShapeDtypeStruct((s, h), F32), jax.ShapeDtypeStruct((1, h), F32),
                         jax.ShapeDtypeStruct((1, h), F32)),
        in_specs=[full, vec, vec, full, full, full], out_specs=(full, vec, vec), name=name,
        compiler_params=_cparams(),
    )(dt_raw, bias.reshape(1, h), a_log.reshape(1, h), dt, ddt, dadt)


def _group_mean(v, n_groups):
    gw = v.shape[-1] // n_groups
    parts = [jnp.broadcast_to(jnp.mean(v[:, k * gw:(k + 1) * gw], axis=-1, keepdims=True), (v.shape[0], gw))
             for k in range(n_groups)]
    return jnp.concatenate(parts, axis=-1)


def ssd_gate_fwd(y, z, gw, *, name):
    s, di = y.shape

    def body(y_ref, z_ref, w_ref, o_ref):
        yg = y_ref[...] * _silu(z_ref[...])
        r = lax.rsqrt(_group_mean(yg * yg, SSD_N_GROUPS) + GATED_NORM_EPS)
        o_ref[...] = (yg * r * w_ref[...]).astype(BF16)

    return _ew_call(body, jax.ShapeDtypeStruct((s, di), BF16), [_wide_rows(di), _wide_rows(di), _whole((1, di))],
                    _wide_rows(di), (y, z, gw.reshape(1, di)), s, name)


def ssd_gate_bwd(y, z, gw, dyn, *, name):
    s, di = y.shape

    def body(y_ref, z_ref, w_ref, dn_ref, dy_ref, dz_ref, dw_ref):
        _first_step(dw_ref)
        yv, zv = y_ref[...], z_ref[...]
        sz = _silu(zv)
        yg = yv * sz
        r = lax.rsqrt(_group_mean(yg * yg, SSD_N_GROUPS) + GATED_NORM_EPS)
        yhat = yg * r
        dn = dn_ref[...]
        dw_ref[...] += jnp.sum(dn * yhat, axis=0, keepdims=True)
        g = dn * w_ref[...]
        dyg = r * (g - yhat * _group_mean(g * yhat, SSD_N_GROUPS))
        dy_ref[...] = dyg * sz
        dz_ref[...] = (dyg * yv * _silu_grad(zv)).astype(BF16)

    return _ew_call(body, (jax.ShapeDtypeStruct((s, di), F32), jax.ShapeDtypeStruct((s, di), BF16),
                           jax.ShapeDtypeStruct((1, di), F32)),
                    [_wide_rows(di), _wide_rows(di), _whole((1, di)), _wide_rows(di)],
                    (_wide_rows(di), _wide_rows(di), _whole((1, di))),
                    (y, z, gw.reshape(1, di), dyn), s, name, carried=True)


def _head_mean(v):
    return _group_mean(v, v.shape[-1] // SB_HEAD_DIM)


def sb_qk_fwd(proj, qw, kw, *, name):
    s, w4 = proj.shape
    w = w4 // 4
    reps = w // SB_HEAD_DIM

    def body(q_ref, k_ref, qw_ref, kw_ref, qn_ref, kn_ref):
        for x_ref, w_ref, o_ref in ((q_ref, qw_ref, qn_ref), (k_ref, kw_ref, kn_ref)):
            xv = x_ref[...]
            r = lax.rsqrt(_head_mean(xv * xv) + NORM_EPS)
            o_ref[...] = (xv * r * jnp.tile(w_ref[...], (1, reps))).astype(BF16)

    shp = jax.ShapeDtypeStruct((s, w), BF16)
    return _ew_call(body, (shp, shp), [_rows(w, 0), _rows(w, 1), _whole((1, SB_HEAD_DIM)), _whole((1, SB_HEAD_DIM))],
                    (_rows(w), _rows(w)), (proj, proj, qw.reshape(1, -1), kw.reshape(1, -1)), s, name)


def sb_gate_fwd(o, proj, *, name):
    s, w = o.shape

    def body(o_ref, g_ref, og_ref):
        og_ref[...] = (o_ref[...] * _silu(g_ref[...])).astype(BF16)

    return _ew_call(body, jax.ShapeDtypeStruct((s, w), BF16), [_rows(w), _rows(w, 3)], _rows(w), (o, proj), s, name)


def sb_gate_bwd(dog, o, proj, *, name):
    s, w = o.shape

    def body(d_ref, o_ref, g_ref, do_ref, dg_ref):
        gv, dv = g_ref[...], d_ref[...]
        do_ref[...] = dv * _silu(gv)
        dg_ref[...] = (dv * o_ref[...] * _silu_grad(gv)).astype(BF16)

    return _ew_call(body, (jax.ShapeDtypeStruct((s, w), F32), jax.ShapeDtypeStruct((s, w), BF16)),
                    [_rows(w), _rows(w), _rows(w, 3)], (_rows(w), _rows(w)), (dog, o, proj), s, name)


def sb_pack_bwd(proj, qw, kw, dqn, dkn, dv, dg, *, name):
    s, w4 = proj.shape
    w = w4 // 4
    reps = w // SB_HEAD_DIM

    def body(q_ref, k_ref, qw_ref, kw_ref, dqn_ref, dkn_ref, dv_ref, dg_ref, dp_ref, dqw_ref, dkw_ref):
        _first_step(dqw_ref, dkw_ref)
        for idx, (x_ref, w_ref, d_ref, dw_ref) in enumerate(((q_ref, qw_ref, dqn_ref, dqw_ref),
                                                           (k_ref, kw_ref, dkn_ref, dkw_ref))):
            xv = x_ref[...]
            r = lax.rsqrt(_head_mean(xv * xv) + NORM_EPS)
            xhat = xv * r
            dn = d_ref[...]
            per_col = jnp.sum(dn * xhat, axis=0, keepdims=True)
            acc = per_col[:, 0:SB_HEAD_DIM]
            for hh in range(1, reps):
                acc = acc + per_col[:, hh * SB_HEAD_DIM:(hh + 1) * SB_HEAD_DIM]
            dw_ref[...] += acc
            g = dn * jnp.tile(w_ref[...], (1, reps))
            dp_ref[:, idx * w:(idx + 1) * w] = (r * (g - xhat * _head_mean(g * xhat))).astype(BF16)
        dp_ref[:, 2 * w:3 * w] = dv_ref[...].astype(BF16)
        dp_ref[:, 3 * w:4 * w] = dg_ref[...]

    vec = _whole((1, SB_HEAD_DIM))
    return _ew_call(body, (jax.ShapeDtypeStruct((s, w4), BF16), jax.ShapeDtypeStruct((1, SB_HEAD_DIM), F32),
                           jax.ShapeDtypeStruct((1, SB_HEAD_DIM), F32)),
                    [_wide_rows(w, 0), _wide_rows(w, 1), vec, vec, _wide_rows(w), _wide_rows(w), _wide_rows(w),
                     _wide_rows(w)],
                    (_wide_rows(w4), vec, vec),
                    (proj, proj, qw.reshape(1, -1), kw.reshape(1, -1), dqn, dkn, dv, dg), s, name, carried=True)


_HBM = pl.BlockSpec(memory_space=pltpu.HBM)


def _mesh_pos():
    return lax.axis_index("x"), lax.axis_index("y"), lax.axis_index("c")


def _other_chips(x, y):
    return [(1 - x, y), (x, 1 - y), (1 - x, 1 - y)]


def all_gather(shards, *, name):
    n = len(shards)

    def body(*refs):
        x_refs, out_refs = refs[:n], refs[n:2 * n]
        send_sems, recv_sems, local_sems = refs[2 * n:]
        x, y, c = _mesh_pos()
        me, sibling = (x, y, c), (x, y, 1 - c)
        chips = _other_chips(x, y)

        def copy(a, k, block, to, src=None):
            dst = out_refs[a].at[4 * block[0] + 2 * block[1] + block[2]]
            return pltpu.make_async_remote_copy(
                src_ref=dst if src is None else src, dst_ref=dst, send_sem=send_sems.at[a, k],
                recv_sem=recv_sems.at[a, k], device_id=to, device_id_type=MESH)

        mine = [pltpu.make_async_copy(x_refs[a], out_refs[a].at[4 * x + 2 * y + c], local_sems.at[a])
                for a in range(n)]
        for cp in mine:
            cp.start()
        first = []
        for a in range(n):
            first.append(copy(a, 0, me, sibling, src=x_refs[a]))
            first += [copy(a, 1 + j, me, (*chip, c), src=x_refs[a]) for j, chip in enumerate(chips)]
        for cp in first:
            cp.start()
        passed = []
        for a in range(n):
            for j, chip in enumerate(chips):
                copy(a, 1 + j, (*chip, c), me).wait_recv()
                fwd = copy(a, 4 + j, (*chip, c), sibling)
                fwd.start()
                passed.append(fwd)
        for a in range(n):
            copy(a, 0, sibling, me).wait_recv()
            for j, chip in enumerate(chips):
                copy(a, 4 + j, (*chip, 1 - c), me).wait_recv()
        for cp in first + passed:
            cp.wait_send()
        for cp in mine:
            cp.wait()

    return pl.pallas_call(
        body, out_shape=tuple(jax.ShapeDtypeStruct((N_DEV,) + t.shape, t.dtype) for t in shards),
        in_specs=[_HBM] * n, out_specs=tuple([_HBM] * n),
        scratch_shapes=[pltpu.SemaphoreType.DMA((n, 7)), pltpu.SemaphoreType.DMA((n, 7)),
                        pltpu.SemaphoreType.DMA((n,))],
        name=name,
    )(*shards)


def sibling_exchange(grads, *, name):
    n = len(grads)

    def body(*refs):
        g_refs, r_refs = refs[:n], refs[n:2 * n]
        send_sems, recv_sems = refs[2 * n:]
        x, y, c = _mesh_pos()
        copies = [pltpu.make_async_remote_copy(
            src_ref=g_refs[a].at[:, 1 - c], dst_ref=r_refs[a], send_sem=send_sems.at[a],
            recv_sem=recv_sems.at[a], device_id=(x, y, 1 - c), device_id_type=MESH) for a in range(n)]
        for cp in copies:
            cp.start()
        for cp in copies:
            cp.wait()

    return pl.pallas_call(
        body, out_shape=tuple(jax.ShapeDtypeStruct((N_CHIP,) + g.shape[2:], g.dtype) for g in grads),
        in_specs=[_HBM] * n, out_specs=tuple([_HBM] * n),
        scratch_shapes=[pltpu.SemaphoreType.DMA((n,)), pltpu.SemaphoreType.DMA((n,))],
        name=name,
    )(*grads)


def chip_exchange(sums, *, name):
    n = len(sums)

    def body(*refs):
        s_refs, r_refs = refs[:n], refs[n:2 * n]
        send_sems, recv_sems = refs[2 * n:]
        x, y, c = _mesh_pos()
        copies = []
        for a in range(n):
            for j, chip in enumerate(_other_chips(x, y)):
                copies.append(pltpu.make_async_remote_copy(
                    src_ref=s_refs[a].at[2 * chip[0] + chip[1]], dst_ref=r_refs[a].at[j],
                    send_sem=send_sems.at[a, j], recv_sem=recv_sems.at[a, j],
                    device_id=(*chip, c), device_id_type=MESH))
        for cp in copies:
            cp.start()
        for cp in copies:
            cp.wait()

    return pl.pallas_call(
        body, out_shape=tuple(jax.ShapeDtypeStruct((N_CHIP - 1,) + t.shape[1:], t.dtype) for t in sums),
        in_specs=[_HBM] * n, out_specs=tuple([_HBM] * n),
        scratch_shapes=[pltpu.SemaphoreType.DMA((n, N_CHIP - 1)), pltpu.SemaphoreType.DMA((n, N_CHIP - 1))],
        name=name,
    )(*sums)


def all_reduce_small(v, *, name):
    r = v.shape[0]

    def body(v_ref, o_ref, buf, send_sems, recv_sems):
        x, y, c = _mesh_pos()
        me = 4 * x + 2 * y + c
        buf[me] = v_ref[...]
        copies = []
        for k in range(1, N_DEV):
            to = ((x + (k >> 2)) % 2, (y + ((k >> 1) & 1)) % 2, (c + (k & 1)) % 2)
            copies.append(pltpu.make_async_remote_copy(
                src_ref=v_ref, dst_ref=buf.at[me], send_sem=send_sems.at[k - 1], recv_sem=recv_sems.at[k - 1],
                device_id=to, device_id_type=MESH))
        for cp in copies:
            cp.start()
        for cp in copies:
            cp.wait()
        acc = buf[0]
        for d in range(1, N_DEV):
            acc = acc + buf[d]
        o_ref[...] = acc

    vm = pl.BlockSpec(memory_space=pltpu.VMEM)
    return pl.pallas_call(
        body, out_shape=jax.ShapeDtypeStruct(v.shape, F32), in_specs=[vm], out_specs=vm,
        scratch_shapes=[pltpu.VMEM((N_DEV, r, LANES), F32), pltpu.SemaphoreType.DMA((N_DEV - 1,)),
                        pltpu.SemaphoreType.DMA((N_DEV - 1,))],
        name=name,
    )(v)


def pair_add(g, r1, core, *, name):
    _, _, rows, cols = g.shape
    tm = _pick(rows, (256, 128))

    def body(c_ref, g_ref, r_ref, o_ref):
        o_ref[...] = (g_ref[...].astype(F32) + r_ref[...].astype(F32)).astype(o_ref.dtype)

    return pl.pallas_call(
        body, out_shape=jax.ShapeDtypeStruct(r1.shape, g.dtype),
        grid_spec=pltpu.PrefetchScalarGridSpec(
            num_scalar_prefetch=1, grid=(N_CHIP, rows // tm),
            in_specs=[pl.BlockSpec((None, None, tm, cols), lambda k, i, c_ref: (k, c_ref[0], i, 0)),
                      pl.BlockSpec((None, tm, cols), lambda k, i, c_ref: (k, i, 0))],
            out_specs=pl.BlockSpec((None, tm, cols), lambda k, i, c_ref: (k, i, 0))),
        name=name, compiler_params=_cparams(("parallel", "parallel")),
    )(core, g, r1)


def _adamw_math(w, g, m, v):
    m = ADAM_B1 * m + (1.0 - ADAM_B1) * g
    v = ADAM_B2 * v + (1.0 - ADAM_B2) * (g * g)
    m_hat = m / (1.0 - ADAM_B1 ** ADAM_STEP)
    v_hat = v / (1.0 - ADAM_B2 ** ADAM_STEP)
    delta = -ADAM_LR * (m_hat / (jnp.sqrt(v_hat) + ADAM_EPS) + ADAM_WD * w)
    return delta, m, v


def adamw_sharded(w, m, v, chip_sums, received, chip, *, name):
    rows, cols = w.shape
    tm = _pick(rows, (256, 128))

    def body(k_ref, w_ref, m_ref, v_ref, t_ref, r_ref, g_ref, d_ref, nm_ref, nv_ref):
        g = t_ref[...].astype(F32)
        for j in range(N_CHIP - 1):
            g = g + r_ref[j].astype(F32)
        d, mm, vv = _adamw_math(w_ref[...], g, m_ref[...], v_ref[...])
        g_ref[...] = g
        d_ref[...] = d
        nm_ref[...] = mm
        nv_ref[...] = vv

    blk = pl.BlockSpec((tm, cols), lambda i, k_ref: (i, 0))
    shp = jax.ShapeDtypeStruct((rows, cols), F32)
    return pl.pallas_call(
        body, out_shape=(shp, shp, shp, shp),
        grid_spec=pltpu.PrefetchScalarGridSpec(
            num_scalar_prefetch=1, grid=(rows // tm,),
            in_specs=[blk, blk, blk,
                      pl.BlockSpec((None, tm, cols), lambda i, k_ref: (k_ref[0], i, 0)),
                      pl.BlockSpec((N_CHIP - 1, tm, cols), lambda i, k_ref: (0, i, 0))],
            out_specs=(blk, blk, blk, blk)),
        name=name, compiler_params=_cparams(("parallel",)),
    )(chip, w, m, v, chip_sums, received)


def adamw_replicated(w, m, v, g, *, name):
    def body(w_ref, m_ref, v_ref, g_ref, d_ref, nm_ref, nv_ref):
        d, mm, vv = _adamw_math(w_ref[...], g_ref[...], m_ref[...], v_ref[...])
        d_ref[...] = d
        nm_ref[...] = mm
        nv_ref[...] = vv

    shp = jax.ShapeDtypeStruct(w.shape, F32)
    return pl.pallas_call(body, out_shape=(shp, shp, shp), name=name, compiler_params=_cparams())(w, m, v, g)


WEIGHT_NAMES = ("norm_w", "ssd_in_w", "ssd_conv_w", "ssd_conv_b", "ssd_dt_bias", "ssd_a_log", "ssd_d",
                "ssd_gnorm_w", "ssd_out_w", "sb_in_w", "sb_qn_w", "sb_kn_w", "sb_out_w", "ple_norm_w",
                "ple_gate_w", "ple_proj_w")
REPLICATED = ("norm_w", "ssd_conv_b", "ssd_dt_bias", "ssd_a_log", "ssd_d", "ssd_gnorm_w", "sb_qn_w", "sb_kn_w",
              "ple_norm_w")
PACK_ROWS = 8


def _pack(parts):
    flat = jnp.concatenate([t.reshape(-1) for t in parts])
    pad = (-flat.shape[0]) % (PACK_ROWS * LANES)
    return jnp.pad(flat, (0, pad)).reshape(-1, LANES)


def _unpack(packed, like):
    flat = packed.reshape(-1)
    out, off = [], 0
    for t in like:
        out.append(flat[off:off + t.size].reshape(t.shape))
        off += t.size
    return out


def _to_group_lanes(v, r):
    t = v.reshape(v.shape[0], SSD_N_GROUPS, r).transpose(1, 0, 2)
    return jnp.pad(t, ((0, 0), (0, 0), (0, LANES - r)))


def _from_group_lanes(t, r):
    return t[:, :, :r].transpose(1, 0, 2).reshape(t.shape[1], SSD_N_GROUPS * r)


def _head_vec(v, r):
    return jnp.pad(v.reshape(SSD_N_GROUPS, 1, r), ((0, 0), (0, 0), (0, LANES - r)))


def _col_blocks(full):
    rows = full.shape[0]
    return full.reshape(rows, N_DEV, -1).transpose(1, 0, 2)


def _from_col_blocks(blocks):
    return blocks.transpose(1, 0, 2).reshape(blocks.shape[1], -1)


def _split_cols(full, widths):
    out, off = [], 0
    for w in widths:
        out.append(full[:, off:off + w])
        off += w
    return out


def kernel(x, p, norm_w, ssd_in_w, ssd_conv_w, ssd_conv_b, ssd_dt_bias, ssd_a_log, ssd_d, ssd_gnorm_w, ssd_out_w, sb_in_w, sb_qn_w, sb_kn_w, sb_out_w, ple_norm_w, ple_gate_w, ple_proj_w, loss_target, m_norm_w, m_ssd_in_w, m_ssd_conv_w, m_ssd_conv_b, m_ssd_dt_bias, m_ssd_a_log, m_ssd_d, m_ssd_gnorm_w, m_ssd_out_w, m_sb_in_w, m_sb_qn_w, m_sb_kn_w, m_sb_out_w, m_ple_norm_w, m_ple_gate_w, m_ple_proj_w, v_norm_w, v_ssd_in_w, v_ssd_conv_w, v_ssd_conv_b, v_ssd_dt_bias, v_ssd_a_log, v_ssd_d, v_ssd_gnorm_w, v_ssd_out_w, v_sb_in_w, v_sb_qn_w, v_sb_kn_w, v_sb_out_w, v_ple_norm_w, v_ple_gate_w, v_ple_proj_w):
    env = dict(locals())
    wts = {n: env[n] for n in WEIGHT_NAMES}
    mom1 = {n: env["m_" + n] for n in WEIGHT_NAMES}
    mom2 = {n: env["v_" + n] for n in WEIGHT_NAMES}

    s, d = x.shape[1], x.shape[2]
    depth = norm_w.shape[0]
    n_ssd, n_sb = ssd_in_w.shape[0], sb_in_w.shape[0]
    di = ssd_out_w.shape[1] * N_DEV
    n_heads = ssd_dt_bias.shape[1]
    hpg = n_heads // SSD_N_GROUPS
    nbc = SSD_N_GROUPS * SSD_D_STATE
    in_segs = (di, di, nbc, nbc, n_heads)
    conv_segs = (di, nbc, nbc)
    sb_w = sb_out_w.shape[1] * N_DEV
    xi, yi, ci = _mesh_pos()
    core = ci.astype(jnp.int32).reshape(1)
    chip = (2 * xi + yi).astype(jnp.int32).reshape(1)

    shards = []
    for j in range(n_ssd):
        shards += [ssd_in_w[j].astype(BF16), ssd_conv_w[j], ssd_out_w[j].astype(BF16)]
    for j in range(n_sb):
        shards += [sb_in_w[j].astype(BF16), sb_out_w[j].astype(BF16)]
    for i in range(depth):
        shards += [ple_gate_w[i].astype(BF16), ple_proj_w[i].astype(BF16)]
    gathered = list(all_gather(shards, name="all_gather_weights"))
    ssd_full, sb_full, ple_full = [], [], []
    for j in range(n_ssd):
        g_in, g_conv, g_out = gathered[3 * j:3 * j + 3]
        ssd_full.append(dict(
            w_in=_split_cols(_from_col_blocks(g_in), in_segs),
            conv_w=_split_cols(_from_col_blocks(g_conv), conv_segs),
            conv_b=_split_cols(ssd_conv_b[j].reshape(1, -1), conv_segs),
            w_out=g_out.reshape(di, d)))
    base = 3 * n_ssd
    for j in range(n_sb):
        sb_full.append(dict(w_in=gathered[base + 2 * j], w_out=gathered[base + 2 * j + 1].reshape(sb_w, d)))
    base += 2 * n_sb
    for i in range(depth):
        ple_full.append(dict(w_gate=gathered[base + 2 * i].reshape(d, d), w_proj=gathered[base + 2 * i + 1]))

    h = x.reshape(s, d)
    saved = []
    for i in range(depth):
        j = i // 2
        sv = dict(h_in=h)
        u = rmsnorm_fwd(h, norm_w[i], name=f"l{i}_norm")
        sv["u"] = u
        if i % 2 == 0:
            fw = ssd_full[j]
            raw = [matmul(u, wseg, name=f"l{i}_in{q}") for q, wseg in enumerate(fw["w_in"])]
            z, dt_raw = raw[0], raw[4]
            act = [ssd_conv_fwd(raw[1 + q], fw["conv_w"][q], fw["conv_b"][q], name=f"l{i}_conv{q}") for q in range(3)]
            dt, a_neg = ssd_dt_fwd(dt_raw, ssd_dt_bias[j], ssd_a_log[j], name=f"l{i}_dt")
            dtp = _to_group_lanes(dt, hpg)
            a_g = _head_vec(a_neg.reshape(-1), hpg)
            d_g = _head_vec(ssd_d[j], hpg)
            y, states = ssd_scan_fwd(act[0], act[1], act[2], dtp, a_g, d_g, heads_per_group=hpg, name=f"l{i}_scan")
            yn = ssd_gate_fwd(y, z, ssd_gnorm_w[j], name=f"l{i}_gate")
            h1 = matmul(yn, fw["w_out"], res=h, name=f"l{i}_out")
            sv.update(raw=raw, act=act, dt=dt, dtp=dtp, a_g=a_g, d_g=d_g, y=y, states=states, yn=yn)
        else:
            fw = sb_full[j]
            proj = matmul(u, fw["w_in"], name=f"l{i}_in")
            qn, kn = sb_qk_fwd(proj, sb_qn_w[j], sb_kn_w[j], name=f"l{i}_qknorm")
            v_off = 2 * sb_w // SB_HEAD_DIM
            o, tot = sb_attn_fwd(qn, kn, proj, v_off=v_off, name=f"l{i}_attn")
            og = sb_gate_fwd(o, proj, name=f"l{i}_gate")
            h1 = matmul(og, fw["w_out"], res=h, name=f"l{i}_out")
            sv.update(proj=proj, qn=qn, kn=kn, o=o, tot=tot, og=og, v_off=v_off)
        t = rmsnorm_fwd(h1, ple_norm_w[i], name=f"l{i}_plenorm")
        gate_pre = matmul(t, ple_full[i]["w_gate"], name=f"l{i}_plegate")
        pp = matmul(p[i, 0], ple_full[i]["w_proj"], name=f"l{i}_pleproj")
        h = ple_fwd(h1, gate_pre, pp, name=f"l{i}_ple")
        sv.update(h1=h1, t=t, gate_pre=gate_pre, pp=pp)
        saved.append(sv)

    loss_part, dh = loss_head(h, loss_target.reshape(s, d), name="loss_head")
    loss = lax.psum(loss_part[0, 0], ("x", "y", "c"))

    big = {}
    small = {n: [None] * wts[n].shape[0] for n in REPLICATED}
    for i in reversed(range(depth)):
        j = i // 2
        sv = saved[i]
        dpp, dgp = ple_bwd(dh, sv["gate_pre"], sv["pp"], name=f"b{i}_ple")
        big["ple_proj_w", i] = matmul(p[i, 0], dpp, mode="tn", out_dtype=BF16, out_blocks=ple_proj_w.shape[2],
                                      name=f"b{i}_pleproj_w")
        big["ple_gate_w", i] = matmul(sv["t"], dgp, mode="tn", out_dtype=BF16, name=f"b{i}_plegate_w").reshape(N_DEV, -1, d)
        dt_ = matmul(dgp, ple_full[i]["w_gate"], mode="nt", name=f"b{i}_plegate_x")
        dh1, g_pn = rmsnorm_bwd(sv["h1"], ple_norm_w[i], dt_, dh, name=f"b{i}_plenorm")
        small["ple_norm_w"][i] = g_pn
        u = sv["u"]
        if i % 2 == 0:
            fw = ssd_full[j]
            raw, act = sv["raw"], sv["act"]
            big["ssd_out_w", j] = matmul(sv["yn"], dh1, mode="tn", out_dtype=BF16, name=f"b{i}_out_w").reshape(N_DEV, -1, d)
            dyn = matmul(dh1, fw["w_out"], mode="nt", name=f"b{i}_out_x")
            dy, dz, g_gn = ssd_gate_bwd(sv["y"], raw[0], ssd_gnorm_w[j], dyn, name=f"b{i}_gate")
            dxs, dbm, dcm, ddtp, dadtp, dd_g = ssd_scan_bwd(act[0], act[1], act[2], sv["dtp"], sv["a_g"], sv["d_g"],
                                                          sv["states"], dy, heads_per_group=hpg, name=f"b{i}_scan")
            ddt_raw, g_dtb, g_alog = ssd_dt_bwd(raw[4], ssd_dt_bias[j], ssd_a_log[j], sv["dt"],
                                                _from_group_lanes(ddtp, hpg), _from_group_lanes(dadtp, hpg),
                                                name=f"b{i}_dt")
            conv_back = [ssd_conv_bwd(raw[1 + q], fw["conv_w"][q], fw["conv_b"][q], dact, name=f"b{i}_conv{q}")
                         for q, dact in enumerate((dxs, dbm, dcm))]
            dsegs = [dz] + [cb[0] for cb in conv_back] + [ddt_raw]
            g_in = jnp.concatenate([matmul(u, ds, mode="tn", out_dtype=BF16, name=f"b{i}_in{q}_w")
                                    for q, ds in enumerate(dsegs)], axis=1)
            big["ssd_in_w", j] = _col_blocks(g_in)
            big["ssd_conv_w", j] = _col_blocks(jnp.concatenate([cb[1] for cb in conv_back], axis=1))
            du = None
            for q, (ds, wseg) in enumerate(zip(dsegs, fw["w_in"])):
                du = matmul(ds, wseg, mode="nt", res=du, name=f"b{i}_in{q}_x")
            small["ssd_conv_b"][j] = jnp.concatenate([cb[2] for cb in conv_back], axis=1)
            small["ssd_dt_bias"][j] = g_dtb
            small["ssd_a_log"][j] = g_alog
            small["ssd_d"][j] = dd_g[:, 0, :hpg]
            small["ssd_gnorm_w"][j] = g_gn
        else:
            fw = sb_full[j]
            proj = sv["proj"]
            big["sb_out_w", j] = matmul(sv["og"], dh1, mode="tn", out_dtype=BF16, name=f"b{i}_out_w").reshape(N_DEV, -1, d)
            dog = matmul(dh1, fw["w_out"], mode="nt", name=f"b{i}_out_x")
            do, dg = sb_gate_bwd(dog, sv["o"], proj, name=f"b{i}_gate")
            dqn, dkn, dv = sb_attn_bwd(sv["qn"], sv["kn"], proj, sv["tot"], do, v_off=sv["v_off"], name=f"b{i}_attn")
            dproj, g_qn, g_kn = sb_pack_bwd(proj, sb_qn_w[j], sb_kn_w[j], dqn, dkn, dv, dg, name=f"b{i}_qknorm")
            big["sb_in_w", j] = matmul(u, dproj, mode="tn", out_dtype=BF16, out_blocks=sb_in_w.shape[2], name=f"b{i}_in_w")
            du = matmul(dproj, fw["w_in"], mode="nt", name=f"b{i}_in_x")
            small["sb_qn_w"][j] = g_qn
            small["sb_kn_w"][j] = g_kn
        dh, g_n = rmsnorm_bwd(sv["h_in"], norm_w[i], du, dh1, name=f"b{i}_norm")
        small["norm_w"][i] = g_n
    grad_x = dh.reshape(x.shape)

    rep_like = [wts[n] for n in REPLICATED]
    g_packed = all_reduce_small(_pack([jnp.stack([t.reshape(-1) for t in small[n]]) for n in REPLICATED]),
                                name="all_reduce_small_grads")
    d_packed, m_packed, v_packed = adamw_replicated(
        _pack(rep_like), _pack([mom1[n] for n in REPLICATED]), _pack([mom2[n] for n in REPLICATED]), g_packed,
        name="adamw_replicated")
    grads = dict(zip(REPLICATED, _unpack(g_packed, rep_like)))
    deltas = dict(zip(REPLICATED, _unpack(d_packed, rep_like)))
    new_m = dict(zip(REPLICATED, _unpack(m_packed, rep_like)))
    new_v = dict(zip(REPLICATED, _unpack(v_packed, rep_like)))

    keys = sorted(big)
    blocks = [big[k].reshape(N_CHIP, 2, *big[k].shape[1:]) for k in keys]
    from_sibling = sibling_exchange(blocks, name="rs_sibling_exchange")
    chip_sums = [pair_add(g, r1, core, name=f"rs_pair_add_{k[0]}{k[1]}") for k, g, r1 in zip(keys, blocks, from_sibling)]
    from_chips = chip_exchange(chip_sums, name="rs_chip_exchange")
    per_layer = {}
    for k, t_sum, recv in zip(keys, chip_sums, from_chips):
        n, idx = k
        per_layer[k] = adamw_sharded(wts[n][idx], mom1[n][idx], mom2[n][idx], t_sum, recv, chip,
                                     name=f"adamw_{n}{idx}")
    for n in WEIGHT_NAMES:
        if n in REPLICATED:
            continue
        layers = [per_layer[n, idx] for idx in range(wts[n].shape[0])]
        grads[n], deltas[n], new_m[n], new_v[n] = (jnp.stack([lay[q] for lay in layers]) for q in range(4))

    return (loss, grad_x, *[grads[n] for n in WEIGHT_NAMES], *[deltas[n] for n in WEIGHT_NAMES],
            *[new_m[n] for n in WEIGHT_NAMES], *[new_v[n] for n in WEIGHT_NAMES])
```

```python
import functools
import math

import jax
import jax.numpy as jnp
from jax import lax
from jax.experimental import pallas as pl
from jax.experimental.pallas import tpu as pltpu

F32 = jnp.float32
BF16 = jnp.bfloat16
MESH = pl.DeviceIdType.MESH

N_DEV = 8
N_CHIP = 4
LANES = 128
VMEM_LIMIT_BYTES = 48 * 1024 * 1024
MATMUL_TILE_BYTES = 28 * 1024 * 1024

NORM_EPS = 1e-6
GATED_NORM_EPS = 1e-5
SSD_HEAD_DIM = 64
SSD_N_GROUPS = 8
SSD_D_STATE = 128
SSD_D_CONV = 4
SSD_CHUNK = 128
SB_HEAD_DIM = 128
PLE_DIM = 256

ADAM_LR = 0.001
ADAM_B1 = 0.9
ADAM_B2 = 0.999
ADAM_EPS = 1e-08
ADAM_WD = 0.01
ADAM_STEP = 10


def _cparams(sem=None, **kw):
    return pltpu.CompilerParams(dimension_semantics=sem, vmem_limit_bytes=VMEM_LIMIT_BYTES, **kw)


def _pick(dim, prefs):
    for t in prefs:
        if dim % t == 0:
            return t
    return dim


def _sigmoid(x):
    return 1.0 / (1.0 + jnp.exp(-x))


def _silu(x):
    return x * _sigmoid(x)


def _silu_grad(x):
    s = _sigmoid(x)
    return s * (1.0 + x * (1.0 - s))


def matmul(a, b, *, mode="nn", out_dtype=F32, res=None, out_blocks=None, name):
    b_blocked = b.ndim == 3
    if mode == "nn":
        m, kc = a.shape
        n = b.shape[-1] * (N_DEV if b_blocked else 1)
    elif mode == "nt":
        m, kc = a.shape
        n = b.shape[-2]
    else:
        kc, m = a.shape
        n = b.shape[-1]
    nb = b.shape[-1] if b_blocked else None
    tn = _pick(n if not out_blocks else out_blocks, (512, 256, 128))
    if b_blocked and mode == "nn":
        tn = _pick(nb, (512, 256, 128))
    k_limit = nb if (b_blocked and mode == "nt") else kc
    tm, tk = None, None
    for tm_try in (1024, 512, 256, 128):
        if m % tm_try:
            continue
        for tk_try in (k_limit, 2048, 1024, 512, 256, 128):
            if tk_try > k_limit or k_limit % tk_try:
                continue
            tiles = 2 * (tm_try * tk_try * a.dtype.itemsize + tk_try * tn * b.dtype.itemsize)
            tiles += tm_try * tn * (2 * jnp.dtype(out_dtype).itemsize + 4 + (8 if res is not None else 0))
            if tiles <= MATMUL_TILE_BYTES:
                tm, tk = tm_try, tk_try
                break
        if tm:
            break
    if tm is None:
        tm, tk = m, k_limit
    nk = kc // tk
    grid = (m // tm, n // tn, nk)

    if mode == "tn":
        a_spec = pl.BlockSpec((tk, tm), lambda i, j, k: (k, i))
        dims = (((0,), (0,)), ((), ()))
    else:
        a_spec = pl.BlockSpec((tm, tk), lambda i, j, k: (i, k))
        dims = (((1,), (0,)), ((), ())) if mode == "nn" else (((1,), (1,)), ((), ()))
    if mode == "nt":
        if b_blocked:
            per = nb // tk
            b_spec = pl.BlockSpec((None, tn, tk), lambda i, j, k: (k // per, j, k % per))
        else:
            b_spec = pl.BlockSpec((tn, tk), lambda i, j, k: (j, k))
    else:
        if b_blocked:
            per = nb // tn
            b_spec = pl.BlockSpec((None, tk, tn), lambda i, j, k: (j // per, k, j % per))
        else:
            b_spec = pl.BlockSpec((tk, tn), lambda i, j, k: (k, j))
    if out_blocks:
        per_o = out_blocks // tn
        out_shape = jax.ShapeDtypeStruct((n // out_blocks, m, out_blocks), out_dtype)
        out_spec = pl.BlockSpec((None, tm, tn), lambda i, j, k: (j // per_o, i, j % per_o))
    else:
        out_shape = jax.ShapeDtypeStruct((m, n), out_dtype)
        out_spec = pl.BlockSpec((tm, tn), lambda i, j, k: (i, j))
    in_specs = [a_spec, b_spec]
    args = [a, b]
    if res is not None:
        in_specs.append(pl.BlockSpec((tm, tn), lambda i, j, k: (i, j)))
        args.append(res)

    def body(*refs):
        a_ref, b_ref = refs[:2]
        r_ref = refs[2] if res is not None else None
        o_ref = refs[3] if res is not None else refs[2]

        def finish(r):
            if res is not None:
                r = r + r_ref[...].astype(F32)
            o_ref[...] = r.astype(out_dtype)

        part = lax.dot_general(a_ref[...].astype(BF16), b_ref[...].astype(BF16), dims, preferred_element_type=F32)
        if nk == 1:
            finish(part)
            return
        acc_ref = refs[-1]
        k = pl.program_id(2)

        @pl.when(k == 0)
        def _():
            acc_ref[...] = part

        @pl.when(k > 0)
        def _():
            acc_ref[...] += part

        @pl.when(k == nk - 1)
        def _():
            finish(acc_ref[...])

    return pl.pallas_call(
        body, out_shape=out_shape, grid=grid, in_specs=in_specs, out_specs=out_spec,
        scratch_shapes=[] if nk == 1 else [pltpu.VMEM((tm, tn), F32)], name=name,
        compiler_params=_cparams(("parallel", "parallel", "arbitrary")),
    )(*args)


def _dot(a, b, dims, precision=None):
    return lax.dot_general(a, b, (dims, ((), ())), preferred_element_type=F32, precision=precision)


_NN = ((1,), (0,))
_NT = ((1,), (1,))
_TN = ((0,), (0,))
_EXACT = lax.Precision.HIGHEST


def _chunk_decay_terms(dt, a):
    ln = dt.shape[0]
    row = lax.broadcasted_iota(jnp.int32, (ln, ln), 0)
    col = lax.broadcasted_iota(jnp.int32, (ln, ln), 1)
    tri = (row >= col).astype(F32)
    a_col = _dot(tri, dt * a, _NN, _EXACT)
    return a_col, a_col.T, row >= col


def ssd_scan_fwd(xs, bm, cm, dtp, a_g, d_g, *, heads_per_group, name):
    s, di = xs.shape
    g_n = SSD_N_GROUPS
    r_n, p_n, n_n, ln = heads_per_group, SSD_HEAD_DIM, SSD_D_STATE, SSD_CHUNK
    nc = s // ln

    def body(xs_ref, bm_ref, cm_ref, dt_ref, a_ref, d_ref, y_ref, st_ref, state):
        c = pl.program_id(1)

        @pl.when(c == 0)
        def _():
            state[...] = jnp.zeros_like(state)

        dt = dt_ref[...]
        a_col_all, a_row_all, causal = _chunk_decay_terms(dt, a_ref[...])
        bmb = bm_ref[...].astype(BF16)
        cmb = cm_ref[...].astype(BF16)
        scores = _dot(cmb, bmb, _NT)
        d_all = d_ref[...]
        for r in range(r_n):
            a_col = a_col_all[:, r:r + 1]
            a_row = a_row_all[r:r + 1, :]
            decay = jnp.exp(jnp.where(causal, a_col - a_row, -jnp.inf))
            x_r = xs_ref[:, r * p_n:(r + 1) * p_n]
            xdt = x_r * dt[:, r:r + 1]
            s_r = state[r]
            st_ref[r] = s_r
            y = _dot((scores * decay).astype(BF16), xdt.astype(BF16), _NN)
            y = y + jnp.exp(a_col) * _dot(cmb, s_r.astype(BF16), _NT)
            y_ref[:, r * p_n:(r + 1) * p_n] = y + d_all[:, r:r + 1] * x_r
            a_last = a_col[ln - 1:ln, :]
            to_end = jnp.exp(a_last - a_col)
            state[r] = s_r * jnp.exp(a_last) + _dot((xdt * to_end).astype(BF16), bmb, _TN)

    return pl.pallas_call(
        body,
        out_shape=(jax.ShapeDtypeStruct((s, di), F32),
                   jax.ShapeDtypeStruct((nc, g_n * r_n, p_n, n_n), F32)),
        grid=(g_n, nc),
        in_specs=[pl.BlockSpec((ln, r_n * p_n), lambda g, c: (c, g)),
                  pl.BlockSpec((ln, n_n), lambda g, c: (c, g)),
                  pl.BlockSpec((ln, n_n), lambda g, c: (c, g)),
                  pl.BlockSpec((None, ln, LANES), lambda g, c: (g, c, 0)),
                  pl.BlockSpec((None, 1, LANES), lambda g, c: (g, 0, 0)),
                  pl.BlockSpec((None, 1, LANES), lambda g, c: (g, 0, 0))],
        out_specs=(pl.BlockSpec((ln, r_n * p_n), lambda g, c: (c, g)),
                   pl.BlockSpec((None, r_n, p_n, n_n), lambda g, c: (c, g, 0, 0))),
        scratch_shapes=[pltpu.VMEM((r_n, p_n, n_n), F32)],
        name=name, compiler_params=_cparams(("parallel", "arbitrary")),
    )(xs, bm, cm, dtp, a_g, d_g)


def ssd_scan_bwd(xs, bm, cm, dtp, a_g, d_g, states, dy, *, heads_per_group, name):
    s, di = xs.shape
    g_n = SSD_N_GROUPS
    r_n, p_n, n_n, ln = heads_per_group, SSD_HEAD_DIM, SSD_D_STATE, SSD_CHUNK
    nc = s // ln

    def body(xs_ref, bm_ref, cm_ref, dt_ref, a_ref, d_ref, st_ref, dy_ref,
             dxs_ref, dbm_ref, dcm_ref, ddt_ref, dadt_ref, dd_ref, dstate):
        c = pl.program_id(1)

        @pl.when(c == 0)
        def _():
            dstate[...] = jnp.zeros_like(dstate)
            dd_ref[...] = jnp.zeros_like(dd_ref)

        dt = dt_ref[...]
        a_col_all, a_row_all, causal = _chunk_decay_terms(dt, a_ref[...])
        bm_f = bm_ref[...]
        cm_f = cm_ref[...]
        bmb = bm_f.astype(BF16)
        cmb = cm_f.astype(BF16)
        scores = _dot(cmb, bmb, _NT)
        d_all = d_ref[...]
        lane = lax.broadcasted_iota(jnp.int32, (1, LANES), 1)
        sub = lax.broadcasted_iota(jnp.int32, (LANES, 1), 0)
        dscores = jnp.zeros((ln, ln), F32)
        dcm = jnp.zeros((ln, n_n), F32)
        dbm = jnp.zeros((ln, n_n), F32)
        da_cols = jnp.zeros((ln, LANES), F32)
        da_rows = jnp.zeros((LANES, ln), F32)
        da_last = jnp.zeros((1, LANES), F32)
        ddt = jnp.zeros((ln, LANES), F32)
        dd = jnp.zeros((1, LANES), F32)
        for r in range(r_n):
            pick_l = (lane == r).astype(F32)
            pick_s = (sub == r).astype(F32)
            a_col = a_col_all[:, r:r + 1]
            a_row = a_row_all[r:r + 1, :]
            decay = jnp.exp(jnp.where(causal, a_col - a_row, -jnp.inf))
            x_r = xs_ref[:, r * p_n:(r + 1) * p_n]
            dt_r = dt[:, r:r + 1]
            xdt = x_r * dt_r
            xdtb = xdt.astype(BF16)
            dy_r = dy_ref[:, r * p_n:(r + 1) * p_n]
            dyb = dy_r.astype(BF16)
            s_in = st_ref[r]
            ds_out = dstate[r]
            ds_outb = ds_out.astype(BF16)
            m_mat = scores * decay
            a_last = a_col[ln - 1:ln, :]
            e_last = jnp.exp(a_last)
            to_end = jnp.exp(a_last - a_col)
            e_col = jnp.exp(a_col)
            dm = _dot(dyb, xdtb, _NT)
            dscores = dscores + dm * decay
            e_mat = dm * m_mat
            da_cols = da_cols + jnp.sum(e_mat, axis=1, keepdims=True) * pick_l
            da_rows = da_rows - pick_s * jnp.sum(e_mat, axis=0, keepdims=True)
            dxdt = _dot(m_mat.astype(BF16), dyb, _TN)
            y_off = e_col * _dot(cmb, s_in.astype(BF16), _NT)
            dy_e = (dy_r * e_col).astype(BF16)
            dcm = dcm + _dot(dy_e, s_in.astype(BF16), _NN)
            ds_in = _dot(dy_e, cmb, _TN)
            da_cols = da_cols + jnp.sum(dy_r * y_off, axis=1, keepdims=True) * pick_l
            bds = _dot(bmb, ds_outb, _NT)
            dxdt = dxdt + to_end * bds
            xdt_e = xdt * to_end
            dbm = dbm + _dot(xdt_e.astype(BF16), ds_outb, _NN)
            w_col = jnp.sum(xdt_e * bds, axis=1, keepdims=True)
            da_cols = da_cols - w_col * pick_l
            last = jnp.sum(w_col, axis=0, keepdims=True) + e_last * jnp.sum(
                jnp.sum(ds_out * s_in, axis=1, keepdims=True), axis=0, keepdims=True)
            da_last = da_last + last * pick_l
            dstate[r] = ds_out * e_last + ds_in
            dxs_ref[:, r * p_n:(r + 1) * p_n] = dxdt * dt_r + d_all[:, r:r + 1] * dy_r
            ddt = ddt + jnp.sum(dxdt * x_r, axis=1, keepdims=True) * pick_l
            dd = dd + jnp.sum(jnp.sum(dy_r * x_r, axis=1, keepdims=True), axis=0, keepdims=True) * pick_l
        dsb = dscores.astype(BF16)
        dcm_ref[...] = dcm + _dot(dsb, bmb, _NN)
        dbm_ref[...] = dbm + _dot(dsb, cmb, _TN)
        da_total = da_cols + da_rows.T
        row = lax.broadcasted_iota(jnp.int32, (ln, ln), 0)
        col = lax.broadcasted_iota(jnp.int32, (ln, ln), 1)
        upper = (col >= row).astype(F32)
        dadt_ref[...] = _dot(upper, da_total, _NN, _EXACT) + da_last
        ddt_ref[...] = ddt
        dd_ref[...] += dd

    last_c = nc - 1
    return pl.pallas_call(
        body,
        out_shape=(jax.ShapeDtypeStruct((s, di), F32),
                   jax.ShapeDtypeStruct(bm.shape, F32),
                   jax.ShapeDtypeStruct(cm.shape, F32),
                   jax.ShapeDtypeStruct(dtp.shape, F32),
                   jax.ShapeDtypeStruct(dtp.shape, F32),
                   jax.ShapeDtypeStruct(d_g.shape, F32)),
        grid=(g_n, nc),
        in_specs=[pl.BlockSpec((ln, r_n * p_n), lambda g, c: (last_c - c, g)),
                  pl.BlockSpec((ln, n_n), lambda g, c: (last_c - c, g)),
                  pl.BlockSpec((ln, n_n), lambda g, c: (last_c - c, g)),
                  pl.BlockSpec((None, ln, LANES), lambda g, c: (g, last_c - c, 0)),
                  pl.BlockSpec((None, 1, LANES), lambda g, c: (g, 0, 0)),
                  pl.BlockSpec((None, 1, LANES), lambda g, c: (g, 0, 0)),
                  pl.BlockSpec((None, r_n, p_n, n_n), lambda g, c: (last_c - c, g, 0, 0)),
                  pl.BlockSpec((ln, r_n * p_n), lambda g, c: (last_c - c, g))],
        out_specs=(pl.BlockSpec((ln, r_n * p_n), lambda g, c: (last_c - c, g)),
                   pl.BlockSpec((ln, n_n), lambda g, c: (last_c - c, g)),
                   pl.BlockSpec((ln, n_n), lambda g, c: (last_c - c, g)),
                   pl.BlockSpec((None, ln, LANES), lambda g, c: (g, last_c - c, 0)),
                   pl.BlockSpec((None, ln, LANES), lambda g, c: (g, last_c - c, 0)),
                   pl.BlockSpec((None, 1, LANES), lambda g, c: (g, 0, 0))),
        scratch_shapes=[pltpu.VMEM((r_n, p_n, n_n), F32)],
        name=name, compiler_params=_cparams(("parallel", "arbitrary")),
    )(xs, bm, cm, dtp, a_g, d_g, states, dy)


SB_TILE = 256


def _tri_sum(x, tri):
    t = x.shape[0]
    hi = x.astype(BF16)
    r1 = x - hi.astype(F32)
    mid = r1.astype(BF16)
    lo = (r1 - mid.astype(F32)).astype(BF16)
    r = _dot(jnp.concatenate([hi, mid, lo], axis=0), tri, _NN)
    return r[:t] + r[t:2 * t] + r[2 * t:]


def _sb_logits(q, k_j, scale, strict):
    z = _dot(q, k_j, _NT) * scale
    sp = jnp.log(1.0 + jnp.exp(-jnp.abs(z)))
    log_b = jnp.minimum(z, 0.0) - sp
    log_1mb = log_b - z
    if strict is not None:
        log_1mb = jnp.where(strict, log_1mb, 0.0)
    return log_b, log_1mb


def _sb_tile(s):
    return _pick(s, (SB_TILE, LANES))


def _sb_iotas(t):
    row = lax.broadcasted_iota(jnp.int32, (t, t), 0)
    col = lax.broadcasted_iota(jnp.int32, (t, t), 1)
    return row, col


def sb_attn_fwd(qn, kn, v, *, v_off=0, name):
    s, w = qn.shape
    dh = SB_HEAD_DIM
    n_h = w // dh
    t = _sb_tile(s)
    scale = 1.0 / math.sqrt(dh)

    def body(q_ref, k_ref, v_ref, o_ref, tot_ref):
        i = pl.program_id(1)
        q = q_ref[...]
        row, col = _sb_iotas(t)
        later = (row > col).astype(BF16)

        def tile(j, acc, run, strict):
            s0 = pl.multiple_of(j * t, t)
            k_j = k_ref[pl.ds(s0, t), :]
            v_j = v_ref[pl.ds(s0, t), :].astype(BF16)
            log_b, log_1mb = _sb_logits(q, k_j, scale, strict)
            att = jnp.exp(log_b + (_tri_sum(log_1mb, later) + run))
            if strict is not None:
                att = jnp.where(strict, att, 0.0)
            acc = acc + _dot(att.astype(BF16), v_j, _NN)
            return acc, run + jnp.sum(log_1mb, axis=1, keepdims=True)

        acc, run = tile(i, jnp.zeros((t, dh), F32), jnp.zeros((t, 1), F32), col < row)
        acc, run = lax.fori_loop(0, i, lambda jj, c: tile(i - 1 - jj, c[0], c[1], None), (acc, run))
        o_ref[...] = acc
        tot_ref[...] = jnp.broadcast_to(run, (t, dh))

    return pl.pallas_call(
        body,
        out_shape=(jax.ShapeDtypeStruct((s, w), F32), jax.ShapeDtypeStruct((s, w), F32)),
        grid=(n_h, s // t),
        in_specs=[pl.BlockSpec((t, dh), lambda h, i: (i, h)),
                  pl.BlockSpec((s, dh), lambda h, i: (0, h)),
                  pl.BlockSpec((s, dh), lambda h, i: (0, v_off + h))],
        out_specs=(pl.BlockSpec((t, dh), lambda h, i: (i, h)),
                   pl.BlockSpec((t, dh), lambda h, i: (i, h))),
        name=name, compiler_params=_cparams(("parallel", "parallel")),
    )(qn, kn, v)


def sb_attn_bwd(qn, kn, v, tot, do, *, v_off=0, name):
    s, w = qn.shape
    dh = SB_HEAD_DIM
    n_h = w // dh
    t = _sb_tile(s)
    scale = 1.0 / math.sqrt(dh)

    def body(q_ref, k_ref, v_ref, tot_ref, do_ref, dq_ref, dk_ref, dv_ref):
        dk_ref[...] = jnp.zeros_like(dk_ref)
        dv_ref[...] = jnp.zeros_like(dv_ref)
        row, col = _sb_iotas(t)
        upto = (row <= col).astype(BF16)
        before = (row < col).astype(BF16)

        def q_block(i, _):
            t0 = pl.multiple_of(i * t, t)
            q = q_ref[pl.ds(t0, t), :]
            do_i = do_ref[pl.ds(t0, t), :].astype(BF16)
            total = tot_ref[pl.ds(t0, t), 0:1]

            def tile(j, dq, run_l, run_g, strict):
                s0 = pl.multiple_of(j * t, t)
                k_j = k_ref[pl.ds(s0, t), :]
                v_j = v_ref[pl.ds(s0, t), :].astype(BF16)
                log_b, log_1mb = _sb_logits(q, k_j, scale, strict)
                att = jnp.exp(log_b + ((total - run_l) - _tri_sum(log_1mb, upto)))
                if strict is not None:
                    att = jnp.where(strict, att, 0.0)
                g = att * _dot(do_i, v_j, _NT)
                c = _tri_sum(g, before) + run_g
                dz = (g - (g + c) * jnp.exp(log_b)) * scale
                if strict is not None:
                    dz = jnp.where(strict, dz, 0.0)
                dz = dz.astype(BF16)
                dq = dq + _dot(dz, k_j, _NN)
                dk_ref[pl.ds(s0, t), :] += _dot(dz, q, _TN)
                dv_ref[pl.ds(s0, t), :] += _dot(att.astype(BF16), do_i, _TN)
                return (dq, run_l + jnp.sum(log_1mb, axis=1, keepdims=True),
                        run_g + jnp.sum(g, axis=1, keepdims=True))

            zero = jnp.zeros((t, 1), F32)
            carry = lax.fori_loop(0, i, lambda j, c: tile(j, c[0], c[1], c[2], None),
                                  (jnp.zeros((t, dh), F32), zero, zero))
            dq, _, _ = tile(i, carry[0], carry[1], carry[2], col < row)
            dq_ref[pl.ds(t0, t), :] = dq
            return 0

        lax.fori_loop(0, s // t, q_block, 0)

    head = pl.BlockSpec((s, dh), lambda h: (0, h))
    return pl.pallas_call(
        body,
        out_shape=tuple(jax.ShapeDtypeStruct((s, w), F32) for _ in range(3)),
        grid=(n_h,),
        in_specs=[head, head, pl.BlockSpec((s, dh), lambda h: (0, v_off + h)), head, head],
        out_specs=(head, head, head),
        name=name, compiler_params=_cparams(("parallel",)),
    )(qn, kn, v, tot, do)


ROW_TILE = 256
WIDE_ROW_TILE = 64


def _rows(width, col=0, tm=ROW_TILE):
    return pl.BlockSpec((tm, width), lambda i: (i, col))


_wide_rows = functools.partial(_rows, tm=WIDE_ROW_TILE)


def _whole(shape):
    return pl.BlockSpec(shape, lambda i: (0,) * len(shape))


def _ew_call(body, out_shape, in_specs, out_specs, args, n_rows, name, carried=False):
    return pl.pallas_call(
        body, out_shape=out_shape, grid=(n_rows // in_specs[0].block_shape[0],), in_specs=in_specs, out_specs=out_specs,
        name=name, compiler_params=_cparams(("arbitrary",) if carried else ("parallel",)),
    )(*args)


def _first_step(*refs):
    @pl.when(pl.program_id(0) == 0)
    def _():
        for r in refs:
            r[...] = jnp.zeros_like(r)


def rmsnorm_fwd(x, w, *, name):
    s, d = x.shape

    def body(x_ref, w_ref, o_ref):
        xv = x_ref[...]
        r = lax.rsqrt(jnp.mean(xv * xv, axis=-1, keepdims=True) + NORM_EPS)
        o_ref[...] = (xv * r * w_ref[...]).astype(BF16)

    return _ew_call(body, jax.ShapeDtypeStruct((s, d), BF16), [_rows(d), _whole((1, d))], _rows(d),
                    (x, w.reshape(1, d)), s, name)


def rmsnorm_bwd(x, w, dy, dres, *, name):
    s, d = x.shape

    def body(x_ref, w_ref, dy_ref, dr_ref, dx_ref, dw_ref):
        _first_step(dw_ref)
        xv = x_ref[...]
        r = lax.rsqrt(jnp.mean(xv * xv, axis=-1, keepdims=True) + NORM_EPS)
        xhat = xv * r
        dyv = dy_ref[...].astype(F32)
        dw_ref[...] += jnp.sum(dyv * xhat, axis=0, keepdims=True)
        g = dyv * w_ref[...]
        dx_ref[...] = dr_ref[...] + r * (g - xhat * jnp.mean(g * xhat, axis=-1, keepdims=True))

    return _ew_call(body, (jax.ShapeDtypeStruct((s, d), F32), jax.ShapeDtypeStruct((1, d), F32)),
                    [_rows(d), _whole((1, d)), _rows(d), _rows(d)], (_rows(d), _whole((1, d))),
                    (x, w.reshape(1, d), dy, dres), s, name, carried=True)


def ple_fwd(h1, gate_pre, pp, *, name):
    s, d = h1.shape

    def body(h_ref, g_ref, p_ref, o_ref):
        o_ref[...] = h_ref[...] + p_ref[...] * _sigmoid(g_ref[...])

    return _ew_call(body, jax.ShapeDtypeStruct((s, d), F32), [_rows(d)] * 3, _rows(d), (h1, gate_pre, pp), s, name)


def ple_bwd(dh2, gate_pre, pp, *, name):
    s, d = dh2.shape

    def body(dh_ref, g_ref, p_ref, dp_ref, dg_ref):
        gate = _sigmoid(g_ref[...])
        dh = dh_ref[...]
        dp_ref[...] = (dh * gate).astype(BF16)
        dg_ref[...] = (dh * p_ref[...] * gate * (1.0 - gate)).astype(BF16)

    shp = jax.ShapeDtypeStruct((s, d), BF16)
    return _ew_call(body, (shp, shp), [_rows(d)] * 3, (_rows(d), _rows(d)), (dh2, gate_pre, pp), s, name)


def loss_head(y, target, *, name):
    s, d = y.shape

    def body(y_ref, t_ref, l_ref, dy_ref):
        _first_step(l_ref)
        err = y_ref[...] - t_ref[...]
        per_tok = jnp.mean(err * err, axis=-1, keepdims=True)
        l_ref[...] += 0.5 * jnp.sum(per_tok, axis=0, keepdims=True)
        dy_ref[...] = err * (1.0 / d)

    return _ew_call(body, (jax.ShapeDtypeStruct((1, 1), F32), jax.ShapeDtypeStruct((s, d), F32)),
                    [_rows(d), _rows(d)], (_whole((1, 1)), _rows(d)), (y, target), s, name, carried=True)


CONV_COL_TILE = 256


def _conv_taps(x, w_ref):
    row = lax.broadcasted_iota(jnp.int32, (x.shape[0], 1), 0)
    acc = x * w_ref[SSD_D_CONV - 1:SSD_D_CONV, :]
    shifted = []
    for d in range(1, SSD_D_CONV):
        xs = jnp.where(row >= d, pltpu.roll(x, d, 0), 0.0)
        shifted.append(xs)
        acc = acc + xs * w_ref[SSD_D_CONV - 1 - d:SSD_D_CONV - d, :]
    return acc, shifted


def ssd_conv_fwd(x, w, b, *, name):
    s, c = x.shape
    tc = _pick(c, (CONV_COL_TILE, LANES))

    def body(x_ref, w_ref, b_ref, o_ref):
        pre, _ = _conv_taps(x_ref[...], w_ref)
        o_ref[...] = _silu(pre + b_ref[...])

    col = pl.BlockSpec((s, tc), lambda j: (0, j))
    return pl.pallas_call(
        body, out_shape=jax.ShapeDtypeStruct((s, c), F32), grid=(c // tc,),
        in_specs=[col, pl.BlockSpec((SSD_D_CONV, tc), lambda j: (0, j)), pl.BlockSpec((1, tc), lambda j: (0, j))],
        out_specs=col, name=name, compiler_params=_cparams(("parallel",)),
    )(x, w, b)


def ssd_conv_bwd(x, w, b, dact, *, name):
    s, c = x.shape
    tc = _pick(c, (CONV_COL_TILE, LANES))

    def body(x_ref, w_ref, b_ref, da_ref, dx_ref, dw_ref, db_ref):
        xv = x_ref[...]
        pre, shifted = _conv_taps(xv, w_ref)
        dpre = da_ref[...] * _silu_grad(pre + b_ref[...])
        db_ref[...] = jnp.sum(dpre, axis=0, keepdims=True)
        row = lax.broadcasted_iota(jnp.int32, (s, 1), 0)
        dx = dpre * w_ref[SSD_D_CONV - 1:SSD_D_CONV, :]
        dw_ref[SSD_D_CONV - 1:SSD_D_CONV, :] = jnp.sum(dpre * xv, axis=0, keepdims=True)
        for d in range(1, SSD_D_CONV):
            k = SSD_D_CONV - 1 - d
            dw_ref[k:k + 1, :] = jnp.sum(dpre * shifted[d - 1], axis=0, keepdims=True)
            up = jnp.where(row < s - d, pltpu.roll(dpre, s - d, 0), 0.0)
            dx = dx + up * w_ref[k:k + 1, :]
        dx_ref[...] = dx.astype(BF16)

    col = pl.BlockSpec((s, tc), lambda j: (0, j))
    wspec = pl.BlockSpec((SSD_D_CONV, tc), lambda j: (0, j))
    bspec = pl.BlockSpec((1, tc), lambda j: (0, j))
    return pl.pallas_call(
        body,
        out_shape=(jax.ShapeDtypeStruct((s, c), BF16), jax.ShapeDtypeStruct((SSD_D_CONV, c), F32),
                   jax.ShapeDtypeStruct((1, c), F32)),
        grid=(c // tc,), in_specs=[col, wspec, bspec, col], out_specs=(col, wspec, bspec),
        name=name, compiler_params=_cparams(("parallel",)),
    )(x, w, b, dact)


def ssd_dt_fwd(dt_raw, bias, a_log, *, name):
    s, h = dt_raw.shape

    def body(r_ref, b_ref, al_ref, dt_ref, a_ref):
        zv = r_ref[...] + b_ref[...]
        dt_ref[...] = jnp.maximum(zv, 0.0) + jnp.log(1.0 + jnp.exp(-jnp.abs(zv)))
        a_ref[...] = -jnp.exp(al_ref[...])

    full = pl.BlockSpec((s, h), lambda: (0, 0))
    vec = pl.BlockSpec((1, h), lambda: (0, 0))
    return pl.pallas_call(
        body, out_shape=(jax.ShapeDtypeStruct((s, h), F32), jax.ShapeDtypeStruct((1, h), F32)),
        in_specs=[full, vec, vec], out_specs=(full, vec), name=name, compiler_params=_cparams(),
    )(dt_raw, bias.reshape(1, h), a_log.reshape(1, h))


def ssd_dt_bwd(dt_raw, bias, a_log, dt, ddt, dadt, *, name):
    s, h = dt_raw.shape

    def body(r_ref, b_ref, al_ref, dt_ref, ddt_ref, dadt_ref, dr_ref, db_ref, dal_ref):
        a = -jnp.exp(al_ref[...])
        dadt_v = dadt_ref[...]
        d_dt = ddt_ref[...] + a * dadt_v
        d_raw = d_dt * _sigmoid(r_ref[...] + b_ref[...])
        dr_ref[...] = d_raw
        db_ref[...] = jnp.sum(d_raw, axis=0, keepdims=True)
        dal_ref[...] = jnp.sum(dadt_v * dt_ref[...], axis=0, keepdims=True) * a

    full = pl.BlockSpec((s, h), lambda: (0, 0))
    vec = pl.BlockSpec((1, h), lambda: (0, 0))
    return pl.pallas_call(
        body, out_shape=(jax.ShapeDtypeStruct((s, h), F32), jax.ShapeDtypeStruct((1, h), F32),
                         jax.ShapeDtypeStruct((1, h), F32)),
        in_specs=[full, vec, vec, full, full, full], out_specs=(full, vec, vec), name=name,
        compiler_params=_cparams(),
    )(dt_raw, bias.reshape(1, h), a_log.reshape(1, h), dt, ddt, dadt)


def _group_mean(v, n_groups):
    gw = v.shape[-1] // n_groups
    parts = [jnp.broadcast_to(jnp.mean(v[:, k * gw:(k + 1) * gw], axis=-1, keepdims=True), (v.shape[0], gw))
             for k in range(n_groups)]
    return jnp.concatenate(parts, axis=-1)


def ssd_gate_fwd(y, z, gw, *, name):
    s, di = y.shape

    def body(y_ref, z_ref, w_ref, o_ref):
        yg = y_ref[...] * _silu(z_ref[...])
        r = lax.rsqrt(_group_mean(yg * yg, SSD_N_GROUPS) + GATED_NORM_EPS)
        o_ref[...] = (yg * r * w_ref[...]).astype(BF16)

    return _ew_call(body, jax.ShapeDtypeStruct((s, di), BF16), [_wide_rows(di), _wide_rows(di), _whole((1, di))],
                    _wide_rows(di), (y, z, gw.reshape(1, di)), s, name)


def ssd_gate_bwd(y, z, gw, dyn, *, name):
    s, di = y.shape

    def body(y_ref, z_ref, w_ref, dn_ref, dy_ref, dz_ref, dw_ref):
        _first_step(dw_ref)
        yv, zv = y_ref[...], z_ref[...]
        sz = _silu(zv)
        yg = yv * sz
        r = lax.rsqrt(_group_mean(yg * yg, SSD_N_GROUPS) + GATED_NORM_EPS)
        yhat = yg * r
        dn = dn_ref[...]
        dw_ref[...] += jnp.sum(dn * yhat, axis=0, keepdims=True)
        g = dn * w_ref[...]
        dyg = r * (g - yhat * _group_mean(g * yhat, SSD_N_GROUPS))
        dy_ref[...] = dyg * sz
        dz_ref[...] = (dyg * yv * _silu_grad(zv)).astype(BF16)

    return _ew_call(body, (jax.ShapeDtypeStruct((s, di), F32), jax.ShapeDtypeStruct((s, di), BF16),
                           jax.ShapeDtypeStruct((1, di), F32)),
                    [_wide_rows(di), _wide_rows(di), _whole((1, di)), _wide_rows(di)],
                    (_wide_rows(di), _wide_rows(di), _whole((1, di))),
                    (y, z, gw.reshape(1, di), dyn), s, name, carried=True)


def _head_mean(v):
    return _group_mean(v, v.shape[-1] // SB_HEAD_DIM)


def sb_qk_fwd(proj, qw, kw, *, name):
    s, w4 = proj.shape
    w = w4 // 4
    reps = w // SB_HEAD_DIM

    def body(q_ref, k_ref, qw_ref, kw_ref, qn_ref, kn_ref):
        for x_ref, w_ref, o_ref in ((q_ref, qw_ref, qn_ref), (k_ref, kw_ref, kn_ref)):
            xv = x_ref[...]
            r = lax.rsqrt(_head_mean(xv * xv) + NORM_EPS)
            o_ref[...] = (xv * r * jnp.tile(w_ref[...], (1, reps))).astype(BF16)

    shp = jax.ShapeDtypeStruct((s, w), BF16)
    return _ew_call(body, (shp, shp), [_rows(w, 0), _rows(w, 1), _whole((1, SB_HEAD_DIM)), _whole((1, SB_HEAD_DIM))],
                    (_rows(w), _rows(w)), (proj, proj, qw.reshape(1, -1), kw.reshape(1, -1)), s, name)


def sb_gate_fwd(o, proj, *, name):
    s, w = o.shape

    def body(o_ref, g_ref, og_ref):
        og_ref[...] = (o_ref[...] * _silu(g_ref[...])).astype(BF16)

    return _ew_call(body, jax.ShapeDtypeStruct((s, w), BF16), [_rows(w), _rows(w, 3)], _rows(w), (o, proj), s, name)


def sb_gate_bwd(dog, o, proj, *, name):
    s, w = o.shape

    def body(d_ref, o_ref, g_ref, do_ref, dg_ref):
        gv, dv = g_ref[...], d_ref[...]
        do_ref[...] = dv * _silu(gv)
        dg_ref[...] = (dv * o_ref[...] * _silu_grad(gv)).astype(BF16)

    return _ew_call(body, (jax.ShapeDtypeStruct((s, w), F32), jax.ShapeDtypeStruct((s, w), BF16)),
                    [_rows(w), _rows(w), _rows(w, 3)], (_rows(w), _rows(w)), (dog, o, proj), s, name)


def sb_pack_bwd(proj, qw, kw, dqn, dkn, dv, dg, *, name):
    s, w4 = proj.shape
    w = w4 // 4
    reps = w // SB_HEAD_DIM

    def body(q_ref, k_ref, qw_ref, kw_ref, dqn_ref, dkn_ref, dv_ref, dg_ref, dp_ref, dqw_ref, dkw_ref):
        _first_step(dqw_ref, dkw_ref)
        for idx, (x_ref, w_ref, d_ref, dw_ref) in enumerate(((q_ref, qw_ref, dqn_ref, dqw_ref),
                                                           (k_ref, kw_ref, dkn_ref, dkw_ref))):
            xv = x_ref[...]
            r = lax.rsqrt(_head_mean(xv * xv) + NORM_EPS)
            xhat = xv * r
            dn = d_ref[...]
            per_col = jnp.sum(dn * xhat, axis=0, keepdims=True)
            acc = per_col[:, 0:SB_HEAD_DIM]
            for hh in range(1, reps):
                acc = acc + per_col[:, hh * SB_HEAD_DIM:(hh + 1) * SB_HEAD_DIM]
            dw_ref[...] += acc
            g = dn * jnp.tile(w_ref[...], (1, reps))
            dp_ref[:, idx * w:(idx + 1) * w] = (r * (g - xhat * _head_mean(g * xhat))).astype(BF16)
        dp_ref[:, 2 * w:3 * w] = dv_ref[...].astype(BF16)
        dp_ref[:, 3 * w:4 * w] = dg_ref[...]

    vec = _whole((1, SB_HEAD_DIM))
    return _ew_call(body, (jax.ShapeDtypeStruct((s, w4), BF16), jax.ShapeDtypeStruct((1, SB_HEAD_DIM), F32),
                           jax.ShapeDtypeStruct((1, SB_HEAD_DIM), F32)),
                    [_wide_rows(w, 0), _wide_rows(w, 1), vec, vec, _wide_rows(w), _wide_rows(w), _wide_rows(w),
                     _wide_rows(w)],
                    (_wide_rows(w4), vec, vec),
                    (proj, proj, qw.reshape(1, -1), kw.reshape(1, -1), dqn, dkn, dv, dg), s, name, carried=True)


_HBM = pl.BlockSpec(memory_space=pltpu.HBM)


def _mesh_pos():
    return lax.axis_index("x"), lax.axis_index("y"), lax.axis_index("c")


def _other_chips(x, y):
    return [(1 - x, y), (x, 1 - y), (1 - x, 1 - y)]


def all_gather(shards, *, name):
    n = len(shards)

    def body(*refs):
        x_refs, out_refs = refs[:n], refs[n:2 * n]
        send_sems, recv_sems, local_sems = refs[2 * n:]
        x, y, c = _mesh_pos()
        me, sibling = (x, y, c), (x, y, 1 - c)
        chips = _other_chips(x, y)

        def copy(a, k, block, to, src=None):
            dst = out_refs[a].at[4 * block[0] + 2 * block[1] + block[2]]
            return pltpu.make_async_remote_copy(
                src_ref=dst if src is None else src, dst_ref=dst, send_sem=send_sems.at[a, k],
                recv_sem=recv_sems.at[a, k], device_id=to, device_id_type=MESH)

        mine = [pltpu.make_async_copy(x_refs[a], out_refs[a].at[4 * x + 2 * y + c], local_sems.at[a])
                for a in range(n)]
        for cp in mine:
            cp.start()
        first = []
        for a in range(n):
            first.append(copy(a, 0, me, sibling, src=x_refs[a]))
            first += [copy(a, 1 + j, me, (*chip, c), src=x_refs[a]) for j, chip in enumerate(chips)]
        for cp in first:
            cp.start()
        passed = []
        for a in range(n):
            for j, chip in enumerate(chips):
                copy(a, 1 + j, (*chip, c), me).wait_recv()
                fwd = copy(a, 4 + j, (*chip, c), sibling)
                fwd.start()
                passed.append(fwd)
        for a in range(n):
            copy(a, 0, sibling, me).wait_recv()
            for j, chip in enumerate(chips):
                copy(a, 4 + j, (*chip, 1 - c), me).wait_recv()
        for cp in first + passed:
            cp.wait_send()
        for cp in mine:
            cp.wait()

    return pl.pallas_call(
        body, out_shape=tuple(jax.ShapeDtypeStruct((N_DEV,) + t.shape, t.dtype) for t in shards),
        in_specs=[_HBM] * n, out_specs=tuple([_HBM] * n),
        scratch_shapes=[pltpu.SemaphoreType.DMA((n, 7)), pltpu.SemaphoreType.DMA((n, 7)),
                        pltpu.SemaphoreType.DMA((n,))],
        name=name,
    )(*shards)


def sibling_exchange(grads, *, name):
    n = len(grads)

    def body(*refs):
        g_refs, r_refs = refs[:n], refs[n:2 * n]
        send_sems, recv_sems = refs[2 * n:]
        x, y, c = _mesh_pos()
        copies = [pltpu.make_async_remote_copy(
            src_ref=g_refs[a].at[:, 1 - c], dst_ref=r_refs[a], send_sem=send_sems.at[a],
            recv_sem=recv_sems.at[a], device_id=(x, y, 1 - c), device_id_type=MESH) for a in range(n)]
        for cp in copies:
            cp.start()
        for cp in copies:
            cp.wait()

    return pl.pallas_call(
        body, out_shape=tuple(jax.ShapeDtypeStruct((N_CHIP,) + g.shape[2:], g.dtype) for g in grads),
        in_specs=[_HBM] * n, out_specs=tuple([_HBM] * n),
        scratch_shapes=[pltpu.SemaphoreType.DMA((n,)), pltpu.SemaphoreType.DMA((n,))],
        name=name,
    )(*grads)


def chip_exchange(sums, *, name):
    n = len(sums)

    def body(*refs):
        s_refs, r_refs = refs[:n], refs[n:2 * n]
        send_sems, recv_sems = refs[2 * n:]
        x, y, c = _mesh_pos()
        copies = []
        for a in range(n):
            for j, chip in enumerate(_other_chips(x, y)):
                copies.append(pltpu.make_async_remote_copy(
                    src_ref=s_refs[a].at[2 * chip[0] + chip[1]], dst_ref=r_refs[a].at[j],
                    send_sem=send_sems.at[a, j], recv_sem=recv_sems.at[a, j],
                    device_id=(*chip, c), device_id_type=MESH))
        for cp in copies:
            cp.start()
        for cp in copies:
            cp.wait()

    return pl.pallas_call(
        body, out_shape=tuple(jax.ShapeDtypeStruct((N_CHIP - 1,) + t.shape[1:], t.dtype) for t in sums),
        in_specs=[_HBM] * n, out_specs=tuple([_HBM] * n),
        scratch_shapes=[pltpu.SemaphoreType.DMA((n, N_CHIP - 1)), pltpu.SemaphoreType.DMA((n, N_CHIP - 1))],
        name=name,
    )(*sums)


def all_reduce_small(v, *, name):
    r = v.shape[0]

    def body(v_ref, o_ref, buf, send_sems, recv_sems):
        x, y, c = _mesh_pos()
        me = 4 * x + 2 * y + c
        buf[me] = v_ref[...]
        copies = []
        for k in range(1, N_DEV):
            to = ((x + (k >> 2)) % 2, (y + ((k >> 1) & 1)) % 2, (c + (k & 1)) % 2)
            copies.append(pltpu.make_async_remote_copy(
                src_ref=v_ref, dst_ref=buf.at[me], send_sem=send_sems.at[k - 1], recv_sem=recv_sems.at[k - 1],
                device_id=to, device_id_type=MESH))
        for cp in copies:
            cp.start()
        for cp in copies:
            cp.wait()
        acc = buf[0]
        for d in range(1, N_DEV):
            acc = acc + buf[d]
        o_ref[...] = acc

    vm = pl.BlockSpec(memory_space=pltpu.VMEM)
    return pl.pallas_call(
        body, out_shape=jax.ShapeDtypeStruct(v.shape, F32), in_specs=[vm], out_specs=vm,
        scratch_shapes=[pltpu.VMEM((N_DEV, r, LANES), F32), pltpu.SemaphoreType.DMA((N_DEV - 1,)),
                        pltpu.SemaphoreType.DMA((N_DEV - 1,))],
        name=name,
    )(v)


def pair_add(g, r1, core, *, name):
    _, _, rows, cols = g.shape
    tm = _pick(rows, (256, 128))

    def body(c_ref, g_ref, r_ref, o_ref):
        o_ref[...] = (g_ref[...].astype(F32) + r_ref[...].astype(F32)).astype(o_ref.dtype)

    return pl.pallas_call(
        body, out_shape=jax.ShapeDtypeStruct(r1.shape, g.dtype),
        grid_spec=pltpu.PrefetchScalarGridSpec(
            num_scalar_prefetch=1, grid=(N_CHIP, rows // tm),
            in_specs=[pl.BlockSpec((None, None, tm, cols), lambda k, i, c_ref: (k, c_ref[0], i, 0)),
                      pl.BlockSpec((None, tm, cols), lambda k, i, c_ref: (k, i, 0))],
            out_specs=pl.BlockSpec((None, tm, cols), lambda k, i, c_ref: (k, i, 0))),
        name=name, compiler_params=_cparams(("parallel", "parallel")),
    )(core, g, r1)


def _adamw_math(w, g, m, v):
    m = ADAM_B1 * m + (1.0 - ADAM_B1) * g
    v = ADAM_B2 * v + (1.0 - ADAM_B2) * (g * g)
    m_hat = m / (1.0 - ADAM_B1 ** ADAM_STEP)
    v_hat = v / (1.0 - ADAM_B2 ** ADAM_STEP)
    delta = -ADAM_LR * (m_hat / (jnp.sqrt(v_hat) + ADAM_EPS) + ADAM_WD * w)
    return delta, m, v


def adamw_sharded(w, m, v, chip_sums, received, chip, *, name):
    rows, cols = w.shape
    tm = _pick(rows, (256, 128))

    def body(k_ref, w_ref, m_ref, v_ref, t_ref, r_ref, g_ref, d_ref, nm_ref, nv_ref):
        g = t_ref[...].astype(F32)
        for j in range(N_CHIP - 1):
            g = g + r_ref[j].astype(F32)
        d, mm, vv = _adamw_math(w_ref[...], g, m_ref[...], v_ref[...])
        g_ref[...] = g
        d_ref[...] = d
        nm_ref[...] = mm
        nv_ref[...] = vv

    blk = pl.BlockSpec((tm, cols), lambda i, k_ref: (i, 0))
    shp = jax.ShapeDtypeStruct((rows, cols), F32)
    return pl.pallas_call(
        body, out_shape=(shp, shp, shp, shp),
        grid_spec=pltpu.PrefetchScalarGridSpec(
            num_scalar_prefetch=1, grid=(rows // tm,),
            in_specs=[blk, blk, blk,
                      pl.BlockSpec((None, tm, cols), lambda i, k_ref: (k_ref[0], i, 0)),
                      pl.BlockSpec((N_CHIP - 1, tm, cols), lambda i, k_ref: (0, i, 0))],
            out_specs=(blk, blk, blk, blk)),
        name=name, compiler_params=_cparams(("parallel",)),
    )(chip, w, m, v, chip_sums, received)


def adamw_replicated(w, m, v, g, *, name):
    def body(w_ref, m_ref, v_ref, g_ref, d_ref, nm_ref, nv_ref):
        d, mm, vv = _adamw_math(w_ref[...], g_ref[...], m_ref[...], v_ref[...])
        d_ref[...] = d
        nm_ref[...] = mm
        nv_ref[...] = vv

    shp = jax.ShapeDtypeStruct(w.shape, F32)
    return pl.pallas_call(body, out_shape=(shp, shp, shp), name=name, compiler_params=_cparams())(w, m, v, g)


WEIGHT_NAMES = ("norm_w", "ssd_in_w", "ssd_conv_w", "ssd_conv_b", "ssd_dt_bias", "ssd_a_log", "ssd_d",
                "ssd_gnorm_w", "ssd_out_w", "sb_in_w", "sb_qn_w", "sb_kn_w", "sb_out_w", "ple_norm_w",
                "ple_gate_w", "ple_proj_w")
REPLICATED = ("norm_w", "ssd_conv_b", "ssd_dt_bias", "ssd_a_log", "ssd_d", "ssd_gnorm_w", "sb_qn_w", "sb_kn_w",
              "ple_norm_w")
PACK_ROWS = 8


def _pack(parts):
    flat = jnp.concatenate([t.reshape(-1) for t in parts])
    pad = (-flat.shape[0]) % (PACK_ROWS * LANES)
    return jnp.pad(flat, (0, pad)).reshape(-1, LANES)


def _unpack(packed, like):
    flat = packed.reshape(-1)
    out, off = [], 0
    for t in like:
        out.append(flat[off:off + t.size].reshape(t.shape))
        off += t.size
    return out


def _to_group_lanes(v, r):
    t = v.reshape(v.shape[0], SSD_N_GROUPS, r).transpose(1, 0, 2)
    return jnp.pad(t, ((0, 0), (0, 0), (0, LANES - r)))


def _from_group_lanes(t, r):
    return t[:, :, :r].transpose(1, 0, 2).reshape(t.shape[1], SSD_N_GROUPS * r)


def _head_vec(v, r):
    return jnp.pad(v.reshape(SSD_N_GROUPS, 1, r), ((0, 0), (0, 0), (0, LANES - r)))


def _col_blocks(full):
    rows = full.shape[0]
    return full.reshape(rows, N_DEV, -1).transpose(1, 0, 2)


def _from_col_blocks(blocks):
    return blocks.transpose(1, 0, 2).reshape(blocks.shape[1], -1)


def _split_cols(full, widths):
    out, off = [], 0
    for w in widths:
        out.append(full[:, off:off + w])
        off += w
    return out


def kernel(x, p, norm_w, ssd_in_w, ssd_conv_w, ssd_conv_b, ssd_dt_bias, ssd_a_log, ssd_d, ssd_gnorm_w, ssd_out_w, sb_in_w, sb_qn_w, sb_kn_w, sb_out_w, ple_norm_w, ple_gate_w, ple_proj_w, loss_target, m_norm_w, m_ssd_in_w, m_ssd_conv_w, m_ssd_conv_b, m_ssd_dt_bias, m_ssd_a_log, m_ssd_d, m_ssd_gnorm_w, m_ssd_out_w, m_sb_in_w, m_sb_qn_w, m_sb_kn_w, m_sb_out_w, m_ple_norm_w, m_ple_gate_w, m_ple_proj_w, v_norm_w, v_ssd_in_w, v_ssd_conv_w, v_ssd_conv_b, v_ssd_dt_bias, v_ssd_a_log, v_ssd_d, v_ssd_gnorm_w, v_ssd_out_w, v_sb_in_w, v_sb_qn_w, v_sb_kn_w, v_sb_out_w, v_ple_norm_w, v_ple_gate_w, v_ple_proj_w):
    env = dict(locals())
    wts = {n: env[n] for n in WEIGHT_NAMES}
    mom1 = {n: env["m_" + n] for n in WEIGHT_NAMES}
    mom2 = {n: env["v_" + n] for n in WEIGHT_NAMES}

    s, d = x.shape[1], x.shape[2]
    depth = norm_w.shape[0]
    n_ssd, n_sb = ssd_in_w.shape[0], sb_in_w.shape[0]
    di = ssd_out_w.shape[1] * N_DEV
    n_heads = ssd_dt_bias.shape[1]
    hpg = n_heads // SSD_N_GROUPS
    nbc = SSD_N_GROUPS * SSD_D_STATE
    in_segs = (di, di, nbc, nbc, n_heads)
    conv_segs = (di, nbc, nbc)
    sb_w = sb_out_w.shape[1] * N_DEV
    xi, yi, ci = _mesh_pos()
    core = ci.astype(jnp.int32).reshape(1)
    chip = (2 * xi + yi).astype(jnp.int32).reshape(1)

    shards = []
    for j in range(n_ssd):
        shards += [ssd_in_w[j].astype(BF16), ssd_conv_w[j], ssd_out_w[j].astype(BF16)]
    for j in range(n_sb):
        shards += [sb_in_w[j].astype(BF16), sb_out_w[j].astype(BF16)]
    for i in range(depth):
        shards += [ple_gate_w[i].astype(BF16), ple_proj_w[i].astype(BF16)]
    gathered = list(all_gather(shards, name="all_gather_weights"))
    ssd_full, sb_full, ple_full = [], [], []
    for j in range(n_ssd):
        g_in, g_conv, g_out = gathered[3 * j:3 * j + 3]
        ssd_full.append(dict(
            w_in=_split_cols(_from_col_blocks(g_in), in_segs),
            conv_w=_split_cols(_from_col_blocks(g_conv), conv_segs),
            conv_b=_split_cols(ssd_conv_b[j].reshape(1, -1), conv_segs),
            w_out=g_out.reshape(di, d)))
    base = 3 * n_ssd
    for j in range(n_sb):
        sb_full.append(dict(w_in=gathered[base + 2 * j], w_out=gathered[base + 2 * j + 1].reshape(sb_w, d)))
    base += 2 * n_sb
    for i in range(depth):
        ple_full.append(dict(w_gate=gathered[base + 2 * i].reshape(d, d), w_proj=gathered[base + 2 * i + 1]))

    h = x.reshape(s, d)
    saved = []
    for i in range(depth):
        j = i // 2
        sv = dict(h_in=h)
        u = rmsnorm_fwd(h, norm_w[i], name=f"l{i}_norm")
        sv["u"] = u
        if i % 2 == 0:
            fw = ssd_full[j]
            raw = [matmul(u, wseg, name=f"l{i}_in{q}") for q, wseg in enumerate(fw["w_in"])]
            z, dt_raw = raw[0], raw[4]
            act = [ssd_conv_fwd(raw[1 + q], fw["conv_w"][q], fw["conv_b"][q], name=f"l{i}_conv{q}") for q in range(3)]
            dt, a_neg = ssd_dt_fwd(dt_raw, ssd_dt_bias[j], ssd_a_log[j], name=f"l{i}_dt")
            dtp = _to_group_lanes(dt, hpg)
            a_g = _head_vec(a_neg.reshape(-1), hpg)
            d_g = _head_vec(ssd_d[j], hpg)
            y, states = ssd_scan_fwd(act[0], act[1], act[2], dtp, a_g, d_g, heads_per_group=hpg, name=f"l{i}_scan")
            yn = ssd_gate_fwd(y, z, ssd_gnorm_w[j], name=f"l{i}_gate")
            h1 = matmul(yn, fw["w_out"], res=h, name=f"l{i}_out")
            sv.update(raw=raw, act=act, dt=dt, dtp=dtp, a_g=a_g, d_g=d_g, y=y, states=states, yn=yn)
        else:
            fw = sb_full[j]
            proj = matmul(u, fw["w_in"], name=f"l{i}_in")
            qn, kn = sb_qk_fwd(proj, sb_qn_w[j], sb_kn_w[j], name=f"l{i}_qknorm")
            v_off = 2 * sb_w // SB_HEAD_DIM
            o, tot = sb_attn_fwd(qn, kn, proj, v_off=v_off, name=f"l{i}_attn")
            og = sb_gate_fwd(o, proj, name=f"l{i}_gate")
            h1 = matmul(og, fw["w_out"], res=h, name=f"l{i}_out")
            sv.update(proj=proj, qn=qn, kn=kn, o=o, tot=tot, og=og, v_off=v_off)
        t = rmsnorm_fwd(h1, ple_norm_w[i], name=f"l{i}_plenorm")
        gate_pre = matmul(t, ple_full[i]["w_gate"], name=f"l{i}_plegate")
        pp = matmul(p[i, 0], ple_full[i]["w_proj"], name=f"l{i}_pleproj")
        h = ple_fwd(h1, gate_pre, pp, name=f"l{i}_ple")
        sv.update(h1=h1, t=t, gate_pre=gate_pre, pp=pp)
        saved.append(sv)

    loss_part, dh = loss_head(h, loss_target.reshape(s, d), name="loss_head")
    loss = lax.psum(loss_part[0, 0], ("x", "y", "c"))

    big = {}
    small = {n: [None] * wts[n].shape[0] for n in REPLICATED}
    for i in reversed(range(depth)):
        j = i // 2
        sv = saved[i]
        dpp, dgp = ple_bwd(dh, sv["gate_pre"], sv["pp"], name=f"b{i}_ple")
        big["ple_proj_w", i] = matmul(p[i, 0], dpp, mode="tn", out_dtype=BF16, out_blocks=ple_proj_w.shape[2],
                                      name=f"b{i}_pleproj_w")
        big["ple_gate_w", i] = matmul(sv["t"], dgp, mode="tn", out_dtype=BF16, name=f"b{i}_plegate_w").reshape(N_DEV, -1, d)
        dt_ = matmul(dgp, ple_full[i]["w_gate"], mode="nt", name=f"b{i}_plegate_x")
        dh1, g_pn = rmsnorm_bwd(sv["h1"], ple_norm_w[i], dt_, dh, name=f"b{i}_plenorm")
        small["ple_norm_w"][i] = g_pn
        u = sv["u"]
        if i % 2 == 0:
            fw = ssd_full[j]
            raw, act = sv["raw"], sv["act"]
            big["ssd_out_w", j] = matmul(sv["yn"], dh1, mode="tn", out_dtype=BF16, name=f"b{i}_out_w").reshape(N_DEV, -1, d)
            dyn = matmul(dh1, fw["w_out"], mode="nt", name=f"b{i}_out_x")
            dy, dz, g_gn = ssd_gate_bwd(sv["y"], raw[0], ssd_gnorm_w[j], dyn, name=f"b{i}_gate")
            dxs, dbm, dcm, ddtp, dadtp, dd_g = ssd_scan_bwd(act[0], act[1], act[2], sv["dtp"], sv["a_g"], sv["d_g"],
                                                          sv["states"], dy, heads_per_group=hpg, name=f"b{i}_scan")
            ddt_raw, g_dtb, g_alog = ssd_dt_bwd(raw[4], ssd_dt_bias[j], ssd_a_log[j], sv["dt"],
                                                _from_group_lanes(ddtp, hpg), _from_group_lanes(dadtp, hpg),
                                                name=f"b{i}_dt")
            conv_back = [ssd_conv_bwd(raw[1 + q], fw["conv_w"][q], fw["conv_b"][q], dact, name=f"b{i}_conv{q}")
                         for q, dact in enumerate((dxs, dbm, dcm))]
            dsegs = [dz] + [cb[0] for cb in conv_back] + [ddt_raw]
            g_in = jnp.concatenate([matmul(u, ds, mode="tn", out_dtype=BF16, name=f"b{i}_in{q}_w")
                                    for q, ds in enumerate(dsegs)], axis=1)
            big["ssd_in_w", j] = _col_blocks(g_in)
            big["ssd_conv_w", j] = _col_blocks(jnp.concatenate([cb[1] for cb in conv_back], axis=1))
            du = None
            for q, (ds, wseg) in enumerate(zip(dsegs, fw["w_in"])):
                du = matmul(ds, wseg, mode="nt", res=du, name=f"b{i}_in{q}_x")
            small["ssd_conv_b"][j] = jnp.concatenate([cb[2] for cb in conv_back], axis=1)
            small["ssd_dt_bias"][j] = g_dtb
            small["ssd_a_log"][j] = g_alog
            small["ssd_d"][j] = dd_g[:, 0, :hpg]
            small["ssd_gnorm_w"][j] = g_gn
        else:
            fw = sb_full[j]
            proj = sv["proj"]
            big["sb_out_w", j] = matmul(sv["og"], dh1, mode="tn", out_dtype=BF16, name=f"b{i}_out_w").reshape(N_DEV, -1, d)
            dog = matmul(dh1, fw["w_out"], mode="nt", name=f"b{i}_out_x")
            do, dg = sb_gate_bwd(dog, sv["o"], proj, name=f"b{i}_gate")
            dqn, dkn, dv = sb_attn_bwd(sv["qn"], sv["kn"], proj, sv["tot"], do, v_off=sv["v_off"], name=f"b{i}_attn")
            dproj, g_qn, g_kn = sb_pack_bwd(proj, sb_qn_w[j], sb_kn_w[j], dqn, dkn, dv, dg, name=f"b{i}_qknorm")
            big["sb_in_w", j] = matmul(u, dproj, mode="tn", out_dtype=BF16, out_blocks=sb_in_w.shape[2], name=f"b{i}_in_w")
            du = matmul(dproj, fw["w_in"], mode="nt", name=f"b{i}_in_x")
            small["sb_qn_w"][j] = g_qn
            small["sb_kn_w"][j] = g_kn
        dh, g_n = rmsnorm_bwd(sv["h_in"], norm_w[i], du, dh1, name=f"b{i}_norm")
        small["norm_w"][i] = g_n
    grad_x = dh.reshape(x.shape)

    rep_like = [wts[n] for n in REPLICATED]
    g_packed = all_reduce_small(_pack([jnp.stack([t.reshape(-1) for t in small[n]]) for n in REPLICATED]),
                                name="all_reduce_small_grads")
    d_packed, m_packed, v_packed = adamw_replicated(
        _pack(rep_like), _pack([mom1[n] for n in REPLICATED]), _pack([mom2[n] for n in REPLICATED]), g_packed,
        name="adamw_replicated")
    grads = dict(zip(REPLICATED, _unpack(g_packed, rep_like)))
    deltas = dict(zip(REPLICATED, _unpack(d_packed, rep_like)))
    new_m = dict(zip(REPLICATED, _unpack(m_packed, rep_like)))
    new_v = dict(zip(REPLICATED, _unpack(v_packed, rep_like)))

    keys = sorted(big)
    blocks = [big[k].reshape(N_CHIP, 2, *big[k].shape[1:]) for k in keys]
    from_sibling = sibling_exchange(blocks, name="rs_sibling_exchange")
    chip_sums = [pair_add(g, r1, core, name=f"rs_pair_add_{k[0]}{k[1]}") for k, g, r1 in zip(keys, blocks, from_sibling)]
    from_chips = chip_exchange(chip_sums, name="rs_chip_exchange")
    per_layer = {}
    for k, t_sum, recv in zip(keys, chip_sums, from_chips):
        n, idx = k
        per_layer[k] = adamw_sharded(wts[n][idx], mom1[n][idx], mom2[n][idx], t_sum, recv, chip,
                                     name=f"adamw_{n}{idx}")
    for n in WEIGHT_NAMES:
        if n in REPLICATED:
            continue
        layers = [per_layer[n, idx] for idx in range(wts[n].shape[0])]
        grads[n], deltas[n], new_m[n], new_v[n] = (jnp.stack([lay[q] for lay in layers]) for q in range(4))

    return (loss, grad_x, *[grads[n] for n in WEIGHT_NAMES], *[deltas[n] for n in WEIGHT_NAMES],
            *[new_m[n] for n in WEIGHT_NAMES], *[new_v[n] for n in WEIGHT_NAMES])
```

```python
import functools
import math

import jax
import jax.numpy as jnp
from jax import lax
from jax.experimental import pallas as pl
from jax.experimental.pallas import tpu as pltpu

F32 = jnp.float32
BF16 = jnp.bfloat16
MESH = pl.DeviceIdType.MESH

N_DEV = 8
N_CHIP = 4
LANES = 128
VMEM_LIMIT_BYTES = 48 * 1024 * 1024
MATMUL_TILE_BYTES = 28 * 1024 * 1024

NORM_EPS = 1e-6
GATED_NORM_EPS = 1e-5
SSD_HEAD_DIM = 64
SSD_N_GROUPS = 8
SSD_D_STATE = 128
SSD_D_CONV = 4
SSD_CHUNK = 128
SB_HEAD_DIM = 128
PLE_DIM = 256

ADAM_LR = 0.001
ADAM_B1 = 0.9
ADAM_B2 = 0.999
ADAM_EPS = 1e-08
ADAM_WD = 0.01
ADAM_STEP = 10


def _cparams(sem=None, **kw):
    return pltpu.CompilerParams(dimension_semantics=sem, vmem_limit_bytes=VMEM_LIMIT_BYTES, **kw)


def _pick(dim, prefs):
    for t in prefs:
        if dim % t == 0:
            return t
    return dim


def _sigmoid(x):
    return 1.0 / (1.0 + jnp.exp(-x))


def _silu(x):
    return x * _sigmoid(x)


def _silu_grad(x):
    s = _sigmoid(x)
    return s * (1.0 + x * (1.0 - s))


def matmul(a, b, *, mode="nn", out_dtype=F32, res=None, out_blocks=None, name):
    b_blocked = b.ndim == 3
    if mode == "nn":
        m, kc = a.shape
        n = b.shape[-1] * (N_DEV if b_blocked else 1)
    elif mode == "nt":
        m, kc = a.shape
        n = b.shape[-2]
    else:
        kc, m = a.shape
        n = b.shape[-1]
    nb = b.shape[-1] if b_blocked else None
    tn = _pick(n if not out_blocks else out_blocks, (512, 256, 128))
    if b_blocked and mode == "nn":
        tn = _pick(nb, (512, 256, 128))
    k_limit = nb if (b_blocked and mode == "nt") else kc
    tm, tk = None, None
    for tm_try in (1024, 512, 256, 128):
        if m % tm_try:
            continue
        for tk_try in (k_limit, 2048, 1024, 512, 256, 128):
            if tk_try > k_limit or k_limit % tk_try:
                continue
            tiles = 2 * (tm_try * tk_try * a.dtype.itemsize + tk_try * tn * b.dtype.itemsize)
            tiles += tm_try * tn * (2 * jnp.dtype(out_dtype).itemsize + 4 + (8 if res is not None else 0))
            if tiles <= MATMUL_TILE_BYTES:
                tm, tk = tm_try, tk_try
                break
        if tm:
            break
    if tm is None:
        tm, tk = m, k_limit
    nk = kc // tk
    grid = (m // tm, n // tn, nk)

    if mode == "tn":
        a_spec = pl.BlockSpec((tk, tm), lambda i, j, k: (k, i))
        dims = (((0,), (0,)), ((), ()))
    else:
        a_spec = pl.BlockSpec((tm, tk), lambda i, j, k: (i, k))
        dims = (((1,), (0,)), ((), ())) if mode == "nn" else (((1,), (1,)), ((), ()))
    if mode == "nt":
        if b_blocked:
            per = nb // tk
            b_spec = pl.BlockSpec((None, tn, tk), lambda i, j, k: (k // per, j, k % per))
        else:
            b_spec = pl.BlockSpec((tn, tk), lambda i, j, k: (j, k))
    else:
        if b_blocked:
            per = nb // tn
            b_spec = pl.BlockSpec((None, tk, tn), lambda i, j, k: (j // per, k, j % per))
        else:
            b_spec = pl.BlockSpec((tk, tn), lambda i, j, k: (k, j))
    if out_blocks:
        per_o = out_blocks // tn
        out_shape = jax.ShapeDtypeStruct((n // out_blocks, m, out_blocks), out_dtype)
        out_spec = pl.BlockSpec((None, tm, tn), lambda i, j, k: (j // per_o, i, j % per_o))
    else:
        out_shape = jax.ShapeDtypeStruct((m, n), out_dtype)
        out_spec = pl.BlockSpec((tm, tn), lambda i, j, k: (i, j))
    in_specs = [a_spec, b_spec]
    args = [a, b]
    if res is not None:
        in_specs.append(pl.BlockSpec((tm, tn), lambda i, j, k: (i, j)))
        args.append(res)

    def body(*refs):
        a_ref, b_ref = refs[:2]
        r_ref = refs[2] if res is not None else None
        o_ref = refs[3] if res is not None else refs[2]

        def finish(r):
            if res is not None:
                r = r + r_ref[...].astype(F32)
            o_ref[...] = r.astype(out_dtype)

        part = lax.dot_general(a_ref[...].astype(BF16), b_ref[...].astype(BF16), dims, preferred_element_type=F32)
        if nk == 1:
            finish(part)
            return
        acc_ref = refs[-1]
        k = pl.program_id(2)

        @pl.when(k == 0)
        def _():
            acc_ref[...] = part

        @pl.when(k > 0)
        def _():
            acc_ref[...] += part

        @pl.when(k == nk - 1)
        def _():
            finish(acc_ref[...])

    return pl.pallas_call(
        body, out_shape=out_shape, grid=grid, in_specs=in_specs, out_specs=out_spec,
        scratch_shapes=[] if nk == 1 else [pltpu.VMEM((tm, tn), F32)], name=name,
        compiler_params=_cparams(("parallel", "parallel", "arbitrary")),
    )(*args)


def _dot(a, b, dims, precision=None):
    return lax.dot_general(a, b, (dims, ((), ())), preferred_element_type=F32, precision=precision)


_NN = ((1,), (0,))
_NT = ((1,), (1,))
_TN = ((0,), (0,))
_EXACT = lax.Precision.HIGHEST


def _chunk_decay_terms(dt, a):
    ln = dt.shape[0]
    row = lax.broadcasted_iota(jnp.int32, (ln, ln), 0)
    col = lax.broadcasted_iota(jnp.int32, (ln, ln), 1)
    tri = (row >= col).astype(F32)
    a_col = _dot(tri, dt * a, _NN, _EXACT)
    return a_col, a_col.T, row >= col


def ssd_scan_fwd(xs, bm, cm, dtp, a_g, d_g, *, heads_per_group, name):
    s, di = xs.shape
    g_n = SSD_N_GROUPS
    r_n, p_n, n_n, ln = heads_per_group, SSD_HEAD_DIM, SSD_D_STATE, SSD_CHUNK
    nc = s // ln

    def body(xs_ref, bm_ref, cm_ref, dt_ref, a_ref, d_ref, y_ref, st_ref, state):
        c = pl.program_id(1)

        @pl.when(c == 0)
        def _():
            state[...] = jnp.zeros_like(state)

        dt = dt_ref[...]
        a_col_all, a_row_all, causal = _chunk_decay_terms(dt, a_ref[...])
        bmb = bm_ref[...].astype(BF16)
        cmb = cm_ref[...].astype(BF16)
        scores = _dot(cmb, bmb, _NT)
        d_all = d_ref[...]
        for r in range(r_n):
            a_col = a_col_all[:, r:r + 1]
            a_row = a_row_all[r:r + 1, :]
            decay = jnp.exp(jnp.where(causal, a_col - a_row, -jnp.inf))
            x_r = xs_ref[:, r * p_n:(r + 1) * p_n]
            xdt = x_r * dt[:, r:r + 1]
            s_r = state[r]
            st_ref[r] = s_r
            y = _dot((scores * decay).astype(BF16), xdt.astype(BF16), _NN)
            y = y + jnp.exp(a_col) * _dot(cmb, s_r.astype(BF16), _NT)
            y_ref[:, r * p_n:(r + 1) * p_n] = y + d_all[:, r:r + 1] * x_r
            a_last = a_col[ln - 1:ln, :]
            to_end = jnp.exp(a_last - a_col)
            state[r] = s_r * jnp.exp(a_last) + _dot((xdt * to_end).astype(BF16), bmb, _TN)

    return pl.pallas_call(
        body,
        out_shape=(jax.ShapeDtypeStruct((s, di), F32),
                   jax.ShapeDtypeStruct((nc, g_n * r_n, p_n, n_n), F32)),
        grid=(g_n, nc),
        in_specs=[pl.BlockSpec((ln, r_n * p_n), lambda g, c: (c, g)),
                  pl.BlockSpec((ln, n_n), lambda g, c: (c, g)),
                  pl.BlockSpec((ln, n_n), lambda g, c: (c, g)),
                  pl.BlockSpec((None, ln, LANES), lambda g, c: (g, c, 0)),
                  pl.BlockSpec((None, 1, LANES), lambda g, c: (g, 0, 0)),
                  pl.BlockSpec((None, 1, LANES), lambda g, c: (g, 0, 0))],
        out_specs=(pl.BlockSpec((ln, r_n * p_n), lambda g, c: (c, g)),
                   pl.BlockSpec((None, r_n, p_n, n_n), lambda g, c: (c, g, 0, 0))),
        scratch_shapes=[pltpu.VMEM((r_n, p_n, n_n), F32)],
        name=name, compiler_params=_cparams(("parallel", "arbitrary")),
    )(xs, bm, cm, dtp, a_g, d_g)


def ssd_scan_bwd(xs, bm, cm, dtp, a_g, d_g, states, dy, *, heads_per_group, name):
    s, di = xs.shape
    g_n = SSD_N_GROUPS
    r_n, p_n, n_n, ln = heads_per_group, SSD_HEAD_DIM, SSD_D_STATE, SSD_CHUNK
    nc = s // ln

    def body(xs_ref, bm_ref, cm_ref, dt_ref, a_ref, d_ref, st_ref, dy_ref,
             dxs_ref, dbm_ref, dcm_ref, ddt_ref, dadt_ref, dd_ref, dstate):
        c = pl.program_id(1)

        @pl.when(c == 0)
        def _():
            dstate[...] = jnp.zeros_like(dstate)
            dd_ref[...] = jnp.zeros_like(dd_ref)

        dt = dt_ref[...]
        a_col_all, a_row_all, causal = _chunk_decay_terms(dt, a_ref[...])
        bm_f = bm_ref[...]
        cm_f = cm_ref[...]
        bmb = bm_f.astype(BF16)
        cmb = cm_f.astype(BF16)
        scores = _dot(cmb, bmb, _NT)
        d_all = d_ref[...]
        lane = lax.broadcasted_iota(jnp.int32, (1, LANES), 1)
        sub = lax.broadcasted_iota(jnp.int32, (LANES, 1), 0)
        dscores = jnp.zeros((ln, ln), F32)
        dcm = jnp.zeros((ln, n_n), F32)
        dbm = jnp.zeros((ln, n_n), F32)
        da_cols = jnp.zeros((ln, LANES), F32)
        da_rows = jnp.zeros((LANES, ln), F32)
        da_last = jnp.zeros((1, LANES), F32)
        ddt = jnp.zeros((ln, LANES), F32)
        dd = jnp.zeros((1, LANES), F32)
        for r in range(r_n):
            pick_l = (lane == r).astype(F32)
            pick_s = (sub == r).astype(F32)
            a_col = a_col_all[:, r:r + 1]
            a_row = a_row_all[r:r + 1, :]
            decay = jnp.exp(jnp.where(causal, a_col - a_row, -jnp.inf))
            x_r = xs_ref[:, r * p_n:(r + 1) * p_n]
            dt_r = dt[:, r:r + 1]
            xdt = x_r * dt_r
            xdtb = xdt.astype(BF16)
            dy_r = dy_ref[:, r * p_n:(r + 1) * p_n]
            dyb = dy_r.astype(BF16)
            s_in = st_ref[r]
            ds_out = dstate[r]
            ds_outb = ds_out.astype(BF16)
            m_mat = scores * decay
            a_last = a_col[ln - 1:ln, :]
            e_last = jnp.exp(a_last)
            to_end = jnp.exp(a_last - a_col)
            e_col = jnp.exp(a_col)
            dm = _dot(dyb, xdtb, _NT)
            dscores = dscores + dm * decay
            e_mat = dm * m_mat
            da_cols = da_cols + jnp.sum(e_mat, axis=1, keepdims=True) * pick_l
            da_rows = da_rows - pick_s * jnp.sum(e_mat, axis=0, keepdims=True)
            dxdt = _dot(m_mat.astype(BF16), dyb, _TN)
            y_off = e_col * _dot(cmb, s_in.astype(BF16), _NT)
            dy_e = (dy_r * e_col).astype(BF16)
            dcm = dcm + _dot(dy_e, s_in.astype(BF16), _NN)
            ds_in = _dot(dy_e, cmb, _TN)
            da_cols = da_cols + jnp.sum(dy_r * y_off, axis=1, keepdims=True) * pick_l
            bds = _dot(bmb, ds_outb, _NT)
            dxdt = dxdt + to_end * bds
            xdt_e = xdt * to_end
            dbm = dbm + _dot(xdt_e.astype(BF16), ds_outb, _NN)
            w_col = jnp.sum(xdt_e * bds, axis=1, keepdims=True)
            da_cols = da_cols - w_col * pick_l
            last = jnp.sum(w_col, axis=0, keepdims=True) + e_last * jnp.sum(
                jnp.sum(ds_out * s_in, axis=1, keepdims=True), axis=0, keepdims=True)
            da_last = da_last + last * pick_l
            dstate[r] = ds_out * e_last + ds_in
            dxs_ref[:, r * p_n:(r + 1) * p_n] = dxdt * dt_r + d_all[:, r:r + 1] * dy_r
            ddt = ddt + jnp.sum(dxdt * x_r, axis=1, keepdims=True) * pick_l
            dd = dd + jnp.sum(jnp.sum(dy_r * x_r, axis=1, keepdims=True), axis=0, keepdims=True) * pick_l
        dsb = dscores.astype(BF16)
        dcm_ref[...] = dcm + _dot(dsb, bmb, _NN)
        dbm_ref[...] = dbm + _dot(dsb, cmb, _TN)
        da_total = da_cols + da_rows.T
        row = lax.broadcasted_iota(jnp.int32, (ln, ln), 0)
        col = lax.broadcasted_iota(jnp.int32, (ln, ln), 1)
        upper = (col >= row).astype(F32)
        dadt_ref[...] = _dot(upper, da_total, _NN, _EXACT) + da_last
        ddt_ref[...] = ddt
        dd_ref[...] += dd

    last_c = nc - 1
    return pl.pallas_call(
        body,
        out_shape=(jax.ShapeDtypeStruct((s, di), F32),
                   jax.ShapeDtypeStruct(bm.shape, F32),
                   jax.ShapeDtypeStruct(cm.shape, F32),
                   jax.ShapeDtypeStruct(dtp.shape, F32),
                   jax.ShapeDtypeStruct(dtp.shape, F32),
                   jax.ShapeDtypeStruct(d_g.shape, F32)),
        grid=(g_n, nc),
        in_specs=[pl.BlockSpec((ln, r_n * p_n), lambda g, c: (last_c - c, g)),
                  pl.BlockSpec((ln, n_n), lambda g, c: (last_c - c, g)),
                  pl.BlockSpec((ln, n_n), lambda g, c: (last_c - c, g)),
                  pl.BlockSpec((None, ln, LANES), lambda g, c: (g, last_c - c, 0)),
                  pl.BlockSpec((None, 1, LANES), lambda g, c: (g, 0, 0)),
                  pl.BlockSpec((None, 1, LANES), lambda g, c: (g, 0, 0)),
                  pl.BlockSpec((None, r_n, p_n, n_n), lambda g, c: (last_c - c, g, 0, 0)),
                  pl.BlockSpec((ln, r_n * p_n), lambda g, c: (last_c - c, g))],
        out_specs=(pl.BlockSpec((ln, r_n * p_n), lambda g, c: (last_c - c, g)),
                   pl.BlockSpec((ln, n_n), lambda g, c: (last_c - c, g)),
                   pl.BlockSpec((ln, n_n), lambda g, c: (last_c - c, g)),
                   pl.BlockSpec((None, ln, LANES), lambda g, c: (g, last_c - c, 0)),
                   pl.BlockSpec((None, ln, LANES), lambda g, c: (g, last_c - c, 0)),
                   pl.BlockSpec((None, 1, LANES), lambda g, c: (g, 0, 0))),
        scratch_shapes=[pltpu.VMEM((r_n, p_n, n_n), F32)],
        name=name, compiler_params=_cparams(("parallel", "arbitrary")),
    )(xs, bm, cm, dtp, a_g, d_g, states, dy)


SB_TILE = 256


def _tri_sum(x, tri):
    t = x.shape[0]
    hi = x.astype(BF16)
    r1 = x - hi.astype(F32)
    mid = r1.astype(BF16)
    lo = (r1 - mid.astype(F32)).astype(BF16)
    r = _dot(jnp.concatenate([hi, mid, lo], axis=0), tri, _NN)
    return r[:t] + r[t:2 * t] + r[2 * t:]


def _sb_logits(q, k_j, scale, strict):
    z = _dot(q, k_j, _NT) * scale
    sp = jnp.log(1.0 + jnp.exp(-jnp.abs(z)))
    log_b = jnp.minimum(z, 0.0) - sp
    log_1mb = log_b - z
    if strict is not None:
        log_1mb = jnp.where(strict, log_1mb, 0.0)
    return log_b, log_1mb


def _sb_tile(s):
    return _pick(s, (SB_TILE, LANES))


def _sb_iotas(t):
    row = lax.broadcasted_iota(jnp.int32, (t, t), 0)
    col = lax.broadcasted_iota(jnp.int32, (t, t), 1)
    return row, col


def sb_attn_fwd(qn, kn, v, *, v_off=0, name):
    s, w = qn.shape
    dh = SB_HEAD_DIM
    n_h = w // dh
    t = _sb_tile(s)
    scale = 1.0 / math.sqrt(dh)

    def body(q_ref, k_ref, v_ref, o_ref, tot_ref):
        i = pl.program_id(1)
        q = q_ref[...]
        row, col = _sb_iotas(t)
        later = (row > col).astype(BF16)

        def tile(j, acc, run, strict):
            s0 = pl.multiple_of(j * t, t)
            k_j = k_ref[pl.ds(s0, t), :]
            v_j = v_ref[pl.ds(s0, t), :].astype(BF16)
            log_b, log_1mb = _sb_logits(q, k_j, scale, strict)
            att = jnp.exp(log_b + (_tri_sum(log_1mb, later) + run))
            if strict is not None:
                att = jnp.where(strict, att, 0.0)
            acc = acc + _dot(att.astype(BF16), v_j, _NN)
            return acc, run + jnp.sum(log_1mb, axis=1, keepdims=True)

        acc, run = tile(i, jnp.zeros((t, dh), F32), jnp.zeros((t, 1), F32), col < row)
        acc, run = lax.fori_loop(0, i, lambda jj, c: tile(i - 1 - jj, c[0], c[1], None), (acc, run))
        o_ref[...] = acc
        tot_ref[...] = jnp.broadcast_to(run, (t, dh))

    return pl.pallas_call(
        body,
        out_shape=(jax.ShapeDtypeStruct((s, w), F32), jax.ShapeDtypeStruct((s, w), F32)),
        grid=(n_h, s // t),
        in_specs=[pl.BlockSpec((t, dh), lambda h, i: (i, h)),
                  pl.BlockSpec((s, dh), lambda h, i: (0, h)),
                  pl.BlockSpec((s, dh), lambda h, i: (0, v_off + h))],
        out_specs=(pl.BlockSpec((t, dh), lambda h, i: (i, h)),
                   pl.BlockSpec((t, dh), lambda h, i: (i, h))),
        name=name, compiler_params=_cparams(("parallel", "parallel")),
    )(qn, kn, v)


def sb_attn_bwd(qn, kn, v, tot, do, *, v_off=0, name):
    s, w = qn.shape
    dh = SB_HEAD_DIM
    n_h = w // dh
    t = _sb_tile(s)
    scale = 1.0 / math.sqrt(dh)

    def body(q_ref, k_ref, v_ref, tot_ref, do_ref, dq_ref, dk_ref, dv_ref):
        dk_ref[...] = jnp.zeros_like(dk_ref)
        dv_ref[...] = jnp.zeros_like(dv_ref)
        row, col = _sb_iotas(t)
        upto = (row <= col).astype(BF16)
        before = (row < col).astype(BF16)

        def q_block(i, _):
            t0 = pl.multiple_of(i * t, t)
            q = q_ref[pl.ds(t0, t), :]
            do_i = do_ref[pl.ds(t0, t), :].astype(BF16)
            total = tot_ref[pl.ds(t0, t), 0:1]

            def tile(j, dq, run_l, run_g, strict):
                s0 = pl.multiple_of(j * t, t)
                k_j = k_ref[pl.ds(s0, t), :]
                v_j = v_ref[pl.ds(s0, t), :].astype(BF16)
                log_b, log_1mb = _sb_logits(q, k_j, scale, strict)
                att = jnp.exp(log_b + ((total - run_l) - _tri_sum(log_1mb, upto)))
                if strict is not None:
                    att = jnp.where(strict, att, 0.0)
                g = att * _dot(do_i, v_j, _NT)
                c = _tri_sum(g, before) + run_g
                dz = (g - (g + c) * jnp.exp(log_b)) * scale
                if strict is not None:
                    dz = jnp.where(strict, dz, 0.0)
                dz = dz.astype(BF16)
                dq = dq + _dot(dz, k_j, _NN)
                dk_ref[pl.ds(s0, t), :] += _dot(dz, q, _TN)
                dv_ref[pl.ds(s0, t), :] += _dot(att.astype(BF16), do_i, _TN)
                return (dq, run_l + jnp.sum(log_1mb, axis=1, keepdims=True),
                        run_g + jnp.sum(g, axis=1, keepdims=True))

            zero = jnp.zeros((t, 1), F32)
            carry = lax.fori_loop(0, i, lambda j, c: tile(j, c[0], c[1], c[2], None),
                                  (jnp.zeros((t, dh), F32), zero, zero))
            dq, _, _ = tile(i, carry[0], carry[1], carry[2], col < row)
            dq_ref[pl.ds(t0, t), :] = dq
            return 0

        lax.fori_loop(0, s // t, q_block, 0)

    head = pl.BlockSpec((s, dh), lambda h: (0, h))
    return pl.pallas_call(
        body,
        out_shape=tuple(jax.ShapeDtypeStruct((s, w), F32) for _ in range(3)),
        grid=(n_h,),
        in_specs=[head, head, pl.BlockSpec((s, dh), lambda h: (0, v_off + h)), head, head],
        out_specs=(head, head, head),
        name=name, compiler_params=_cparams(("parallel",)),
    )(qn, kn, v, tot, do)


ROW_TILE = 256
WIDE_ROW_TILE = 64


def _rows(width, col=0, tm=ROW_TILE):
    return pl.BlockSpec((tm, width), lambda i: (i, col))


_wide_rows = functools.partial(_rows, tm=WIDE_ROW_TILE)


def _whole(shape):
    return pl.BlockSpec(shape, lambda i: (0,) * len(shape))


def _ew_call(body, out_shape, in_specs, out_specs, args, n_rows, name, carried=False):
    return pl.pallas_call(
        body, out_shape=out_shape, grid=(n_rows // in_specs[0].block_shape[0],), in_specs=in_specs, out_specs=out_specs,
        name=name, compiler_params=_cparams(("arbitrary",) if carried else ("parallel",)),
    )(*args)


def _first_step(*refs):
    @pl.when(pl.program_id(0) == 0)
    def _():
        for r in refs:
            r[...] = jnp.zeros_like(r)


def rmsnorm_fwd(x, w, *, name):
    s, d = x.shape

    def body(x_ref, w_ref, o_ref):
        xv = x_ref[...]
        r = lax.rsqrt(jnp.mean(xv * xv, axis=-1, keepdims=True) + NORM_EPS)
        o_ref[...] = (xv * r * w_ref[...]).astype(BF16)

    return _ew_call(body, jax.ShapeDtypeStruct((s, d), BF16), [_rows(d), _whole((1, d))], _rows(d),
                    (x, w.reshape(1, d)), s, name)


def rmsnorm_bwd(x, w, dy, dres, *, name):
    s, d = x.shape

    def body(x_ref, w_ref, dy_ref, dr_ref, dx_ref, dw_ref):
        _first_step(dw_ref)
        xv = x_ref[...]
        r = lax.rsqrt(jnp.mean(xv * xv, axis=-1, keepdims=True) + NORM_EPS)
        xhat = xv * r
        dyv = dy_ref[...].astype(F32)
        dw_ref[...] += jnp.sum(dyv * xhat, axis=0, keepdims=True)
        g = dyv * w_ref[...]
        dx_ref[...] = dr_ref[...] + r * (g - xhat * jnp.mean(g * xhat, axis=-1, keepdims=True))

    return _ew_call(body, (jax.ShapeDtypeStruct((s, d), F32), jax.ShapeDtypeStruct((1, d), F32)),
                    [_rows(d), _whole((1, d)), _rows(d), _rows(d)], (_rows(d), _whole((1, d))),
                    (x, w.reshape(1, d), dy, dres), s, name, carried=True)


def ple_fwd(h1, gate_pre, pp, *, name):
    s, d = h1.shape

    def body(h_ref, g_ref, p_ref, o_ref):
        o_ref[...] = h_ref[...] + p_ref[...] * _sigmoid(g_ref[...])

    return _ew_call(body, jax.ShapeDtypeStruct((s, d), F32), [_rows(d)] * 3, _rows(d), (h1, gate_pre, pp), s, name)


def ple_bwd(dh2, gate_pre, pp, after, *, name):
    s, d = dh2.shape

    def body(dh_ref, g_ref, p_ref, after_ref, dp_ref, dg_ref):
        gate = _sigmoid(g_ref[...])
        dh = dh_ref[...]
        dp_ref[...] = (dh * gate).astype(BF16)
        dg_ref[...] = (dh * p_ref[...] * gate * (1.0 - gate)).astype(BF16)

    shp = jax.ShapeDtypeStruct((s, d), BF16)
    return _ew_call(body, (shp, shp), [_rows(d)] * 3 + [_whole(TOKEN_SHAPE)], (_rows(d), _rows(d)),
                    (dh2, gate_pre, pp, after), s, name)


def loss_head(y, target, *, name):
    s, d = y.shape

    def body(y_ref, t_ref, l_ref, dy_ref):
        _first_step(l_ref)
        err = y_ref[...] - t_ref[...]
        per_tok = jnp.mean(err * err, axis=-1, keepdims=True)
        l_ref[...] += 0.5 * jnp.sum(per_tok, axis=0, keepdims=True)
        dy_ref[...] = err * (1.0 / d)

    return _ew_call(body, (jax.ShapeDtypeStruct((1, 1), F32), jax.ShapeDtypeStruct((s, d), F32)),
                    [_rows(d), _rows(d)], (_whole((1, 1)), _rows(d)), (y, target), s, name, carried=True)


CONV_COL_TILE = 256


def _conv_taps(x, w_ref):
    row = lax.broadcasted_iota(jnp.int32, (x.shape[0], 1), 0)
    acc = x * w_ref[SSD_D_CONV - 1:SSD_D_CONV, :]
    shifted = []
    for d in range(1, SSD_D_CONV):
        xs = jnp.where(row >= d, pltpu.roll(x, d, 0), 0.0)
        shifted.append(xs)
        acc = acc + xs * w_ref[SSD_D_CONV - 1 - d:SSD_D_CONV - d, :]
    return acc, shifted


def ssd_conv_fwd(x, w, b, *, name):
    s, c = x.shape
    tc = _pick(c, (CONV_COL_TILE, LANES))

    def body(x_ref, w_ref, b_ref, o_ref):
        pre, _ = _conv_taps(x_ref[...], w_ref)
        o_ref[...] = _silu(pre + b_ref[...])

    col = pl.BlockSpec((s, tc), lambda j: (0, j))
    return pl.pallas_call(
        body, out_shape=jax.ShapeDtypeStruct((s, c), F32), grid=(c // tc,),
        in_specs=[col, pl.BlockSpec((SSD_D_CONV, tc), lambda j: (0, j)), pl.BlockSpec((1, tc), lambda j: (0, j))],
        out_specs=col, name=name, compiler_params=_cparams(("parallel",)),
    )(x, w, b)


def ssd_conv_bwd(x, w, b, dact, *, name):
    s, c = x.shape
    tc = _pick(c, (CONV_COL_TILE, LANES))

    def body(x_ref, w_ref, b_ref, da_ref, dx_ref, dw_ref, db_ref):
        xv = x_ref[...]
        pre, shifted = _conv_taps(xv, w_ref)
        dpre = da_ref[...] * _silu_grad(pre + b_ref[...])
        db_ref[...] = jnp.sum(dpre, axis=0, keepdims=True)
        row = lax.broadcasted_iota(jnp.int32, (s, 1), 0)
        dx = dpre * w_ref[SSD_D_CONV - 1:SSD_D_CONV, :]
        dw_ref[SSD_D_CONV - 1:SSD_D_CONV, :] = jnp.sum(dpre * xv, axis=0, keepdims=True)
        for d in range(1, SSD_D_CONV):
            k = SSD_D_CONV - 1 - d
            dw_ref[k:k + 1, :] = jnp.sum(dpre * shifted[d - 1], axis=0, keepdims=True)
            up = jnp.where(row < s - d, pltpu.roll(dpre, s - d, 0), 0.0)
            dx = dx + up * w_ref[k:k + 1, :]
        dx_ref[...] = dx.astype(BF16)

    col = pl.BlockSpec((s, tc), lambda j: (0, j))
    wspec = pl.BlockSpec((SSD_D_CONV, tc), lambda j: (0, j))
    bspec = pl.BlockSpec((1, tc), lambda j: (0, j))
    return pl.pallas_call(
        body,
        out_shape=(jax.ShapeDtypeStruct((s, c), BF16), jax.ShapeDtypeStruct((SSD_D_CONV, c), F32),
                   jax.ShapeDtypeStruct((1, c), F32)),
        grid=(c // tc,), in_specs=[col, wspec, bspec, col], out_specs=(col, wspec, bspec),
        name=name, compiler_params=_cparams(("parallel",)),
    )(x, w, b, dact)


def ssd_dt_fwd(dt_raw, bias, a_log, *, name):
    s, h = dt_raw.shape

    def body(r_ref, b_ref, al_ref, dt_ref, a_ref):
        zv = r_ref[...] + b_ref[...]
        dt_ref[...] = jnp.maximum(zv, 0.0) + jnp.log(1.0 + jnp.exp(-jnp.abs(zv)))
        a_ref[...] = -jnp.exp(al_ref[...])

    full = pl.BlockSpec((s, h), lambda: (0, 0))
    vec = pl.BlockSpec((1, h), lambda: (0, 0))
    return pl.pallas_call(
        body, out_shape=(jax.ShapeDtypeStruct((s, h), F32), jax.ShapeDtypeStruct((1, h), F32)),
        in_specs=[full, vec, vec], out_specs=(full, vec), name=name, compiler_params=_cparams(),
    )(dt_raw, bias.reshape(1, h), a_log.reshape(1, h))


def ssd_dt_bwd(dt_raw, bias, a_log, dt, ddt, dadt, *, name):
    s, h = dt_raw.shape

    def body(r_ref, b_ref, al_ref, dt_ref, ddt_ref, dadt_ref, dr_ref, db_ref, dal_ref):
        a = -jnp.exp(al_ref[...])
        dadt_v = dadt_ref[...]
        d_dt = ddt_ref[...] + a * dadt_v
        d_raw = d_dt * _sigmoid(r_ref[...] + b_ref[...])
        dr_ref[...] = d_raw
        db_ref[...] = jnp.sum(d_raw, axis=0, keepdims=True)
        dal_ref[...] = jnp.sum(dadt_v * dt_ref[...], axis=0, keepdims=True) * a

    full = pl.BlockSpec((s, h), lambda: (0, 0))
    vec = pl.BlockSpec((1, h), lambda: (0, 0))
    return pl.pallas_call(
        body, out_shape=(jax.ShapeDtypeStruct((s, h), F32), jax.ShapeDtypeStruct((1, h), F32),
                         jax.ShapeDtypeStruct((1, h), F32)),
        in_specs=[full, vec, vec, full, full, full], out_specs=(full, vec, vec), name=name,
        compiler_params=_cparams(),
    )(dt_raw, bias.reshape(1, h), a_log.reshape(1, h), dt, ddt, dadt)


def _group_mean(v, n_groups):
    gw = v.shape[-1] // n_groups
    parts = [jnp.broadcast_to(jnp.mean(v[:, k * gw:(k + 1) * gw], axis=-1, keepdims=True), (v.shape[0], gw))
             for k in range(n_groups)]
    return jnp.concatenate(parts, axis=-1)


def ssd_gate_fwd(y, z, gw, *, name):
    s, di = y.shape

    def body(y_ref, z_ref, w_ref, o_ref):
        yg = y_ref[...] * _silu(z_ref[...])
        r = lax.rsqrt(_group_mean(yg * yg, SSD_N_GROUPS) + GATED_NORM_EPS)
        o_ref[...] = (yg * r * w_ref[...]).astype(BF16)

    return _ew_call(body, jax.ShapeDtypeStruct((s, di), BF16), [_wide_rows(di), _wide_rows(di), _whole((1, di))],
                    _wide_rows(di), (y, z, gw.reshape(1, di)), s, name)


def ssd_gate_bwd(y, z, gw, dyn, *, name):
    s, di = y.shape

    def body(y_ref, z_ref, w_ref, dn_ref, dy_ref, dz_ref, dw_ref):
        _first_step(dw_ref)
        yv, zv = y_ref[...], z_ref[...]
        sz = _silu(zv)
        yg = yv * sz
        r = lax.rsqrt(_group_mean(yg * yg, SSD_N_GROUPS) + GATED_NORM_EPS)
        yhat = yg * r
        dn = dn_ref[...]
        dw_ref[...] += jnp.sum(dn * yhat, axis=0, keepdims=True)
        g = dn * w_ref[...]
        dyg = r * (g - yhat * _group_mean(g * yhat, SSD_N_GROUPS))
        dy_ref[...] = dyg * sz
        dz_ref[...] = (dyg * yv * _silu_grad(zv)).astype(BF16)

    return _ew_call(body, (jax.ShapeDtypeStruct((s, di), F32), jax.ShapeDtypeStruct((s, di), BF16),
                           jax.ShapeDtypeStruct((1, di), F32)),
                    [_wide_rows(di), _wide_rows(di), _whole((1, di)), _wide_rows(di)],
                    (_wide_rows(di), _wide_rows(di), _whole((1, di))),
                    (y, z, gw.reshape(1, di), dyn), s, name, carried=True)


def _head_mean(v):
    return _group_mean(v, v.shape[-1] // SB_HEAD_DIM)


def sb_qk_fwd(proj, qw, kw, *, name):
    s, w4 = proj.shape
    w = w4 // 4
    reps = w // SB_HEAD_DIM

    def body(q_ref, k_ref, qw_ref, kw_ref, qn_ref, kn_ref):
        for x_ref, w_ref, o_ref in ((q_ref, qw_ref, qn_ref), (k_ref, kw_ref, kn_ref)):
            xv = x_ref[...]
            r = lax.rsqrt(_head_mean(xv * xv) + NORM_EPS)
            o_ref[...] = (xv * r * jnp.tile(w_ref[...], (1, reps))).astype(BF16)

    shp = jax.ShapeDtypeStruct((s, w), BF16)
    return _ew_call(body, (shp, shp), [_rows(w, 0), _rows(w, 1), _whole((1, SB_HEAD_DIM)), _whole((1, SB_HEAD_DIM))],
                    (_rows(w), _rows(w)), (proj, proj, qw.reshape(1, -1), kw.reshape(1, -1)), s, name)


def sb_gate_fwd(o, proj, *, name):
    s, w = o.shape

    def body(o_ref, g_ref, og_ref):
        og_ref[...] = (o_ref[...] * _silu(g_ref[...])).astype(BF16)

    return _ew_call(body, jax.ShapeDtypeStruct((s, w), BF16), [_rows(w), _rows(w, 3)], _rows(w), (o, proj), s, name)


def sb_gate_bwd(dog, o, proj, *, name):
    s, w = o.shape

    def body(d_ref, o_ref, g_ref, do_ref, dg_ref):
        gv, dv = g_ref[...], d_ref[...]
        do_ref[...] = dv * _silu(gv)
        dg_ref[...] = (dv * o_ref[...] * _silu_grad(gv)).astype(BF16)

    return _ew_call(body, (jax.ShapeDtypeStruct((s, w), F32), jax.ShapeDtypeStruct((s, w), BF16)),
                    [_rows(w), _rows(w), _rows(w, 3)], (_rows(w), _rows(w)), (dog, o, proj), s, name)


def sb_pack_bwd(proj, qw, kw, dqn, dkn, dv, dg, *, name):
    s, w4 = proj.shape
    w = w4 // 4
    reps = w // SB_HEAD_DIM

    def body(q_ref, k_ref, qw_ref, kw_ref, dqn_ref, dkn_ref, dv_ref, dg_ref, dp_ref, dqw_ref, dkw_ref):
        _first_step(dqw_ref, dkw_ref)
        for idx, (x_ref, w_ref, d_ref, dw_ref) in enumerate(((q_ref, qw_ref, dqn_ref, dqw_ref),
                                                           (k_ref, kw_ref, dkn_ref, dkw_ref))):
            xv = x_ref[...]
            r = lax.rsqrt(_head_mean(xv * xv) + NORM_EPS)
            xhat = xv * r
            dn = d_ref[...]
            per_col = jnp.sum(dn * xhat, axis=0, keepdims=True)
            acc = per_col[:, 0:SB_HEAD_DIM]
            for hh in range(1, reps):
                acc = acc + per_col[:, hh * SB_HEAD_DIM:(hh + 1) * SB_HEAD_DIM]
            dw_ref[...] += acc
            g = dn * jnp.tile(w_ref[...], (1, reps))
            dp_ref[:, idx * w:(idx + 1) * w] = (r * (g - xhat * _head_mean(g * xhat))).astype(BF16)
        dp_ref[:, 2 * w:3 * w] = dv_ref[...].astype(BF16)
        dp_ref[:, 3 * w:4 * w] = dg_ref[...]

    vec = _whole((1, SB_HEAD_DIM))
    return _ew_call(body, (jax.ShapeDtypeStruct((s, w4), BF16), jax.ShapeDtypeStruct((1, SB_HEAD_DIM), F32),
                           jax.ShapeDtypeStruct((1, SB_HEAD_DIM), F32)),
                    [_wide_rows(w, 0), _wide_rows(w, 1), vec, vec, _wide_rows(w), _wide_rows(w), _wide_rows(w),
                     _wide_rows(w)],
                    (_wide_rows(w4), vec, vec),
                    (proj, proj, qw.reshape(1, -1), kw.reshape(1, -1), dqn, dkn, dv, dg), s, name, carried=True)


_HBM = pl.BlockSpec(memory_space=pltpu.HBM)


def _mesh_pos():
    return lax.axis_index("x"), lax.axis_index("y"), lax.axis_index("c")


def _other_chips(x, y):
    return [(1 - x, y), (x, 1 - y), (1 - x, 1 - y)]


_SEM = pl.BlockSpec(memory_space=pltpu.SEMAPHORE)
_ANY = pl.BlockSpec(memory_space=pl.ANY)
_DATAFLOW = pltpu.SideEffectType.DATAFLOW_SIDE_EFFECTING
N_PEER_CHIPS = N_CHIP - 1
TOKEN_SHAPE = (8, LANES)


def _in_hbm(t):
    return pltpu.with_memory_space_constraint(t, pltpu.HBM)


def _ici_copies(kind, src_refs, land_refs, send_sems, recv_sems, arrivals=False):
    x, y, c = _mesh_pos()
    out = []
    for a in range(len(src_refs)):
        for j, chip in enumerate(_other_chips(x, y)):
            if kind == "gather":
                src = src_refs[a]
                dst = land_refs[a].at[4 * chip[0] + 2 * chip[1] + c if arrivals else 4 * x + 2 * y + c]
            else:
                src, dst = src_refs[a].at[2 * chip[0] + chip[1]], land_refs[a].at[j]
            k = a * N_PEER_CHIPS + j
            out.append(pltpu.make_async_remote_copy(
                src_ref=src, dst_ref=dst, send_sem=send_sems.at[k], recv_sem=recv_sems.at[k],
                device_id=(*chip, c), device_id_type=MESH))
    return out


def ici_start(kind, srcs, lands, after=(), *, name):
    n = len(srcs)

    def body(*refs):
        first_out = 2 * n + len(after)
        for cp in _ici_copies(kind, refs[:n], refs[n:2 * n], refs[first_out], refs[first_out + 1]):
            cp.start()
        refs[-1][...] = jnp.zeros(TOKEN_SHAPE, F32)

    outs = pl.pallas_call(
        body, name=name,
        out_shape=(pltpu.SemaphoreType.DMA((n * N_PEER_CHIPS,)), pltpu.SemaphoreType.DMA((n * N_PEER_CHIPS,)),
                   *[pltpu.HBM(t.shape, t.dtype) for t in srcs], *[pltpu.HBM(t.shape, t.dtype) for t in lands],
                   jax.ShapeDtypeStruct(TOKEN_SHAPE, F32)),
        in_specs=[_HBM] * (2 * n) + [_ANY] * len(after),
        out_specs=(_SEM, _SEM, *([_HBM] * (2 * n)), pl.BlockSpec(memory_space=pltpu.VMEM)),
        input_output_aliases={k: 2 + k for k in range(2 * n)},
        compiler_params=pltpu.CompilerParams(has_side_effects=_DATAFLOW),
    )(*[_in_hbm(t) for t in srcs], *[_in_hbm(t) for t in lands], *after)
    return outs[0], outs[1], list(outs[2:2 + n]), list(outs[2 + n:2 + 2 * n]), outs[-1]


def ici_wait(kind, started, after, *, name):
    send_sems, recv_sems, srcs, lands, _ = started
    n = len(srcs)

    def body(*refs):
        for cp in _ici_copies(kind, refs[:n], refs[n:2 * n], refs[2 * n], refs[2 * n + 1]):
            cp.wait_send()
        for cp in _ici_copies(kind, refs[:n], refs[n:2 * n], refs[2 * n], refs[2 * n + 1], arrivals=True):
            cp.wait_recv()

    outs = pl.pallas_call(
        body, name=name,
        out_shape=tuple(pltpu.HBM(t.shape, t.dtype) for t in (*srcs, *lands)),
        in_specs=[_HBM] * (2 * n) + [_SEM, _SEM] + [_ANY] * len(after),
        out_specs=tuple([_HBM] * (2 * n)),
        input_output_aliases={k: k for k in range(2 * n)},
        compiler_params=pltpu.CompilerParams(has_side_effects=_DATAFLOW),
    )(*srcs, *lands, send_sems, recv_sems, *after)
    return list(outs[:n]), list(outs[n:])


def gather_finish(shards, lands, *, name):
    n = len(shards)

    def body(*refs):
        x_refs, in_refs, out_refs = refs[:n], refs[n:2 * n], refs[2 * n:3 * n]
        send_sems, recv_sems, local_sems = refs[3 * n:]
        x, y, c = _mesh_pos()
        mine = [(x, y)] + _other_chips(x, y)
        local = [pltpu.make_async_copy(x_refs[a], out_refs[a].at[4 * x + 2 * y + c], local_sems.at[a])
                 for a in range(n)]
        for cp in local:
            cp.start()
        sends, arrivals = [], []
        for a in range(n):
            for k, chip in enumerate(mine):
                blk = 4 * chip[0] + 2 * chip[1] + c
                sems = dict(send_sem=send_sems.at[a, k], recv_sem=recv_sems.at[a, k], device_id=(x, y, 1 - c),
                            device_id_type=MESH)
                src = x_refs[a] if k == 0 else in_refs[a].at[blk]
                sends.append(pltpu.make_async_remote_copy(src_ref=src, dst_ref=out_refs[a].at[blk], **sems))
                arrivals.append(pltpu.make_async_remote_copy(
                    src_ref=src, dst_ref=out_refs[a].at[4 * chip[0] + 2 * chip[1] + (1 - c)], **sems))
        for cp in sends:
            cp.start()
        for cp in sends:
            cp.wait_send()
        for cp in arrivals:
            cp.wait_recv()
        for cp in local:
            cp.wait()

    return pl.pallas_call(
        body, out_shape=tuple(jax.ShapeDtypeStruct(t.shape, t.dtype) for t in lands),
        in_specs=[_HBM] * (2 * n), out_specs=tuple([_HBM] * n),
        input_output_aliases={n + a: a for a in range(n)},
        scratch_shapes=[pltpu.SemaphoreType.DMA((n, N_CHIP)), pltpu.SemaphoreType.DMA((n, N_CHIP)),
                        pltpu.SemaphoreType.DMA((n,))],
        name=name,
    )(*shards, *lands)


def sibling_exchange(grads, *, name):
    n = len(grads)

    def body(*refs):
        g_refs, r_refs = refs[:n], refs[n:2 * n]
        send_sems, recv_sems = refs[2 * n:]
        x, y, c = _mesh_pos()
        copies = [pltpu.make_async_remote_copy(
            src_ref=g_refs[a].at[:, 1 - c], dst_ref=r_refs[a], send_sem=send_sems.at[a],
            recv_sem=recv_sems.at[a], device_id=(x, y, 1 - c), device_id_type=MESH) for a in range(n)]
        for cp in copies:
            cp.start()
        for cp in copies:
            cp.wait()

    return pl.pallas_call(
        body, out_shape=tuple(jax.ShapeDtypeStruct((N_CHIP,) + g.shape[2:], g.dtype) for g in grads),
        in_specs=[_HBM] * n, out_specs=tuple([_HBM] * n),
        scratch_shapes=[pltpu.SemaphoreType.DMA((n,)), pltpu.SemaphoreType.DMA((n,))],
        name=name,
    )(*grads)


def all_reduce_small(v, *, name):
    r = v.shape[0]

    def body(v_ref, o_ref, buf, send_sems, recv_sems):
        x, y, c = _mesh_pos()
        me = 4 * x + 2 * y + c
        buf[me] = v_ref[...]
        copies = []
        for k in range(1, N_DEV):
            to = ((x + (k >> 2)) % 2, (y + ((k >> 1) & 1)) % 2, (c + (k & 1)) % 2)
            copies.append(pltpu.make_async_remote_copy(
                src_ref=v_ref, dst_ref=buf.at[me], send_sem=send_sems.at[k - 1], recv_sem=recv_sems.at[k - 1],
                device_id=to, device_id_type=MESH))
        for cp in copies:
            cp.start()
        for cp in copies:
            cp.wait()
        acc = buf[0]
        for d in range(1, N_DEV):
            acc = acc + buf[d]
        o_ref[...] = acc

    vm = pl.BlockSpec(memory_space=pltpu.VMEM)
    return pl.pallas_call(
        body, out_shape=jax.ShapeDtypeStruct(v.shape, F32), in_specs=[vm], out_specs=vm,
        scratch_shapes=[pltpu.VMEM((N_DEV, r, LANES), F32), pltpu.SemaphoreType.DMA((N_DEV - 1,)),
                        pltpu.SemaphoreType.DMA((N_DEV - 1,))],
        name=name,
    )(v)


def pair_add(g, r1, core, *, name):
    _, _, rows, cols = g.shape
    tm = _pick(rows, (256, 128))

    def body(c_ref, g_ref, r_ref, o_ref):
        o_ref[...] = (g_ref[...].astype(F32) + r_ref[...].astype(F32)).astype(o_ref.dtype)

    return pl.pallas_call(
        body, out_shape=jax.ShapeDtypeStruct(r1.shape, g.dtype),
        grid_spec=pltpu.PrefetchScalarGridSpec(
            num_scalar_prefetch=1, grid=(N_CHIP, rows // tm),
            in_specs=[pl.BlockSpec((None, None, tm, cols), lambda k, i, c_ref: (k, c_ref[0], i, 0)),
                      pl.BlockSpec((None, tm, cols), lambda k, i, c_ref: (k, i, 0))],
            out_specs=pl.BlockSpec((None, tm, cols), lambda k, i, c_ref: (k, i, 0))),
        name=name, compiler_params=_cparams(("parallel", "parallel")),
    )(core, g, r1)


def _adamw_math(w, g, m, v):
    m = ADAM_B1 * m + (1.0 - ADAM_B1) * g
    v = ADAM_B2 * v + (1.0 - ADAM_B2) * (g * g)
    m_hat = m / (1.0 - ADAM_B1 ** ADAM_STEP)
    v_hat = v / (1.0 - ADAM_B2 ** ADAM_STEP)
    delta = -ADAM_LR * (m_hat / (jnp.sqrt(v_hat) + ADAM_EPS) + ADAM_WD * w)
    return delta, m, v


def adamw_sharded(w, m, v, chip_sums, received, chip, *, name):
    rows, cols = w.shape
    tm = _pick(rows, (256, 128))

    def body(k_ref, w_ref, m_ref, v_ref, t_ref, r_ref, g_ref, d_ref, nm_ref, nv_ref):
        g = t_ref[...].astype(F32)
        for j in range(N_CHIP - 1):
            g = g + r_ref[j].astype(F32)
        d, mm, vv = _adamw_math(w_ref[...], g, m_ref[...], v_ref[...])
        g_ref[...] = g
        d_ref[...] = d
        nm_ref[...] = mm
        nv_ref[...] = vv

    blk = pl.BlockSpec((tm, cols), lambda i, k_ref: (i, 0))
    shp = jax.ShapeDtypeStruct((rows, cols), F32)
    return pl.pallas_call(
        body, out_shape=(shp, shp, shp, shp),
        grid_spec=pltpu.PrefetchScalarGridSpec(
            num_scalar_prefetch=1, grid=(rows // tm,),
            in_specs=[blk, blk, blk,
                      pl.BlockSpec((None, tm, cols), lambda i, k_ref: (k_ref[0], i, 0)),
                      pl.BlockSpec((N_CHIP - 1, tm, cols), lambda i, k_ref: (0, i, 0))],
            out_specs=(blk, blk, blk, blk)),
        name=name, compiler_params=_cparams(("parallel",)),
    )(chip, w, m, v, chip_sums, received)


def adamw_replicated(w, m, v, g, *, name):
    def body(w_ref, m_ref, v_ref, g_ref, d_ref, nm_ref, nv_ref):
        d, mm, vv = _adamw_math(w_ref[...], g_ref[...], m_ref[...], v_ref[...])
        d_ref[...] = d
        nm_ref[...] = mm
        nv_ref[...] = vv

    shp = jax.ShapeDtypeStruct(w.shape, F32)
    return pl.pallas_call(body, out_shape=(shp, shp, shp), name=name, compiler_params=_cparams())(w, m, v, g)


WEIGHT_NAMES = ("norm_w", "ssd_in_w", "ssd_conv_w", "ssd_conv_b", "ssd_dt_bias", "ssd_a_log", "ssd_d",
                "ssd_gnorm_w", "ssd_out_w", "sb_in_w", "sb_qn_w", "sb_kn_w", "sb_out_w", "ple_norm_w",
                "ple_gate_w", "ple_proj_w")
REPLICATED = ("norm_w", "ssd_conv_b", "ssd_dt_bias", "ssd_a_log", "ssd_d", "ssd_gnorm_w", "sb_qn_w", "sb_kn_w",
              "ple_norm_w")
PACK_ROWS = 8


def _pack(parts):
    flat = jnp.concatenate([t.reshape(-1) for t in parts])
    pad = (-flat.shape[0]) % (PACK_ROWS * LANES)
    return jnp.pad(flat, (0, pad)).reshape(-1, LANES)


def _unpack(packed, like):
    flat = packed.reshape(-1)
    out, off = [], 0
    for t in like:
        out.append(flat[off:off + t.size].reshape(t.shape))
        off += t.size
    return out


def _to_group_lanes(v, r):
    t = v.reshape(v.shape[0], SSD_N_GROUPS, r).transpose(1, 0, 2)
    return jnp.pad(t, ((0, 0), (0, 0), (0, LANES - r)))


def _from_group_lanes(t, r):
    return t[:, :, :r].transpose(1, 0, 2).reshape(t.shape[1], SSD_N_GROUPS * r)


def _head_vec(v, r):
    return jnp.pad(v.reshape(SSD_N_GROUPS, 1, r), ((0, 0), (0, 0), (0, LANES - r)))


def _col_blocks(full):
    rows = full.shape[0]
    return full.reshape(rows, N_DEV, -1).transpose(1, 0, 2)


def _from_col_blocks(blocks):
    return blocks.transpose(1, 0, 2).reshape(blocks.shape[1], -1)


def _split_cols(full, widths):
    out, off = [], 0
    for w in widths:
        out.append(full[:, off:off + w])
        off += w
    return out


def kernel(x, p, norm_w, ssd_in_w, ssd_conv_w, ssd_conv_b, ssd_dt_bias, ssd_a_log, ssd_d, ssd_gnorm_w, ssd_out_w, sb_in_w, sb_qn_w, sb_kn_w, sb_out_w, ple_norm_w, ple_gate_w, ple_proj_w, loss_target, m_norm_w, m_ssd_in_w, m_ssd_conv_w, m_ssd_conv_b, m_ssd_dt_bias, m_ssd_a_log, m_ssd_d, m_ssd_gnorm_w, m_ssd_out_w, m_sb_in_w, m_sb_qn_w, m_sb_kn_w, m_sb_out_w, m_ple_norm_w, m_ple_gate_w, m_ple_proj_w, v_norm_w, v_ssd_in_w, v_ssd_conv_w, v_ssd_conv_b, v_ssd_dt_bias, v_ssd_a_log, v_ssd_d, v_ssd_gnorm_w, v_ssd_out_w, v_sb_in_w, v_sb_qn_w, v_sb_kn_w, v_sb_out_w, v_ple_norm_w, v_ple_gate_w, v_ple_proj_w):
    env = dict(locals())
    wts = {n: env[n] for n in WEIGHT_NAMES}
    mom1 = {n: env["m_" + n] for n in WEIGHT_NAMES}
    mom2 = {n: env["v_" + n] for n in WEIGHT_NAMES}

    s, d = x.shape[1], x.shape[2]
    depth = norm_w.shape[0]
    n_ssd, n_sb = ssd_in_w.shape[0], sb_in_w.shape[0]
    di = ssd_out_w.shape[1] * N_DEV
    n_heads = ssd_dt_bias.shape[1]
    hpg = n_heads // SSD_N_GROUPS
    nbc = SSD_N_GROUPS * SSD_D_STATE
    in_segs = (di, di, nbc, nbc, n_heads)
    conv_segs = (di, nbc, nbc)
    sb_w = sb_out_w.shape[1] * N_DEV
    xi, yi, ci = _mesh_pos()
    core = ci.astype(jnp.int32).reshape(1)
    chip = (2 * xi + yi).astype(jnp.int32).reshape(1)

    def layer_keys(i):
        j = i // 2
        mixer = [("ssd_in_w", j), ("ssd_conv_w", j), ("ssd_out_w", j)] if i % 2 == 0 else [("sb_in_w", j), ("sb_out_w", j)]
        return mixer + [("ple_gate_w", i), ("ple_proj_w", i)]

    def shard_of(key):
        t = wts[key[0]][key[1]]
        return t if key[0] == "ssd_conv_w" else t.astype(BF16)

    gathers = []
    for i in range(depth):
        shards = [shard_of(k) for k in layer_keys(i)]
        gathers.append(ici_start("gather", shards, [lax.empty((N_DEV,) + t.shape, t.dtype) for t in shards],
                                 after=[g[4] for g in gathers[-1:]], name=f"ag{i}_start"))
    all_started = gathers[-1][4]
    ssd_full, sb_full, ple_full = {}, {}, {}

    def gather_layer(i, after):
        shards, lands = ici_wait("gather", gathers[i], [after], name=f"ag{i}_wait")
        full = dict(zip(layer_keys(i), gather_finish(shards, lands, name=f"ag{i}_finish")))
        j = i // 2
        if i % 2 == 0:
            ssd_full[j] = dict(
                w_in=_split_cols(_from_col_blocks(full["ssd_in_w", j]), in_segs),
                conv_w=_split_cols(_from_col_blocks(full["ssd_conv_w", j]), conv_segs),
                conv_b=_split_cols(ssd_conv_b[j].reshape(1, -1), conv_segs),
                w_out=full["ssd_out_w", j].reshape(di, d))
        else:
            sb_full[j] = dict(w_in=full["sb_in_w", j], w_out=full["sb_out_w", j].reshape(sb_w, d))
        ple_full[i] = dict(w_gate=full["ple_gate_w", i].reshape(d, d), w_proj=full["ple_proj_w", i])

    h = x.reshape(s, d)
    saved = []
    for i in range(depth):
        j = i // 2
        gather_layer(i, all_started if i == 0 else h)
        sv = dict(h_in=h)
        u = rmsnorm_fwd(h, norm_w[i], name=f"l{i}_norm")
        sv["u"] = u
        if i % 2 == 0:
            fw = ssd_full[j]
            raw = [matmul(u, wseg, name=f"l{i}_in{q}") for q, wseg in enumerate(fw["w_in"])]
            z, dt_raw = raw[0], raw[4]
            act = [ssd_conv_fwd(raw[1 + q], fw["conv_w"][q], fw["conv_b"][q], name=f"l{i}_conv{q}") for q in range(3)]
            dt, a_neg = ssd_dt_fwd(dt_raw, ssd_dt_bias[j], ssd_a_log[j], name=f"l{i}_dt")
            dtp = _to_group_lanes(dt, hpg)
            a_g = _head_vec(a_neg.reshape(-1), hpg)
            d_g = _head_vec(ssd_d[j], hpg)
            y, states = ssd_scan_fwd(act[0], act[1], act[2], dtp, a_g, d_g, heads_per_group=hpg, name=f"l{i}_scan")
            yn = ssd_gate_fwd(y, z, ssd_gnorm_w[j], name=f"l{i}_gate")
            h1 = matmul(yn, fw["w_out"], res=h, name=f"l{i}_out")
            sv.update(raw=raw, act=act, dt=dt, dtp=dtp, a_g=a_g, d_g=d_g, y=y, states=states, yn=yn)
        else:
            fw = sb_full[j]
            proj = matmul(u, fw["w_in"], name=f"l{i}_in")
            qn, kn = sb_qk_fwd(proj, sb_qn_w[j], sb_kn_w[j], name=f"l{i}_qknorm")
            v_off = 2 * sb_w // SB_HEAD_DIM
            o, tot = sb_attn_fwd(qn, kn, proj, v_off=v_off, name=f"l{i}_attn")
            og = sb_gate_fwd(o, proj, name=f"l{i}_gate")
            h1 = matmul(og, fw["w_out"], res=h, name=f"l{i}_out")
            sv.update(proj=proj, qn=qn, kn=kn, o=o, tot=tot, og=og, v_off=v_off)
        t = rmsnorm_fwd(h1, ple_norm_w[i], name=f"l{i}_plenorm")
        gate_pre = matmul(t, ple_full[i]["w_gate"], name=f"l{i}_plegate")
        pp = matmul(p[i, 0], ple_full[i]["w_proj"], name=f"l{i}_pleproj")
        h = ple_fwd(h1, gate_pre, pp, name=f"l{i}_ple")
        sv.update(h1=h1, t=t, gate_pre=gate_pre, pp=pp)
        saved.append(sv)

    loss_part, dh = loss_head(h, loss_target.reshape(s, d), name="loss_head")
    loss = lax.psum(loss_part[0, 0], ("x", "y", "c"))

    big = {}
    small = {n: [None] * wts[n].shape[0] for n in REPLICATED}
    scatters = {}
    order_after = jnp.zeros(TOKEN_SHAPE, F32)
    for i in reversed(range(depth)):
        j = i // 2
        sv = saved[i]
        dpp, dgp = ple_bwd(dh, sv["gate_pre"], sv["pp"], order_after, name=f"b{i}_ple")
        big["ple_proj_w", i] = matmul(p[i, 0], dpp, mode="tn", out_dtype=BF16, out_blocks=ple_proj_w.shape[2],
                                      name=f"b{i}_pleproj_w")
        big["ple_gate_w", i] = matmul(sv["t"], dgp, mode="tn", out_dtype=BF16, name=f"b{i}_plegate_w").reshape(N_DEV, -1, d)
        dt_ = matmul(dgp, ple_full[i]["w_gate"], mode="nt", name=f"b{i}_plegate_x")
        dh1, g_pn = rmsnorm_bwd(sv["h1"], ple_norm_w[i], dt_, dh, name=f"b{i}_plenorm")
        small["ple_norm_w"][i] = g_pn
        u = sv["u"]
        if i % 2 == 0:
            fw = ssd_full[j]
            raw, act = sv["raw"], sv["act"]
            big["ssd_out_w", j] = matmul(sv["yn"], dh1, mode="tn", out_dtype=BF16, name=f"b{i}_out_w").reshape(N_DEV, -1, d)
            dyn = matmul(dh1, fw["w_out"], mode="nt", name=f"b{i}_out_x")
            dy, dz, g_gn = ssd_gate_bwd(sv["y"], raw[0], ssd_gnorm_w[j], dyn, name=f"b{i}_gate")
            dxs, dbm, dcm, ddtp, dadtp, dd_g = ssd_scan_bwd(act[0], act[1], act[2], sv["dtp"], sv["a_g"], sv["d_g"],
                                                          sv["states"], dy, heads_per_group=hpg, name=f"b{i}_scan")
            ddt_raw, g_dtb, g_alog = ssd_dt_bwd(raw[4], ssd_dt_bias[j], ssd_a_log[j], sv["dt"],
                                                _from_group_lanes(ddtp, hpg), _from_group_lanes(dadtp, hpg),
                                                name=f"b{i}_dt")
            conv_back = [ssd_conv_bwd(raw[1 + q], fw["conv_w"][q], fw["conv_b"][q], dact, name=f"b{i}_conv{q}")
                         for q, dact in enumerate((dxs, dbm, dcm))]
            dsegs = [dz] + [cb[0] for cb in conv_back] + [ddt_raw]
            g_in = jnp.concatenate([matmul(u, ds, mode="tn", out_dtype=BF16, name=f"b{i}_in{q}_w")
                                    for q, ds in enumerate(dsegs)], axis=1)
            big["ssd_in_w", j] = _col_blocks(g_in)
            big["ssd_conv_w", j] = _col_blocks(jnp.concatenate([cb[1] for cb in conv_back], axis=1))
            du = None
            for q, (ds, wseg) in enumerate(zip(dsegs, fw["w_in"])):
                du = matmul(ds, wseg, mode="nt", res=du, name=f"b{i}_in{q}_x")
            small["ssd_conv_b"][j] = jnp.concatenate([cb[2] for cb in conv_back], axis=1)
            small["ssd_dt_bias"][j] = g_dtb
            small["ssd_a_log"][j] = g_alog
            small["ssd_d"][j] = dd_g[:, 0, :hpg]
            small["ssd_gnorm_w"][j] = g_gn
        else:
            fw = sb_full[j]
            proj = sv["proj"]
            big["sb_out_w", j] = matmul(sv["og"], dh1, mode="tn", out_dtype=BF16, name=f"b{i}_out_w").reshape(N_DEV, -1, d)
            dog = matmul(dh1, fw["w_out"], mode="nt", name=f"b{i}_out_x")
            do, dg = sb_gate_bwd(dog, sv["o"], proj, name=f"b{i}_gate")
            dqn, dkn, dv = sb_attn_bwd(sv["qn"], sv["kn"], proj, sv["tot"], do, v_off=sv["v_off"], name=f"b{i}_attn")
            dproj, g_qn, g_kn = sb_pack_bwd(proj, sb_qn_w[j], sb_kn_w[j], dqn, dkn, dv, dg, name=f"b{i}_qknorm")
            big["sb_in_w", j] = matmul(u, dproj, mode="tn", out_dtype=BF16, out_blocks=sb_in_w.shape[2], name=f"b{i}_in_w")
            du = matmul(dproj, fw["w_in"], mode="nt", name=f"b{i}_in_x")
            small["sb_qn_w"][j] = g_qn
            small["sb_kn_w"][j] = g_kn
        dh, g_n = rmsnorm_bwd(sv["h_in"], norm_w[i], du, dh1, name=f"b{i}_norm")
        small["norm_w"][i] = g_n
        blocks = [big[k].reshape(N_CHIP, 2, *big[k].shape[1:]) for k in layer_keys(i)]
        from_sibling = sibling_exchange(blocks, name=f"rs{i}_sibling")
        sums = [pair_add(g, r1, core, name=f"rs{i}_pair_add{q}") for q, (g, r1) in enumerate(zip(blocks, from_sibling))]
        scatters[i] = ici_start("scatter", sums, [lax.empty((N_PEER_CHIPS,) + t.shape[1:], t.dtype) for t in sums],
                                name=f"rs{i}_start")
        order_after = scatters[i][4]
    grad_x = dh.reshape(x.shape)

    rep_like = [wts[n] for n in REPLICATED]
    g_packed = all_reduce_small(_pack([jnp.stack([t.reshape(-1) for t in small[n]]) for n in REPLICATED]),
                                name="all_reduce_small_grads")
    d_packed, m_packed, v_packed = adamw_replicated(
        _pack(rep_like), _pack([mom1[n] for n in REPLICATED]), _pack([mom2[n] for n in REPLICATED]), g_packed,
        name="adamw_replicated")
    grads = dict(zip(REPLICATED, _unpack(g_packed, rep_like)))
    deltas = dict(zip(REPLICATED, _unpack(d_packed, rep_like)))
    new_m = dict(zip(REPLICATED, _unpack(m_packed, rep_like)))
    new_v = dict(zip(REPLICATED, _unpack(v_packed, rep_like)))

    per_layer = {}
    after = [order_after]
    for i in reversed(range(depth)):
        sums, received = ici_wait("scatter", scatters[i], after, name=f"rs{i}_wait")
        for k, t_sum, recv in zip(layer_keys(i), sums, received):
            n, idx = k
            per_layer[k] = adamw_sharded(wts[n][idx], mom1[n][idx], mom2[n][idx], t_sum, recv, chip,
                                         name=f"adamw_{n}{idx}")
        after = [per_layer[k][0] for k in layer_keys(i)]
    for n in WEIGHT_NAMES:
        if n in REPLICATED:
            continue
        layers = [per_layer[n, idx] for idx in range(wts[n].shape[0])]
        grads[n], deltas[n], new_m[n], new_v[n] = (jnp.stack([lay[q] for lay in layers]) for q in range(4))

    return (loss, grad_x, *[grads[n] for n in WEIGHT_NAMES], *[deltas[n] for n in WEIGHT_NAMES],
            *[new_m[n] for n in WEIGHT_NAMES], *[new_v[n] for n in WEIGHT_NAMES])
```

```python
import functools
import math

import jax
import jax.numpy as jnp
from jax import lax
from jax.experimental import pallas as pl
from jax.experimental.pallas import tpu as pltpu

F32 = jnp.float32
BF16 = jnp.bfloat16
MESH = pl.DeviceIdType.MESH

N_DEV = 8
N_CHIP = 4
LANES = 128
VMEM_LIMIT_BYTES = 48 * 1024 * 1024
MATMUL_TILE_BYTES = 28 * 1024 * 1024

NORM_EPS = 1e-6
GATED_NORM_EPS = 1e-5
SSD_HEAD_DIM = 64
SSD_N_GROUPS = 8
SSD_D_STATE = 128
SSD_D_CONV = 4
SSD_CHUNK = 128
SB_HEAD_DIM = 128
PLE_DIM = 256

ADAM_LR = 0.001
ADAM_B1 = 0.9
ADAM_B2 = 0.999
ADAM_EPS = 1e-08
ADAM_WD = 0.01
ADAM_STEP = 10


def _cparams(sem=None, **kw):
    return pltpu.CompilerParams(dimension_semantics=sem, vmem_limit_bytes=VMEM_LIMIT_BYTES, **kw)


def _pick(dim, prefs):
    for t in prefs:
        if dim % t == 0:
            return t
    return dim


def _sigmoid(x):
    return 1.0 / (1.0 + jnp.exp(-x))


def _silu(x):
    return x * _sigmoid(x)


def _silu_grad(x):
    s = _sigmoid(x)
    return s * (1.0 + x * (1.0 - s))


def matmul(a, b, *, mode="nn", out_dtype=F32, res=None, out_blocks=None, name):
    b_blocked = b.ndim == 3
    if mode == "nn":
        m, kc = a.shape
        n = b.shape[-1] * (N_DEV if b_blocked else 1)
    elif mode == "nt":
        m, kc = a.shape
        n = b.shape[-2]
    else:
        kc, m = a.shape
        n = b.shape[-1]
    nb = b.shape[-1] if b_blocked else None
    tn = _pick(n if not out_blocks else out_blocks, (512, 256, 128))
    if b_blocked and mode == "nn":
        tn = _pick(nb, (512, 256, 128))
    k_limit = nb if (b_blocked and mode == "nt") else kc
    tm, tk = None, None
    for tm_try in (1024, 512, 256, 128):
        if m % tm_try:
            continue
        for tk_try in (k_limit, 2048, 1024, 512, 256, 128):
            if tk_try > k_limit or k_limit % tk_try:
                continue
            tiles = 2 * (tm_try * tk_try * a.dtype.itemsize + tk_try * tn * b.dtype.itemsize)
            tiles += tm_try * tn * (2 * jnp.dtype(out_dtype).itemsize + 4 + (8 if res is not None else 0))
            if tiles <= MATMUL_TILE_BYTES:
                tm, tk = tm_try, tk_try
                break
        if tm:
            break
    if tm is None:
        tm, tk = m, k_limit
    nk = kc // tk
    grid = (m // tm, n // tn, nk)

    if mode == "tn":
        a_spec = pl.BlockSpec((tk, tm), lambda i, j, k: (k, i))
        dims = (((0,), (0,)), ((), ()))
    else:
        a_spec = pl.BlockSpec((tm, tk), lambda i, j, k: (i, k))
        dims = (((1,), (0,)), ((), ())) if mode == "nn" else (((1,), (1,)), ((), ()))
    if mode == "nt":
        if b_blocked:
            per = nb // tk
            b_spec = pl.BlockSpec((None, tn, tk), lambda i, j, k: (k // per, j, k % per))
        else:
            b_spec = pl.BlockSpec((tn, tk), lambda i, j, k: (j, k))
    else:
        if b_blocked:
            per = nb // tn
            b_spec = pl.BlockSpec((None, tk, tn), lambda i, j, k: (j // per, k, j % per))
        else:
            b_spec = pl.BlockSpec((tk, tn), lambda i, j, k: (k, j))
    if out_blocks:
        per_o = out_blocks // tn
        out_shape = jax.ShapeDtypeStruct((n // out_blocks, m, out_blocks), out_dtype)
        out_spec = pl.BlockSpec((None, tm, tn), lambda i, j, k: (j // per_o, i, j % per_o))
    else:
        out_shape = jax.ShapeDtypeStruct((m, n), out_dtype)
        out_spec = pl.BlockSpec((tm, tn), lambda i, j, k: (i, j))
    in_specs = [a_spec, b_spec]
    args = [a, b]
    if res is not None:
        in_specs.append(pl.BlockSpec((tm, tn), lambda i, j, k: (i, j)))
        args.append(res)

    def body(*refs):
        a_ref, b_ref = refs[:2]
        r_ref = refs[2] if res is not None else None
        o_ref = refs[3] if res is not None else refs[2]

        def finish(r):
            if res is not None:
                r = r + r_ref[...].astype(F32)
            o_ref[...] = r.astype(out_dtype)

        part = lax.dot_general(a_ref[...].astype(BF16), b_ref[...].astype(BF16), dims, preferred_element_type=F32)
        if nk == 1:
            finish(part)
            return
        acc_ref = refs[-1]
        k = pl.program_id(2)

        @pl.when(k == 0)
        def _():
            acc_ref[...] = part

        @pl.when(k > 0)
        def _():
            acc_ref[...] += part

        @pl.when(k == nk - 1)
        def _():
            finish(acc_ref[...])

    return pl.pallas_call(
        body, out_shape=out_shape, grid=grid, in_specs=in_specs, out_specs=out_spec,
        scratch_shapes=[] if nk == 1 else [pltpu.VMEM((tm, tn), F32)], name=name,
        compiler_params=_cparams(("parallel", "parallel", "arbitrary")),
    )(*args)


def _dot(a, b, dims, precision=None):
    return lax.dot_general(a, b, (dims, ((), ())), preferred_element_type=F32, precision=precision)


_NN = ((1,), (0,))
_NT = ((1,), (1,))
_TN = ((0,), (0,))
_EXACT = lax.Precision.HIGHEST


def _chunk_decay_terms(dt, a):
    ln = dt.shape[0]
    row = lax.broadcasted_iota(jnp.int32, (ln, ln), 0)
    col = lax.broadcasted_iota(jnp.int32, (ln, ln), 1)
    tri = (row >= col).astype(F32)
    a_col = _dot(tri, dt * a, _NN, _EXACT)
    return a_col, a_col.T, row >= col


def _exact_dot(x, sel, terms):
    t = x.shape[0]
    parts, rest = [], x
    for k in range(terms):
        piece = rest.astype(BF16)
        parts.append(piece)
        if k + 1 < terms:
            rest = rest - piece.astype(F32)
    r = _dot(jnp.concatenate(parts, axis=0), sel, _NN)
    out = r[:t]
    for k in range(1, terms):
        out = out + r[k * t:(k + 1) * t]
    return out


def ssd_selectors(r_n):
    lane = jnp.arange(LANES)
    spread64 = (lane[:, None] == jnp.arange(r_n * SSD_HEAD_DIM)[None, :] // SSD_HEAD_DIM).astype(BF16)
    pair_sum = jnp.stack([lane[None, :] == 2 * q + lane[:, None] // SSD_HEAD_DIM for q in range(r_n // 2)]).astype(BF16)
    row_sum = jnp.stack([jnp.broadcast_to(lane[None, :] == r, (LANES, LANES)) for r in range(r_n)]).astype(BF16)
    return spread64, pair_sum, row_sum


def _ssd_chunk_setup(dt, a, spread64):
    ln = dt.shape[0]
    a_col, a_row, causal = _chunk_decay_terms(dt, a)
    ea = jnp.exp(a_col)
    te = jnp.exp(a_col[ln - 1:ln, :] - a_col)
    return (a_row, a_col, _exact_dot(dt, spread64, 2), _exact_dot(ea, spread64, 2), _exact_dot(te, spread64, 2),
            ea, causal)


def ssd_scan_fwd(xs, bm, cm, dtp, a_g, d_x, selectors, *, heads_per_group, name):
    s, di = xs.shape
    g_n = SSD_N_GROUPS
    r_n, p_n, n_n, ln = heads_per_group, SSD_HEAD_DIM, SSD_D_STATE, SSD_CHUNK
    nc = s // ln
    pairs, pw = r_n // 2, 2 * p_n
    spread64 = selectors[0]

    def body(xs_ref, bm_ref, cm_ref, dt_ref, a_ref, d_ref, s64_ref, y_ref, st_ref, state):
        c = pl.program_id(1)

        @pl.when(c == 0)
        def _():
            state[...] = jnp.zeros_like(state)

        a_row, a_col, dt_x, ea_x, te_x, _, causal = _ssd_chunk_setup(dt_ref[...], a_ref[...], s64_ref[...])
        bm_f = bm_ref[...]
        bmb = bm_f.astype(BF16)
        bm_t = bm_f.T.astype(BF16)
        cmb = cm_ref[...].astype(BF16)
        scores = _dot(cmb, bmb, _NT)
        first_head = lax.broadcasted_iota(jnp.int32, (1, pw), 1) < p_n
        for q in range(pairs):
            sl = slice(q * pw, (q + 1) * pw)
            x2 = xs_ref[:, sl]
            xdt2 = x2 * dt_x[:, sl]
            xdt2b = xdt2.astype(BF16)
            y_heads = []
            for r in (2 * q, 2 * q + 1):
                decay = jnp.exp(jnp.where(causal, a_col[:, r:r + 1] - a_row[r:r + 1, :], -jnp.inf))
                y_heads.append(_dot((scores * decay).astype(BF16), xdt2b, _NN))
            s2t = state[q]
            st_ref[q] = s2t
            y2 = jnp.where(first_head, y_heads[0], y_heads[1])
            y2 = y2 + ea_x[:, sl] * _dot(cmb, s2t.astype(BF16), _NN)
            y_ref[:, sl] = y2 + d_ref[:, sl] * x2
            state[q] = s2t * ea_x[ln - 1:ln, sl] + _dot(bm_t, (xdt2 * te_x[:, sl]).astype(BF16), _NN)

    whole = lambda t: pl.BlockSpec(t.shape, lambda g, c: (0,) * t.ndim)
    return pl.pallas_call(
        body,
        out_shape=(jax.ShapeDtypeStruct((s, di), F32),
                   jax.ShapeDtypeStruct((nc, g_n * pairs, n_n, pw), F32)),
        grid=(g_n, nc),
        in_specs=[pl.BlockSpec((ln, r_n * p_n), lambda g, c: (c, g)),
                  pl.BlockSpec((ln, n_n), lambda g, c: (c, g)),
                  pl.BlockSpec((ln, n_n), lambda g, c: (c, g)),
                  pl.BlockSpec((None, ln, LANES), lambda g, c: (g, c, 0)),
                  pl.BlockSpec((None, 1, LANES), lambda g, c: (g, 0, 0)),
                  pl.BlockSpec((None, 1, r_n * p_n), lambda g, c: (g, 0, 0)),
                  whole(spread64)],
        out_specs=(pl.BlockSpec((ln, r_n * p_n), lambda g, c: (c, g)),
                   pl.BlockSpec((None, pairs, n_n, pw), lambda g, c: (c, g, 0, 0))),
        scratch_shapes=[pltpu.VMEM((pairs, n_n, pw), F32)],
        name=name, compiler_params=_cparams(("parallel", "arbitrary")),
    )(xs, bm, cm, dtp, a_g, d_x, spread64)


def _row8(v):
    return jnp.broadcast_to(v, (8, v.shape[1]))


def ssd_scan_bwd(xs, bm, cm, dtp, a_g, d_x, selectors, states, dy, *, heads_per_group, name):
    s, di = xs.shape
    g_n = SSD_N_GROUPS
    r_n, p_n, n_n, ln = heads_per_group, SSD_HEAD_DIM, SSD_D_STATE, SSD_CHUNK
    nc = s // ln
    pairs, pw = r_n // 2, 2 * p_n
    spread64, pair_sum, row_sum = selectors

    def body(xs_ref, bm_ref, cm_ref, dt_ref, a_ref, d_ref, s64_ref, ps_ref, rs_ref, st_ref, dy_ref,
             dxs_ref, dbm_ref, dcm_ref, ddt_ref, dadt_ref, dd_ref, dstate, da_rows):
        c = pl.program_id(1)

        @pl.when(c == 0)
        def _():
            dstate[...] = jnp.zeros_like(dstate)
            dd_ref[...] = jnp.zeros_like(dd_ref)

        a_row, a_col, dt_x, ea_x, te_x, ea, causal = _ssd_chunk_setup(dt_ref[...], a_ref[...], s64_ref[...])
        row = lax.broadcasted_iota(jnp.int32, (ln, ln), 0)
        col = lax.broadcasted_iota(jnp.int32, (ln, ln), 1)
        causal_t = col >= row
        bmb = bm_ref[...].astype(BF16)
        cm_f = cm_ref[...]
        cmb = cm_f.astype(BF16)
        cm_t = cm_f.T.astype(BF16)
        scores = _dot(cmb, bmb, _NT)
        scores_t = _dot(bmb, cmb, _NT)
        first_head = lax.broadcasted_iota(jnp.int32, (1, pw), 1) < p_n
        e_last = ea[ln - 1:ln, :]
        da_rows[...] = jnp.zeros_like(da_rows)
        dscores = jnp.zeros((ln, ln), F32)
        dcm = jnp.zeros((ln, n_n), F32)
        dbm = jnp.zeros((ln, n_n), F32)
        da_cols = jnp.zeros((ln, LANES), F32)
        da_last = jnp.zeros((1, LANES), F32)
        ddt = jnp.zeros((ln, LANES), F32)
        dd = jnp.zeros((1, LANES), F32)
        for q in range(pairs):
            sl = slice(q * pw, (q + 1) * pw)
            sum2 = ps_ref[q]
            x2 = xs_ref[:, sl]
            dt2 = dt_x[:, sl]
            xdt2 = x2 * dt2
            xdt2b = xdt2.astype(BF16)
            dy2 = dy_ref[:, sl]
            dy2b = dy2.astype(BF16)
            dxdt_heads = []
            for h, r in enumerate((2 * q, 2 * q + 1)):
                a_r = jnp.broadcast_to(a_col[:, r:r + 1], (ln, ln))
                decay = jnp.exp(jnp.where(causal, a_r - a_row[r:r + 1, :], -jnp.inf))
                decay_t = jnp.exp(jnp.where(causal_t, a_row[r:r + 1, :] - a_r, -jnp.inf))
                dy_h = jnp.where(first_head if h == 0 else jnp.logical_not(first_head), dy2, 0.0).astype(BF16)
                dm = _dot(dy_h, xdt2b, _NT)
                dscores = dscores + dm * decay
                e_mat = dm * (scores * decay)
                da_cols = da_cols + _exact_dot(e_mat, rs_ref[r], 2)
                da_rows[r:r + 1, :] = -jnp.sum(e_mat, axis=0, keepdims=True)
                dxdt_heads.append(_dot((scores_t * decay_t).astype(BF16), dy2b, _NN))
            dxdt2 = jnp.where(first_head, dxdt_heads[0], dxdt_heads[1])
            s2t = st_ref[q]
            s2tb = s2t.astype(BF16)
            ds2t = dstate[q]
            ds2tb = ds2t.astype(BF16)
            ea2, te2 = ea_x[:, sl], te_x[:, sl]
            y_off2 = ea2 * _dot(cmb, s2tb, _NN)
            dy_e2 = (dy2 * ea2).astype(BF16)
            dcm = dcm + _dot(dy_e2, s2tb, _NT)
            ds_in = _dot(cm_t, dy_e2, _NN)
            da_cols = da_cols + _exact_dot(dy2 * y_off2, sum2, 2)
            bds2 = _dot(bmb, ds2tb, _NN)
            dxdt2 = dxdt2 + te2 * bds2
            xdt_e2 = xdt2 * te2
            dbm = dbm + _dot(xdt_e2.astype(BF16), ds2tb, _NT)
            w_cols = _exact_dot(xdt_e2 * bds2, sum2, 2)
            da_cols = da_cols - w_cols
            state_dot = _exact_dot(_row8(jnp.sum(ds2t * s2t, axis=0, keepdims=True)), sum2, 2)[0:1]
            da_last = da_last + jnp.sum(w_cols, axis=0, keepdims=True) + e_last * state_dot
            dstate[q] = ds2t * ea_x[ln - 1:ln, sl] + ds_in
            dxs_ref[:, sl] = dxdt2 * dt2 + d_ref[:, sl] * dy2
            ddt = ddt + _exact_dot(dxdt2 * x2, sum2, 2)
            dd = dd + _exact_dot(_row8(jnp.sum(dy2 * x2, axis=0, keepdims=True)), sum2, 2)[0:1]
        dcm_ref[...] = dcm + _dot(dscores.astype(BF16), bmb, _NN)
        dbm_ref[...] = dbm + _dot(dscores.T.astype(BF16), cmb, _NN)
        da_total = da_cols + da_rows[...].T
        upper = causal_t.astype(F32)
        dadt_ref[...] = _dot(upper, da_total, _NN, _EXACT) + da_last
        ddt_ref[...] = ddt
        dd_ref[...] += dd

    last_c = nc - 1
    whole = lambda t: pl.BlockSpec(t.shape, lambda g, c: (0,) * t.ndim)
    return pl.pallas_call(
        body,
        out_shape=(jax.ShapeDtypeStruct((s, di), F32),
                   jax.ShapeDtypeStruct(bm.shape, F32),
                   jax.ShapeDtypeStruct(cm.shape, F32),
                   jax.ShapeDtypeStruct(dtp.shape, F32),
                   jax.ShapeDtypeStruct(dtp.shape, F32),
                   jax.ShapeDtypeStruct(a_g.shape, F32)),
        grid=(g_n, nc),
        in_specs=[pl.BlockSpec((ln, r_n * p_n), lambda g, c: (last_c - c, g)),
                  pl.BlockSpec((ln, n_n), lambda g, c: (last_c - c, g)),
                  pl.BlockSpec((ln, n_n), lambda g, c: (last_c - c, g)),
                  pl.BlockSpec((None, ln, LANES), lambda g, c: (g, last_c - c, 0)),
                  pl.BlockSpec((None, 1, LANES), lambda g, c: (g, 0, 0)),
                  pl.BlockSpec((None, 1, r_n * p_n), lambda g, c: (g, 0, 0)),
                  whole(spread64), whole(pair_sum), whole(row_sum),
                  pl.BlockSpec((None, pairs, n_n, pw), lambda g, c: (last_c - c, g, 0, 0)),
                  pl.BlockSpec((ln, r_n * p_n), lambda g, c: (last_c - c, g))],
        out_specs=(pl.BlockSpec((ln, r_n * p_n), lambda g, c: (last_c - c, g)),
                   pl.BlockSpec((ln, n_n), lambda g, c: (last_c - c, g)),
                   pl.BlockSpec((ln, n_n), lambda g, c: (last_c - c, g)),
                   pl.BlockSpec((None, ln, LANES), lambda g, c: (g, last_c - c, 0)),
                   pl.BlockSpec((None, ln, LANES), lambda g, c: (g, last_c - c, 0)),
                   pl.BlockSpec((None, 1, LANES), lambda g, c: (g, 0, 0))),
        scratch_shapes=[pltpu.VMEM((pairs, n_n, pw), F32), pltpu.VMEM((LANES, ln), F32)],
        name=name, compiler_params=_cparams(("parallel", "arbitrary")),
    )(xs, bm, cm, dtp, a_g, d_x, spread64, pair_sum, row_sum, states, dy)


SB_TILE = 256


def _tri_sum(x, tri):
    t = x.shape[0]
    hi = x.astype(BF16)
    r1 = x - hi.astype(F32)
    mid = r1.astype(BF16)
    lo = (r1 - mid.astype(F32)).astype(BF16)
    r = _dot(jnp.concatenate([hi, mid, lo], axis=0), tri, _NN)
    return r[:t] + r[t:2 * t] + r[2 * t:]


def _sb_logits(q, k_j, scale, strict):
    z = _dot(q, k_j, _NT) * scale
    sp = jnp.log(1.0 + jnp.exp(-jnp.abs(z)))
    log_b = jnp.minimum(z, 0.0) - sp
    log_1mb = log_b - z
    if strict is not None:
        log_1mb = jnp.where(strict, log_1mb, 0.0)
    return log_b, log_1mb


def _sb_tile(s):
    return _pick(s, (SB_TILE, LANES))


def _sb_iotas(t):
    row = lax.broadcasted_iota(jnp.int32, (t, t), 0)
    col = lax.broadcasted_iota(jnp.int32, (t, t), 1)
    return row, col


def sb_attn_fwd(qn, kn, v, *, v_off=0, name):
    s, w = qn.shape
    dh = SB_HEAD_DIM
    n_h = w // dh
    t = _sb_tile(s)
    scale = 1.0 / math.sqrt(dh)

    def body(q_ref, k_ref, v_ref, o_ref, tot_ref):
        i = pl.program_id(1)
        q = q_ref[...]
        row, col = _sb_iotas(t)
        later = (row > col).astype(BF16)

        def tile(j, acc, run, strict):
            s0 = pl.multiple_of(j * t, t)
            k_j = k_ref[pl.ds(s0, t), :]
            v_j = v_ref[pl.ds(s0, t), :].astype(BF16)
            log_b, log_1mb = _sb_logits(q, k_j, scale, strict)
            att = jnp.exp(log_b + (_tri_sum(log_1mb, later) + run))
            if strict is not None:
                att = jnp.where(strict, att, 0.0)
            acc = acc + _dot(att.astype(BF16), v_j, _NN)
            return acc, run + jnp.sum(log_1mb, axis=1, keepdims=True)

        acc, run = tile(i, jnp.zeros((t, dh), F32), jnp.zeros((t, 1), F32), col < row)
        acc, run = lax.fori_loop(0, i, lambda jj, c: tile(i - 1 - jj, c[0], c[1], None), (acc, run))
        o_ref[...] = acc
        tot_ref[...] = jnp.broadcast_to(run, (t, dh))

    return pl.pallas_call(
        body,
        out_shape=(jax.ShapeDtypeStruct((s, w), F32), jax.ShapeDtypeStruct((s, w), F32)),
        grid=(n_h, s // t),
        in_specs=[pl.BlockSpec((t, dh), lambda h, i: (i, h)),
                  pl.BlockSpec((s, dh), lambda h, i: (0, h)),
                  pl.BlockSpec((s, dh), lambda h, i: (0, v_off + h))],
        out_specs=(pl.BlockSpec((t, dh), lambda h, i: (i, h)),
                   pl.BlockSpec((t, dh), lambda h, i: (i, h))),
        name=name, compiler_params=_cparams(("parallel", "parallel")),
    )(qn, kn, v)


def sb_attn_bwd(qn, kn, v, tot, do, *, v_off=0, name):
    s, w = qn.shape
    dh = SB_HEAD_DIM
    n_h = w // dh
    t = _sb_tile(s)
    scale = 1.0 / math.sqrt(dh)

    def body(q_ref, k_ref, v_ref, tot_ref, do_ref, dq_ref, dk_ref, dv_ref):
        dk_ref[...] = jnp.zeros_like(dk_ref)
        dv_ref[...] = jnp.zeros_like(dv_ref)
        row, col = _sb_iotas(t)
        upto = (row <= col).astype(BF16)
        before = (row < col).astype(BF16)

        def q_block(i, _):
            t0 = pl.multiple_of(i * t, t)
            q = q_ref[pl.ds(t0, t), :]
            do_i = do_ref[pl.ds(t0, t), :].astype(BF16)
            total = tot_ref[pl.ds(t0, t), 0:1]

            def tile(j, dq, run_l, run_g, strict):
                s0 = pl.multiple_of(j * t, t)
                k_j = k_ref[pl.ds(s0, t), :]
                v_j = v_ref[pl.ds(s0, t), :].astype(BF16)
                log_b, log_1mb = _sb_logits(q, k_j, scale, strict)
                att = jnp.exp(log_b + ((total - run_l) - _tri_sum(log_1mb, upto)))
                if strict is not None:
                    att = jnp.where(strict, att, 0.0)
                g = att * _dot(do_i, v_j, _NT)
                c = _tri_sum(g, before) + run_g
                dz = (g - (g + c) * jnp.exp(log_b)) * scale
                if strict is not None:
                    dz = jnp.where(strict, dz, 0.0)
                dz = dz.astype(BF16)
                dq = dq + _dot(dz, k_j, _NN)
                dk_ref[pl.ds(s0, t), :] += _dot(dz, q, _TN)
                dv_ref[pl.ds(s0, t), :] += _dot(att.astype(BF16), do_i, _TN)
                return (dq, run_l + jnp.sum(log_1mb, axis=1, keepdims=True),
                        run_g + jnp.sum(g, axis=1, keepdims=True))

            zero = jnp.zeros((t, 1), F32)
            carry = lax.fori_loop(0, i, lambda j, c: tile(j, c[0], c[1], c[2], None),
                                  (jnp.zeros((t, dh), F32), zero, zero))
            dq, _, _ = tile(i, carry[0], carry[1], carry[2], col < row)
            dq_ref[pl.ds(t0, t), :] = dq
            return 0

        lax.fori_loop(0, s // t, q_block, 0)

    head = pl.BlockSpec((s, dh), lambda h: (0, h))
    return pl.pallas_call(
        body,
        out_shape=tuple(jax.ShapeDtypeStruct((s, w), F32) for _ in range(3)),
        grid=(n_h,),
        in_specs=[head, head, pl.BlockSpec((s, dh), lambda h: (0, v_off + h)), head, head],
        out_specs=(head, head, head),
        name=name, compiler_params=_cparams(("parallel",)),
    )(qn, kn, v, tot, do)


ROW_TILE = 256
WIDE_ROW_TILE = 64


def _rows(width, col=0, tm=ROW_TILE):
    return pl.BlockSpec((tm, width), lambda i: (i, col))


_wide_rows = functools.partial(_rows, tm=WIDE_ROW_TILE)


def _whole(shape):
    return pl.BlockSpec(shape, lambda i: (0,) * len(shape))


def _ew_call(body, out_shape, in_specs, out_specs, args, n_rows, name, carried=False):
    return pl.pallas_call(
        body, out_shape=out_shape, grid=(n_rows // in_specs[0].block_shape[0],), in_specs=in_specs, out_specs=out_specs,
        name=name, compiler_params=_cparams(("arbitrary",) if carried else ("parallel",)),
    )(*args)


def _first_step(*refs):
    @pl.when(pl.program_id(0) == 0)
    def _():
        for r in refs:
            r[...] = jnp.zeros_like(r)


def rmsnorm_fwd(x, w, *, name):
    s, d = x.shape

    def body(x_ref, w_ref, o_ref):
        xv = x_ref[...]
        r = lax.rsqrt(jnp.mean(xv * xv, axis=-1, keepdims=True) + NORM_EPS)
        o_ref[...] = (xv * r * w_ref[...]).astype(BF16)

    return _ew_call(body, jax.ShapeDtypeStruct((s, d), BF16), [_rows(d), _whole((1, d))], _rows(d),
                    (x, w.reshape(1, d)), s, name)


def rmsnorm_bwd(x, w, dy, dres, *, name):
    s, d = x.shape

    def body(x_ref, w_ref, dy_ref, dr_ref, dx_ref, dw_ref):
        _first_step(dw_ref)
        xv = x_ref[...]
        r = lax.rsqrt(jnp.mean(xv * xv, axis=-1, keepdims=True) + NORM_EPS)
        xhat = xv * r
        dyv = dy_ref[...].astype(F32)
        dw_ref[...] += jnp.sum(dyv * xhat, axis=0, keepdims=True)
        g = dyv * w_ref[...]
        dx_ref[...] = dr_ref[...] + r * (g - xhat * jnp.mean(g * xhat, axis=-1, keepdims=True))

    return _ew_call(body, (jax.ShapeDtypeStruct((s, d), F32), jax.ShapeDtypeStruct((1, d), F32)),
                    [_rows(d), _whole((1, d)), _rows(d), _rows(d)], (_rows(d), _whole((1, d))),
                    (x, w.reshape(1, d), dy, dres), s, name, carried=True)


def ple_fwd(h1, gate_pre, pp, *, name):
    s, d = h1.shape

    def body(h_ref, g_ref, p_ref, o_ref):
        o_ref[...] = h_ref[...] + p_ref[...] * _sigmoid(g_ref[...])

    return _ew_call(body, jax.ShapeDtypeStruct((s, d), F32), [_rows(d)] * 3, _rows(d), (h1, gate_pre, pp), s, name)


def ple_bwd(dh2, gate_pre, pp, after, *, name):
    s, d = dh2.shape

    def body(dh_ref, g_ref, p_ref, after_ref, dp_ref, dg_ref):
        gate = _sigmoid(g_ref[...])
        dh = dh_ref[...]
        dp_ref[...] = (dh * gate).astype(BF16)
        dg_ref[...] = (dh * p_ref[...] * gate * (1.0 - gate)).astype(BF16)

    shp = jax.ShapeDtypeStruct((s, d), BF16)
    return _ew_call(body, (shp, shp), [_rows(d)] * 3 + [_whole(TOKEN_SHAPE)], (_rows(d), _rows(d)),
                    (dh2, gate_pre, pp, after), s, name)


def loss_head(y, target, *, name):
    s, d = y.shape

    def body(y_ref, t_ref, l_ref, dy_ref):
        _first_step(l_ref)
        err = y_ref[...] - t_ref[...]
        per_tok = jnp.mean(err * err, axis=-1, keepdims=True)
        l_ref[...] += 0.5 * jnp.sum(per_tok, axis=0, keepdims=True)
        dy_ref[...] = err * (1.0 / d)

    return _ew_call(body, (jax.ShapeDtypeStruct((1, 1), F32), jax.ShapeDtypeStruct((s, d), F32)),
                    [_rows(d), _rows(d)], (_whole((1, 1)), _rows(d)), (y, target), s, name, carried=True)


CONV_COL_TILE = 256


def _conv_taps(x, w_ref):
    row = lax.broadcasted_iota(jnp.int32, (x.shape[0], 1), 0)
    acc = x * w_ref[SSD_D_CONV - 1:SSD_D_CONV, :]
    shifted = []
    for d in range(1, SSD_D_CONV):
        xs = jnp.where(row >= d, pltpu.roll(x, d, 0), 0.0)
        shifted.append(xs)
        acc = acc + xs * w_ref[SSD_D_CONV - 1 - d:SSD_D_CONV - d, :]
    return acc, shifted


def ssd_conv_fwd(x, w, b, *, name):
    s, c = x.shape
    tc = _pick(c, (CONV_COL_TILE, LANES))

    def body(x_ref, w_ref, b_ref, o_ref):
        pre, _ = _conv_taps(x_ref[...], w_ref)
        o_ref[...] = _silu(pre + b_ref[...])

    col = pl.BlockSpec((s, tc), lambda j: (0, j))
    return pl.pallas_call(
        body, out_shape=jax.ShapeDtypeStruct((s, c), F32), grid=(c // tc,),
        in_specs=[col, pl.BlockSpec((SSD_D_CONV, tc), lambda j: (0, j)), pl.BlockSpec((1, tc), lambda j: (0, j))],
        out_specs=col, name=name, compiler_params=_cparams(("parallel",)),
    )(x, w, b)


def ssd_conv_bwd(x, w, b, dact, *, name):
    s, c = x.shape
    tc = _pick(c, (CONV_COL_TILE, LANES))

    def body(x_ref, w_ref, b_ref, da_ref, dx_ref, dw_ref, db_ref):
        xv = x_ref[...]
        pre, shifted = _conv_taps(xv, w_ref)
        dpre = da_ref[...] * _silu_grad(pre + b_ref[...])
        db_ref[...] = jnp.sum(dpre, axis=0, keepdims=True)
        row = lax.broadcasted_iota(jnp.int32, (s, 1), 0)
        dx = dpre * w_ref[SSD_D_CONV - 1:SSD_D_CONV, :]
        dw_ref[SSD_D_CONV - 1:SSD_D_CONV, :] = jnp.sum(dpre * xv, axis=0, keepdims=True)
        for d in range(1, SSD_D_CONV):
            k = SSD_D_CONV - 1 - d
            dw_ref[k:k + 1, :] = jnp.sum(dpre * shifted[d - 1], axis=0, keepdims=True)
            up = jnp.where(row < s - d, pltpu.roll(dpre, s - d, 0), 0.0)
            dx = dx + up * w_ref[k:k + 1, :]
        dx_ref[...] = dx.astype(BF16)

    col = pl.BlockSpec((s, tc), lambda j: (0, j))
    wspec = pl.BlockSpec((SSD_D_CONV, tc), lambda j: (0, j))
    bspec = pl.BlockSpec((1, tc), lambda j: (0, j))
    return pl.pallas_call(
        body,
        out_shape=(jax.ShapeDtypeStruct((s, c), BF16), jax.ShapeDtypeStruct((SSD_D_CONV, c), F32),
                   jax.ShapeDtypeStruct((1, c), F32)),
        grid=(c // tc,), in_specs=[col, wspec, bspec, col], out_specs=(col, wspec, bspec),
        name=name, compiler_params=_cparams(("parallel",)),
    )(x, w, b, dact)


def ssd_dt_fwd(dt_raw, bias, a_log, *, name):
    s, h = dt_raw.shape

    def body(r_ref, b_ref, al_ref, dt_ref, a_ref):
        zv = r_ref[...] + b_ref[...]
        dt_ref[...] = jnp.maximum(zv, 0.0) + jnp.log(1.0 + jnp.exp(-jnp.abs(zv)))
        a_ref[...] = -jnp.exp(al_ref[...])

    full = pl.BlockSpec((s, h), lambda: (0, 0))
    vec = pl.BlockSpec((1, h), lambda: (0, 0))
    return pl.pallas_call(
        body, out_shape=(jax.ShapeDtypeStruct((s, h), F32), jax.ShapeDtypeStruct((1, h), F32)),
        in_specs=[full, vec, vec], out_specs=(full, vec), name=name, compiler_params=_cparams(),
    )(dt_raw, bias.reshape(1, h), a_log.reshape(1, h))


def ssd_dt_bwd(dt_raw, bias, a_log, dt, ddt, dadt, *, name):
    s, h = dt_raw.shape

    def body(r_ref, b_ref, al_ref, dt_ref, ddt_ref, dadt_ref, dr_ref, db_ref, dal_ref):
        a = -jnp.exp(al_ref[...])
        dadt_v = dadt_ref[...]
        d_dt = ddt_ref[...] + a * dadt_v
        d_raw = d_dt * _sigmoid(r_ref[...] + b_ref[...])
        dr_ref[...] = d_raw
        db_ref[...] = jnp.sum(d_raw, axis=0, keepdims=True)
        dal_ref[...] = jnp.sum(dadt_v * dt_ref[...], axis=0, keepdims=True) * a

    full = pl.BlockSpec((s, h), lambda: (0, 0))
    vec = pl.BlockSpec((1, h), lambda: (0, 0))
    return pl.pallas_call(
        body, out_shape=(jax.ShapeDtypeStruct((s, h), F32), jax.ShapeDtypeStruct((1, h), F32),
                         jax.ShapeDtypeStruct((1, h), F32)),
        in_specs=[full, vec, vec, full, full, full], out_specs=(full, vec, vec), name=name,
        compiler_params=_cparams(),
    )(dt_raw, bias.reshape(1, h), a_log.reshape(1, h), dt, ddt, dadt)


def _group_mean(v, n_groups):
    gw = v.shape[-1] // n_groups
    parts = [jnp.broadcast_to(jnp.mean(v[:, k * gw:(k + 1) * gw], axis=-1, keepdims=True), (v.shape[0], gw))
             for k in range(n_groups)]
    return jnp.concatenate(parts, axis=-1)


def ssd_gate_fwd(y, z, gw, *, name):
    s, di = y.shape

    def body(y_ref, z_ref, w_ref, o_ref):
        yg = y_ref[...] * _silu(z_ref[...])
        r = lax.rsqrt(_group_mean(yg * yg, SSD_N_GROUPS) + GATED_NORM_EPS)
        o_ref[...] = (yg * r * w_ref[...]).astype(BF16)

    return _ew_call(body, jax.ShapeDtypeStruct((s, di), BF16), [_wide_rows(di), _wide_rows(di), _whole((1, di))],
                    _wide_rows(di), (y, z, gw.reshape(1, di)), s, name)


def ssd_gate_bwd(y, z, gw, dyn, *, name):
    s, di = y.shape

    def body(y_ref, z_ref, w_ref, dn_ref, dy_ref, dz_ref, dw_ref):
        _first_step(dw_ref)
        yv, zv = y_ref[...], z_ref[...]
        sz = _silu(zv)
        yg = yv * sz
        r = lax.rsqrt(_group_mean(yg * yg, SSD_N_GROUPS) + GATED_NORM_EPS)
        yhat = yg * r
        dn = dn_ref[...]
        dw_ref[...] += jnp.sum(dn * yhat, axis=0, keepdims=True)
        g = dn * w_ref[...]
        dyg = r * (g - yhat * _group_mean(g * yhat, SSD_N_GROUPS))
        dy_ref[...] = dyg * sz
        dz_ref[...] = (dyg * yv * _silu_grad(zv)).astype(BF16)

    return _ew_call(body, (jax.ShapeDtypeStruct((s, di), F32), jax.ShapeDtypeStruct((s, di), BF16),
                           jax.ShapeDtypeStruct((1, di), F32)),
                    [_wide_rows(di), _wide_rows(di), _whole((1, di)), _wide_rows(di)],
                    (_wide_rows(di), _wide_rows(di), _whole((1, di))),
                    (y, z, gw.reshape(1, di), dyn), s, name, carried=True)


def _head_mean(v):
    return _group_mean(v, v.shape[-1] // SB_HEAD_DIM)


def sb_qk_fwd(proj, qw, kw, *, name):
    s, w4 = proj.shape
    w = w4 // 4
    reps = w // SB_HEAD_DIM

    def body(q_ref, k_ref, qw_ref, kw_ref, qn_ref, kn_ref):
        for x_ref, w_ref, o_ref in ((q_ref, qw_ref, qn_ref), (k_ref, kw_ref, kn_ref)):
            xv = x_ref[...]
            r = lax.rsqrt(_head_mean(xv * xv) + NORM_EPS)
            o_ref[...] = (xv * r * jnp.tile(w_ref[...], (1, reps))).astype(BF16)

    shp = jax.ShapeDtypeStruct((s, w), BF16)
    return _ew_call(body, (shp, shp), [_rows(w, 0), _rows(w, 1), _whole((1, SB_HEAD_DIM)), _whole((1, SB_HEAD_DIM))],
                    (_rows(w), _rows(w)), (proj, proj, qw.reshape(1, -1), kw.reshape(1, -1)), s, name)


def sb_gate_fwd(o, proj, *, name):
    s, w = o.shape

    def body(o_ref, g_ref, og_ref):
        og_ref[...] = (o_ref[...] * _silu(g_ref[...])).astype(BF16)

    return _ew_call(body, jax.ShapeDtypeStruct((s, w), BF16), [_rows(w), _rows(w, 3)], _rows(w), (o, proj), s, name)


def sb_gate_bwd(dog, o, proj, *, name):
    s, w = o.shape

    def body(d_ref, o_ref, g_ref, do_ref, dg_ref):
        gv, dv = g_ref[...], d_ref[...]
        do_ref[...] = dv * _silu(gv)
        dg_ref[...] = (dv * o_ref[...] * _silu_grad(gv)).astype(BF16)

    return _ew_call(body, (jax.ShapeDtypeStruct((s, w), F32), jax.ShapeDtypeStruct((s, w), BF16)),
                    [_rows(w), _rows(w), _rows(w, 3)], (_rows(w), _rows(w)), (dog, o, proj), s, name)


def sb_pack_bwd(proj, qw, kw, dqn, dkn, dv, dg, *, name):
    s, w4 = proj.shape
    w = w4 // 4
    reps = w // SB_HEAD_DIM

    def body(q_ref, k_ref, qw_ref, kw_ref, dqn_ref, dkn_ref, dv_ref, dg_ref, dp_ref, dqw_ref, dkw_ref):
        _first_step(dqw_ref, dkw_ref)
        for idx, (x_ref, w_ref, d_ref, dw_ref) in enumerate(((q_ref, qw_ref, dqn_ref, dqw_ref),
                                                           (k_ref, kw_ref, dkn_ref, dkw_ref))):
            xv = x_ref[...]
            r = lax.rsqrt(_head_mean(xv * xv) + NORM_EPS)
            xhat = xv * r
            dn = d_ref[...]
            per_col = jnp.sum(dn * xhat, axis=0, keepdims=True)
            acc = per_col[:, 0:SB_HEAD_DIM]
            for hh in range(1, reps):
                acc = acc + per_col[:, hh * SB_HEAD_DIM:(hh + 1) * SB_HEAD_DIM]
            dw_ref[...] += acc
            g = dn * jnp.tile(w_ref[...], (1, reps))
            dp_ref[:, idx * w:(idx + 1) * w] = (r * (g - xhat * _head_mean(g * xhat))).astype(BF16)
        dp_ref[:, 2 * w:3 * w] = dv_ref[...].astype(BF16)
        dp_ref[:, 3 * w:4 * w] = dg_ref[...]

    vec = _whole((1, SB_HEAD_DIM))
    return _ew_call(body, (jax.ShapeDtypeStruct((s, w4), BF16), jax.ShapeDtypeStruct((1, SB_HEAD_DIM), F32),
                           jax.ShapeDtypeStruct((1, SB_HEAD_DIM), F32)),
                    [_wide_rows(w, 0), _wide_rows(w, 1), vec, vec, _wide_rows(w), _wide_rows(w), _wide_rows(w),
                     _wide_rows(w)],
                    (_wide_rows(w4), vec, vec),
                    (proj, proj, qw.reshape(1, -1), kw.reshape(1, -1), dqn, dkn, dv, dg), s, name, carried=True)


_HBM = pl.BlockSpec(memory_space=pltpu.HBM)


def _mesh_pos():
    return lax.axis_index("x"), lax.axis_index("y"), lax.axis_index("c")


def _other_chips(x, y):
    return [(1 - x, y), (x, 1 - y), (1 - x, 1 - y)]


_SEM = pl.BlockSpec(memory_space=pltpu.SEMAPHORE)
_ANY = pl.BlockSpec(memory_space=pl.ANY)
_DATAFLOW = pltpu.SideEffectType.DATAFLOW_SIDE_EFFECTING
N_PEER_CHIPS = N_CHIP - 1
TOKEN_SHAPE = (8, LANES)


def _in_hbm(t):
    return pltpu.with_memory_space_constraint(t, pltpu.HBM)


def _ici_copies(kind, src_refs, land_refs, send_sems, recv_sems, arrivals=False):
    x, y, c = _mesh_pos()
    out = []
    for a in range(len(src_refs)):
        for j, chip in enumerate(_other_chips(x, y)):
            if kind == "gather":
                src = src_refs[a]
                dst = land_refs[a].at[4 * chip[0] + 2 * chip[1] + c if arrivals else 4 * x + 2 * y + c]
            else:
                src, dst = src_refs[a].at[2 * chip[0] + chip[1]], land_refs[a].at[j]
            k = a * N_PEER_CHIPS + j
            out.append(pltpu.make_async_remote_copy(
                src_ref=src, dst_ref=dst, send_sem=send_sems.at[k], recv_sem=recv_sems.at[k],
                device_id=(*chip, c), device_id_type=MESH))
    return out


def ici_start(kind, srcs, lands, after=(), *, name):
    n = len(srcs)

    def body(*refs):
        first_out = 2 * n + len(after)
        for cp in _ici_copies(kind, refs[:n], refs[n:2 * n], refs[first_out], refs[first_out + 1]):
            cp.start()
        refs[-1][...] = jnp.zeros(TOKEN_SHAPE, F32)

    outs = pl.pallas_call(
        body, name=name,
        out_shape=(pltpu.SemaphoreType.DMA((n * N_PEER_CHIPS,)), pltpu.SemaphoreType.DMA((n * N_PEER_CHIPS,)),
                   *[pltpu.HBM(t.shape, t.dtype) for t in srcs], *[pltpu.HBM(t.shape, t.dtype) for t in lands],
                   jax.ShapeDtypeStruct(TOKEN_SHAPE, F32)),
        in_specs=[_HBM] * (2 * n) + [_ANY] * len(after),
        out_specs=(_SEM, _SEM, *([_HBM] * (2 * n)), pl.BlockSpec(memory_space=pltpu.VMEM)),
        input_output_aliases={k: 2 + k for k in range(2 * n)},
        compiler_params=pltpu.CompilerParams(has_side_effects=_DATAFLOW),
    )(*[_in_hbm(t) for t in srcs], *[_in_hbm(t) for t in lands], *after)
    return outs[0], outs[1], list(outs[2:2 + n]), list(outs[2 + n:2 + 2 * n]), outs[-1]


def ici_wait(kind, started, after, *, name):
    send_sems, recv_sems, srcs, lands, _ = started
    n = len(srcs)

    def body(*refs):
        for cp in _ici_copies(kind, refs[:n], refs[n:2 * n], refs[2 * n], refs[2 * n + 1]):
            cp.wait_send()
        for cp in _ici_copies(kind, refs[:n], refs[n:2 * n], refs[2 * n], refs[2 * n + 1], arrivals=True):
            cp.wait_recv()

    outs = pl.pallas_call(
        body, name=name,
        out_shape=tuple(pltpu.HBM(t.shape, t.dtype) for t in (*srcs, *lands)),
        in_specs=[_HBM] * (2 * n) + [_SEM, _SEM] + [_ANY] * len(after),
        out_specs=tuple([_HBM] * (2 * n)),
        input_output_aliases={k: k for k in range(2 * n)},
        compiler_params=pltpu.CompilerParams(has_side_effects=_DATAFLOW),
    )(*srcs, *lands, send_sems, recv_sems, *after)
    return list(outs[:n]), list(outs[n:])


def gather_finish(shards, lands, *, name):
    n = len(shards)
    by_core = [t.reshape(N_CHIP, 2, *t.shape[1:]) for t in lands]

    def body(*refs):
        x_refs, out_refs = refs[:n], refs[2 * n:3 * n]
        send_sems, recv_sems, local_sems = refs[3 * n:]
        x, y, c = _mesh_pos()
        local = [pltpu.make_async_copy(x_refs[a], out_refs[a].at[2 * x + y, c], local_sems.at[a]) for a in range(n)]
        for cp in local:
            cp.start()
        for cp in local:
            cp.wait()
        sends = [pltpu.make_async_remote_copy(
            src_ref=out_refs[a].at[:, c], dst_ref=out_refs[a].at[:, c], send_sem=send_sems.at[a],
            recv_sem=recv_sems.at[a], device_id=(x, y, 1 - c), device_id_type=MESH) for a in range(n)]
        for cp in sends:
            cp.start()
        for cp in sends:
            cp.wait_send()
        for a in range(n):
            pltpu.make_async_remote_copy(
                src_ref=out_refs[a].at[:, c], dst_ref=out_refs[a].at[:, 1 - c], send_sem=send_sems.at[a],
                recv_sem=recv_sems.at[a], device_id=(x, y, 1 - c), device_id_type=MESH).wait_recv()

    outs = pl.pallas_call(
        body, out_shape=tuple(jax.ShapeDtypeStruct(t.shape, t.dtype) for t in by_core),
        in_specs=[_HBM] * (2 * n), out_specs=tuple([_HBM] * n),
        input_output_aliases={n + a: a for a in range(n)},
        scratch_shapes=[pltpu.SemaphoreType.DMA((n,)), pltpu.SemaphoreType.DMA((n,)), pltpu.SemaphoreType.DMA((n,))],
        name=name,
    )(*shards, *by_core)
    return [o.reshape(t.shape) for o, t in zip(outs, lands)]


def sibling_exchange(grads, *, name):
    n = len(grads)

    def body(*refs):
        g_refs, r_refs = refs[:n], refs[n:2 * n]
        send_sems, recv_sems = refs[2 * n:]
        x, y, c = _mesh_pos()
        copies = [pltpu.make_async_remote_copy(
            src_ref=g_refs[a].at[:, 1 - c], dst_ref=r_refs[a], send_sem=send_sems.at[a],
            recv_sem=recv_sems.at[a], device_id=(x, y, 1 - c), device_id_type=MESH) for a in range(n)]
        for cp in copies:
            cp.start()
        for cp in copies:
            cp.wait()

    return pl.pallas_call(
        body, out_shape=tuple(jax.ShapeDtypeStruct((N_CHIP,) + g.shape[2:], g.dtype) for g in grads),
        in_specs=[_HBM] * n, out_specs=tuple([_HBM] * n),
        scratch_shapes=[pltpu.SemaphoreType.DMA((n,)), pltpu.SemaphoreType.DMA((n,))],
        name=name,
    )(*grads)


def all_reduce_small(v, *, name):
    r = v.shape[0]

    def body(v_ref, o_ref, buf, send_sems, recv_sems):
        x, y, c = _mesh_pos()
        me = 4 * x + 2 * y + c
        buf[me] = v_ref[...]
        copies = []
        for k in range(1, N_DEV):
            to = ((x + (k >> 2)) % 2, (y + ((k >> 1) & 1)) % 2, (c + (k & 1)) % 2)
            copies.append(pltpu.make_async_remote_copy(
                src_ref=v_ref, dst_ref=buf.at[me], send_sem=send_sems.at[k - 1], recv_sem=recv_sems.at[k - 1],
                device_id=to, device_id_type=MESH))
        for cp in copies:
            cp.start()
        for cp in copies:
            cp.wait()
        acc = buf[0]
        for d in range(1, N_DEV):
            acc = acc + buf[d]
        o_ref[...] = acc

    vm = pl.BlockSpec(memory_space=pltpu.VMEM)
    return pl.pallas_call(
        body, out_shape=jax.ShapeDtypeStruct(v.shape, F32), in_specs=[vm], out_specs=vm,
        scratch_shapes=[pltpu.VMEM((N_DEV, r, LANES), F32), pltpu.SemaphoreType.DMA((N_DEV - 1,)),
                        pltpu.SemaphoreType.DMA((N_DEV - 1,))],
        name=name,
    )(v)


def pair_add(g, r1, core, *, name):
    _, _, rows, cols = g.shape
    tm = _pick(rows, (256, 128))

    def body(c_ref, g_ref, r_ref, o_ref):
        o_ref[...] = (g_ref[...].astype(F32) + r_ref[...].astype(F32)).astype(o_ref.dtype)

    return pl.pallas_call(
        body, out_shape=jax.ShapeDtypeStruct(r1.shape, g.dtype),
        grid_spec=pltpu.PrefetchScalarGridSpec(
            num_scalar_prefetch=1, grid=(N_CHIP, rows // tm),
            in_specs=[pl.BlockSpec((None, None, tm, cols), lambda k, i, c_ref: (k, c_ref[0], i, 0)),
                      pl.BlockSpec((None, tm, cols), lambda k, i, c_ref: (k, i, 0))],
            out_specs=pl.BlockSpec((None, tm, cols), lambda k, i, c_ref: (k, i, 0))),
        name=name, compiler_params=_cparams(("parallel", "parallel")),
    )(core, g, r1)


def _adamw_math(w, g, m, v):
    m = ADAM_B1 * m + (1.0 - ADAM_B1) * g
    v = ADAM_B2 * v + (1.0 - ADAM_B2) * (g * g)
    m_hat = m / (1.0 - ADAM_B1 ** ADAM_STEP)
    v_hat = v / (1.0 - ADAM_B2 ** ADAM_STEP)
    delta = -ADAM_LR * (m_hat / (jnp.sqrt(v_hat) + ADAM_EPS) + ADAM_WD * w)
    return delta, m, v


def adamw_sharded(w, m, v, layer, chip_sums, received, chip, into, *, name):
    _, rows, cols = w.shape
    tm = _pick(rows, (256, 128))

    def body(k_ref, w_ref, m_ref, v_ref, t_ref, r_ref, *rest):
        g_ref, d_ref, nm_ref, nv_ref = rest[-4:]
        g = t_ref[...].astype(F32)
        for j in range(N_CHIP - 1):
            g = g + r_ref[j].astype(F32)
        d, mm, vv = _adamw_math(w_ref[...], g, m_ref[...], v_ref[...])
        g_ref[...] = g
        d_ref[...] = d
        nm_ref[...] = mm
        nv_ref[...] = vv

    blk = pl.BlockSpec((None, tm, cols), lambda i, k_ref: (layer, i, 0))
    shp = jax.ShapeDtypeStruct(w.shape, F32)
    in_specs = [blk, blk, blk,
                pl.BlockSpec((None, tm, cols), lambda i, k_ref: (k_ref[0], i, 0)),
                pl.BlockSpec((N_CHIP - 1, tm, cols), lambda i, k_ref: (0, i, 0))]
    operands = [chip, w, m, v, chip_sums, received]
    aliases = {}
    if into is not None:
        aliases = {len(operands) + q: q for q in range(4)}
        in_specs += [_ANY] * 4
        operands += list(into)
    return pl.pallas_call(
        body, out_shape=(shp, shp, shp, shp),
        grid_spec=pltpu.PrefetchScalarGridSpec(
            num_scalar_prefetch=1, grid=(rows // tm,), in_specs=in_specs, out_specs=(blk, blk, blk, blk)),
        input_output_aliases=aliases,
        name=name, compiler_params=_cparams(("parallel",)),
    )(*operands)


def adamw_replicated(w, m, v, g, *, name):
    def body(w_ref, m_ref, v_ref, g_ref, d_ref, nm_ref, nv_ref):
        d, mm, vv = _adamw_math(w_ref[...], g_ref[...], m_ref[...], v_ref[...])
        d_ref[...] = d
        nm_ref[...] = mm
        nv_ref[...] = vv

    shp = jax.ShapeDtypeStruct(w.shape, F32)
    return pl.pallas_call(body, out_shape=(shp, shp, shp), name=name, compiler_params=_cparams())(w, m, v, g)


WEIGHT_NAMES = ("norm_w", "ssd_in_w", "ssd_conv_w", "ssd_conv_b", "ssd_dt_bias", "ssd_a_log", "ssd_d",
                "ssd_gnorm_w", "ssd_out_w", "sb_in_w", "sb_qn_w", "sb_kn_w", "sb_out_w", "ple_norm_w",
                "ple_gate_w", "ple_proj_w")
REPLICATED = ("norm_w", "ssd_conv_b", "ssd_dt_bias", "ssd_a_log", "ssd_d", "ssd_gnorm_w", "sb_qn_w", "sb_kn_w",
              "ple_norm_w")
PACK_ROWS = 8


def _pack(parts):
    flat = jnp.concatenate([t.reshape(-1) for t in parts])
    pad = (-flat.shape[0]) % (PACK_ROWS * LANES)
    return jnp.pad(flat, (0, pad)).reshape(-1, LANES)


def _unpack(packed, like):
    flat = packed.reshape(-1)
    out, off = [], 0
    for t in like:
        out.append(flat[off:off + t.size].reshape(t.shape))
        off += t.size
    return out


def _to_group_lanes(v, r):
    t = v.reshape(v.shape[0], SSD_N_GROUPS, r).transpose(1, 0, 2)
    return jnp.pad(t, ((0, 0), (0, 0), (0, LANES - r)))


def _from_group_lanes(t, r):
    return t[:, :, :r].transpose(1, 0, 2).reshape(t.shape[1], SSD_N_GROUPS * r)


def _head_vec(v, r):
    return jnp.pad(v.reshape(SSD_N_GROUPS, 1, r), ((0, 0), (0, 0), (0, LANES - r)))


def _col_blocks(full):
    rows = full.shape[0]
    return full.reshape(rows, N_DEV, -1).transpose(1, 0, 2)


def _from_col_blocks(blocks):
    return blocks.transpose(1, 0, 2).reshape(blocks.shape[1], -1)


def _split_cols(full, widths):
    out, off = [], 0
    for w in widths:
        out.append(full[:, off:off + w])
        off += w
    return out


def kernel(x, p, norm_w, ssd_in_w, ssd_conv_w, ssd_conv_b, ssd_dt_bias, ssd_a_log, ssd_d, ssd_gnorm_w, ssd_out_w, sb_in_w, sb_qn_w, sb_kn_w, sb_out_w, ple_norm_w, ple_gate_w, ple_proj_w, loss_target, m_norm_w, m_ssd_in_w, m_ssd_conv_w, m_ssd_conv_b, m_ssd_dt_bias, m_ssd_a_log, m_ssd_d, m_ssd_gnorm_w, m_ssd_out_w, m_sb_in_w, m_sb_qn_w, m_sb_kn_w, m_sb_out_w, m_ple_norm_w, m_ple_gate_w, m_ple_proj_w, v_norm_w, v_ssd_in_w, v_ssd_conv_w, v_ssd_conv_b, v_ssd_dt_bias, v_ssd_a_log, v_ssd_d, v_ssd_gnorm_w, v_ssd_out_w, v_sb_in_w, v_sb_qn_w, v_sb_kn_w, v_sb_out_w, v_ple_norm_w, v_ple_gate_w, v_ple_proj_w):
    env = dict(locals())
    wts = {n: env[n] for n in WEIGHT_NAMES}
    mom1 = {n: env["m_" + n] for n in WEIGHT_NAMES}
    mom2 = {n: env["v_" + n] for n in WEIGHT_NAMES}

    s, d = x.shape[1], x.shape[2]
    depth = norm_w.shape[0]
    n_ssd, n_sb = ssd_in_w.shape[0], sb_in_w.shape[0]
    di = ssd_out_w.shape[1] * N_DEV
    n_heads = ssd_dt_bias.shape[1]
    hpg = n_heads // SSD_N_GROUPS
    nbc = SSD_N_GROUPS * SSD_D_STATE
    in_segs = (di, di, nbc, nbc, n_heads)
    conv_segs = (di, nbc, nbc)
    sb_w = sb_out_w.shape[1] * N_DEV
    selectors = ssd_selectors(hpg)
    xi, yi, ci = _mesh_pos()
    core = ci.astype(jnp.int32).reshape(1)
    chip = (2 * xi + yi).astype(jnp.int32).reshape(1)

    def layer_keys(i):
        j = i // 2
        mixer = [("ssd_in_w", j), ("ssd_conv_w", j), ("ssd_out_w", j)] if i % 2 == 0 else [("sb_in_w", j), ("sb_out_w", j)]
        return mixer + [("ple_gate_w", i), ("ple_proj_w", i)]

    def shard_of(key):
        t = wts[key[0]][key[1]]
        return t if key[0] == "ssd_conv_w" else t.astype(BF16)

    gathers = []
    for i in range(depth):
        shards = [shard_of(k) for k in layer_keys(i)]
        gathers.append(ici_start("gather", shards, [lax.empty((N_DEV,) + t.shape, t.dtype) for t in shards],
                                 after=[g[4] for g in gathers[-1:]], name=f"ag{i}_start"))
    all_started = gathers[-1][4]
    ssd_full, sb_full, ple_full = {}, {}, {}

    def gather_layer(i, after):
        shards, lands = ici_wait("gather", gathers[i], [after], name=f"ag{i}_wait")
        full = dict(zip(layer_keys(i), gather_finish(shards, lands, name=f"ag{i}_finish")))
        j = i // 2
        if i % 2 == 0:
            ssd_full[j] = dict(
                w_in=_split_cols(_from_col_blocks(full["ssd_in_w", j]), in_segs),
                conv_w=_split_cols(_from_col_blocks(full["ssd_conv_w", j]), conv_segs),
                conv_b=_split_cols(ssd_conv_b[j].reshape(1, -1), conv_segs),
                w_out=full["ssd_out_w", j].reshape(di, d))
        else:
            sb_full[j] = dict(w_in=full["sb_in_w", j], w_out=full["sb_out_w", j].reshape(sb_w, d))
        ple_full[i] = dict(w_gate=full["ple_gate_w", i].reshape(d, d), w_proj=full["ple_proj_w", i])

    h = x.reshape(s, d)
    saved = []
    for i in range(depth):
        j = i // 2
        gather_layer(i, all_started if i == 0 else h)
        sv = dict(h_in=h)
        u = rmsnorm_fwd(h, norm_w[i], name=f"l{i}_norm")
        sv["u"] = u
        if i % 2 == 0:
            fw = ssd_full[j]
            raw = [matmul(u, wseg, name=f"l{i}_in{q}") for q, wseg in enumerate(fw["w_in"])]
            z, dt_raw = raw[0], raw[4]
            act = [ssd_conv_fwd(raw[1 + q], fw["conv_w"][q], fw["conv_b"][q], name=f"l{i}_conv{q}") for q in range(3)]
            dt, a_neg = ssd_dt_fwd(dt_raw, ssd_dt_bias[j], ssd_a_log[j], name=f"l{i}_dt")
            dtp = _to_group_lanes(dt, hpg)
            a_g = _head_vec(a_neg.reshape(-1), hpg)
            d_x = jnp.repeat(ssd_d[j].reshape(SSD_N_GROUPS, 1, hpg), SSD_HEAD_DIM, axis=2)
            y, states = ssd_scan_fwd(act[0], act[1], act[2], dtp, a_g, d_x, selectors, heads_per_group=hpg,
                                     name=f"l{i}_scan")
            yn = ssd_gate_fwd(y, z, ssd_gnorm_w[j], name=f"l{i}_gate")
            h1 = matmul(yn, fw["w_out"], res=h, name=f"l{i}_out")
            sv.update(raw=raw, act=act, dt=dt, dtp=dtp, a_g=a_g, d_x=d_x, y=y, states=states, yn=yn)
        else:
            fw = sb_full[j]
            proj = matmul(u, fw["w_in"], name=f"l{i}_in")
            qn, kn = sb_qk_fwd(proj, sb_qn_w[j], sb_kn_w[j], name=f"l{i}_qknorm")
            v_off = 2 * sb_w // SB_HEAD_DIM
            o, tot = sb_attn_fwd(qn, kn, proj, v_off=v_off, name=f"l{i}_attn")
            og = sb_gate_fwd(o, proj, name=f"l{i}_gate")
            h1 = matmul(og, fw["w_out"], res=h, name=f"l{i}_out")
            sv.update(proj=proj, qn=qn, kn=kn, o=o, tot=tot, og=og, v_off=v_off)
        t = rmsnorm_fwd(h1, ple_norm_w[i], name=f"l{i}_plenorm")
        gate_pre = matmul(t, ple_full[i]["w_gate"], name=f"l{i}_plegate")
        pp = matmul(p[i, 0], ple_full[i]["w_proj"], name=f"l{i}_pleproj")
        h = ple_fwd(h1, gate_pre, pp, name=f"l{i}_ple")
        sv.update(h1=h1, t=t, gate_pre=gate_pre, pp=pp)
        saved.append(sv)

    loss_part, dh = loss_head(h, loss_target.reshape(s, d), name="loss_head")
    loss = lax.psum(loss_part[0, 0], ("x", "y", "c"))

    big = {}
    small = {n: [None] * wts[n].shape[0] for n in REPLICATED}
    scatters = {}
    order_after = jnp.zeros(TOKEN_SHAPE, F32)
    for i in reversed(range(depth)):
        j = i // 2
        sv = saved[i]
        dpp, dgp = ple_bwd(dh, sv["gate_pre"], sv["pp"], order_after, name=f"b{i}_ple")
        big["ple_proj_w", i] = matmul(p[i, 0], dpp, mode="tn", out_dtype=BF16, out_blocks=ple_proj_w.shape[2],
                                      name=f"b{i}_pleproj_w")
        big["ple_gate_w", i] = matmul(sv["t"], dgp, mode="tn", out_dtype=BF16, name=f"b{i}_plegate_w").reshape(N_DEV, -1, d)
        dt_ = matmul(dgp, ple_full[i]["w_gate"], mode="nt", name=f"b{i}_plegate_x")
        dh1, g_pn = rmsnorm_bwd(sv["h1"], ple_norm_w[i], dt_, dh, name=f"b{i}_plenorm")
        small["ple_norm_w"][i] = g_pn
        u = sv["u"]
        if i % 2 == 0:
            fw = ssd_full[j]
            raw, act = sv["raw"], sv["act"]
            big["ssd_out_w", j] = matmul(sv["yn"], dh1, mode="tn", out_dtype=BF16, name=f"b{i}_out_w").reshape(N_DEV, -1, d)
            dyn = matmul(dh1, fw["w_out"], mode="nt", name=f"b{i}_out_x")
            dy, dz, g_gn = ssd_gate_bwd(sv["y"], raw[0], ssd_gnorm_w[j], dyn, name=f"b{i}_gate")
            dxs, dbm, dcm, ddtp, dadtp, dd_g = ssd_scan_bwd(act[0], act[1], act[2], sv["dtp"], sv["a_g"], sv["d_x"], selectors,
                                                          sv["states"], dy, heads_per_group=hpg, name=f"b{i}_scan")
            ddt_raw, g_dtb, g_alog = ssd_dt_bwd(raw[4], ssd_dt_bias[j], ssd_a_log[j], sv["dt"],
                                                _from_group_lanes(ddtp, hpg), _from_group_lanes(dadtp, hpg),
                                                name=f"b{i}_dt")
            conv_back = [ssd_conv_bwd(raw[1 + q], fw["conv_w"][q], fw["conv_b"][q], dact, name=f"b{i}_conv{q}")
                         for q, dact in enumerate((dxs, dbm, dcm))]
            dsegs = [dz] + [cb[0] for cb in conv_back] + [ddt_raw]
            g_in = jnp.concatenate([matmul(u, ds, mode="tn", out_dtype=BF16, name=f"b{i}_in{q}_w")
                                    for q, ds in enumerate(dsegs)], axis=1)
            big["ssd_in_w", j] = _col_blocks(g_in)
            big["ssd_conv_w", j] = _col_blocks(jnp.concatenate([cb[1] for cb in conv_back], axis=1))
            du = None
            for q, (ds, wseg) in enumerate(zip(dsegs, fw["w_in"])):
                du = matmul(ds, wseg, mode="nt", res=du, name=f"b{i}_in{q}_x")
            small["ssd_conv_b"][j] = jnp.concatenate([cb[2] for cb in conv_back], axis=1)
            small["ssd_dt_bias"][j] = g_dtb
            small["ssd_a_log"][j] = g_alog
            small["ssd_d"][j] = dd_g[:, 0, :hpg]
            small["ssd_gnorm_w"][j] = g_gn
        else:
            fw = sb_full[j]
            proj = sv["proj"]
            big["sb_out_w", j] = matmul(sv["og"], dh1, mode="tn", out_dtype=BF16, name=f"b{i}_out_w").reshape(N_DEV, -1, d)
            dog = matmul(dh1, fw["w_out"], mode="nt", name=f"b{i}_out_x")
            do, dg = sb_gate_bwd(dog, sv["o"], proj, name=f"b{i}_gate")
            dqn, dkn, dv = sb_attn_bwd(sv["qn"], sv["kn"], proj, sv["tot"], do, v_off=sv["v_off"], name=f"b{i}_attn")
            dproj, g_qn, g_kn = sb_pack_bwd(proj, sb_qn_w[j], sb_kn_w[j], dqn, dkn, dv, dg, name=f"b{i}_qknorm")
            big["sb_in_w", j] = matmul(u, dproj, mode="tn", out_dtype=BF16, out_blocks=sb_in_w.shape[2], name=f"b{i}_in_w")
            du = matmul(dproj, fw["w_in"], mode="nt", name=f"b{i}_in_x")
            small["sb_qn_w"][j] = g_qn
            small["sb_kn_w"][j] = g_kn
        dh, g_n = rmsnorm_bwd(sv["h_in"], norm_w[i], du, dh1, name=f"b{i}_norm")
        small["norm_w"][i] = g_n
        blocks = [big[k].reshape(N_CHIP, 2, *big[k].shape[1:]) for k in layer_keys(i)]
        from_sibling = sibling_exchange(blocks, name=f"rs{i}_sibling")
        sums = [pair_add(g, r1, core, name=f"rs{i}_pair_add{q}") for q, (g, r1) in enumerate(zip(blocks, from_sibling))]
        scatters[i] = ici_start("scatter", sums, [lax.empty((N_PEER_CHIPS,) + t.shape[1:], t.dtype) for t in sums],
                                name=f"rs{i}_start")
        order_after = scatters[i][4]
    grad_x = dh.reshape(x.shape)

    rep_like = [wts[n] for n in REPLICATED]
    g_packed = all_reduce_small(_pack([jnp.stack([t.reshape(-1) for t in small[n]]) for n in REPLICATED]),
                                name="all_reduce_small_grads")
    d_packed, m_packed, v_packed = adamw_replicated(
        _pack(rep_like), _pack([mom1[n] for n in REPLICATED]), _pack([mom2[n] for n in REPLICATED]), g_packed,
        name="adamw_replicated")
    grads = dict(zip(REPLICATED, _unpack(g_packed, rep_like)))
    deltas = dict(zip(REPLICATED, _unpack(d_packed, rep_like)))
    new_m = dict(zip(REPLICATED, _unpack(m_packed, rep_like)))
    new_v = dict(zip(REPLICATED, _unpack(v_packed, rep_like)))

    updated = {}
    after = [order_after]
    for i in reversed(range(depth)):
        sums, received = ici_wait("scatter", scatters[i], after, name=f"rs{i}_wait")
        for (n, idx), t_sum, recv in zip(layer_keys(i), sums, received):
            updated[n] = adamw_sharded(wts[n], mom1[n], mom2[n], idx, t_sum, recv, chip, updated.get(n),
                                       name=f"adamw_{n}{idx}")
        after = [updated[n][0] for n, _ in layer_keys(i)]
    for n, (g_n, d_n, m_n, v_n) in updated.items():
        grads[n], deltas[n], new_m[n], new_v[n] = g_n, d_n, m_n, v_n

    return (loss, grad_x, *[grads[n] for n in WEIGHT_NAMES], *[deltas[n] for n in WEIGHT_NAMES],
            *[new_m[n] for n in WEIGHT_NAMES], *[new_v[n] for n in WEIGHT_NAMES])
```

```python
import functools
import math

import jax
import jax.numpy as jnp
from jax import lax
from jax.experimental import pallas as pl
from jax.experimental.pallas import tpu as pltpu

F32 = jnp.float32
BF16 = jnp.bfloat16
MESH = pl.DeviceIdType.MESH

N_DEV = 8
N_CHIP = 4
LANES = 128
VMEM_LIMIT_BYTES = 48 * 1024 * 1024
MATMUL_TILE_BYTES = 28 * 1024 * 1024

NORM_EPS = 1e-6
GATED_NORM_EPS = 1e-5
SSD_HEAD_DIM = 64
SSD_N_GROUPS = 8
SSD_D_STATE = 128
SSD_D_CONV = 4
SSD_CHUNK = 128
SB_HEAD_DIM = 128
PLE_DIM = 256

ADAM_LR = 0.001
ADAM_B1 = 0.9
ADAM_B2 = 0.999
ADAM_EPS = 1e-08
ADAM_WD = 0.01
ADAM_STEP = 10


def _cparams(sem=None, **kw):
    return pltpu.CompilerParams(dimension_semantics=sem, vmem_limit_bytes=VMEM_LIMIT_BYTES, **kw)


def _pick(dim, prefs):
    for t in prefs:
        if dim % t == 0:
            return t
    return dim


def _sigmoid(x):
    return 1.0 / (1.0 + jnp.exp(-x))


def _silu(x):
    return x * _sigmoid(x)


def _silu_grad(x):
    s = _sigmoid(x)
    return s * (1.0 + x * (1.0 - s))


def matmul(a, b, *, mode="nn", out_dtype=F32, res=None, out_blocks=None, name):
    b_blocked = b.ndim == 3
    if mode == "nn":
        m, kc = a.shape
        n = b.shape[-1] * (N_DEV if b_blocked else 1)
    elif mode == "nt":
        m, kc = a.shape
        n = b.shape[-2]
    else:
        kc, m = a.shape
        n = b.shape[-1]
    nb = b.shape[-1] if b_blocked else None
    tn = _pick(n if not out_blocks else out_blocks, (512, 256, 128))
    if b_blocked and mode == "nn":
        tn = _pick(nb, (512, 256, 128))
    k_limit = nb if (b_blocked and mode == "nt") else kc
    tm, tk = None, None
    for tm_try in (1024, 512, 256, 128):
        if m % tm_try:
            continue
        for tk_try in (k_limit, 2048, 1024, 512, 256, 128):
            if tk_try > k_limit or k_limit % tk_try:
                continue
            tiles = 2 * (tm_try * tk_try * a.dtype.itemsize + tk_try * tn * b.dtype.itemsize)
            tiles += tm_try * tn * (2 * jnp.dtype(out_dtype).itemsize + 4 + (8 if res is not None else 0))
            if tiles <= MATMUL_TILE_BYTES:
                tm, tk = tm_try, tk_try
                break
        if tm:
            break
    if tm is None:
        tm, tk = m, k_limit
    nk = kc // tk
    grid = (m // tm, n // tn, nk)

    if mode == "tn":
        a_spec = pl.BlockSpec((tk, tm), lambda i, j, k: (k, i))
        dims = (((0,), (0,)), ((), ()))
    else:
        a_spec = pl.BlockSpec((tm, tk), lambda i, j, k: (i, k))
        dims = (((1,), (0,)), ((), ())) if mode == "nn" else (((1,), (1,)), ((), ()))
    if mode == "nt":
        if b_blocked:
            per = nb // tk
            b_spec = pl.BlockSpec((None, tn, tk), lambda i, j, k: (k // per, j, k % per))
        else:
            b_spec = pl.BlockSpec((tn, tk), lambda i, j, k: (j, k))
    else:
        if b_blocked:
            per = nb // tn
            b_spec = pl.BlockSpec((None, tk, tn), lambda i, j, k: (j // per, k, j % per))
        else:
            b_spec = pl.BlockSpec((tk, tn), lambda i, j, k: (k, j))
    if out_blocks:
        per_o = out_blocks // tn
        out_shape = jax.ShapeDtypeStruct((n // out_blocks, m, out_blocks), out_dtype)
        out_spec = pl.BlockSpec((None, tm, tn), lambda i, j, k: (j // per_o, i, j % per_o))
    else:
        out_shape = jax.ShapeDtypeStruct((m, n), out_dtype)
        out_spec = pl.BlockSpec((tm, tn), lambda i, j, k: (i, j))
    in_specs = [a_spec, b_spec]
    args = [a, b]
    if res is not None:
        in_specs.append(pl.BlockSpec((tm, tn), lambda i, j, k: (i, j)))
        args.append(res)

    def body(*refs):
        a_ref, b_ref = refs[:2]
        r_ref = refs[2] if res is not None else None
        o_ref = refs[3] if res is not None else refs[2]

        def finish(r):
            if res is not None:
                r = r + r_ref[...].astype(F32)
            o_ref[...] = r.astype(out_dtype)

        part = lax.dot_general(a_ref[...].astype(BF16), b_ref[...].astype(BF16), dims, preferred_element_type=F32)
        if nk == 1:
            finish(part)
            return
        acc_ref = refs[-1]
        k = pl.program_id(2)

        @pl.when(k == 0)
        def _():
            acc_ref[...] = part

        @pl.when(k > 0)
        def _():
            acc_ref[...] += part

        @pl.when(k == nk - 1)
        def _():
            finish(acc_ref[...])

    return pl.pallas_call(
        body, out_shape=out_shape, grid=grid, in_specs=in_specs, out_specs=out_spec,
        scratch_shapes=[] if nk == 1 else [pltpu.VMEM((tm, tn), F32)], name=name,
        compiler_params=_cparams(("parallel", "parallel", "arbitrary")),
    )(*args)


def _dot(a, b, dims, precision=None):
    return lax.dot_general(a, b, (dims, ((), ())), preferred_element_type=F32, precision=precision)


_NN = ((1,), (0,))
_NT = ((1,), (1,))
_TN = ((0,), (0,))
_EXACT = lax.Precision.HIGHEST


def _chunk_decay_terms(dt, a):
    ln = dt.shape[0]
    row = lax.broadcasted_iota(jnp.int32, (ln, ln), 0)
    col = lax.broadcasted_iota(jnp.int32, (ln, ln), 1)
    tri = (row >= col).astype(F32)
    a_col = _dot(tri, dt * a, _NN, _EXACT)
    return a_col, a_col.T, row >= col


def _exact_dot(x, sel, terms):
    t = x.shape[0]
    parts, rest = [], x
    for k in range(terms):
        piece = rest.astype(BF16)
        parts.append(piece)
        if k + 1 < terms:
            rest = rest - piece.astype(F32)
    r = _dot(jnp.concatenate(parts, axis=0), sel, _NN)
    out = r[:t]
    for k in range(1, terms):
        out = out + r[k * t:(k + 1) * t]
    return out


def ssd_selectors(r_n):
    lane = jnp.arange(LANES)
    spread64 = (lane[:, None] == jnp.arange(r_n * SSD_HEAD_DIM)[None, :] // SSD_HEAD_DIM).astype(BF16)
    pair_sum = jnp.stack([lane[None, :] == 2 * q + lane[:, None] // SSD_HEAD_DIM for q in range(r_n // 2)]).astype(BF16)
    row_sum = jnp.stack([jnp.broadcast_to(lane[None, :] == r, (LANES, LANES)) for r in range(r_n)]).astype(BF16)
    return spread64, pair_sum, row_sum


def _ssd_chunk_setup(dt, a, spread64):
    ln = dt.shape[0]
    a_col, a_row, causal = _chunk_decay_terms(dt, a)
    ea = jnp.exp(a_col)
    te = jnp.exp(a_col[ln - 1:ln, :] - a_col)
    return (a_row, a_col, _exact_dot(dt, spread64, 2), _exact_dot(ea, spread64, 2), _exact_dot(te, spread64, 2),
            ea, causal)


def ssd_scan_fwd(xs, bm, cm, dtp, a_g, d_x, selectors, *, heads_per_group, name):
    s, di = xs.shape
    g_n = SSD_N_GROUPS
    r_n, p_n, n_n, ln = heads_per_group, SSD_HEAD_DIM, SSD_D_STATE, SSD_CHUNK
    nc = s // ln
    pairs, pw = r_n // 2, 2 * p_n
    spread64 = selectors[0]

    def body(xs_ref, bm_ref, cm_ref, dt_ref, a_ref, d_ref, s64_ref, y_ref, st_ref, state):
        c = pl.program_id(1)

        @pl.when(c == 0)
        def _():
            state[...] = jnp.zeros_like(state)

        a_row, a_col, dt_x, ea_x, te_x, _, causal = _ssd_chunk_setup(dt_ref[...], a_ref[...], s64_ref[...])
        bm_f = bm_ref[...]
        bmb = bm_f.astype(BF16)
        bm_t = bm_f.T.astype(BF16)
        cmb = cm_ref[...].astype(BF16)
        scores = _dot(cmb, bmb, _NT)
        first_head = lax.broadcasted_iota(jnp.int32, (1, pw), 1) < p_n
        for q in range(pairs):
            sl = slice(q * pw, (q + 1) * pw)
            x2 = xs_ref[:, sl]
            xdt2 = x2 * dt_x[:, sl]
            xdt2b = xdt2.astype(BF16)
            y_heads = []
            for r in (2 * q, 2 * q + 1):
                decay = jnp.exp(jnp.where(causal, a_col[:, r:r + 1] - a_row[r:r + 1, :], -jnp.inf))
                y_heads.append(_dot((scores * decay).astype(BF16), xdt2b, _NN))
            s2t = state[q]
            st_ref[q] = s2t
            y2 = jnp.where(first_head, y_heads[0], y_heads[1])
            y2 = y2 + ea_x[:, sl] * _dot(cmb, s2t.astype(BF16), _NN)
            y_ref[:, sl] = y2 + d_ref[:, sl] * x2
            state[q] = s2t * ea_x[ln - 1:ln, sl] + _dot(bm_t, (xdt2 * te_x[:, sl]).astype(BF16), _NN)

    whole = lambda t: pl.BlockSpec(t.shape, lambda g, c: (0,) * t.ndim)
    return pl.pallas_call(
        body,
        out_shape=(jax.ShapeDtypeStruct((s, di), F32),
                   jax.ShapeDtypeStruct((nc, g_n * pairs, n_n, pw), F32)),
        grid=(g_n, nc),
        in_specs=[pl.BlockSpec((ln, r_n * p_n), lambda g, c: (c, g)),
                  pl.BlockSpec((ln, n_n), lambda g, c: (c, g)),
                  pl.BlockSpec((ln, n_n), lambda g, c: (c, g)),
                  pl.BlockSpec((None, ln, LANES), lambda g, c: (g, c, 0)),
                  pl.BlockSpec((None, 1, LANES), lambda g, c: (g, 0, 0)),
                  pl.BlockSpec((None, 1, r_n * p_n), lambda g, c: (g, 0, 0)),
                  whole(spread64)],
        out_specs=(pl.BlockSpec((ln, r_n * p_n), lambda g, c: (c, g)),
                   pl.BlockSpec((None, pairs, n_n, pw), lambda g, c: (c, g, 0, 0))),
        scratch_shapes=[pltpu.VMEM((pairs, n_n, pw), F32)],
        name=name, compiler_params=_cparams(("parallel", "arbitrary")),
    )(xs, bm, cm, dtp, a_g, d_x, spread64)


def _row8(v):
    return jnp.broadcast_to(v, (8, v.shape[1]))


def ssd_scan_bwd(xs, bm, cm, dtp, a_g, d_x, selectors, states, dy, *, heads_per_group, name):
    s, di = xs.shape
    g_n = SSD_N_GROUPS
    r_n, p_n, n_n, ln = heads_per_group, SSD_HEAD_DIM, SSD_D_STATE, SSD_CHUNK
    nc = s // ln
    pairs, pw = r_n // 2, 2 * p_n
    spread64, pair_sum, row_sum = selectors

    def body(xs_ref, bm_ref, cm_ref, dt_ref, a_ref, d_ref, s64_ref, ps_ref, rs_ref, st_ref, dy_ref,
             dxs_ref, dbm_ref, dcm_ref, ddt_ref, dadt_ref, dd_ref, dstate, da_rows):
        c = pl.program_id(1)

        @pl.when(c == 0)
        def _():
            dstate[...] = jnp.zeros_like(dstate)
            dd_ref[...] = jnp.zeros_like(dd_ref)

        a_row, a_col, dt_x, ea_x, te_x, ea, causal = _ssd_chunk_setup(dt_ref[...], a_ref[...], s64_ref[...])
        row = lax.broadcasted_iota(jnp.int32, (ln, ln), 0)
        col = lax.broadcasted_iota(jnp.int32, (ln, ln), 1)
        causal_t = col >= row
        bmb = bm_ref[...].astype(BF16)
        cm_f = cm_ref[...]
        cmb = cm_f.astype(BF16)
        cm_t = cm_f.T.astype(BF16)
        scores = _dot(cmb, bmb, _NT)
        scores_t = _dot(bmb, cmb, _NT)
        first_head = lax.broadcasted_iota(jnp.int32, (1, pw), 1) < p_n
        e_last = ea[ln - 1:ln, :]
        da_rows[...] = jnp.zeros_like(da_rows)
        dscores = jnp.zeros((ln, ln), F32)
        dcm = jnp.zeros((ln, n_n), F32)
        dbm = jnp.zeros((ln, n_n), F32)
        da_cols = jnp.zeros((ln, LANES), F32)
        da_last = jnp.zeros((1, LANES), F32)
        ddt = jnp.zeros((ln, LANES), F32)
        dd = jnp.zeros((1, LANES), F32)
        for q in range(pairs):
            sl = slice(q * pw, (q + 1) * pw)
            sum2 = ps_ref[q]
            x2 = xs_ref[:, sl]
            dt2 = dt_x[:, sl]
            xdt2 = x2 * dt2
            xdt2b = xdt2.astype(BF16)
            dy2 = dy_ref[:, sl]
            dy2b = dy2.astype(BF16)
            dxdt_heads = []
            for h, r in enumerate((2 * q, 2 * q + 1)):
                a_r = jnp.broadcast_to(a_col[:, r:r + 1], (ln, ln))
                decay = jnp.exp(jnp.where(causal, a_r - a_row[r:r + 1, :], -jnp.inf))
                decay_t = jnp.exp(jnp.where(causal_t, a_row[r:r + 1, :] - a_r, -jnp.inf))
                dy_h = jnp.where(first_head if h == 0 else jnp.logical_not(first_head), dy2, 0.0).astype(BF16)
                dm = _dot(dy_h, xdt2b, _NT)
                dscores = dscores + dm * decay
                e_mat = dm * (scores * decay)
                da_cols = da_cols + _exact_dot(e_mat, rs_ref[r], 2)
                da_rows[r:r + 1, :] = -jnp.sum(e_mat, axis=0, keepdims=True)
                dxdt_heads.append(_dot((scores_t * decay_t).astype(BF16), dy2b, _NN))
            dxdt2 = jnp.where(first_head, dxdt_heads[0], dxdt_heads[1])
            s2t = st_ref[q]
            s2tb = s2t.astype(BF16)
            ds2t = dstate[q]
            ds2tb = ds2t.astype(BF16)
            ea2, te2 = ea_x[:, sl], te_x[:, sl]
            y_off2 = ea2 * _dot(cmb, s2tb, _NN)
            dy_e2 = (dy2 * ea2).astype(BF16)
            dcm = dcm + _dot(dy_e2, s2tb, _NT)
            ds_in = _dot(cm_t, dy_e2, _NN)
            da_cols = da_cols + _exact_dot(dy2 * y_off2, sum2, 2)
            bds2 = _dot(bmb, ds2tb, _NN)
            dxdt2 = dxdt2 + te2 * bds2
            xdt_e2 = xdt2 * te2
            dbm = dbm + _dot(xdt_e2.astype(BF16), ds2tb, _NT)
            w_cols = _exact_dot(xdt_e2 * bds2, sum2, 2)
            da_cols = da_cols - w_cols
            state_dot = _exact_dot(_row8(jnp.sum(ds2t * s2t, axis=0, keepdims=True)), sum2, 2)[0:1]
            da_last = da_last + jnp.sum(w_cols, axis=0, keepdims=True) + e_last * state_dot
            dstate[q] = ds2t * ea_x[ln - 1:ln, sl] + ds_in
            dxs_ref[:, sl] = dxdt2 * dt2 + d_ref[:, sl] * dy2
            ddt = ddt + _exact_dot(dxdt2 * x2, sum2, 2)
            dd = dd + _exact_dot(_row8(jnp.sum(dy2 * x2, axis=0, keepdims=True)), sum2, 2)[0:1]
        dcm_ref[...] = dcm + _dot(dscores.astype(BF16), bmb, _NN)
        dbm_ref[...] = dbm + _dot(dscores.T.astype(BF16), cmb, _NN)
        da_total = da_cols + da_rows[...].T
        upper = causal_t.astype(F32)
        dadt_ref[...] = _dot(upper, da_total, _NN, _EXACT) + da_last
        ddt_ref[...] = ddt
        dd_ref[...] += dd

    last_c = nc - 1
    whole = lambda t: pl.BlockSpec(t.shape, lambda g, c: (0,) * t.ndim)
    return pl.pallas_call(
        body,
        out_shape=(jax.ShapeDtypeStruct((s, di), F32),
                   jax.ShapeDtypeStruct(bm.shape, F32),
                   jax.ShapeDtypeStruct(cm.shape, F32),
                   jax.ShapeDtypeStruct(dtp.shape, F32),
                   jax.ShapeDtypeStruct(dtp.shape, F32),
                   jax.ShapeDtypeStruct(a_g.shape, F32)),
        grid=(g_n, nc),
        in_specs=[pl.BlockSpec((ln, r_n * p_n), lambda g, c: (last_c - c, g)),
                  pl.BlockSpec((ln, n_n), lambda g, c: (last_c - c, g)),
                  pl.BlockSpec((ln, n_n), lambda g, c: (last_c - c, g)),
                  pl.BlockSpec((None, ln, LANES), lambda g, c: (g, last_c - c, 0)),
                  pl.BlockSpec((None, 1, LANES), lambda g, c: (g, 0, 0)),
                  pl.BlockSpec((None, 1, r_n * p_n), lambda g, c: (g, 0, 0)),
                  whole(spread64), whole(pair_sum), whole(row_sum),
                  pl.BlockSpec((None, pairs, n_n, pw), lambda g, c: (last_c - c, g, 0, 0)),
                  pl.BlockSpec((ln, r_n * p_n), lambda g, c: (last_c - c, g))],
        out_specs=(pl.BlockSpec((ln, r_n * p_n), lambda g, c: (last_c - c, g)),
                   pl.BlockSpec((ln, n_n), lambda g, c: (last_c - c, g)),
                   pl.BlockSpec((ln, n_n), lambda g, c: (last_c - c, g)),
                   pl.BlockSpec((None, ln, LANES), lambda g, c: (g, last_c - c, 0)),
                   pl.BlockSpec((None, ln, LANES), lambda g, c: (g, last_c - c, 0)),
                   pl.BlockSpec((None, 1, LANES), lambda g, c: (g, 0, 0))),
        scratch_shapes=[pltpu.VMEM((pairs, n_n, pw), F32), pltpu.VMEM((LANES, ln), F32)],
        name=name, compiler_params=_cparams(("parallel", "arbitrary")),
    )(xs, bm, cm, dtp, a_g, d_x, spread64, pair_sum, row_sum, states, dy)


SB_TILE = 256


def _tri_sum(x, tri):
    t = x.shape[0]
    hi = x.astype(BF16)
    r1 = x - hi.astype(F32)
    mid = r1.astype(BF16)
    lo = (r1 - mid.astype(F32)).astype(BF16)
    r = _dot(jnp.concatenate([hi, mid, lo], axis=0), tri, _NN)
    return r[:t] + r[t:2 * t] + r[2 * t:]


def _sb_logits(q, k_j, scale, strict):
    z = _dot(q, k_j, _NT) * scale
    sp = jnp.log(1.0 + jnp.exp(-jnp.abs(z)))
    log_b = jnp.minimum(z, 0.0) - sp
    log_1mb = log_b - z
    if strict is not None:
        log_1mb = jnp.where(strict, log_1mb, 0.0)
    return log_b, log_1mb


def _sb_tile(s):
    return _pick(s, (SB_TILE, LANES))


def _sb_iotas(t):
    row = lax.broadcasted_iota(jnp.int32, (t, t), 0)
    col = lax.broadcasted_iota(jnp.int32, (t, t), 1)
    return row, col


def sb_attn_fwd(qn, kn, v, *, v_off=0, name):
    s, w = qn.shape
    dh = SB_HEAD_DIM
    n_h = w // dh
    t = _sb_tile(s)
    scale = 1.0 / math.sqrt(dh)

    def body(q_ref, k_ref, v_ref, o_ref, tot_ref):
        i = pl.program_id(1)
        q = q_ref[...]
        row, col = _sb_iotas(t)
        later = (row > col).astype(BF16)

        def tile(j, acc, run, strict):
            s0 = pl.multiple_of(j * t, t)
            k_j = k_ref[pl.ds(s0, t), :]
            v_j = v_ref[pl.ds(s0, t), :].astype(BF16)
            log_b, log_1mb = _sb_logits(q, k_j, scale, strict)
            att = jnp.exp(log_b + (_tri_sum(log_1mb, later) + run))
            if strict is not None:
                att = jnp.where(strict, att, 0.0)
            acc = acc + _dot(att.astype(BF16), v_j, _NN)
            return acc, run + jnp.sum(log_1mb, axis=1, keepdims=True)

        acc, run = tile(i, jnp.zeros((t, dh), F32), jnp.zeros((t, 1), F32), col < row)
        acc, run = lax.fori_loop(0, i, lambda jj, c: tile(i - 1 - jj, c[0], c[1], None), (acc, run))
        o_ref[...] = acc
        tot_ref[...] = jnp.broadcast_to(run, (t, dh))

    return pl.pallas_call(
        body,
        out_shape=(jax.ShapeDtypeStruct((s, w), F32), jax.ShapeDtypeStruct((s, w), F32)),
        grid=(n_h, s // t),
        in_specs=[pl.BlockSpec((t, dh), lambda h, i: (i, h)),
                  pl.BlockSpec((s, dh), lambda h, i: (0, h)),
                  pl.BlockSpec((s, dh), lambda h, i: (0, v_off + h))],
        out_specs=(pl.BlockSpec((t, dh), lambda h, i: (i, h)),
                   pl.BlockSpec((t, dh), lambda h, i: (i, h))),
        name=name, compiler_params=_cparams(("parallel", "parallel")),
    )(qn, kn, v)


def sb_attn_bwd(qn, kn, v, tot, do, *, v_off=0, name):
    s, w = qn.shape
    dh = SB_HEAD_DIM
    n_h = w // dh
    t = _sb_tile(s)
    scale = 1.0 / math.sqrt(dh)

    def body(q_ref, k_ref, v_ref, tot_ref, do_ref, dq_ref, dk_ref, dv_ref):
        dk_ref[...] = jnp.zeros_like(dk_ref)
        dv_ref[...] = jnp.zeros_like(dv_ref)
        row, col = _sb_iotas(t)
        upto = (row <= col).astype(BF16)
        before = (row < col).astype(BF16)

        def q_block(i, _):
            t0 = pl.multiple_of(i * t, t)
            q = q_ref[pl.ds(t0, t), :]
            do_i = do_ref[pl.ds(t0, t), :].astype(BF16)
            total = tot_ref[pl.ds(t0, t), 0:1]

            def tile(j, dq, run_l, run_g, strict):
                s0 = pl.multiple_of(j * t, t)
                k_j = k_ref[pl.ds(s0, t), :]
                v_j = v_ref[pl.ds(s0, t), :].astype(BF16)
                log_b, log_1mb = _sb_logits(q, k_j, scale, strict)
                att = jnp.exp(log_b + ((total - run_l) - _tri_sum(log_1mb, upto)))
                if strict is not None:
                    att = jnp.where(strict, att, 0.0)
                g = att * _dot(do_i, v_j, _NT)
                c = _tri_sum(g, before) + run_g
                dz = (g - (g + c) * jnp.exp(log_b)) * scale
                if strict is not None:
                    dz = jnp.where(strict, dz, 0.0)
                dz = dz.astype(BF16)
                dq = dq + _dot(dz, k_j, _NN)
                dk_ref[pl.ds(s0, t), :] += _dot(dz, q, _TN)
                dv_ref[pl.ds(s0, t), :] += _dot(att.astype(BF16), do_i, _TN)
                return (dq, run_l + jnp.sum(log_1mb, axis=1, keepdims=True),
                        run_g + jnp.sum(g, axis=1, keepdims=True))

            zero = jnp.zeros((t, 1), F32)
            carry = lax.fori_loop(0, i, lambda j, c: tile(j, c[0], c[1], c[2], None),
                                  (jnp.zeros((t, dh), F32), zero, zero))
            dq, _, _ = tile(i, carry[0], carry[1], carry[2], col < row)
            dq_ref[pl.ds(t0, t), :] = dq
            return 0

        lax.fori_loop(0, s // t, q_block, 0)

    head = pl.BlockSpec((s, dh), lambda h: (0, h))
    return pl.pallas_call(
        body,
        out_shape=tuple(jax.ShapeDtypeStruct((s, w), F32) for _ in range(3)),
        grid=(n_h,),
        in_specs=[head, head, pl.BlockSpec((s, dh), lambda h: (0, v_off + h)), head, head],
        out_specs=(head, head, head),
        name=name, compiler_params=_cparams(("parallel",)),
    )(qn, kn, v, tot, do)


ROW_TILE = 256
WIDE_ROW_TILE = 64


def _rows(width, col=0, tm=ROW_TILE):
    return pl.BlockSpec((tm, width), lambda i: (i, col))


_wide_rows = functools.partial(_rows, tm=WIDE_ROW_TILE)


def _whole(shape):
    return pl.BlockSpec(shape, lambda i: (0,) * len(shape))


def _ew_call(body, out_shape, in_specs, out_specs, args, n_rows, name, carried=False):
    return pl.pallas_call(
        body, out_shape=out_shape, grid=(n_rows // in_specs[0].block_shape[0],), in_specs=in_specs, out_specs=out_specs,
        name=name, compiler_params=_cparams(("arbitrary",) if carried else ("parallel",)),
    )(*args)


def _first_step(*refs):
    @pl.when(pl.program_id(0) == 0)
    def _():
        for r in refs:
            r[...] = jnp.zeros_like(r)


def rmsnorm_fwd(x, w, after=None, *, name):
    s, d = x.shape

    def body(x_ref, w_ref, *rest):
        o_ref = rest[-1]
        xv = x_ref[...]
        r = lax.rsqrt(jnp.mean(xv * xv, axis=-1, keepdims=True) + NORM_EPS)
        o_ref[...] = (xv * r * w_ref[...]).astype(BF16)

    extra = [] if after is None else [after]
    return _ew_call(body, jax.ShapeDtypeStruct((s, d), BF16),
                    [_rows(d), _whole((1, d))] + [_whole(TOKEN_SHAPE)] * len(extra), _rows(d),
                    (x, w.reshape(1, d), *extra), s, name)


def rmsnorm_bwd(x, w, dy, dres, *, name):
    s, d = x.shape

    def body(x_ref, w_ref, dy_ref, dr_ref, dx_ref, dw_ref):
        _first_step(dw_ref)
        xv = x_ref[...]
        r = lax.rsqrt(jnp.mean(xv * xv, axis=-1, keepdims=True) + NORM_EPS)
        xhat = xv * r
        dyv = dy_ref[...].astype(F32)
        dw_ref[...] += jnp.sum(dyv * xhat, axis=0, keepdims=True)
        g = dyv * w_ref[...]
        dx_ref[...] = dr_ref[...] + r * (g - xhat * jnp.mean(g * xhat, axis=-1, keepdims=True))

    return _ew_call(body, (jax.ShapeDtypeStruct((s, d), F32), jax.ShapeDtypeStruct((1, d), F32)),
                    [_rows(d), _whole((1, d)), _rows(d), _rows(d)], (_rows(d), _whole((1, d))),
                    (x, w.reshape(1, d), dy, dres), s, name, carried=True)


def ple_fwd(h1, gate_pre, pp, *, name):
    s, d = h1.shape

    def body(h_ref, g_ref, p_ref, o_ref):
        o_ref[...] = h_ref[...] + p_ref[...] * _sigmoid(g_ref[...])

    return _ew_call(body, jax.ShapeDtypeStruct((s, d), F32), [_rows(d)] * 3, _rows(d), (h1, gate_pre, pp), s, name)


def ple_bwd(dh2, gate_pre, pp, after, *, name):
    s, d = dh2.shape

    def body(dh_ref, g_ref, p_ref, after_ref, dp_ref, dg_ref):
        gate = _sigmoid(g_ref[...])
        dh = dh_ref[...]
        dp_ref[...] = (dh * gate).astype(BF16)
        dg_ref[...] = (dh * p_ref[...] * gate * (1.0 - gate)).astype(BF16)

    shp = jax.ShapeDtypeStruct((s, d), BF16)
    return _ew_call(body, (shp, shp), [_rows(d)] * 3 + [_whole(TOKEN_SHAPE)], (_rows(d), _rows(d)),
                    (dh2, gate_pre, pp, after), s, name)


def loss_head(y, target, *, name):
    s, d = y.shape

    def body(y_ref, t_ref, l_ref, dy_ref):
        _first_step(l_ref)
        err = y_ref[...] - t_ref[...]
        per_tok = jnp.mean(err * err, axis=-1, keepdims=True)
        l_ref[...] += 0.5 * jnp.sum(per_tok, axis=0, keepdims=True)
        dy_ref[...] = err * (1.0 / d)

    return _ew_call(body, (jax.ShapeDtypeStruct((1, 1), F32), jax.ShapeDtypeStruct((s, d), F32)),
                    [_rows(d), _rows(d)], (_whole((1, 1)), _rows(d)), (y, target), s, name, carried=True)


CONV_COL_TILE = 256


def _conv_taps(x, w_ref):
    row = lax.broadcasted_iota(jnp.int32, (x.shape[0], 1), 0)
    acc = x * w_ref[SSD_D_CONV - 1:SSD_D_CONV, :]
    shifted = []
    for d in range(1, SSD_D_CONV):
        xs = jnp.where(row >= d, pltpu.roll(x, d, 0), 0.0)
        shifted.append(xs)
        acc = acc + xs * w_ref[SSD_D_CONV - 1 - d:SSD_D_CONV - d, :]
    return acc, shifted


def ssd_conv_fwd(x, w, b, *, name):
    s, c = x.shape
    tc = _pick(c, (CONV_COL_TILE, LANES))

    def body(x_ref, w_ref, b_ref, o_ref):
        pre, _ = _conv_taps(x_ref[...], w_ref)
        o_ref[...] = _silu(pre + b_ref[...])

    col = pl.BlockSpec((s, tc), lambda j: (0, j))
    return pl.pallas_call(
        body, out_shape=jax.ShapeDtypeStruct((s, c), F32), grid=(c // tc,),
        in_specs=[col, pl.BlockSpec((SSD_D_CONV, tc), lambda j: (0, j)), pl.BlockSpec((1, tc), lambda j: (0, j))],
        out_specs=col, name=name, compiler_params=_cparams(("parallel",)),
    )(x, w, b)


def ssd_conv_bwd(x, w, b, dact, *, name):
    s, c = x.shape
    tc = _pick(c, (CONV_COL_TILE, LANES))

    def body(x_ref, w_ref, b_ref, da_ref, dx_ref, dw_ref, db_ref):
        xv = x_ref[...]
        pre, shifted = _conv_taps(xv, w_ref)
        dpre = da_ref[...] * _silu_grad(pre + b_ref[...])
        db_ref[...] = jnp.sum(dpre, axis=0, keepdims=True)
        row = lax.broadcasted_iota(jnp.int32, (s, 1), 0)
        dx = dpre * w_ref[SSD_D_CONV - 1:SSD_D_CONV, :]
        dw_ref[SSD_D_CONV - 1:SSD_D_CONV, :] = jnp.sum(dpre * xv, axis=0, keepdims=True)
        for d in range(1, SSD_D_CONV):
            k = SSD_D_CONV - 1 - d
            dw_ref[k:k + 1, :] = jnp.sum(dpre * shifted[d - 1], axis=0, keepdims=True)
            up = jnp.where(row < s - d, pltpu.roll(dpre, s - d, 0), 0.0)
            dx = dx + up * w_ref[k:k + 1, :]
        dx_ref[...] = dx.astype(BF16)

    col = pl.BlockSpec((s, tc), lambda j: (0, j))
    wspec = pl.BlockSpec((SSD_D_CONV, tc), lambda j: (0, j))
    bspec = pl.BlockSpec((1, tc), lambda j: (0, j))
    return pl.pallas_call(
        body,
        out_shape=(jax.ShapeDtypeStruct((s, c), BF16), jax.ShapeDtypeStruct((SSD_D_CONV, c), F32),
                   jax.ShapeDtypeStruct((1, c), F32)),
        grid=(c // tc,), in_specs=[col, wspec, bspec, col], out_specs=(col, wspec, bspec),
        name=name, compiler_params=_cparams(("parallel",)),
    )(x, w, b, dact)


def ssd_dt_fwd(dt_raw, bias, a_log, *, name):
    s, h = dt_raw.shape

    def body(r_ref, b_ref, al_ref, dt_ref, a_ref):
        zv = r_ref[...] + b_ref[...]
        dt_ref[...] = jnp.maximum(zv, 0.0) + jnp.log(1.0 + jnp.exp(-jnp.abs(zv)))
        a_ref[...] = -jnp.exp(al_ref[...])

    full = pl.BlockSpec((s, h), lambda: (0, 0))
    vec = pl.BlockSpec((1, h), lambda: (0, 0))
    return pl.pallas_call(
        body, out_shape=(jax.ShapeDtypeStruct((s, h), F32), jax.ShapeDtypeStruct((1, h), F32)),
        in_specs=[full, vec, vec], out_specs=(full, vec), name=name, compiler_params=_cparams(),
    )(dt_raw, bias.reshape(1, h), a_log.reshape(1, h))


def ssd_dt_bwd(dt_raw, bias, a_log, dt, ddt, dadt, *, name):
    s, h = dt_raw.shape

    def body(r_ref, b_ref, al_ref, dt_ref, ddt_ref, dadt_ref, dr_ref, db_ref, dal_ref):
        a = -jnp.exp(al_ref[...])
        dadt_v = dadt_ref[...]
        d_dt = ddt_ref[...] + a * dadt_v
        d_raw = d_dt * _sigmoid(r_ref[...] + b_ref[...])
        dr_ref[...] = d_raw
        db_ref[...] = jnp.sum(d_raw, axis=0, keepdims=True)
        dal_ref[...] = jnp.sum(dadt_v * dt_ref[...], axis=0, keepdims=True) * a

    full = pl.BlockSpec((s, h), lambda: (0, 0))
    vec = pl.BlockSpec((1, h), lambda: (0, 0))
    return pl.pallas_call(
        body, out_shape=(jax.ShapeDtypeStruct((s, h), F32), jax.ShapeDtypeStruct((1, h), F32),
                         jax.ShapeDtypeStruct((1, h), F32)),
        in_specs=[full, vec, vec, full, full, full], out_specs=(full, vec, vec), name=name,
        compiler_params=_cparams(),
    )(dt_raw, bias.reshape(1, h), a_log.reshape(1, h), dt, ddt, dadt)


def _group_mean(v, n_groups):
    gw = v.shape[-1] // n_groups
    parts = [jnp.broadcast_to(jnp.mean(v[:, k * gw:(k + 1) * gw], axis=-1, keepdims=True), (v.shape[0], gw))
             for k in range(n_groups)]
    return jnp.concatenate(parts, axis=-1)


def ssd_gate_fwd(y, z, gw, *, name):
    s, di = y.shape

    def body(y_ref, z_ref, w_ref, o_ref):
        yg = y_ref[...] * _silu(z_ref[...])
        r = lax.rsqrt(_group_mean(yg * yg, SSD_N_GROUPS) + GATED_NORM_EPS)
        o_ref[...] = (yg * r * w_ref[...]).astype(BF16)

    return _ew_call(body, jax.ShapeDtypeStruct((s, di), BF16), [_wide_rows(di), _wide_rows(di), _whole((1, di))],
                    _wide_rows(di), (y, z, gw.reshape(1, di)), s, name)


def ssd_gate_bwd(y, z, gw, dyn, *, name):
    s, di = y.shape

    def body(y_ref, z_ref, w_ref, dn_ref, dy_ref, dz_ref, dw_ref):
        _first_step(dw_ref)
        yv, zv = y_ref[...], z_ref[...]
        sz = _silu(zv)
        yg = yv * sz
        r = lax.rsqrt(_group_mean(yg * yg, SSD_N_GROUPS) + GATED_NORM_EPS)
        yhat = yg * r
        dn = dn_ref[...]
        dw_ref[...] += jnp.sum(dn * yhat, axis=0, keepdims=True)
        g = dn * w_ref[...]
        dyg = r * (g - yhat * _group_mean(g * yhat, SSD_N_GROUPS))
        dy_ref[...] = dyg * sz
        dz_ref[...] = (dyg * yv * _silu_grad(zv)).astype(BF16)

    return _ew_call(body, (jax.ShapeDtypeStruct((s, di), F32), jax.ShapeDtypeStruct((s, di), BF16),
                           jax.ShapeDtypeStruct((1, di), F32)),
                    [_wide_rows(di), _wide_rows(di), _whole((1, di)), _wide_rows(di)],
                    (_wide_rows(di), _wide_rows(di), _whole((1, di))),
                    (y, z, gw.reshape(1, di), dyn), s, name, carried=True)


def _head_mean(v):
    return _group_mean(v, v.shape[-1] // SB_HEAD_DIM)


def sb_qk_fwd(proj, qw, kw, *, name):
    s, w4 = proj.shape
    w = w4 // 4
    reps = w // SB_HEAD_DIM

    def body(q_ref, k_ref, qw_ref, kw_ref, qn_ref, kn_ref):
        for x_ref, w_ref, o_ref in ((q_ref, qw_ref, qn_ref), (k_ref, kw_ref, kn_ref)):
            xv = x_ref[...]
            r = lax.rsqrt(_head_mean(xv * xv) + NORM_EPS)
            o_ref[...] = (xv * r * jnp.tile(w_ref[...], (1, reps))).astype(BF16)

    shp = jax.ShapeDtypeStruct((s, w), BF16)
    return _ew_call(body, (shp, shp), [_rows(w, 0), _rows(w, 1), _whole((1, SB_HEAD_DIM)), _whole((1, SB_HEAD_DIM))],
                    (_rows(w), _rows(w)), (proj, proj, qw.reshape(1, -1), kw.reshape(1, -1)), s, name)


def sb_gate_fwd(o, proj, *, name):
    s, w = o.shape

    def body(o_ref, g_ref, og_ref):
        og_ref[...] = (o_ref[...] * _silu(g_ref[...])).astype(BF16)

    return _ew_call(body, jax.ShapeDtypeStruct((s, w), BF16), [_rows(w), _rows(w, 3)], _rows(w), (o, proj), s, name)


def sb_gate_bwd(dog, o, proj, *, name):
    s, w = o.shape

    def body(d_ref, o_ref, g_ref, do_ref, dg_ref):
        gv, dv = g_ref[...], d_ref[...]
        do_ref[...] = dv * _silu(gv)
        dg_ref[...] = (dv * o_ref[...] * _silu_grad(gv)).astype(BF16)

    return _ew_call(body, (jax.ShapeDtypeStruct((s, w), F32), jax.ShapeDtypeStruct((s, w), BF16)),
                    [_rows(w), _rows(w), _rows(w, 3)], (_rows(w), _rows(w)), (dog, o, proj), s, name)


def sb_pack_bwd(proj, qw, kw, dqn, dkn, dv, dg, *, name):
    s, w4 = proj.shape
    w = w4 // 4
    reps = w // SB_HEAD_DIM

    def body(q_ref, k_ref, qw_ref, kw_ref, dqn_ref, dkn_ref, dv_ref, dg_ref, dp_ref, dqw_ref, dkw_ref):
        _first_step(dqw_ref, dkw_ref)
        for idx, (x_ref, w_ref, d_ref, dw_ref) in enumerate(((q_ref, qw_ref, dqn_ref, dqw_ref),
                                                           (k_ref, kw_ref, dkn_ref, dkw_ref))):
            xv = x_ref[...]
            r = lax.rsqrt(_head_mean(xv * xv) + NORM_EPS)
            xhat = xv * r
            dn = d_ref[...]
            per_col = jnp.sum(dn * xhat, axis=0, keepdims=True)
            acc = per_col[:, 0:SB_HEAD_DIM]
            for hh in range(1, reps):
                acc = acc + per_col[:, hh * SB_HEAD_DIM:(hh + 1) * SB_HEAD_DIM]
            dw_ref[...] += acc
            g = dn * jnp.tile(w_ref[...], (1, reps))
            dp_ref[:, idx * w:(idx + 1) * w] = (r * (g - xhat * _head_mean(g * xhat))).astype(BF16)
        dp_ref[:, 2 * w:3 * w] = dv_ref[...].astype(BF16)
        dp_ref[:, 3 * w:4 * w] = dg_ref[...]

    vec = _whole((1, SB_HEAD_DIM))
    return _ew_call(body, (jax.ShapeDtypeStruct((s, w4), BF16), jax.ShapeDtypeStruct((1, SB_HEAD_DIM), F32),
                           jax.ShapeDtypeStruct((1, SB_HEAD_DIM), F32)),
                    [_wide_rows(w, 0), _wide_rows(w, 1), vec, vec, _wide_rows(w), _wide_rows(w), _wide_rows(w),
                     _wide_rows(w)],
                    (_wide_rows(w4), vec, vec),
                    (proj, proj, qw.reshape(1, -1), kw.reshape(1, -1), dqn, dkn, dv, dg), s, name, carried=True)


_HBM = pl.BlockSpec(memory_space=pltpu.HBM)


def _mesh_pos():
    return lax.axis_index("x"), lax.axis_index("y"), lax.axis_index("c")


def _other_chips(x, y):
    return [(1 - x, y), (x, 1 - y), (1 - x, 1 - y)]


_SEM = pl.BlockSpec(memory_space=pltpu.SEMAPHORE)
_ANY = pl.BlockSpec(memory_space=pl.ANY)
_DATAFLOW = pltpu.SideEffectType.DATAFLOW_SIDE_EFFECTING
N_PEER_CHIPS = N_CHIP - 1
TOKEN_SHAPE = (8, LANES)


def _in_hbm(t):
    return pltpu.with_memory_space_constraint(t, pltpu.HBM)


def _ici_copies(kind, src_refs, land_refs, send_sems, recv_sems, arrivals=False):
    x, y, c = _mesh_pos()
    out = []
    for a in range(len(land_refs)):
        if kind == "pass":
            out.append(pltpu.make_async_remote_copy(
                src_ref=land_refs[a].at[:, c], dst_ref=land_refs[a].at[:, 1 - c if arrivals else c],
                send_sem=send_sems.at[a], recv_sem=recv_sems.at[a], device_id=(x, y, 1 - c), device_id_type=MESH))
            continue
        for j, chip in enumerate(_other_chips(x, y)):
            if kind == "gather":
                src = land_refs[a].at[4 * x + 2 * y + c]
                dst = land_refs[a].at[4 * chip[0] + 2 * chip[1] + c] if arrivals else src
            else:
                src, dst = src_refs[a].at[2 * chip[0] + chip[1]], land_refs[a].at[j]
            k = a * N_PEER_CHIPS + j
            out.append(pltpu.make_async_remote_copy(
                src_ref=src, dst_ref=dst, send_sem=send_sems.at[k], recv_sem=recv_sems.at[k],
                device_id=(*chip, c), device_id_type=MESH))
    return out


def _n_copies(kind, lands):
    return len(lands) * (1 if kind == "pass" else N_PEER_CHIPS)


def ici_start(kind, srcs, lands, after=(), *, name):
    ns, nb = len(srcs), len(srcs) + len(lands)
    n_sem = _n_copies(kind, lands)

    def body(*refs):
        first_out = nb + len(after)
        for cp in _ici_copies(kind, refs[:ns], refs[ns:nb], refs[first_out], refs[first_out + 1]):
            cp.start()
        refs[-1][...] = jnp.zeros(TOKEN_SHAPE, F32)

    outs = pl.pallas_call(
        body, name=name,
        out_shape=(pltpu.SemaphoreType.DMA((n_sem,)), pltpu.SemaphoreType.DMA((n_sem,)),
                   *[pltpu.HBM(t.shape, t.dtype) for t in (*srcs, *lands)], jax.ShapeDtypeStruct(TOKEN_SHAPE, F32)),
        in_specs=[_HBM] * nb + [_ANY] * len(after),
        out_specs=(_SEM, _SEM, *([_HBM] * nb), pl.BlockSpec(memory_space=pltpu.VMEM)),
        input_output_aliases={k: 2 + k for k in range(nb)},
        compiler_params=pltpu.CompilerParams(has_side_effects=_DATAFLOW),
    )(*[_in_hbm(t) for t in (*srcs, *lands)], *after)
    return outs[0], outs[1], list(outs[2:2 + ns]), list(outs[2 + ns:2 + nb]), outs[-1]


def ici_wait(kind, started, after, *, name):
    send_sems, recv_sems, srcs, lands, _ = started
    ns, nb = len(srcs), len(srcs) + len(lands)

    def body(*refs):
        for cp in _ici_copies(kind, refs[:ns], refs[ns:nb], refs[nb], refs[nb + 1]):
            cp.wait_send()
        for cp in _ici_copies(kind, refs[:ns], refs[ns:nb], refs[nb], refs[nb + 1], arrivals=True):
            cp.wait_recv()

    outs = pl.pallas_call(
        body, name=name,
        out_shape=tuple(pltpu.HBM(t.shape, t.dtype) for t in (*srcs, *lands)),
        in_specs=[_HBM] * nb + [_SEM, _SEM] + [_ANY] * len(after),
        out_specs=tuple([_HBM] * nb),
        input_output_aliases={k: k for k in range(nb)},
        compiler_params=pltpu.CompilerParams(has_side_effects=_DATAFLOW),
    )(*srcs, *lands, send_sems, recv_sems, *after)
    return list(outs[:ns]), list(outs[ns:])


def sibling_exchange(grads, *, name):
    n = len(grads)

    def body(*refs):
        g_refs, r_refs = refs[:n], refs[n:2 * n]
        send_sems, recv_sems = refs[2 * n:]
        x, y, c = _mesh_pos()
        copies = [pltpu.make_async_remote_copy(
            src_ref=g_refs[a].at[:, 1 - c], dst_ref=r_refs[a], send_sem=send_sems.at[a],
            recv_sem=recv_sems.at[a], device_id=(x, y, 1 - c), device_id_type=MESH) for a in range(n)]
        for cp in copies:
            cp.start()
        for cp in copies:
            cp.wait()

    return pl.pallas_call(
        body, out_shape=tuple(jax.ShapeDtypeStruct((N_CHIP,) + g.shape[2:], g.dtype) for g in grads),
        in_specs=[_HBM] * n, out_specs=tuple([_HBM] * n),
        scratch_shapes=[pltpu.SemaphoreType.DMA((n,)), pltpu.SemaphoreType.DMA((n,))],
        name=name,
    )(*grads)


def all_reduce_small(v, *, name):
    r = v.shape[0]

    def body(v_ref, o_ref, buf, send_sems, recv_sems):
        x, y, c = _mesh_pos()
        me = 4 * x + 2 * y + c
        buf[me] = v_ref[...]
        copies = []
        for k in range(1, N_DEV):
            to = ((x + (k >> 2)) % 2, (y + ((k >> 1) & 1)) % 2, (c + (k & 1)) % 2)
            copies.append(pltpu.make_async_remote_copy(
                src_ref=v_ref, dst_ref=buf.at[me], send_sem=send_sems.at[k - 1], recv_sem=recv_sems.at[k - 1],
                device_id=to, device_id_type=MESH))
        for cp in copies:
            cp.start()
        for cp in copies:
            cp.wait()
        acc = buf[0]
        for d in range(1, N_DEV):
            acc = acc + buf[d]
        o_ref[...] = acc

    vm = pl.BlockSpec(memory_space=pltpu.VMEM)
    return pl.pallas_call(
        body, out_shape=jax.ShapeDtypeStruct(v.shape, F32), in_specs=[vm], out_specs=vm,
        scratch_shapes=[pltpu.VMEM((N_DEV, r, LANES), F32), pltpu.SemaphoreType.DMA((N_DEV - 1,)),
                        pltpu.SemaphoreType.DMA((N_DEV - 1,))],
        name=name,
    )(v)


def pair_add(g, r1, core, *, name):
    _, _, rows, cols = g.shape
    tm = _pick(rows, (256, 128))

    def body(c_ref, g_ref, r_ref, o_ref):
        o_ref[...] = (g_ref[...].astype(F32) + r_ref[...].astype(F32)).astype(o_ref.dtype)

    return pl.pallas_call(
        body, out_shape=jax.ShapeDtypeStruct(r1.shape, g.dtype),
        grid_spec=pltpu.PrefetchScalarGridSpec(
            num_scalar_prefetch=1, grid=(N_CHIP, rows // tm),
            in_specs=[pl.BlockSpec((None, None, tm, cols), lambda k, i, c_ref: (k, c_ref[0], i, 0)),
                      pl.BlockSpec((None, tm, cols), lambda k, i, c_ref: (k, i, 0))],
            out_specs=pl.BlockSpec((None, tm, cols), lambda k, i, c_ref: (k, i, 0))),
        name=name, compiler_params=_cparams(("parallel", "parallel")),
    )(core, g, r1)


def _adamw_math(w, g, m, v):
    m = ADAM_B1 * m + (1.0 - ADAM_B1) * g
    v = ADAM_B2 * v + (1.0 - ADAM_B2) * (g * g)
    m_hat = m / (1.0 - ADAM_B1 ** ADAM_STEP)
    v_hat = v / (1.0 - ADAM_B2 ** ADAM_STEP)
    delta = -ADAM_LR * (m_hat / (jnp.sqrt(v_hat) + ADAM_EPS) + ADAM_WD * w)
    return delta, m, v


def adamw_sharded(w, m, v, layer, chip_sums, received, chip, into, *, name):
    _, rows, cols = w.shape
    tm = _pick(rows, (256, 128))

    def body(k_ref, w_ref, m_ref, v_ref, t_ref, r_ref, *rest):
        g_ref, d_ref, nm_ref, nv_ref = rest[-4:]
        g = t_ref[...].astype(F32)
        for j in range(N_CHIP - 1):
            g = g + r_ref[j].astype(F32)
        d, mm, vv = _adamw_math(w_ref[...], g, m_ref[...], v_ref[...])
        g_ref[...] = g
        d_ref[...] = d
        nm_ref[...] = mm
        nv_ref[...] = vv

    blk = pl.BlockSpec((None, tm, cols), lambda i, k_ref: (layer, i, 0))
    shp = jax.ShapeDtypeStruct(w.shape, F32)
    in_specs = [blk, blk, blk,
                pl.BlockSpec((None, tm, cols), lambda i, k_ref: (k_ref[0], i, 0)),
                pl.BlockSpec((N_CHIP - 1, tm, cols), lambda i, k_ref: (0, i, 0))]
    operands = [chip, w, m, v, chip_sums, received]
    aliases = {}
    if into is not None:
        aliases = {len(operands) + q: q for q in range(4)}
        in_specs += [_ANY] * 4
        operands += list(into)
    return pl.pallas_call(
        body, out_shape=(shp, shp, shp, shp),
        grid_spec=pltpu.PrefetchScalarGridSpec(
            num_scalar_prefetch=1, grid=(rows // tm,), in_specs=in_specs, out_specs=(blk, blk, blk, blk)),
        input_output_aliases=aliases,
        name=name, compiler_params=_cparams(("parallel",)),
    )(*operands)


def adamw_replicated(w, m, v, g, *, name):
    def body(w_ref, m_ref, v_ref, g_ref, d_ref, nm_ref, nv_ref):
        d, mm, vv = _adamw_math(w_ref[...], g_ref[...], m_ref[...], v_ref[...])
        d_ref[...] = d
        nm_ref[...] = mm
        nv_ref[...] = vv

    shp = jax.ShapeDtypeStruct(w.shape, F32)
    return pl.pallas_call(body, out_shape=(shp, shp, shp), name=name, compiler_params=_cparams())(w, m, v, g)


WEIGHT_NAMES = ("norm_w", "ssd_in_w", "ssd_conv_w", "ssd_conv_b", "ssd_dt_bias", "ssd_a_log", "ssd_d",
                "ssd_gnorm_w", "ssd_out_w", "sb_in_w", "sb_qn_w", "sb_kn_w", "sb_out_w", "ple_norm_w",
                "ple_gate_w", "ple_proj_w")
REPLICATED = ("norm_w", "ssd_conv_b", "ssd_dt_bias", "ssd_a_log", "ssd_d", "ssd_gnorm_w", "sb_qn_w", "sb_kn_w",
              "ple_norm_w")
PACK_ROWS = 8


def _pack(parts):
    flat = jnp.concatenate([t.reshape(-1) for t in parts])
    pad = (-flat.shape[0]) % (PACK_ROWS * LANES)
    return jnp.pad(flat, (0, pad)).reshape(-1, LANES)


def _unpack(packed, like):
    flat = packed.reshape(-1)
    out, off = [], 0
    for t in like:
        out.append(flat[off:off + t.size].reshape(t.shape))
        off += t.size
    return out


def _to_group_lanes(v, r):
    t = v.reshape(v.shape[0], SSD_N_GROUPS, r).transpose(1, 0, 2)
    return jnp.pad(t, ((0, 0), (0, 0), (0, LANES - r)))


def _from_group_lanes(t, r):
    return t[:, :, :r].transpose(1, 0, 2).reshape(t.shape[1], SSD_N_GROUPS * r)


def _head_vec(v, r):
    return jnp.pad(v.reshape(SSD_N_GROUPS, 1, r), ((0, 0), (0, 0), (0, LANES - r)))


def _col_blocks(full):
    rows = full.shape[0]
    return full.reshape(rows, N_DEV, -1).transpose(1, 0, 2)


def _from_col_blocks(blocks):
    return blocks.transpose(1, 0, 2).reshape(blocks.shape[1], -1)


def _split_cols(full, widths):
    out, off = [], 0
    for w in widths:
        out.append(full[:, off:off + w])
        off += w
    return out


def kernel(x, p, norm_w, ssd_in_w, ssd_conv_w, ssd_conv_b, ssd_dt_bias, ssd_a_log, ssd_d, ssd_gnorm_w, ssd_out_w, sb_in_w, sb_qn_w, sb_kn_w, sb_out_w, ple_norm_w, ple_gate_w, ple_proj_w, loss_target, m_norm_w, m_ssd_in_w, m_ssd_conv_w, m_ssd_conv_b, m_ssd_dt_bias, m_ssd_a_log, m_ssd_d, m_ssd_gnorm_w, m_ssd_out_w, m_sb_in_w, m_sb_qn_w, m_sb_kn_w, m_sb_out_w, m_ple_norm_w, m_ple_gate_w, m_ple_proj_w, v_norm_w, v_ssd_in_w, v_ssd_conv_w, v_ssd_conv_b, v_ssd_dt_bias, v_ssd_a_log, v_ssd_d, v_ssd_gnorm_w, v_ssd_out_w, v_sb_in_w, v_sb_qn_w, v_sb_kn_w, v_sb_out_w, v_ple_norm_w, v_ple_gate_w, v_ple_proj_w):
    env = dict(locals())
    wts = {n: env[n] for n in WEIGHT_NAMES}
    mom1 = {n: env["m_" + n] for n in WEIGHT_NAMES}
    mom2 = {n: env["v_" + n] for n in WEIGHT_NAMES}

    s, d = x.shape[1], x.shape[2]
    depth = norm_w.shape[0]
    n_ssd, n_sb = ssd_in_w.shape[0], sb_in_w.shape[0]
    di = ssd_out_w.shape[1] * N_DEV
    n_heads = ssd_dt_bias.shape[1]
    hpg = n_heads // SSD_N_GROUPS
    nbc = SSD_N_GROUPS * SSD_D_STATE
    in_segs = (di, di, nbc, nbc, n_heads)
    conv_segs = (di, nbc, nbc)
    sb_w = sb_out_w.shape[1] * N_DEV
    selectors = ssd_selectors(hpg)
    xi, yi, ci = _mesh_pos()
    core = ci.astype(jnp.int32).reshape(1)
    chip = (2 * xi + yi).astype(jnp.int32).reshape(1)

    def layer_keys(i):
        j = i // 2
        mixer = [("ssd_in_w", j), ("ssd_conv_w", j), ("ssd_out_w", j)] if i % 2 == 0 else [("sb_in_w", j), ("sb_out_w", j)]
        return mixer + [("ple_gate_w", i), ("ple_proj_w", i)]

    def shard_of(key):
        t = wts[key[0]][key[1]]
        return t if key[0] == "ssd_conv_w" else t.astype(BF16)

    me_block = 4 * xi + 2 * yi + ci

    def landing_zone(t):
        return lax.dynamic_update_index_in_dim(lax.empty((N_DEV,) + t.shape, t.dtype), t, me_block, 0)

    gathers = []
    for i in range(depth):
        gathers.append(ici_start("gather", [], [landing_zone(shard_of(k)) for k in layer_keys(i)],
                                 after=[g[4] for g in gathers[-1:]], name=f"ag{i}_start"))
    all_started = gathers[-1][4]
    ssd_full, sb_full, ple_full = {}, {}, {}

    def pass_to_sibling(i, after):
        _, lands = ici_wait("gather", gathers[i], after, name=f"ag{i}_wait")
        return ici_start("pass", [], [t.reshape(N_CHIP, 2, *t.shape[1:]) for t in lands], name=f"ag{i}_pass_start")

    def gather_layer(i, passing, after):
        _, lands = ici_wait("pass", passing, after, name=f"ag{i}_pass_wait")
        full = {k: t.reshape(N_DEV, *t.shape[2:]) for k, t in zip(layer_keys(i), lands)}
        j = i // 2
        if i % 2 == 0:
            ssd_full[j] = dict(
                w_in=_split_cols(_from_col_blocks(full["ssd_in_w", j]), in_segs),
                conv_w=_split_cols(_from_col_blocks(full["ssd_conv_w", j]), conv_segs),
                conv_b=_split_cols(ssd_conv_b[j].reshape(1, -1), conv_segs),
                w_out=full["ssd_out_w", j].reshape(di, d))
        else:
            sb_full[j] = dict(w_in=full["sb_in_w", j], w_out=full["sb_out_w", j].reshape(sb_w, d))
        ple_full[i] = dict(w_gate=full["ple_gate_w", i].reshape(d, d), w_proj=full["ple_proj_w", i])

    h = x.reshape(s, d)
    saved = []
    passing = pass_to_sibling(0, [all_started])
    for i in range(depth):
        j = i // 2
        gather_layer(i, passing, [all_started if i == 0 else h])
        passing = pass_to_sibling(i + 1, [passing[4], h]) if i + 1 < depth else None
        sv = dict(h_in=h)
        u = rmsnorm_fwd(h, norm_w[i], passing[4] if passing else all_started, name=f"l{i}_norm")
        sv["u"] = u
        if i % 2 == 0:
            fw = ssd_full[j]
            raw = [matmul(u, wseg, name=f"l{i}_in{q}") for q, wseg in enumerate(fw["w_in"])]
            z, dt_raw = raw[0], raw[4]
            act = [ssd_conv_fwd(raw[1 + q], fw["conv_w"][q], fw["conv_b"][q], name=f"l{i}_conv{q}") for q in range(3)]
            dt, a_neg = ssd_dt_fwd(dt_raw, ssd_dt_bias[j], ssd_a_log[j], name=f"l{i}_dt")
            dtp = _to_group_lanes(dt, hpg)
            a_g = _head_vec(a_neg.reshape(-1), hpg)
            d_x = jnp.repeat(ssd_d[j].reshape(SSD_N_GROUPS, 1, hpg), SSD_HEAD_DIM, axis=2)
            y, states = ssd_scan_fwd(act[0], act[1], act[2], dtp, a_g, d_x, selectors, heads_per_group=hpg,
                                     name=f"l{i}_scan")
            yn = ssd_gate_fwd(y, z, ssd_gnorm_w[j], name=f"l{i}_gate")
            h1 = matmul(yn, fw["w_out"], res=h, name=f"l{i}_out")
            sv.update(raw=raw, act=act, dt=dt, dtp=dtp, a_g=a_g, d_x=d_x, y=y, states=states, yn=yn)
        else:
            fw = sb_full[j]
            proj = matmul(u, fw["w_in"], name=f"l{i}_in")
            qn, kn = sb_qk_fwd(proj, sb_qn_w[j], sb_kn_w[j], name=f"l{i}_qknorm")
            v_off = 2 * sb_w // SB_HEAD_DIM
            o, tot = sb_attn_fwd(qn, kn, proj, v_off=v_off, name=f"l{i}_attn")
            og = sb_gate_fwd(o, proj, name=f"l{i}_gate")
            h1 = matmul(og, fw["w_out"], res=h, name=f"l{i}_out")
            sv.update(proj=proj, qn=qn, kn=kn, o=o, tot=tot, og=og, v_off=v_off)
        t = rmsnorm_fwd(h1, ple_norm_w[i], name=f"l{i}_plenorm")
        gate_pre = matmul(t, ple_full[i]["w_gate"], name=f"l{i}_plegate")
        pp = matmul(p[i, 0], ple_full[i]["w_proj"], name=f"l{i}_pleproj")
        h = ple_fwd(h1, gate_pre, pp, name=f"l{i}_ple")
        sv.update(h1=h1, t=t, gate_pre=gate_pre, pp=pp)
        saved.append(sv)

    loss_part, dh = loss_head(h, loss_target.reshape(s, d), name="loss_head")
    loss = lax.psum(loss_part[0, 0], ("x", "y", "c"))

    big = {}
    small = {n: [None] * wts[n].shape[0] for n in REPLICATED}
    scatters = {}
    order_after = jnp.zeros(TOKEN_SHAPE, F32)
    for i in reversed(range(depth)):
        j = i // 2
        sv = saved[i]
        dpp, dgp = ple_bwd(dh, sv["gate_pre"], sv["pp"], order_after, name=f"b{i}_ple")
        big["ple_proj_w", i] = matmul(p[i, 0], dpp, mode="tn", out_dtype=BF16, out_blocks=ple_proj_w.shape[2],
                                      name=f"b{i}_pleproj_w")
        big["ple_gate_w", i] = matmul(sv["t"], dgp, mode="tn", out_dtype=BF16, name=f"b{i}_plegate_w").reshape(N_DEV, -1, d)
        dt_ = matmul(dgp, ple_full[i]["w_gate"], mode="nt", name=f"b{i}_plegate_x")
        dh1, g_pn = rmsnorm_bwd(sv["h1"], ple_norm_w[i], dt_, dh, name=f"b{i}_plenorm")
        small["ple_norm_w"][i] = g_pn
        u = sv["u"]
        if i % 2 == 0:
            fw = ssd_full[j]
            raw, act = sv["raw"], sv["act"]
            big["ssd_out_w", j] = matmul(sv["yn"], dh1, mode="tn", out_dtype=BF16, name=f"b{i}_out_w").reshape(N_DEV, -1, d)
            dyn = matmul(dh1, fw["w_out"], mode="nt", name=f"b{i}_out_x")
            dy, dz, g_gn = ssd_gate_bwd(sv["y"], raw[0], ssd_gnorm_w[j], dyn, name=f"b{i}_gate")
            dxs, dbm, dcm, ddtp, dadtp, dd_g = ssd_scan_bwd(act[0], act[1], act[2], sv["dtp"], sv["a_g"], sv["d_x"], selectors,
                                                          sv["states"], dy, heads_per_group=hpg, name=f"b{i}_scan")
            ddt_raw, g_dtb, g_alog = ssd_dt_bwd(raw[4], ssd_dt_bias[j], ssd_a_log[j], sv["dt"],
                                                _from_group_lanes(ddtp, hpg), _from_group_lanes(dadtp, hpg),
                                                name=f"b{i}_dt")
            conv_back = [ssd_conv_bwd(raw[1 + q], fw["conv_w"][q], fw["conv_b"][q], dact, name=f"b{i}_conv{q}")
                         for q, dact in enumerate((dxs, dbm, dcm))]
            dsegs = [dz] + [cb[0] for cb in conv_back] + [ddt_raw]
            g_in = jnp.concatenate([matmul(u, ds, mode="tn", out_dtype=BF16, name=f"b{i}_in{q}_w")
                                    for q, ds in enumerate(dsegs)], axis=1)
            big["ssd_in_w", j] = _col_blocks(g_in)
            big["ssd_conv_w", j] = _col_blocks(jnp.concatenate([cb[1] for cb in conv_back], axis=1))
            du = None
            for q, (ds, wseg) in enumerate(zip(dsegs, fw["w_in"])):
                du = matmul(ds, wseg, mode="nt", res=du, name=f"b{i}_in{q}_x")
            small["ssd_conv_b"][j] = jnp.concatenate([cb[2] for cb in conv_back], axis=1)
            small["ssd_dt_bias"][j] = g_dtb
            small["ssd_a_log"][j] = g_alog
            small["ssd_d"][j] = dd_g[:, 0, :hpg]
            small["ssd_gnorm_w"][j] = g_gn
        else:
            fw = sb_full[j]
            proj = sv["proj"]
            big["sb_out_w", j] = matmul(sv["og"], dh1, mode="tn", out_dtype=BF16, name=f"b{i}_out_w").reshape(N_DEV, -1, d)
            dog = matmul(dh1, fw["w_out"], mode="nt", name=f"b{i}_out_x")
            do, dg = sb_gate_bwd(dog, sv["o"], proj, name=f"b{i}_gate")
            dqn, dkn, dv = sb_attn_bwd(sv["qn"], sv["kn"], proj, sv["tot"], do, v_off=sv["v_off"], name=f"b{i}_attn")
            dproj, g_qn, g_kn = sb_pack_bwd(proj, sb_qn_w[j], sb_kn_w[j], dqn, dkn, dv, dg, name=f"b{i}_qknorm")
            big["sb_in_w", j] = matmul(u, dproj, mode="tn", out_dtype=BF16, out_blocks=sb_in_w.shape[2], name=f"b{i}_in_w")
            du = matmul(dproj, fw["w_in"], mode="nt", name=f"b{i}_in_x")
            small["sb_qn_w"][j] = g_qn
            small["sb_kn_w"][j] = g_kn
        dh, g_n = rmsnorm_bwd(sv["h_in"], norm_w[i], du, dh1, name=f"b{i}_norm")
        small["norm_w"][i] = g_n
        blocks = [big[k].reshape(N_CHIP, 2, *big[k].shape[1:]) for k in layer_keys(i)]
        from_sibling = sibling_exchange(blocks, name=f"rs{i}_sibling")
        sums = [pair_add(g, r1, core, name=f"rs{i}_pair_add{q}") for q, (g, r1) in enumerate(zip(blocks, from_sibling))]
        scatters[i] = ici_start("scatter", sums, [lax.empty((N_PEER_CHIPS,) + t.shape[1:], t.dtype) for t in sums],
                                name=f"rs{i}_start")
        order_after = scatters[i][4]
    grad_x = dh.reshape(x.shape)

    rep_like = [wts[n] for n in REPLICATED]
    g_packed = all_reduce_small(_pack([jnp.stack([t.reshape(-1) for t in small[n]]) for n in REPLICATED]),
                                name="all_reduce_small_grads")
    d_packed, m_packed, v_packed = adamw_replicated(
        _pack(rep_like), _pack([mom1[n] for n in REPLICATED]), _pack([mom2[n] for n in REPLICATED]), g_packed,
        name="adamw_replicated")
    grads = dict(zip(REPLICATED, _unpack(g_packed, rep_like)))
    deltas = dict(zip(REPLICATED, _unpack(d_packed, rep_like)))
    new_m = dict(zip(REPLICATED, _unpack(m_packed, rep_like)))
    new_v = dict(zip(REPLICATED, _unpack(v_packed, rep_like)))

    updated = {}
    after = [order_after]
    for i in reversed(range(depth)):
        sums, received = ici_wait("scatter", scatters[i], after, name=f"rs{i}_wait")
        for (n, idx), t_sum, recv in zip(layer_keys(i), sums, received):
            updated[n] = adamw_sharded(wts[n], mom1[n], mom2[n], idx, t_sum, recv, chip, updated.get(n),
                                       name=f"adamw_{n}{idx}")
        after = [updated[n][0] for n, _ in layer_keys(i)]
    for n, (g_n, d_n, m_n, v_n) in updated.items():
        grads[n], deltas[n], new_m[n], new_v[n] = g_n, d_n, m_n, v_n

    return (loss, grad_x, *[grads[n] for n in WEIGHT_NAMES], *[deltas[n] for n in WEIGHT_NAMES],
            *[new_m[n] for n in WEIGHT_NAMES], *[new_v[n] for n in WEIGHT_NAMES])
```

```python
import functools
import math

import jax
import jax.numpy as jnp
from jax import lax
from jax.experimental import pallas as pl
from jax.experimental.pallas import tpu as pltpu

F32 = jnp.float32
BF16 = jnp.bfloat16
MESH = pl.DeviceIdType.MESH

N_DEV = 8
N_CHIP = 4
LANES = 128
VMEM_LIMIT_BYTES = 48 * 1024 * 1024
MATMUL_TILE_BYTES = 28 * 1024 * 1024

NORM_EPS = 1e-6
GATED_NORM_EPS = 1e-5
SSD_HEAD_DIM = 64
SSD_N_GROUPS = 8
SSD_D_STATE = 128
SSD_D_CONV = 4
SSD_CHUNK = 128
SB_HEAD_DIM = 128
PLE_DIM = 256

ADAM_LR = 0.001
ADAM_B1 = 0.9
ADAM_B2 = 0.999
ADAM_EPS = 1e-08
ADAM_WD = 0.01
ADAM_STEP = 10


def _cparams(sem=None, **kw):
    return pltpu.CompilerParams(dimension_semantics=sem, vmem_limit_bytes=VMEM_LIMIT_BYTES, **kw)


def _pick(dim, prefs):
    for t in prefs:
        if dim % t == 0:
            return t
    return dim


def _sigmoid(x):
    return 1.0 / (1.0 + jnp.exp(-x))


def _silu(x):
    return x * _sigmoid(x)


def _silu_grad(x):
    s = _sigmoid(x)
    return s * (1.0 + x * (1.0 - s))


def matmul(a, b, *, mode="nn", out_dtype=F32, res=None, out_blocks=None, after=None, name):
    b_blocked = b.ndim == 3
    if mode == "nn":
        m, kc = a.shape
        n = b.shape[-1] * (N_DEV if b_blocked else 1)
    elif mode == "nt":
        m, kc = a.shape
        n = b.shape[-2]
    else:
        kc, m = a.shape
        n = b.shape[-1]
    nb = b.shape[-1] if b_blocked else None
    tn = _pick(n if not out_blocks else out_blocks, (512, 256, 128))
    if b_blocked and mode == "nn":
        tn = _pick(nb, (512, 256, 128))
    k_limit = nb if (b_blocked and mode == "nt") else kc
    tm, tk = None, None
    for tm_try in (1024, 512, 256, 128):
        if m % tm_try:
            continue
        for tk_try in (k_limit, 2048, 1024, 512, 256, 128):
            if tk_try > k_limit or k_limit % tk_try:
                continue
            tiles = 2 * (tm_try * tk_try * a.dtype.itemsize + tk_try * tn * b.dtype.itemsize)
            tiles += tm_try * tn * (2 * jnp.dtype(out_dtype).itemsize + 4 + (8 if res is not None else 0))
            if tiles <= MATMUL_TILE_BYTES:
                tm, tk = tm_try, tk_try
                break
        if tm:
            break
    if tm is None:
        tm, tk = m, k_limit
    nk = kc // tk
    grid = (m // tm, n // tn, nk)

    if mode == "tn":
        a_spec = pl.BlockSpec((tk, tm), lambda i, j, k: (k, i))
        dims = (((0,), (0,)), ((), ()))
    else:
        a_spec = pl.BlockSpec((tm, tk), lambda i, j, k: (i, k))
        dims = (((1,), (0,)), ((), ())) if mode == "nn" else (((1,), (1,)), ((), ()))
    if mode == "nt":
        if b_blocked:
            per = nb // tk
            b_spec = pl.BlockSpec((None, tn, tk), lambda i, j, k: (k // per, j, k % per))
        else:
            b_spec = pl.BlockSpec((tn, tk), lambda i, j, k: (j, k))
    else:
        if b_blocked:
            per = nb // tn
            b_spec = pl.BlockSpec((None, tk, tn), lambda i, j, k: (j // per, k, j % per))
        else:
            b_spec = pl.BlockSpec((tk, tn), lambda i, j, k: (k, j))
    if out_blocks:
        per_o = out_blocks // tn
        out_shape = jax.ShapeDtypeStruct((n // out_blocks, m, out_blocks), out_dtype)
        out_spec = pl.BlockSpec((None, tm, tn), lambda i, j, k: (j // per_o, i, j % per_o))
    else:
        out_shape = jax.ShapeDtypeStruct((m, n), out_dtype)
        out_spec = pl.BlockSpec((tm, tn), lambda i, j, k: (i, j))
    in_specs = [a_spec, b_spec]
    args = [a, b]
    if res is not None:
        in_specs.append(pl.BlockSpec((tm, tn), lambda i, j, k: (i, j)))
        args.append(res)
    if after is not None:
        in_specs.append(pl.BlockSpec(memory_space=pl.ANY))
        args.append(after)
    n_in = len(args)

    def body(*refs):
        a_ref, b_ref = refs[:2]
        r_ref = refs[2] if res is not None else None
        o_ref = refs[n_in]

        def finish(r):
            if res is not None:
                r = r + r_ref[...].astype(F32)
            o_ref[...] = r.astype(out_dtype)

        part = lax.dot_general(a_ref[...].astype(BF16), b_ref[...].astype(BF16), dims, preferred_element_type=F32)
        if nk == 1:
            finish(part)
            return
        acc_ref = refs[-1]
        k = pl.program_id(2)

        @pl.when(k == 0)
        def _():
            acc_ref[...] = part

        @pl.when(k > 0)
        def _():
            acc_ref[...] += part

        @pl.when(k == nk - 1)
        def _():
            finish(acc_ref[...])

    return pl.pallas_call(
        body, out_shape=out_shape, grid=grid, in_specs=in_specs, out_specs=out_spec,
        scratch_shapes=[] if nk == 1 else [pltpu.VMEM((tm, tn), F32)], name=name,
        compiler_params=_cparams(("parallel", "parallel", "arbitrary")),
    )(*args)


def _dot(a, b, dims, precision=None):
    return lax.dot_general(a, b, (dims, ((), ())), preferred_element_type=F32, precision=precision)


_NN = ((1,), (0,))
_NT = ((1,), (1,))
_TN = ((0,), (0,))
_EXACT = lax.Precision.HIGHEST


def _chunk_decay_terms(dt, a):
    ln = dt.shape[0]
    row = lax.broadcasted_iota(jnp.int32, (ln, ln), 0)
    col = lax.broadcasted_iota(jnp.int32, (ln, ln), 1)
    tri = (row >= col).astype(F32)
    a_col = _dot(tri, dt * a, _NN, _EXACT)
    return a_col, a_col.T, row >= col


def _exact_dot(x, sel, terms):
    t = x.shape[0]
    parts, rest = [], x
    for k in range(terms):
        piece = rest.astype(BF16)
        parts.append(piece)
        if k + 1 < terms:
            rest = rest - piece.astype(F32)
    r = _dot(jnp.concatenate(parts, axis=0), sel, _NN)
    out = r[:t]
    for k in range(1, terms):
        out = out + r[k * t:(k + 1) * t]
    return out


def ssd_selectors(r_n):
    lane = jnp.arange(LANES)
    spread64 = (lane[:, None] == jnp.arange(r_n * SSD_HEAD_DIM)[None, :] // SSD_HEAD_DIM).astype(BF16)
    pair_sum = jnp.stack([lane[None, :] == 2 * q + lane[:, None] // SSD_HEAD_DIM for q in range(r_n // 2)]).astype(BF16)
    row_sum = jnp.stack([jnp.broadcast_to(lane[None, :] == r, (LANES, LANES)) for r in range(r_n)]).astype(BF16)
    return spread64, pair_sum, row_sum


def _ssd_chunk_setup(dt, a, spread64):
    ln = dt.shape[0]
    a_col, a_row, causal = _chunk_decay_terms(dt, a)
    ea = jnp.exp(a_col)
    te = jnp.exp(a_col[ln - 1:ln, :] - a_col)
    return (a_row, a_col, _exact_dot(dt, spread64, 2), _exact_dot(ea, spread64, 2), _exact_dot(te, spread64, 2),
            ea, causal)


def ssd_scan_fwd(xs, bm, cm, dtp, a_g, d_x, selectors, *, heads_per_group, name):
    s, di = xs.shape
    g_n = SSD_N_GROUPS
    r_n, p_n, n_n, ln = heads_per_group, SSD_HEAD_DIM, SSD_D_STATE, SSD_CHUNK
    nc = s // ln
    pairs, pw = r_n // 2, 2 * p_n
    spread64 = selectors[0]

    def body(xs_ref, bm_ref, cm_ref, dt_ref, a_ref, d_ref, s64_ref, y_ref, st_ref, state):
        c = pl.program_id(1)

        @pl.when(c == 0)
        def _():
            state[...] = jnp.zeros_like(state)

        a_row, a_col, dt_x, ea_x, te_x, _, causal = _ssd_chunk_setup(dt_ref[...], a_ref[...], s64_ref[...])
        bm_f = bm_ref[...]
        bmb = bm_f.astype(BF16)
        bm_t = bm_f.T.astype(BF16)
        cmb = cm_ref[...].astype(BF16)
        scores = _dot(cmb, bmb, _NT)
        first_head = lax.broadcasted_iota(jnp.int32, (1, pw), 1) < p_n
        for q in range(pairs):
            sl = slice(q * pw, (q + 1) * pw)
            x2 = xs_ref[:, sl]
            xdt2 = x2 * dt_x[:, sl]
            xdt2b = xdt2.astype(BF16)
            y_heads = []
            for r in (2 * q, 2 * q + 1):
                decay = jnp.exp(jnp.where(causal, a_col[:, r:r + 1] - a_row[r:r + 1, :], -jnp.inf))
                y_heads.append(_dot((scores * decay).astype(BF16), xdt2b, _NN))
            s2t = state[q]
            st_ref[q] = s2t
            y2 = jnp.where(first_head, y_heads[0], y_heads[1])
            y2 = y2 + ea_x[:, sl] * _dot(cmb, s2t.astype(BF16), _NN)
            y_ref[:, sl] = y2 + d_ref[:, sl] * x2
            state[q] = s2t * ea_x[ln - 1:ln, sl] + _dot(bm_t, (xdt2 * te_x[:, sl]).astype(BF16), _NN)

    whole = lambda t: pl.BlockSpec(t.shape, lambda g, c: (0,) * t.ndim)
    return pl.pallas_call(
        body,
        out_shape=(jax.ShapeDtypeStruct((s, di), F32),
                   jax.ShapeDtypeStruct((nc, g_n * pairs, n_n, pw), F32)),
        grid=(g_n, nc),
        in_specs=[pl.BlockSpec((ln, r_n * p_n), lambda g, c: (c, g)),
                  pl.BlockSpec((ln, n_n), lambda g, c: (c, g)),
                  pl.BlockSpec((ln, n_n), lambda g, c: (c, g)),
                  pl.BlockSpec((None, ln, LANES), lambda g, c: (g, c, 0)),
                  pl.BlockSpec((None, 1, LANES), lambda g, c: (g, 0, 0)),
                  pl.BlockSpec((None, 1, r_n * p_n), lambda g, c: (g, 0, 0)),
                  whole(spread64)],
        out_specs=(pl.BlockSpec((ln, r_n * p_n), lambda g, c: (c, g)),
                   pl.BlockSpec((None, pairs, n_n, pw), lambda g, c: (c, g, 0, 0))),
        scratch_shapes=[pltpu.VMEM((pairs, n_n, pw), F32)],
        name=name, compiler_params=_cparams(("parallel", "arbitrary")),
    )(xs, bm, cm, dtp, a_g, d_x, spread64)


def _row8(v):
    return jnp.broadcast_to(v, (8, v.shape[1]))


def ssd_scan_bwd(xs, bm, cm, dtp, a_g, d_x, selectors, states, dy, *, heads_per_group, name):
    s, di = xs.shape
    g_n = SSD_N_GROUPS
    r_n, p_n, n_n, ln = heads_per_group, SSD_HEAD_DIM, SSD_D_STATE, SSD_CHUNK
    nc = s // ln
    pairs, pw = r_n // 2, 2 * p_n
    spread64, pair_sum, row_sum = selectors

    def body(xs_ref, bm_ref, cm_ref, dt_ref, a_ref, d_ref, s64_ref, ps_ref, rs_ref, st_ref, dy_ref,
             dxs_ref, dbm_ref, dcm_ref, ddt_ref, dadt_ref, dd_ref, dstate, da_rows):
        c = pl.program_id(1)

        @pl.when(c == 0)
        def _():
            dstate[...] = jnp.zeros_like(dstate)
            dd_ref[...] = jnp.zeros_like(dd_ref)

        a_row, a_col, dt_x, ea_x, te_x, ea, causal = _ssd_chunk_setup(dt_ref[...], a_ref[...], s64_ref[...])
        row = lax.broadcasted_iota(jnp.int32, (ln, ln), 0)
        col = lax.broadcasted_iota(jnp.int32, (ln, ln), 1)
        causal_t = col >= row
        bmb = bm_ref[...].astype(BF16)
        cm_f = cm_ref[...]
        cmb = cm_f.astype(BF16)
        cm_t = cm_f.T.astype(BF16)
        scores = _dot(cmb, bmb, _NT)
        scores_t = _dot(bmb, cmb, _NT)
        first_head = lax.broadcasted_iota(jnp.int32, (1, pw), 1) < p_n
        e_last = ea[ln - 1:ln, :]
        da_rows[...] = jnp.zeros_like(da_rows)
        dscores = jnp.zeros((ln, ln), F32)
        dcm = jnp.zeros((ln, n_n), F32)
        dbm = jnp.zeros((ln, n_n), F32)
        da_cols = jnp.zeros((ln, LANES), F32)
        da_last = jnp.zeros((1, LANES), F32)
        ddt = jnp.zeros((ln, LANES), F32)
        dd = jnp.zeros((1, LANES), F32)
        for q in range(pairs):
            sl = slice(q * pw, (q + 1) * pw)
            sum2 = ps_ref[q]
            x2 = xs_ref[:, sl]
            dt2 = dt_x[:, sl]
            xdt2 = x2 * dt2
            xdt2b = xdt2.astype(BF16)
            dy2 = dy_ref[:, sl]
            dy2b = dy2.astype(BF16)
            dxdt_heads = []
            for h, r in enumerate((2 * q, 2 * q + 1)):
                a_r = jnp.broadcast_to(a_col[:, r:r + 1], (ln, ln))
                decay = jnp.exp(jnp.where(causal, a_r - a_row[r:r + 1, :], -jnp.inf))
                decay_t = jnp.exp(jnp.where(causal_t, a_row[r:r + 1, :] - a_r, -jnp.inf))
                dy_h = jnp.where(first_head if h == 0 else jnp.logical_not(first_head), dy2, 0.0).astype(BF16)
                dm = _dot(dy_h, xdt2b, _NT)
                dscores = dscores + dm * decay
                e_mat = dm * (scores * decay)
                da_cols = da_cols + _exact_dot(e_mat, rs_ref[r], 2)
                da_rows[r:r + 1, :] = -jnp.sum(e_mat, axis=0, keepdims=True)
                dxdt_heads.append(_dot((scores_t * decay_t).astype(BF16), dy2b, _NN))
            dxdt2 = jnp.where(first_head, dxdt_heads[0], dxdt_heads[1])
            s2t = st_ref[q]
            s2tb = s2t.astype(BF16)
            ds2t = dstate[q]
            ds2tb = ds2t.astype(BF16)
            ea2, te2 = ea_x[:, sl], te_x[:, sl]
            y_off2 = ea2 * _dot(cmb, s2tb, _NN)
            dy_e2 = (dy2 * ea2).astype(BF16)
            dcm = dcm + _dot(dy_e2, s2tb, _NT)
            ds_in = _dot(cm_t, dy_e2, _NN)
            da_cols = da_cols + _exact_dot(dy2 * y_off2, sum2, 2)
            bds2 = _dot(bmb, ds2tb, _NN)
            dxdt2 = dxdt2 + te2 * bds2
            xdt_e2 = xdt2 * te2
            dbm = dbm + _dot(xdt_e2.astype(BF16), ds2tb, _NT)
            w_cols = _exact_dot(xdt_e2 * bds2, sum2, 2)
            da_cols = da_cols - w_cols
            state_dot = _exact_dot(_row8(jnp.sum(ds2t * s2t, axis=0, keepdims=True)), sum2, 2)[0:1]
            da_last = da_last + jnp.sum(w_cols, axis=0, keepdims=True) + e_last * state_dot
            dstate[q] = ds2t * ea_x[ln - 1:ln, sl] + ds_in
            dxs_ref[:, sl] = dxdt2 * dt2 + d_ref[:, sl] * dy2
            ddt = ddt + _exact_dot(dxdt2 * x2, sum2, 2)
            dd = dd + _exact_dot(_row8(jnp.sum(dy2 * x2, axis=0, keepdims=True)), sum2, 2)[0:1]
        dcm_ref[...] = dcm + _dot(dscores.astype(BF16), bmb, _NN)
        dbm_ref[...] = dbm + _dot(dscores.T.astype(BF16), cmb, _NN)
        da_total = da_cols + da_rows[...].T
        upper = causal_t.astype(F32)
        dadt_ref[...] = _dot(upper, da_total, _NN, _EXACT) + da_last
        ddt_ref[...] = ddt
        dd_ref[...] += dd

    last_c = nc - 1
    whole = lambda t: pl.BlockSpec(t.shape, lambda g, c: (0,) * t.ndim)
    return pl.pallas_call(
        body,
        out_shape=(jax.ShapeDtypeStruct((s, di), F32),
                   jax.ShapeDtypeStruct(bm.shape, F32),
                   jax.ShapeDtypeStruct(cm.shape, F32),
                   jax.ShapeDtypeStruct(dtp.shape, F32),
                   jax.ShapeDtypeStruct(dtp.shape, F32),
                   jax.ShapeDtypeStruct(a_g.shape, F32)),
        grid=(g_n, nc),
        in_specs=[pl.BlockSpec((ln, r_n * p_n), lambda g, c: (last_c - c, g)),
                  pl.BlockSpec((ln, n_n), lambda g, c: (last_c - c, g)),
                  pl.BlockSpec((ln, n_n), lambda g, c: (last_c - c, g)),
                  pl.BlockSpec((None, ln, LANES), lambda g, c: (g, last_c - c, 0)),
                  pl.BlockSpec((None, 1, LANES), lambda g, c: (g, 0, 0)),
                  pl.BlockSpec((None, 1, r_n * p_n), lambda g, c: (g, 0, 0)),
                  whole(spread64), whole(pair_sum), whole(row_sum),
                  pl.BlockSpec((None, pairs, n_n, pw), lambda g, c: (last_c - c, g, 0, 0)),
                  pl.BlockSpec((ln, r_n * p_n), lambda g, c: (last_c - c, g))],
        out_specs=(pl.BlockSpec((ln, r_n * p_n), lambda g, c: (last_c - c, g)),
                   pl.BlockSpec((ln, n_n), lambda g, c: (last_c - c, g)),
                   pl.BlockSpec((ln, n_n), lambda g, c: (last_c - c, g)),
                   pl.BlockSpec((None, ln, LANES), lambda g, c: (g, last_c - c, 0)),
                   pl.BlockSpec((None, ln, LANES), lambda g, c: (g, last_c - c, 0)),
                   pl.BlockSpec((None, 1, LANES), lambda g, c: (g, 0, 0))),
        scratch_shapes=[pltpu.VMEM((pairs, n_n, pw), F32), pltpu.VMEM((LANES, ln), F32)],
        name=name, compiler_params=_cparams(("parallel", "arbitrary")),
    )(xs, bm, cm, dtp, a_g, d_x, spread64, pair_sum, row_sum, states, dy)


SB_TILE = 256


def _tri_sum(x, tri):
    t = x.shape[0]
    hi = x.astype(BF16)
    r1 = x - hi.astype(F32)
    mid = r1.astype(BF16)
    lo = (r1 - mid.astype(F32)).astype(BF16)
    r = _dot(jnp.concatenate([hi, mid, lo], axis=0), tri, _NN)
    return r[:t] + r[t:2 * t] + r[2 * t:]


def _sb_logits(q, k_j, scale, strict):
    z = _dot(q, k_j, _NT) * scale
    sp = jnp.log(1.0 + jnp.exp(-jnp.abs(z)))
    log_b = jnp.minimum(z, 0.0) - sp
    log_1mb = log_b - z
    if strict is not None:
        log_1mb = jnp.where(strict, log_1mb, 0.0)
    return log_b, log_1mb


def _sb_tile(s):
    return _pick(s, (SB_TILE, LANES))


def _sb_iotas(t):
    row = lax.broadcasted_iota(jnp.int32, (t, t), 0)
    col = lax.broadcasted_iota(jnp.int32, (t, t), 1)
    return row, col


def sb_attn_fwd(qn, kn, v, *, v_off=0, name):
    s, w = qn.shape
    dh = SB_HEAD_DIM
    n_h = w // dh
    t = _sb_tile(s)
    scale = 1.0 / math.sqrt(dh)

    def body(q_ref, k_ref, v_ref, o_ref, tot_ref):
        i = pl.program_id(1)
        q = q_ref[...]
        row, col = _sb_iotas(t)
        later = (row > col).astype(BF16)

        def tile(j, acc, run, strict):
            s0 = pl.multiple_of(j * t, t)
            k_j = k_ref[pl.ds(s0, t), :]
            v_j = v_ref[pl.ds(s0, t), :].astype(BF16)
            log_b, log_1mb = _sb_logits(q, k_j, scale, strict)
            att = jnp.exp(log_b + (_tri_sum(log_1mb, later) + run))
            if strict is not None:
                att = jnp.where(strict, att, 0.0)
            acc = acc + _dot(att.astype(BF16), v_j, _NN)
            return acc, run + jnp.sum(log_1mb, axis=1, keepdims=True)

        acc, run = tile(i, jnp.zeros((t, dh), F32), jnp.zeros((t, 1), F32), col < row)
        acc, run = lax.fori_loop(0, i, lambda jj, c: tile(i - 1 - jj, c[0], c[1], None), (acc, run))
        o_ref[...] = acc
        tot_ref[...] = jnp.broadcast_to(run, (t, dh))

    return pl.pallas_call(
        body,
        out_shape=(jax.ShapeDtypeStruct((s, w), F32), jax.ShapeDtypeStruct((s, w), F32)),
        grid=(n_h, s // t),
        in_specs=[pl.BlockSpec((t, dh), lambda h, i: (i, h)),
                  pl.BlockSpec((s, dh), lambda h, i: (0, h)),
                  pl.BlockSpec((s, dh), lambda h, i: (0, v_off + h))],
        out_specs=(pl.BlockSpec((t, dh), lambda h, i: (i, h)),
                   pl.BlockSpec((t, dh), lambda h, i: (i, h))),
        name=name, compiler_params=_cparams(("parallel", "parallel")),
    )(qn, kn, v)


def sb_attn_bwd(qn, kn, v, tot, do, *, v_off=0, name):
    s, w = qn.shape
    dh = SB_HEAD_DIM
    n_h = w // dh
    t = _sb_tile(s)
    scale = 1.0 / math.sqrt(dh)

    def body(q_ref, k_ref, v_ref, tot_ref, do_ref, dq_ref, dk_ref, dv_ref):
        dk_ref[...] = jnp.zeros_like(dk_ref)
        dv_ref[...] = jnp.zeros_like(dv_ref)
        row, col = _sb_iotas(t)
        upto = (row <= col).astype(BF16)
        before = (row < col).astype(BF16)

        def q_block(i, _):
            t0 = pl.multiple_of(i * t, t)
            q = q_ref[pl.ds(t0, t), :]
            do_i = do_ref[pl.ds(t0, t), :].astype(BF16)
            total = tot_ref[pl.ds(t0, t), 0:1]

            def tile(j, dq, run_l, run_g, strict):
                s0 = pl.multiple_of(j * t, t)
                k_j = k_ref[pl.ds(s0, t), :]
                v_j = v_ref[pl.ds(s0, t), :].astype(BF16)
                log_b, log_1mb = _sb_logits(q, k_j, scale, strict)
                att = jnp.exp(log_b + ((total - run_l) - _tri_sum(log_1mb, upto)))
                if strict is not None:
                    att = jnp.where(strict, att, 0.0)
                g = att * _dot(do_i, v_j, _NT)
                c = _tri_sum(g, before) + run_g
                dz = (g - (g + c) * jnp.exp(log_b)) * scale
                if strict is not None:
                    dz = jnp.where(strict, dz, 0.0)
                dz = dz.astype(BF16)
                dq = dq + _dot(dz, k_j, _NN)
                dk_ref[pl.ds(s0, t), :] += _dot(dz, q, _TN)
                dv_ref[pl.ds(s0, t), :] += _dot(att.astype(BF16), do_i, _TN)
                return (dq, run_l + jnp.sum(log_1mb, axis=1, keepdims=True),
                        run_g + jnp.sum(g, axis=1, keepdims=True))

            zero = jnp.zeros((t, 1), F32)
            carry = lax.fori_loop(0, i, lambda j, c: tile(j, c[0], c[1], c[2], None),
                                  (jnp.zeros((t, dh), F32), zero, zero))
            dq, _, _ = tile(i, carry[0], carry[1], carry[2], col < row)
            dq_ref[pl.ds(t0, t), :] = dq
            return 0

        lax.fori_loop(0, s // t, q_block, 0)

    head = pl.BlockSpec((s, dh), lambda h: (0, h))
    return pl.pallas_call(
        body,
        out_shape=tuple(jax.ShapeDtypeStruct((s, w), F32) for _ in range(3)),
        grid=(n_h,),
        in_specs=[head, head, pl.BlockSpec((s, dh), lambda h: (0, v_off + h)), head, head],
        out_specs=(head, head, head),
        name=name, compiler_params=_cparams(("parallel",)),
    )(qn, kn, v, tot, do)


ROW_TILE = 256
WIDE_ROW_TILE = 64


def _rows(width, col=0, tm=ROW_TILE):
    return pl.BlockSpec((tm, width), lambda i: (i, col))


_wide_rows = functools.partial(_rows, tm=WIDE_ROW_TILE)


def _whole(shape):
    return pl.BlockSpec(shape, lambda i: (0,) * len(shape))


def _ew_call(body, out_shape, in_specs, out_specs, args, n_rows, name, carried=False):
    return pl.pallas_call(
        body, out_shape=out_shape, grid=(n_rows // in_specs[0].block_shape[0],), in_specs=in_specs, out_specs=out_specs,
        name=name, compiler_params=_cparams(("arbitrary",) if carried else ("parallel",)),
    )(*args)


def _first_step(*refs):
    @pl.when(pl.program_id(0) == 0)
    def _():
        for r in refs:
            r[...] = jnp.zeros_like(r)


def rmsnorm_fwd(x, w, after=None, *, name):
    s, d = x.shape

    def body(x_ref, w_ref, *rest):
        o_ref = rest[-1]
        xv = x_ref[...]
        r = lax.rsqrt(jnp.mean(xv * xv, axis=-1, keepdims=True) + NORM_EPS)
        o_ref[...] = (xv * r * w_ref[...]).astype(BF16)

    extra = [] if after is None else [after]
    return _ew_call(body, jax.ShapeDtypeStruct((s, d), BF16),
                    [_rows(d), _whole((1, d))] + [_whole(TOKEN_SHAPE)] * len(extra), _rows(d),
                    (x, w.reshape(1, d), *extra), s, name)


def rmsnorm_bwd(x, w, dy, dres, *, name):
    s, d = x.shape

    def body(x_ref, w_ref, dy_ref, dr_ref, dx_ref, dw_ref):
        _first_step(dw_ref)
        xv = x_ref[...]
        r = lax.rsqrt(jnp.mean(xv * xv, axis=-1, keepdims=True) + NORM_EPS)
        xhat = xv * r
        dyv = dy_ref[...].astype(F32)
        dw_ref[...] += jnp.sum(dyv * xhat, axis=0, keepdims=True)
        g = dyv * w_ref[...]
        dx_ref[...] = dr_ref[...] + r * (g - xhat * jnp.mean(g * xhat, axis=-1, keepdims=True))

    return _ew_call(body, (jax.ShapeDtypeStruct((s, d), F32), jax.ShapeDtypeStruct((1, d), F32)),
                    [_rows(d), _whole((1, d)), _rows(d), _rows(d)], (_rows(d), _whole((1, d))),
                    (x, w.reshape(1, d), dy, dres), s, name, carried=True)


def ple_fwd(h1, gate_pre, pp, *, name):
    s, d = h1.shape

    def body(h_ref, g_ref, p_ref, o_ref):
        o_ref[...] = h_ref[...] + p_ref[...] * _sigmoid(g_ref[...])

    return _ew_call(body, jax.ShapeDtypeStruct((s, d), F32), [_rows(d)] * 3, _rows(d), (h1, gate_pre, pp), s, name)


def ple_bwd(dh2, gate_pre, pp, after, *, name):
    s, d = dh2.shape

    def body(dh_ref, g_ref, p_ref, after_ref, dp_ref, dg_ref):
        gate = _sigmoid(g_ref[...])
        dh = dh_ref[...]
        dp_ref[...] = (dh * gate).astype(BF16)
        dg_ref[...] = (dh * p_ref[...] * gate * (1.0 - gate)).astype(BF16)

    shp = jax.ShapeDtypeStruct((s, d), BF16)
    return _ew_call(body, (shp, shp), [_rows(d)] * 3 + [_whole(TOKEN_SHAPE)], (_rows(d), _rows(d)),
                    (dh2, gate_pre, pp, after), s, name)


def loss_head(y, target, *, name):
    s, d = y.shape

    def body(y_ref, t_ref, l_ref, dy_ref):
        _first_step(l_ref)
        err = y_ref[...] - t_ref[...]
        per_tok = jnp.mean(err * err, axis=-1, keepdims=True)
        l_ref[...] += 0.5 * jnp.sum(per_tok, axis=0, keepdims=True)
        dy_ref[...] = err * (1.0 / d)

    return _ew_call(body, (jax.ShapeDtypeStruct((1, 1), F32), jax.ShapeDtypeStruct((s, d), F32)),
                    [_rows(d), _rows(d)], (_whole((1, 1)), _rows(d)), (y, target), s, name, carried=True)


CONV_COL_TILE = 256


def _conv_taps(x, w_ref):
    row = lax.broadcasted_iota(jnp.int32, (x.shape[0], 1), 0)
    acc = x * w_ref[SSD_D_CONV - 1:SSD_D_CONV, :]
    shifted = []
    for d in range(1, SSD_D_CONV):
        xs = jnp.where(row >= d, pltpu.roll(x, d, 0), 0.0)
        shifted.append(xs)
        acc = acc + xs * w_ref[SSD_D_CONV - 1 - d:SSD_D_CONV - d, :]
    return acc, shifted


def ssd_conv_fwd(x, w, b, *, name):
    s, c = x.shape
    tc = _pick(c, (CONV_COL_TILE, LANES))

    def body(x_ref, w_ref, b_ref, o_ref):
        pre, _ = _conv_taps(x_ref[...], w_ref)
        o_ref[...] = _silu(pre + b_ref[...])

    col = pl.BlockSpec((s, tc), lambda j: (0, j))
    return pl.pallas_call(
        body, out_shape=jax.ShapeDtypeStruct((s, c), F32), grid=(c // tc,),
        in_specs=[col, pl.BlockSpec((SSD_D_CONV, tc), lambda j: (0, j)), pl.BlockSpec((1, tc), lambda j: (0, j))],
        out_specs=col, name=name, compiler_params=_cparams(("parallel",)),
    )(x, w, b)


def ssd_conv_bwd(x, w, b, dact, *, name):
    s, c = x.shape
    tc = _pick(c, (CONV_COL_TILE, LANES))

    def body(x_ref, w_ref, b_ref, da_ref, dx_ref, dw_ref, db_ref):
        xv = x_ref[...]
        pre, shifted = _conv_taps(xv, w_ref)
        dpre = da_ref[...] * _silu_grad(pre + b_ref[...])
        db_ref[...] = jnp.sum(dpre, axis=0, keepdims=True)
        row = lax.broadcasted_iota(jnp.int32, (s, 1), 0)
        dx = dpre * w_ref[SSD_D_CONV - 1:SSD_D_CONV, :]
        dw_ref[SSD_D_CONV - 1:SSD_D_CONV, :] = jnp.sum(dpre * xv, axis=0, keepdims=True)
        for d in range(1, SSD_D_CONV):
            k = SSD_D_CONV - 1 - d
            dw_ref[k:k + 1, :] = jnp.sum(dpre * shifted[d - 1], axis=0, keepdims=True)
            up = jnp.where(row < s - d, pltpu.roll(dpre, s - d, 0), 0.0)
            dx = dx + up * w_ref[k:k + 1, :]
        dx_ref[...] = dx.astype(BF16)

    col = pl.BlockSpec((s, tc), lambda j: (0, j))
    wspec = pl.BlockSpec((SSD_D_CONV, tc), lambda j: (0, j))
    bspec = pl.BlockSpec((1, tc), lambda j: (0, j))
    return pl.pallas_call(
        body,
        out_shape=(jax.ShapeDtypeStruct((s, c), BF16), jax.ShapeDtypeStruct((SSD_D_CONV, c), F32),
                   jax.ShapeDtypeStruct((1, c), F32)),
        grid=(c // tc,), in_specs=[col, wspec, bspec, col], out_specs=(col, wspec, bspec),
        name=name, compiler_params=_cparams(("parallel",)),
    )(x, w, b, dact)


def ssd_dt_fwd(dt_raw, bias, a_log, *, name):
    s, h = dt_raw.shape

    def body(r_ref, b_ref, al_ref, dt_ref, a_ref):
        zv = r_ref[...] + b_ref[...]
        dt_ref[...] = jnp.maximum(zv, 0.0) + jnp.log(1.0 + jnp.exp(-jnp.abs(zv)))
        a_ref[...] = -jnp.exp(al_ref[...])

    full = pl.BlockSpec((s, h), lambda: (0, 0))
    vec = pl.BlockSpec((1, h), lambda: (0, 0))
    return pl.pallas_call(
        body, out_shape=(jax.ShapeDtypeStruct((s, h), F32), jax.ShapeDtypeStruct((1, h), F32)),
        in_specs=[full, vec, vec], out_specs=(full, vec), name=name, compiler_params=_cparams(),
    )(dt_raw, bias.reshape(1, h), a_log.reshape(1, h))


def ssd_dt_bwd(dt_raw, bias, a_log, dt, ddt, dadt, *, name):
    s, h = dt_raw.shape

    def body(r_ref, b_ref, al_ref, dt_ref, ddt_ref, dadt_ref, dr_ref, db_ref, dal_ref):
        a = -jnp.exp(al_ref[...])
        dadt_v = dadt_ref[...]
        d_dt = ddt_ref[...] + a * dadt_v
        d_raw = d_dt * _sigmoid(r_ref[...] + b_ref[...])
        dr_ref[...] = d_raw
        db_ref[...] = jnp.sum(d_raw, axis=0, keepdims=True)
        dal_ref[...] = jnp.sum(dadt_v * dt_ref[...], axis=0, keepdims=True) * a

    full = pl.BlockSpec((s, h), lambda: (0, 0))
    vec = pl.BlockSpec((1, h), lambda: (0, 0))
    return pl.pallas_call(
        body, out_shape=(jax.ShapeDtypeStruct((s, h), F32), jax.ShapeDtypeStruct((1, h), F32),
                         jax.ShapeDtypeStruct((1, h), F32)),
        in_specs=[full, vec, vec, full, full, full], out_specs=(full, vec, vec), name=name,
        compiler_params=_cparams(),
    )(dt_raw, bias.reshape(1, h), a_log.reshape(1, h), dt, ddt, dadt)


def _group_mean(v, n_groups):
    gw = v.shape[-1] // n_groups
    parts = [jnp.broadcast_to(jnp.mean(v[:, k * gw:(k + 1) * gw], axis=-1, keepdims=True), (v.shape[0], gw))
             for k in range(n_groups)]
    return jnp.concatenate(parts, axis=-1)


def ssd_gate_fwd(y, z, gw, *, name):
    s, di = y.shape

    def body(y_ref, z_ref, w_ref, o_ref):
        yg = y_ref[...] * _silu(z_ref[...])
        r = lax.rsqrt(_group_mean(yg * yg, SSD_N_GROUPS) + GATED_NORM_EPS)
        o_ref[...] = (yg * r * w_ref[...]).astype(BF16)

    return _ew_call(body, jax.ShapeDtypeStruct((s, di), BF16), [_wide_rows(di), _wide_rows(di), _whole((1, di))],
                    _wide_rows(di), (y, z, gw.reshape(1, di)), s, name)


def ssd_gate_bwd(y, z, gw, dyn, *, name):
    s, di = y.shape

    def body(y_ref, z_ref, w_ref, dn_ref, dy_ref, dz_ref, dw_ref):
        _first_step(dw_ref)
        yv, zv = y_ref[...], z_ref[...]
        sz = _silu(zv)
        yg = yv * sz
        r = lax.rsqrt(_group_mean(yg * yg, SSD_N_GROUPS) + GATED_NORM_EPS)
        yhat = yg * r
        dn = dn_ref[...]
        dw_ref[...] += jnp.sum(dn * yhat, axis=0, keepdims=True)
        g = dn * w_ref[...]
        dyg = r * (g - yhat * _group_mean(g * yhat, SSD_N_GROUPS))
        dy_ref[...] = dyg * sz
        dz_ref[...] = (dyg * yv * _silu_grad(zv)).astype(BF16)

    return _ew_call(body, (jax.ShapeDtypeStruct((s, di), F32), jax.ShapeDtypeStruct((s, di), BF16),
                           jax.ShapeDtypeStruct((1, di), F32)),
                    [_wide_rows(di), _wide_rows(di), _whole((1, di)), _wide_rows(di)],
                    (_wide_rows(di), _wide_rows(di), _whole((1, di))),
                    (y, z, gw.reshape(1, di), dyn), s, name, carried=True)


def _head_mean(v):
    return _group_mean(v, v.shape[-1] // SB_HEAD_DIM)


def sb_qk_fwd(proj, qw, kw, *, name):
    s, w4 = proj.shape
    w = w4 // 4
    reps = w // SB_HEAD_DIM

    def body(q_ref, k_ref, qw_ref, kw_ref, qn_ref, kn_ref):
        for x_ref, w_ref, o_ref in ((q_ref, qw_ref, qn_ref), (k_ref, kw_ref, kn_ref)):
            xv = x_ref[...]
            r = lax.rsqrt(_head_mean(xv * xv) + NORM_EPS)
            o_ref[...] = (xv * r * jnp.tile(w_ref[...], (1, reps))).astype(BF16)

    shp = jax.ShapeDtypeStruct((s, w), BF16)
    return _ew_call(body, (shp, shp), [_rows(w, 0), _rows(w, 1), _whole((1, SB_HEAD_DIM)), _whole((1, SB_HEAD_DIM))],
                    (_rows(w), _rows(w)), (proj, proj, qw.reshape(1, -1), kw.reshape(1, -1)), s, name)


def sb_gate_fwd(o, proj, *, name):
    s, w = o.shape

    def body(o_ref, g_ref, og_ref):
        og_ref[...] = (o_ref[...] * _silu(g_ref[...])).astype(BF16)

    return _ew_call(body, jax.ShapeDtypeStruct((s, w), BF16), [_rows(w), _rows(w, 3)], _rows(w), (o, proj), s, name)


def sb_gate_bwd(dog, o, proj, *, name):
    s, w = o.shape

    def body(d_ref, o_ref, g_ref, do_ref, dg_ref):
        gv, dv = g_ref[...], d_ref[...]
        do_ref[...] = dv * _silu(gv)
        dg_ref[...] = (dv * o_ref[...] * _silu_grad(gv)).astype(BF16)

    return _ew_call(body, (jax.ShapeDtypeStruct((s, w), F32), jax.ShapeDtypeStruct((s, w), BF16)),
                    [_rows(w), _rows(w), _rows(w, 3)], (_rows(w), _rows(w)), (dog, o, proj), s, name)


def sb_pack_bwd(proj, qw, kw, dqn, dkn, dv, dg, *, name):
    s, w4 = proj.shape
    w = w4 // 4
    reps = w // SB_HEAD_DIM

    def body(q_ref, k_ref, qw_ref, kw_ref, dqn_ref, dkn_ref, dv_ref, dg_ref, dp_ref, dqw_ref, dkw_ref):
        _first_step(dqw_ref, dkw_ref)
        for idx, (x_ref, w_ref, d_ref, dw_ref) in enumerate(((q_ref, qw_ref, dqn_ref, dqw_ref),
                                                           (k_ref, kw_ref, dkn_ref, dkw_ref))):
            xv = x_ref[...]
            r = lax.rsqrt(_head_mean(xv * xv) + NORM_EPS)
            xhat = xv * r
            dn = d_ref[...]
            per_col = jnp.sum(dn * xhat, axis=0, keepdims=True)
            acc = per_col[:, 0:SB_HEAD_DIM]
            for hh in range(1, reps):
                acc = acc + per_col[:, hh * SB_HEAD_DIM:(hh + 1) * SB_HEAD_DIM]
            dw_ref[...] += acc
            g = dn * jnp.tile(w_ref[...], (1, reps))
            dp_ref[:, idx * w:(idx + 1) * w] = (r * (g - xhat * _head_mean(g * xhat))).astype(BF16)
        dp_ref[:, 2 * w:3 * w] = dv_ref[...].astype(BF16)
        dp_ref[:, 3 * w:4 * w] = dg_ref[...]

    vec = _whole((1, SB_HEAD_DIM))
    return _ew_call(body, (jax.ShapeDtypeStruct((s, w4), BF16), jax.ShapeDtypeStruct((1, SB_HEAD_DIM), F32),
                           jax.ShapeDtypeStruct((1, SB_HEAD_DIM), F32)),
                    [_wide_rows(w, 0), _wide_rows(w, 1), vec, vec, _wide_rows(w), _wide_rows(w), _wide_rows(w),
                     _wide_rows(w)],
                    (_wide_rows(w4), vec, vec),
                    (proj, proj, qw.reshape(1, -1), kw.reshape(1, -1), dqn, dkn, dv, dg), s, name, carried=True)


_HBM = pl.BlockSpec(memory_space=pltpu.HBM)


def _mesh_pos():
    return lax.axis_index("x"), lax.axis_index("y"), lax.axis_index("c")


def _other_chips(x, y):
    return [(1 - x, y), (x, 1 - y), (1 - x, 1 - y)]


_SEM = pl.BlockSpec(memory_space=pltpu.SEMAPHORE)
_ANY = pl.BlockSpec(memory_space=pl.ANY)
_DATAFLOW = pltpu.SideEffectType.DATAFLOW_SIDE_EFFECTING
N_PEER_CHIPS = N_CHIP - 1
TOKEN_SHAPE = (8, LANES)


def _in_hbm(t):
    return pltpu.with_memory_space_constraint(t, pltpu.HBM)


def _ici_copies(kind, src_refs, land_refs, send_sems, recv_sems, arrivals=False):
    x, y, c = _mesh_pos()
    out = []
    for a in range(len(land_refs)):
        if kind in ("pass", "swap"):
            if kind == "pass":
                src, dst = land_refs[a].at[:, c], land_refs[a].at[:, 1 - c if arrivals else c]
            else:
                src, dst = src_refs[a].at[:, 1 - c], land_refs[a]
            out.append(pltpu.make_async_remote_copy(
                src_ref=src, dst_ref=dst, send_sem=send_sems.at[a], recv_sem=recv_sems.at[a],
                device_id=(x, y, 1 - c), device_id_type=MESH))
            continue
        for j, chip in enumerate(_other_chips(x, y)):
            if kind == "gather":
                src = land_refs[a].at[4 * x + 2 * y + c]
                dst = land_refs[a].at[4 * chip[0] + 2 * chip[1] + c] if arrivals else src
            else:
                src, dst = src_refs[a].at[2 * chip[0] + chip[1]], land_refs[a].at[j]
            k = a * N_PEER_CHIPS + j
            out.append(pltpu.make_async_remote_copy(
                src_ref=src, dst_ref=dst, send_sem=send_sems.at[k], recv_sem=recv_sems.at[k],
                device_id=(*chip, c), device_id_type=MESH))
    return out


def _n_copies(kind, lands):
    return len(lands) * (1 if kind in ("pass", "swap") else N_PEER_CHIPS)


def ici_start(kind, srcs, lands, after=(), *, name):
    ns, nb = len(srcs), len(srcs) + len(lands)
    n_sem = _n_copies(kind, lands)

    def body(*refs):
        first_out = nb + len(after)
        for cp in _ici_copies(kind, refs[:ns], refs[ns:nb], refs[first_out], refs[first_out + 1]):
            cp.start()
        refs[-1][...] = jnp.zeros(TOKEN_SHAPE, F32)

    outs = pl.pallas_call(
        body, name=name,
        out_shape=(pltpu.SemaphoreType.DMA((n_sem,)), pltpu.SemaphoreType.DMA((n_sem,)),
                   *[pltpu.HBM(t.shape, t.dtype) for t in (*srcs, *lands)], jax.ShapeDtypeStruct(TOKEN_SHAPE, F32)),
        in_specs=[_HBM] * nb + [_ANY] * len(after),
        out_specs=(_SEM, _SEM, *([_HBM] * nb), pl.BlockSpec(memory_space=pltpu.VMEM)),
        input_output_aliases={k: 2 + k for k in range(nb)},
        compiler_params=pltpu.CompilerParams(has_side_effects=_DATAFLOW),
    )(*[_in_hbm(t) for t in (*srcs, *lands)], *after)
    return outs[0], outs[1], list(outs[2:2 + ns]), list(outs[2 + ns:2 + nb]), outs[-1]


def ici_wait(kind, started, after, *, name):
    send_sems, recv_sems, srcs, lands, _ = started
    ns, nb = len(srcs), len(srcs) + len(lands)

    def body(*refs):
        for cp in _ici_copies(kind, refs[:ns], refs[ns:nb], refs[nb], refs[nb + 1]):
            cp.wait_send()
        for cp in _ici_copies(kind, refs[:ns], refs[ns:nb], refs[nb], refs[nb + 1], arrivals=True):
            cp.wait_recv()

    outs = pl.pallas_call(
        body, name=name,
        out_shape=tuple(pltpu.HBM(t.shape, t.dtype) for t in (*srcs, *lands)),
        in_specs=[_HBM] * nb + [_SEM, _SEM] + [_ANY] * len(after),
        out_specs=tuple([_HBM] * nb),
        input_output_aliases={k: k for k in range(nb)},
        compiler_params=pltpu.CompilerParams(has_side_effects=_DATAFLOW),
    )(*srcs, *lands, send_sems, recv_sems, *after)
    return list(outs[:ns]), list(outs[ns:])


def all_reduce_small(v, *, name):
    r = v.shape[0]

    def body(v_ref, o_ref, buf, send_sems, recv_sems):
        x, y, c = _mesh_pos()
        me = 4 * x + 2 * y + c
        buf[me] = v_ref[...]
        copies = []
        for k in range(1, N_DEV):
            to = ((x + (k >> 2)) % 2, (y + ((k >> 1) & 1)) % 2, (c + (k & 1)) % 2)
            copies.append(pltpu.make_async_remote_copy(
                src_ref=v_ref, dst_ref=buf.at[me], send_sem=send_sems.at[k - 1], recv_sem=recv_sems.at[k - 1],
                device_id=to, device_id_type=MESH))
        for cp in copies:
            cp.start()
        for cp in copies:
            cp.wait()
        acc = buf[0]
        for d in range(1, N_DEV):
            acc = acc + buf[d]
        o_ref[...] = acc

    vm = pl.BlockSpec(memory_space=pltpu.VMEM)
    return pl.pallas_call(
        body, out_shape=jax.ShapeDtypeStruct(v.shape, F32), in_specs=[vm], out_specs=vm,
        scratch_shapes=[pltpu.VMEM((N_DEV, r, LANES), F32), pltpu.SemaphoreType.DMA((N_DEV - 1,)),
                        pltpu.SemaphoreType.DMA((N_DEV - 1,))],
        name=name,
    )(v)


def pair_add(g, r1, core, *, name):
    _, _, rows, cols = g.shape
    tm = _pick(rows, (256, 128))

    def body(c_ref, g_ref, r_ref, o_ref):
        o_ref[...] = (g_ref[...].astype(F32) + r_ref[...].astype(F32)).astype(o_ref.dtype)

    return pl.pallas_call(
        body, out_shape=jax.ShapeDtypeStruct(r1.shape, g.dtype),
        grid_spec=pltpu.PrefetchScalarGridSpec(
            num_scalar_prefetch=1, grid=(N_CHIP, rows // tm),
            in_specs=[pl.BlockSpec((None, None, tm, cols), lambda k, i, c_ref: (k, c_ref[0], i, 0)),
                      pl.BlockSpec((None, tm, cols), lambda k, i, c_ref: (k, i, 0))],
            out_specs=pl.BlockSpec((None, tm, cols), lambda k, i, c_ref: (k, i, 0))),
        name=name, compiler_params=_cparams(("parallel", "parallel")),
    )(core, g, r1)


def _adamw_math(w, g, m, v):
    m = ADAM_B1 * m + (1.0 - ADAM_B1) * g
    v = ADAM_B2 * v + (1.0 - ADAM_B2) * (g * g)
    m_hat = m / (1.0 - ADAM_B1 ** ADAM_STEP)
    v_hat = v / (1.0 - ADAM_B2 ** ADAM_STEP)
    delta = -ADAM_LR * (m_hat / (jnp.sqrt(v_hat) + ADAM_EPS) + ADAM_WD * w)
    return delta, m, v


def adamw_sharded(w, m, v, layer, chip_sums, received, chip, into, *, name):
    _, rows, cols = w.shape
    tm = _pick(rows, (256, 128))

    def body(k_ref, w_ref, m_ref, v_ref, t_ref, r_ref, *rest):
        g_ref, d_ref, nm_ref, nv_ref = rest[-4:]
        g = t_ref[...].astype(F32)
        for j in range(N_CHIP - 1):
            g = g + r_ref[j].astype(F32)
        d, mm, vv = _adamw_math(w_ref[...], g, m_ref[...], v_ref[...])
        g_ref[...] = g
        d_ref[...] = d
        nm_ref[...] = mm
        nv_ref[...] = vv

    blk = pl.BlockSpec((None, tm, cols), lambda i, k_ref: (layer, i, 0))
    shp = jax.ShapeDtypeStruct(w.shape, F32)
    in_specs = [blk, blk, blk,
                pl.BlockSpec((None, tm, cols), lambda i, k_ref: (k_ref[0], i, 0)),
                pl.BlockSpec((N_CHIP - 1, tm, cols), lambda i, k_ref: (0, i, 0))]
    operands = [chip, w, m, v, chip_sums, received]
    aliases = {}
    if into is not None:
        aliases = {len(operands) + q: q for q in range(4)}
        in_specs += [_ANY] * 4
        operands += list(into)
    return pl.pallas_call(
        body, out_shape=(shp, shp, shp, shp),
        grid_spec=pltpu.PrefetchScalarGridSpec(
            num_scalar_prefetch=1, grid=(rows // tm,), in_specs=in_specs, out_specs=(blk, blk, blk, blk)),
        input_output_aliases=aliases,
        name=name, compiler_params=_cparams(("parallel",)),
    )(*operands)


def adamw_replicated(w, m, v, g, *, name):
    def body(w_ref, m_ref, v_ref, g_ref, d_ref, nm_ref, nv_ref):
        d, mm, vv = _adamw_math(w_ref[...], g_ref[...], m_ref[...], v_ref[...])
        d_ref[...] = d
        nm_ref[...] = mm
        nv_ref[...] = vv

    shp = jax.ShapeDtypeStruct(w.shape, F32)
    return pl.pallas_call(body, out_shape=(shp, shp, shp), name=name, compiler_params=_cparams())(w, m, v, g)


WEIGHT_NAMES = ("norm_w", "ssd_in_w", "ssd_conv_w", "ssd_conv_b", "ssd_dt_bias", "ssd_a_log", "ssd_d",
                "ssd_gnorm_w", "ssd_out_w", "sb_in_w", "sb_qn_w", "sb_kn_w", "sb_out_w", "ple_norm_w",
                "ple_gate_w", "ple_proj_w")
REPLICATED = ("norm_w", "ssd_conv_b", "ssd_dt_bias", "ssd_a_log", "ssd_d", "ssd_gnorm_w", "sb_qn_w", "sb_kn_w",
              "ple_norm_w")
PACK_ROWS = 8


def _pack(parts):
    flat = jnp.concatenate([t.reshape(-1) for t in parts])
    pad = (-flat.shape[0]) % (PACK_ROWS * LANES)
    return jnp.pad(flat, (0, pad)).reshape(-1, LANES)


def _unpack(packed, like):
    flat = packed.reshape(-1)
    out, off = [], 0
    for t in like:
        out.append(flat[off:off + t.size].reshape(t.shape))
        off += t.size
    return out


def _to_group_lanes(v, r):
    t = v.reshape(v.shape[0], SSD_N_GROUPS, r).transpose(1, 0, 2)
    return jnp.pad(t, ((0, 0), (0, 0), (0, LANES - r)))


def _from_group_lanes(t, r):
    return t[:, :, :r].transpose(1, 0, 2).reshape(t.shape[1], SSD_N_GROUPS * r)


def _head_vec(v, r):
    return jnp.pad(v.reshape(SSD_N_GROUPS, 1, r), ((0, 0), (0, 0), (0, LANES - r)))


def _col_blocks(full):
    rows = full.shape[0]
    return full.reshape(rows, N_DEV, -1).transpose(1, 0, 2)


def _from_col_blocks(blocks):
    return blocks.transpose(1, 0, 2).reshape(blocks.shape[1], -1)


def _split_cols(full, widths):
    out, off = [], 0
    for w in widths:
        out.append(full[:, off:off + w])
        off += w
    return out


def kernel(x, p, norm_w, ssd_in_w, ssd_conv_w, ssd_conv_b, ssd_dt_bias, ssd_a_log, ssd_d, ssd_gnorm_w, ssd_out_w, sb_in_w, sb_qn_w, sb_kn_w, sb_out_w, ple_norm_w, ple_gate_w, ple_proj_w, loss_target, m_norm_w, m_ssd_in_w, m_ssd_conv_w, m_ssd_conv_b, m_ssd_dt_bias, m_ssd_a_log, m_ssd_d, m_ssd_gnorm_w, m_ssd_out_w, m_sb_in_w, m_sb_qn_w, m_sb_kn_w, m_sb_out_w, m_ple_norm_w, m_ple_gate_w, m_ple_proj_w, v_norm_w, v_ssd_in_w, v_ssd_conv_w, v_ssd_conv_b, v_ssd_dt_bias, v_ssd_a_log, v_ssd_d, v_ssd_gnorm_w, v_ssd_out_w, v_sb_in_w, v_sb_qn_w, v_sb_kn_w, v_sb_out_w, v_ple_norm_w, v_ple_gate_w, v_ple_proj_w):
    env = dict(locals())
    wts = {n: env[n] for n in WEIGHT_NAMES}
    mom1 = {n: env["m_" + n] for n in WEIGHT_NAMES}
    mom2 = {n: env["v_" + n] for n in WEIGHT_NAMES}

    s, d = x.shape[1], x.shape[2]
    depth = norm_w.shape[0]
    n_ssd, n_sb = ssd_in_w.shape[0], sb_in_w.shape[0]
    di = ssd_out_w.shape[1] * N_DEV
    n_heads = ssd_dt_bias.shape[1]
    hpg = n_heads // SSD_N_GROUPS
    nbc = SSD_N_GROUPS * SSD_D_STATE
    in_segs = (di, di, nbc, nbc, n_heads)
    conv_segs = (di, nbc, nbc)
    sb_w = sb_out_w.shape[1] * N_DEV
    selectors = ssd_selectors(hpg)
    xi, yi, ci = _mesh_pos()
    core = ci.astype(jnp.int32).reshape(1)
    chip = (2 * xi + yi).astype(jnp.int32).reshape(1)

    def layer_keys(i):
        j = i // 2
        mixer = [("ssd_in_w", j), ("ssd_conv_w", j), ("ssd_out_w", j)] if i % 2 == 0 else [("sb_in_w", j), ("sb_out_w", j)]
        return mixer + [("ple_gate_w", i), ("ple_proj_w", i)]

    def shard_of(key):
        t = wts[key[0]][key[1]]
        return t if key[0] == "ssd_conv_w" else t.astype(BF16)

    me_block = 4 * xi + 2 * yi + ci

    def landing_zone(t):
        return lax.dynamic_update_index_in_dim(lax.empty((N_DEV,) + t.shape, t.dtype), t, me_block, 0)

    def groups(i):
        keys = layer_keys(i)
        return [keys[:2], keys[2:]] if i == 0 else [keys]

    gathers, prev = {}, []
    for i in range(depth):
        for q, keys in enumerate(groups(i)):
            gathers[i, q] = ici_start("gather", [], [landing_zone(shard_of(k)) for k in keys], after=prev,
                                      name=f"ag{i}{'ab'[q]}_start")
            prev = [gathers[i, q][4]]
    all_started = prev[0]
    full, ssd_full, passing = {}, {}, {}

    def hand_over(i, q, after):
        _, lands = ici_wait("gather", gathers[i, q], after, name=f"ag{i}{'ab'[q]}_wait")
        passing[i, q] = ici_start("pass", [], [t.reshape(N_CHIP, 2, *t.shape[1:]) for t in lands],
                                  name=f"ag{i}{'ab'[q]}_pass_start")

    def arrive(i, q, after):
        _, lands = ici_wait("pass", passing[i, q], after, name=f"ag{i}{'ab'[q]}_pass_wait")
        for k, t in zip(groups(i)[q], lands):
            full[k] = t.reshape(N_DEV, *t.shape[2:])

    def w_out_of(i):
        return full["ssd_out_w", i // 2].reshape(di, d) if i % 2 == 0 else full["sb_out_w", i // 2].reshape(sb_w, d)

    h = x.reshape(s, d)
    saved = []
    hand_over(0, 0, [all_started])
    arrive(0, 0, [all_started])
    for i in range(depth):
        j = i // 2
        if i > 0:
            arrive(i, 0, [h])
        sv = dict(h_in=h)
        u = rmsnorm_fwd(h, norm_w[i], name=f"l{i}_norm")
        sv["u"] = u
        if i % 2 == 0:
            fw = ssd_full[j] = dict(
                w_in=_split_cols(_from_col_blocks(full["ssd_in_w", j]), in_segs),
                conv_w=_split_cols(_from_col_blocks(full["ssd_conv_w", j]), conv_segs),
                conv_b=_split_cols(ssd_conv_b[j].reshape(1, -1), conv_segs))
            raw = [matmul(u, wseg, name=f"l{i}_in{q}") for q, wseg in enumerate(fw["w_in"])]
            if i == 0:
                hand_over(0, 1, [raw[4]])
            z, dt_raw = raw[0], raw[4]
            act = [ssd_conv_fwd(raw[1 + q], fw["conv_w"][q], fw["conv_b"][q], name=f"l{i}_conv{q}") for q in range(3)]
            dt, a_neg = ssd_dt_fwd(dt_raw, ssd_dt_bias[j], ssd_a_log[j], name=f"l{i}_dt")
            dtp = _to_group_lanes(dt, hpg)
            a_g = _head_vec(a_neg.reshape(-1), hpg)
            d_x = jnp.repeat(ssd_d[j].reshape(SSD_N_GROUPS, 1, hpg), SSD_HEAD_DIM, axis=2)
            y, states = ssd_scan_fwd(act[0], act[1], act[2], dtp, a_g, d_x, selectors, heads_per_group=hpg,
                                     name=f"l{i}_scan")
            yn = ssd_gate_fwd(y, z, ssd_gnorm_w[j], name=f"l{i}_gate")
            if i == 0:
                arrive(0, 1, [yn])
            h1 = matmul(yn, w_out_of(i), res=h, name=f"l{i}_out")
            sv.update(raw=raw, act=act, dt=dt, dtp=dtp, a_g=a_g, d_x=d_x, y=y, states=states, yn=yn)
        else:
            proj = matmul(u, full["sb_in_w", j], name=f"l{i}_in")
            qn, kn = sb_qk_fwd(proj, sb_qn_w[j], sb_kn_w[j], name=f"l{i}_qknorm")
            v_off = 2 * sb_w // SB_HEAD_DIM
            o, tot = sb_attn_fwd(qn, kn, proj, v_off=v_off, name=f"l{i}_attn")
            og = sb_gate_fwd(o, proj, name=f"l{i}_gate")
            h1 = matmul(og, w_out_of(i), res=h, name=f"l{i}_out")
            sv.update(proj=proj, qn=qn, kn=kn, o=o, tot=tot, og=og, v_off=v_off)
        if i + 1 < depth:
            hand_over(i + 1, 0, [h1])
        t = rmsnorm_fwd(h1, ple_norm_w[i], passing[i + 1, 0][4] if i + 1 < depth else None, name=f"l{i}_plenorm")
        gate_pre = matmul(t, full["ple_gate_w", i].reshape(d, d), name=f"l{i}_plegate")
        pp = matmul(p[i, 0], full["ple_proj_w", i], name=f"l{i}_pleproj")
        h = ple_fwd(h1, gate_pre, pp, name=f"l{i}_ple")
        sv.update(h1=h1, t=t, gate_pre=gate_pre, pp=pp)
        saved.append(sv)

    loss_part, dh = loss_head(h, loss_target.reshape(s, d), name="loss_head")
    loss = lax.psum(loss_part[0, 0], ("x", "y", "c"))

    big = {}
    small = {n: [None] * wts[n].shape[0] for n in REPLICATED}
    swaps, scatters = {}, {}
    order_after = jnp.zeros(TOKEN_SHAPE, F32)
    pending = None

    def send_to_sibling(i, q):
        blocks = [big[k].reshape(N_CHIP, 2, *big[k].shape[1:]) for k in groups(i)[::-1][q]]
        swaps[i, q] = ici_start("swap", blocks, [lax.empty((N_CHIP,) + t.shape[2:], t.dtype) for t in blocks],
                                name=f"rs{i}{'ab'[q]}_swap_start")
        return swaps[i, q][4]

    def send_to_chips(i, q, after):
        blocks, from_sibling = ici_wait("swap", swaps[i, q], after, name=f"rs{i}{'ab'[q]}_swap_wait")
        sums = [pair_add(g, r1, core, name=f"rs{i}{'ab'[q]}_pair_add{a}")
                for a, (g, r1) in enumerate(zip(blocks, from_sibling))]
        scatters[i, q] = ici_start("scatter", sums, [lax.empty((N_PEER_CHIPS,) + t.shape[1:], t.dtype) for t in sums],
                                   name=f"rs{i}{'ab'[q]}_start")
        return scatters[i, q][4]

    for i in reversed(range(depth)):
        j = i // 2
        sv = saved[i]
        dpp, dgp = ple_bwd(dh, sv["gate_pre"], sv["pp"], order_after, name=f"b{i}_ple")
        big["ple_proj_w", i] = matmul(p[i, 0], dpp, mode="tn", out_dtype=BF16, out_blocks=ple_proj_w.shape[2],
                                      name=f"b{i}_pleproj_w")
        big["ple_gate_w", i] = matmul(sv["t"], dgp, mode="tn", out_dtype=BF16, name=f"b{i}_plegate_w").reshape(N_DEV, -1, d)
        dt_ = matmul(dgp, full["ple_gate_w", i].reshape(d, d), mode="nt", name=f"b{i}_plegate_x")
        dh1, g_pn = rmsnorm_bwd(sv["h1"], ple_norm_w[i], dt_, dh, name=f"b{i}_plenorm")
        small["ple_norm_w"][i] = g_pn
        behind = send_to_chips(*pending, [dh1]) if pending is not None else None
        pending = None
        u = sv["u"]
        if i % 2 == 0:
            fw = ssd_full[j]
            raw, act = sv["raw"], sv["act"]
            big["ssd_out_w", j] = matmul(sv["yn"], dh1, mode="tn", out_dtype=BF16, name=f"b{i}_out_w").reshape(N_DEV, -1, d)
            if i == 0:
                send_to_sibling(0, 0)
            dyn = matmul(dh1, w_out_of(i), mode="nt", after=behind, name=f"b{i}_out_x")
            dy, dz, g_gn = ssd_gate_bwd(sv["y"], raw[0], ssd_gnorm_w[j], dyn, name=f"b{i}_gate")
            dxs, dbm, dcm, ddtp, dadtp, dd_g = ssd_scan_bwd(act[0], act[1], act[2], sv["dtp"], sv["a_g"], sv["d_x"], selectors,
                                                          sv["states"], dy, heads_per_group=hpg, name=f"b{i}_scan")
            behind = send_to_chips(0, 0, [dxs]) if i == 0 else None
            ddt_raw, g_dtb, g_alog = ssd_dt_bwd(raw[4], ssd_dt_bias[j], ssd_a_log[j], sv["dt"],
                                                _from_group_lanes(ddtp, hpg), _from_group_lanes(dadtp, hpg),
                                                name=f"b{i}_dt")
            conv_back = [ssd_conv_bwd(raw[1 + q], fw["conv_w"][q], fw["conv_b"][q], dact, name=f"b{i}_conv{q}")
                         for q, dact in enumerate((dxs, dbm, dcm))]
            dsegs = [dz] + [cb[0] for cb in conv_back] + [ddt_raw]
            g_in = jnp.concatenate([matmul(u, ds, mode="tn", out_dtype=BF16, name=f"b{i}_in{q}_w")
                                    for q, ds in enumerate(dsegs)], axis=1)
            big["ssd_in_w", j] = _col_blocks(g_in)
            big["ssd_conv_w", j] = _col_blocks(jnp.concatenate([cb[1] for cb in conv_back], axis=1))
            du = None
            for q, (ds, wseg) in enumerate(zip(dsegs, fw["w_in"])):
                du = matmul(ds, wseg, mode="nt", res=du, after=behind if q == 0 else None, name=f"b{i}_in{q}_x")
            small["ssd_conv_b"][j] = jnp.concatenate([cb[2] for cb in conv_back], axis=1)
            small["ssd_dt_bias"][j] = g_dtb
            small["ssd_a_log"][j] = g_alog
            small["ssd_d"][j] = dd_g[:, 0, :hpg]
            small["ssd_gnorm_w"][j] = g_gn
        else:
            proj = sv["proj"]
            big["sb_out_w", j] = matmul(sv["og"], dh1, mode="tn", out_dtype=BF16, name=f"b{i}_out_w").reshape(N_DEV, -1, d)
            dog = matmul(dh1, w_out_of(i), mode="nt", after=behind, name=f"b{i}_out_x")
            do, dg = sb_gate_bwd(dog, sv["o"], proj, name=f"b{i}_gate")
            dqn, dkn, dv = sb_attn_bwd(sv["qn"], sv["kn"], proj, sv["tot"], do, v_off=sv["v_off"], name=f"b{i}_attn")
            dproj, g_qn, g_kn = sb_pack_bwd(proj, sb_qn_w[j], sb_kn_w[j], dqn, dkn, dv, dg, name=f"b{i}_qknorm")
            big["sb_in_w", j] = matmul(u, dproj, mode="tn", out_dtype=BF16, out_blocks=sb_in_w.shape[2], name=f"b{i}_in_w")
            du = matmul(dproj, full["sb_in_w", j], mode="nt", name=f"b{i}_in_x")
            small["sb_qn_w"][j] = g_qn
            small["sb_kn_w"][j] = g_kn
        dh, g_n = rmsnorm_bwd(sv["h_in"], norm_w[i], du, dh1, name=f"b{i}_norm")
        small["norm_w"][i] = g_n
        pending = (i, len(groups(i)) - 1)
        order_after = send_to_sibling(*pending)
    send_to_chips(*pending, [dh])
    grad_x = dh.reshape(x.shape)

    rep_like = [wts[n] for n in REPLICATED]
    g_packed = all_reduce_small(_pack([jnp.stack([t.reshape(-1) for t in small[n]]) for n in REPLICATED]),
                                name="all_reduce_small_grads")
    d_packed, m_packed, v_packed = adamw_replicated(
        _pack(rep_like), _pack([mom1[n] for n in REPLICATED]), _pack([mom2[n] for n in REPLICATED]), g_packed,
        name="adamw_replicated")
    grads = dict(zip(REPLICATED, _unpack(g_packed, rep_like)))
    deltas = dict(zip(REPLICATED, _unpack(d_packed, rep_like)))
    new_m = dict(zip(REPLICATED, _unpack(m_packed, rep_like)))
    new_v = dict(zip(REPLICATED, _unpack(v_packed, rep_like)))

    updated = {}
    after = [scatters[0, len(groups(0)) - 1][4]]
    for i in reversed(range(depth)):
        for q, keys in enumerate(groups(i)[::-1]):
            sums, received = ici_wait("scatter", scatters[i, q], after, name=f"rs{i}{'ab'[q]}_wait")
            for (n, idx), t_sum, recv in zip(keys, sums, received):
                updated[n] = adamw_sharded(wts[n], mom1[n], mom2[n], idx, t_sum, recv, chip, updated.get(n),
                                           name=f"adamw_{n}{idx}")
            after = [updated[n][0] for n, _ in keys]
    for n, (g_n, d_n, m_n, v_n) in updated.items():
        grads[n], deltas[n], new_m[n], new_v[n] = g_n, d_n, m_n, v_n

    return (loss, grad_x, *[grads[n] for n in WEIGHT_NAMES], *[deltas[n] for n in WEIGHT_NAMES],
            *[new_m[n] for n in WEIGHT_NAMES], *[new_v[n] for n in WEIGHT_NAMES])
```

```python
import functools
import math

import jax
import jax.numpy as jnp
from jax import lax
from jax.experimental import pallas as pl
from jax.experimental.pallas import tpu as pltpu

F32 = jnp.float32
BF16 = jnp.bfloat16
MESH = pl.DeviceIdType.MESH

N_DEV = 8
N_CHIP = 4
LANES = 128
VMEM_LIMIT_BYTES = 48 * 1024 * 1024
MATMUL_TILE_BYTES = 28 * 1024 * 1024

NORM_EPS = 1e-6
GATED_NORM_EPS = 1e-5
SSD_HEAD_DIM = 64
SSD_N_GROUPS = 8
SSD_D_STATE = 128
SSD_D_CONV = 4
SSD_CHUNK = 128
SB_HEAD_DIM = 128
PLE_DIM = 256

ADAM_LR = 0.001
ADAM_B1 = 0.9
ADAM_B2 = 0.999
ADAM_EPS = 1e-08
ADAM_WD = 0.01
ADAM_STEP = 10


def _cparams(sem=None, **kw):
    return pltpu.CompilerParams(dimension_semantics=sem, vmem_limit_bytes=VMEM_LIMIT_BYTES, **kw)


def _pick(dim, prefs):
    for t in prefs:
        if dim % t == 0:
            return t
    return dim


def _sigmoid(x):
    return 1.0 / (1.0 + jnp.exp(-x))


def _silu(x):
    return x * _sigmoid(x)


def _silu_grad(x):
    s = _sigmoid(x)
    return s * (1.0 + x * (1.0 - s))


def matmul(a, b, *, mode="nn", out_dtype=F32, res=None, out_blocks=None, after=None, name):
    b_blocked = b.ndim == 3
    if mode == "nn":
        m, kc = a.shape
        n = b.shape[-1] * (N_DEV if b_blocked else 1)
    elif mode == "nt":
        m, kc = a.shape
        n = b.shape[-2]
    else:
        kc, m = a.shape
        n = b.shape[-1]
    nb = b.shape[-1] if b_blocked else None
    tn = _pick(n if not out_blocks else out_blocks, (512, 256, 128))
    if b_blocked and mode == "nn":
        tn = _pick(nb, (512, 256, 128))
    k_limit = nb if (b_blocked and mode == "nt") else kc
    tm, tk = None, None
    for tm_try in (1024, 512, 256, 128):
        if m % tm_try:
            continue
        for tk_try in (k_limit, 2048, 1024, 512, 256, 128):
            if tk_try > k_limit or k_limit % tk_try:
                continue
            tiles = 2 * (tm_try * tk_try * a.dtype.itemsize + tk_try * tn * b.dtype.itemsize)
            tiles += tm_try * tn * (2 * jnp.dtype(out_dtype).itemsize + 4 + (8 if res is not None else 0))
            if tiles <= MATMUL_TILE_BYTES:
                tm, tk = tm_try, tk_try
                break
        if tm:
            break
    if tm is None:
        tm, tk = m, k_limit
    nk = kc // tk
    grid = (m // tm, n // tn, nk)

    if mode == "tn":
        a_spec = pl.BlockSpec((tk, tm), lambda i, j, k: (k, i))
        dims = (((0,), (0,)), ((), ()))
    else:
        a_spec = pl.BlockSpec((tm, tk), lambda i, j, k: (i, k))
        dims = (((1,), (0,)), ((), ())) if mode == "nn" else (((1,), (1,)), ((), ()))
    if mode == "nt":
        if b_blocked:
            per = nb // tk
            b_spec = pl.BlockSpec((None, tn, tk), lambda i, j, k: (k // per, j, k % per))
        else:
            b_spec = pl.BlockSpec((tn, tk), lambda i, j, k: (j, k))
    else:
        if b_blocked:
            per = nb // tn
            b_spec = pl.BlockSpec((None, tk, tn), lambda i, j, k: (j // per, k, j % per))
        else:
            b_spec = pl.BlockSpec((tk, tn), lambda i, j, k: (k, j))
    if out_blocks:
        per_o = out_blocks // tn
        out_shape = jax.ShapeDtypeStruct((n // out_blocks, m, out_blocks), out_dtype)
        out_spec = pl.BlockSpec((None, tm, tn), lambda i, j, k: (j // per_o, i, j % per_o))
    else:
        out_shape = jax.ShapeDtypeStruct((m, n), out_dtype)
        out_spec = pl.BlockSpec((tm, tn), lambda i, j, k: (i, j))
    in_specs = [a_spec, b_spec]
    args = [a, b]
    if res is not None:
        in_specs.append(pl.BlockSpec((tm, tn), lambda i, j, k: (i, j)))
        args.append(res)
    if after is not None:
        in_specs.append(pl.BlockSpec(memory_space=pl.ANY))
        args.append(after)
    n_in = len(args)

    def body(*refs):
        a_ref, b_ref = refs[:2]
        r_ref = refs[2] if res is not None else None
        o_ref = refs[n_in]

        def finish(r):
            if res is not None:
                r = r + r_ref[...].astype(F32)
            o_ref[...] = r.astype(out_dtype)

        part = lax.dot_general(a_ref[...].astype(BF16), b_ref[...].astype(BF16), dims, preferred_element_type=F32)
        if nk == 1:
            finish(part)
            return
        acc_ref = refs[-1]
        k = pl.program_id(2)

        @pl.when(k == 0)
        def _():
            acc_ref[...] = part

        @pl.when(k > 0)
        def _():
            acc_ref[...] += part

        @pl.when(k == nk - 1)
        def _():
            finish(acc_ref[...])

    return pl.pallas_call(
        body, out_shape=out_shape, grid=grid, in_specs=in_specs, out_specs=out_spec,
        scratch_shapes=[] if nk == 1 else [pltpu.VMEM((tm, tn), F32)], name=name,
        compiler_params=_cparams(("parallel", "parallel", "arbitrary")),
    )(*args)


def _dot(a, b, dims, precision=None):
    return lax.dot_general(a, b, (dims, ((), ())), preferred_element_type=F32, precision=precision)


_NN = ((1,), (0,))
_NT = ((1,), (1,))
_TN = ((0,), (0,))
_EXACT = lax.Precision.HIGHEST


def _chunk_decay_terms(dt, a):
    ln = dt.shape[0]
    row = lax.broadcasted_iota(jnp.int32, (ln, ln), 0)
    col = lax.broadcasted_iota(jnp.int32, (ln, ln), 1)
    tri = (row >= col).astype(F32)
    a_col = _dot(tri, dt * a, _NN, _EXACT)
    return a_col, a_col.T, row >= col


def _exact_dot(x, sel, terms):
    t = x.shape[0]
    parts, rest = [], x
    for k in range(terms):
        piece = rest.astype(BF16)
        parts.append(piece)
        if k + 1 < terms:
            rest = rest - piece.astype(F32)
    r = _dot(jnp.concatenate(parts, axis=0), sel, _NN)
    out = r[:t]
    for k in range(1, terms):
        out = out + r[k * t:(k + 1) * t]
    return out


def ssd_selectors(r_n):
    lane = jnp.arange(LANES)
    spread64 = (lane[:, None] == jnp.arange(r_n * SSD_HEAD_DIM)[None, :] // SSD_HEAD_DIM).astype(BF16)
    pair_sum = jnp.stack([lane[None, :] == 2 * q + lane[:, None] // SSD_HEAD_DIM for q in range(r_n // 2)]).astype(BF16)
    row_sum = jnp.stack([jnp.broadcast_to(lane[None, :] == r, (LANES, LANES)) for r in range(r_n)]).astype(BF16)
    return spread64, pair_sum, row_sum


def _ssd_chunk_setup(dt, a, spread64):
    ln = dt.shape[0]
    a_col, a_row, causal = _chunk_decay_terms(dt, a)
    ea = jnp.exp(a_col)
    te = jnp.exp(a_col[ln - 1:ln, :] - a_col)
    return (a_row, a_col, _exact_dot(dt, spread64, 2), _exact_dot(ea, spread64, 2), _exact_dot(te, spread64, 2),
            ea, causal)


def ssd_scan_fwd(xs, bm, cm, dtp, a_g, d_x, selectors, *, heads_per_group, name):
    s, di = xs.shape
    g_n = SSD_N_GROUPS
    r_n, p_n, n_n, ln = heads_per_group, SSD_HEAD_DIM, SSD_D_STATE, SSD_CHUNK
    nc = s // ln
    pairs, pw = r_n // 2, 2 * p_n
    spread64 = selectors[0]

    def body(xs_ref, bm_ref, cm_ref, dt_ref, a_ref, d_ref, s64_ref, y_ref, st_ref, state):
        c = pl.program_id(1)

        @pl.when(c == 0)
        def _():
            state[...] = jnp.zeros_like(state)

        a_row, a_col, dt_x, ea_x, te_x, _, causal = _ssd_chunk_setup(dt_ref[...], a_ref[...], s64_ref[...])
        bm_f = bm_ref[...]
        bmb = bm_f.astype(BF16)
        bm_t = bm_f.T.astype(BF16)
        cmb = cm_ref[...].astype(BF16)
        scores = _dot(cmb, bmb, _NT)
        first_head = lax.broadcasted_iota(jnp.int32, (1, pw), 1) < p_n
        for q in range(pairs):
            sl = slice(q * pw, (q + 1) * pw)
            x2 = xs_ref[:, sl]
            xdt2 = x2 * dt_x[:, sl]
            xdt2b = xdt2.astype(BF16)
            y_heads = []
            for r in (2 * q, 2 * q + 1):
                decay = jnp.exp(jnp.where(causal, a_col[:, r:r + 1] - a_row[r:r + 1, :], -jnp.inf))
                y_heads.append(_dot((scores * decay).astype(BF16), xdt2b, _NN))
            s2t = state[q]
            st_ref[q] = s2t
            y2 = jnp.where(first_head, y_heads[0], y_heads[1])
            y2 = y2 + ea_x[:, sl] * _dot(cmb, s2t.astype(BF16), _NN)
            y_ref[:, sl] = y2 + d_ref[:, sl] * x2
            state[q] = s2t * ea_x[ln - 1:ln, sl] + _dot(bm_t, (xdt2 * te_x[:, sl]).astype(BF16), _NN)

    whole = lambda t: pl.BlockSpec(t.shape, lambda g, c: (0,) * t.ndim)
    return pl.pallas_call(
        body,
        out_shape=(jax.ShapeDtypeStruct((s, di), F32),
                   jax.ShapeDtypeStruct((nc, g_n * pairs, n_n, pw), F32)),
        grid=(g_n, nc),
        in_specs=[pl.BlockSpec((ln, r_n * p_n), lambda g, c: (c, g)),
                  pl.BlockSpec((ln, n_n), lambda g, c: (c, g)),
                  pl.BlockSpec((ln, n_n), lambda g, c: (c, g)),
                  pl.BlockSpec((None, ln, LANES), lambda g, c: (g, c, 0)),
                  pl.BlockSpec((None, 1, LANES), lambda g, c: (g, 0, 0)),
                  pl.BlockSpec((None, 1, r_n * p_n), lambda g, c: (g, 0, 0)),
                  whole(spread64)],
        out_specs=(pl.BlockSpec((ln, r_n * p_n), lambda g, c: (c, g)),
                   pl.BlockSpec((None, pairs, n_n, pw), lambda g, c: (c, g, 0, 0))),
        scratch_shapes=[pltpu.VMEM((pairs, n_n, pw), F32)],
        name=name, compiler_params=_cparams(("parallel", "arbitrary")),
    )(xs, bm, cm, dtp, a_g, d_x, spread64)


def _row8(v):
    return jnp.broadcast_to(v, (8, v.shape[1]))


def ssd_scan_bwd(xs, bm, cm, dtp, a_g, d_x, selectors, states, dy, *, heads_per_group, name):
    s, di = xs.shape
    g_n = SSD_N_GROUPS
    r_n, p_n, n_n, ln = heads_per_group, SSD_HEAD_DIM, SSD_D_STATE, SSD_CHUNK
    nc = s // ln
    pairs, pw = r_n // 2, 2 * p_n
    spread64, pair_sum, row_sum = selectors

    def body(xs_ref, bm_ref, cm_ref, dt_ref, a_ref, d_ref, s64_ref, ps_ref, rs_ref, st_ref, dy_ref,
             dxs_ref, dbm_ref, dcm_ref, ddt_ref, dadt_ref, dd_ref, dstate, da_rows):
        c = pl.program_id(1)

        @pl.when(c == 0)
        def _():
            dstate[...] = jnp.zeros_like(dstate)
            dd_ref[...] = jnp.zeros_like(dd_ref)

        a_row, a_col, dt_x, ea_x, te_x, ea, causal = _ssd_chunk_setup(dt_ref[...], a_ref[...], s64_ref[...])
        row = lax.broadcasted_iota(jnp.int32, (ln, ln), 0)
        col = lax.broadcasted_iota(jnp.int32, (ln, ln), 1)
        causal_t = col >= row
        bmb = bm_ref[...].astype(BF16)
        cm_f = cm_ref[...]
        cmb = cm_f.astype(BF16)
        cm_t = cm_f.T.astype(BF16)
        scores = _dot(cmb, bmb, _NT)
        scores_t = _dot(bmb, cmb, _NT)
        first_head = lax.broadcasted_iota(jnp.int32, (1, pw), 1) < p_n
        e_last = ea[ln - 1:ln, :]
        da_rows[...] = jnp.zeros_like(da_rows)
        dscores = jnp.zeros((ln, ln), F32)
        dcm = jnp.zeros((ln, n_n), F32)
        dbm = jnp.zeros((ln, n_n), F32)
        da_cols = jnp.zeros((ln, LANES), F32)
        da_last = jnp.zeros((1, LANES), F32)
        ddt = jnp.zeros((ln, LANES), F32)
        dd = jnp.zeros((1, LANES), F32)
        for q in range(pairs):
            sl = slice(q * pw, (q + 1) * pw)
            sum2 = ps_ref[q]
            x2 = xs_ref[:, sl]
            dt2 = dt_x[:, sl]
            xdt2 = x2 * dt2
            xdt2b = xdt2.astype(BF16)
            dy2 = dy_ref[:, sl]
            dy2b = dy2.astype(BF16)
            dxdt_heads = []
            for h, r in enumerate((2 * q, 2 * q + 1)):
                a_r = jnp.broadcast_to(a_col[:, r:r + 1], (ln, ln))
                decay = jnp.exp(jnp.where(causal, a_r - a_row[r:r + 1, :], -jnp.inf))
                decay_t = jnp.exp(jnp.where(causal_t, a_row[r:r + 1, :] - a_r, -jnp.inf))
                dy_h = jnp.where(first_head if h == 0 else jnp.logical_not(first_head), dy2, 0.0).astype(BF16)
                dm = _dot(dy_h, xdt2b, _NT)
                dscores = dscores + dm * decay
                e_mat = dm * (scores * decay)
                da_cols = da_cols + _exact_dot(e_mat, rs_ref[r], 2)
                da_rows[r:r + 1, :] = -jnp.sum(e_mat, axis=0, keepdims=True)
                dxdt_heads.append(_dot((scores_t * decay_t).astype(BF16), dy2b, _NN))
            dxdt2 = jnp.where(first_head, dxdt_heads[0], dxdt_heads[1])
            s2t = st_ref[q]
            s2tb = s2t.astype(BF16)
            ds2t = dstate[q]
            ds2tb = ds2t.astype(BF16)
            ea2, te2 = ea_x[:, sl], te_x[:, sl]
            y_off2 = ea2 * _dot(cmb, s2tb, _NN)
            dy_e2 = (dy2 * ea2).astype(BF16)
            dcm = dcm + _dot(dy_e2, s2tb, _NT)
            ds_in = _dot(cm_t, dy_e2, _NN)
            da_cols = da_cols + _exact_dot(dy2 * y_off2, sum2, 2)
            bds2 = _dot(bmb, ds2tb, _NN)
            dxdt2 = dxdt2 + te2 * bds2
            xdt_e2 = xdt2 * te2
            dbm = dbm + _dot(xdt_e2.astype(BF16), ds2tb, _NT)
            w_cols = _exact_dot(xdt_e2 * bds2, sum2, 2)
            da_cols = da_cols - w_cols
            state_dot = _exact_dot(_row8(jnp.sum(ds2t * s2t, axis=0, keepdims=True)), sum2, 2)[0:1]
            da_last = da_last + jnp.sum(w_cols, axis=0, keepdims=True) + e_last * state_dot
            dstate[q] = ds2t * ea_x[ln - 1:ln, sl] + ds_in
            dxs_ref[:, sl] = dxdt2 * dt2 + d_ref[:, sl] * dy2
            ddt = ddt + _exact_dot(dxdt2 * x2, sum2, 2)
            dd = dd + _exact_dot(_row8(jnp.sum(dy2 * x2, axis=0, keepdims=True)), sum2, 2)[0:1]
        dcm_ref[...] = dcm + _dot(dscores.astype(BF16), bmb, _NN)
        dbm_ref[...] = dbm + _dot(dscores.T.astype(BF16), cmb, _NN)
        da_total = da_cols + da_rows[...].T
        upper = causal_t.astype(F32)
        dadt_ref[...] = _dot(upper, da_total, _NN, _EXACT) + da_last
        ddt_ref[...] = ddt
        dd_ref[...] += dd

    last_c = nc - 1
    whole = lambda t: pl.BlockSpec(t.shape, lambda g, c: (0,) * t.ndim)
    return pl.pallas_call(
        body,
        out_shape=(jax.ShapeDtypeStruct((s, di), F32),
                   jax.ShapeDtypeStruct(bm.shape, F32),
                   jax.ShapeDtypeStruct(cm.shape, F32),
                   jax.ShapeDtypeStruct(dtp.shape, F32),
                   jax.ShapeDtypeStruct(dtp.shape, F32),
                   jax.ShapeDtypeStruct(a_g.shape, F32)),
        grid=(g_n, nc),
        in_specs=[pl.BlockSpec((ln, r_n * p_n), lambda g, c: (last_c - c, g)),
                  pl.BlockSpec((ln, n_n), lambda g, c: (last_c - c, g)),
                  pl.BlockSpec((ln, n_n), lambda g, c: (last_c - c, g)),
                  pl.BlockSpec((None, ln, LANES), lambda g, c: (g, last_c - c, 0)),
                  pl.BlockSpec((None, 1, LANES), lambda g, c: (g, 0, 0)),
                  pl.BlockSpec((None, 1, r_n * p_n), lambda g, c: (g, 0, 0)),
                  whole(spread64), whole(pair_sum), whole(row_sum),
                  pl.BlockSpec((None, pairs, n_n, pw), lambda g, c: (last_c - c, g, 0, 0)),
                  pl.BlockSpec((ln, r_n * p_n), lambda g, c: (last_c - c, g))],
        out_specs=(pl.BlockSpec((ln, r_n * p_n), lambda g, c: (last_c - c, g)),
                   pl.BlockSpec((ln, n_n), lambda g, c: (last_c - c, g)),
                   pl.BlockSpec((ln, n_n), lambda g, c: (last_c - c, g)),
                   pl.BlockSpec((None, ln, LANES), lambda g, c: (g, last_c - c, 0)),
                   pl.BlockSpec((None, ln, LANES), lambda g, c: (g, last_c - c, 0)),
                   pl.BlockSpec((None, 1, LANES), lambda g, c: (g, 0, 0))),
        scratch_shapes=[pltpu.VMEM((pairs, n_n, pw), F32), pltpu.VMEM((LANES, ln), F32)],
        name=name, compiler_params=_cparams(("parallel", "arbitrary")),
    )(xs, bm, cm, dtp, a_g, d_x, spread64, pair_sum, row_sum, states, dy)


SB_Q_TILE = 1024
SB_K_TILE = 256


def _tri_sum(x, tri):
    t = x.shape[0]
    hi = x.astype(BF16)
    r1 = x - hi.astype(F32)
    mid = r1.astype(BF16)
    lo = (r1 - mid.astype(F32)).astype(BF16)
    r = _dot(jnp.concatenate([hi, mid, lo], axis=0), tri, _NN)
    return r[:t] + r[t:2 * t] + r[2 * t:]


def _sb_logits(q, k_j, scale, strict):
    z = _dot(q, k_j, _NT) * scale
    sp = jnp.log(1.0 + jnp.exp(-jnp.abs(z)))
    log_b = jnp.minimum(z, 0.0) - sp
    log_1mb = log_b - z
    if strict is not None:
        log_1mb = jnp.where(strict, log_1mb, 0.0)
    return log_b, log_1mb


def _sb_tiles(s):
    tq = _pick(s, (SB_Q_TILE, 2 * SB_K_TILE, SB_K_TILE, LANES))
    return tq, min(tq, SB_K_TILE)


def _sb_diag_mask(rows, tk):
    return lax.broadcasted_iota(jnp.int32, (rows, tk), 1) < lax.broadcasted_iota(jnp.int32, (rows, tk), 0)


def _sb_iotas(t):
    row = lax.broadcasted_iota(jnp.int32, (t, t), 0)
    col = lax.broadcasted_iota(jnp.int32, (t, t), 1)
    return row, col


def sb_attn_fwd(qn, kn, v, *, v_off=0, name):
    s, w = qn.shape
    dh = SB_HEAD_DIM
    n_h = w // dh
    tq, tk = _sb_tiles(s)
    per = tq // tk
    scale = 1.0 / math.sqrt(dh)

    def body(q_ref, k_ref, v_ref, o_ref, tot_ref):
        i = pl.program_id(1)
        q = q_ref[...]
        row, col = _sb_iotas(tk)
        later = (row > col).astype(BF16)

        def tile(q_rows, j, acc, run, mask):
            s0 = pl.multiple_of(j * tk, tk)
            k_j = k_ref[pl.ds(s0, tk), :]
            v_j = v_ref[pl.ds(s0, tk), :].astype(BF16)
            log_b, log_1mb = _sb_logits(q_rows, k_j, scale, mask)
            att = jnp.exp(log_b + (_tri_sum(log_1mb, later) + run))
            if mask is not None:
                att = jnp.where(mask, att, 0.0)
            return acc + _dot(att.astype(BF16), v_j, _NN), run + jnp.sum(log_1mb, axis=1, keepdims=True)

        acc, run = jnp.zeros((tq, dh), F32), jnp.zeros((tq, 1), F32)
        for d in reversed(range(per)):
            r0 = d * tk
            a2, r2 = tile(q[r0:], i * per + d, acc[r0:], run[r0:], _sb_diag_mask(tq - r0, tk))
            acc = a2 if r0 == 0 else jnp.concatenate([acc[:r0], a2], axis=0)
            run = r2 if r0 == 0 else jnp.concatenate([run[:r0], r2], axis=0)

        def group(gg, c):
            for d in reversed(range(per)):
                c = tile(q, (i - 1 - gg) * per + d, c[0], c[1], None)
            return c

        acc, run = lax.fori_loop(0, i, group, (acc, run))
        o_ref[...] = acc
        tot_ref[...] = jnp.broadcast_to(run, (tq, dh))

    return pl.pallas_call(
        body,
        out_shape=(jax.ShapeDtypeStruct((s, w), F32), jax.ShapeDtypeStruct((s, w), F32)),
        grid=(n_h, s // tq),
        in_specs=[pl.BlockSpec((tq, dh), lambda h, i: (i, h)),
                  pl.BlockSpec((s, dh), lambda h, i: (0, h)),
                  pl.BlockSpec((s, dh), lambda h, i: (0, v_off + h))],
        out_specs=(pl.BlockSpec((tq, dh), lambda h, i: (i, h)),
                   pl.BlockSpec((tq, dh), lambda h, i: (i, h))),
        name=name, compiler_params=_cparams(("parallel", "parallel")),
    )(qn, kn, v)


def sb_attn_bwd(qn, kn, v, tot, do, *, v_off=0, name):
    s, w = qn.shape
    dh = SB_HEAD_DIM
    n_h = w // dh
    tq, tk = _sb_tiles(s)
    per = tq // tk
    scale = 1.0 / math.sqrt(dh)

    def body(q_ref, k_ref, v_ref, tot_ref, do_ref, dq_ref, dk_ref, dv_ref):
        dk_ref[...] = jnp.zeros_like(dk_ref)
        dv_ref[...] = jnp.zeros_like(dv_ref)
        row, col = _sb_iotas(tk)
        upto = (row <= col).astype(BF16)
        before = (row < col).astype(BF16)

        def q_block(i, _):
            t0 = pl.multiple_of(i * tq, tq)
            q = q_ref[pl.ds(t0, tq), :]
            do_i = do_ref[pl.ds(t0, tq), :].astype(BF16)
            total = tot_ref[pl.ds(t0, tq), 0:1]

            def tile(r0, j, dq, run_l, run_g, mask):
                s0 = pl.multiple_of(j * tk, tk)
                k_j = k_ref[pl.ds(s0, tk), :]
                v_j = v_ref[pl.ds(s0, tk), :].astype(BF16)
                q_r, do_r = q[r0:], do_i[r0:]
                log_b, log_1mb = _sb_logits(q_r, k_j, scale, mask)
                att = jnp.exp(log_b + ((total[r0:] - run_l) - _tri_sum(log_1mb, upto)))
                if mask is not None:
                    att = jnp.where(mask, att, 0.0)
                g = att * _dot(do_r, v_j, _NT)
                c = _tri_sum(g, before) + run_g
                dz = (g - (g + c) * jnp.exp(log_b)) * scale
                if mask is not None:
                    dz = jnp.where(mask, dz, 0.0)
                dz = dz.astype(BF16)
                dk_ref[pl.ds(s0, tk), :] += _dot(dz, q_r, _TN)
                dv_ref[pl.ds(s0, tk), :] += _dot(att.astype(BF16), do_r, _TN)
                return (dq + _dot(dz, k_j, _NN), run_l + jnp.sum(log_1mb, axis=1, keepdims=True),
                        run_g + jnp.sum(g, axis=1, keepdims=True))

            def group(gg, c):
                for d in range(per):
                    c = tile(0, gg * per + d, c[0], c[1], c[2], None)
                return c

            zero = jnp.zeros((tq, 1), F32)
            dq, run_l, run_g = lax.fori_loop(0, i, group, (jnp.zeros((tq, dh), F32), zero, zero))
            for d in range(per):
                r0 = d * tk
                p_dq, p_l, p_g = tile(r0, i * per + d, dq[r0:], run_l[r0:], run_g[r0:], _sb_diag_mask(tq - r0, tk))
                if r0 == 0:
                    dq, run_l, run_g = p_dq, p_l, p_g
                else:
                    dq = jnp.concatenate([dq[:r0], p_dq], axis=0)
                    run_l = jnp.concatenate([run_l[:r0], p_l], axis=0)
                    run_g = jnp.concatenate([run_g[:r0], p_g], axis=0)
            dq_ref[pl.ds(t0, tq), :] = dq
            return 0

        lax.fori_loop(0, s // tq, q_block, 0)

    head = pl.BlockSpec((s, dh), lambda h: (0, h))
    return pl.pallas_call(
        body,
        out_shape=tuple(jax.ShapeDtypeStruct((s, w), F32) for _ in range(3)),
        grid=(n_h,),
        in_specs=[head, head, pl.BlockSpec((s, dh), lambda h: (0, v_off + h)), head, head],
        out_specs=(head, head, head),
        name=name, compiler_params=_cparams(("parallel",)),
    )(qn, kn, v, tot, do)


ROW_TILE = 256
WIDE_ROW_TILE = 64


def _rows(width, col=0, tm=ROW_TILE):
    return pl.BlockSpec((tm, width), lambda i: (i, col))


_wide_rows = functools.partial(_rows, tm=WIDE_ROW_TILE)


def _whole(shape):
    return pl.BlockSpec(shape, lambda i: (0,) * len(shape))


def _ew_call(body, out_shape, in_specs, out_specs, args, n_rows, name, carried=False):
    return pl.pallas_call(
        body, out_shape=out_shape, grid=(n_rows // in_specs[0].block_shape[0],), in_specs=in_specs, out_specs=out_specs,
        name=name, compiler_params=_cparams(("arbitrary",) if carried else ("parallel",)),
    )(*args)


def _first_step(*refs):
    @pl.when(pl.program_id(0) == 0)
    def _():
        for r in refs:
            r[...] = jnp.zeros_like(r)


def rmsnorm_fwd(x, w, after=None, *, name):
    s, d = x.shape

    def body(x_ref, w_ref, *rest):
        o_ref = rest[-1]
        xv = x_ref[...]
        r = lax.rsqrt(jnp.mean(xv * xv, axis=-1, keepdims=True) + NORM_EPS)
        o_ref[...] = (xv * r * w_ref[...]).astype(BF16)

    extra = [] if after is None else [after]
    return _ew_call(body, jax.ShapeDtypeStruct((s, d), BF16),
                    [_rows(d), _whole((1, d))] + [_whole(TOKEN_SHAPE)] * len(extra), _rows(d),
                    (x, w.reshape(1, d), *extra), s, name)


def rmsnorm_bwd(x, w, dy, dres, *, name):
    s, d = x.shape

    def body(x_ref, w_ref, dy_ref, dr_ref, dx_ref, dw_ref):
        _first_step(dw_ref)
        xv = x_ref[...]
        r = lax.rsqrt(jnp.mean(xv * xv, axis=-1, keepdims=True) + NORM_EPS)
        xhat = xv * r
        dyv = dy_ref[...].astype(F32)
        dw_ref[...] += jnp.sum(dyv * xhat, axis=0, keepdims=True)
        g = dyv * w_ref[...]
        dx_ref[...] = dr_ref[...] + r * (g - xhat * jnp.mean(g * xhat, axis=-1, keepdims=True))

    return _ew_call(body, (jax.ShapeDtypeStruct((s, d), F32), jax.ShapeDtypeStruct((1, d), F32)),
                    [_rows(d), _whole((1, d)), _rows(d), _rows(d)], (_rows(d), _whole((1, d))),
                    (x, w.reshape(1, d), dy, dres), s, name, carried=True)


def ple_fwd(h1, gate_pre, pp, *, name):
    s, d = h1.shape

    def body(h_ref, g_ref, p_ref, o_ref):
        o_ref[...] = h_ref[...] + p_ref[...] * _sigmoid(g_ref[...])

    return _ew_call(body, jax.ShapeDtypeStruct((s, d), F32), [_rows(d)] * 3, _rows(d), (h1, gate_pre, pp), s, name)


def ple_bwd(dh2, gate_pre, pp, after, *, name):
    s, d = dh2.shape

    def body(dh_ref, g_ref, p_ref, after_ref, dp_ref, dg_ref):
        gate = _sigmoid(g_ref[...])
        dh = dh_ref[...]
        dp_ref[...] = (dh * gate).astype(BF16)
        dg_ref[...] = (dh * p_ref[...] * gate * (1.0 - gate)).astype(BF16)

    shp = jax.ShapeDtypeStruct((s, d), BF16)
    return _ew_call(body, (shp, shp), [_rows(d)] * 3 + [_whole(TOKEN_SHAPE)], (_rows(d), _rows(d)),
                    (dh2, gate_pre, pp, after), s, name)


def loss_head(y, target, *, name):
    s, d = y.shape

    def body(y_ref, t_ref, l_ref, dy_ref):
        _first_step(l_ref)
        err = y_ref[...] - t_ref[...]
        per_tok = jnp.mean(err * err, axis=-1, keepdims=True)
        l_ref[...] += 0.5 * jnp.sum(per_tok, axis=0, keepdims=True)
        dy_ref[...] = err * (1.0 / d)

    return _ew_call(body, (jax.ShapeDtypeStruct((1, 1), F32), jax.ShapeDtypeStruct((s, d), F32)),
                    [_rows(d), _rows(d)], (_whole((1, 1)), _rows(d)), (y, target), s, name, carried=True)


CONV_COL_TILE = 256


def _conv_taps(x, w_ref):
    row = lax.broadcasted_iota(jnp.int32, (x.shape[0], 1), 0)
    acc = x * w_ref[SSD_D_CONV - 1:SSD_D_CONV, :]
    shifted = []
    for d in range(1, SSD_D_CONV):
        xs = jnp.where(row >= d, pltpu.roll(x, d, 0), 0.0)
        shifted.append(xs)
        acc = acc + xs * w_ref[SSD_D_CONV - 1 - d:SSD_D_CONV - d, :]
    return acc, shifted


def ssd_conv_fwd(x, w, b, *, name):
    s, c = x.shape
    tc = _pick(c, (CONV_COL_TILE, LANES))

    def body(x_ref, w_ref, b_ref, o_ref):
        pre, _ = _conv_taps(x_ref[...], w_ref)
        o_ref[...] = _silu(pre + b_ref[...])

    col = pl.BlockSpec((s, tc), lambda j: (0, j))
    return pl.pallas_call(
        body, out_shape=jax.ShapeDtypeStruct((s, c), F32), grid=(c // tc,),
        in_specs=[col, pl.BlockSpec((SSD_D_CONV, tc), lambda j: (0, j)), pl.BlockSpec((1, tc), lambda j: (0, j))],
        out_specs=col, name=name, compiler_params=_cparams(("parallel",)),
    )(x, w, b)


def ssd_conv_bwd(x, w, b, dact, *, name):
    s, c = x.shape
    tc = _pick(c, (CONV_COL_TILE, LANES))

    def body(x_ref, w_ref, b_ref, da_ref, dx_ref, dw_ref, db_ref):
        xv = x_ref[...]
        pre, shifted = _conv_taps(xv, w_ref)
        dpre = da_ref[...] * _silu_grad(pre + b_ref[...])
        db_ref[...] = jnp.sum(dpre, axis=0, keepdims=True)
        row = lax.broadcasted_iota(jnp.int32, (s, 1), 0)
        dx = dpre * w_ref[SSD_D_CONV - 1:SSD_D_CONV, :]
        dw_ref[SSD_D_CONV - 1:SSD_D_CONV, :] = jnp.sum(dpre * xv, axis=0, keepdims=True)
        for d in range(1, SSD_D_CONV):
            k = SSD_D_CONV - 1 - d
            dw_ref[k:k + 1, :] = jnp.sum(dpre * shifted[d - 1], axis=0, keepdims=True)
            up = jnp.where(row < s - d, pltpu.roll(dpre, s - d, 0), 0.0)
            dx = dx + up * w_ref[k:k + 1, :]
        dx_ref[...] = dx.astype(BF16)

    col = pl.BlockSpec((s, tc), lambda j: (0, j))
    wspec = pl.BlockSpec((SSD_D_CONV, tc), lambda j: (0, j))
    bspec = pl.BlockSpec((1, tc), lambda j: (0, j))
    return pl.pallas_call(
        body,
        out_shape=(jax.ShapeDtypeStruct((s, c), BF16), jax.ShapeDtypeStruct((SSD_D_CONV, c), F32),
                   jax.ShapeDtypeStruct((1, c), F32)),
        grid=(c // tc,), in_specs=[col, wspec, bspec, col], out_specs=(col, wspec, bspec),
        name=name, compiler_params=_cparams(("parallel",)),
    )(x, w, b, dact)


def ssd_dt_fwd(dt_raw, bias, a_log, *, name):
    s, h = dt_raw.shape

    def body(r_ref, b_ref, al_ref, dt_ref, a_ref):
        zv = r_ref[...] + b_ref[...]
        dt_ref[...] = jnp.maximum(zv, 0.0) + jnp.log(1.0 + jnp.exp(-jnp.abs(zv)))
        a_ref[...] = -jnp.exp(al_ref[...])

    full = pl.BlockSpec((s, h), lambda: (0, 0))
    vec = pl.BlockSpec((1, h), lambda: (0, 0))
    return pl.pallas_call(
        body, out_shape=(jax.ShapeDtypeStruct((s, h), F32), jax.ShapeDtypeStruct((1, h), F32)),
        in_specs=[full, vec, vec], out_specs=(full, vec), name=name, compiler_params=_cparams(),
    )(dt_raw, bias.reshape(1, h), a_log.reshape(1, h))


def ssd_dt_bwd(dt_raw, bias, a_log, dt, ddt, dadt, *, name):
    s, h = dt_raw.shape

    def body(r_ref, b_ref, al_ref, dt_ref, ddt_ref, dadt_ref, dr_ref, db_ref, dal_ref):
        a = -jnp.exp(al_ref[...])
        dadt_v = dadt_ref[...]
        d_dt = ddt_ref[...] + a * dadt_v
        d_raw = d_dt * _sigmoid(r_ref[...] + b_ref[...])
        dr_ref[...] = d_raw
        db_ref[...] = jnp.sum(d_raw, axis=0, keepdims=True)
        dal_ref[...] = jnp.sum(dadt_v * dt_ref[...], axis=0, keepdims=True) * a

    full = pl.BlockSpec((s, h), lambda: (0, 0))
    vec = pl.BlockSpec((1, h), lambda: (0, 0))
    return pl.pallas_call(
        body, out_shape=(jax.ShapeDtypeStruct((s, h), F32), jax.ShapeDtypeStruct((1, h), F32),
                         jax.ShapeDtypeStruct((1, h), F32)),
        in_specs=[full, vec, vec, full, full, full], out_specs=(full, vec, vec), name=name,
        compiler_params=_cparams(),
    )(dt_raw, bias.reshape(1, h), a_log.reshape(1, h), dt, ddt, dadt)


def _group_mean(v, n_groups):
    gw = v.shape[-1] // n_groups
    parts = [jnp.broadcast_to(jnp.mean(v[:, k * gw:(k + 1) * gw], axis=-1, keepdims=True), (v.shape[0], gw))
             for k in range(n_groups)]
    return jnp.concatenate(parts, axis=-1)


def ssd_gate_fwd(y, z, gw, *, name):
    s, di = y.shape

    def body(y_ref, z_ref, w_ref, o_ref):
        yg = y_ref[...] * _silu(z_ref[...])
        r = lax.rsqrt(_group_mean(yg * yg, SSD_N_GROUPS) + GATED_NORM_EPS)
        o_ref[...] = (yg * r * w_ref[...]).astype(BF16)

    return _ew_call(body, jax.ShapeDtypeStruct((s, di), BF16), [_wide_rows(di), _wide_rows(di), _whole((1, di))],
                    _wide_rows(di), (y, z, gw.reshape(1, di)), s, name)


def ssd_gate_bwd(y, z, gw, dyn, *, name):
    s, di = y.shape

    def body(y_ref, z_ref, w_ref, dn_ref, dy_ref, dz_ref, dw_ref):
        _first_step(dw_ref)
        yv, zv = y_ref[...], z_ref[...]
        sz = _silu(zv)
        yg = yv * sz
        r = lax.rsqrt(_group_mean(yg * yg, SSD_N_GROUPS) + GATED_NORM_EPS)
        yhat = yg * r
        dn = dn_ref[...]
        dw_ref[...] += jnp.sum(dn * yhat, axis=0, keepdims=True)
        g = dn * w_ref[...]
        dyg = r * (g - yhat * _group_mean(g * yhat, SSD_N_GROUPS))
        dy_ref[...] = dyg * sz
        dz_ref[...] = (dyg * yv * _silu_grad(zv)).astype(BF16)

    return _ew_call(body, (jax.ShapeDtypeStruct((s, di), F32), jax.ShapeDtypeStruct((s, di), BF16),
                           jax.ShapeDtypeStruct((1, di), F32)),
                    [_wide_rows(di), _wide_rows(di), _whole((1, di)), _wide_rows(di)],
                    (_wide_rows(di), _wide_rows(di), _whole((1, di))),
                    (y, z, gw.reshape(1, di), dyn), s, name, carried=True)


def _head_mean(v):
    return _group_mean(v, v.shape[-1] // SB_HEAD_DIM)


def sb_qk_fwd(proj, qw, kw, *, name):
    s, w4 = proj.shape
    w = w4 // 4
    reps = w // SB_HEAD_DIM

    def body(q_ref, k_ref, qw_ref, kw_ref, qn_ref, kn_ref):
        for x_ref, w_ref, o_ref in ((q_ref, qw_ref, qn_ref), (k_ref, kw_ref, kn_ref)):
            xv = x_ref[...]
            r = lax.rsqrt(_head_mean(xv * xv) + NORM_EPS)
            o_ref[...] = (xv * r * jnp.tile(w_ref[...], (1, reps))).astype(BF16)

    shp = jax.ShapeDtypeStruct((s, w), BF16)
    return _ew_call(body, (shp, shp), [_rows(w, 0), _rows(w, 1), _whole((1, SB_HEAD_DIM)), _whole((1, SB_HEAD_DIM))],
                    (_rows(w), _rows(w)), (proj, proj, qw.reshape(1, -1), kw.reshape(1, -1)), s, name)


def sb_gate_fwd(o, proj, *, name):
    s, w = o.shape

    def body(o_ref, g_ref, og_ref):
        og_ref[...] = (o_ref[...] * _silu(g_ref[...])).astype(BF16)

    return _ew_call(body, jax.ShapeDtypeStruct((s, w), BF16), [_rows(w), _rows(w, 3)], _rows(w), (o, proj), s, name)


def sb_gate_bwd(dog, o, proj, *, name):
    s, w = o.shape

    def body(d_ref, o_ref, g_ref, do_ref, dg_ref):
        gv, dv = g_ref[...], d_ref[...]
        do_ref[...] = dv * _silu(gv)
        dg_ref[...] = (dv * o_ref[...] * _silu_grad(gv)).astype(BF16)

    return _ew_call(body, (jax.ShapeDtypeStruct((s, w), F32), jax.ShapeDtypeStruct((s, w), BF16)),
                    [_rows(w), _rows(w), _rows(w, 3)], (_rows(w), _rows(w)), (dog, o, proj), s, name)


def sb_pack_bwd(proj, qw, kw, dqn, dkn, dv, dg, *, name):
    s, w4 = proj.shape
    w = w4 // 4
    reps = w // SB_HEAD_DIM

    def body(q_ref, k_ref, qw_ref, kw_ref, dqn_ref, dkn_ref, dv_ref, dg_ref, dp_ref, dqw_ref, dkw_ref):
        _first_step(dqw_ref, dkw_ref)
        for idx, (x_ref, w_ref, d_ref, dw_ref) in enumerate(((q_ref, qw_ref, dqn_ref, dqw_ref),
                                                           (k_ref, kw_ref, dkn_ref, dkw_ref))):
            xv = x_ref[...]
            r = lax.rsqrt(_head_mean(xv * xv) + NORM_EPS)
            xhat = xv * r
            dn = d_ref[...]
            per_col = jnp.sum(dn * xhat, axis=0, keepdims=True)
            acc = per_col[:, 0:SB_HEAD_DIM]
            for hh in range(1, reps):
                acc = acc + per_col[:, hh * SB_HEAD_DIM:(hh + 1) * SB_HEAD_DIM]
            dw_ref[...] += acc
            g = dn * jnp.tile(w_ref[...], (1, reps))
            dp_ref[:, idx * w:(idx + 1) * w] = (r * (g - xhat * _head_mean(g * xhat))).astype(BF16)
        dp_ref[:, 2 * w:3 * w] = dv_ref[...].astype(BF16)
        dp_ref[:, 3 * w:4 * w] = dg_ref[...]

    vec = _whole((1, SB_HEAD_DIM))
    return _ew_call(body, (jax.ShapeDtypeStruct((s, w4), BF16), jax.ShapeDtypeStruct((1, SB_HEAD_DIM), F32),
                           jax.ShapeDtypeStruct((1, SB_HEAD_DIM), F32)),
                    [_wide_rows(w, 0), _wide_rows(w, 1), vec, vec, _wide_rows(w), _wide_rows(w), _wide_rows(w),
                     _wide_rows(w)],
                    (_wide_rows(w4), vec, vec),
                    (proj, proj, qw.reshape(1, -1), kw.reshape(1, -1), dqn, dkn, dv, dg), s, name, carried=True)


_HBM = pl.BlockSpec(memory_space=pltpu.HBM)


def _mesh_pos():
    return lax.axis_index("x"), lax.axis_index("y"), lax.axis_index("c")


def _other_chips(x, y):
    return [(1 - x, y), (x, 1 - y), (1 - x, 1 - y)]


_SEM = pl.BlockSpec(memory_space=pltpu.SEMAPHORE)
_ANY = pl.BlockSpec(memory_space=pl.ANY)
_DATAFLOW = pltpu.SideEffectType.DATAFLOW_SIDE_EFFECTING
N_PEER_CHIPS = N_CHIP - 1
TOKEN_SHAPE = (8, LANES)


def _in_hbm(t):
    return pltpu.with_memory_space_constraint(t, pltpu.HBM)


def _ici_copies(kind, src_refs, land_refs, send_sems, recv_sems, arrivals=False):
    x, y, c = _mesh_pos()
    out = []
    for a in range(len(land_refs)):
        if kind in ("pass", "swap"):
            if kind == "pass":
                src, dst = land_refs[a].at[:, c], land_refs[a].at[:, 1 - c if arrivals else c]
            else:
                src, dst = src_refs[a].at[:, 1 - c], land_refs[a]
            out.append(pltpu.make_async_remote_copy(
                src_ref=src, dst_ref=dst, send_sem=send_sems.at[a], recv_sem=recv_sems.at[a],
                device_id=(x, y, 1 - c), device_id_type=MESH))
            continue
        for j, chip in enumerate(_other_chips(x, y)):
            if kind == "gather":
                src = land_refs[a].at[4 * x + 2 * y + c]
                dst = land_refs[a].at[4 * chip[0] + 2 * chip[1] + c] if arrivals else src
            else:
                src, dst = src_refs[a].at[2 * chip[0] + chip[1]], land_refs[a].at[j]
            k = a * N_PEER_CHIPS + j
            out.append(pltpu.make_async_remote_copy(
                src_ref=src, dst_ref=dst, send_sem=send_sems.at[k], recv_sem=recv_sems.at[k],
                device_id=(*chip, c), device_id_type=MESH))
    return out


def _n_copies(kind, lands):
    return len(lands) * (1 if kind in ("pass", "swap") else N_PEER_CHIPS)


def ici_start(kind, srcs, lands, after=(), *, name):
    ns, nb = len(srcs), len(srcs) + len(lands)
    n_sem = _n_copies(kind, lands)

    def body(*refs):
        first_out = nb + len(after)
        for cp in _ici_copies(kind, refs[:ns], refs[ns:nb], refs[first_out], refs[first_out + 1]):
            cp.start()
        refs[-1][...] = jnp.zeros(TOKEN_SHAPE, F32)

    outs = pl.pallas_call(
        body, name=name,
        out_shape=(pltpu.SemaphoreType.DMA((n_sem,)), pltpu.SemaphoreType.DMA((n_sem,)),
                   *[pltpu.HBM(t.shape, t.dtype) for t in (*srcs, *lands)], jax.ShapeDtypeStruct(TOKEN_SHAPE, F32)),
        in_specs=[_HBM] * nb + [_ANY] * len(after),
        out_specs=(_SEM, _SEM, *([_HBM] * nb), pl.BlockSpec(memory_space=pltpu.VMEM)),
        input_output_aliases={k: 2 + k for k in range(nb)},
        compiler_params=pltpu.CompilerParams(has_side_effects=_DATAFLOW),
    )(*[_in_hbm(t) for t in (*srcs, *lands)], *after)
    return outs[0], outs[1], list(outs[2:2 + ns]), list(outs[2 + ns:2 + nb]), outs[-1]


def ici_wait(kind, started, after, *, name):
    send_sems, recv_sems, srcs, lands, _ = started
    ns, nb = len(srcs), len(srcs) + len(lands)

    def body(*refs):
        for cp in _ici_copies(kind, refs[:ns], refs[ns:nb], refs[nb], refs[nb + 1]):
            cp.wait_send()
        for cp in _ici_copies(kind, refs[:ns], refs[ns:nb], refs[nb], refs[nb + 1], arrivals=True):
            cp.wait_recv()

    outs = pl.pallas_call(
        body, name=name,
        out_shape=tuple(pltpu.HBM(t.shape, t.dtype) for t in (*srcs, *lands)),
        in_specs=[_HBM] * nb + [_SEM, _SEM] + [_ANY] * len(after),
        out_specs=tuple([_HBM] * nb),
        input_output_aliases={k: k for k in range(nb)},
        compiler_params=pltpu.CompilerParams(has_side_effects=_DATAFLOW),
    )(*srcs, *lands, send_sems, recv_sems, *after)
    return list(outs[:ns]), list(outs[ns:])


def all_reduce_small(v, *, name):
    r = v.shape[0]

    def body(v_ref, o_ref, buf, send_sems, recv_sems):
        x, y, c = _mesh_pos()
        me = 4 * x + 2 * y + c
        buf[me] = v_ref[...]
        copies = []
        for k in range(1, N_DEV):
            to = ((x + (k >> 2)) % 2, (y + ((k >> 1) & 1)) % 2, (c + (k & 1)) % 2)
            copies.append(pltpu.make_async_remote_copy(
                src_ref=v_ref, dst_ref=buf.at[me], send_sem=send_sems.at[k - 1], recv_sem=recv_sems.at[k - 1],
                device_id=to, device_id_type=MESH))
        for cp in copies:
            cp.start()
        for cp in copies:
            cp.wait()
        acc = buf[0]
        for d in range(1, N_DEV):
            acc = acc + buf[d]
        o_ref[...] = acc

    vm = pl.BlockSpec(memory_space=pltpu.VMEM)
    return pl.pallas_call(
        body, out_shape=jax.ShapeDtypeStruct(v.shape, F32), in_specs=[vm], out_specs=vm,
        scratch_shapes=[pltpu.VMEM((N_DEV, r, LANES), F32), pltpu.SemaphoreType.DMA((N_DEV - 1,)),
                        pltpu.SemaphoreType.DMA((N_DEV - 1,))],
        name=name,
    )(v)


def pair_add(g, r1, core, *, name):
    _, _, rows, cols = g.shape
    tm = _pick(rows, (256, 128))

    def body(c_ref, g_ref, r_ref, o_ref):
        o_ref[...] = (g_ref[...].astype(F32) + r_ref[...].astype(F32)).astype(o_ref.dtype)

    return pl.pallas_call(
        body, out_shape=jax.ShapeDtypeStruct(r1.shape, g.dtype),
        grid_spec=pltpu.PrefetchScalarGridSpec(
            num_scalar_prefetch=1, grid=(N_CHIP, rows // tm),
            in_specs=[pl.BlockSpec((None, None, tm, cols), lambda k, i, c_ref: (k, c_ref[0], i, 0)),
                      pl.BlockSpec((None, tm, cols), lambda k, i, c_ref: (k, i, 0))],
            out_specs=pl.BlockSpec((None, tm, cols), lambda k, i, c_ref: (k, i, 0))),
        name=name, compiler_params=_cparams(("parallel", "parallel")),
    )(core, g, r1)


def _adamw_math(w, g, m, v):
    m = ADAM_B1 * m + (1.0 - ADAM_B1) * g
    v = ADAM_B2 * v + (1.0 - ADAM_B2) * (g * g)
    m_hat = m / (1.0 - ADAM_B1 ** ADAM_STEP)
    v_hat = v / (1.0 - ADAM_B2 ** ADAM_STEP)
    delta = -ADAM_LR * (m_hat / (jnp.sqrt(v_hat) + ADAM_EPS) + ADAM_WD * w)
    return delta, m, v


def adamw_sharded(w, m, v, layer, chip_sums, received, chip, into, *, name):
    _, rows, cols = w.shape
    tm = _pick(rows, (256, 128))

    def body(k_ref, w_ref, m_ref, v_ref, t_ref, r_ref, *rest):
        g_ref, d_ref, nm_ref, nv_ref = rest[-4:]
        g = t_ref[...].astype(F32)
        for j in range(N_CHIP - 1):
            g = g + r_ref[j].astype(F32)
        d, mm, vv = _adamw_math(w_ref[...], g, m_ref[...], v_ref[...])
        g_ref[...] = g
        d_ref[...] = d
        nm_ref[...] = mm
        nv_ref[...] = vv

    blk = pl.BlockSpec((None, tm, cols), lambda i, k_ref: (layer, i, 0))
    shp = jax.ShapeDtypeStruct(w.shape, F32)
    in_specs = [blk, blk, blk,
                pl.BlockSpec((None, tm, cols), lambda i, k_ref: (k_ref[0], i, 0)),
                pl.BlockSpec((N_CHIP - 1, tm, cols), lambda i, k_ref: (0, i, 0))]
    operands = [chip, w, m, v, chip_sums, received]
    aliases = {}
    if into is not None:
        aliases = {len(operands) + q: q for q in range(4)}
        in_specs += [_ANY] * 4
        operands += list(into)
    return pl.pallas_call(
        body, out_shape=(shp, shp, shp, shp),
        grid_spec=pltpu.PrefetchScalarGridSpec(
            num_scalar_prefetch=1, grid=(rows // tm,), in_specs=in_specs, out_specs=(blk, blk, blk, blk)),
        input_output_aliases=aliases,
        name=name, compiler_params=_cparams(("parallel",)),
    )(*operands)


def adamw_replicated(w, m, v, g, *, name):
    def body(w_ref, m_ref, v_ref, g_ref, d_ref, nm_ref, nv_ref):
        d, mm, vv = _adamw_math(w_ref[...], g_ref[...], m_ref[...], v_ref[...])
        d_ref[...] = d
        nm_ref[...] = mm
        nv_ref[...] = vv

    shp = jax.ShapeDtypeStruct(w.shape, F32)
    return pl.pallas_call(body, out_shape=(shp, shp, shp), name=name, compiler_params=_cparams())(w, m, v, g)


WEIGHT_NAMES = ("norm_w", "ssd_in_w", "ssd_conv_w", "ssd_conv_b", "ssd_dt_bias", "ssd_a_log", "ssd_d",
                "ssd_gnorm_w", "ssd_out_w", "sb_in_w", "sb_qn_w", "sb_kn_w", "sb_out_w", "ple_norm_w",
                "ple_gate_w", "ple_proj_w")
REPLICATED = ("norm_w", "ssd_conv_b", "ssd_dt_bias", "ssd_a_log", "ssd_d", "ssd_gnorm_w", "sb_qn_w", "sb_kn_w",
              "ple_norm_w")
PACK_ROWS = 8


def _pack(parts):
    flat = jnp.concatenate([t.reshape(-1) for t in parts])
    pad = (-flat.shape[0]) % (PACK_ROWS * LANES)
    return jnp.pad(flat, (0, pad)).reshape(-1, LANES)


def _unpack(packed, like):
    flat = packed.reshape(-1)
    out, off = [], 0
    for t in like:
        out.append(flat[off:off + t.size].reshape(t.shape))
        off += t.size
    return out


def _to_group_lanes(v, r):
    t = v.reshape(v.shape[0], SSD_N_GROUPS, r).transpose(1, 0, 2)
    return jnp.pad(t, ((0, 0), (0, 0), (0, LANES - r)))


def _from_group_lanes(t, r):
    return t[:, :, :r].transpose(1, 0, 2).reshape(t.shape[1], SSD_N_GROUPS * r)


def _head_vec(v, r):
    return jnp.pad(v.reshape(SSD_N_GROUPS, 1, r), ((0, 0), (0, 0), (0, LANES - r)))


def _col_blocks(full):
    rows = full.shape[0]
    return full.reshape(rows, N_DEV, -1).transpose(1, 0, 2)


def _from_col_blocks(blocks):
    return blocks.transpose(1, 0, 2).reshape(blocks.shape[1], -1)


def _split_cols(full, widths):
    out, off = [], 0
    for w in widths:
        out.append(full[:, off:off + w])
        off += w
    return out


def kernel(x, p, norm_w, ssd_in_w, ssd_conv_w, ssd_conv_b, ssd_dt_bias, ssd_a_log, ssd_d, ssd_gnorm_w, ssd_out_w, sb_in_w, sb_qn_w, sb_kn_w, sb_out_w, ple_norm_w, ple_gate_w, ple_proj_w, loss_target, m_norm_w, m_ssd_in_w, m_ssd_conv_w, m_ssd_conv_b, m_ssd_dt_bias, m_ssd_a_log, m_ssd_d, m_ssd_gnorm_w, m_ssd_out_w, m_sb_in_w, m_sb_qn_w, m_sb_kn_w, m_sb_out_w, m_ple_norm_w, m_ple_gate_w, m_ple_proj_w, v_norm_w, v_ssd_in_w, v_ssd_conv_w, v_ssd_conv_b, v_ssd_dt_bias, v_ssd_a_log, v_ssd_d, v_ssd_gnorm_w, v_ssd_out_w, v_sb_in_w, v_sb_qn_w, v_sb_kn_w, v_sb_out_w, v_ple_norm_w, v_ple_gate_w, v_ple_proj_w):
    env = dict(locals())
    wts = {n: env[n] for n in WEIGHT_NAMES}
    mom1 = {n: env["m_" + n] for n in WEIGHT_NAMES}
    mom2 = {n: env["v_" + n] for n in WEIGHT_NAMES}

    s, d = x.shape[1], x.shape[2]
    depth = norm_w.shape[0]
    n_ssd, n_sb = ssd_in_w.shape[0], sb_in_w.shape[0]
    di = ssd_out_w.shape[1] * N_DEV
    n_heads = ssd_dt_bias.shape[1]
    hpg = n_heads // SSD_N_GROUPS
    nbc = SSD_N_GROUPS * SSD_D_STATE
    in_segs = (di, di, nbc, nbc, n_heads)
    conv_segs = (di, nbc, nbc)
    sb_w = sb_out_w.shape[1] * N_DEV
    selectors = ssd_selectors(hpg)
    xi, yi, ci = _mesh_pos()
    core = ci.astype(jnp.int32).reshape(1)
    chip = (2 * xi + yi).astype(jnp.int32).reshape(1)

    def layer_keys(i):
        j = i // 2
        mixer = [("ssd_in_w", j), ("ssd_conv_w", j), ("ssd_out_w", j)] if i % 2 == 0 else [("sb_in_w", j), ("sb_out_w", j)]
        return mixer + [("ple_gate_w", i), ("ple_proj_w", i)]

    def shard_of(key):
        t = wts[key[0]][key[1]]
        return t if key[0] == "ssd_conv_w" else t.astype(BF16)

    me_block = 4 * xi + 2 * yi + ci

    def landing_zone(t):
        return lax.dynamic_update_index_in_dim(lax.empty((N_DEV,) + t.shape, t.dtype), t, me_block, 0)

    def groups(i):
        keys = layer_keys(i)
        return [keys[:2], keys[2:]] if i == 0 else [keys]

    gathers, prev = {}, []
    for i in range(depth):
        for q, keys in enumerate(groups(i)):
            gathers[i, q] = ici_start("gather", [], [landing_zone(shard_of(k)) for k in keys], after=prev,
                                      name=f"ag{i}{'ab'[q]}_start")
            prev = [gathers[i, q][4]]
    all_started = prev[0]
    full, ssd_full, passing = {}, {}, {}

    def hand_over(i, q, after):
        _, lands = ici_wait("gather", gathers[i, q], after, name=f"ag{i}{'ab'[q]}_wait")
        passing[i, q] = ici_start("pass", [], [t.reshape(N_CHIP, 2, *t.shape[1:]) for t in lands],
                                  name=f"ag{i}{'ab'[q]}_pass_start")

    def arrive(i, q, after):
        _, lands = ici_wait("pass", passing[i, q], after, name=f"ag{i}{'ab'[q]}_pass_wait")
        for k, t in zip(groups(i)[q], lands):
            full[k] = t.reshape(N_DEV, *t.shape[2:])

    def w_out_of(i):
        return full["ssd_out_w", i // 2].reshape(di, d) if i % 2 == 0 else full["sb_out_w", i // 2].reshape(sb_w, d)

    h = x.reshape(s, d)
    saved = []
    hand_over(0, 0, [all_started])
    arrive(0, 0, [all_started])
    for i in range(depth):
        j = i // 2
        if i > 0:
            arrive(i, 0, [h])
        sv = dict(h_in=h)
        u = rmsnorm_fwd(h, norm_w[i], name=f"l{i}_norm")
        sv["u"] = u
        if i % 2 == 0:
            fw = ssd_full[j] = dict(
                w_in=_split_cols(_from_col_blocks(full["ssd_in_w", j]), in_segs),
                conv_w=_split_cols(_from_col_blocks(full["ssd_conv_w", j]), conv_segs),
                conv_b=_split_cols(ssd_conv_b[j].reshape(1, -1), conv_segs))
            raw = [matmul(u, wseg, name=f"l{i}_in{q}") for q, wseg in enumerate(fw["w_in"])]
            if i == 0:
                hand_over(0, 1, [raw[4]])
            z, dt_raw = raw[0], raw[4]
            act = [ssd_conv_fwd(raw[1 + q], fw["conv_w"][q], fw["conv_b"][q], name=f"l{i}_conv{q}") for q in range(3)]
            dt, a_neg = ssd_dt_fwd(dt_raw, ssd_dt_bias[j], ssd_a_log[j], name=f"l{i}_dt")
            dtp = _to_group_lanes(dt, hpg)
            a_g = _head_vec(a_neg.reshape(-1), hpg)
            d_x = jnp.repeat(ssd_d[j].reshape(SSD_N_GROUPS, 1, hpg), SSD_HEAD_DIM, axis=2)
            y, states = ssd_scan_fwd(act[0], act[1], act[2], dtp, a_g, d_x, selectors, heads_per_group=hpg,
                                     name=f"l{i}_scan")
            yn = ssd_gate_fwd(y, z, ssd_gnorm_w[j], name=f"l{i}_gate")
            if i == 0:
                arrive(0, 1, [yn])
            h1 = matmul(yn, w_out_of(i), res=h, name=f"l{i}_out")
            sv.update(raw=raw, act=act, dt=dt, dtp=dtp, a_g=a_g, d_x=d_x, y=y, states=states, yn=yn)
        else:
            proj = matmul(u, full["sb_in_w", j], name=f"l{i}_in")
            qn, kn = sb_qk_fwd(proj, sb_qn_w[j], sb_kn_w[j], name=f"l{i}_qknorm")
            v_off = 2 * sb_w // SB_HEAD_DIM
            o, tot = sb_attn_fwd(qn, kn, proj, v_off=v_off, name=f"l{i}_attn")
            og = sb_gate_fwd(o, proj, name=f"l{i}_gate")
            h1 = matmul(og, w_out_of(i), res=h, name=f"l{i}_out")
            sv.update(proj=proj, qn=qn, kn=kn, o=o, tot=tot, og=og, v_off=v_off)
        if i + 1 < depth:
            hand_over(i + 1, 0, [h1])
        t = rmsnorm_fwd(h1, ple_norm_w[i], passing[i + 1, 0][4] if i + 1 < depth else None, name=f"l{i}_plenorm")
        gate_pre = matmul(t, full["ple_gate_w", i].reshape(d, d), name=f"l{i}_plegate")
        pp = matmul(p[i, 0], full["ple_proj_w", i], name=f"l{i}_pleproj")
        h = ple_fwd(h1, gate_pre, pp, name=f"l{i}_ple")
        sv.update(h1=h1, t=t, gate_pre=gate_pre, pp=pp)
        saved.append(sv)

    loss_part, dh = loss_head(h, loss_target.reshape(s, d), name="loss_head")
    loss = lax.psum(loss_part[0, 0], ("x", "y", "c"))

    big = {}
    small = {n: [None] * wts[n].shape[0] for n in REPLICATED}
    swaps, scatters = {}, {}
    order_after = jnp.zeros(TOKEN_SHAPE, F32)
    pending = None

    def send_to_sibling(i, q):
        blocks = [big[k].reshape(N_CHIP, 2, *big[k].shape[1:]) for k in groups(i)[::-1][q]]
        swaps[i, q] = ici_start("swap", blocks, [lax.empty((N_CHIP,) + t.shape[2:], t.dtype) for t in blocks],
                                name=f"rs{i}{'ab'[q]}_swap_start")
        return swaps[i, q][4]

    def send_to_chips(i, q, after):
        blocks, from_sibling = ici_wait("swap", swaps[i, q], after, name=f"rs{i}{'ab'[q]}_swap_wait")
        sums = [pair_add(g, r1, core, name=f"rs{i}{'ab'[q]}_pair_add{a}")
                for a, (g, r1) in enumerate(zip(blocks, from_sibling))]
        scatters[i, q] = ici_start("scatter", sums, [lax.empty((N_PEER_CHIPS,) + t.shape[1:], t.dtype) for t in sums],
                                   name=f"rs{i}{'ab'[q]}_start")
        return scatters[i, q][4]

    for i in reversed(range(depth)):
        j = i // 2
        sv = saved[i]
        dpp, dgp = ple_bwd(dh, sv["gate_pre"], sv["pp"], order_after, name=f"b{i}_ple")
        big["ple_proj_w", i] = matmul(p[i, 0], dpp, mode="tn", out_dtype=BF16, out_blocks=ple_proj_w.shape[2],
                                      name=f"b{i}_pleproj_w")
        big["ple_gate_w", i] = matmul(sv["t"], dgp, mode="tn", out_dtype=BF16, name=f"b{i}_plegate_w").reshape(N_DEV, -1, d)
        dt_ = matmul(dgp, full["ple_gate_w", i].reshape(d, d), mode="nt", name=f"b{i}_plegate_x")
        dh1, g_pn = rmsnorm_bwd(sv["h1"], ple_norm_w[i], dt_, dh, name=f"b{i}_plenorm")
        small["ple_norm_w"][i] = g_pn
        behind = send_to_chips(*pending, [dh1]) if pending is not None else None
        pending = None
        u = sv["u"]
        if i % 2 == 0:
            fw = ssd_full[j]
            raw, act = sv["raw"], sv["act"]
            big["ssd_out_w", j] = matmul(sv["yn"], dh1, mode="tn", out_dtype=BF16, name=f"b{i}_out_w").reshape(N_DEV, -1, d)
            if i == 0:
                send_to_sibling(0, 0)
            dyn = matmul(dh1, w_out_of(i), mode="nt", after=behind, name=f"b{i}_out_x")
            dy, dz, g_gn = ssd_gate_bwd(sv["y"], raw[0], ssd_gnorm_w[j], dyn, name=f"b{i}_gate")
            dxs, dbm, dcm, ddtp, dadtp, dd_g = ssd_scan_bwd(act[0], act[1], act[2], sv["dtp"], sv["a_g"], sv["d_x"], selectors,
                                                          sv["states"], dy, heads_per_group=hpg, name=f"b{i}_scan")
            behind = send_to_chips(0, 0, [dxs]) if i == 0 else None
            ddt_raw, g_dtb, g_alog = ssd_dt_bwd(raw[4], ssd_dt_bias[j], ssd_a_log[j], sv["dt"],
                                                _from_group_lanes(ddtp, hpg), _from_group_lanes(dadtp, hpg),
                                                name=f"b{i}_dt")
            conv_back = [ssd_conv_bwd(raw[1 + q], fw["conv_w"][q], fw["conv_b"][q], dact, name=f"b{i}_conv{q}")
                         for q, dact in enumerate((dxs, dbm, dcm))]
            dsegs = [dz] + [cb[0] for cb in conv_back] + [ddt_raw]
            g_in = jnp.concatenate([matmul(u, ds, mode="tn", out_dtype=BF16, name=f"b{i}_in{q}_w")
                                    for q, ds in enumerate(dsegs)], axis=1)
            big["ssd_in_w", j] = _col_blocks(g_in)
            big["ssd_conv_w", j] = _col_blocks(jnp.concatenate([cb[1] for cb in conv_back], axis=1))
            du = None
            for q, (ds, wseg) in enumerate(zip(dsegs, fw["w_in"])):
                du = matmul(ds, wseg, mode="nt", res=du, after=behind if q == 0 else None, name=f"b{i}_in{q}_x")
            small["ssd_conv_b"][j] = jnp.concatenate([cb[2] for cb in conv_back], axis=1)
            small["ssd_dt_bias"][j] = g_dtb
            small["ssd_a_log"][j] = g_alog
            small["ssd_d"][j] = dd_g[:, 0, :hpg]
            small["ssd_gnorm_w"][j] = g_gn
        else:
            proj = sv["proj"]
            big["sb_out_w", j] = matmul(sv["og"], dh1, mode="tn", out_dtype=BF16, name=f"b{i}_out_w").reshape(N_DEV, -1, d)
            dog = matmul(dh1, w_out_of(i), mode="nt", after=behind, name=f"b{i}_out_x")
            do, dg = sb_gate_bwd(dog, sv["o"], proj, name=f"b{i}_gate")
            dqn, dkn, dv = sb_attn_bwd(sv["qn"], sv["kn"], proj, sv["tot"], do, v_off=sv["v_off"], name=f"b{i}_attn")
            dproj, g_qn, g_kn = sb_pack_bwd(proj, sb_qn_w[j], sb_kn_w[j], dqn, dkn, dv, dg, name=f"b{i}_qknorm")
            big["sb_in_w", j] = matmul(u, dproj, mode="tn", out_dtype=BF16, out_blocks=sb_in_w.shape[2], name=f"b{i}_in_w")
            du = matmul(dproj, full["sb_in_w", j], mode="nt", name=f"b{i}_in_x")
            small["sb_qn_w"][j] = g_qn
            small["sb_kn_w"][j] = g_kn
        dh, g_n = rmsnorm_bwd(sv["h_in"], norm_w[i], du, dh1, name=f"b{i}_norm")
        small["norm_w"][i] = g_n
        pending = (i, len(groups(i)) - 1)
        order_after = send_to_sibling(*pending)
    send_to_chips(*pending, [dh])
    grad_x = dh.reshape(x.shape)

    rep_like = [wts[n] for n in REPLICATED]
    g_packed = all_reduce_small(_pack([jnp.stack([t.reshape(-1) for t in small[n]]) for n in REPLICATED]),
                                name="all_reduce_small_grads")
    d_packed, m_packed, v_packed = adamw_replicated(
        _pack(rep_like), _pack([mom1[n] for n in REPLICATED]), _pack([mom2[n] for n in REPLICATED]), g_packed,
        name="adamw_replicated")
    grads = dict(zip(REPLICATED, _unpack(g_packed, rep_like)))
    deltas = dict(zip(REPLICATED, _unpack(d_packed, rep_like)))
    new_m = dict(zip(REPLICATED, _unpack(m_packed, rep_like)))
    new_v = dict(zip(REPLICATED, _unpack(v_packed, rep_like)))

    updated = {}
    after = [scatters[0, len(groups(0)) - 1][4]]
    for i in reversed(range(depth)):
        for q, keys in enumerate(groups(i)[::-1]):
            sums, received = ici_wait("scatter", scatters[i, q], after, name=f"rs{i}{'ab'[q]}_wait")
            for (n, idx), t_sum, recv in zip(keys, sums, received):
                updated[n] = adamw_sharded(wts[n], mom1[n], mom2[n], idx, t_sum, recv, chip, updated.get(n),
                                           name=f"adamw_{n}{idx}")
            after = [updated[n][0] for n, _ in keys]
    for n, (g_n, d_n, m_n, v_n) in updated.items():
        grads[n], deltas[n], new_m[n], new_v[n] = g_n, d_n, m_n, v_n

    return (loss, grad_x, *[grads[n] for n in WEIGHT_NAMES], *[deltas[n] for n in WEIGHT_NAMES],
            *[new_m[n] for n in WEIGHT_NAMES], *[new_v[n] for n in WEIGHT_NAMES])
```

```python
import functools
import math

import jax
import jax.numpy as jnp
from jax import lax
from jax.experimental import pallas as pl
from jax.experimental.pallas import tpu as pltpu

F32 = jnp.float32
BF16 = jnp.bfloat16
MESH = pl.DeviceIdType.MESH

N_DEV = 8
N_CHIP = 4
LANES = 128
VMEM_LIMIT_BYTES = 56 * 1024 * 1024
MATMUL_TILE_BYTES = 36 * 1024 * 1024

NORM_EPS = 1e-6
GATED_NORM_EPS = 1e-5
SSD_HEAD_DIM = 64
SSD_N_GROUPS = 8
SSD_D_STATE = 128
SSD_D_CONV = 4
SSD_CHUNK = 128
SB_HEAD_DIM = 128
PLE_DIM = 256

ADAM_LR = 0.001
ADAM_B1 = 0.9
ADAM_B2 = 0.999
ADAM_EPS = 1e-08
ADAM_WD = 0.01
ADAM_STEP = 10


def _cparams(sem=None, **kw):
    return pltpu.CompilerParams(dimension_semantics=sem, vmem_limit_bytes=VMEM_LIMIT_BYTES, **kw)


def _pick(dim, prefs):
    for t in prefs:
        if dim % t == 0:
            return t
    return dim


def _sigmoid(x):
    return 1.0 / (1.0 + jnp.exp(-x))


def _silu(x):
    return x * _sigmoid(x)


def _silu_grad(x):
    s = _sigmoid(x)
    return s * (1.0 + x * (1.0 - s))


def matmul(a, b, *, mode="nn", out_dtype=F32, res=None, out_blocks=None, after=None, name):
    b_blocked = b.ndim == 3
    if mode == "nn":
        m, kc = a.shape
        n = b.shape[-1] * (N_DEV if b_blocked else 1)
    elif mode == "nt":
        m, kc = a.shape
        n = b.shape[-2]
    else:
        kc, m = a.shape
        n = b.shape[-1]
    nb = b.shape[-1] if b_blocked else None
    tn = _pick(n if not out_blocks else out_blocks, (512, 256, 128))
    if b_blocked and mode == "nn":
        tn = _pick(nb, (512, 256, 128))
    k_unit = nb if (b_blocked and mode == "nt") else 1
    tm, tk = None, None
    for tm_try in (1024, 512, 256, 128):
        if m % tm_try:
            continue
        for tk_try in (kc, kc // 2, kc // 4, 2048, 1024, 512, 256, 128):
            if tk_try > kc or tk_try < k_unit or kc % tk_try or tk_try % k_unit:
                continue
            tiles = 2 * (tm_try * tk_try * a.dtype.itemsize + tk_try * tn * b.dtype.itemsize)
            tiles += tm_try * tn * (2 * jnp.dtype(out_dtype).itemsize + 4 + (8 if res is not None else 0))
            if tiles <= MATMUL_TILE_BYTES:
                tm, tk = tm_try, tk_try
                break
        if tm:
            break
    if tm is None:
        tm, tk = m, max(k_unit, LANES if kc % LANES == 0 else kc)
    nk = kc // tk
    grid = (m // tm, n // tn, nk)

    if mode == "tn":
        a_spec = pl.BlockSpec((tk, tm), lambda i, j, k: (k, i))
        dims = (((0,), (0,)), ((), ()))
    else:
        a_spec = pl.BlockSpec((tm, tk), lambda i, j, k: (i, k))
        dims = (((1,), (0,)), ((), ())) if mode == "nn" else (((1,), (1,)), ((), ()))
    if mode == "nt":
        if b_blocked:
            b_spec = pl.BlockSpec((tk // nb, tn, nb), lambda i, j, k: (k, j, 0))
        else:
            b_spec = pl.BlockSpec((tn, tk), lambda i, j, k: (j, k))
    else:
        if b_blocked:
            per = nb // tn
            b_spec = pl.BlockSpec((None, tk, tn), lambda i, j, k: (j // per, k, j % per))
        else:
            b_spec = pl.BlockSpec((tk, tn), lambda i, j, k: (k, j))
    if out_blocks:
        per_o = out_blocks // tn
        out_shape = jax.ShapeDtypeStruct((n // out_blocks, m, out_blocks), out_dtype)
        out_spec = pl.BlockSpec((None, tm, tn), lambda i, j, k: (j // per_o, i, j % per_o))
    else:
        out_shape = jax.ShapeDtypeStruct((m, n), out_dtype)
        out_spec = pl.BlockSpec((tm, tn), lambda i, j, k: (i, j))
    in_specs = [a_spec, b_spec]
    args = [a, b]
    if res is not None:
        in_specs.append(pl.BlockSpec((tm, tn), lambda i, j, k: (i, j)))
        args.append(res)
    if after is not None:
        in_specs.append(pl.BlockSpec(memory_space=pl.ANY))
        args.append(after)
    n_in = len(args)

    def body(*refs):
        a_ref, b_ref = refs[:2]
        r_ref = refs[2] if res is not None else None
        o_ref = refs[n_in]

        def finish(r):
            if res is not None:
                r = r + r_ref[...].astype(F32)
            o_ref[...] = r.astype(out_dtype)

        if b_blocked and mode == "nt":
            part = None
            for blk in range(tk // nb):
                term = lax.dot_general(a_ref[:, blk * nb:(blk + 1) * nb].astype(BF16), b_ref[blk].astype(BF16), dims,
                                       preferred_element_type=F32)
                part = term if part is None else part + term
        else:
            part = lax.dot_general(a_ref[...].astype(BF16), b_ref[...].astype(BF16), dims, preferred_element_type=F32)
        if nk == 1:
            finish(part)
            return
        acc_ref = refs[-1]
        k = pl.program_id(2)

        @pl.when(k == 0)
        def _():
            acc_ref[...] = part

        @pl.when(k > 0)
        def _():
            acc_ref[...] += part

        @pl.when(k == nk - 1)
        def _():
            finish(acc_ref[...])

    return pl.pallas_call(
        body, out_shape=out_shape, grid=grid, in_specs=in_specs, out_specs=out_spec,
        scratch_shapes=[] if nk == 1 else [pltpu.VMEM((tm, tn), F32)], name=name,
        compiler_params=_cparams(("parallel", "parallel", "arbitrary")),
    )(*args)


def _dot(a, b, dims, precision=None):
    return lax.dot_general(a, b, (dims, ((), ())), preferred_element_type=F32, precision=precision)


_NN = ((1,), (0,))
_NT = ((1,), (1,))
_TN = ((0,), (0,))
_EXACT = lax.Precision.HIGHEST


def _chunk_decay_terms(dt, a):
    ln = dt.shape[0]
    row = lax.broadcasted_iota(jnp.int32, (ln, ln), 0)
    col = lax.broadcasted_iota(jnp.int32, (ln, ln), 1)
    tri = (row >= col).astype(F32)
    a_col = _dot(tri, dt * a, _NN, _EXACT)
    return a_col, a_col.T, row >= col


def _exact_dot(x, sel, terms):
    t = x.shape[0]
    parts, rest = [], x
    for k in range(terms):
        piece = rest.astype(BF16)
        parts.append(piece)
        if k + 1 < terms:
            rest = rest - piece.astype(F32)
    r = _dot(jnp.concatenate(parts, axis=0), sel, _NN)
    out = r[:t]
    for k in range(1, terms):
        out = out + r[k * t:(k + 1) * t]
    return out


def ssd_selectors(r_n):
    lane = jnp.arange(LANES)
    spread64 = (lane[:, None] == jnp.arange(r_n * SSD_HEAD_DIM)[None, :] // SSD_HEAD_DIM).astype(BF16)
    pair_sum = jnp.stack([lane[None, :] == 2 * q + lane[:, None] // SSD_HEAD_DIM for q in range(r_n // 2)]).astype(BF16)
    row_sum = jnp.stack([jnp.broadcast_to(lane[None, :] == r, (LANES, LANES)) for r in range(r_n)]).astype(BF16)
    return spread64, pair_sum, row_sum


def _ssd_chunk_setup(dt, a, spread64):
    ln = dt.shape[0]
    a_col, a_row, causal = _chunk_decay_terms(dt, a)
    ea = jnp.exp(a_col)
    te = jnp.exp(a_col[ln - 1:ln, :] - a_col)
    return (a_row, a_col, _exact_dot(dt, spread64, 2), _exact_dot(ea, spread64, 2), _exact_dot(te, spread64, 2),
            ea, causal)


def ssd_scan_fwd(xs, bm, cm, dtp, a_g, d_x, selectors, *, heads_per_group, name):
    s, di = xs.shape
    g_n = SSD_N_GROUPS
    r_n, p_n, n_n, ln = heads_per_group, SSD_HEAD_DIM, SSD_D_STATE, SSD_CHUNK
    nc = s // ln
    pairs, pw = r_n // 2, 2 * p_n
    spread64 = selectors[0]

    def body(xs_ref, bm_ref, cm_ref, dt_ref, a_ref, d_ref, s64_ref, y_ref, st_ref, state):
        c = pl.program_id(1)

        @pl.when(c == 0)
        def _():
            state[...] = jnp.zeros_like(state)

        a_row, a_col, dt_x, ea_x, te_x, _, causal = _ssd_chunk_setup(dt_ref[...], a_ref[...], s64_ref[...])
        bm_f = bm_ref[...]
        bmb = bm_f.astype(BF16)
        bm_t = bm_f.T.astype(BF16)
        cmb = cm_ref[...].astype(BF16)
        scores = _dot(cmb, bmb, _NT)
        first_head = lax.broadcasted_iota(jnp.int32, (1, pw), 1) < p_n
        for q in range(pairs):
            sl = slice(q * pw, (q + 1) * pw)
            x2 = xs_ref[:, sl]
            xdt2 = x2 * dt_x[:, sl]
            xdt2b = xdt2.astype(BF16)
            y_heads = []
            for r in (2 * q, 2 * q + 1):
                decay = jnp.exp(jnp.where(causal, a_col[:, r:r + 1] - a_row[r:r + 1, :], -jnp.inf))
                y_heads.append(_dot((scores * decay).astype(BF16), xdt2b, _NN))
            s2t = state[q]
            st_ref[q] = s2t
            y2 = jnp.where(first_head, y_heads[0], y_heads[1])
            y2 = y2 + ea_x[:, sl] * _dot(cmb, s2t.astype(BF16), _NN)
            y_ref[:, sl] = y2 + d_ref[:, sl] * x2
            state[q] = s2t * ea_x[ln - 1:ln, sl] + _dot(bm_t, (xdt2 * te_x[:, sl]).astype(BF16), _NN)

    whole = lambda t: pl.BlockSpec(t.shape, lambda g, c: (0,) * t.ndim)
    return pl.pallas_call(
        body,
        out_shape=(jax.ShapeDtypeStruct((s, di), F32),
                   jax.ShapeDtypeStruct((nc, g_n * pairs, n_n, pw), F32)),
        grid=(g_n, nc),
        in_specs=[pl.BlockSpec((ln, r_n * p_n), lambda g, c: (c, g)),
                  pl.BlockSpec((ln, n_n), lambda g, c: (c, g)),
                  pl.BlockSpec((ln, n_n), lambda g, c: (c, g)),
                  pl.BlockSpec((None, ln, LANES), lambda g, c: (g, c, 0)),
                  pl.BlockSpec((None, 1, LANES), lambda g, c: (g, 0, 0)),
                  pl.BlockSpec((None, 1, r_n * p_n), lambda g, c: (g, 0, 0)),
                  whole(spread64)],
        out_specs=(pl.BlockSpec((ln, r_n * p_n), lambda g, c: (c, g)),
                   pl.BlockSpec((None, pairs, n_n, pw), lambda g, c: (c, g, 0, 0))),
        scratch_shapes=[pltpu.VMEM((pairs, n_n, pw), F32)],
        name=name, compiler_params=_cparams(("parallel", "arbitrary")),
    )(xs, bm, cm, dtp, a_g, d_x, spread64)


def _row8(v):
    return jnp.broadcast_to(v, (8, v.shape[1]))


def ssd_scan_bwd(xs, bm, cm, dtp, a_g, d_x, selectors, states, dy, *, heads_per_group, name):
    s, di = xs.shape
    g_n = SSD_N_GROUPS
    r_n, p_n, n_n, ln = heads_per_group, SSD_HEAD_DIM, SSD_D_STATE, SSD_CHUNK
    nc = s // ln
    pairs, pw = r_n // 2, 2 * p_n
    spread64, pair_sum, row_sum = selectors

    def body(xs_ref, bm_ref, cm_ref, dt_ref, a_ref, d_ref, s64_ref, ps_ref, rs_ref, st_ref, dy_ref,
             dxs_ref, dbm_ref, dcm_ref, ddt_ref, dadt_ref, dd_ref, dstate, da_rows):
        c = pl.program_id(1)

        @pl.when(c == 0)
        def _():
            dstate[...] = jnp.zeros_like(dstate)
            dd_ref[...] = jnp.zeros_like(dd_ref)

        a_row, a_col, dt_x, ea_x, te_x, ea, causal = _ssd_chunk_setup(dt_ref[...], a_ref[...], s64_ref[...])
        row = lax.broadcasted_iota(jnp.int32, (ln, ln), 0)
        col = lax.broadcasted_iota(jnp.int32, (ln, ln), 1)
        causal_t = col >= row
        bmb = bm_ref[...].astype(BF16)
        cm_f = cm_ref[...]
        cmb = cm_f.astype(BF16)
        cm_t = cm_f.T.astype(BF16)
        scores = _dot(cmb, bmb, _NT)
        scores_t = _dot(bmb, cmb, _NT)
        first_head = lax.broadcasted_iota(jnp.int32, (1, pw), 1) < p_n
        e_last = ea[ln - 1:ln, :]
        da_rows[...] = jnp.zeros_like(da_rows)
        dscores = jnp.zeros((ln, ln), F32)
        dcm = jnp.zeros((ln, n_n), F32)
        dbm = jnp.zeros((ln, n_n), F32)
        da_cols = jnp.zeros((ln, LANES), F32)
        da_last = jnp.zeros((1, LANES), F32)
        ddt = jnp.zeros((ln, LANES), F32)
        dd = jnp.zeros((1, LANES), F32)
        for q in range(pairs):
            sl = slice(q * pw, (q + 1) * pw)
            sum2 = ps_ref[q]
            x2 = xs_ref[:, sl]
            dt2 = dt_x[:, sl]
            xdt2 = x2 * dt2
            xdt2b = xdt2.astype(BF16)
            dy2 = dy_ref[:, sl]
            dy2b = dy2.astype(BF16)
            dxdt_heads = []
            for h, r in enumerate((2 * q, 2 * q + 1)):
                a_r = jnp.broadcast_to(a_col[:, r:r + 1], (ln, ln))
                decay = jnp.exp(jnp.where(causal, a_r - a_row[r:r + 1, :], -jnp.inf))
                decay_t = jnp.exp(jnp.where(causal_t, a_row[r:r + 1, :] - a_r, -jnp.inf))
                dy_h = jnp.where(first_head if h == 0 else jnp.logical_not(first_head), dy2, 0.0).astype(BF16)
                dm = _dot(dy_h, xdt2b, _NT)
                dscores = dscores + dm * decay
                e_mat = dm * (scores * decay)
                da_cols = da_cols + _exact_dot(e_mat, rs_ref[r], 2)
                da_rows[r:r + 1, :] = -jnp.sum(e_mat, axis=0, keepdims=True)
                dxdt_heads.append(_dot((scores_t * decay_t).astype(BF16), dy2b, _NN))
            dxdt2 = jnp.where(first_head, dxdt_heads[0], dxdt_heads[1])
            s2t = st_ref[q]
            s2tb = s2t.astype(BF16)
            ds2t = dstate[q]
            ds2tb = ds2t.astype(BF16)
            ea2, te2 = ea_x[:, sl], te_x[:, sl]
            y_off2 = ea2 * _dot(cmb, s2tb, _NN)
            dy_e2 = (dy2 * ea2).astype(BF16)
            dcm = dcm + _dot(dy_e2, s2tb, _NT)
            ds_in = _dot(cm_t, dy_e2, _NN)
            da_cols = da_cols + _exact_dot(dy2 * y_off2, sum2, 2)
            bds2 = _dot(bmb, ds2tb, _NN)
            dxdt2 = dxdt2 + te2 * bds2
            xdt_e2 = xdt2 * te2
            dbm = dbm + _dot(xdt_e2.astype(BF16), ds2tb, _NT)
            w_cols = _exact_dot(xdt_e2 * bds2, sum2, 2)
            da_cols = da_cols - w_cols
            state_dot = _exact_dot(_row8(jnp.sum(ds2t * s2t, axis=0, keepdims=True)), sum2, 2)[0:1]
            da_last = da_last + jnp.sum(w_cols, axis=0, keepdims=True) + e_last * state_dot
            dstate[q] = ds2t * ea_x[ln - 1:ln, sl] + ds_in
            dxs_ref[:, sl] = dxdt2 * dt2 + d_ref[:, sl] * dy2
            ddt = ddt + _exact_dot(dxdt2 * x2, sum2, 2)
            dd = dd + _exact_dot(_row8(jnp.sum(dy2 * x2, axis=0, keepdims=True)), sum2, 2)[0:1]
        dcm_ref[...] = dcm + _dot(dscores.astype(BF16), bmb, _NN)
        dbm_ref[...] = dbm + _dot(dscores.T.astype(BF16), cmb, _NN)
        da_total = da_cols + da_rows[...].T
        upper = causal_t.astype(F32)
        dadt_ref[...] = _dot(upper, da_total, _NN, _EXACT) + da_last
        ddt_ref[...] = ddt
        dd_ref[...] += dd

    last_c = nc - 1
    whole = lambda t: pl.BlockSpec(t.shape, lambda g, c: (0,) * t.ndim)
    return pl.pallas_call(
        body,
        out_shape=(jax.ShapeDtypeStruct((s, di), F32),
                   jax.ShapeDtypeStruct(bm.shape, F32),
                   jax.ShapeDtypeStruct(cm.shape, F32),
                   jax.ShapeDtypeStruct(dtp.shape, F32),
                   jax.ShapeDtypeStruct(dtp.shape, F32),
                   jax.ShapeDtypeStruct(a_g.shape, F32)),
        grid=(g_n, nc),
        in_specs=[pl.BlockSpec((ln, r_n * p_n), lambda g, c: (last_c - c, g)),
                  pl.BlockSpec((ln, n_n), lambda g, c: (last_c - c, g)),
                  pl.BlockSpec((ln, n_n), lambda g, c: (last_c - c, g)),
                  pl.BlockSpec((None, ln, LANES), lambda g, c: (g, last_c - c, 0)),
                  pl.BlockSpec((None, 1, LANES), lambda g, c: (g, 0, 0)),
                  pl.BlockSpec((None, 1, r_n * p_n), lambda g, c: (g, 0, 0)),
                  whole(spread64), whole(pair_sum), whole(row_sum),
                  pl.BlockSpec((None, pairs, n_n, pw), lambda g, c: (last_c - c, g, 0, 0)),
                  pl.BlockSpec((ln, r_n * p_n), lambda g, c: (last_c - c, g))],
        out_specs=(pl.BlockSpec((ln, r_n * p_n), lambda g, c: (last_c - c, g)),
                   pl.BlockSpec((ln, n_n), lambda g, c: (last_c - c, g)),
                   pl.BlockSpec((ln, n_n), lambda g, c: (last_c - c, g)),
                   pl.BlockSpec((None, ln, LANES), lambda g, c: (g, last_c - c, 0)),
                   pl.BlockSpec((None, ln, LANES), lambda g, c: (g, last_c - c, 0)),
                   pl.BlockSpec((None, 1, LANES), lambda g, c: (g, 0, 0))),
        scratch_shapes=[pltpu.VMEM((pairs, n_n, pw), F32), pltpu.VMEM((LANES, ln), F32)],
        name=name, compiler_params=_cparams(("parallel", "arbitrary")),
    )(xs, bm, cm, dtp, a_g, d_x, spread64, pair_sum, row_sum, states, dy)


SB_Q_TILE = 1024
SB_K_TILE = 256


def _tri_sum(x, tri):
    t = x.shape[0]
    hi = x.astype(BF16)
    r1 = x - hi.astype(F32)
    mid = r1.astype(BF16)
    lo = (r1 - mid.astype(F32)).astype(BF16)
    r = _dot(jnp.concatenate([hi, mid, lo], axis=0), tri, _NN)
    return r[:t] + r[t:2 * t] + r[2 * t:]


def _sb_logits(q, k_j, scale, strict):
    z = _dot(q, k_j, _NT) * scale
    sp = jnp.log(1.0 + jnp.exp(-jnp.abs(z)))
    log_b = jnp.minimum(z, 0.0) - sp
    log_1mb = log_b - z
    if strict is not None:
        log_1mb = jnp.where(strict, log_1mb, 0.0)
    return log_b, log_1mb


def _sb_tiles(s):
    tq = _pick(s, (SB_Q_TILE, 2 * SB_K_TILE, SB_K_TILE, LANES))
    return tq, min(tq, SB_K_TILE)


def _sb_diag_mask(rows, tk):
    return lax.broadcasted_iota(jnp.int32, (rows, tk), 1) < lax.broadcasted_iota(jnp.int32, (rows, tk), 0)


def _sb_iotas(t):
    row = lax.broadcasted_iota(jnp.int32, (t, t), 0)
    col = lax.broadcasted_iota(jnp.int32, (t, t), 1)
    return row, col


def sb_attn_fwd(qn, kn, v, *, v_off=0, name):
    s, w = qn.shape
    dh = SB_HEAD_DIM
    n_h = w // dh
    tq, tk = _sb_tiles(s)
    per = tq // tk
    scale = 1.0 / math.sqrt(dh)

    def body(q_ref, k_ref, v_ref, o_ref, tot_ref):
        i = pl.program_id(1)
        q = q_ref[...]
        row, col = _sb_iotas(tk)
        later = (row > col).astype(BF16)

        def tile(q_rows, j, acc, run, mask):
            s0 = pl.multiple_of(j * tk, tk)
            k_j = k_ref[pl.ds(s0, tk), :]
            v_j = v_ref[pl.ds(s0, tk), :].astype(BF16)
            log_b, log_1mb = _sb_logits(q_rows, k_j, scale, mask)
            att = jnp.exp(log_b + (_tri_sum(log_1mb, later) + run))
            if mask is not None:
                att = jnp.where(mask, att, 0.0)
            return acc + _dot(att.astype(BF16), v_j, _NN), run + jnp.sum(log_1mb, axis=1, keepdims=True)

        acc, run = jnp.zeros((tq, dh), F32), jnp.zeros((tq, 1), F32)
        for d in reversed(range(per)):
            r0 = d * tk
            a2, r2 = tile(q[r0:], i * per + d, acc[r0:], run[r0:], _sb_diag_mask(tq - r0, tk))
            acc = a2 if r0 == 0 else jnp.concatenate([acc[:r0], a2], axis=0)
            run = r2 if r0 == 0 else jnp.concatenate([run[:r0], r2], axis=0)

        def group(gg, c):
            for d in reversed(range(per)):
                c = tile(q, (i - 1 - gg) * per + d, c[0], c[1], None)
            return c

        acc, run = lax.fori_loop(0, i, group, (acc, run))
        o_ref[...] = acc
        tot_ref[...] = jnp.broadcast_to(run, (tq, dh))

    return pl.pallas_call(
        body,
        out_shape=(jax.ShapeDtypeStruct((s, w), F32), jax.ShapeDtypeStruct((s, w), F32)),
        grid=(n_h, s // tq),
        in_specs=[pl.BlockSpec((tq, dh), lambda h, i: (i, h)),
                  pl.BlockSpec((s, dh), lambda h, i: (0, h)),
                  pl.BlockSpec((s, dh), lambda h, i: (0, v_off + h))],
        out_specs=(pl.BlockSpec((tq, dh), lambda h, i: (i, h)),
                   pl.BlockSpec((tq, dh), lambda h, i: (i, h))),
        name=name, compiler_params=_cparams(("parallel", "parallel")),
    )(qn, kn, v)


def sb_attn_bwd(qn, kn, v, tot, do, *, v_off=0, name):
    s, w = qn.shape
    dh = SB_HEAD_DIM
    n_h = w // dh
    tq, tk = _sb_tiles(s)
    per = tq // tk
    scale = 1.0 / math.sqrt(dh)

    def body(q_ref, k_ref, v_ref, tot_ref, do_ref, dq_ref, dk_ref, dv_ref):
        dk_ref[...] = jnp.zeros_like(dk_ref)
        dv_ref[...] = jnp.zeros_like(dv_ref)
        row, col = _sb_iotas(tk)
        upto = (row <= col).astype(BF16)
        before = (row < col).astype(BF16)

        def q_block(i, _):
            t0 = pl.multiple_of(i * tq, tq)
            q = q_ref[pl.ds(t0, tq), :]
            do_i = do_ref[pl.ds(t0, tq), :].astype(BF16)
            total = tot_ref[pl.ds(t0, tq), 0:1]

            def tile(r0, j, dq, run_l, run_g, mask):
                s0 = pl.multiple_of(j * tk, tk)
                k_j = k_ref[pl.ds(s0, tk), :]
                v_j = v_ref[pl.ds(s0, tk), :].astype(BF16)
                q_r, do_r = q[r0:], do_i[r0:]
                log_b, log_1mb = _sb_logits(q_r, k_j, scale, mask)
                att = jnp.exp(log_b + ((total[r0:] - run_l) - _tri_sum(log_1mb, upto)))
                if mask is not None:
                    att = jnp.where(mask, att, 0.0)
                g = att * _dot(do_r, v_j, _NT)
                c = _tri_sum(g, before) + run_g
                dz = (g - (g + c) * jnp.exp(log_b)) * scale
                if mask is not None:
                    dz = jnp.where(mask, dz, 0.0)
                dz = dz.astype(BF16)
                dk_ref[pl.ds(s0, tk), :] += _dot(dz, q_r, _TN)
                dv_ref[pl.ds(s0, tk), :] += _dot(att.astype(BF16), do_r, _TN)
                return (dq + _dot(dz, k_j, _NN), run_l + jnp.sum(log_1mb, axis=1, keepdims=True),
                        run_g + jnp.sum(g, axis=1, keepdims=True))

            def group(gg, c):
                for d in range(per):
                    c = tile(0, gg * per + d, c[0], c[1], c[2], None)
                return c

            zero = jnp.zeros((tq, 1), F32)
            dq, run_l, run_g = lax.fori_loop(0, i, group, (jnp.zeros((tq, dh), F32), zero, zero))
            for d in range(per):
                r0 = d * tk
                p_dq, p_l, p_g = tile(r0, i * per + d, dq[r0:], run_l[r0:], run_g[r0:], _sb_diag_mask(tq - r0, tk))
                if r0 == 0:
                    dq, run_l, run_g = p_dq, p_l, p_g
                else:
                    dq = jnp.concatenate([dq[:r0], p_dq], axis=0)
                    run_l = jnp.concatenate([run_l[:r0], p_l], axis=0)
                    run_g = jnp.concatenate([run_g[:r0], p_g], axis=0)
            dq_ref[pl.ds(t0, tq), :] = dq
            return 0

        lax.fori_loop(0, s // tq, q_block, 0)

    head = pl.BlockSpec((s, dh), lambda h: (0, h))
    return pl.pallas_call(
        body,
        out_shape=tuple(jax.ShapeDtypeStruct((s, w), F32) for _ in range(3)),
        grid=(n_h,),
        in_specs=[head, head, pl.BlockSpec((s, dh), lambda h: (0, v_off + h)), head, head],
        out_specs=(head, head, head),
        name=name, compiler_params=_cparams(("parallel",)),
    )(qn, kn, v, tot, do)


ROW_TILE = 256
WIDE_ROW_TILE = 64


def _rows(width, col=0, tm=ROW_TILE):
    return pl.BlockSpec((tm, width), lambda i: (i, col))


_wide_rows = functools.partial(_rows, tm=WIDE_ROW_TILE)


def _whole(shape):
    return pl.BlockSpec(shape, lambda i: (0,) * len(shape))


def _ew_call(body, out_shape, in_specs, out_specs, args, n_rows, name, carried=False):
    return pl.pallas_call(
        body, out_shape=out_shape, grid=(n_rows // in_specs[0].block_shape[0],), in_specs=in_specs, out_specs=out_specs,
        name=name, compiler_params=_cparams(("arbitrary",) if carried else ("parallel",)),
    )(*args)


def _first_step(*refs):
    @pl.when(pl.program_id(0) == 0)
    def _():
        for r in refs:
            r[...] = jnp.zeros_like(r)


def rmsnorm_fwd(x, w, after=None, *, name):
    s, d = x.shape

    def body(x_ref, w_ref, *rest):
        o_ref = rest[-1]
        xv = x_ref[...]
        r = lax.rsqrt(jnp.mean(xv * xv, axis=-1, keepdims=True) + NORM_EPS)
        o_ref[...] = (xv * r * w_ref[...]).astype(BF16)

    extra = [] if after is None else [after]
    return _ew_call(body, jax.ShapeDtypeStruct((s, d), BF16),
                    [_rows(d), _whole((1, d))] + [_whole(TOKEN_SHAPE)] * len(extra), _rows(d),
                    (x, w.reshape(1, d), *extra), s, name)


def rmsnorm_bwd(x, w, dy, dres, *, name):
    s, d = x.shape

    def body(x_ref, w_ref, dy_ref, dr_ref, dx_ref, dw_ref):
        _first_step(dw_ref)
        xv = x_ref[...]
        r = lax.rsqrt(jnp.mean(xv * xv, axis=-1, keepdims=True) + NORM_EPS)
        xhat = xv * r
        dyv = dy_ref[...].astype(F32)
        dw_ref[...] += jnp.sum(dyv * xhat, axis=0, keepdims=True)
        g = dyv * w_ref[...]
        dx_ref[...] = dr_ref[...] + r * (g - xhat * jnp.mean(g * xhat, axis=-1, keepdims=True))

    return _ew_call(body, (jax.ShapeDtypeStruct((s, d), F32), jax.ShapeDtypeStruct((1, d), F32)),
                    [_rows(d), _whole((1, d)), _rows(d), _rows(d)], (_rows(d), _whole((1, d))),
                    (x, w.reshape(1, d), dy, dres), s, name, carried=True)


def ple_fwd(h1, gate_pre, pp, *, name):
    s, d = h1.shape

    def body(h_ref, g_ref, p_ref, o_ref):
        o_ref[...] = h_ref[...] + p_ref[...] * _sigmoid(g_ref[...])

    return _ew_call(body, jax.ShapeDtypeStruct((s, d), F32), [_rows(d)] * 3, _rows(d), (h1, gate_pre, pp), s, name)


def ple_bwd(dh2, gate_pre, pp, after, *, name):
    s, d = dh2.shape

    def body(dh_ref, g_ref, p_ref, after_ref, dp_ref, dg_ref):
        gate = _sigmoid(g_ref[...])
        dh = dh_ref[...]
        dp_ref[...] = (dh * gate).astype(BF16)
        dg_ref[...] = (dh * p_ref[...] * gate * (1.0 - gate)).astype(BF16)

    shp = jax.ShapeDtypeStruct((s, d), BF16)
    return _ew_call(body, (shp, shp), [_rows(d)] * 3 + [_whole(TOKEN_SHAPE)], (_rows(d), _rows(d)),
                    (dh2, gate_pre, pp, after), s, name)


def loss_head(y, target, *, name):
    s, d = y.shape

    def body(y_ref, t_ref, l_ref, dy_ref):
        _first_step(l_ref)
        err = y_ref[...] - t_ref[...]
        per_tok = jnp.mean(err * err, axis=-1, keepdims=True)
        l_ref[...] += 0.5 * jnp.sum(per_tok, axis=0, keepdims=True)
        dy_ref[...] = err * (1.0 / d)

    return _ew_call(body, (jax.ShapeDtypeStruct((1, 1), F32), jax.ShapeDtypeStruct((s, d), F32)),
                    [_rows(d), _rows(d)], (_whole((1, 1)), _rows(d)), (y, target), s, name, carried=True)


CONV_COL_TILE = 256


def _conv_taps(x, w_ref):
    row = lax.broadcasted_iota(jnp.int32, (x.shape[0], 1), 0)
    acc = x * w_ref[SSD_D_CONV - 1:SSD_D_CONV, :]
    shifted = []
    for d in range(1, SSD_D_CONV):
        xs = jnp.where(row >= d, pltpu.roll(x, d, 0), 0.0)
        shifted.append(xs)
        acc = acc + xs * w_ref[SSD_D_CONV - 1 - d:SSD_D_CONV - d, :]
    return acc, shifted


def ssd_conv_fwd(x, w, b, *, name):
    s, c = x.shape
    tc = _pick(c, (CONV_COL_TILE, LANES))

    def body(x_ref, w_ref, b_ref, o_ref):
        pre, _ = _conv_taps(x_ref[...], w_ref)
        o_ref[...] = _silu(pre + b_ref[...])

    col = pl.BlockSpec((s, tc), lambda j: (0, j))
    return pl.pallas_call(
        body, out_shape=jax.ShapeDtypeStruct((s, c), F32), grid=(c // tc,),
        in_specs=[col, pl.BlockSpec((SSD_D_CONV, tc), lambda j: (0, j)), pl.BlockSpec((1, tc), lambda j: (0, j))],
        out_specs=col, name=name, compiler_params=_cparams(("parallel",)),
    )(x, w, b)


def ssd_conv_bwd(x, w, b, dact, *, name):
    s, c = x.shape
    tc = _pick(c, (CONV_COL_TILE, LANES))

    def body(x_ref, w_ref, b_ref, da_ref, dx_ref, dw_ref, db_ref):
        xv = x_ref[...]
        pre, shifted = _conv_taps(xv, w_ref)
        dpre = da_ref[...] * _silu_grad(pre + b_ref[...])
        db_ref[...] = jnp.sum(dpre, axis=0, keepdims=True)
        row = lax.broadcasted_iota(jnp.int32, (s, 1), 0)
        dx = dpre * w_ref[SSD_D_CONV - 1:SSD_D_CONV, :]
        dw_ref[SSD_D_CONV - 1:SSD_D_CONV, :] = jnp.sum(dpre * xv, axis=0, keepdims=True)
        for d in range(1, SSD_D_CONV):
            k = SSD_D_CONV - 1 - d
            dw_ref[k:k + 1, :] = jnp.sum(dpre * shifted[d - 1], axis=0, keepdims=True)
            up = jnp.where(row < s - d, pltpu.roll(dpre, s - d, 0), 0.0)
            dx = dx + up * w_ref[k:k + 1, :]
        dx_ref[...] = dx.astype(BF16)

    col = pl.BlockSpec((s, tc), lambda j: (0, j))
    wspec = pl.BlockSpec((SSD_D_CONV, tc), lambda j: (0, j))
    bspec = pl.BlockSpec((1, tc), lambda j: (0, j))
    return pl.pallas_call(
        body,
        out_shape=(jax.ShapeDtypeStruct((s, c), BF16), jax.ShapeDtypeStruct((SSD_D_CONV, c), F32),
                   jax.ShapeDtypeStruct((1, c), F32)),
        grid=(c // tc,), in_specs=[col, wspec, bspec, col], out_specs=(col, wspec, bspec),
        name=name, compiler_params=_cparams(("parallel",)),
    )(x, w, b, dact)


def ssd_dt_fwd(dt_raw, bias, a_log, *, name):
    s, h = dt_raw.shape

    def body(r_ref, b_ref, al_ref, dt_ref, a_ref):
        zv = r_ref[...] + b_ref[...]
        dt_ref[...] = jnp.maximum(zv, 0.0) + jnp.log(1.0 + jnp.exp(-jnp.abs(zv)))
        a_ref[...] = -jnp.exp(al_ref[...])

    full = pl.BlockSpec((s, h), lambda: (0, 0))
    vec = pl.BlockSpec((1, h), lambda: (0, 0))
    return pl.pallas_call(
        body, out_shape=(jax.ShapeDtypeStruct((s, h), F32), jax.ShapeDtypeStruct((1, h), F32)),
        in_specs=[full, vec, vec], out_specs=(full, vec), name=name, compiler_params=_cparams(),
    )(dt_raw, bias.reshape(1, h), a_log.reshape(1, h))


def ssd_dt_bwd(dt_raw, bias, a_log, dt, ddt, dadt, *, name):
    s, h = dt_raw.shape

    def body(r_ref, b_ref, al_ref, dt_ref, ddt_ref, dadt_ref, dr_ref, db_ref, dal_ref):
        a = -jnp.exp(al_ref[...])
        dadt_v = dadt_ref[...]
        d_dt = ddt_ref[...] + a * dadt_v
        d_raw = d_dt * _sigmoid(r_ref[...] + b_ref[...])
        dr_ref[...] = d_raw
        db_ref[...] = jnp.sum(d_raw, axis=0, keepdims=True)
        dal_ref[...] = jnp.sum(dadt_v * dt_ref[...], axis=0, keepdims=True) * a

    full = pl.BlockSpec((s, h), lambda: (0, 0))
    vec = pl.BlockSpec((1, h), lambda: (0, 0))
    return pl.pallas_call(
        body, out_shape=(jax.ShapeDtypeStruct((s, h), F32), jax.ShapeDtypeStruct((1, h), F32),
                         jax.ShapeDtypeStruct((1, h), F32)),
        in_specs=[full, vec, vec, full, full, full], out_specs=(full, vec, vec), name=name,
        compiler_params=_cparams(),
    )(dt_raw, bias.reshape(1, h), a_log.reshape(1, h), dt, ddt, dadt)


def _group_mean(v, n_groups):
    gw = v.shape[-1] // n_groups
    parts = [jnp.broadcast_to(jnp.mean(v[:, k * gw:(k + 1) * gw], axis=-1, keepdims=True), (v.shape[0], gw))
             for k in range(n_groups)]
    return jnp.concatenate(parts, axis=-1)


def ssd_gate_fwd(y, z, gw, *, name):
    s, di = y.shape

    def body(y_ref, z_ref, w_ref, o_ref):
        yg = y_ref[...] * _silu(z_ref[...])
        r = lax.rsqrt(_group_mean(yg * yg, SSD_N_GROUPS) + GATED_NORM_EPS)
        o_ref[...] = (yg * r * w_ref[...]).astype(BF16)

    return _ew_call(body, jax.ShapeDtypeStruct((s, di), BF16), [_wide_rows(di), _wide_rows(di), _whole((1, di))],
                    _wide_rows(di), (y, z, gw.reshape(1, di)), s, name)


def ssd_gate_bwd(y, z, gw, dyn, *, name):
    s, di = y.shape

    def body(y_ref, z_ref, w_ref, dn_ref, dy_ref, dz_ref, dw_ref):
        _first_step(dw_ref)
        yv, zv = y_ref[...], z_ref[...]
        sz = _silu(zv)
        yg = yv * sz
        r = lax.rsqrt(_group_mean(yg * yg, SSD_N_GROUPS) + GATED_NORM_EPS)
        yhat = yg * r
        dn = dn_ref[...]
        dw_ref[...] += jnp.sum(dn * yhat, axis=0, keepdims=True)
        g = dn * w_ref[...]
        dyg = r * (g - yhat * _group_mean(g * yhat, SSD_N_GROUPS))
        dy_ref[...] = dyg * sz
        dz_ref[...] = (dyg * yv * _silu_grad(zv)).astype(BF16)

    return _ew_call(body, (jax.ShapeDtypeStruct((s, di), F32), jax.ShapeDtypeStruct((s, di), BF16),
                           jax.ShapeDtypeStruct((1, di), F32)),
                    [_wide_rows(di), _wide_rows(di), _whole((1, di)), _wide_rows(di)],
                    (_wide_rows(di), _wide_rows(di), _whole((1, di))),
                    (y, z, gw.reshape(1, di), dyn), s, name, carried=True)


def _head_mean(v):
    return _group_mean(v, v.shape[-1] // SB_HEAD_DIM)


def sb_qk_fwd(proj, qw, kw, *, name):
    s, w4 = proj.shape
    w = w4 // 4
    reps = w // SB_HEAD_DIM

    def body(q_ref, k_ref, qw_ref, kw_ref, qn_ref, kn_ref):
        for x_ref, w_ref, o_ref in ((q_ref, qw_ref, qn_ref), (k_ref, kw_ref, kn_ref)):
            xv = x_ref[...]
            r = lax.rsqrt(_head_mean(xv * xv) + NORM_EPS)
            o_ref[...] = (xv * r * jnp.tile(w_ref[...], (1, reps))).astype(BF16)

    shp = jax.ShapeDtypeStruct((s, w), BF16)
    return _ew_call(body, (shp, shp), [_rows(w, 0), _rows(w, 1), _whole((1, SB_HEAD_DIM)), _whole((1, SB_HEAD_DIM))],
                    (_rows(w), _rows(w)), (proj, proj, qw.reshape(1, -1), kw.reshape(1, -1)), s, name)


def sb_gate_fwd(o, proj, *, name):
    s, w = o.shape

    def body(o_ref, g_ref, og_ref):
        og_ref[...] = (o_ref[...] * _silu(g_ref[...])).astype(BF16)

    return _ew_call(body, jax.ShapeDtypeStruct((s, w), BF16), [_rows(w), _rows(w, 3)], _rows(w), (o, proj), s, name)


def sb_gate_bwd(dog, o, proj, *, name):
    s, w = o.shape

    def body(d_ref, o_ref, g_ref, do_ref, dg_ref):
        gv, dv = g_ref[...], d_ref[...]
        do_ref[...] = dv * _silu(gv)
        dg_ref[...] = (dv * o_ref[...] * _silu_grad(gv)).astype(BF16)

    return _ew_call(body, (jax.ShapeDtypeStruct((s, w), F32), jax.ShapeDtypeStruct((s, w), BF16)),
                    [_rows(w), _rows(w), _rows(w, 3)], (_rows(w), _rows(w)), (dog, o, proj), s, name)


def sb_pack_bwd(proj, qw, kw, dqn, dkn, dv, dg, *, name):
    s, w4 = proj.shape
    w = w4 // 4
    reps = w // SB_HEAD_DIM

    def body(q_ref, k_ref, qw_ref, kw_ref, dqn_ref, dkn_ref, dv_ref, dg_ref, dp_ref, dqw_ref, dkw_ref):
        _first_step(dqw_ref, dkw_ref)
        for idx, (x_ref, w_ref, d_ref, dw_ref) in enumerate(((q_ref, qw_ref, dqn_ref, dqw_ref),
                                                           (k_ref, kw_ref, dkn_ref, dkw_ref))):
            xv = x_ref[...]
            r = lax.rsqrt(_head_mean(xv * xv) + NORM_EPS)
            xhat = xv * r
            dn = d_ref[...]
            per_col = jnp.sum(dn * xhat, axis=0, keepdims=True)
            acc = per_col[:, 0:SB_HEAD_DIM]
            for hh in range(1, reps):
                acc = acc + per_col[:, hh * SB_HEAD_DIM:(hh + 1) * SB_HEAD_DIM]
            dw_ref[...] += acc
            g = dn * jnp.tile(w_ref[...], (1, reps))
            dp_ref[:, idx * w:(idx + 1) * w] = (r * (g - xhat * _head_mean(g * xhat))).astype(BF16)
        dp_ref[:, 2 * w:3 * w] = dv_ref[...].astype(BF16)
        dp_ref[:, 3 * w:4 * w] = dg_ref[...]

    vec = _whole((1, SB_HEAD_DIM))
    return _ew_call(body, (jax.ShapeDtypeStruct((s, w4), BF16), jax.ShapeDtypeStruct((1, SB_HEAD_DIM), F32),
                           jax.ShapeDtypeStruct((1, SB_HEAD_DIM), F32)),
                    [_wide_rows(w, 0), _wide_rows(w, 1), vec, vec, _wide_rows(w), _wide_rows(w), _wide_rows(w),
                     _wide_rows(w)],
                    (_wide_rows(w4), vec, vec),
                    (proj, proj, qw.reshape(1, -1), kw.reshape(1, -1), dqn, dkn, dv, dg), s, name, carried=True)


_HBM = pl.BlockSpec(memory_space=pltpu.HBM)


def _mesh_pos():
    return lax.axis_index("x"), lax.axis_index("y"), lax.axis_index("c")


def _other_chips(x, y):
    return [(1 - x, y), (x, 1 - y), (1 - x, 1 - y)]


_SEM = pl.BlockSpec(memory_space=pltpu.SEMAPHORE)
_ANY = pl.BlockSpec(memory_space=pl.ANY)
_DATAFLOW = pltpu.SideEffectType.DATAFLOW_SIDE_EFFECTING
N_PEER_CHIPS = N_CHIP - 1
TOKEN_SHAPE = (8, LANES)


def _in_hbm(t):
    return pltpu.with_memory_space_constraint(t, pltpu.HBM)


def _ici_copies(kind, src_refs, land_refs, send_sems, recv_sems, arrivals=False):
    x, y, c = _mesh_pos()
    out = []
    for a in range(len(land_refs)):
        if kind in ("pass", "swap"):
            if kind == "pass":
                src, dst = land_refs[a].at[:, c], land_refs[a].at[:, 1 - c if arrivals else c]
            else:
                src, dst = src_refs[a].at[:, 1 - c], land_refs[a]
            out.append(pltpu.make_async_remote_copy(
                src_ref=src, dst_ref=dst, send_sem=send_sems.at[a], recv_sem=recv_sems.at[a],
                device_id=(x, y, 1 - c), device_id_type=MESH))
            continue
        for j, chip in enumerate(_other_chips(x, y)):
            if kind == "gather":
                src = land_refs[a].at[4 * x + 2 * y + c]
                dst = land_refs[a].at[4 * chip[0] + 2 * chip[1] + c] if arrivals else src
            else:
                src, dst = src_refs[a].at[2 * chip[0] + chip[1]], land_refs[a].at[j]
            k = a * N_PEER_CHIPS + j
            out.append(pltpu.make_async_remote_copy(
                src_ref=src, dst_ref=dst, send_sem=send_sems.at[k], recv_sem=recv_sems.at[k],
                device_id=(*chip, c), device_id_type=MESH))
    return out


def _n_copies(kind, lands):
    return len(lands) * (1 if kind in ("pass", "swap") else N_PEER_CHIPS)


def ici_start(kind, srcs, lands, after=(), *, name):
    ns, nb = len(srcs), len(srcs) + len(lands)
    n_sem = _n_copies(kind, lands)

    def body(*refs):
        first_out = nb + len(after)
        for cp in _ici_copies(kind, refs[:ns], refs[ns:nb], refs[first_out], refs[first_out + 1]):
            cp.start()
        refs[-1][...] = jnp.zeros(TOKEN_SHAPE, F32)

    outs = pl.pallas_call(
        body, name=name,
        out_shape=(pltpu.SemaphoreType.DMA((n_sem,)), pltpu.SemaphoreType.DMA((n_sem,)),
                   *[pltpu.HBM(t.shape, t.dtype) for t in (*srcs, *lands)], jax.ShapeDtypeStruct(TOKEN_SHAPE, F32)),
        in_specs=[_HBM] * nb + [_ANY] * len(after),
        out_specs=(_SEM, _SEM, *([_HBM] * nb), pl.BlockSpec(memory_space=pltpu.VMEM)),
        input_output_aliases={k: 2 + k for k in range(nb)},
        compiler_params=pltpu.CompilerParams(has_side_effects=_DATAFLOW),
    )(*[_in_hbm(t) for t in (*srcs, *lands)], *after)
    return outs[0], outs[1], list(outs[2:2 + ns]), list(outs[2 + ns:2 + nb]), outs[-1]


def ici_wait(kind, started, after, *, name):
    send_sems, recv_sems, srcs, lands, _ = started
    ns, nb = len(srcs), len(srcs) + len(lands)

    def body(*refs):
        for cp in _ici_copies(kind, refs[:ns], refs[ns:nb], refs[nb], refs[nb + 1]):
            cp.wait_send()
        for cp in _ici_copies(kind, refs[:ns], refs[ns:nb], refs[nb], refs[nb + 1], arrivals=True):
            cp.wait_recv()

    outs = pl.pallas_call(
        body, name=name,
        out_shape=tuple(pltpu.HBM(t.shape, t.dtype) for t in (*srcs, *lands)),
        in_specs=[_HBM] * nb + [_SEM, _SEM] + [_ANY] * len(after),
        out_specs=tuple([_HBM] * nb),
        input_output_aliases={k: k for k in range(nb)},
        compiler_params=pltpu.CompilerParams(has_side_effects=_DATAFLOW),
    )(*srcs, *lands, send_sems, recv_sems, *after)
    return list(outs[:ns]), list(outs[ns:])


def all_reduce_small(v, *, name):
    r = v.shape[0]

    def body(v_ref, o_ref, buf, send_sems, recv_sems):
        x, y, c = _mesh_pos()
        me = 4 * x + 2 * y + c
        buf[me] = v_ref[...]
        copies = []
        for k in range(1, N_DEV):
            to = ((x + (k >> 2)) % 2, (y + ((k >> 1) & 1)) % 2, (c + (k & 1)) % 2)
            copies.append(pltpu.make_async_remote_copy(
                src_ref=v_ref, dst_ref=buf.at[me], send_sem=send_sems.at[k - 1], recv_sem=recv_sems.at[k - 1],
                device_id=to, device_id_type=MESH))
        for cp in copies:
            cp.start()
        for cp in copies:
            cp.wait()
        acc = buf[0]
        for d in range(1, N_DEV):
            acc = acc + buf[d]
        o_ref[...] = acc

    vm = pl.BlockSpec(memory_space=pltpu.VMEM)
    return pl.pallas_call(
        body, out_shape=jax.ShapeDtypeStruct(v.shape, F32), in_specs=[vm], out_specs=vm,
        scratch_shapes=[pltpu.VMEM((N_DEV, r, LANES), F32), pltpu.SemaphoreType.DMA((N_DEV - 1,)),
                        pltpu.SemaphoreType.DMA((N_DEV - 1,))],
        name=name,
    )(v)


def pair_add(g, r1, core, *, name):
    _, _, rows, cols = g.shape
    tm = _pick(rows, (256, 128))

    def body(c_ref, g_ref, r_ref, o_ref):
        o_ref[...] = (g_ref[...].astype(F32) + r_ref[...].astype(F32)).astype(o_ref.dtype)

    return pl.pallas_call(
        body, out_shape=jax.ShapeDtypeStruct(r1.shape, g.dtype),
        grid_spec=pltpu.PrefetchScalarGridSpec(
            num_scalar_prefetch=1, grid=(N_CHIP, rows // tm),
            in_specs=[pl.BlockSpec((None, None, tm, cols), lambda k, i, c_ref: (k, c_ref[0], i, 0)),
                      pl.BlockSpec((None, tm, cols), lambda k, i, c_ref: (k, i, 0))],
            out_specs=pl.BlockSpec((None, tm, cols), lambda k, i, c_ref: (k, i, 0))),
        name=name, compiler_params=_cparams(("parallel", "parallel")),
    )(core, g, r1)


def _adamw_math(w, g, m, v):
    m = ADAM_B1 * m + (1.0 - ADAM_B1) * g
    v = ADAM_B2 * v + (1.0 - ADAM_B2) * (g * g)
    m_hat = m / (1.0 - ADAM_B1 ** ADAM_STEP)
    v_hat = v / (1.0 - ADAM_B2 ** ADAM_STEP)
    delta = -ADAM_LR * (m_hat / (jnp.sqrt(v_hat) + ADAM_EPS) + ADAM_WD * w)
    return delta, m, v


def adamw_sharded(w, m, v, layer, chip_sums, received, chip, into, *, name):
    _, rows, cols = w.shape
    tm = _pick(rows, (256, 128))

    def body(k_ref, w_ref, m_ref, v_ref, t_ref, r_ref, *rest):
        g_ref, d_ref, nm_ref, nv_ref = rest[-4:]
        g = t_ref[...].astype(F32)
        for j in range(N_CHIP - 1):
            g = g + r_ref[j].astype(F32)
        d, mm, vv = _adamw_math(w_ref[...], g, m_ref[...], v_ref[...])
        g_ref[...] = g
        d_ref[...] = d
        nm_ref[...] = mm
        nv_ref[...] = vv

    blk = pl.BlockSpec((None, tm, cols), lambda i, k_ref: (layer, i, 0))
    shp = jax.ShapeDtypeStruct(w.shape, F32)
    in_specs = [blk, blk, blk,
                pl.BlockSpec((None, tm, cols), lambda i, k_ref: (k_ref[0], i, 0)),
                pl.BlockSpec((N_CHIP - 1, tm, cols), lambda i, k_ref: (0, i, 0))]
    operands = [chip, w, m, v, chip_sums, received]
    aliases = {}
    if into is not None:
        aliases = {len(operands) + q: q for q in range(4)}
        in_specs += [_ANY] * 4
        operands += list(into)
    return pl.pallas_call(
        body, out_shape=(shp, shp, shp, shp),
        grid_spec=pltpu.PrefetchScalarGridSpec(
            num_scalar_prefetch=1, grid=(rows // tm,), in_specs=in_specs, out_specs=(blk, blk, blk, blk)),
        input_output_aliases=aliases,
        name=name, compiler_params=_cparams(("parallel",)),
    )(*operands)


def adamw_replicated(w, m, v, g, *, name):
    def body(w_ref, m_ref, v_ref, g_ref, d_ref, nm_ref, nv_ref):
        d, mm, vv = _adamw_math(w_ref[...], g_ref[...], m_ref[...], v_ref[...])
        d_ref[...] = d
        nm_ref[...] = mm
        nv_ref[...] = vv

    shp = jax.ShapeDtypeStruct(w.shape, F32)
    return pl.pallas_call(body, out_shape=(shp, shp, shp), name=name, compiler_params=_cparams())(w, m, v, g)


WEIGHT_NAMES = ("norm_w", "ssd_in_w", "ssd_conv_w", "ssd_conv_b", "ssd_dt_bias", "ssd_a_log", "ssd_d",
                "ssd_gnorm_w", "ssd_out_w", "sb_in_w", "sb_qn_w", "sb_kn_w", "sb_out_w", "ple_norm_w",
                "ple_gate_w", "ple_proj_w")
REPLICATED = ("norm_w", "ssd_conv_b", "ssd_dt_bias", "ssd_a_log", "ssd_d", "ssd_gnorm_w", "sb_qn_w", "sb_kn_w",
              "ple_norm_w")
PACK_ROWS = 8


def _pack(parts):
    flat = jnp.concatenate([t.reshape(-1) for t in parts])
    pad = (-flat.shape[0]) % (PACK_ROWS * LANES)
    return jnp.pad(flat, (0, pad)).reshape(-1, LANES)


def _unpack(packed, like):
    flat = packed.reshape(-1)
    out, off = [], 0
    for t in like:
        out.append(flat[off:off + t.size].reshape(t.shape))
        off += t.size
    return out


def _to_group_lanes(v, r):
    t = v.reshape(v.shape[0], SSD_N_GROUPS, r).transpose(1, 0, 2)
    return jnp.pad(t, ((0, 0), (0, 0), (0, LANES - r)))


def _from_group_lanes(t, r):
    return t[:, :, :r].transpose(1, 0, 2).reshape(t.shape[1], SSD_N_GROUPS * r)


def _head_vec(v, r):
    return jnp.pad(v.reshape(SSD_N_GROUPS, 1, r), ((0, 0), (0, 0), (0, LANES - r)))


def _col_blocks(full):
    rows = full.shape[0]
    return full.reshape(rows, N_DEV, -1).transpose(1, 0, 2)


def _from_col_blocks(blocks):
    return blocks.transpose(1, 0, 2).reshape(blocks.shape[1], -1)


def _split_cols(full, widths):
    out, off = [], 0
    for w in widths:
        out.append(full[:, off:off + w])
        off += w
    return out


def kernel(x, p, norm_w, ssd_in_w, ssd_conv_w, ssd_conv_b, ssd_dt_bias, ssd_a_log, ssd_d, ssd_gnorm_w, ssd_out_w, sb_in_w, sb_qn_w, sb_kn_w, sb_out_w, ple_norm_w, ple_gate_w, ple_proj_w, loss_target, m_norm_w, m_ssd_in_w, m_ssd_conv_w, m_ssd_conv_b, m_ssd_dt_bias, m_ssd_a_log, m_ssd_d, m_ssd_gnorm_w, m_ssd_out_w, m_sb_in_w, m_sb_qn_w, m_sb_kn_w, m_sb_out_w, m_ple_norm_w, m_ple_gate_w, m_ple_proj_w, v_norm_w, v_ssd_in_w, v_ssd_conv_w, v_ssd_conv_b, v_ssd_dt_bias, v_ssd_a_log, v_ssd_d, v_ssd_gnorm_w, v_ssd_out_w, v_sb_in_w, v_sb_qn_w, v_sb_kn_w, v_sb_out_w, v_ple_norm_w, v_ple_gate_w, v_ple_proj_w):
    env = dict(locals())
    wts = {n: env[n] for n in WEIGHT_NAMES}
    mom1 = {n: env["m_" + n] for n in WEIGHT_NAMES}
    mom2 = {n: env["v_" + n] for n in WEIGHT_NAMES}

    s, d = x.shape[1], x.shape[2]
    depth = norm_w.shape[0]
    n_ssd, n_sb = ssd_in_w.shape[0], sb_in_w.shape[0]
    di = ssd_out_w.shape[1] * N_DEV
    n_heads = ssd_dt_bias.shape[1]
    hpg = n_heads // SSD_N_GROUPS
    nbc = SSD_N_GROUPS * SSD_D_STATE
    in_segs = (di, di, nbc, nbc, n_heads)
    conv_segs = (di, nbc, nbc)
    sb_w = sb_out_w.shape[1] * N_DEV
    selectors = ssd_selectors(hpg)
    xi, yi, ci = _mesh_pos()
    core = ci.astype(jnp.int32).reshape(1)
    chip = (2 * xi + yi).astype(jnp.int32).reshape(1)

    def layer_keys(i):
        j = i // 2
        mixer = [("ssd_in_w", j), ("ssd_conv_w", j), ("ssd_out_w", j)] if i % 2 == 0 else [("sb_in_w", j), ("sb_out_w", j)]
        return mixer + [("ple_gate_w", i), ("ple_proj_w", i)]

    me_block = 4 * xi + 2 * yi + ci

    def landing_zone(t):
        return lax.dynamic_update_index_in_dim(lax.empty((N_DEV,) + t.shape, t.dtype), t, me_block, 0)

    def groups(i):
        keys = layer_keys(i)
        return [keys[:2], keys[2:]] if i == 0 else [keys]

    gathers, prev = {}, []
    for i in range(depth):
        for q, keys in enumerate(groups(i)):
            shards = [wts[n][idx] for n, idx in keys]
            if prev:
                shards = lax.optimization_barrier((prev[0], shards))[1]
            lands = [landing_zone(t if n == "ssd_conv_w" else t.astype(BF16)) for (n, _), t in zip(keys, shards)]
            gathers[i, q] = ici_start("gather", [], lands, after=prev, name=f"ag{i}{'ab'[q]}_start")
            prev = [gathers[i, q][4]]
    all_started = prev[0]
    full, ssd_full, passing = {}, {}, {}

    def hand_over(i, q, after):
        _, lands = ici_wait("gather", gathers[i, q], after, name=f"ag{i}{'ab'[q]}_wait")
        passing[i, q] = ici_start("pass", [], [t.reshape(N_CHIP, 2, *t.shape[1:]) for t in lands],
                                  name=f"ag{i}{'ab'[q]}_pass_start")

    def arrive(i, q, after):
        _, lands = ici_wait("pass", passing[i, q], after, name=f"ag{i}{'ab'[q]}_pass_wait")
        for k, t in zip(groups(i)[q], lands):
            full[k] = t.reshape(N_DEV, *t.shape[2:])

    def w_out_of(i):
        return full["ssd_out_w", i // 2].reshape(di, d) if i % 2 == 0 else full["sb_out_w", i // 2].reshape(sb_w, d)

    h = x.reshape(s, d)
    saved = []
    hand_over(0, 0, [all_started])
    arrive(0, 0, [all_started])
    for i in range(depth):
        j = i // 2
        if i > 0:
            arrive(i, 0, [h])
        sv = dict(h_in=h)
        u = rmsnorm_fwd(h, norm_w[i], name=f"l{i}_norm")
        sv["u"] = u
        if i % 2 == 0:
            fw = ssd_full[j] = dict(
                w_in=_split_cols(_from_col_blocks(full["ssd_in_w", j]), in_segs),
                conv_w=_split_cols(_from_col_blocks(full["ssd_conv_w", j]), conv_segs),
                conv_b=_split_cols(ssd_conv_b[j].reshape(1, -1), conv_segs))
            raw = [matmul(u, wseg, name=f"l{i}_in{q}") for q, wseg in enumerate(fw["w_in"])]
            if i == 0:
                hand_over(0, 1, [raw[4]])
            z, dt_raw = raw[0], raw[4]
            act = [ssd_conv_fwd(raw[1 + q], fw["conv_w"][q], fw["conv_b"][q], name=f"l{i}_conv{q}") for q in range(3)]
            dt, a_neg = ssd_dt_fwd(dt_raw, ssd_dt_bias[j], ssd_a_log[j], name=f"l{i}_dt")
            dtp = _to_group_lanes(dt, hpg)
            a_g = _head_vec(a_neg.reshape(-1), hpg)
            d_x = jnp.repeat(ssd_d[j].reshape(SSD_N_GROUPS, 1, hpg), SSD_HEAD_DIM, axis=2)
            y, states = ssd_scan_fwd(act[0], act[1], act[2], dtp, a_g, d_x, selectors, heads_per_group=hpg,
                                     name=f"l{i}_scan")
            yn = ssd_gate_fwd(y, z, ssd_gnorm_w[j], name=f"l{i}_gate")
            if i == 0:
                arrive(0, 1, [yn])
            h1 = matmul(yn, w_out_of(i), res=h, name=f"l{i}_out")
            sv.update(raw=raw, act=act, dt=dt, dtp=dtp, a_g=a_g, d_x=d_x, y=y, states=states, yn=yn)
        else:
            proj = matmul(u, full["sb_in_w", j], name=f"l{i}_in")
            qn, kn = sb_qk_fwd(proj, sb_qn_w[j], sb_kn_w[j], name=f"l{i}_qknorm")
            v_off = 2 * sb_w // SB_HEAD_DIM
            o, tot = sb_attn_fwd(qn, kn, proj, v_off=v_off, name=f"l{i}_attn")
            og = sb_gate_fwd(o, proj, name=f"l{i}_gate")
            h1 = matmul(og, w_out_of(i), res=h, name=f"l{i}_out")
            sv.update(proj=proj, qn=qn, kn=kn, o=o, tot=tot, og=og, v_off=v_off)
        if i + 1 < depth:
            hand_over(i + 1, 0, [h1])
        t = rmsnorm_fwd(h1, ple_norm_w[i], passing[i + 1, 0][4] if i + 1 < depth else None, name=f"l{i}_plenorm")
        gate_pre = matmul(t, full["ple_gate_w", i].reshape(d, d), name=f"l{i}_plegate")
        pp = matmul(p[i, 0], full["ple_proj_w", i], name=f"l{i}_pleproj")
        h = ple_fwd(h1, gate_pre, pp, name=f"l{i}_ple")
        sv.update(h1=h1, t=t, gate_pre=gate_pre, pp=pp)
        saved.append(sv)

    loss_part, dh = loss_head(h, loss_target.reshape(s, d), name="loss_head")
    loss = lax.psum(loss_part[0, 0], ("x", "y", "c"))

    big = {}
    small = {n: [None] * wts[n].shape[0] for n in REPLICATED}
    swaps, scatters = {}, {}
    order_after = jnp.zeros(TOKEN_SHAPE, F32)
    pending = None

    def send_to_sibling(i, q):
        blocks = [big[k].reshape(N_CHIP, 2, *big[k].shape[1:]) for k in groups(i)[::-1][q]]
        swaps[i, q] = ici_start("swap", blocks, [lax.empty((N_CHIP,) + t.shape[2:], t.dtype) for t in blocks],
                                name=f"rs{i}{'ab'[q]}_swap_start")
        return swaps[i, q][4]

    def send_to_chips(i, q, after):
        blocks, from_sibling = ici_wait("swap", swaps[i, q], after, name=f"rs{i}{'ab'[q]}_swap_wait")
        sums = [pair_add(g, r1, core, name=f"rs{i}{'ab'[q]}_pair_add{a}")
                for a, (g, r1) in enumerate(zip(blocks, from_sibling))]
        scatters[i, q] = ici_start("scatter", sums, [lax.empty((N_PEER_CHIPS,) + t.shape[1:], t.dtype) for t in sums],
                                   name=f"rs{i}{'ab'[q]}_start")
        return scatters[i, q][4]

    for i in reversed(range(depth)):
        j = i // 2
        sv = saved[i]
        dpp, dgp = ple_bwd(dh, sv["gate_pre"], sv["pp"], order_after, name=f"b{i}_ple")
        big["ple_proj_w", i] = matmul(p[i, 0], dpp, mode="tn", out_dtype=BF16, out_blocks=ple_proj_w.shape[2],
                                      name=f"b{i}_pleproj_w")
        big["ple_gate_w", i] = matmul(sv["t"], dgp, mode="tn", out_dtype=BF16, name=f"b{i}_plegate_w").reshape(N_DEV, -1, d)
        dt_ = matmul(dgp, full["ple_gate_w", i].reshape(d, d), mode="nt", name=f"b{i}_plegate_x")
        dh1, g_pn = rmsnorm_bwd(sv["h1"], ple_norm_w[i], dt_, dh, name=f"b{i}_plenorm")
        small["ple_norm_w"][i] = g_pn
        behind = send_to_chips(*pending, [dh1]) if pending is not None else None
        pending = None
        u = sv["u"]
        if i % 2 == 0:
            fw = ssd_full[j]
            raw, act = sv["raw"], sv["act"]
            big["ssd_out_w", j] = matmul(sv["yn"], dh1, mode="tn", out_dtype=BF16, name=f"b{i}_out_w").reshape(N_DEV, -1, d)
            if i == 0:
                send_to_sibling(0, 0)
            dyn = matmul(dh1, w_out_of(i), mode="nt", after=behind, name=f"b{i}_out_x")
            dy, dz, g_gn = ssd_gate_bwd(sv["y"], raw[0], ssd_gnorm_w[j], dyn, name=f"b{i}_gate")
            dxs, dbm, dcm, ddtp, dadtp, dd_g = ssd_scan_bwd(act[0], act[1], act[2], sv["dtp"], sv["a_g"], sv["d_x"], selectors,
                                                          sv["states"], dy, heads_per_group=hpg, name=f"b{i}_scan")
            behind = send_to_chips(0, 0, [dxs]) if i == 0 else None
            ddt_raw, g_dtb, g_alog = ssd_dt_bwd(raw[4], ssd_dt_bias[j], ssd_a_log[j], sv["dt"],
                                                _from_group_lanes(ddtp, hpg), _from_group_lanes(dadtp, hpg),
                                                name=f"b{i}_dt")
            conv_back = [ssd_conv_bwd(raw[1 + q], fw["conv_w"][q], fw["conv_b"][q], dact, name=f"b{i}_conv{q}")
                         for q, dact in enumerate((dxs, dbm, dcm))]
            dsegs = [dz] + [cb[0] for cb in conv_back] + [ddt_raw]
            g_in = jnp.concatenate([matmul(u, ds, mode="tn", out_dtype=BF16, name=f"b{i}_in{q}_w")
                                    for q, ds in enumerate(dsegs)], axis=1)
            big["ssd_in_w", j] = _col_blocks(g_in)
            big["ssd_conv_w", j] = _col_blocks(jnp.concatenate([cb[1] for cb in conv_back], axis=1))
            du = None
            for q, (ds, wseg) in enumerate(zip(dsegs, fw["w_in"])):
                du = matmul(ds, wseg, mode="nt", res=du, after=behind if q == 0 else None, name=f"b{i}_in{q}_x")
            small["ssd_conv_b"][j] = jnp.concatenate([cb[2] for cb in conv_back], axis=1)
            small["ssd_dt_bias"][j] = g_dtb
            small["ssd_a_log"][j] = g_alog
            small["ssd_d"][j] = dd_g[:, 0, :hpg]
            small["ssd_gnorm_w"][j] = g_gn
        else:
            proj = sv["proj"]
            big["sb_out_w", j] = matmul(sv["og"], dh1, mode="tn", out_dtype=BF16, name=f"b{i}_out_w").reshape(N_DEV, -1, d)
            dog = matmul(dh1, w_out_of(i), mode="nt", after=behind, name=f"b{i}_out_x")
            do, dg = sb_gate_bwd(dog, sv["o"], proj, name=f"b{i}_gate")
            dqn, dkn, dv = sb_attn_bwd(sv["qn"], sv["kn"], proj, sv["tot"], do, v_off=sv["v_off"], name=f"b{i}_attn")
            dproj, g_qn, g_kn = sb_pack_bwd(proj, sb_qn_w[j], sb_kn_w[j], dqn, dkn, dv, dg, name=f"b{i}_qknorm")
            big["sb_in_w", j] = matmul(u, dproj, mode="tn", out_dtype=BF16, out_blocks=sb_in_w.shape[2], name=f"b{i}_in_w")
            du = matmul(dproj, full["sb_in_w", j], mode="nt", name=f"b{i}_in_x")
            small["sb_qn_w"][j] = g_qn
            small["sb_kn_w"][j] = g_kn
        dh, g_n = rmsnorm_bwd(sv["h_in"], norm_w[i], du, dh1, name=f"b{i}_norm")
        small["norm_w"][i] = g_n
        pending = (i, len(groups(i)) - 1)
        order_after = send_to_sibling(*pending)
    send_to_chips(*pending, [dh])
    grad_x = dh.reshape(x.shape)

    rep_like = [wts[n] for n in REPLICATED]
    g_packed = all_reduce_small(_pack([jnp.stack([t.reshape(-1) for t in small[n]]) for n in REPLICATED]),
                                name="all_reduce_small_grads")
    d_packed, m_packed, v_packed = adamw_replicated(
        _pack(rep_like), _pack([mom1[n] for n in REPLICATED]), _pack([mom2[n] for n in REPLICATED]), g_packed,
        name="adamw_replicated")
    grads = dict(zip(REPLICATED, _unpack(g_packed, rep_like)))
    deltas = dict(zip(REPLICATED, _unpack(d_packed, rep_like)))
    new_m = dict(zip(REPLICATED, _unpack(m_packed, rep_like)))
    new_v = dict(zip(REPLICATED, _unpack(v_packed, rep_like)))

    updated = {}
    after = [scatters[0, len(groups(0)) - 1][4]]
    for i in reversed(range(depth)):
        for q, keys in enumerate(groups(i)[::-1]):
            sums, received = ici_wait("scatter", scatters[i, q], after, name=f"rs{i}{'ab'[q]}_wait")
            for (n, idx), t_sum, recv in zip(keys, sums, received):
                updated[n] = adamw_sharded(wts[n], mom1[n], mom2[n], idx, t_sum, recv, chip, updated.get(n),
                                           name=f"adamw_{n}{idx}")
            after = [updated[n][0] for n, _ in keys]
    for n, (g_n, d_n, m_n, v_n) in updated.items():
        grads[n], deltas[n], new_m[n], new_v[n] = g_n, d_n, m_n, v_n

    return (loss, grad_x, *[grads[n] for n in WEIGHT_NAMES], *[deltas[n] for n in WEIGHT_NAMES],
            *[new_m[n] for n in WEIGHT_NAMES], *[new_v[n] for n in WEIGHT_NAMES])
```

```python
import functools
import math

import jax
import jax.numpy as jnp
from jax import lax
from jax.experimental import pallas as pl
from jax.experimental.pallas import tpu as pltpu

F32 = jnp.float32
BF16 = jnp.bfloat16
MESH = pl.DeviceIdType.MESH

N_DEV = 8
N_CHIP = 4
LANES = 128
VMEM_LIMIT_BYTES = 56 * 1024 * 1024
MATMUL_TILE_BYTES = 36 * 1024 * 1024

NORM_EPS = 1e-6
GATED_NORM_EPS = 1e-5
SSD_HEAD_DIM = 64
SSD_N_GROUPS = 8
SSD_D_STATE = 128
SSD_D_CONV = 4
SSD_CHUNK = 128
SB_HEAD_DIM = 128
PLE_DIM = 256

ADAM_LR = 0.001
ADAM_B1 = 0.9
ADAM_B2 = 0.999
ADAM_EPS = 1e-08
ADAM_WD = 0.01
ADAM_STEP = 10


def _cparams(sem=None, **kw):
    return pltpu.CompilerParams(dimension_semantics=sem, vmem_limit_bytes=VMEM_LIMIT_BYTES, **kw)


def _pick(dim, prefs):
    for t in prefs:
        if dim % t == 0:
            return t
    return dim


def _sigmoid(x):
    return 1.0 / (1.0 + jnp.exp(-x))


def _silu(x):
    return x * _sigmoid(x)


def _silu_grad(x):
    s = _sigmoid(x)
    return s * (1.0 + x * (1.0 - s))


def matmul(a, b, *, mode="nn", out_dtype=F32, res=None, out_blocks=None, after=None, name):
    b_blocked = b.ndim == 3
    if mode == "nn":
        m, kc = a.shape
        n = b.shape[-1] * (N_DEV if b_blocked else 1)
    elif mode == "nt":
        m, kc = a.shape
        n = b.shape[-2]
    else:
        kc, m = a.shape
        n = b.shape[-1]
    nb = b.shape[-1] if b_blocked else None
    tn = _pick(n if not out_blocks else out_blocks, (512, 256, 128))
    if b_blocked and mode == "nn":
        tn = _pick(nb, (512, 256, 128))
    k_unit = nb if (b_blocked and mode == "nt") else 1
    tm, tk = None, None
    for tm_try in (1024, 512, 256, 128):
        if m % tm_try:
            continue
        for tk_try in (kc, kc // 2, kc // 4, 2048, 1024, 512, 256, 128):
            if tk_try > kc or tk_try < k_unit or kc % tk_try or tk_try % k_unit:
                continue
            tiles = 2 * (tm_try * tk_try * a.dtype.itemsize + tk_try * tn * b.dtype.itemsize)
            tiles += tm_try * tn * (2 * jnp.dtype(out_dtype).itemsize + 4 + (8 if res is not None else 0))
            if tiles <= MATMUL_TILE_BYTES:
                tm, tk = tm_try, tk_try
                break
        if tm:
            break
    if tm is None:
        tm, tk = m, max(k_unit, LANES if kc % LANES == 0 else kc)
    nk = kc // tk
    grid = (m // tm, n // tn, nk)

    if mode == "tn":
        a_spec = pl.BlockSpec((tk, tm), lambda i, j, k: (k, i))
        dims = (((0,), (0,)), ((), ()))
    else:
        a_spec = pl.BlockSpec((tm, tk), lambda i, j, k: (i, k))
        dims = (((1,), (0,)), ((), ())) if mode == "nn" else (((1,), (1,)), ((), ()))
    if mode == "nt":
        if b_blocked:
            b_spec = pl.BlockSpec((tk // nb, tn, nb), lambda i, j, k: (k, j, 0))
        else:
            b_spec = pl.BlockSpec((tn, tk), lambda i, j, k: (j, k))
    else:
        if b_blocked:
            per = nb // tn
            b_spec = pl.BlockSpec((None, tk, tn), lambda i, j, k: (j // per, k, j % per))
        else:
            b_spec = pl.BlockSpec((tk, tn), lambda i, j, k: (k, j))
    if out_blocks:
        per_o = out_blocks // tn
        out_shape = jax.ShapeDtypeStruct((n // out_blocks, m, out_blocks), out_dtype)
        out_spec = pl.BlockSpec((None, tm, tn), lambda i, j, k: (j // per_o, i, j % per_o))
    else:
        out_shape = jax.ShapeDtypeStruct((m, n), out_dtype)
        out_spec = pl.BlockSpec((tm, tn), lambda i, j, k: (i, j))
    in_specs = [a_spec, b_spec]
    args = [a, b]
    if res is not None:
        in_specs.append(pl.BlockSpec((tm, tn), lambda i, j, k: (i, j)))
        args.append(res)
    if after is not None:
        in_specs.append(pl.BlockSpec(memory_space=pl.ANY))
        args.append(after)
    n_in = len(args)

    def body(*refs):
        a_ref, b_ref = refs[:2]
        r_ref = refs[2] if res is not None else None
        o_ref = refs[n_in]

        def finish(r):
            if res is not None:
                r = r + r_ref[...].astype(F32)
            o_ref[...] = r.astype(out_dtype)

        if b_blocked and mode == "nt":
            part = None
            for blk in range(tk // nb):
                term = lax.dot_general(a_ref[:, blk * nb:(blk + 1) * nb].astype(BF16), b_ref[blk].astype(BF16), dims,
                                       preferred_element_type=F32)
                part = term if part is None else part + term
        else:
            part = lax.dot_general(a_ref[...].astype(BF16), b_ref[...].astype(BF16), dims, preferred_element_type=F32)
        if nk == 1:
            finish(part)
            return
        acc_ref = refs[-1]
        k = pl.program_id(2)

        @pl.when(k == 0)
        def _():
            acc_ref[...] = part

        @pl.when(k > 0)
        def _():
            acc_ref[...] += part

        @pl.when(k == nk - 1)
        def _():
            finish(acc_ref[...])

    return pl.pallas_call(
        body, out_shape=out_shape, grid=grid, in_specs=in_specs, out_specs=out_spec,
        scratch_shapes=[] if nk == 1 else [pltpu.VMEM((tm, tn), F32)], name=name,
        compiler_params=_cparams(("parallel", "parallel", "arbitrary")),
    )(*args)


def _dot(a, b, dims, precision=None):
    return lax.dot_general(a, b, (dims, ((), ())), preferred_element_type=F32, precision=precision)


_NN = ((1,), (0,))
_NT = ((1,), (1,))
_TN = ((0,), (0,))
_EXACT = lax.Precision.HIGHEST


def _chunk_decay_terms(dt, a):
    ln = dt.shape[0]
    row = lax.broadcasted_iota(jnp.int32, (ln, ln), 0)
    col = lax.broadcasted_iota(jnp.int32, (ln, ln), 1)
    tri = (row >= col).astype(F32)
    a_col = _dot(tri, dt * a, _NN, _EXACT)
    return a_col, a_col.T, row >= col


def _exact_dot(x, sel, terms):
    t = x.shape[0]
    parts, rest = [], x
    for k in range(terms):
        piece = rest.astype(BF16)
        parts.append(piece)
        if k + 1 < terms:
            rest = rest - piece.astype(F32)
    r = _dot(jnp.concatenate(parts, axis=0), sel, _NN)
    out = r[:t]
    for k in range(1, terms):
        out = out + r[k * t:(k + 1) * t]
    return out


def ssd_selectors(r_n):
    lane = jnp.arange(LANES)
    spread64 = (lane[:, None] == jnp.arange(r_n * SSD_HEAD_DIM)[None, :] // SSD_HEAD_DIM).astype(BF16)
    pair_sum = jnp.stack([lane[None, :] == 2 * q + lane[:, None] // SSD_HEAD_DIM for q in range(r_n // 2)]).astype(BF16)
    row_sum = jnp.stack([jnp.broadcast_to(lane[None, :] == r, (LANES, LANES)) for r in range(r_n)]).astype(BF16)
    return spread64, pair_sum, row_sum


def _ssd_chunk_setup(dt, a, spread64):
    ln = dt.shape[0]
    a_col, a_row, causal = _chunk_decay_terms(dt, a)
    ea = jnp.exp(a_col)
    te = jnp.exp(a_col[ln - 1:ln, :] - a_col)
    return (a_row, a_col, _exact_dot(dt, spread64, 2), _exact_dot(ea, spread64, 2), _exact_dot(te, spread64, 2),
            ea, causal)


def ssd_scan_fwd(xs, bm, cm, dtp, a_g, d_x, selectors, *, heads_per_group, name):
    s, di = xs.shape
    g_n = SSD_N_GROUPS
    r_n, p_n, n_n, ln = heads_per_group, SSD_HEAD_DIM, SSD_D_STATE, SSD_CHUNK
    nc = s // ln
    pairs, pw = r_n // 2, 2 * p_n
    spread64 = selectors[0]

    def body(xs_ref, bm_ref, cm_ref, dt_ref, a_ref, d_ref, s64_ref, y_ref, st_ref, state):
        c = pl.program_id(1)

        @pl.when(c == 0)
        def _():
            state[...] = jnp.zeros_like(state)

        a_row, a_col, dt_x, ea_x, te_x, _, causal = _ssd_chunk_setup(dt_ref[...], a_ref[...], s64_ref[...])
        bm_f = bm_ref[...]
        bmb = bm_f.astype(BF16)
        bm_t = bm_f.T.astype(BF16)
        cmb = cm_ref[...].astype(BF16)
        scores = _dot(cmb, bmb, _NT)
        first_head = lax.broadcasted_iota(jnp.int32, (1, pw), 1) < p_n
        for q in range(pairs):
            sl = slice(q * pw, (q + 1) * pw)
            x2 = xs_ref[:, sl]
            xdt2 = x2 * dt_x[:, sl]
            xdt2b = xdt2.astype(BF16)
            y_heads = []
            for r in (2 * q, 2 * q + 1):
                decay = jnp.exp(jnp.where(causal, a_col[:, r:r + 1] - a_row[r:r + 1, :], -jnp.inf))
                y_heads.append(_dot((scores * decay).astype(BF16), xdt2b, _NN))
            s2t = state[q]
            st_ref[q] = s2t
            y2 = jnp.where(first_head, y_heads[0], y_heads[1])
            y2 = y2 + ea_x[:, sl] * _dot(cmb, s2t.astype(BF16), _NN)
            y_ref[:, sl] = y2 + d_ref[:, sl] * x2
            state[q] = s2t * ea_x[ln - 1:ln, sl] + _dot(bm_t, (xdt2 * te_x[:, sl]).astype(BF16), _NN)

    whole = lambda t: pl.BlockSpec(t.shape, lambda g, c: (0,) * t.ndim)
    return pl.pallas_call(
        body,
        out_shape=(jax.ShapeDtypeStruct((s, di), F32),
                   jax.ShapeDtypeStruct((nc, g_n * pairs, n_n, pw), F32)),
        grid=(g_n, nc),
        in_specs=[pl.BlockSpec((ln, r_n * p_n), lambda g, c: (c, g)),
                  pl.BlockSpec((ln, n_n), lambda g, c: (c, g)),
                  pl.BlockSpec((ln, n_n), lambda g, c: (c, g)),
                  pl.BlockSpec((None, ln, LANES), lambda g, c: (g, c, 0)),
                  pl.BlockSpec((None, 1, LANES), lambda g, c: (g, 0, 0)),
                  pl.BlockSpec((None, 1, r_n * p_n), lambda g, c: (g, 0, 0)),
                  whole(spread64)],
        out_specs=(pl.BlockSpec((ln, r_n * p_n), lambda g, c: (c, g)),
                   pl.BlockSpec((None, pairs, n_n, pw), lambda g, c: (c, g, 0, 0))),
        scratch_shapes=[pltpu.VMEM((pairs, n_n, pw), F32)],
        name=name, compiler_params=_cparams(("parallel", "arbitrary")),
    )(xs, bm, cm, dtp, a_g, d_x, spread64)


def _row8(v):
    return jnp.broadcast_to(v, (8, v.shape[1]))


def ssd_scan_bwd(xs, bm, cm, dtp, a_g, d_x, selectors, states, dy, *, heads_per_group, name):
    s, di = xs.shape
    g_n = SSD_N_GROUPS
    r_n, p_n, n_n, ln = heads_per_group, SSD_HEAD_DIM, SSD_D_STATE, SSD_CHUNK
    nc = s // ln
    pairs, pw = r_n // 2, 2 * p_n
    spread64, pair_sum, row_sum = selectors

    def body(xs_ref, bm_ref, cm_ref, dt_ref, a_ref, d_ref, s64_ref, ps_ref, rs_ref, st_ref, dy_ref,
             dxs_ref, dbm_ref, dcm_ref, ddt_ref, dadt_ref, dd_ref, dstate, da_rows):
        c = pl.program_id(1)

        @pl.when(c == 0)
        def _():
            dstate[...] = jnp.zeros_like(dstate)
            dd_ref[...] = jnp.zeros_like(dd_ref)

        a_row, a_col, dt_x, ea_x, te_x, ea, causal = _ssd_chunk_setup(dt_ref[...], a_ref[...], s64_ref[...])
        row = lax.broadcasted_iota(jnp.int32, (ln, ln), 0)
        col = lax.broadcasted_iota(jnp.int32, (ln, ln), 1)
        causal_t = col >= row
        bmb = bm_ref[...].astype(BF16)
        cm_f = cm_ref[...]
        cmb = cm_f.astype(BF16)
        cm_t = cm_f.T.astype(BF16)
        scores = _dot(cmb, bmb, _NT)
        scores_t = _dot(bmb, cmb, _NT)
        first_head = lax.broadcasted_iota(jnp.int32, (1, pw), 1) < p_n
        e_last = ea[ln - 1:ln, :]
        da_rows[...] = jnp.zeros_like(da_rows)
        dscores = jnp.zeros((ln, ln), F32)
        dcm = jnp.zeros((ln, n_n), F32)
        dbm = jnp.zeros((ln, n_n), F32)
        da_cols = jnp.zeros((ln, LANES), F32)
        da_last = jnp.zeros((1, LANES), F32)
        ddt = jnp.zeros((ln, LANES), F32)
        dd = jnp.zeros((1, LANES), F32)
        for q in range(pairs):
            sl = slice(q * pw, (q + 1) * pw)
            sum2 = ps_ref[q]
            x2 = xs_ref[:, sl]
            dt2 = dt_x[:, sl]
            xdt2 = x2 * dt2
            xdt2b = xdt2.astype(BF16)
            dy2 = dy_ref[:, sl]
            dy2b = dy2.astype(BF16)
            dxdt_heads = []
            for h, r in enumerate((2 * q, 2 * q + 1)):
                a_r = jnp.broadcast_to(a_col[:, r:r + 1], (ln, ln))
                decay = jnp.exp(jnp.where(causal, a_r - a_row[r:r + 1, :], -jnp.inf))
                decay_t = jnp.exp(jnp.where(causal_t, a_row[r:r + 1, :] - a_r, -jnp.inf))
                dy_h = jnp.where(first_head if h == 0 else jnp.logical_not(first_head), dy2, 0.0).astype(BF16)
                dm = _dot(dy_h, xdt2b, _NT)
                dscores = dscores + dm * decay
                e_mat = dm * (scores * decay)
                da_cols = da_cols + _exact_dot(e_mat, rs_ref[r], 2)
                da_rows[r:r + 1, :] = -jnp.sum(e_mat, axis=0, keepdims=True)
                dxdt_heads.append(_dot((scores_t * decay_t).astype(BF16), dy2b, _NN))
            dxdt2 = jnp.where(first_head, dxdt_heads[0], dxdt_heads[1])
            s2t = st_ref[q]
            s2tb = s2t.astype(BF16)
            ds2t = dstate[q]
            ds2tb = ds2t.astype(BF16)
            ea2, te2 = ea_x[:, sl], te_x[:, sl]
            y_off2 = ea2 * _dot(cmb, s2tb, _NN)
            dy_e2 = (dy2 * ea2).astype(BF16)
            dcm = dcm + _dot(dy_e2, s2tb, _NT)
            ds_in = _dot(cm_t, dy_e2, _NN)
            da_cols = da_cols + _exact_dot(dy2 * y_off2, sum2, 2)
            bds2 = _dot(bmb, ds2tb, _NN)
            dxdt2 = dxdt2 + te2 * bds2
            xdt_e2 = xdt2 * te2
            dbm = dbm + _dot(xdt_e2.astype(BF16), ds2tb, _NT)
            w_cols = _exact_dot(xdt_e2 * bds2, sum2, 2)
            da_cols = da_cols - w_cols
            state_dot = _exact_dot(_row8(jnp.sum(ds2t * s2t, axis=0, keepdims=True)), sum2, 2)[0:1]
            da_last = da_last + jnp.sum(w_cols, axis=0, keepdims=True) + e_last * state_dot
            dstate[q] = ds2t * ea_x[ln - 1:ln, sl] + ds_in
            dxs_ref[:, sl] = dxdt2 * dt2 + d_ref[:, sl] * dy2
            ddt = ddt + _exact_dot(dxdt2 * x2, sum2, 2)
            dd = dd + _exact_dot(_row8(jnp.sum(dy2 * x2, axis=0, keepdims=True)), sum2, 2)[0:1]
        dcm_ref[...] = dcm + _dot(dscores.astype(BF16), bmb, _NN)
        dbm_ref[...] = dbm + _dot(dscores.T.astype(BF16), cmb, _NN)
        da_total = da_cols + da_rows[...].T
        upper = causal_t.astype(F32)
        dadt_ref[...] = _dot(upper, da_total, _NN, _EXACT) + da_last
        ddt_ref[...] = ddt
        dd_ref[...] += dd

    last_c = nc - 1
    whole = lambda t: pl.BlockSpec(t.shape, lambda g, c: (0,) * t.ndim)
    return pl.pallas_call(
        body,
        out_shape=(jax.ShapeDtypeStruct((s, di), F32),
                   jax.ShapeDtypeStruct(bm.shape, F32),
                   jax.ShapeDtypeStruct(cm.shape, F32),
                   jax.ShapeDtypeStruct(dtp.shape, F32),
                   jax.ShapeDtypeStruct(dtp.shape, F32),
                   jax.ShapeDtypeStruct(a_g.shape, F32)),
        grid=(g_n, nc),
        in_specs=[pl.BlockSpec((ln, r_n * p_n), lambda g, c: (last_c - c, g)),
                  pl.BlockSpec((ln, n_n), lambda g, c: (last_c - c, g)),
                  pl.BlockSpec((ln, n_n), lambda g, c: (last_c - c, g)),
                  pl.BlockSpec((None, ln, LANES), lambda g, c: (g, last_c - c, 0)),
                  pl.BlockSpec((None, 1, LANES), lambda g, c: (g, 0, 0)),
                  pl.BlockSpec((None, 1, r_n * p_n), lambda g, c: (g, 0, 0)),
                  whole(spread64), whole(pair_sum), whole(row_sum),
                  pl.BlockSpec((None, pairs, n_n, pw), lambda g, c: (last_c - c, g, 0, 0)),
                  pl.BlockSpec((ln, r_n * p_n), lambda g, c: (last_c - c, g))],
        out_specs=(pl.BlockSpec((ln, r_n * p_n), lambda g, c: (last_c - c, g)),
                   pl.BlockSpec((ln, n_n), lambda g, c: (last_c - c, g)),
                   pl.BlockSpec((ln, n_n), lambda g, c: (last_c - c, g)),
                   pl.BlockSpec((None, ln, LANES), lambda g, c: (g, last_c - c, 0)),
                   pl.BlockSpec((None, ln, LANES), lambda g, c: (g, last_c - c, 0)),
                   pl.BlockSpec((None, 1, LANES), lambda g, c: (g, 0, 0))),
        scratch_shapes=[pltpu.VMEM((pairs, n_n, pw), F32), pltpu.VMEM((LANES, ln), F32)],
        name=name, compiler_params=_cparams(("parallel", "arbitrary")),
    )(xs, bm, cm, dtp, a_g, d_x, spread64, pair_sum, row_sum, states, dy)


SB_Q_TILE = 1024
SB_K_TILE = 256


def _tri_sum(x, tri):
    t = x.shape[0]
    hi = x.astype(BF16)
    r1 = x - hi.astype(F32)
    mid = r1.astype(BF16)
    lo = (r1 - mid.astype(F32)).astype(BF16)
    r = _dot(jnp.concatenate([hi, mid, lo], axis=0), tri, _NN)
    return r[:t] + r[t:2 * t] + r[2 * t:]


def _sb_logits(q, k_j, scale, strict):
    z = _dot(q, k_j, _NT) * scale
    sp = jnp.log(1.0 + jnp.exp(-jnp.abs(z)))
    log_b = jnp.minimum(z, 0.0) - sp
    log_1mb = log_b - z
    if strict is not None:
        log_1mb = jnp.where(strict, log_1mb, 0.0)
    return log_b, log_1mb


def _sb_tiles(s):
    tq = _pick(s, (SB_Q_TILE, 2 * SB_K_TILE, SB_K_TILE, LANES))
    return tq, min(tq, SB_K_TILE)


def _sb_diag_mask(rows, tk):
    return lax.broadcasted_iota(jnp.int32, (rows, tk), 1) < lax.broadcasted_iota(jnp.int32, (rows, tk), 0)


def _sb_iotas(t):
    row = lax.broadcasted_iota(jnp.int32, (t, t), 0)
    col = lax.broadcasted_iota(jnp.int32, (t, t), 1)
    return row, col


def sb_attn_fwd(qn, kn, v, *, v_off=0, name):
    s, w = qn.shape
    dh = SB_HEAD_DIM
    n_h = w // dh
    tq, tk = _sb_tiles(s)
    per = tq // tk
    scale = 1.0 / math.sqrt(dh)

    def body(q_ref, k_ref, v_ref, o_ref, tot_ref):
        i = pl.program_id(1)
        q = q_ref[...]
        row, col = _sb_iotas(tk)
        later = (row > col).astype(BF16)

        def tile(q_rows, j, acc, run, mask):
            s0 = pl.multiple_of(j * tk, tk)
            k_j = k_ref[pl.ds(s0, tk), :]
            v_j = v_ref[pl.ds(s0, tk), :].astype(BF16)
            log_b, log_1mb = _sb_logits(q_rows, k_j, scale, mask)
            att = jnp.exp(log_b + (_tri_sum(log_1mb, later) + run))
            if mask is not None:
                att = jnp.where(mask, att, 0.0)
            return acc + _dot(att.astype(BF16), v_j, _NN), run + jnp.sum(log_1mb, axis=1, keepdims=True)

        acc, run = jnp.zeros((tq, dh), F32), jnp.zeros((tq, 1), F32)
        for d in reversed(range(per)):
            r0 = d * tk
            a2, r2 = tile(q[r0:], i * per + d, acc[r0:], run[r0:], _sb_diag_mask(tq - r0, tk))
            acc = a2 if r0 == 0 else jnp.concatenate([acc[:r0], a2], axis=0)
            run = r2 if r0 == 0 else jnp.concatenate([run[:r0], r2], axis=0)

        def group(gg, c):
            for d in reversed(range(per)):
                c = tile(q, (i - 1 - gg) * per + d, c[0], c[1], None)
            return c

        acc, run = lax.fori_loop(0, i, group, (acc, run))
        o_ref[...] = acc
        tot_ref[...] = jnp.broadcast_to(run, (tq, dh))

    return pl.pallas_call(
        body,
        out_shape=(jax.ShapeDtypeStruct((s, w), F32), jax.ShapeDtypeStruct((s, w), F32)),
        grid=(n_h, s // tq),
        in_specs=[pl.BlockSpec((tq, dh), lambda h, i: (i, h)),
                  pl.BlockSpec((s, dh), lambda h, i: (0, h)),
                  pl.BlockSpec((s, dh), lambda h, i: (0, v_off + h))],
        out_specs=(pl.BlockSpec((tq, dh), lambda h, i: (i, h)),
                   pl.BlockSpec((tq, dh), lambda h, i: (i, h))),
        name=name, compiler_params=_cparams(("parallel", "parallel")),
    )(qn, kn, v)


def sb_attn_bwd(qn, kn, v, tot, do, *, v_off=0, name):
    s, w = qn.shape
    dh = SB_HEAD_DIM
    n_h = w // dh
    tq, tk = _sb_tiles(s)
    per = tq // tk
    scale = 1.0 / math.sqrt(dh)

    def body(q_ref, k_ref, v_ref, tot_ref, do_ref, dq_ref, dk_ref, dv_ref):
        dk_ref[...] = jnp.zeros_like(dk_ref)
        dv_ref[...] = jnp.zeros_like(dv_ref)
        row, col = _sb_iotas(tk)
        upto = (row <= col).astype(BF16)
        before = (row < col).astype(BF16)

        def q_block(i, _):
            t0 = pl.multiple_of(i * tq, tq)
            q = q_ref[pl.ds(t0, tq), :]
            do_i = do_ref[pl.ds(t0, tq), :].astype(BF16)
            total = tot_ref[pl.ds(t0, tq), 0:1]

            def tile(r0, j, dq, run_l, run_g, mask):
                s0 = pl.multiple_of(j * tk, tk)
                k_j = k_ref[pl.ds(s0, tk), :]
                v_j = v_ref[pl.ds(s0, tk), :].astype(BF16)
                q_r, do_r = q[r0:], do_i[r0:]
                log_b, log_1mb = _sb_logits(q_r, k_j, scale, mask)
                att = jnp.exp(log_b + ((total[r0:] - run_l) - _tri_sum(log_1mb, upto)))
                if mask is not None:
                    att = jnp.where(mask, att, 0.0)
                g = att * _dot(do_r, v_j, _NT)
                c = _tri_sum(g, before) + run_g
                dz = (g - (g + c) * jnp.exp(log_b)) * scale
                if mask is not None:
                    dz = jnp.where(mask, dz, 0.0)
                dz = dz.astype(BF16)
                dk_ref[pl.ds(s0, tk), :] += _dot(dz, q_r, _TN)
                dv_ref[pl.ds(s0, tk), :] += _dot(att.astype(BF16), do_r, _TN)
                return (dq + _dot(dz, k_j, _NN), run_l + jnp.sum(log_1mb, axis=1, keepdims=True),
                        run_g + jnp.sum(g, axis=1, keepdims=True))

            def group(gg, c):
                for d in range(per):
                    c = tile(0, gg * per + d, c[0], c[1], c[2], None)
                return c

            zero = jnp.zeros((tq, 1), F32)
            dq, run_l, run_g = lax.fori_loop(0, i, group, (jnp.zeros((tq, dh), F32), zero, zero))
            for d in range(per):
                r0 = d * tk
                p_dq, p_l, p_g = tile(r0, i * per + d, dq[r0:], run_l[r0:], run_g[r0:], _sb_diag_mask(tq - r0, tk))
                if r0 == 0:
                    dq, run_l, run_g = p_dq, p_l, p_g
                else:
                    dq = jnp.concatenate([dq[:r0], p_dq], axis=0)
                    run_l = jnp.concatenate([run_l[:r0], p_l], axis=0)
                    run_g = jnp.concatenate([run_g[:r0], p_g], axis=0)
            dq_ref[pl.ds(t0, tq), :] = dq
            return 0

        lax.fori_loop(0, s // tq, q_block, 0)

    head = pl.BlockSpec((s, dh), lambda h: (0, h))
    return pl.pallas_call(
        body,
        out_shape=tuple(jax.ShapeDtypeStruct((s, w), F32) for _ in range(3)),
        grid=(n_h,),
        in_specs=[head, head, pl.BlockSpec((s, dh), lambda h: (0, v_off + h)), head, head],
        out_specs=(head, head, head),
        name=name, compiler_params=_cparams(("parallel",)),
    )(qn, kn, v, tot, do)


ROW_TILE = 256
WIDE_ROW_TILE = 64


def _rows(width, col=0, tm=ROW_TILE):
    return pl.BlockSpec((tm, width), lambda i: (i, col))


_wide_rows = functools.partial(_rows, tm=WIDE_ROW_TILE)


def _whole(shape):
    return pl.BlockSpec(shape, lambda i: (0,) * len(shape))


def _ew_call(body, out_shape, in_specs, out_specs, args, n_rows, name, carried=False):
    return pl.pallas_call(
        body, out_shape=out_shape, grid=(n_rows // in_specs[0].block_shape[0],), in_specs=in_specs, out_specs=out_specs,
        name=name, compiler_params=_cparams(("arbitrary",) if carried else ("parallel",)),
    )(*args)


def _first_step(*refs):
    @pl.when(pl.program_id(0) == 0)
    def _():
        for r in refs:
            r[...] = jnp.zeros_like(r)


def rmsnorm_fwd(x, w, after=None, *, name):
    s, d = x.shape

    def body(x_ref, w_ref, *rest):
        o_ref = rest[-1]
        xv = x_ref[...]
        r = lax.rsqrt(jnp.mean(xv * xv, axis=-1, keepdims=True) + NORM_EPS)
        o_ref[...] = (xv * r * w_ref[...]).astype(BF16)

    extra = [] if after is None else [after]
    return _ew_call(body, jax.ShapeDtypeStruct((s, d), BF16),
                    [_rows(d), _whole((1, d))] + [_whole(TOKEN_SHAPE)] * len(extra), _rows(d),
                    (x, w.reshape(1, d), *extra), s, name)


def rmsnorm_bwd(x, w, dy, dres, *, name):
    s, d = x.shape

    def body(x_ref, w_ref, dy_ref, dr_ref, dx_ref, dw_ref):
        _first_step(dw_ref)
        xv = x_ref[...]
        r = lax.rsqrt(jnp.mean(xv * xv, axis=-1, keepdims=True) + NORM_EPS)
        xhat = xv * r
        dyv = dy_ref[...].astype(F32)
        dw_ref[...] += jnp.sum(dyv * xhat, axis=0, keepdims=True)
        g = dyv * w_ref[...]
        dx_ref[...] = dr_ref[...] + r * (g - xhat * jnp.mean(g * xhat, axis=-1, keepdims=True))

    return _ew_call(body, (jax.ShapeDtypeStruct((s, d), F32), jax.ShapeDtypeStruct((1, d), F32)),
                    [_rows(d), _whole((1, d)), _rows(d), _rows(d)], (_rows(d), _whole((1, d))),
                    (x, w.reshape(1, d), dy, dres), s, name, carried=True)


def ple_fwd(h1, gate_pre, pp, *, name):
    s, d = h1.shape

    def body(h_ref, g_ref, p_ref, o_ref):
        o_ref[...] = h_ref[...] + p_ref[...] * _sigmoid(g_ref[...])

    return _ew_call(body, jax.ShapeDtypeStruct((s, d), F32), [_rows(d)] * 3, _rows(d), (h1, gate_pre, pp), s, name)


def ple_bwd(dh2, gate_pre, pp, after, *, name):
    s, d = dh2.shape

    def body(dh_ref, g_ref, p_ref, after_ref, dp_ref, dg_ref):
        gate = _sigmoid(g_ref[...])
        dh = dh_ref[...]
        dp_ref[...] = (dh * gate).astype(BF16)
        dg_ref[...] = (dh * p_ref[...] * gate * (1.0 - gate)).astype(BF16)

    shp = jax.ShapeDtypeStruct((s, d), BF16)
    return _ew_call(body, (shp, shp), [_rows(d)] * 3 + [_whole(TOKEN_SHAPE)], (_rows(d), _rows(d)),
                    (dh2, gate_pre, pp, after), s, name)


def loss_head(y, target, *, name):
    s, d = y.shape

    def body(y_ref, t_ref, l_ref, dy_ref):
        _first_step(l_ref)
        err = y_ref[...] - t_ref[...]
        per_tok = jnp.mean(err * err, axis=-1, keepdims=True)
        l_ref[...] += 0.5 * jnp.sum(per_tok, axis=0, keepdims=True)
        dy_ref[...] = err * (1.0 / d)

    return _ew_call(body, (jax.ShapeDtypeStruct((1, 1), F32), jax.ShapeDtypeStruct((s, d), F32)),
                    [_rows(d), _rows(d)], (_whole((1, 1)), _rows(d)), (y, target), s, name, carried=True)


CONV_COL_TILE = 256


def _conv_taps(x, w_ref):
    row = lax.broadcasted_iota(jnp.int32, (x.shape[0], 1), 0)
    acc = x * w_ref[SSD_D_CONV - 1:SSD_D_CONV, :]
    shifted = []
    for d in range(1, SSD_D_CONV):
        xs = jnp.where(row >= d, pltpu.roll(x, d, 0), 0.0)
        shifted.append(xs)
        acc = acc + xs * w_ref[SSD_D_CONV - 1 - d:SSD_D_CONV - d, :]
    return acc, shifted


def ssd_conv_fwd(x, w, b, *, name):
    s, c = x.shape
    tc = _pick(c, (CONV_COL_TILE, LANES))

    def body(x_ref, w_ref, b_ref, o_ref):
        pre, _ = _conv_taps(x_ref[...], w_ref)
        o_ref[...] = _silu(pre + b_ref[...])

    col = pl.BlockSpec((s, tc), lambda j: (0, j))
    return pl.pallas_call(
        body, out_shape=jax.ShapeDtypeStruct((s, c), F32), grid=(c // tc,),
        in_specs=[col, pl.BlockSpec((SSD_D_CONV, tc), lambda j: (0, j)), pl.BlockSpec((1, tc), lambda j: (0, j))],
        out_specs=col, name=name, compiler_params=_cparams(("parallel",)),
    )(x, w, b)


def ssd_conv_bwd(x, w, b, dact, *, name):
    s, c = x.shape
    tc = _pick(c, (CONV_COL_TILE, LANES))

    def body(x_ref, w_ref, b_ref, da_ref, dx_ref, dw_ref, db_ref):
        xv = x_ref[...]
        pre, shifted = _conv_taps(xv, w_ref)
        dpre = da_ref[...] * _silu_grad(pre + b_ref[...])
        db_ref[...] = jnp.sum(dpre, axis=0, keepdims=True)
        row = lax.broadcasted_iota(jnp.int32, (s, 1), 0)
        dx = dpre * w_ref[SSD_D_CONV - 1:SSD_D_CONV, :]
        dw_ref[SSD_D_CONV - 1:SSD_D_CONV, :] = jnp.sum(dpre * xv, axis=0, keepdims=True)
        for d in range(1, SSD_D_CONV):
            k = SSD_D_CONV - 1 - d
            dw_ref[k:k + 1, :] = jnp.sum(dpre * shifted[d - 1], axis=0, keepdims=True)
            up = jnp.where(row < s - d, pltpu.roll(dpre, s - d, 0), 0.0)
            dx = dx + up * w_ref[k:k + 1, :]
        dx_ref[...] = dx.astype(BF16)

    col = pl.BlockSpec((s, tc), lambda j: (0, j))
    wspec = pl.BlockSpec((SSD_D_CONV, tc), lambda j: (0, j))
    bspec = pl.BlockSpec((1, tc), lambda j: (0, j))
    return pl.pallas_call(
        body,
        out_shape=(jax.ShapeDtypeStruct((s, c), BF16), jax.ShapeDtypeStruct((SSD_D_CONV, c), F32),
                   jax.ShapeDtypeStruct((1, c), F32)),
        grid=(c // tc,), in_specs=[col, wspec, bspec, col], out_specs=(col, wspec, bspec),
        name=name, compiler_params=_cparams(("parallel",)),
    )(x, w, b, dact)


def ssd_dt_fwd(dt_raw, bias, a_log, *, name):
    s, h = dt_raw.shape

    def body(r_ref, b_ref, al_ref, dt_ref, a_ref):
        zv = r_ref[...] + b_ref[...]
        dt_ref[...] = jnp.maximum(zv, 0.0) + jnp.log(1.0 + jnp.exp(-jnp.abs(zv)))
        a_ref[...] = -jnp.exp(al_ref[...])

    full = pl.BlockSpec((s, h), lambda: (0, 0))
    vec = pl.BlockSpec((1, h), lambda: (0, 0))
    return pl.pallas_call(
        body, out_shape=(jax.ShapeDtypeStruct((s, h), F32), jax.ShapeDtypeStruct((1, h), F32)),
        in_specs=[full, vec, vec], out_specs=(full, vec), name=name, compiler_params=_cparams(),
    )(dt_raw, bias.reshape(1, h), a_log.reshape(1, h))


def ssd_dt_bwd(dt_raw, bias, a_log, dt, ddt, dadt, *, name):
    s, h = dt_raw.shape

    def body(r_ref, b_ref, al_ref, dt_ref, ddt_ref, dadt_ref, dr_ref, db_ref, dal_ref):
        a = -jnp.exp(al_ref[...])
        dadt_v = dadt_ref[...]
        d_dt = ddt_ref[...] + a * dadt_v
        d_raw = d_dt * _sigmoid(r_ref[...] + b_ref[...])
        dr_ref[...] = d_raw
        db_ref[...] = jnp.sum(d_raw, axis=0, keepdims=True)
        dal_ref[...] = jnp.sum(dadt_v * dt_ref[...], axis=0, keepdims=True) * a

    full = pl.BlockSpec((s, h), lambda: (0, 0))
    vec = pl.BlockSpec((1, h), lambda: (0, 0))
    return pl.pallas_call(
        body, out_shape=(jax.ShapeDtypeStruct((s, h), F32), jax.ShapeDtypeStruct((1, h), F32),
                         jax.ShapeDtypeStruct((1, h), F32)),
        in_specs=[full, vec, vec, full, full, full], out_specs=(full, vec, vec), name=name,
        compiler_params=_cparams(),
    )(dt_raw, bias.reshape(1, h), a_log.reshape(1, h), dt, ddt, dadt)


def _group_mean(v, n_groups):
    gw = v.shape[-1] // n_groups
    parts = [jnp.broadcast_to(jnp.mean(v[:, k * gw:(k + 1) * gw], axis=-1, keepdims=True), (v.shape[0], gw))
             for k in range(n_groups)]
    return jnp.concatenate(parts, axis=-1)


def ssd_gate_fwd(y, z, gw, *, name):
    s, di = y.shape

    def body(y_ref, z_ref, w_ref, o_ref):
        yg = y_ref[...] * _silu(z_ref[...])
        r = lax.rsqrt(_group_mean(yg * yg, SSD_N_GROUPS) + GATED_NORM_EPS)
        o_ref[...] = (yg * r * w_ref[...]).astype(BF16)

    return _ew_call(body, jax.ShapeDtypeStruct((s, di), BF16), [_wide_rows(di), _wide_rows(di), _whole((1, di))],
                    _wide_rows(di), (y, z, gw.reshape(1, di)), s, name)


def ssd_gate_bwd(y, z, gw, dyn, *, name):
    s, di = y.shape

    def body(y_ref, z_ref, w_ref, dn_ref, dy_ref, dz_ref, dw_ref):
        _first_step(dw_ref)
        yv, zv = y_ref[...], z_ref[...]
        sz = _silu(zv)
        yg = yv * sz
        r = lax.rsqrt(_group_mean(yg * yg, SSD_N_GROUPS) + GATED_NORM_EPS)
        yhat = yg * r
        dn = dn_ref[...]
        dw_ref[...] += jnp.sum(dn * yhat, axis=0, keepdims=True)
        g = dn * w_ref[...]
        dyg = r * (g - yhat * _group_mean(g * yhat, SSD_N_GROUPS))
        dy_ref[...] = dyg * sz
        dz_ref[...] = (dyg * yv * _silu_grad(zv)).astype(BF16)

    return _ew_call(body, (jax.ShapeDtypeStruct((s, di), F32), jax.ShapeDtypeStruct((s, di), BF16),
                           jax.ShapeDtypeStruct((1, di), F32)),
                    [_wide_rows(di), _wide_rows(di), _whole((1, di)), _wide_rows(di)],
                    (_wide_rows(di), _wide_rows(di), _whole((1, di))),
                    (y, z, gw.reshape(1, di), dyn), s, name, carried=True)


def _head_mean(v):
    return _group_mean(v, v.shape[-1] // SB_HEAD_DIM)


def sb_qk_fwd(proj, qw, kw, *, name):
    s, w4 = proj.shape
    w = w4 // 4
    reps = w // SB_HEAD_DIM

    def body(q_ref, k_ref, qw_ref, kw_ref, qn_ref, kn_ref):
        for x_ref, w_ref, o_ref in ((q_ref, qw_ref, qn_ref), (k_ref, kw_ref, kn_ref)):
            xv = x_ref[...]
            r = lax.rsqrt(_head_mean(xv * xv) + NORM_EPS)
            o_ref[...] = (xv * r * jnp.tile(w_ref[...], (1, reps))).astype(BF16)

    shp = jax.ShapeDtypeStruct((s, w), BF16)
    return _ew_call(body, (shp, shp), [_rows(w, 0), _rows(w, 1), _whole((1, SB_HEAD_DIM)), _whole((1, SB_HEAD_DIM))],
                    (_rows(w), _rows(w)), (proj, proj, qw.reshape(1, -1), kw.reshape(1, -1)), s, name)


def sb_gate_fwd(o, proj, *, name):
    s, w = o.shape

    def body(o_ref, g_ref, og_ref):
        og_ref[...] = (o_ref[...] * _silu(g_ref[...])).astype(BF16)

    return _ew_call(body, jax.ShapeDtypeStruct((s, w), BF16), [_rows(w), _rows(w, 3)], _rows(w), (o, proj), s, name)


def sb_gate_bwd(dog, o, proj, *, name):
    s, w = o.shape

    def body(d_ref, o_ref, g_ref, do_ref, dg_ref):
        gv, dv = g_ref[...], d_ref[...]
        do_ref[...] = dv * _silu(gv)
        dg_ref[...] = (dv * o_ref[...] * _silu_grad(gv)).astype(BF16)

    return _ew_call(body, (jax.ShapeDtypeStruct((s, w), F32), jax.ShapeDtypeStruct((s, w), BF16)),
                    [_rows(w), _rows(w), _rows(w, 3)], (_rows(w), _rows(w)), (dog, o, proj), s, name)


def sb_pack_bwd(proj, qw, kw, dqn, dkn, dv, dg, *, name):
    s, w4 = proj.shape
    w = w4 // 4
    reps = w // SB_HEAD_DIM

    def body(q_ref, k_ref, qw_ref, kw_ref, dqn_ref, dkn_ref, dv_ref, dg_ref, dp_ref, dqw_ref, dkw_ref):
        _first_step(dqw_ref, dkw_ref)
        for idx, (x_ref, w_ref, d_ref, dw_ref) in enumerate(((q_ref, qw_ref, dqn_ref, dqw_ref),
                                                           (k_ref, kw_ref, dkn_ref, dkw_ref))):
            xv = x_ref[...]
            r = lax.rsqrt(_head_mean(xv * xv) + NORM_EPS)
            xhat = xv * r
            dn = d_ref[...]
            per_col = jnp.sum(dn * xhat, axis=0, keepdims=True)
            acc = per_col[:, 0:SB_HEAD_DIM]
            for hh in range(1, reps):
                acc = acc + per_col[:, hh * SB_HEAD_DIM:(hh + 1) * SB_HEAD_DIM]
            dw_ref[...] += acc
            g = dn * jnp.tile(w_ref[...], (1, reps))
            dp_ref[:, idx * w:(idx + 1) * w] = (r * (g - xhat * _head_mean(g * xhat))).astype(BF16)
        dp_ref[:, 2 * w:3 * w] = dv_ref[...].astype(BF16)
        dp_ref[:, 3 * w:4 * w] = dg_ref[...]

    vec = _whole((1, SB_HEAD_DIM))
    return _ew_call(body, (jax.ShapeDtypeStruct((s, w4), BF16), jax.ShapeDtypeStruct((1, SB_HEAD_DIM), F32),
                           jax.ShapeDtypeStruct((1, SB_HEAD_DIM), F32)),
                    [_wide_rows(w, 0), _wide_rows(w, 1), vec, vec, _wide_rows(w), _wide_rows(w), _wide_rows(w),
                     _wide_rows(w)],
                    (_wide_rows(w4), vec, vec),
                    (proj, proj, qw.reshape(1, -1), kw.reshape(1, -1), dqn, dkn, dv, dg), s, name, carried=True)


_HBM = pl.BlockSpec(memory_space=pltpu.HBM)


def _mesh_pos():
    return lax.axis_index("x"), lax.axis_index("y"), lax.axis_index("c")


def _other_chips(x, y):
    return [(1 - x, y), (x, 1 - y), (1 - x, 1 - y)]


_SEM = pl.BlockSpec(memory_space=pltpu.SEMAPHORE)
_ANY = pl.BlockSpec(memory_space=pl.ANY)
_DATAFLOW = pltpu.SideEffectType.DATAFLOW_SIDE_EFFECTING
N_PEER_CHIPS = N_CHIP - 1
TOKEN_SHAPE = (8, LANES)


def _in_hbm(t):
    return pltpu.with_memory_space_constraint(t, pltpu.HBM)


def _ici_copies(kind, src_refs, land_refs, send_sems, recv_sems, arrivals=False):
    x, y, c = _mesh_pos()
    out = []
    for a in range(len(land_refs)):
        if kind in ("pass", "swap"):
            if kind == "pass":
                src, dst = land_refs[a].at[:, c], land_refs[a].at[:, 1 - c if arrivals else c]
            else:
                src, dst = src_refs[a].at[:, 1 - c], land_refs[a]
            out.append(pltpu.make_async_remote_copy(
                src_ref=src, dst_ref=dst, send_sem=send_sems.at[a], recv_sem=recv_sems.at[a],
                device_id=(x, y, 1 - c), device_id_type=MESH))
            continue
        for j, chip in enumerate(_other_chips(x, y)):
            if kind == "gather":
                src = land_refs[a].at[4 * x + 2 * y + c]
                dst = land_refs[a].at[4 * chip[0] + 2 * chip[1] + c] if arrivals else src
            else:
                src, dst = src_refs[a].at[2 * chip[0] + chip[1]], land_refs[a].at[j]
            k = a * N_PEER_CHIPS + j
            out.append(pltpu.make_async_remote_copy(
                src_ref=src, dst_ref=dst, send_sem=send_sems.at[k], recv_sem=recv_sems.at[k],
                device_id=(*chip, c), device_id_type=MESH))
    return out


def _n_copies(kind, lands):
    return len(lands) * (1 if kind in ("pass", "swap") else N_PEER_CHIPS)


def ici_start(kind, srcs, lands, after=(), *, name):
    ns, nb = len(srcs), len(srcs) + len(lands)
    n_sem = _n_copies(kind, lands)

    def body(*refs):
        first_out = nb + len(after)
        for cp in _ici_copies(kind, refs[:ns], refs[ns:nb], refs[first_out], refs[first_out + 1]):
            cp.start()
        refs[-1][...] = jnp.zeros(TOKEN_SHAPE, F32)

    outs = pl.pallas_call(
        body, name=name,
        out_shape=(pltpu.SemaphoreType.DMA((n_sem,)), pltpu.SemaphoreType.DMA((n_sem,)),
                   *[pltpu.HBM(t.shape, t.dtype) for t in (*srcs, *lands)], jax.ShapeDtypeStruct(TOKEN_SHAPE, F32)),
        in_specs=[_HBM] * nb + [_ANY] * len(after),
        out_specs=(_SEM, _SEM, *([_HBM] * nb), pl.BlockSpec(memory_space=pltpu.VMEM)),
        input_output_aliases={k: 2 + k for k in range(nb)},
        compiler_params=pltpu.CompilerParams(has_side_effects=_DATAFLOW),
    )(*[_in_hbm(t) for t in (*srcs, *lands)], *after)
    return outs[0], outs[1], list(outs[2:2 + ns]), list(outs[2 + ns:2 + nb]), outs[-1]


def ici_wait(kind, started, after, *, name):
    send_sems, recv_sems, srcs, lands, _ = started
    ns, nb = len(srcs), len(srcs) + len(lands)

    def body(*refs):
        for cp in _ici_copies(kind, refs[:ns], refs[ns:nb], refs[nb], refs[nb + 1]):
            cp.wait_send()
        for cp in _ici_copies(kind, refs[:ns], refs[ns:nb], refs[nb], refs[nb + 1], arrivals=True):
            cp.wait_recv()

    outs = pl.pallas_call(
        body, name=name,
        out_shape=tuple(pltpu.HBM(t.shape, t.dtype) for t in (*srcs, *lands)),
        in_specs=[_HBM] * nb + [_SEM, _SEM] + [_ANY] * len(after),
        out_specs=tuple([_HBM] * nb),
        input_output_aliases={k: k for k in range(nb)},
        compiler_params=pltpu.CompilerParams(has_side_effects=_DATAFLOW),
    )(*srcs, *lands, send_sems, recv_sems, *after)
    return list(outs[:ns]), list(outs[ns:])


def all_reduce_small(v, *, name):
    r = v.shape[0]

    def body(v_ref, o_ref, buf, send_sems, recv_sems):
        x, y, c = _mesh_pos()
        me = 4 * x + 2 * y + c
        buf[me] = v_ref[...]
        copies = []
        for k in range(1, N_DEV):
            to = ((x + (k >> 2)) % 2, (y + ((k >> 1) & 1)) % 2, (c + (k & 1)) % 2)
            copies.append(pltpu.make_async_remote_copy(
                src_ref=v_ref, dst_ref=buf.at[me], send_sem=send_sems.at[k - 1], recv_sem=recv_sems.at[k - 1],
                device_id=to, device_id_type=MESH))
        for cp in copies:
            cp.start()
        for cp in copies:
            cp.wait()
        acc = buf[0]
        for d in range(1, N_DEV):
            acc = acc + buf[d]
        o_ref[...] = acc

    vm = pl.BlockSpec(memory_space=pltpu.VMEM)
    return pl.pallas_call(
        body, out_shape=jax.ShapeDtypeStruct(v.shape, F32), in_specs=[vm], out_specs=vm,
        scratch_shapes=[pltpu.VMEM((N_DEV, r, LANES), F32), pltpu.SemaphoreType.DMA((N_DEV - 1,)),
                        pltpu.SemaphoreType.DMA((N_DEV - 1,))],
        name=name,
    )(v)


def pair_add(g, r1, core, *, name):
    _, _, rows, cols = g.shape
    tm = _pick(rows, (256, 128))

    def body(c_ref, g_ref, r_ref, o_ref):
        o_ref[...] = (g_ref[...].astype(F32) + r_ref[...].astype(F32)).astype(o_ref.dtype)

    return pl.pallas_call(
        body, out_shape=jax.ShapeDtypeStruct(r1.shape, g.dtype),
        grid_spec=pltpu.PrefetchScalarGridSpec(
            num_scalar_prefetch=1, grid=(N_CHIP, rows // tm),
            in_specs=[pl.BlockSpec((None, None, tm, cols), lambda k, i, c_ref: (k, c_ref[0], i, 0)),
                      pl.BlockSpec((None, tm, cols), lambda k, i, c_ref: (k, i, 0))],
            out_specs=pl.BlockSpec((None, tm, cols), lambda k, i, c_ref: (k, i, 0))),
        name=name, compiler_params=_cparams(("parallel", "parallel")),
    )(core, g, r1)


def _adamw_math(w, g, m, v):
    m = ADAM_B1 * m + (1.0 - ADAM_B1) * g
    v = ADAM_B2 * v + (1.0 - ADAM_B2) * (g * g)
    m_hat = m / (1.0 - ADAM_B1 ** ADAM_STEP)
    v_hat = v / (1.0 - ADAM_B2 ** ADAM_STEP)
    delta = -ADAM_LR * (m_hat / (jnp.sqrt(v_hat) + ADAM_EPS) + ADAM_WD * w)
    return delta, m, v


def adamw_sharded(w, m, v, layer, chip_sums, received, chip, into, *, name):
    _, rows, cols = w.shape
    tm = _pick(rows, (256, 128))

    def body(k_ref, w_ref, m_ref, v_ref, t_ref, r_ref, *rest):
        g_ref, d_ref, nm_ref, nv_ref, token_ref = rest[-5:]
        g = t_ref[...].astype(F32)
        for j in range(N_CHIP - 1):
            g = g + r_ref[j].astype(F32)
        d, mm, vv = _adamw_math(w_ref[...], g, m_ref[...], v_ref[...])
        g_ref[...] = g
        d_ref[...] = d
        nm_ref[...] = mm
        nv_ref[...] = vv
        token_ref[...] = jnp.zeros(TOKEN_SHAPE, F32)

    blk = pl.BlockSpec((None, tm, cols), lambda i, k_ref: (layer, i, 0))
    shp = jax.ShapeDtypeStruct(w.shape, F32)
    in_specs = [blk, blk, blk,
                pl.BlockSpec((None, tm, cols), lambda i, k_ref: (k_ref[0], i, 0)),
                pl.BlockSpec((N_CHIP - 1, tm, cols), lambda i, k_ref: (0, i, 0))]
    operands = [chip, w, m, v, chip_sums, received]
    aliases = {}
    if into is not None:
        aliases = {len(operands) + q: q for q in range(4)}
        in_specs += [_ANY] * 4
        operands += list(into)
    outs = pl.pallas_call(
        body, out_shape=(shp, shp, shp, shp, jax.ShapeDtypeStruct(TOKEN_SHAPE, F32)),
        grid_spec=pltpu.PrefetchScalarGridSpec(
            num_scalar_prefetch=1, grid=(rows // tm,), in_specs=in_specs,
            out_specs=(blk, blk, blk, blk, pl.BlockSpec(TOKEN_SHAPE, lambda i, k_ref: (0, 0)))),
        input_output_aliases=aliases,
        name=name, compiler_params=_cparams(("arbitrary",)),
    )(*operands)
    return outs[:4], outs[4]


def adamw_replicated(w, m, v, g, *, name):
    def body(w_ref, m_ref, v_ref, g_ref, d_ref, nm_ref, nv_ref):
        d, mm, vv = _adamw_math(w_ref[...], g_ref[...], m_ref[...], v_ref[...])
        d_ref[...] = d
        nm_ref[...] = mm
        nv_ref[...] = vv

    shp = jax.ShapeDtypeStruct(w.shape, F32)
    return pl.pallas_call(body, out_shape=(shp, shp, shp), name=name, compiler_params=_cparams())(w, m, v, g)


WEIGHT_NAMES = ("norm_w", "ssd_in_w", "ssd_conv_w", "ssd_conv_b", "ssd_dt_bias", "ssd_a_log", "ssd_d",
                "ssd_gnorm_w", "ssd_out_w", "sb_in_w", "sb_qn_w", "sb_kn_w", "sb_out_w", "ple_norm_w",
                "ple_gate_w", "ple_proj_w")
REPLICATED = ("norm_w", "ssd_conv_b", "ssd_dt_bias", "ssd_a_log", "ssd_d", "ssd_gnorm_w", "sb_qn_w", "sb_kn_w",
              "ple_norm_w")
PACK_ROWS = 8


def _pack(parts):
    flat = jnp.concatenate([t.reshape(-1) for t in parts])
    pad = (-flat.shape[0]) % (PACK_ROWS * LANES)
    return jnp.pad(flat, (0, pad)).reshape(-1, LANES)


def _unpack(packed, like):
    flat = packed.reshape(-1)
    out, off = [], 0
    for t in like:
        out.append(flat[off:off + t.size].reshape(t.shape))
        off += t.size
    return out


def _to_group_lanes(v, r):
    t = v.reshape(v.shape[0], SSD_N_GROUPS, r).transpose(1, 0, 2)
    return jnp.pad(t, ((0, 0), (0, 0), (0, LANES - r)))


def _from_group_lanes(t, r):
    return t[:, :, :r].transpose(1, 0, 2).reshape(t.shape[1], SSD_N_GROUPS * r)


def _head_vec(v, r):
    return jnp.pad(v.reshape(SSD_N_GROUPS, 1, r), ((0, 0), (0, 0), (0, LANES - r)))


def _col_blocks(full):
    rows = full.shape[0]
    return full.reshape(rows, N_DEV, -1).transpose(1, 0, 2)


def _from_col_blocks(blocks):
    return blocks.transpose(1, 0, 2).reshape(blocks.shape[1], -1)


def _split_cols(full, widths):
    out, off = [], 0
    for w in widths:
        out.append(full[:, off:off + w])
        off += w
    return out


def kernel(x, p, norm_w, ssd_in_w, ssd_conv_w, ssd_conv_b, ssd_dt_bias, ssd_a_log, ssd_d, ssd_gnorm_w, ssd_out_w, sb_in_w, sb_qn_w, sb_kn_w, sb_out_w, ple_norm_w, ple_gate_w, ple_proj_w, loss_target, m_norm_w, m_ssd_in_w, m_ssd_conv_w, m_ssd_conv_b, m_ssd_dt_bias, m_ssd_a_log, m_ssd_d, m_ssd_gnorm_w, m_ssd_out_w, m_sb_in_w, m_sb_qn_w, m_sb_kn_w, m_sb_out_w, m_ple_norm_w, m_ple_gate_w, m_ple_proj_w, v_norm_w, v_ssd_in_w, v_ssd_conv_w, v_ssd_conv_b, v_ssd_dt_bias, v_ssd_a_log, v_ssd_d, v_ssd_gnorm_w, v_ssd_out_w, v_sb_in_w, v_sb_qn_w, v_sb_kn_w, v_sb_out_w, v_ple_norm_w, v_ple_gate_w, v_ple_proj_w):
    env = dict(locals())
    wts = {n: env[n] for n in WEIGHT_NAMES}
    mom1 = {n: env["m_" + n] for n in WEIGHT_NAMES}
    mom2 = {n: env["v_" + n] for n in WEIGHT_NAMES}

    s, d = x.shape[1], x.shape[2]
    depth = norm_w.shape[0]
    n_ssd, n_sb = ssd_in_w.shape[0], sb_in_w.shape[0]
    di = ssd_out_w.shape[1] * N_DEV
    n_heads = ssd_dt_bias.shape[1]
    hpg = n_heads // SSD_N_GROUPS
    nbc = SSD_N_GROUPS * SSD_D_STATE
    in_segs = (di, di, nbc, nbc, n_heads)
    conv_segs = (di, nbc, nbc)
    sb_w = sb_out_w.shape[1] * N_DEV
    selectors = ssd_selectors(hpg)
    xi, yi, ci = _mesh_pos()
    core = ci.astype(jnp.int32).reshape(1)
    chip = (2 * xi + yi).astype(jnp.int32).reshape(1)

    def layer_keys(i):
        j = i // 2
        mixer = [("ssd_in_w", j), ("ssd_conv_w", j), ("ssd_out_w", j)] if i % 2 == 0 else [("sb_in_w", j), ("sb_out_w", j)]
        return mixer + [("ple_gate_w", i), ("ple_proj_w", i)]

    me_block = 4 * xi + 2 * yi + ci

    def landing_zone(t):
        return lax.dynamic_update_index_in_dim(lax.empty((N_DEV,) + t.shape, t.dtype), t, me_block, 0)

    def groups(i):
        keys = layer_keys(i)
        return [keys[:2], keys[2:]] if i == 0 else [keys]

    gathers, prev = {}, []
    for i in range(depth):
        for q, keys in enumerate(groups(i)):
            shards = [wts[n][idx] for n, idx in keys]
            if prev:
                shards = lax.optimization_barrier((prev[0], shards))[1]
            lands = [landing_zone(t if n == "ssd_conv_w" else t.astype(BF16)) for (n, _), t in zip(keys, shards)]
            gathers[i, q] = ici_start("gather", [], lands, after=prev, name=f"ag{i}{'ab'[q]}_start")
            prev = [gathers[i, q][4]]
    all_started = prev[0]
    full, ssd_full, passing = {}, {}, {}

    def hand_over(i, q, after):
        _, lands = ici_wait("gather", gathers[i, q], after, name=f"ag{i}{'ab'[q]}_wait")
        passing[i, q] = ici_start("pass", [], [t.reshape(N_CHIP, 2, *t.shape[1:]) for t in lands],
                                  name=f"ag{i}{'ab'[q]}_pass_start")

    def arrive(i, q, after):
        _, lands = ici_wait("pass", passing[i, q], after, name=f"ag{i}{'ab'[q]}_pass_wait")
        for k, t in zip(groups(i)[q], lands):
            full[k] = t.reshape(N_DEV, *t.shape[2:])

    def w_out_of(i):
        return full["ssd_out_w", i // 2].reshape(di, d) if i % 2 == 0 else full["sb_out_w", i // 2].reshape(sb_w, d)

    h = x.reshape(s, d)
    saved = []
    hand_over(0, 0, [all_started])
    arrive(0, 0, [all_started])
    for i in range(depth):
        j = i // 2
        if i > 0:
            arrive(i, 0, [h])
        sv = dict(h_in=h)
        u = rmsnorm_fwd(h, norm_w[i], name=f"l{i}_norm")
        sv["u"] = u
        if i % 2 == 0:
            fw = ssd_full[j] = dict(
                w_in=_split_cols(_from_col_blocks(full["ssd_in_w", j]), in_segs),
                conv_w=_split_cols(_from_col_blocks(full["ssd_conv_w", j]), conv_segs),
                conv_b=_split_cols(ssd_conv_b[j].reshape(1, -1), conv_segs))
            raw = [matmul(u, wseg, name=f"l{i}_in{q}") for q, wseg in enumerate(fw["w_in"])]
            if i == 0:
                hand_over(0, 1, [raw[4]])
            z, dt_raw = raw[0], raw[4]
            act = [ssd_conv_fwd(raw[1 + q], fw["conv_w"][q], fw["conv_b"][q], name=f"l{i}_conv{q}") for q in range(3)]
            dt, a_neg = ssd_dt_fwd(dt_raw, ssd_dt_bias[j], ssd_a_log[j], name=f"l{i}_dt")
            dtp = _to_group_lanes(dt, hpg)
            a_g = _head_vec(a_neg.reshape(-1), hpg)
            d_x = jnp.repeat(ssd_d[j].reshape(SSD_N_GROUPS, 1, hpg), SSD_HEAD_DIM, axis=2)
            y, states = ssd_scan_fwd(act[0], act[1], act[2], dtp, a_g, d_x, selectors, heads_per_group=hpg,
                                     name=f"l{i}_scan")
            yn = ssd_gate_fwd(y, z, ssd_gnorm_w[j], name=f"l{i}_gate")
            if i == 0:
                arrive(0, 1, [yn])
            h1 = matmul(yn, w_out_of(i), res=h, name=f"l{i}_out")
            sv.update(raw=raw, act=act, dt=dt, dtp=dtp, a_g=a_g, d_x=d_x, y=y, states=states, yn=yn)
        else:
            proj = matmul(u, full["sb_in_w", j], name=f"l{i}_in")
            qn, kn = sb_qk_fwd(proj, sb_qn_w[j], sb_kn_w[j], name=f"l{i}_qknorm")
            v_off = 2 * sb_w // SB_HEAD_DIM
            o, tot = sb_attn_fwd(qn, kn, proj, v_off=v_off, name=f"l{i}_attn")
            og = sb_gate_fwd(o, proj, name=f"l{i}_gate")
            h1 = matmul(og, w_out_of(i), res=h, name=f"l{i}_out")
            sv.update(proj=proj, qn=qn, kn=kn, o=o, tot=tot, og=og, v_off=v_off)
        if i + 1 < depth:
            hand_over(i + 1, 0, [h1])
        t = rmsnorm_fwd(h1, ple_norm_w[i], passing[i + 1, 0][4] if i + 1 < depth else None, name=f"l{i}_plenorm")
        gate_pre = matmul(t, full["ple_gate_w", i].reshape(d, d), name=f"l{i}_plegate")
        pp = matmul(p[i, 0], full["ple_proj_w", i], name=f"l{i}_pleproj")
        h = ple_fwd(h1, gate_pre, pp, name=f"l{i}_ple")
        sv.update(h1=h1, t=t, gate_pre=gate_pre, pp=pp)
        saved.append(sv)

    loss_part, dh = loss_head(h, loss_target.reshape(s, d), name="loss_head")
    loss = lax.psum(loss_part[0, 0], ("x", "y", "c"))

    big = {}
    small = {n: [None] * wts[n].shape[0] for n in REPLICATED}
    swaps, scatters = {}, {}
    order_after = jnp.zeros(TOKEN_SHAPE, F32)
    pending = None

    def send_to_sibling(i, q):
        blocks = [big[k].reshape(N_CHIP, 2, *big[k].shape[1:]) for k in groups(i)[::-1][q]]
        swaps[i, q] = ici_start("swap", blocks, [lax.empty((N_CHIP,) + t.shape[2:], t.dtype) for t in blocks],
                                name=f"rs{i}{'ab'[q]}_swap_start")
        return swaps[i, q][4]

    def send_to_chips(i, q, after):
        blocks, from_sibling = ici_wait("swap", swaps[i, q], after, name=f"rs{i}{'ab'[q]}_swap_wait")
        sums = [pair_add(g, r1, core, name=f"rs{i}{'ab'[q]}_pair_add{a}")
                for a, (g, r1) in enumerate(zip(blocks, from_sibling))]
        scatters[i, q] = ici_start("scatter", sums, [lax.empty((N_PEER_CHIPS,) + t.shape[1:], t.dtype) for t in sums],
                                   name=f"rs{i}{'ab'[q]}_start")
        return scatters[i, q][4]

    for i in reversed(range(depth)):
        j = i // 2
        sv = saved[i]
        dpp, dgp = ple_bwd(dh, sv["gate_pre"], sv["pp"], order_after, name=f"b{i}_ple")
        big["ple_proj_w", i] = matmul(p[i, 0], dpp, mode="tn", out_dtype=BF16, out_blocks=ple_proj_w.shape[2],
                                      name=f"b{i}_pleproj_w")
        big["ple_gate_w", i] = matmul(sv["t"], dgp, mode="tn", out_dtype=BF16, name=f"b{i}_plegate_w").reshape(N_DEV, -1, d)
        dt_ = matmul(dgp, full["ple_gate_w", i].reshape(d, d), mode="nt", name=f"b{i}_plegate_x")
        dh1, g_pn = rmsnorm_bwd(sv["h1"], ple_norm_w[i], dt_, dh, name=f"b{i}_plenorm")
        small["ple_norm_w"][i] = g_pn
        behind = send_to_chips(*pending, [dh1]) if pending is not None else None
        pending = None
        u = sv["u"]
        if i % 2 == 0:
            fw = ssd_full[j]
            raw, act = sv["raw"], sv["act"]
            big["ssd_out_w", j] = matmul(sv["yn"], dh1, mode="tn", out_dtype=BF16, name=f"b{i}_out_w").reshape(N_DEV, -1, d)
            if i == 0:
                send_to_sibling(0, 0)
            dyn = matmul(dh1, w_out_of(i), mode="nt", after=behind, name=f"b{i}_out_x")
            dy, dz, g_gn = ssd_gate_bwd(sv["y"], raw[0], ssd_gnorm_w[j], dyn, name=f"b{i}_gate")
            dxs, dbm, dcm, ddtp, dadtp, dd_g = ssd_scan_bwd(act[0], act[1], act[2], sv["dtp"], sv["a_g"], sv["d_x"], selectors,
                                                          sv["states"], dy, heads_per_group=hpg, name=f"b{i}_scan")
            behind = send_to_chips(0, 0, [dxs]) if i == 0 else None
            ddt_raw, g_dtb, g_alog = ssd_dt_bwd(raw[4], ssd_dt_bias[j], ssd_a_log[j], sv["dt"],
                                                _from_group_lanes(ddtp, hpg), _from_group_lanes(dadtp, hpg),
                                                name=f"b{i}_dt")
            conv_back = [ssd_conv_bwd(raw[1 + q], fw["conv_w"][q], fw["conv_b"][q], dact, name=f"b{i}_conv{q}")
                         for q, dact in enumerate((dxs, dbm, dcm))]
            dsegs = [dz] + [cb[0] for cb in conv_back] + [ddt_raw]
            g_in = jnp.concatenate([matmul(u, ds, mode="tn", out_dtype=BF16, name=f"b{i}_in{q}_w")
                                    for q, ds in enumerate(dsegs)], axis=1)
            big["ssd_in_w", j] = _col_blocks(g_in)
            big["ssd_conv_w", j] = _col_blocks(jnp.concatenate([cb[1] for cb in conv_back], axis=1))
            du = None
            for q, (ds, wseg) in enumerate(zip(dsegs, fw["w_in"])):
                du = matmul(ds, wseg, mode="nt", res=du, after=behind if q == 0 else None, name=f"b{i}_in{q}_x")
            small["ssd_conv_b"][j] = jnp.concatenate([cb[2] for cb in conv_back], axis=1)
            small["ssd_dt_bias"][j] = g_dtb
            small["ssd_a_log"][j] = g_alog
            small["ssd_d"][j] = dd_g[:, 0, :hpg]
            small["ssd_gnorm_w"][j] = g_gn
        else:
            proj = sv["proj"]
            big["sb_out_w", j] = matmul(sv["og"], dh1, mode="tn", out_dtype=BF16, name=f"b{i}_out_w").reshape(N_DEV, -1, d)
            dog = matmul(dh1, w_out_of(i), mode="nt", after=behind, name=f"b{i}_out_x")
            do, dg = sb_gate_bwd(dog, sv["o"], proj, name=f"b{i}_gate")
            dqn, dkn, dv = sb_attn_bwd(sv["qn"], sv["kn"], proj, sv["tot"], do, v_off=sv["v_off"], name=f"b{i}_attn")
            dproj, g_qn, g_kn = sb_pack_bwd(proj, sb_qn_w[j], sb_kn_w[j], dqn, dkn, dv, dg, name=f"b{i}_qknorm")
            big["sb_in_w", j] = matmul(u, dproj, mode="tn", out_dtype=BF16, out_blocks=sb_in_w.shape[2], name=f"b{i}_in_w")
            du = matmul(dproj, full["sb_in_w", j], mode="nt", name=f"b{i}_in_x")
            small["sb_qn_w"][j] = g_qn
            small["sb_kn_w"][j] = g_kn
        dh, g_n = rmsnorm_bwd(sv["h_in"], norm_w[i], du, dh1, name=f"b{i}_norm")
        small["norm_w"][i] = g_n
        pending = (i, len(groups(i)) - 1)
        order_after = send_to_sibling(*pending)
    send_to_chips(*pending, [dh])
    grad_x = dh.reshape(x.shape)

    rep_like = [wts[n] for n in REPLICATED]
    g_packed = all_reduce_small(_pack([jnp.stack([t.reshape(-1) for t in small[n]]) for n in REPLICATED]),
                                name="all_reduce_small_grads")
    d_packed, m_packed, v_packed = adamw_replicated(
        _pack(rep_like), _pack([mom1[n] for n in REPLICATED]), _pack([mom2[n] for n in REPLICATED]), g_packed,
        name="adamw_replicated")
    grads = dict(zip(REPLICATED, _unpack(g_packed, rep_like)))
    deltas = dict(zip(REPLICATED, _unpack(d_packed, rep_like)))
    new_m = dict(zip(REPLICATED, _unpack(m_packed, rep_like)))
    new_v = dict(zip(REPLICATED, _unpack(v_packed, rep_like)))

    updated = {}
    after = [scatters[0, len(groups(0)) - 1][4]]
    for i in reversed(range(depth)):
        for q, keys in enumerate(groups(i)[::-1]):
            sums, received = ici_wait("scatter", scatters[i, q], after, name=f"rs{i}{'ab'[q]}_wait")
            after = []
            for (n, idx), t_sum, recv in zip(keys, sums, received):
                updated[n], done = adamw_sharded(wts[n], mom1[n], mom2[n], idx, t_sum, recv, chip, updated.get(n),
                                                 name=f"adamw_{n}{idx}")
                after.append(done)
    for n, (g_n, d_n, m_n, v_n) in updated.items():
        grads[n], deltas[n], new_m[n], new_v[n] = g_n, d_n, m_n, v_n

    return (loss, grad_x, *[grads[n] for n in WEIGHT_NAMES], *[deltas[n] for n in WEIGHT_NAMES],
            *[new_m[n] for n in WEIGHT_NAMES], *[new_v[n] for n in WEIGHT_NAMES])
```

```python
import functools
import math

import jax
import jax.numpy as jnp
from jax import lax
from jax.experimental import pallas as pl
from jax.experimental.pallas import tpu as pltpu

F32 = jnp.float32
BF16 = jnp.bfloat16
MESH = pl.DeviceIdType.MESH

N_DEV = 8
N_CHIP = 4
LANES = 128
VMEM_LIMIT_BYTES = 56 * 1024 * 1024
MATMUL_TILE_BYTES = 36 * 1024 * 1024

NORM_EPS = 1e-6
GATED_NORM_EPS = 1e-5
SSD_HEAD_DIM = 64
SSD_N_GROUPS = 8
SSD_D_STATE = 128
SSD_D_CONV = 4
SSD_CHUNK = 128
SB_HEAD_DIM = 128
PLE_DIM = 256

ADAM_LR = 0.001
ADAM_B1 = 0.9
ADAM_B2 = 0.999
ADAM_EPS = 1e-08
ADAM_WD = 0.01
ADAM_STEP = 10


def _cparams(sem=None, **kw):
    return pltpu.CompilerParams(dimension_semantics=sem, vmem_limit_bytes=VMEM_LIMIT_BYTES, **kw)


def _pick(dim, prefs):
    for t in prefs:
        if dim % t == 0:
            return t
    return dim


def _sigmoid(x):
    return 1.0 / (1.0 + jnp.exp(-x))


def _silu(x):
    return x * _sigmoid(x)


def _silu_grad(x):
    s = _sigmoid(x)
    return s * (1.0 + x * (1.0 - s))


def matmul(a, b, *, mode="nn", out_dtype=F32, res=None, out_blocks=None, after=None, name):
    b_blocked = b.ndim == 3
    if mode == "nn":
        m, kc = a.shape
        n = b.shape[-1] * (N_DEV if b_blocked else 1)
    elif mode == "nt":
        m, kc = a.shape
        n = b.shape[-2]
    else:
        kc, m = a.shape
        n = b.shape[-1]
    nb = b.shape[-1] if b_blocked else None
    tn = _pick(n if not out_blocks else out_blocks, (512, 256, 128))
    if b_blocked and mode == "nn":
        tn = _pick(nb, (512, 256, 128))
    k_unit = nb if (b_blocked and mode == "nt") else 1
    tm, tk = None, None
    for tm_try in (1024, 512, 256, 128):
        if m % tm_try:
            continue
        for tk_try in (kc, kc // 2, kc // 4, 2048, 1024, 512, 256, 128):
            if tk_try > kc or tk_try < k_unit or kc % tk_try or tk_try % k_unit:
                continue
            tiles = 2 * (tm_try * tk_try * a.dtype.itemsize + tk_try * tn * b.dtype.itemsize)
            tiles += tm_try * tn * (2 * jnp.dtype(out_dtype).itemsize + 4 + (8 if res is not None else 0))
            if tiles <= MATMUL_TILE_BYTES:
                tm, tk = tm_try, tk_try
                break
        if tm:
            break
    if tm is None:
        tm, tk = m, max(k_unit, LANES if kc % LANES == 0 else kc)
    nk = kc // tk
    grid = (m // tm, n // tn, nk)

    if mode == "tn":
        a_spec = pl.BlockSpec((tk, tm), lambda i, j, k: (k, i))
        dims = (((0,), (0,)), ((), ()))
    else:
        a_spec = pl.BlockSpec((tm, tk), lambda i, j, k: (i, k))
        dims = (((1,), (0,)), ((), ())) if mode == "nn" else (((1,), (1,)), ((), ()))
    if mode == "nt":
        if b_blocked:
            b_spec = pl.BlockSpec((tk // nb, tn, nb), lambda i, j, k: (k, j, 0))
        else:
            b_spec = pl.BlockSpec((tn, tk), lambda i, j, k: (j, k))
    else:
        if b_blocked:
            per = nb // tn
            b_spec = pl.BlockSpec((None, tk, tn), lambda i, j, k: (j // per, k, j % per))
        else:
            b_spec = pl.BlockSpec((tk, tn), lambda i, j, k: (k, j))
    if out_blocks:
        per_o = out_blocks // tn
        out_shape = jax.ShapeDtypeStruct((n // out_blocks, m, out_blocks), out_dtype)
        out_spec = pl.BlockSpec((None, tm, tn), lambda i, j, k: (j // per_o, i, j % per_o))
    else:
        out_shape = jax.ShapeDtypeStruct((m, n), out_dtype)
        out_spec = pl.BlockSpec((tm, tn), lambda i, j, k: (i, j))
    in_specs = [a_spec, b_spec]
    args = [a, b]
    if res is not None:
        in_specs.append(pl.BlockSpec((tm, tn), lambda i, j, k: (i, j)))
        args.append(res)
    if after is not None:
        in_specs.append(pl.BlockSpec(memory_space=pl.ANY))
        args.append(after)
    n_in = len(args)

    def body(*refs):
        a_ref, b_ref = refs[:2]
        r_ref = refs[2] if res is not None else None
        o_ref = refs[n_in]

        def finish(r):
            if res is not None:
                r = r + r_ref[...].astype(F32)
            o_ref[...] = r.astype(out_dtype)

        if b_blocked and mode == "nt":
            part = None
            for blk in range(tk // nb):
                term = lax.dot_general(a_ref[:, blk * nb:(blk + 1) * nb].astype(BF16), b_ref[blk].astype(BF16), dims,
                                       preferred_element_type=F32)
                part = term if part is None else part + term
        else:
            part = lax.dot_general(a_ref[...].astype(BF16), b_ref[...].astype(BF16), dims, preferred_element_type=F32)
        if nk == 1:
            finish(part)
            return
        acc_ref = refs[-1]
        k = pl.program_id(2)

        @pl.when(k == 0)
        def _():
            acc_ref[...] = part

        @pl.when(k > 0)
        def _():
            acc_ref[...] += part

        @pl.when(k == nk - 1)
        def _():
            finish(acc_ref[...])

    return pl.pallas_call(
        body, out_shape=out_shape, grid=grid, in_specs=in_specs, out_specs=out_spec,
        scratch_shapes=[] if nk == 1 else [pltpu.VMEM((tm, tn), F32)], name=name,
        compiler_params=_cparams(("parallel", "parallel", "arbitrary")),
    )(*args)


def _dot(a, b, dims, precision=None):
    return lax.dot_general(a, b, (dims, ((), ())), preferred_element_type=F32, precision=precision)


_NN = ((1,), (0,))
_NT = ((1,), (1,))
_TN = ((0,), (0,))
_EXACT = lax.Precision.HIGHEST


def _chunk_decay_terms(dt, a):
    ln = dt.shape[0]
    row = lax.broadcasted_iota(jnp.int32, (ln, ln), 0)
    col = lax.broadcasted_iota(jnp.int32, (ln, ln), 1)
    tri = (row >= col).astype(F32)
    a_col = _dot(tri, dt * a, _NN, _EXACT)
    return a_col, a_col.T, row >= col


def _exact_dot(x, sel, terms):
    t = x.shape[0]
    parts, rest = [], x
    for k in range(terms):
        piece = rest.astype(BF16)
        parts.append(piece)
        if k + 1 < terms:
            rest = rest - piece.astype(F32)
    r = _dot(jnp.concatenate(parts, axis=0), sel, _NN)
    out = r[:t]
    for k in range(1, terms):
        out = out + r[k * t:(k + 1) * t]
    return out


def ssd_selectors(r_n):
    lane = jnp.arange(LANES)
    spread64 = (lane[:, None] == jnp.arange(r_n * SSD_HEAD_DIM)[None, :] // SSD_HEAD_DIM).astype(BF16)
    pair_sum = jnp.stack([lane[None, :] == 2 * q + lane[:, None] // SSD_HEAD_DIM for q in range(r_n // 2)]).astype(BF16)
    row_sum = jnp.stack([jnp.broadcast_to(lane[None, :] == r, (LANES, LANES)) for r in range(r_n)]).astype(BF16)
    return spread64, pair_sum, row_sum


def _ssd_chunk_setup(dt, a, spread64):
    ln = dt.shape[0]
    a_col, a_row, causal = _chunk_decay_terms(dt, a)
    ea = jnp.exp(a_col)
    te = jnp.exp(a_col[ln - 1:ln, :] - a_col)
    return (a_row, a_col, _exact_dot(dt, spread64, 2), _exact_dot(ea, spread64, 2), _exact_dot(te, spread64, 2),
            ea, causal)


def ssd_scan_fwd(xs, bm, cm, dtp, a_g, d_x, selectors, *, heads_per_group, name):
    s, di = xs.shape
    g_n = SSD_N_GROUPS
    r_n, p_n, n_n, ln = heads_per_group, SSD_HEAD_DIM, SSD_D_STATE, SSD_CHUNK
    nc = s // ln
    pairs, pw = r_n // 2, 2 * p_n
    spread64 = selectors[0]

    def body(xs_ref, bm_ref, cm_ref, dt_ref, a_ref, d_ref, s64_ref, y_ref, st_ref, state):
        c = pl.program_id(1)

        @pl.when(c == 0)
        def _():
            state[...] = jnp.zeros_like(state)

        a_row, a_col, dt_x, ea_x, te_x, _, causal = _ssd_chunk_setup(dt_ref[...], a_ref[...], s64_ref[...])
        bm_f = bm_ref[...]
        bmb = bm_f.astype(BF16)
        bm_t = bm_f.T.astype(BF16)
        cmb = cm_ref[...].astype(BF16)
        scores = _dot(cmb, bmb, _NT)
        first_head = lax.broadcasted_iota(jnp.int32, (1, pw), 1) < p_n
        for q in range(pairs):
            sl = slice(q * pw, (q + 1) * pw)
            x2 = xs_ref[:, sl]
            xdt2 = x2 * dt_x[:, sl]
            xdt2b = xdt2.astype(BF16)
            y_heads = []
            for r in (2 * q, 2 * q + 1):
                decay = jnp.exp(jnp.where(causal, a_col[:, r:r + 1] - a_row[r:r + 1, :], -jnp.inf))
                y_heads.append(_dot((scores * decay).astype(BF16), xdt2b, _NN))
            s2t = state[q]
            st_ref[q] = s2t
            y2 = jnp.where(first_head, y_heads[0], y_heads[1])
            y2 = y2 + ea_x[:, sl] * _dot(cmb, s2t.astype(BF16), _NN)
            y_ref[:, sl] = y2 + d_ref[:, sl] * x2
            state[q] = s2t * ea_x[ln - 1:ln, sl] + _dot(bm_t, (xdt2 * te_x[:, sl]).astype(BF16), _NN)

    whole = lambda t: pl.BlockSpec(t.shape, lambda g, c: (0,) * t.ndim)
    return pl.pallas_call(
        body,
        out_shape=(jax.ShapeDtypeStruct((s, di), F32),
                   jax.ShapeDtypeStruct((nc, g_n * pairs, n_n, pw), F32)),
        grid=(g_n, nc),
        in_specs=[pl.BlockSpec((ln, r_n * p_n), lambda g, c: (c, g)),
                  pl.BlockSpec((ln, n_n), lambda g, c: (c, g)),
                  pl.BlockSpec((ln, n_n), lambda g, c: (c, g)),
                  pl.BlockSpec((None, ln, LANES), lambda g, c: (g, c, 0)),
                  pl.BlockSpec((None, 1, LANES), lambda g, c: (g, 0, 0)),
                  pl.BlockSpec((None, 1, r_n * p_n), lambda g, c: (g, 0, 0)),
                  whole(spread64)],
        out_specs=(pl.BlockSpec((ln, r_n * p_n), lambda g, c: (c, g)),
                   pl.BlockSpec((None, pairs, n_n, pw), lambda g, c: (c, g, 0, 0))),
        scratch_shapes=[pltpu.VMEM((pairs, n_n, pw), F32)],
        name=name, compiler_params=_cparams(("parallel", "arbitrary")),
    )(xs, bm, cm, dtp, a_g, d_x, spread64)


def _row8(v):
    return jnp.broadcast_to(v, (8, v.shape[1]))


def ssd_scan_bwd(xs, bm, cm, dtp, a_g, d_x, selectors, states, dy, *, heads_per_group, name):
    s, di = xs.shape
    g_n = SSD_N_GROUPS
    r_n, p_n, n_n, ln = heads_per_group, SSD_HEAD_DIM, SSD_D_STATE, SSD_CHUNK
    nc = s // ln
    pairs, pw = r_n // 2, 2 * p_n
    spread64, pair_sum, row_sum = selectors

    def body(xs_ref, bm_ref, cm_ref, dt_ref, a_ref, d_ref, s64_ref, ps_ref, rs_ref, st_ref, dy_ref,
             dxs_ref, dbm_ref, dcm_ref, ddt_ref, dadt_ref, dd_ref, dstate, da_rows):
        c = pl.program_id(1)

        @pl.when(c == 0)
        def _():
            dstate[...] = jnp.zeros_like(dstate)
            dd_ref[...] = jnp.zeros_like(dd_ref)

        a_row, a_col, dt_x, ea_x, te_x, ea, causal = _ssd_chunk_setup(dt_ref[...], a_ref[...], s64_ref[...])
        row = lax.broadcasted_iota(jnp.int32, (ln, ln), 0)
        col = lax.broadcasted_iota(jnp.int32, (ln, ln), 1)
        causal_t = col >= row
        bmb = bm_ref[...].astype(BF16)
        cm_f = cm_ref[...]
        cmb = cm_f.astype(BF16)
        cm_t = cm_f.T.astype(BF16)
        scores = _dot(cmb, bmb, _NT)
        scores_t = _dot(bmb, cmb, _NT)
        first_head = lax.broadcasted_iota(jnp.int32, (1, pw), 1) < p_n
        e_last = ea[ln - 1:ln, :]
        da_rows[...] = jnp.zeros_like(da_rows)
        dscores = jnp.zeros((ln, ln), F32)
        dcm = jnp.zeros((ln, n_n), F32)
        dbm = jnp.zeros((ln, n_n), F32)
        da_cols = jnp.zeros((ln, LANES), F32)
        da_last = jnp.zeros((1, LANES), F32)
        ddt = jnp.zeros((ln, LANES), F32)
        dd = jnp.zeros((1, LANES), F32)
        for q in range(pairs):
            sl = slice(q * pw, (q + 1) * pw)
            sum2 = ps_ref[q]
            x2 = xs_ref[:, sl]
            dt2 = dt_x[:, sl]
            xdt2 = x2 * dt2
            xdt2b = xdt2.astype(BF16)
            dy2 = dy_ref[:, sl]
            dy2b = dy2.astype(BF16)
            dxdt_heads = []
            for h, r in enumerate((2 * q, 2 * q + 1)):
                a_r = jnp.broadcast_to(a_col[:, r:r + 1], (ln, ln))
                decay = jnp.exp(jnp.where(causal, a_r - a_row[r:r + 1, :], -jnp.inf))
                decay_t = jnp.exp(jnp.where(causal_t, a_row[r:r + 1, :] - a_r, -jnp.inf))
                dy_h = jnp.where(first_head if h == 0 else jnp.logical_not(first_head), dy2, 0.0).astype(BF16)
                dm = _dot(dy_h, xdt2b, _NT)
                dscores = dscores + dm * decay
                e_mat = dm * (scores * decay)
                da_cols = da_cols + _exact_dot(e_mat, rs_ref[r], 2)
                da_rows[r:r + 1, :] = -jnp.sum(e_mat, axis=0, keepdims=True)
                dxdt_heads.append(_dot((scores_t * decay_t).astype(BF16), dy2b, _NN))
            dxdt2 = jnp.where(first_head, dxdt_heads[0], dxdt_heads[1])
            s2t = st_ref[q]
            s2tb = s2t.astype(BF16)
            ds2t = dstate[q]
            ds2tb = ds2t.astype(BF16)
            ea2, te2 = ea_x[:, sl], te_x[:, sl]
            y_off2 = ea2 * _dot(cmb, s2tb, _NN)
            dy_e2 = (dy2 * ea2).astype(BF16)
            dcm = dcm + _dot(dy_e2, s2tb, _NT)
            ds_in = _dot(cm_t, dy_e2, _NN)
            da_cols = da_cols + _exact_dot(dy2 * y_off2, sum2, 2)
            bds2 = _dot(bmb, ds2tb, _NN)
            dxdt2 = dxdt2 + te2 * bds2
            xdt_e2 = xdt2 * te2
            dbm = dbm + _dot(xdt_e2.astype(BF16), ds2tb, _NT)
            w_cols = _exact_dot(xdt_e2 * bds2, sum2, 2)
            da_cols = da_cols - w_cols
            state_dot = _exact_dot(_row8(jnp.sum(ds2t * s2t, axis=0, keepdims=True)), sum2, 2)[0:1]
            da_last = da_last + jnp.sum(w_cols, axis=0, keepdims=True) + e_last * state_dot
            dstate[q] = ds2t * ea_x[ln - 1:ln, sl] + ds_in
            dxs_ref[:, sl] = dxdt2 * dt2 + d_ref[:, sl] * dy2
            ddt = ddt + _exact_dot(dxdt2 * x2, sum2, 2)
            dd = dd + _exact_dot(_row8(jnp.sum(dy2 * x2, axis=0, keepdims=True)), sum2, 2)[0:1]
        dcm_ref[...] = dcm + _dot(dscores.astype(BF16), bmb, _NN)
        dbm_ref[...] = dbm + _dot(dscores.T.astype(BF16), cmb, _NN)
        da_total = da_cols + da_rows[...].T
        upper = causal_t.astype(F32)
        dadt_ref[...] = _dot(upper, da_total, _NN, _EXACT) + da_last
        ddt_ref[...] = ddt
        dd_ref[...] += dd

    last_c = nc - 1
    whole = lambda t: pl.BlockSpec(t.shape, lambda g, c: (0,) * t.ndim)
    return pl.pallas_call(
        body,
        out_shape=(jax.ShapeDtypeStruct((s, di), F32),
                   jax.ShapeDtypeStruct(bm.shape, F32),
                   jax.ShapeDtypeStruct(cm.shape, F32),
                   jax.ShapeDtypeStruct(dtp.shape, F32),
                   jax.ShapeDtypeStruct(dtp.shape, F32),
                   jax.ShapeDtypeStruct(a_g.shape, F32)),
        grid=(g_n, nc),
        in_specs=[pl.BlockSpec((ln, r_n * p_n), lambda g, c: (last_c - c, g)),
                  pl.BlockSpec((ln, n_n), lambda g, c: (last_c - c, g)),
                  pl.BlockSpec((ln, n_n), lambda g, c: (last_c - c, g)),
                  pl.BlockSpec((None, ln, LANES), lambda g, c: (g, last_c - c, 0)),
                  pl.BlockSpec((None, 1, LANES), lambda g, c: (g, 0, 0)),
                  pl.BlockSpec((None, 1, r_n * p_n), lambda g, c: (g, 0, 0)),
                  whole(spread64), whole(pair_sum), whole(row_sum),
                  pl.BlockSpec((None, pairs, n_n, pw), lambda g, c: (last_c - c, g, 0, 0)),
                  pl.BlockSpec((ln, r_n * p_n), lambda g, c: (last_c - c, g))],
        out_specs=(pl.BlockSpec((ln, r_n * p_n), lambda g, c: (last_c - c, g)),
                   pl.BlockSpec((ln, n_n), lambda g, c: (last_c - c, g)),
                   pl.BlockSpec((ln, n_n), lambda g, c: (last_c - c, g)),
                   pl.BlockSpec((None, ln, LANES), lambda g, c: (g, last_c - c, 0)),
                   pl.BlockSpec((None, ln, LANES), lambda g, c: (g, last_c - c, 0)),
                   pl.BlockSpec((None, 1, LANES), lambda g, c: (g, 0, 0))),
        scratch_shapes=[pltpu.VMEM((pairs, n_n, pw), F32), pltpu.VMEM((LANES, ln), F32)],
        name=name, compiler_params=_cparams(("parallel", "arbitrary")),
    )(xs, bm, cm, dtp, a_g, d_x, spread64, pair_sum, row_sum, states, dy)


SB_Q_TILE = 2048
SB_K_TILE = 256


def _tri_sum(x, tri):
    t = x.shape[0]
    hi = x.astype(BF16)
    r1 = x - hi.astype(F32)
    mid = r1.astype(BF16)
    lo = (r1 - mid.astype(F32)).astype(BF16)
    r = _dot(jnp.concatenate([hi, mid, lo], axis=0), tri, _NN)
    return r[:t] + r[t:2 * t] + r[2 * t:]


def _sb_logits(q, k_j, scale, strict):
    z = _dot(q, k_j, _NT) * scale
    sp = jnp.log(1.0 + jnp.exp(-jnp.abs(z)))
    log_b = jnp.minimum(z, 0.0) - sp
    log_1mb = log_b - z
    if strict is not None:
        log_1mb = jnp.where(strict, log_1mb, 0.0)
    return log_b, log_1mb


def _sb_tiles(s):
    tq = _pick(s, (SB_Q_TILE, 2 * SB_K_TILE, SB_K_TILE, LANES))
    return tq, min(tq, SB_K_TILE)


def _sb_diag_mask(rows, tk):
    return lax.broadcasted_iota(jnp.int32, (rows, tk), 1) < lax.broadcasted_iota(jnp.int32, (rows, tk), 0)


def _sb_iotas(t):
    row = lax.broadcasted_iota(jnp.int32, (t, t), 0)
    col = lax.broadcasted_iota(jnp.int32, (t, t), 1)
    return row, col


def sb_attn_fwd(qn, kn, v, *, v_off=0, name):
    s, w = qn.shape
    dh = SB_HEAD_DIM
    n_h = w // dh
    tq, tk = _sb_tiles(s)
    per = tq // tk
    scale = 1.0 / math.sqrt(dh)

    def body(q_ref, k_ref, v_ref, o_ref, tot_ref):
        i = pl.program_id(1)
        q = q_ref[...]
        row, col = _sb_iotas(tk)
        later = (row > col).astype(BF16)

        def tile(q_rows, j, acc, run, mask):
            s0 = pl.multiple_of(j * tk, tk)
            k_j = k_ref[pl.ds(s0, tk), :]
            v_j = v_ref[pl.ds(s0, tk), :].astype(BF16)
            log_b, log_1mb = _sb_logits(q_rows, k_j, scale, mask)
            att = jnp.exp(log_b + (_tri_sum(log_1mb, later) + run))
            if mask is not None:
                att = jnp.where(mask, att, 0.0)
            return acc + _dot(att.astype(BF16), v_j, _NN), run + jnp.sum(log_1mb, axis=1, keepdims=True)

        acc, run = jnp.zeros((tq, dh), F32), jnp.zeros((tq, 1), F32)
        for d in reversed(range(per)):
            r0 = d * tk
            a2, r2 = tile(q[r0:], i * per + d, acc[r0:], run[r0:], _sb_diag_mask(tq - r0, tk))
            acc = a2 if r0 == 0 else jnp.concatenate([acc[:r0], a2], axis=0)
            run = r2 if r0 == 0 else jnp.concatenate([run[:r0], r2], axis=0)

        def group(gg, c):
            for d in reversed(range(per)):
                c = tile(q, (i - 1 - gg) * per + d, c[0], c[1], None)
            return c

        acc, run = lax.fori_loop(0, i, group, (acc, run))
        o_ref[...] = acc
        tot_ref[...] = jnp.broadcast_to(run, (tq, dh))

    return pl.pallas_call(
        body,
        out_shape=(jax.ShapeDtypeStruct((s, w), F32), jax.ShapeDtypeStruct((s, w), F32)),
        grid=(n_h, s // tq),
        in_specs=[pl.BlockSpec((tq, dh), lambda h, i: (i, h)),
                  pl.BlockSpec((s, dh), lambda h, i: (0, h)),
                  pl.BlockSpec((s, dh), lambda h, i: (0, v_off + h))],
        out_specs=(pl.BlockSpec((tq, dh), lambda h, i: (i, h)),
                   pl.BlockSpec((tq, dh), lambda h, i: (i, h))),
        name=name, compiler_params=_cparams(("parallel", "parallel")),
    )(qn, kn, v)


def sb_attn_bwd(qn, kn, v, tot, do, *, v_off=0, name):
    s, w = qn.shape
    dh = SB_HEAD_DIM
    n_h = w // dh
    tq, tk = _sb_tiles(s)
    per = tq // tk
    scale = 1.0 / math.sqrt(dh)

    def body(q_ref, k_ref, v_ref, tot_ref, do_ref, dq_ref, dk_ref, dv_ref):
        dk_ref[...] = jnp.zeros_like(dk_ref)
        dv_ref[...] = jnp.zeros_like(dv_ref)
        row, col = _sb_iotas(tk)
        upto = (row <= col).astype(BF16)
        before = (row < col).astype(BF16)

        def q_block(i, _):
            t0 = pl.multiple_of(i * tq, tq)
            q = q_ref[pl.ds(t0, tq), :]
            do_i = do_ref[pl.ds(t0, tq), :].astype(BF16)
            total = tot_ref[pl.ds(t0, tq), 0:1]

            def tile(r0, j, dq, run_l, run_g, mask):
                s0 = pl.multiple_of(j * tk, tk)
                k_j = k_ref[pl.ds(s0, tk), :]
                v_j = v_ref[pl.ds(s0, tk), :].astype(BF16)
                q_r, do_r = q[r0:], do_i[r0:]
                log_b, log_1mb = _sb_logits(q_r, k_j, scale, mask)
                att = jnp.exp(log_b + ((total[r0:] - run_l) - _tri_sum(log_1mb, upto)))
                if mask is not None:
                    att = jnp.where(mask, att, 0.0)
                g = att * _dot(do_r, v_j, _NT)
                c = _tri_sum(g, before) + run_g
                dz = (g - (g + c) * jnp.exp(log_b)) * scale
                if mask is not None:
                    dz = jnp.where(mask, dz, 0.0)
                dz = dz.astype(BF16)
                dk_ref[pl.ds(s0, tk), :] += _dot(dz, q_r, _TN)
                dv_ref[pl.ds(s0, tk), :] += _dot(att.astype(BF16), do_r, _TN)
                return (dq + _dot(dz, k_j, _NN), run_l + jnp.sum(log_1mb, axis=1, keepdims=True),
                        run_g + jnp.sum(g, axis=1, keepdims=True))

            def group(gg, c):
                for d in range(per):
                    c = tile(0, gg * per + d, c[0], c[1], c[2], None)
                return c

            zero = jnp.zeros((tq, 1), F32)
            dq, run_l, run_g = lax.fori_loop(0, i, group, (jnp.zeros((tq, dh), F32), zero, zero))
            for d in range(per):
                r0 = d * tk
                p_dq, p_l, p_g = tile(r0, i * per + d, dq[r0:], run_l[r0:], run_g[r0:], _sb_diag_mask(tq - r0, tk))
                if r0 == 0:
                    dq, run_l, run_g = p_dq, p_l, p_g
                else:
                    dq = jnp.concatenate([dq[:r0], p_dq], axis=0)
                    run_l = jnp.concatenate([run_l[:r0], p_l], axis=0)
                    run_g = jnp.concatenate([run_g[:r0], p_g], axis=0)
            dq_ref[pl.ds(t0, tq), :] = dq
            return 0

        lax.fori_loop(0, s // tq, q_block, 0)

    head = pl.BlockSpec((s, dh), lambda h: (0, h))
    return pl.pallas_call(
        body,
        out_shape=tuple(jax.ShapeDtypeStruct((s, w), F32) for _ in range(3)),
        grid=(n_h,),
        in_specs=[head, head, pl.BlockSpec((s, dh), lambda h: (0, v_off + h)), head, head],
        out_specs=(head, head, head),
        name=name, compiler_params=_cparams(("parallel",)),
    )(qn, kn, v, tot, do)


ROW_TILE = 256
WIDE_ROW_TILE = 64


def _rows(width, col=0, tm=ROW_TILE):
    return pl.BlockSpec((tm, width), lambda i: (i, col))


_wide_rows = functools.partial(_rows, tm=WIDE_ROW_TILE)


def _whole(shape):
    return pl.BlockSpec(shape, lambda i: (0,) * len(shape))


def _ew_call(body, out_shape, in_specs, out_specs, args, n_rows, name, carried=False):
    return pl.pallas_call(
        body, out_shape=out_shape, grid=(n_rows // in_specs[0].block_shape[0],), in_specs=in_specs, out_specs=out_specs,
        name=name, compiler_params=_cparams(("arbitrary",) if carried else ("parallel",)),
    )(*args)


def _first_step(*refs):
    @pl.when(pl.program_id(0) == 0)
    def _():
        for r in refs:
            r[...] = jnp.zeros_like(r)


def rmsnorm_fwd(x, w, after=None, *, name):
    s, d = x.shape

    def body(x_ref, w_ref, *rest):
        o_ref = rest[-1]
        xv = x_ref[...]
        r = lax.rsqrt(jnp.mean(xv * xv, axis=-1, keepdims=True) + NORM_EPS)
        o_ref[...] = (xv * r * w_ref[...]).astype(BF16)

    extra = [] if after is None else [after]
    return _ew_call(body, jax.ShapeDtypeStruct((s, d), BF16),
                    [_rows(d), _whole((1, d))] + [_whole(TOKEN_SHAPE)] * len(extra), _rows(d),
                    (x, w.reshape(1, d), *extra), s, name)


def rmsnorm_bwd(x, w, dy, dres, *, name):
    s, d = x.shape

    def body(x_ref, w_ref, dy_ref, dr_ref, dx_ref, dw_ref):
        _first_step(dw_ref)
        xv = x_ref[...]
        r = lax.rsqrt(jnp.mean(xv * xv, axis=-1, keepdims=True) + NORM_EPS)
        xhat = xv * r
        dyv = dy_ref[...].astype(F32)
        dw_ref[...] += jnp.sum(dyv * xhat, axis=0, keepdims=True)
        g = dyv * w_ref[...]
        dx_ref[...] = dr_ref[...] + r * (g - xhat * jnp.mean(g * xhat, axis=-1, keepdims=True))

    return _ew_call(body, (jax.ShapeDtypeStruct((s, d), F32), jax.ShapeDtypeStruct((1, d), F32)),
                    [_rows(d), _whole((1, d)), _rows(d), _rows(d)], (_rows(d), _whole((1, d))),
                    (x, w.reshape(1, d), dy, dres), s, name, carried=True)


def ple_fwd(h1, gate_pre, pp, *, name):
    s, d = h1.shape

    def body(h_ref, g_ref, p_ref, o_ref):
        o_ref[...] = h_ref[...] + p_ref[...] * _sigmoid(g_ref[...])

    return _ew_call(body, jax.ShapeDtypeStruct((s, d), F32), [_rows(d)] * 3, _rows(d), (h1, gate_pre, pp), s, name)


def ple_bwd(dh2, gate_pre, pp, after, *, name):
    s, d = dh2.shape

    def body(dh_ref, g_ref, p_ref, after_ref, dp_ref, dg_ref):
        gate = _sigmoid(g_ref[...])
        dh = dh_ref[...]
        dp_ref[...] = (dh * gate).astype(BF16)
        dg_ref[...] = (dh * p_ref[...] * gate * (1.0 - gate)).astype(BF16)

    shp = jax.ShapeDtypeStruct((s, d), BF16)
    return _ew_call(body, (shp, shp), [_rows(d)] * 3 + [_whole(TOKEN_SHAPE)], (_rows(d), _rows(d)),
                    (dh2, gate_pre, pp, after), s, name)


def loss_head(y, target, *, name):
    s, d = y.shape

    def body(y_ref, t_ref, l_ref, dy_ref):
        _first_step(l_ref)
        err = y_ref[...] - t_ref[...]
        per_tok = jnp.mean(err * err, axis=-1, keepdims=True)
        l_ref[...] += 0.5 * jnp.sum(per_tok, axis=0, keepdims=True)
        dy_ref[...] = err * (1.0 / d)

    return _ew_call(body, (jax.ShapeDtypeStruct((1, 1), F32), jax.ShapeDtypeStruct((s, d), F32)),
                    [_rows(d), _rows(d)], (_whole((1, 1)), _rows(d)), (y, target), s, name, carried=True)


CONV_COL_TILE = 256


def _conv_taps(x, w_ref):
    row = lax.broadcasted_iota(jnp.int32, (x.shape[0], 1), 0)
    acc = x * w_ref[SSD_D_CONV - 1:SSD_D_CONV, :]
    shifted = []
    for d in range(1, SSD_D_CONV):
        xs = jnp.where(row >= d, pltpu.roll(x, d, 0), 0.0)
        shifted.append(xs)
        acc = acc + xs * w_ref[SSD_D_CONV - 1 - d:SSD_D_CONV - d, :]
    return acc, shifted


def ssd_conv_fwd(x, w, b, *, name):
    s, c = x.shape
    tc = _pick(c, (CONV_COL_TILE, LANES))

    def body(x_ref, w_ref, b_ref, o_ref):
        pre, _ = _conv_taps(x_ref[...], w_ref)
        o_ref[...] = _silu(pre + b_ref[...])

    col = pl.BlockSpec((s, tc), lambda j: (0, j))
    return pl.pallas_call(
        body, out_shape=jax.ShapeDtypeStruct((s, c), F32), grid=(c // tc,),
        in_specs=[col, pl.BlockSpec((SSD_D_CONV, tc), lambda j: (0, j)), pl.BlockSpec((1, tc), lambda j: (0, j))],
        out_specs=col, name=name, compiler_params=_cparams(("parallel",)),
    )(x, w, b)


def ssd_conv_bwd(x, w, b, dact, *, name):
    s, c = x.shape
    tc = _pick(c, (CONV_COL_TILE, LANES))

    def body(x_ref, w_ref, b_ref, da_ref, dx_ref, dw_ref, db_ref):
        xv = x_ref[...]
        pre, shifted = _conv_taps(xv, w_ref)
        dpre = da_ref[...] * _silu_grad(pre + b_ref[...])
        db_ref[...] = jnp.sum(dpre, axis=0, keepdims=True)
        row = lax.broadcasted_iota(jnp.int32, (s, 1), 0)
        dx = dpre * w_ref[SSD_D_CONV - 1:SSD_D_CONV, :]
        dw_ref[SSD_D_CONV - 1:SSD_D_CONV, :] = jnp.sum(dpre * xv, axis=0, keepdims=True)
        for d in range(1, SSD_D_CONV):
            k = SSD_D_CONV - 1 - d
            dw_ref[k:k + 1, :] = jnp.sum(dpre * shifted[d - 1], axis=0, keepdims=True)
            up = jnp.where(row < s - d, pltpu.roll(dpre, s - d, 0), 0.0)
            dx = dx + up * w_ref[k:k + 1, :]
        dx_ref[...] = dx.astype(BF16)

    col = pl.BlockSpec((s, tc), lambda j: (0, j))
    wspec = pl.BlockSpec((SSD_D_CONV, tc), lambda j: (0, j))
    bspec = pl.BlockSpec((1, tc), lambda j: (0, j))
    return pl.pallas_call(
        body,
        out_shape=(jax.ShapeDtypeStruct((s, c), BF16), jax.ShapeDtypeStruct((SSD_D_CONV, c), F32),
                   jax.ShapeDtypeStruct((1, c), F32)),
        grid=(c // tc,), in_specs=[col, wspec, bspec, col], out_specs=(col, wspec, bspec),
        name=name, compiler_params=_cparams(("parallel",)),
    )(x, w, b, dact)


def ssd_dt_fwd(dt_raw, bias, a_log, *, name):
    s, h = dt_raw.shape

    def body(r_ref, b_ref, al_ref, dt_ref, a_ref):
        zv = r_ref[...] + b_ref[...]
        dt_ref[...] = jnp.maximum(zv, 0.0) + jnp.log(1.0 + jnp.exp(-jnp.abs(zv)))
        a_ref[...] = -jnp.exp(al_ref[...])

    full = pl.BlockSpec((s, h), lambda: (0, 0))
    vec = pl.BlockSpec((1, h), lambda: (0, 0))
    return pl.pallas_call(
        body, out_shape=(jax.ShapeDtypeStruct((s, h), F32), jax.ShapeDtypeStruct((1, h), F32)),
        in_specs=[full, vec, vec], out_specs=(full, vec), name=name, compiler_params=_cparams(),
    )(dt_raw, bias.reshape(1, h), a_log.reshape(1, h))


def ssd_dt_bwd(dt_raw, bias, a_log, dt, ddt, dadt, *, name):
    s, h = dt_raw.shape

    def body(r_ref, b_ref, al_ref, dt_ref, ddt_ref, dadt_ref, dr_ref, db_ref, dal_ref):
        a = -jnp.exp(al_ref[...])
        dadt_v = dadt_ref[...]
        d_dt = ddt_ref[...] + a * dadt_v
        d_raw = d_dt * _sigmoid(r_ref[...] + b_ref[...])
        dr_ref[...] = d_raw
        db_ref[...] = jnp.sum(d_raw, axis=0, keepdims=True)
        dal_ref[...] = jnp.sum(dadt_v * dt_ref[...], axis=0, keepdims=True) * a

    full = pl.BlockSpec((s, h), lambda: (0, 0))
    vec = pl.BlockSpec((1, h), lambda: (0, 0))
    return pl.pallas_call(
        body, out_shape=(jax.ShapeDtypeStruct((s, h), F32), jax.ShapeDtypeStruct((1, h), F32),
                         jax.ShapeDtypeStruct((1, h), F32)),
        in_specs=[full, vec, vec, full, full, full], out_specs=(full, vec, vec), name=name,
        compiler_params=_cparams(),
    )(dt_raw, bias.reshape(1, h), a_log.reshape(1, h), dt, ddt, dadt)


def _group_mean(v, n_groups):
    gw = v.shape[-1] // n_groups
    parts = [jnp.broadcast_to(jnp.mean(v[:, k * gw:(k + 1) * gw], axis=-1, keepdims=True), (v.shape[0], gw))
             for k in range(n_groups)]
    return jnp.concatenate(parts, axis=-1)


def ssd_gate_fwd(y, z, gw, *, name):
    s, di = y.shape

    def body(y_ref, z_ref, w_ref, o_ref):
        yg = y_ref[...] * _silu(z_ref[...])
        r = lax.rsqrt(_group_mean(yg * yg, SSD_N_GROUPS) + GATED_NORM_EPS)
        o_ref[...] = (yg * r * w_ref[...]).astype(BF16)

    return _ew_call(body, jax.ShapeDtypeStruct((s, di), BF16), [_wide_rows(di), _wide_rows(di), _whole((1, di))],
                    _wide_rows(di), (y, z, gw.reshape(1, di)), s, name)


def ssd_gate_bwd(y, z, gw, dyn, *, name):
    s, di = y.shape

    def body(y_ref, z_ref, w_ref, dn_ref, dy_ref, dz_ref, dw_ref):
        _first_step(dw_ref)
        yv, zv = y_ref[...], z_ref[...]
        sz = _silu(zv)
        yg = yv * sz
        r = lax.rsqrt(_group_mean(yg * yg, SSD_N_GROUPS) + GATED_NORM_EPS)
        yhat = yg * r
        dn = dn_ref[...]
        dw_ref[...] += jnp.sum(dn * yhat, axis=0, keepdims=True)
        g = dn * w_ref[...]
        dyg = r * (g - yhat * _group_mean(g * yhat, SSD_N_GROUPS))
        dy_ref[...] = dyg * sz
        dz_ref[...] = (dyg * yv * _silu_grad(zv)).astype(BF16)

    return _ew_call(body, (jax.ShapeDtypeStruct((s, di), F32), jax.ShapeDtypeStruct((s, di), BF16),
                           jax.ShapeDtypeStruct((1, di), F32)),
                    [_wide_rows(di), _wide_rows(di), _whole((1, di)), _wide_rows(di)],
                    (_wide_rows(di), _wide_rows(di), _whole((1, di))),
                    (y, z, gw.reshape(1, di), dyn), s, name, carried=True)


def _head_mean(v):
    return _group_mean(v, v.shape[-1] // SB_HEAD_DIM)


def sb_qk_fwd(proj, qw, kw, *, name):
    s, w4 = proj.shape
    w = w4 // 4
    reps = w // SB_HEAD_DIM

    def body(q_ref, k_ref, qw_ref, kw_ref, qn_ref, kn_ref):
        for x_ref, w_ref, o_ref in ((q_ref, qw_ref, qn_ref), (k_ref, kw_ref, kn_ref)):
            xv = x_ref[...]
            r = lax.rsqrt(_head_mean(xv * xv) + NORM_EPS)
            o_ref[...] = (xv * r * jnp.tile(w_ref[...], (1, reps))).astype(BF16)

    shp = jax.ShapeDtypeStruct((s, w), BF16)
    return _ew_call(body, (shp, shp), [_rows(w, 0), _rows(w, 1), _whole((1, SB_HEAD_DIM)), _whole((1, SB_HEAD_DIM))],
                    (_rows(w), _rows(w)), (proj, proj, qw.reshape(1, -1), kw.reshape(1, -1)), s, name)


def sb_gate_fwd(o, proj, *, name):
    s, w = o.shape

    def body(o_ref, g_ref, og_ref):
        og_ref[...] = (o_ref[...] * _silu(g_ref[...])).astype(BF16)

    return _ew_call(body, jax.ShapeDtypeStruct((s, w), BF16), [_rows(w), _rows(w, 3)], _rows(w), (o, proj), s, name)


def sb_gate_bwd(dog, o, proj, *, name):
    s, w = o.shape

    def body(d_ref, o_ref, g_ref, do_ref, dg_ref):
        gv, dv = g_ref[...], d_ref[...]
        do_ref[...] = dv * _silu(gv)
        dg_ref[...] = (dv * o_ref[...] * _silu_grad(gv)).astype(BF16)

    return _ew_call(body, (jax.ShapeDtypeStruct((s, w), F32), jax.ShapeDtypeStruct((s, w), BF16)),
                    [_rows(w), _rows(w), _rows(w, 3)], (_rows(w), _rows(w)), (dog, o, proj), s, name)


def sb_pack_bwd(proj, qw, kw, dqn, dkn, dv, dg, *, name):
    s, w4 = proj.shape
    w = w4 // 4
    reps = w // SB_HEAD_DIM

    def body(q_ref, k_ref, qw_ref, kw_ref, dqn_ref, dkn_ref, dv_ref, dg_ref, dp_ref, dqw_ref, dkw_ref):
        _first_step(dqw_ref, dkw_ref)
        for idx, (x_ref, w_ref, d_ref, dw_ref) in enumerate(((q_ref, qw_ref, dqn_ref, dqw_ref),
                                                           (k_ref, kw_ref, dkn_ref, dkw_ref))):
            xv = x_ref[...]
            r = lax.rsqrt(_head_mean(xv * xv) + NORM_EPS)
            xhat = xv * r
            dn = d_ref[...]
            per_col = jnp.sum(dn * xhat, axis=0, keepdims=True)
            acc = per_col[:, 0:SB_HEAD_DIM]
            for hh in range(1, reps):
                acc = acc + per_col[:, hh * SB_HEAD_DIM:(hh + 1) * SB_HEAD_DIM]
            dw_ref[...] += acc
            g = dn * jnp.tile(w_ref[...], (1, reps))
            dp_ref[:, idx * w:(idx + 1) * w] = (r * (g - xhat * _head_mean(g * xhat))).astype(BF16)
        dp_ref[:, 2 * w:3 * w] = dv_ref[...].astype(BF16)
        dp_ref[:, 3 * w:4 * w] = dg_ref[...]

    vec = _whole((1, SB_HEAD_DIM))
    return _ew_call(body, (jax.ShapeDtypeStruct((s, w4), BF16), jax.ShapeDtypeStruct((1, SB_HEAD_DIM), F32),
                           jax.ShapeDtypeStruct((1, SB_HEAD_DIM), F32)),
                    [_wide_rows(w, 0), _wide_rows(w, 1), vec, vec, _wide_rows(w), _wide_rows(w), _wide_rows(w),
                     _wide_rows(w)],
                    (_wide_rows(w4), vec, vec),
                    (proj, proj, qw.reshape(1, -1), kw.reshape(1, -1), dqn, dkn, dv, dg), s, name, carried=True)


_HBM = pl.BlockSpec(memory_space=pltpu.HBM)


def _mesh_pos():
    return lax.axis_index("x"), lax.axis_index("y"), lax.axis_index("c")


def _other_chips(x, y):
    return [(1 - x, y), (x, 1 - y), (1 - x, 1 - y)]


_SEM = pl.BlockSpec(memory_space=pltpu.SEMAPHORE)
_ANY = pl.BlockSpec(memory_space=pl.ANY)
_DATAFLOW = pltpu.SideEffectType.DATAFLOW_SIDE_EFFECTING
N_PEER_CHIPS = N_CHIP - 1
TOKEN_SHAPE = (8, LANES)


def _in_hbm(t):
    return pltpu.with_memory_space_constraint(t, pltpu.HBM)


def _ici_copies(kind, src_refs, land_refs, send_sems, recv_sems, arrivals=False):
    x, y, c = _mesh_pos()
    out = []
    for a in range(len(land_refs)):
        if kind in ("pass", "swap"):
            if kind == "pass":
                src, dst = land_refs[a].at[:, c], land_refs[a].at[:, 1 - c if arrivals else c]
            else:
                src, dst = src_refs[a].at[:, 1 - c], land_refs[a]
            out.append(pltpu.make_async_remote_copy(
                src_ref=src, dst_ref=dst, send_sem=send_sems.at[a], recv_sem=recv_sems.at[a],
                device_id=(x, y, 1 - c), device_id_type=MESH))
            continue
        for j, chip in enumerate(_other_chips(x, y)):
            if kind == "gather":
                src = land_refs[a].at[4 * x + 2 * y + c]
                dst = land_refs[a].at[4 * chip[0] + 2 * chip[1] + c] if arrivals else src
            else:
                src, dst = src_refs[a].at[2 * chip[0] + chip[1]], land_refs[a].at[j]
            k = a * N_PEER_CHIPS + j
            out.append(pltpu.make_async_remote_copy(
                src_ref=src, dst_ref=dst, send_sem=send_sems.at[k], recv_sem=recv_sems.at[k],
                device_id=(*chip, c), device_id_type=MESH))
    return out


def _n_copies(kind, lands):
    return len(lands) * (1 if kind in ("pass", "swap") else N_PEER_CHIPS)


def ici_start(kind, srcs, lands, after=(), *, name):
    ns, nb = len(srcs), len(srcs) + len(lands)
    n_sem = _n_copies(kind, lands)

    def body(*refs):
        first_out = nb + len(after)
        for cp in _ici_copies(kind, refs[:ns], refs[ns:nb], refs[first_out], refs[first_out + 1]):
            cp.start()
        refs[-1][...] = jnp.zeros(TOKEN_SHAPE, F32)

    outs = pl.pallas_call(
        body, name=name,
        out_shape=(pltpu.SemaphoreType.DMA((n_sem,)), pltpu.SemaphoreType.DMA((n_sem,)),
                   *[pltpu.HBM(t.shape, t.dtype) for t in (*srcs, *lands)], jax.ShapeDtypeStruct(TOKEN_SHAPE, F32)),
        in_specs=[_HBM] * nb + [_ANY] * len(after),
        out_specs=(_SEM, _SEM, *([_HBM] * nb), pl.BlockSpec(memory_space=pltpu.VMEM)),
        input_output_aliases={k: 2 + k for k in range(nb)},
        compiler_params=pltpu.CompilerParams(has_side_effects=_DATAFLOW),
    )(*[_in_hbm(t) for t in (*srcs, *lands)], *after)
    return outs[0], outs[1], list(outs[2:2 + ns]), list(outs[2 + ns:2 + nb]), outs[-1]


def ici_wait(kind, started, after, *, name):
    send_sems, recv_sems, srcs, lands, _ = started
    ns, nb = len(srcs), len(srcs) + len(lands)

    def body(*refs):
        for cp in _ici_copies(kind, refs[:ns], refs[ns:nb], refs[nb], refs[nb + 1]):
            cp.wait_send()
        for cp in _ici_copies(kind, refs[:ns], refs[ns:nb], refs[nb], refs[nb + 1], arrivals=True):
            cp.wait_recv()

    outs = pl.pallas_call(
        body, name=name,
        out_shape=tuple(pltpu.HBM(t.shape, t.dtype) for t in (*srcs, *lands)),
        in_specs=[_HBM] * nb + [_SEM, _SEM] + [_ANY] * len(after),
        out_specs=tuple([_HBM] * nb),
        input_output_aliases={k: k for k in range(nb)},
        compiler_params=pltpu.CompilerParams(has_side_effects=_DATAFLOW),
    )(*srcs, *lands, send_sems, recv_sems, *after)
    return list(outs[:ns]), list(outs[ns:])


def all_reduce_small(v, *, name):
    r = v.shape[0]

    def body(v_ref, o_ref, buf, send_sems, recv_sems):
        x, y, c = _mesh_pos()
        me = 4 * x + 2 * y + c
        buf[me] = v_ref[...]
        copies = []
        for k in range(1, N_DEV):
            to = ((x + (k >> 2)) % 2, (y + ((k >> 1) & 1)) % 2, (c + (k & 1)) % 2)
            copies.append(pltpu.make_async_remote_copy(
                src_ref=v_ref, dst_ref=buf.at[me], send_sem=send_sems.at[k - 1], recv_sem=recv_sems.at[k - 1],
                device_id=to, device_id_type=MESH))
        for cp in copies:
            cp.start()
        for cp in copies:
            cp.wait()
        acc = buf[0]
        for d in range(1, N_DEV):
            acc = acc + buf[d]
        o_ref[...] = acc

    vm = pl.BlockSpec(memory_space=pltpu.VMEM)
    return pl.pallas_call(
        body, out_shape=jax.ShapeDtypeStruct(v.shape, F32), in_specs=[vm], out_specs=vm,
        scratch_shapes=[pltpu.VMEM((N_DEV, r, LANES), F32), pltpu.SemaphoreType.DMA((N_DEV - 1,)),
                        pltpu.SemaphoreType.DMA((N_DEV - 1,))],
        name=name,
    )(v)


def pair_add(g, r1, core, *, name):
    _, _, rows, cols = g.shape
    tm = _pick(rows, (256, 128))

    def body(c_ref, g_ref, r_ref, o_ref):
        o_ref[...] = (g_ref[...].astype(F32) + r_ref[...].astype(F32)).astype(o_ref.dtype)

    return pl.pallas_call(
        body, out_shape=jax.ShapeDtypeStruct(r1.shape, g.dtype),
        grid_spec=pltpu.PrefetchScalarGridSpec(
            num_scalar_prefetch=1, grid=(N_CHIP, rows // tm),
            in_specs=[pl.BlockSpec((None, None, tm, cols), lambda k, i, c_ref: (k, c_ref[0], i, 0)),
                      pl.BlockSpec((None, tm, cols), lambda k, i, c_ref: (k, i, 0))],
            out_specs=pl.BlockSpec((None, tm, cols), lambda k, i, c_ref: (k, i, 0))),
        name=name, compiler_params=_cparams(("parallel", "parallel")),
    )(core, g, r1)


def _adamw_math(w, g, m, v):
    m = ADAM_B1 * m + (1.0 - ADAM_B1) * g
    v = ADAM_B2 * v + (1.0 - ADAM_B2) * (g * g)
    m_hat = m / (1.0 - ADAM_B1 ** ADAM_STEP)
    v_hat = v / (1.0 - ADAM_B2 ** ADAM_STEP)
    delta = -ADAM_LR * (m_hat / (jnp.sqrt(v_hat) + ADAM_EPS) + ADAM_WD * w)
    return delta, m, v


def adamw_sharded(w, m, v, layer, chip_sums, received, chip, into, *, name):
    _, rows, cols = w.shape
    tm = _pick(rows, (256, 128))

    def body(k_ref, w_ref, m_ref, v_ref, t_ref, r_ref, *rest):
        g_ref, d_ref, nm_ref, nv_ref, token_ref = rest[-5:]
        g = t_ref[...].astype(F32)
        for j in range(N_CHIP - 1):
            g = g + r_ref[j].astype(F32)
        d, mm, vv = _adamw_math(w_ref[...], g, m_ref[...], v_ref[...])
        g_ref[...] = g
        d_ref[...] = d
        nm_ref[...] = mm
        nv_ref[...] = vv
        token_ref[...] = jnp.zeros(TOKEN_SHAPE, F32)

    blk = pl.BlockSpec((None, tm, cols), lambda i, k_ref: (layer, i, 0))
    shp = jax.ShapeDtypeStruct(w.shape, F32)
    in_specs = [blk, blk, blk,
                pl.BlockSpec((None, tm, cols), lambda i, k_ref: (k_ref[0], i, 0)),
                pl.BlockSpec((N_CHIP - 1, tm, cols), lambda i, k_ref: (0, i, 0))]
    operands = [chip, w, m, v, chip_sums, received]
    aliases = {}
    if into is not None:
        aliases = {len(operands) + q: q for q in range(4)}
        in_specs += [_ANY] * 4
        operands += list(into)
    outs = pl.pallas_call(
        body, out_shape=(shp, shp, shp, shp, jax.ShapeDtypeStruct(TOKEN_SHAPE, F32)),
        grid_spec=pltpu.PrefetchScalarGridSpec(
            num_scalar_prefetch=1, grid=(rows // tm,), in_specs=in_specs,
            out_specs=(blk, blk, blk, blk, pl.BlockSpec(TOKEN_SHAPE, lambda i, k_ref: (0, 0)))),
        input_output_aliases=aliases,
        name=name, compiler_params=_cparams(("arbitrary",)),
    )(*operands)
    return outs[:4], outs[4]


def adamw_replicated(w, m, v, g, *, name):
    def body(w_ref, m_ref, v_ref, g_ref, d_ref, nm_ref, nv_ref):
        d, mm, vv = _adamw_math(w_ref[...], g_ref[...], m_ref[...], v_ref[...])
        d_ref[...] = d
        nm_ref[...] = mm
        nv_ref[...] = vv

    shp = jax.ShapeDtypeStruct(w.shape, F32)
    return pl.pallas_call(body, out_shape=(shp, shp, shp), name=name, compiler_params=_cparams())(w, m, v, g)


WEIGHT_NAMES = ("norm_w", "ssd_in_w", "ssd_conv_w", "ssd_conv_b", "ssd_dt_bias", "ssd_a_log", "ssd_d",
                "ssd_gnorm_w", "ssd_out_w", "sb_in_w", "sb_qn_w", "sb_kn_w", "sb_out_w", "ple_norm_w",
                "ple_gate_w", "ple_proj_w")
REPLICATED = ("norm_w", "ssd_conv_b", "ssd_dt_bias", "ssd_a_log", "ssd_d", "ssd_gnorm_w", "sb_qn_w", "sb_kn_w",
              "ple_norm_w")
PACK_ROWS = 8


def _pack(parts):
    flat = jnp.concatenate([t.reshape(-1) for t in parts])
    pad = (-flat.shape[0]) % (PACK_ROWS * LANES)
    return jnp.pad(flat, (0, pad)).reshape(-1, LANES)


def _unpack(packed, like):
    flat = packed.reshape(-1)
    out, off = [], 0
    for t in like:
        out.append(flat[off:off + t.size].reshape(t.shape))
        off += t.size
    return out


def _to_group_lanes(v, r):
    t = v.reshape(v.shape[0], SSD_N_GROUPS, r).transpose(1, 0, 2)
    return jnp.pad(t, ((0, 0), (0, 0), (0, LANES - r)))


def _from_group_lanes(t, r):
    return t[:, :, :r].transpose(1, 0, 2).reshape(t.shape[1], SSD_N_GROUPS * r)


def _head_vec(v, r):
    return jnp.pad(v.reshape(SSD_N_GROUPS, 1, r), ((0, 0), (0, 0), (0, LANES - r)))


def _col_blocks(full):
    rows = full.shape[0]
    return full.reshape(rows, N_DEV, -1).transpose(1, 0, 2)


def _from_col_blocks(blocks):
    return blocks.transpose(1, 0, 2).reshape(blocks.shape[1], -1)


def _split_cols(full, widths):
    out, off = [], 0
    for w in widths:
        out.append(full[:, off:off + w])
        off += w
    return out


def kernel(x, p, norm_w, ssd_in_w, ssd_conv_w, ssd_conv_b, ssd_dt_bias, ssd_a_log, ssd_d, ssd_gnorm_w, ssd_out_w, sb_in_w, sb_qn_w, sb_kn_w, sb_out_w, ple_norm_w, ple_gate_w, ple_proj_w, loss_target, m_norm_w, m_ssd_in_w, m_ssd_conv_w, m_ssd_conv_b, m_ssd_dt_bias, m_ssd_a_log, m_ssd_d, m_ssd_gnorm_w, m_ssd_out_w, m_sb_in_w, m_sb_qn_w, m_sb_kn_w, m_sb_out_w, m_ple_norm_w, m_ple_gate_w, m_ple_proj_w, v_norm_w, v_ssd_in_w, v_ssd_conv_w, v_ssd_conv_b, v_ssd_dt_bias, v_ssd_a_log, v_ssd_d, v_ssd_gnorm_w, v_ssd_out_w, v_sb_in_w, v_sb_qn_w, v_sb_kn_w, v_sb_out_w, v_ple_norm_w, v_ple_gate_w, v_ple_proj_w):
    env = dict(locals())
    wts = {n: env[n] for n in WEIGHT_NAMES}
    mom1 = {n: env["m_" + n] for n in WEIGHT_NAMES}
    mom2 = {n: env["v_" + n] for n in WEIGHT_NAMES}

    s, d = x.shape[1], x.shape[2]
    depth = norm_w.shape[0]
    n_ssd, n_sb = ssd_in_w.shape[0], sb_in_w.shape[0]
    di = ssd_out_w.shape[1] * N_DEV
    n_heads = ssd_dt_bias.shape[1]
    hpg = n_heads // SSD_N_GROUPS
    nbc = SSD_N_GROUPS * SSD_D_STATE
    in_segs = (di, di, nbc, nbc, n_heads)
    conv_segs = (di, nbc, nbc)
    sb_w = sb_out_w.shape[1] * N_DEV
    selectors = ssd_selectors(hpg)
    xi, yi, ci = _mesh_pos()
    core = ci.astype(jnp.int32).reshape(1)
    chip = (2 * xi + yi).astype(jnp.int32).reshape(1)

    def layer_keys(i):
        j = i // 2
        mixer = [("ssd_in_w", j), ("ssd_conv_w", j), ("ssd_out_w", j)] if i % 2 == 0 else [("sb_in_w", j), ("sb_out_w", j)]
        return mixer + [("ple_gate_w", i), ("ple_proj_w", i)]

    me_block = 4 * xi + 2 * yi + ci

    def landing_zone(t):
        return lax.dynamic_update_index_in_dim(lax.empty((N_DEV,) + t.shape, t.dtype), t, me_block, 0)

    def groups(i):
        keys = layer_keys(i)
        return [keys[:2], keys[2:]] if i == 0 else [keys]

    gathers, prev = {}, []
    for i in range(depth):
        for q, keys in enumerate(groups(i)):
            shards = [wts[n][idx] for n, idx in keys]
            if prev:
                shards = lax.optimization_barrier((prev[0], shards))[1]
            lands = [landing_zone(t if n == "ssd_conv_w" else t.astype(BF16)) for (n, _), t in zip(keys, shards)]
            gathers[i, q] = ici_start("gather", [], lands, after=prev, name=f"ag{i}{'ab'[q]}_start")
            prev = [gathers[i, q][4]]
    all_started = prev[0]
    full, ssd_full, passing = {}, {}, {}

    def hand_over(i, q, after):
        _, lands = ici_wait("gather", gathers[i, q], after, name=f"ag{i}{'ab'[q]}_wait")
        passing[i, q] = ici_start("pass", [], [t.reshape(N_CHIP, 2, *t.shape[1:]) for t in lands],
                                  name=f"ag{i}{'ab'[q]}_pass_start")

    def arrive(i, q, after):
        _, lands = ici_wait("pass", passing[i, q], after, name=f"ag{i}{'ab'[q]}_pass_wait")
        for k, t in zip(groups(i)[q], lands):
            full[k] = t.reshape(N_DEV, *t.shape[2:])

    def w_out_of(i):
        return full["ssd_out_w", i // 2].reshape(di, d) if i % 2 == 0 else full["sb_out_w", i // 2].reshape(sb_w, d)

    h = x.reshape(s, d)
    saved = []
    hand_over(0, 0, [all_started])
    arrive(0, 0, [all_started])
    for i in range(depth):
        j = i // 2
        if i > 0:
            arrive(i, 0, [h])
        sv = dict(h_in=h)
        u = rmsnorm_fwd(h, norm_w[i], name=f"l{i}_norm")
        sv["u"] = u
        if i % 2 == 0:
            fw = ssd_full[j] = dict(
                w_in=_split_cols(_from_col_blocks(full["ssd_in_w", j]), in_segs),
                conv_w=_split_cols(_from_col_blocks(full["ssd_conv_w", j]), conv_segs),
                conv_b=_split_cols(ssd_conv_b[j].reshape(1, -1), conv_segs))
            raw = [matmul(u, wseg, name=f"l{i}_in{q}") for q, wseg in enumerate(fw["w_in"])]
            if i == 0:
                hand_over(0, 1, [raw[4]])
            z, dt_raw = raw[0], raw[4]
            act = [ssd_conv_fwd(raw[1 + q], fw["conv_w"][q], fw["conv_b"][q], name=f"l{i}_conv{q}") for q in range(3)]
            dt, a_neg = ssd_dt_fwd(dt_raw, ssd_dt_bias[j], ssd_a_log[j], name=f"l{i}_dt")
            dtp = _to_group_lanes(dt, hpg)
            a_g = _head_vec(a_neg.reshape(-1), hpg)
            d_x = jnp.repeat(ssd_d[j].reshape(SSD_N_GROUPS, 1, hpg), SSD_HEAD_DIM, axis=2)
            y, states = ssd_scan_fwd(act[0], act[1], act[2], dtp, a_g, d_x, selectors, heads_per_group=hpg,
                                     name=f"l{i}_scan")
            yn = ssd_gate_fwd(y, z, ssd_gnorm_w[j], name=f"l{i}_gate")
            if i == 0:
                arrive(0, 1, [yn])
            h1 = matmul(yn, w_out_of(i), res=h, name=f"l{i}_out")
            sv.update(raw=raw, act=act, dt=dt, dtp=dtp, a_g=a_g, d_x=d_x, y=y, states=states, yn=yn)
        else:
            proj = matmul(u, full["sb_in_w", j], name=f"l{i}_in")
            qn, kn = sb_qk_fwd(proj, sb_qn_w[j], sb_kn_w[j], name=f"l{i}_qknorm")
            v_off = 2 * sb_w // SB_HEAD_DIM
            o, tot = sb_attn_fwd(qn, kn, proj, v_off=v_off, name=f"l{i}_attn")
            og = sb_gate_fwd(o, proj, name=f"l{i}_gate")
            h1 = matmul(og, w_out_of(i), res=h, name=f"l{i}_out")
            sv.update(proj=proj, qn=qn, kn=kn, o=o, tot=tot, og=og, v_off=v_off)
        if i + 1 < depth:
            hand_over(i + 1, 0, [h1])
        t = rmsnorm_fwd(h1, ple_norm_w[i], passing[i + 1, 0][4] if i + 1 < depth else None, name=f"l{i}_plenorm")
        gate_pre = matmul(t, full["ple_gate_w", i].reshape(d, d), name=f"l{i}_plegate")
        pp = matmul(p[i, 0], full["ple_proj_w", i], name=f"l{i}_pleproj")
        h = ple_fwd(h1, gate_pre, pp, name=f"l{i}_ple")
        sv.update(h1=h1, t=t, gate_pre=gate_pre, pp=pp)
        saved.append(sv)

    loss_part, dh = loss_head(h, loss_target.reshape(s, d), name="loss_head")
    loss = lax.psum(loss_part[0, 0], ("x", "y", "c"))

    big = {}
    small = {n: [None] * wts[n].shape[0] for n in REPLICATED}
    swaps, scatters = {}, {}
    order_after = jnp.zeros(TOKEN_SHAPE, F32)
    pending = None

    def send_to_sibling(i, q):
        blocks = [big[k].reshape(N_CHIP, 2, *big[k].shape[1:]) for k in groups(i)[::-1][q]]
        swaps[i, q] = ici_start("swap", blocks, [lax.empty((N_CHIP,) + t.shape[2:], t.dtype) for t in blocks],
                                name=f"rs{i}{'ab'[q]}_swap_start")
        return swaps[i, q][4]

    def send_to_chips(i, q, after):
        blocks, from_sibling = ici_wait("swap", swaps[i, q], after, name=f"rs{i}{'ab'[q]}_swap_wait")
        sums = [pair_add(g, r1, core, name=f"rs{i}{'ab'[q]}_pair_add{a}")
                for a, (g, r1) in enumerate(zip(blocks, from_sibling))]
        scatters[i, q] = ici_start("scatter", sums, [lax.empty((N_PEER_CHIPS,) + t.shape[1:], t.dtype) for t in sums],
                                   name=f"rs{i}{'ab'[q]}_start")
        return scatters[i, q][4]

    for i in reversed(range(depth)):
        j = i // 2
        sv = saved[i]
        dpp, dgp = ple_bwd(dh, sv["gate_pre"], sv["pp"], order_after, name=f"b{i}_ple")
        big["ple_proj_w", i] = matmul(p[i, 0], dpp, mode="tn", out_dtype=BF16, out_blocks=ple_proj_w.shape[2],
                                      name=f"b{i}_pleproj_w")
        big["ple_gate_w", i] = matmul(sv["t"], dgp, mode="tn", out_dtype=BF16, name=f"b{i}_plegate_w").reshape(N_DEV, -1, d)
        dt_ = matmul(dgp, full["ple_gate_w", i].reshape(d, d), mode="nt", name=f"b{i}_plegate_x")
        dh1, g_pn = rmsnorm_bwd(sv["h1"], ple_norm_w[i], dt_, dh, name=f"b{i}_plenorm")
        small["ple_norm_w"][i] = g_pn
        behind = send_to_chips(*pending, [dh1]) if pending is not None else None
        pending = None
        u = sv["u"]
        if i % 2 == 0:
            fw = ssd_full[j]
            raw, act = sv["raw"], sv["act"]
            big["ssd_out_w", j] = matmul(sv["yn"], dh1, mode="tn", out_dtype=BF16, name=f"b{i}_out_w").reshape(N_DEV, -1, d)
            if i == 0:
                send_to_sibling(0, 0)
            dyn = matmul(dh1, w_out_of(i), mode="nt", after=behind, name=f"b{i}_out_x")
            dy, dz, g_gn = ssd_gate_bwd(sv["y"], raw[0], ssd_gnorm_w[j], dyn, name=f"b{i}_gate")
            dxs, dbm, dcm, ddtp, dadtp, dd_g = ssd_scan_bwd(act[0], act[1], act[2], sv["dtp"], sv["a_g"], sv["d_x"], selectors,
                                                          sv["states"], dy, heads_per_group=hpg, name=f"b{i}_scan")
            behind = send_to_chips(0, 0, [dxs]) if i == 0 else None
            ddt_raw, g_dtb, g_alog = ssd_dt_bwd(raw[4], ssd_dt_bias[j], ssd_a_log[j], sv["dt"],
                                                _from_group_lanes(ddtp, hpg), _from_group_lanes(dadtp, hpg),
                                                name=f"b{i}_dt")
            conv_back = [ssd_conv_bwd(raw[1 + q], fw["conv_w"][q], fw["conv_b"][q], dact, name=f"b{i}_conv{q}")
                         for q, dact in enumerate((dxs, dbm, dcm))]
            dsegs = [dz] + [cb[0] for cb in conv_back] + [ddt_raw]
            g_in = jnp.concatenate([matmul(u, ds, mode="tn", out_dtype=BF16, name=f"b{i}_in{q}_w")
                                    for q, ds in enumerate(dsegs)], axis=1)
            big["ssd_in_w", j] = _col_blocks(g_in)
            big["ssd_conv_w", j] = _col_blocks(jnp.concatenate([cb[1] for cb in conv_back], axis=1))
            du = None
            for q, (ds, wseg) in enumerate(zip(dsegs, fw["w_in"])):
                du = matmul(ds, wseg, mode="nt", res=du, after=behind if q == 0 else None, name=f"b{i}_in{q}_x")
            small["ssd_conv_b"][j] = jnp.concatenate([cb[2] for cb in conv_back], axis=1)
            small["ssd_dt_bias"][j] = g_dtb
            small["ssd_a_log"][j] = g_alog
            small["ssd_d"][j] = dd_g[:, 0, :hpg]
            small["ssd_gnorm_w"][j] = g_gn
        else:
            proj = sv["proj"]
            big["sb_out_w", j] = matmul(sv["og"], dh1, mode="tn", out_dtype=BF16, name=f"b{i}_out_w").reshape(N_DEV, -1, d)
            dog = matmul(dh1, w_out_of(i), mode="nt", after=behind, name=f"b{i}_out_x")
            do, dg = sb_gate_bwd(dog, sv["o"], proj, name=f"b{i}_gate")
            dqn, dkn, dv = sb_attn_bwd(sv["qn"], sv["kn"], proj, sv["tot"], do, v_off=sv["v_off"], name=f"b{i}_attn")
            dproj, g_qn, g_kn = sb_pack_bwd(proj, sb_qn_w[j], sb_kn_w[j], dqn, dkn, dv, dg, name=f"b{i}_qknorm")
            big["sb_in_w", j] = matmul(u, dproj, mode="tn", out_dtype=BF16, out_blocks=sb_in_w.shape[2], name=f"b{i}_in_w")
            du = matmul(dproj, full["sb_in_w", j], mode="nt", name=f"b{i}_in_x")
            small["sb_qn_w"][j] = g_qn
            small["sb_kn_w"][j] = g_kn
        dh, g_n = rmsnorm_bwd(sv["h_in"], norm_w[i], du, dh1, name=f"b{i}_norm")
        small["norm_w"][i] = g_n
        pending = (i, len(groups(i)) - 1)
        order_after = send_to_sibling(*pending)
    send_to_chips(*pending, [dh])
    grad_x = dh.reshape(x.shape)

    rep_like = [wts[n] for n in REPLICATED]
    g_packed = all_reduce_small(_pack([jnp.stack([t.reshape(-1) for t in small[n]]) for n in REPLICATED]),
                                name="all_reduce_small_grads")
    d_packed, m_packed, v_packed = adamw_replicated(
        _pack(rep_like), _pack([mom1[n] for n in REPLICATED]), _pack([mom2[n] for n in REPLICATED]), g_packed,
        name="adamw_replicated")
    grads = dict(zip(REPLICATED, _unpack(g_packed, rep_like)))
    deltas = dict(zip(REPLICATED, _unpack(d_packed, rep_like)))
    new_m = dict(zip(REPLICATED, _unpack(m_packed, rep_like)))
    new_v = dict(zip(REPLICATED, _unpack(v_packed, rep_like)))

    updated = {}
    after = [scatters[0, len(groups(0)) - 1][4]]
    for i in reversed(range(depth)):
        for q, keys in enumerate(groups(i)[::-1]):
            sums, received = ici_wait("scatter", scatters[i, q], after, name=f"rs{i}{'ab'[q]}_wait")
            after = []
            for (n, idx), t_sum, recv in zip(keys, sums, received):
                updated[n], done = adamw_sharded(wts[n], mom1[n], mom2[n], idx, t_sum, recv, chip, updated.get(n),
                                                 name=f"adamw_{n}{idx}")
                after.append(done)
    for n, (g_n, d_n, m_n, v_n) in updated.items():
        grads[n], deltas[n], new_m[n], new_v[n] = g_n, d_n, m_n, v_n

    return (loss, grad_x, *[grads[n] for n in WEIGHT_NAMES], *[deltas[n] for n in WEIGHT_NAMES],
            *[new_m[n] for n in WEIGHT_NAMES], *[new_v[n] for n in WEIGHT_NAMES])
```

```python
import functools
import math

import jax
import jax.numpy as jnp
from jax import lax
from jax.experimental import pallas as pl
from jax.experimental.pallas import tpu as pltpu

F32 = jnp.float32
BF16 = jnp.bfloat16
MESH = pl.DeviceIdType.MESH

N_DEV = 8
N_CHIP = 4
LANES = 128
VMEM_LIMIT_BYTES = 56 * 1024 * 1024
MATMUL_TILE_BYTES = 36 * 1024 * 1024

NORM_EPS = 1e-6
GATED_NORM_EPS = 1e-5
SSD_HEAD_DIM = 64
SSD_N_GROUPS = 8
SSD_D_STATE = 128
SSD_D_CONV = 4
SSD_CHUNK = 128
SSD_FWD_CHUNKS_PER_STEP = 8
SSD_BWD_CHUNKS_PER_STEP = 4
SB_HEAD_DIM = 128
PLE_DIM = 256

ADAM_LR = 0.001
ADAM_B1 = 0.9
ADAM_B2 = 0.999
ADAM_EPS = 1e-08
ADAM_WD = 0.01
ADAM_STEP = 10


def _cparams(sem=None, **kw):
    return pltpu.CompilerParams(dimension_semantics=sem, vmem_limit_bytes=VMEM_LIMIT_BYTES, **kw)


def _pick(dim, prefs):
    for t in prefs:
        if dim % t == 0:
            return t
    return dim


def _sigmoid(x):
    return 1.0 / (1.0 + jnp.exp(-x))


def _silu(x):
    return x * _sigmoid(x)


def _silu_grad(x):
    s = _sigmoid(x)
    return s * (1.0 + x * (1.0 - s))


def matmul(a, b, *, mode="nn", out_dtype=F32, res=None, out_blocks=None, after=None, name):
    b_blocked = b.ndim == 3
    if mode == "nn":
        m, kc = a.shape
        n = b.shape[-1] * (N_DEV if b_blocked else 1)
    elif mode == "nt":
        m, kc = a.shape
        n = b.shape[-2]
    else:
        kc, m = a.shape
        n = b.shape[-1]
    nb = b.shape[-1] if b_blocked else None
    tn = _pick(n if not out_blocks else out_blocks, (512, 256, 128))
    if b_blocked and mode == "nn":
        tn = _pick(nb, (512, 256, 128))
    k_unit = nb if (b_blocked and mode == "nt") else 1
    tm, tk = None, None
    for tm_try in (1024, 512, 256, 128):
        if m % tm_try:
            continue
        for tk_try in (kc, kc // 2, kc // 4, 2048, 1024, 512, 256, 128):
            if tk_try > kc or tk_try < k_unit or kc % tk_try or tk_try % k_unit:
                continue
            tiles = 2 * (tm_try * tk_try * a.dtype.itemsize + tk_try * tn * b.dtype.itemsize)
            tiles += tm_try * tn * (2 * jnp.dtype(out_dtype).itemsize + 4 + (8 if res is not None else 0))
            if tiles <= MATMUL_TILE_BYTES:
                tm, tk = tm_try, tk_try
                break
        if tm:
            break
    if tm is None:
        tm, tk = m, max(k_unit, LANES if kc % LANES == 0 else kc)
    nk = kc // tk
    grid = (m // tm, n // tn, nk)

    if mode == "tn":
        a_spec = pl.BlockSpec((tk, tm), lambda i, j, k: (k, i))
        dims = (((0,), (0,)), ((), ()))
    else:
        a_spec = pl.BlockSpec((tm, tk), lambda i, j, k: (i, k))
        dims = (((1,), (0,)), ((), ())) if mode == "nn" else (((1,), (1,)), ((), ()))
    if mode == "nt":
        if b_blocked:
            b_spec = pl.BlockSpec((tk // nb, tn, nb), lambda i, j, k: (k, j, 0))
        else:
            b_spec = pl.BlockSpec((tn, tk), lambda i, j, k: (j, k))
    else:
        if b_blocked:
            per = nb // tn
            b_spec = pl.BlockSpec((None, tk, tn), lambda i, j, k: (j // per, k, j % per))
        else:
            b_spec = pl.BlockSpec((tk, tn), lambda i, j, k: (k, j))
    if out_blocks:
        per_o = out_blocks // tn
        out_shape = jax.ShapeDtypeStruct((n // out_blocks, m, out_blocks), out_dtype)
        out_spec = pl.BlockSpec((None, tm, tn), lambda i, j, k: (j // per_o, i, j % per_o))
    else:
        out_shape = jax.ShapeDtypeStruct((m, n), out_dtype)
        out_spec = pl.BlockSpec((tm, tn), lambda i, j, k: (i, j))
    in_specs = [a_spec, b_spec]
    args = [a, b]
    if res is not None:
        in_specs.append(pl.BlockSpec((tm, tn), lambda i, j, k: (i, j)))
        args.append(res)
    if after is not None:
        in_specs.append(pl.BlockSpec(memory_space=pl.ANY))
        args.append(after)
    n_in = len(args)

    def body(*refs):
        a_ref, b_ref = refs[:2]
        r_ref = refs[2] if res is not None else None
        o_ref = refs[n_in]

        def finish(r):
            if res is not None:
                r = r + r_ref[...].astype(F32)
            o_ref[...] = r.astype(out_dtype)

        if b_blocked and mode == "nt":
            part = None
            for blk in range(tk // nb):
                term = lax.dot_general(a_ref[:, blk * nb:(blk + 1) * nb].astype(BF16), b_ref[blk].astype(BF16), dims,
                                       preferred_element_type=F32)
                part = term if part is None else part + term
        else:
            part = lax.dot_general(a_ref[...].astype(BF16), b_ref[...].astype(BF16), dims, preferred_element_type=F32)
        if nk == 1:
            finish(part)
            return
        acc_ref = refs[-1]
        k = pl.program_id(2)

        @pl.when(k == 0)
        def _():
            acc_ref[...] = part

        @pl.when(k > 0)
        def _():
            acc_ref[...] += part

        @pl.when(k == nk - 1)
        def _():
            finish(acc_ref[...])

    return pl.pallas_call(
        body, out_shape=out_shape, grid=grid, in_specs=in_specs, out_specs=out_spec,
        scratch_shapes=[] if nk == 1 else [pltpu.VMEM((tm, tn), F32)], name=name,
        compiler_params=_cparams(("parallel", "parallel", "arbitrary")),
    )(*args)


def _dot(a, b, dims, precision=None):
    return lax.dot_general(a, b, (dims, ((), ())), preferred_element_type=F32, precision=precision)


_NN = ((1,), (0,))
_NT = ((1,), (1,))
_TN = ((0,), (0,))
_EXACT = lax.Precision.HIGHEST


def _chunk_decay_terms(dt, a):
    ln = dt.shape[0]
    row = lax.broadcasted_iota(jnp.int32, (ln, ln), 0)
    col = lax.broadcasted_iota(jnp.int32, (ln, ln), 1)
    tri = (row >= col).astype(F32)
    a_col = _dot(tri, dt * a, _NN, _EXACT)
    return a_col, a_col.T, row >= col


def _exact_dot(x, sel, terms):
    t = x.shape[0]
    parts, rest = [], x
    for k in range(terms):
        piece = rest.astype(BF16)
        parts.append(piece)
        if k + 1 < terms:
            rest = rest - piece.astype(F32)
    r = _dot(jnp.concatenate(parts, axis=0), sel, _NN)
    out = r[:t]
    for k in range(1, terms):
        out = out + r[k * t:(k + 1) * t]
    return out


def ssd_selectors(r_n):
    lane = jnp.arange(LANES)
    spread64 = (lane[:, None] == jnp.arange(r_n * SSD_HEAD_DIM)[None, :] // SSD_HEAD_DIM).astype(BF16)
    pair_sum = jnp.stack([lane[None, :] == 2 * q + lane[:, None] // SSD_HEAD_DIM for q in range(r_n // 2)]).astype(BF16)
    row_sum = jnp.stack([jnp.broadcast_to(lane[None, :] == r, (LANES, LANES)) for r in range(r_n)]).astype(BF16)
    return spread64, pair_sum, row_sum


def _ssd_chunk_setup(dt, a, spread64):
    ln = dt.shape[0]
    a_col, a_row, causal = _chunk_decay_terms(dt, a)
    ea = jnp.exp(a_col)
    te = jnp.exp(a_col[ln - 1:ln, :] - a_col)
    return (a_row, a_col, _exact_dot(dt, spread64, 2), _exact_dot(ea, spread64, 2), _exact_dot(te, spread64, 2),
            ea, causal)


def ssd_scan_fwd(xs, bm, cm, dtp, a_g, d_x, selectors, *, heads_per_group, name):
    s, di = xs.shape
    g_n = SSD_N_GROUPS
    r_n, p_n, n_n, ln = heads_per_group, SSD_HEAD_DIM, SSD_D_STATE, SSD_CHUNK
    nc = s // ln
    cps = _pick(nc, (SSD_FWD_CHUNKS_PER_STEP, 4, 2, 1))
    pairs, pw = r_n // 2, 2 * p_n
    spread64 = selectors[0]

    def body(xs_ref, bm_ref, cm_ref, dt_ref, a_ref, d_ref, s64_ref, y_ref, st_ref, state):
        c = pl.program_id(1)

        @pl.when(c == 0)
        def _():
            state[...] = jnp.zeros_like(state)

        first_head = lax.broadcasted_iota(jnp.int32, (1, pw), 1) < p_n
        for sub in range(cps):
            rows = slice(sub * ln, (sub + 1) * ln)
            a_row, a_col, dt_x, ea_x, te_x, _, causal = _ssd_chunk_setup(dt_ref[rows, :], a_ref[...], s64_ref[...])
            bm_f = bm_ref[rows, :]
            bmb = bm_f.astype(BF16)
            bm_t = bm_f.T.astype(BF16)
            cmb = cm_ref[rows, :].astype(BF16)
            scores = _dot(cmb, bmb, _NT)
            for q in range(pairs):
                sl = slice(q * pw, (q + 1) * pw)
                x2 = xs_ref[rows, sl]
                xdt2 = x2 * dt_x[:, sl]
                xdt2b = xdt2.astype(BF16)
                y_heads = []
                for r in (2 * q, 2 * q + 1):
                    decay = jnp.exp(jnp.where(causal, a_col[:, r:r + 1] - a_row[r:r + 1, :], -jnp.inf))
                    y_heads.append(_dot((scores * decay).astype(BF16), xdt2b, _NN))
                s2t = state[q]
                st_ref[sub, q] = s2t
                y2 = jnp.where(first_head, y_heads[0], y_heads[1])
                y2 = y2 + ea_x[:, sl] * _dot(cmb, s2t.astype(BF16), _NN)
                y_ref[rows, sl] = y2 + d_ref[:, sl] * x2
                state[q] = s2t * ea_x[ln - 1:ln, sl] + _dot(bm_t, (xdt2 * te_x[:, sl]).astype(BF16), _NN)

    whole = lambda t: pl.BlockSpec(t.shape, lambda g, c: (0,) * t.ndim)
    step = cps * ln
    return pl.pallas_call(
        body,
        out_shape=(jax.ShapeDtypeStruct((s, di), F32),
                   jax.ShapeDtypeStruct((nc, g_n * pairs, n_n, pw), F32)),
        grid=(g_n, nc // cps),
        in_specs=[pl.BlockSpec((step, r_n * p_n), lambda g, c: (c, g)),
                  pl.BlockSpec((step, n_n), lambda g, c: (c, g)),
                  pl.BlockSpec((step, n_n), lambda g, c: (c, g)),
                  pl.BlockSpec((None, step, LANES), lambda g, c: (g, c, 0)),
                  pl.BlockSpec((None, 1, LANES), lambda g, c: (g, 0, 0)),
                  pl.BlockSpec((None, 1, r_n * p_n), lambda g, c: (g, 0, 0)),
                  whole(spread64)],
        out_specs=(pl.BlockSpec((step, r_n * p_n), lambda g, c: (c, g)),
                   pl.BlockSpec((cps, pairs, n_n, pw), lambda g, c: (c, g, 0, 0))),
        scratch_shapes=[pltpu.VMEM((pairs, n_n, pw), F32)],
        name=name, compiler_params=_cparams(("parallel", "arbitrary")),
    )(xs, bm, cm, dtp, a_g, d_x, spread64)


def _row8(v):
    return jnp.broadcast_to(v, (8, v.shape[1]))


def ssd_scan_bwd(xs, bm, cm, dtp, a_g, d_x, selectors, states, dy, *, heads_per_group, name):
    s, di = xs.shape
    g_n = SSD_N_GROUPS
    r_n, p_n, n_n, ln = heads_per_group, SSD_HEAD_DIM, SSD_D_STATE, SSD_CHUNK
    nc = s // ln
    cps = _pick(nc, (SSD_BWD_CHUNKS_PER_STEP, 2, 1))
    pairs, pw = r_n // 2, 2 * p_n
    spread64, pair_sum, row_sum = selectors

    def body(xs_ref, bm_ref, cm_ref, dt_ref, a_ref, d_ref, s64_ref, ps_ref, rs_ref, st_ref, dy_ref,
             dxs_ref, dbm_ref, dcm_ref, ddt_ref, dadt_ref, dd_ref, dstate, da_rows):
        c = pl.program_id(1)

        @pl.when(c == 0)
        def _():
            dstate[...] = jnp.zeros_like(dstate)
            dd_ref[...] = jnp.zeros_like(dd_ref)

        row = lax.broadcasted_iota(jnp.int32, (ln, ln), 0)
        col = lax.broadcasted_iota(jnp.int32, (ln, ln), 1)
        causal_t = col >= row
        upper = causal_t.astype(F32)
        first_head = lax.broadcasted_iota(jnp.int32, (1, pw), 1) < p_n
        for sub in reversed(range(cps)):
            rows = slice(sub * ln, (sub + 1) * ln)
            a_row, a_col, dt_x, ea_x, te_x, ea, causal = _ssd_chunk_setup(dt_ref[rows, :], a_ref[...], s64_ref[...])
            bmb = bm_ref[rows, :].astype(BF16)
            cm_f = cm_ref[rows, :]
            cmb = cm_f.astype(BF16)
            cm_t = cm_f.T.astype(BF16)
            scores = _dot(cmb, bmb, _NT)
            scores_t = _dot(bmb, cmb, _NT)
            e_last = ea[ln - 1:ln, :]
            da_rows[...] = jnp.zeros_like(da_rows)
            dscores = jnp.zeros((ln, ln), F32)
            dcm = jnp.zeros((ln, n_n), F32)
            dbm = jnp.zeros((ln, n_n), F32)
            da_cols = jnp.zeros((ln, LANES), F32)
            da_last = jnp.zeros((1, LANES), F32)
            ddt = jnp.zeros((ln, LANES), F32)
            dd = jnp.zeros((1, LANES), F32)
            for q in range(pairs):
                sl = slice(q * pw, (q + 1) * pw)
                sum2 = ps_ref[q]
                x2 = xs_ref[rows, sl]
                dt2 = dt_x[:, sl]
                xdt2 = x2 * dt2
                xdt2b = xdt2.astype(BF16)
                dy2 = dy_ref[rows, sl]
                dy2b = dy2.astype(BF16)
                dxdt_heads = []
                for h, r in enumerate((2 * q, 2 * q + 1)):
                    a_r = jnp.broadcast_to(a_col[:, r:r + 1], (ln, ln))
                    decay = jnp.exp(jnp.where(causal, a_r - a_row[r:r + 1, :], -jnp.inf))
                    decay_t = jnp.exp(jnp.where(causal_t, a_row[r:r + 1, :] - a_r, -jnp.inf))
                    dy_h = jnp.where(first_head if h == 0 else jnp.logical_not(first_head), dy2, 0.0).astype(BF16)
                    dm = _dot(dy_h, xdt2b, _NT)
                    dscores = dscores + dm * decay
                    e_mat = dm * (scores * decay)
                    da_cols = da_cols + _exact_dot(e_mat, rs_ref[r], 2)
                    da_rows[r:r + 1, :] = -jnp.sum(e_mat, axis=0, keepdims=True)
                    dxdt_heads.append(_dot((scores_t * decay_t).astype(BF16), dy2b, _NN))
                dxdt2 = jnp.where(first_head, dxdt_heads[0], dxdt_heads[1])
                s2t = st_ref[sub, q]
                s2tb = s2t.astype(BF16)
                ds2t = dstate[q]
                ds2tb = ds2t.astype(BF16)
                ea2, te2 = ea_x[:, sl], te_x[:, sl]
                y_off2 = ea2 * _dot(cmb, s2tb, _NN)
                dy_e2 = (dy2 * ea2).astype(BF16)
                dcm = dcm + _dot(dy_e2, s2tb, _NT)
                ds_in = _dot(cm_t, dy_e2, _NN)
                da_cols = da_cols + _exact_dot(dy2 * y_off2, sum2, 2)
                bds2 = _dot(bmb, ds2tb, _NN)
                dxdt2 = dxdt2 + te2 * bds2
                xdt_e2 = xdt2 * te2
                dbm = dbm + _dot(xdt_e2.astype(BF16), ds2tb, _NT)
                w_cols = _exact_dot(xdt_e2 * bds2, sum2, 2)
                da_cols = da_cols - w_cols
                state_dot = _exact_dot(_row8(jnp.sum(ds2t * s2t, axis=0, keepdims=True)), sum2, 2)[0:1]
                da_last = da_last + jnp.sum(w_cols, axis=0, keepdims=True) + e_last * state_dot
                dstate[q] = ds2t * ea_x[ln - 1:ln, sl] + ds_in
                dxs_ref[rows, sl] = dxdt2 * dt2 + d_ref[:, sl] * dy2
                ddt = ddt + _exact_dot(dxdt2 * x2, sum2, 2)
                dd = dd + _exact_dot(_row8(jnp.sum(dy2 * x2, axis=0, keepdims=True)), sum2, 2)[0:1]
            dcm_ref[rows, :] = dcm + _dot(dscores.astype(BF16), bmb, _NN)
            dbm_ref[rows, :] = dbm + _dot(dscores.T.astype(BF16), cmb, _NN)
            da_total = da_cols + da_rows[...].T
            dadt_ref[rows, :] = _dot(upper, da_total, _NN, _EXACT) + da_last
            ddt_ref[rows, :] = ddt
            dd_ref[...] += dd

    step, last_c = cps * ln, nc // cps - 1
    whole = lambda t: pl.BlockSpec(t.shape, lambda g, c: (0,) * t.ndim)
    return pl.pallas_call(
        body,
        out_shape=(jax.ShapeDtypeStruct((s, di), F32),
                   jax.ShapeDtypeStruct(bm.shape, F32),
                   jax.ShapeDtypeStruct(cm.shape, F32),
                   jax.ShapeDtypeStruct(dtp.shape, F32),
                   jax.ShapeDtypeStruct(dtp.shape, F32),
                   jax.ShapeDtypeStruct(a_g.shape, F32)),
        grid=(g_n, nc // cps),
        in_specs=[pl.BlockSpec((step, r_n * p_n), lambda g, c: (last_c - c, g)),
                  pl.BlockSpec((step, n_n), lambda g, c: (last_c - c, g)),
                  pl.BlockSpec((step, n_n), lambda g, c: (last_c - c, g)),
                  pl.BlockSpec((None, step, LANES), lambda g, c: (g, last_c - c, 0)),
                  pl.BlockSpec((None, 1, LANES), lambda g, c: (g, 0, 0)),
                  pl.BlockSpec((None, 1, r_n * p_n), lambda g, c: (g, 0, 0)),
                  whole(spread64), whole(pair_sum), whole(row_sum),
                  pl.BlockSpec((cps, pairs, n_n, pw), lambda g, c: (last_c - c, g, 0, 0)),
                  pl.BlockSpec((step, r_n * p_n), lambda g, c: (last_c - c, g))],
        out_specs=(pl.BlockSpec((step, r_n * p_n), lambda g, c: (last_c - c, g)),
                   pl.BlockSpec((step, n_n), lambda g, c: (last_c - c, g)),
                   pl.BlockSpec((step, n_n), lambda g, c: (last_c - c, g)),
                   pl.BlockSpec((None, step, LANES), lambda g, c: (g, last_c - c, 0)),
                   pl.BlockSpec((None, step, LANES), lambda g, c: (g, last_c - c, 0)),
                   pl.BlockSpec((None, 1, LANES), lambda g, c: (g, 0, 0))),
        scratch_shapes=[pltpu.VMEM((pairs, n_n, pw), F32), pltpu.VMEM((LANES, ln), F32)],
        name=name, compiler_params=_cparams(("parallel", "arbitrary")),
    )(xs, bm, cm, dtp, a_g, d_x, spread64, pair_sum, row_sum, states, dy)


SB_Q_TILE = 2048
SB_K_TILE = 256


def _tri_sum(x, tri):
    t = x.shape[0]
    hi = x.astype(BF16)
    r1 = x - hi.astype(F32)
    mid = r1.astype(BF16)
    lo = (r1 - mid.astype(F32)).astype(BF16)
    r = _dot(jnp.concatenate([hi, mid, lo], axis=0), tri, _NN)
    return r[:t] + r[t:2 * t] + r[2 * t:]


def _sb_logits(q, k_j, scale, strict):
    z = _dot(q, k_j, _NT) * scale
    sp = jnp.log(1.0 + jnp.exp(-jnp.abs(z)))
    log_b = jnp.minimum(z, 0.0) - sp
    log_1mb = log_b - z
    if strict is not None:
        log_1mb = jnp.where(strict, log_1mb, 0.0)
    return log_b, log_1mb


def _sb_tiles(s):
    tq = _pick(s, (SB_Q_TILE, 2 * SB_K_TILE, SB_K_TILE, LANES))
    return tq, min(tq, SB_K_TILE)


def _sb_diag_mask(rows, tk):
    return lax.broadcasted_iota(jnp.int32, (rows, tk), 1) < lax.broadcasted_iota(jnp.int32, (rows, tk), 0)


def _sb_iotas(t):
    row = lax.broadcasted_iota(jnp.int32, (t, t), 0)
    col = lax.broadcasted_iota(jnp.int32, (t, t), 1)
    return row, col


def sb_attn_fwd(qn, kn, v, *, v_off=0, name):
    s, w = qn.shape
    dh = SB_HEAD_DIM
    n_h = w // dh
    tq, tk = _sb_tiles(s)
    per = tq // tk
    scale = 1.0 / math.sqrt(dh)

    def body(q_ref, k_ref, v_ref, o_ref, tot_ref):
        i = pl.program_id(1)
        q = q_ref[...]
        row, col = _sb_iotas(tk)
        later = (row > col).astype(BF16)

        def tile(q_rows, j, acc, run, mask):
            s0 = pl.multiple_of(j * tk, tk)
            k_j = k_ref[pl.ds(s0, tk), :]
            v_j = v_ref[pl.ds(s0, tk), :].astype(BF16)
            log_b, log_1mb = _sb_logits(q_rows, k_j, scale, mask)
            att = jnp.exp(log_b + (_tri_sum(log_1mb, later) + run))
            if mask is not None:
                att = jnp.where(mask, att, 0.0)
            return acc + _dot(att.astype(BF16), v_j, _NN), run + jnp.sum(log_1mb, axis=1, keepdims=True)

        acc, run = jnp.zeros((tq, dh), F32), jnp.zeros((tq, 1), F32)
        for d in reversed(range(per)):
            r0 = d * tk
            a2, r2 = tile(q[r0:], i * per + d, acc[r0:], run[r0:], _sb_diag_mask(tq - r0, tk))
            acc = a2 if r0 == 0 else jnp.concatenate([acc[:r0], a2], axis=0)
            run = r2 if r0 == 0 else jnp.concatenate([run[:r0], r2], axis=0)

        def group(gg, c):
            for d in reversed(range(per)):
                c = tile(q, (i - 1 - gg) * per + d, c[0], c[1], None)
            return c

        acc, run = lax.fori_loop(0, i, group, (acc, run))
        o_ref[...] = acc
        tot_ref[...] = jnp.broadcast_to(run, (tq, dh))

    return pl.pallas_call(
        body,
        out_shape=(jax.ShapeDtypeStruct((s, w), F32), jax.ShapeDtypeStruct((s, w), F32)),
        grid=(n_h, s // tq),
        in_specs=[pl.BlockSpec((tq, dh), lambda h, i: (i, h)),
                  pl.BlockSpec((s, dh), lambda h, i: (0, h)),
                  pl.BlockSpec((s, dh), lambda h, i: (0, v_off + h))],
        out_specs=(pl.BlockSpec((tq, dh), lambda h, i: (i, h)),
                   pl.BlockSpec((tq, dh), lambda h, i: (i, h))),
        name=name, compiler_params=_cparams(("parallel", "parallel")),
    )(qn, kn, v)


def sb_attn_bwd(qn, kn, v, tot, do, *, v_off=0, name):
    s, w = qn.shape
    dh = SB_HEAD_DIM
    n_h = w // dh
    tq, tk = _sb_tiles(s)
    per = tq // tk
    scale = 1.0 / math.sqrt(dh)

    def body(q_ref, k_ref, v_ref, tot_ref, do_ref, dq_ref, dk_ref, dv_ref):
        dk_ref[...] = jnp.zeros_like(dk_ref)
        dv_ref[...] = jnp.zeros_like(dv_ref)
        row, col = _sb_iotas(tk)
        upto = (row <= col).astype(BF16)
        before = (row < col).astype(BF16)

        def q_block(i, _):
            t0 = pl.multiple_of(i * tq, tq)
            q = q_ref[pl.ds(t0, tq), :]
            do_i = do_ref[pl.ds(t0, tq), :].astype(BF16)
            total = tot_ref[pl.ds(t0, tq), 0:1]

            def tile(r0, j, dq, run_l, run_g, mask):
                s0 = pl.multiple_of(j * tk, tk)
                k_j = k_ref[pl.ds(s0, tk), :]
                v_j = v_ref[pl.ds(s0, tk), :].astype(BF16)
                q_r, do_r = q[r0:], do_i[r0:]
                log_b, log_1mb = _sb_logits(q_r, k_j, scale, mask)
                att = jnp.exp(log_b + ((total[r0:] - run_l) - _tri_sum(log_1mb, upto)))
                if mask is not None:
                    att = jnp.where(mask, att, 0.0)
                g = att * _dot(do_r, v_j, _NT)
                c = _tri_sum(g, before) + run_g
                dz = (g - (g + c) * jnp.exp(log_b)) * scale
                if mask is not None:
                    dz = jnp.where(mask, dz, 0.0)
                dz = dz.astype(BF16)
                dk_ref[pl.ds(s0, tk), :] += _dot(dz, q_r, _TN)
                dv_ref[pl.ds(s0, tk), :] += _dot(att.astype(BF16), do_r, _TN)
                return (dq + _dot(dz, k_j, _NN), run_l + jnp.sum(log_1mb, axis=1, keepdims=True),
                        run_g + jnp.sum(g, axis=1, keepdims=True))

            def group(gg, c):
                for d in range(per):
                    c = tile(0, gg * per + d, c[0], c[1], c[2], None)
                return c

            zero = jnp.zeros((tq, 1), F32)
            dq, run_l, run_g = lax.fori_loop(0, i, group, (jnp.zeros((tq, dh), F32), zero, zero))
            for d in range(per):
                r0 = d * tk
                p_dq, p_l, p_g = tile(r0, i * per + d, dq[r0:], run_l[r0:], run_g[r0:], _sb_diag_mask(tq - r0, tk))
                if r0 == 0:
                    dq, run_l, run_g = p_dq, p_l, p_g
                else:
                    dq = jnp.concatenate([dq[:r0], p_dq], axis=0)
                    run_l = jnp.concatenate([run_l[:r0], p_l], axis=0)
                    run_g = jnp.concatenate([run_g[:r0], p_g], axis=0)
            dq_ref[pl.ds(t0, tq), :] = dq
            return 0

        lax.fori_loop(0, s // tq, q_block, 0)

    head = pl.BlockSpec((s, dh), lambda h: (0, h))
    return pl.pallas_call(
        body,
        out_shape=tuple(jax.ShapeDtypeStruct((s, w), F32) for _ in range(3)),
        grid=(n_h,),
        in_specs=[head, head, pl.BlockSpec((s, dh), lambda h: (0, v_off + h)), head, head],
        out_specs=(head, head, head),
        name=name, compiler_params=_cparams(("parallel",)),
    )(qn, kn, v, tot, do)


ROW_TILE = 256
WIDE_ROW_TILE = 64


def _rows(width, col=0, tm=ROW_TILE):
    return pl.BlockSpec((tm, width), lambda i: (i, col))


_wide_rows = functools.partial(_rows, tm=WIDE_ROW_TILE)


def _whole(shape):
    return pl.BlockSpec(shape, lambda i: (0,) * len(shape))


def _ew_call(body, out_shape, in_specs, out_specs, args, n_rows, name, carried=False):
    return pl.pallas_call(
        body, out_shape=out_shape, grid=(n_rows // in_specs[0].block_shape[0],), in_specs=in_specs, out_specs=out_specs,
        name=name, compiler_params=_cparams(("arbitrary",) if carried else ("parallel",)),
    )(*args)


def _first_step(*refs):
    @pl.when(pl.program_id(0) == 0)
    def _():
        for r in refs:
            r[...] = jnp.zeros_like(r)


def rmsnorm_fwd(x, w, after=None, *, name):
    s, d = x.shape

    def body(x_ref, w_ref, *rest):
        o_ref = rest[-1]
        xv = x_ref[...]
        r = lax.rsqrt(jnp.mean(xv * xv, axis=-1, keepdims=True) + NORM_EPS)
        o_ref[...] = (xv * r * w_ref[...]).astype(BF16)

    extra = [] if after is None else [after]
    return _ew_call(body, jax.ShapeDtypeStruct((s, d), BF16),
                    [_rows(d), _whole((1, d))] + [_whole(TOKEN_SHAPE)] * len(extra), _rows(d),
                    (x, w.reshape(1, d), *extra), s, name)


def rmsnorm_bwd(x, w, dy, dres, *, name):
    s, d = x.shape

    def body(x_ref, w_ref, dy_ref, dr_ref, dx_ref, dw_ref):
        _first_step(dw_ref)
        xv = x_ref[...]
        r = lax.rsqrt(jnp.mean(xv * xv, axis=-1, keepdims=True) + NORM_EPS)
        xhat = xv * r
        dyv = dy_ref[...].astype(F32)
        dw_ref[...] += jnp.sum(dyv * xhat, axis=0, keepdims=True)
        g = dyv * w_ref[...]
        dx_ref[...] = dr_ref[...] + r * (g - xhat * jnp.mean(g * xhat, axis=-1, keepdims=True))

    return _ew_call(body, (jax.ShapeDtypeStruct((s, d), F32), jax.ShapeDtypeStruct((1, d), F32)),
                    [_rows(d), _whole((1, d)), _rows(d), _rows(d)], (_rows(d), _whole((1, d))),
                    (x, w.reshape(1, d), dy, dres), s, name, carried=True)


def ple_fwd(h1, gate_pre, pp, *, name):
    s, d = h1.shape

    def body(h_ref, g_ref, p_ref, o_ref):
        o_ref[...] = h_ref[...] + p_ref[...] * _sigmoid(g_ref[...])

    return _ew_call(body, jax.ShapeDtypeStruct((s, d), F32), [_rows(d)] * 3, _rows(d), (h1, gate_pre, pp), s, name)


def ple_bwd(dh2, gate_pre, pp, after, *, name):
    s, d = dh2.shape

    def body(dh_ref, g_ref, p_ref, after_ref, dp_ref, dg_ref):
        gate = _sigmoid(g_ref[...])
        dh = dh_ref[...]
        dp_ref[...] = (dh * gate).astype(BF16)
        dg_ref[...] = (dh * p_ref[...] * gate * (1.0 - gate)).astype(BF16)

    shp = jax.ShapeDtypeStruct((s, d), BF16)
    return _ew_call(body, (shp, shp), [_rows(d)] * 3 + [_whole(TOKEN_SHAPE)], (_rows(d), _rows(d)),
                    (dh2, gate_pre, pp, after), s, name)


def loss_head(y, target, *, name):
    s, d = y.shape

    def body(y_ref, t_ref, l_ref, dy_ref):
        _first_step(l_ref)
        err = y_ref[...] - t_ref[...]
        per_tok = jnp.mean(err * err, axis=-1, keepdims=True)
        l_ref[...] += 0.5 * jnp.sum(per_tok, axis=0, keepdims=True)
        dy_ref[...] = err * (1.0 / d)

    return _ew_call(body, (jax.ShapeDtypeStruct((1, 1), F32), jax.ShapeDtypeStruct((s, d), F32)),
                    [_rows(d), _rows(d)], (_whole((1, 1)), _rows(d)), (y, target), s, name, carried=True)


CONV_COL_TILE = 256


def _conv_taps(x, w_ref):
    row = lax.broadcasted_iota(jnp.int32, (x.shape[0], 1), 0)
    acc = x * w_ref[SSD_D_CONV - 1:SSD_D_CONV, :]
    shifted = []
    for d in range(1, SSD_D_CONV):
        xs = jnp.where(row >= d, pltpu.roll(x, d, 0), 0.0)
        shifted.append(xs)
        acc = acc + xs * w_ref[SSD_D_CONV - 1 - d:SSD_D_CONV - d, :]
    return acc, shifted


def ssd_conv_fwd(x, w, b, *, name):
    s, c = x.shape
    tc = _pick(c, (CONV_COL_TILE, LANES))

    def body(x_ref, w_ref, b_ref, o_ref):
        pre, _ = _conv_taps(x_ref[...], w_ref)
        o_ref[...] = _silu(pre + b_ref[...])

    col = pl.BlockSpec((s, tc), lambda j: (0, j))
    return pl.pallas_call(
        body, out_shape=jax.ShapeDtypeStruct((s, c), F32), grid=(c // tc,),
        in_specs=[col, pl.BlockSpec((SSD_D_CONV, tc), lambda j: (0, j)), pl.BlockSpec((1, tc), lambda j: (0, j))],
        out_specs=col, name=name, compiler_params=_cparams(("parallel",)),
    )(x, w, b)


def ssd_conv_bwd(x, w, b, dact, *, name):
    s, c = x.shape
    tc = _pick(c, (CONV_COL_TILE, LANES))

    def body(x_ref, w_ref, b_ref, da_ref, dx_ref, dw_ref, db_ref):
        xv = x_ref[...]
        pre, shifted = _conv_taps(xv, w_ref)
        dpre = da_ref[...] * _silu_grad(pre + b_ref[...])
        db_ref[...] = jnp.sum(dpre, axis=0, keepdims=True)
        row = lax.broadcasted_iota(jnp.int32, (s, 1), 0)
        dx = dpre * w_ref[SSD_D_CONV - 1:SSD_D_CONV, :]
        dw_ref[SSD_D_CONV - 1:SSD_D_CONV, :] = jnp.sum(dpre * xv, axis=0, keepdims=True)
        for d in range(1, SSD_D_CONV):
            k = SSD_D_CONV - 1 - d
            dw_ref[k:k + 1, :] = jnp.sum(dpre * shifted[d - 1], axis=0, keepdims=True)
            up = jnp.where(row < s - d, pltpu.roll(dpre, s - d, 0), 0.0)
            dx = dx + up * w_ref[k:k + 1, :]
        dx_ref[...] = dx.astype(BF16)

    col = pl.BlockSpec((s, tc), lambda j: (0, j))
    wspec = pl.BlockSpec((SSD_D_CONV, tc), lambda j: (0, j))
    bspec = pl.BlockSpec((1, tc), lambda j: (0, j))
    return pl.pallas_call(
        body,
        out_shape=(jax.ShapeDtypeStruct((s, c), BF16), jax.ShapeDtypeStruct((SSD_D_CONV, c), F32),
                   jax.ShapeDtypeStruct((1, c), F32)),
        grid=(c // tc,), in_specs=[col, wspec, bspec, col], out_specs=(col, wspec, bspec),
        name=name, compiler_params=_cparams(("parallel",)),
    )(x, w, b, dact)


def ssd_dt_fwd(dt_raw, bias, a_log, *, name):
    s, h = dt_raw.shape

    def body(r_ref, b_ref, al_ref, dt_ref, a_ref):
        zv = r_ref[...] + b_ref[...]
        dt_ref[...] = jnp.maximum(zv, 0.0) + jnp.log(1.0 + jnp.exp(-jnp.abs(zv)))
        a_ref[...] = -jnp.exp(al_ref[...])

    full = pl.BlockSpec((s, h), lambda: (0, 0))
    vec = pl.BlockSpec((1, h), lambda: (0, 0))
    return pl.pallas_call(
        body, out_shape=(jax.ShapeDtypeStruct((s, h), F32), jax.ShapeDtypeStruct((1, h), F32)),
        in_specs=[full, vec, vec], out_specs=(full, vec), name=name, compiler_params=_cparams(),
    )(dt_raw, bias.reshape(1, h), a_log.reshape(1, h))


def ssd_dt_bwd(dt_raw, bias, a_log, dt, ddt, dadt, *, name):
    s, h = dt_raw.shape

    def body(r_ref, b_ref, al_ref, dt_ref, ddt_ref, dadt_ref, dr_ref, db_ref, dal_ref):
        a = -jnp.exp(al_ref[...])
        dadt_v = dadt_ref[...]
        d_dt = ddt_ref[...] + a * dadt_v
        d_raw = d_dt * _sigmoid(r_ref[...] + b_ref[...])
        dr_ref[...] = d_raw
        db_ref[...] = jnp.sum(d_raw, axis=0, keepdims=True)
        dal_ref[...] = jnp.sum(dadt_v * dt_ref[...], axis=0, keepdims=True) * a

    full = pl.BlockSpec((s, h), lambda: (0, 0))
    vec = pl.BlockSpec((1, h), lambda: (0, 0))
    return pl.pallas_call(
        body, out_shape=(jax.ShapeDtypeStruct((s, h), F32), jax.ShapeDtypeStruct((1, h), F32),
                         jax.ShapeDtypeStruct((1, h), F32)),
        in_specs=[full, vec, vec, full, full, full], out_specs=(full, vec, vec), name=name,
        compiler_params=_cparams(),
    )(dt_raw, bias.reshape(1, h), a_log.reshape(1, h), dt, ddt, dadt)


def _group_mean(v, n_groups):
    gw = v.shape[-1] // n_groups
    parts = [jnp.broadcast_to(jnp.mean(v[:, k * gw:(k + 1) * gw], axis=-1, keepdims=True), (v.shape[0], gw))
             for k in range(n_groups)]
    return jnp.concatenate(parts, axis=-1)


def ssd_gate_fwd(y, z, gw, *, name):
    s, di = y.shape

    def body(y_ref, z_ref, w_ref, o_ref):
        yg = y_ref[...] * _silu(z_ref[...])
        r = lax.rsqrt(_group_mean(yg * yg, SSD_N_GROUPS) + GATED_NORM_EPS)
        o_ref[...] = (yg * r * w_ref[...]).astype(BF16)

    return _ew_call(body, jax.ShapeDtypeStruct((s, di), BF16), [_wide_rows(di), _wide_rows(di), _whole((1, di))],
                    _wide_rows(di), (y, z, gw.reshape(1, di)), s, name)


def ssd_gate_bwd(y, z, gw, dyn, *, name):
    s, di = y.shape

    def body(y_ref, z_ref, w_ref, dn_ref, dy_ref, dz_ref, dw_ref):
        _first_step(dw_ref)
        yv, zv = y_ref[...], z_ref[...]
        sz = _silu(zv)
        yg = yv * sz
        r = lax.rsqrt(_group_mean(yg * yg, SSD_N_GROUPS) + GATED_NORM_EPS)
        yhat = yg * r
        dn = dn_ref[...]
        dw_ref[...] += jnp.sum(dn * yhat, axis=0, keepdims=True)
        g = dn * w_ref[...]
        dyg = r * (g - yhat * _group_mean(g * yhat, SSD_N_GROUPS))
        dy_ref[...] = dyg * sz
        dz_ref[...] = (dyg * yv * _silu_grad(zv)).astype(BF16)

    return _ew_call(body, (jax.ShapeDtypeStruct((s, di), F32), jax.ShapeDtypeStruct((s, di), BF16),
                           jax.ShapeDtypeStruct((1, di), F32)),
                    [_wide_rows(di), _wide_rows(di), _whole((1, di)), _wide_rows(di)],
                    (_wide_rows(di), _wide_rows(di), _whole((1, di))),
                    (y, z, gw.reshape(1, di), dyn), s, name, carried=True)


def _head_mean(v):
    return _group_mean(v, v.shape[-1] // SB_HEAD_DIM)


def sb_qk_fwd(proj, qw, kw, *, name):
    s, w4 = proj.shape
    w = w4 // 4
    reps = w // SB_HEAD_DIM

    def body(q_ref, k_ref, qw_ref, kw_ref, qn_ref, kn_ref):
        for x_ref, w_ref, o_ref in ((q_ref, qw_ref, qn_ref), (k_ref, kw_ref, kn_ref)):
            xv = x_ref[...]
            r = lax.rsqrt(_head_mean(xv * xv) + NORM_EPS)
            o_ref[...] = (xv * r * jnp.tile(w_ref[...], (1, reps))).astype(BF16)

    shp = jax.ShapeDtypeStruct((s, w), BF16)
    return _ew_call(body, (shp, shp), [_rows(w, 0), _rows(w, 1), _whole((1, SB_HEAD_DIM)), _whole((1, SB_HEAD_DIM))],
                    (_rows(w), _rows(w)), (proj, proj, qw.reshape(1, -1), kw.reshape(1, -1)), s, name)


def sb_gate_fwd(o, proj, *, name):
    s, w = o.shape

    def body(o_ref, g_ref, og_ref):
        og_ref[...] = (o_ref[...] * _silu(g_ref[...])).astype(BF16)

    return _ew_call(body, jax.ShapeDtypeStruct((s, w), BF16), [_rows(w), _rows(w, 3)], _rows(w), (o, proj), s, name)


def sb_gate_bwd(dog, o, proj, *, name):
    s, w = o.shape

    def body(d_ref, o_ref, g_ref, do_ref, dg_ref):
        gv, dv = g_ref[...], d_ref[...]
        do_ref[...] = dv * _silu(gv)
        dg_ref[...] = (dv * o_ref[...] * _silu_grad(gv)).astype(BF16)

    return _ew_call(body, (jax.ShapeDtypeStruct((s, w), F32), jax.ShapeDtypeStruct((s, w), BF16)),
                    [_rows(w), _rows(w), _rows(w, 3)], (_rows(w), _rows(w)), (dog, o, proj), s, name)


def sb_pack_bwd(proj, qw, kw, dqn, dkn, dv, dg, *, name):
    s, w4 = proj.shape
    w = w4 // 4
    reps = w // SB_HEAD_DIM

    def body(q_ref, k_ref, qw_ref, kw_ref, dqn_ref, dkn_ref, dv_ref, dg_ref, dp_ref, dqw_ref, dkw_ref):
        _first_step(dqw_ref, dkw_ref)
        for idx, (x_ref, w_ref, d_ref, dw_ref) in enumerate(((q_ref, qw_ref, dqn_ref, dqw_ref),
                                                           (k_ref, kw_ref, dkn_ref, dkw_ref))):
            xv = x_ref[...]
            r = lax.rsqrt(_head_mean(xv * xv) + NORM_EPS)
            xhat = xv * r
            dn = d_ref[...]
            per_col = jnp.sum(dn * xhat, axis=0, keepdims=True)
            acc = per_col[:, 0:SB_HEAD_DIM]
            for hh in range(1, reps):
                acc = acc + per_col[:, hh * SB_HEAD_DIM:(hh + 1) * SB_HEAD_DIM]
            dw_ref[...] += acc
            g = dn * jnp.tile(w_ref[...], (1, reps))
            dp_ref[:, idx * w:(idx + 1) * w] = (r * (g - xhat * _head_mean(g * xhat))).astype(BF16)
        dp_ref[:, 2 * w:3 * w] = dv_ref[...].astype(BF16)
        dp_ref[:, 3 * w:4 * w] = dg_ref[...]

    vec = _whole((1, SB_HEAD_DIM))
    return _ew_call(body, (jax.ShapeDtypeStruct((s, w4), BF16), jax.ShapeDtypeStruct((1, SB_HEAD_DIM), F32),
                           jax.ShapeDtypeStruct((1, SB_HEAD_DIM), F32)),
                    [_wide_rows(w, 0), _wide_rows(w, 1), vec, vec, _wide_rows(w), _wide_rows(w), _wide_rows(w),
                     _wide_rows(w)],
                    (_wide_rows(w4), vec, vec),
                    (proj, proj, qw.reshape(1, -1), kw.reshape(1, -1), dqn, dkn, dv, dg), s, name, carried=True)


_HBM = pl.BlockSpec(memory_space=pltpu.HBM)


def _mesh_pos():
    return lax.axis_index("x"), lax.axis_index("y"), lax.axis_index("c")


def _other_chips(x, y):
    return [(1 - x, y), (x, 1 - y), (1 - x, 1 - y)]


_SEM = pl.BlockSpec(memory_space=pltpu.SEMAPHORE)
_ANY = pl.BlockSpec(memory_space=pl.ANY)
_DATAFLOW = pltpu.SideEffectType.DATAFLOW_SIDE_EFFECTING
N_PEER_CHIPS = N_CHIP - 1
TOKEN_SHAPE = (8, LANES)


def _in_hbm(t):
    return pltpu.with_memory_space_constraint(t, pltpu.HBM)


def _ici_copies(kind, src_refs, land_refs, send_sems, recv_sems, arrivals=False):
    x, y, c = _mesh_pos()
    out = []
    for a in range(len(land_refs)):
        if kind in ("pass", "swap"):
            if kind == "pass":
                src, dst = land_refs[a].at[:, c], land_refs[a].at[:, 1 - c if arrivals else c]
            else:
                src, dst = src_refs[a].at[:, 1 - c], land_refs[a]
            out.append(pltpu.make_async_remote_copy(
                src_ref=src, dst_ref=dst, send_sem=send_sems.at[a], recv_sem=recv_sems.at[a],
                device_id=(x, y, 1 - c), device_id_type=MESH))
            continue
        for j, chip in enumerate(_other_chips(x, y)):
            if kind == "gather":
                src = land_refs[a].at[4 * x + 2 * y + c]
                dst = land_refs[a].at[4 * chip[0] + 2 * chip[1] + c] if arrivals else src
            else:
                src, dst = src_refs[a].at[2 * chip[0] + chip[1]], land_refs[a].at[j]
            k = a * N_PEER_CHIPS + j
            out.append(pltpu.make_async_remote_copy(
                src_ref=src, dst_ref=dst, send_sem=send_sems.at[k], recv_sem=recv_sems.at[k],
                device_id=(*chip, c), device_id_type=MESH))
    return out


def _n_copies(kind, lands):
    return len(lands) * (1 if kind in ("pass", "swap") else N_PEER_CHIPS)


def ici_start(kind, srcs, lands, after=(), *, name):
    ns, nb = len(srcs), len(srcs) + len(lands)
    n_sem = _n_copies(kind, lands)

    def body(*refs):
        first_out = nb + len(after)
        for cp in _ici_copies(kind, refs[:ns], refs[ns:nb], refs[first_out], refs[first_out + 1]):
            cp.start()
        refs[-1][...] = jnp.zeros(TOKEN_SHAPE, F32)

    outs = pl.pallas_call(
        body, name=name,
        out_shape=(pltpu.SemaphoreType.DMA((n_sem,)), pltpu.SemaphoreType.DMA((n_sem,)),
                   *[pltpu.HBM(t.shape, t.dtype) for t in (*srcs, *lands)], jax.ShapeDtypeStruct(TOKEN_SHAPE, F32)),
        in_specs=[_HBM] * nb + [_ANY] * len(after),
        out_specs=(_SEM, _SEM, *([_HBM] * nb), pl.BlockSpec(memory_space=pltpu.VMEM)),
        input_output_aliases={k: 2 + k for k in range(nb)},
        compiler_params=pltpu.CompilerParams(has_side_effects=_DATAFLOW),
    )(*[_in_hbm(t) for t in (*srcs, *lands)], *after)
    return outs[0], outs[1], list(outs[2:2 + ns]), list(outs[2 + ns:2 + nb]), outs[-1]


def ici_wait(kind, started, after, *, name):
    send_sems, recv_sems, srcs, lands, _ = started
    ns, nb = len(srcs), len(srcs) + len(lands)

    def body(*refs):
        for cp in _ici_copies(kind, refs[:ns], refs[ns:nb], refs[nb], refs[nb + 1]):
            cp.wait_send()
        for cp in _ici_copies(kind, refs[:ns], refs[ns:nb], refs[nb], refs[nb + 1], arrivals=True):
            cp.wait_recv()

    outs = pl.pallas_call(
        body, name=name,
        out_shape=tuple(pltpu.HBM(t.shape, t.dtype) for t in (*srcs, *lands)),
        in_specs=[_HBM] * nb + [_SEM, _SEM] + [_ANY] * len(after),
        out_specs=tuple([_HBM] * nb),
        input_output_aliases={k: k for k in range(nb)},
        compiler_params=pltpu.CompilerParams(has_side_effects=_DATAFLOW),
    )(*srcs, *lands, send_sems, recv_sems, *after)
    return list(outs[:ns]), list(outs[ns:])


def all_reduce_small(v, *, name):
    r = v.shape[0]

    def body(v_ref, o_ref, buf, send_sems, recv_sems):
        x, y, c = _mesh_pos()
        me = 4 * x + 2 * y + c
        buf[me] = v_ref[...]
        copies = []
        for k in range(1, N_DEV):
            to = ((x + (k >> 2)) % 2, (y + ((k >> 1) & 1)) % 2, (c + (k & 1)) % 2)
            copies.append(pltpu.make_async_remote_copy(
                src_ref=v_ref, dst_ref=buf.at[me], send_sem=send_sems.at[k - 1], recv_sem=recv_sems.at[k - 1],
                device_id=to, device_id_type=MESH))
        for cp in copies:
            cp.start()
        for cp in copies:
            cp.wait()
        acc = buf[0]
        for d in range(1, N_DEV):
            acc = acc + buf[d]
        o_ref[...] = acc

    vm = pl.BlockSpec(memory_space=pltpu.VMEM)
    return pl.pallas_call(
        body, out_shape=jax.ShapeDtypeStruct(v.shape, F32), in_specs=[vm], out_specs=vm,
        scratch_shapes=[pltpu.VMEM((N_DEV, r, LANES), F32), pltpu.SemaphoreType.DMA((N_DEV - 1,)),
                        pltpu.SemaphoreType.DMA((N_DEV - 1,))],
        name=name,
    )(v)


def pair_add(g, r1, core, *, name):
    _, _, rows, cols = g.shape
    tm = _pick(rows, (256, 128))

    def body(c_ref, g_ref, r_ref, o_ref):
        o_ref[...] = (g_ref[...].astype(F32) + r_ref[...].astype(F32)).astype(o_ref.dtype)

    return pl.pallas_call(
        body, out_shape=jax.ShapeDtypeStruct(r1.shape, g.dtype),
        grid_spec=pltpu.PrefetchScalarGridSpec(
            num_scalar_prefetch=1, grid=(N_CHIP, rows // tm),
            in_specs=[pl.BlockSpec((None, None, tm, cols), lambda k, i, c_ref: (k, c_ref[0], i, 0)),
                      pl.BlockSpec((None, tm, cols), lambda k, i, c_ref: (k, i, 0))],
            out_specs=pl.BlockSpec((None, tm, cols), lambda k, i, c_ref: (k, i, 0))),
        name=name, compiler_params=_cparams(("parallel", "parallel")),
    )(core, g, r1)


def _adamw_math(w, g, m, v):
    m = ADAM_B1 * m + (1.0 - ADAM_B1) * g
    v = ADAM_B2 * v + (1.0 - ADAM_B2) * (g * g)
    m_hat = m / (1.0 - ADAM_B1 ** ADAM_STEP)
    v_hat = v / (1.0 - ADAM_B2 ** ADAM_STEP)
    delta = -ADAM_LR * (m_hat / (jnp.sqrt(v_hat) + ADAM_EPS) + ADAM_WD * w)
    return delta, m, v


def adamw_sharded(w, m, v, layer, chip_sums, received, chip, into, *, name):
    _, rows, cols = w.shape
    tm = _pick(rows, (256, 128))

    def body(k_ref, w_ref, m_ref, v_ref, t_ref, r_ref, *rest):
        g_ref, d_ref, nm_ref, nv_ref, token_ref = rest[-5:]
        g = t_ref[...].astype(F32)
        for j in range(N_CHIP - 1):
            g = g + r_ref[j].astype(F32)
        d, mm, vv = _adamw_math(w_ref[...], g, m_ref[...], v_ref[...])
        g_ref[...] = g
        d_ref[...] = d
        nm_ref[...] = mm
        nv_ref[...] = vv
        token_ref[...] = jnp.zeros(TOKEN_SHAPE, F32)

    blk = pl.BlockSpec((None, tm, cols), lambda i, k_ref: (layer, i, 0))
    shp = jax.ShapeDtypeStruct(w.shape, F32)
    in_specs = [blk, blk, blk,
                pl.BlockSpec((None, tm, cols), lambda i, k_ref: (k_ref[0], i, 0)),
                pl.BlockSpec((N_CHIP - 1, tm, cols), lambda i, k_ref: (0, i, 0))]
    operands = [chip, w, m, v, chip_sums, received]
    aliases = {}
    if into is not None:
        aliases = {len(operands) + q: q for q in range(4)}
        in_specs += [_ANY] * 4
        operands += list(into)
    outs = pl.pallas_call(
        body, out_shape=(shp, shp, shp, shp, jax.ShapeDtypeStruct(TOKEN_SHAPE, F32)),
        grid_spec=pltpu.PrefetchScalarGridSpec(
            num_scalar_prefetch=1, grid=(rows // tm,), in_specs=in_specs,
            out_specs=(blk, blk, blk, blk, pl.BlockSpec(TOKEN_SHAPE, lambda i, k_ref: (0, 0)))),
        input_output_aliases=aliases,
        name=name, compiler_params=_cparams(("arbitrary",)),
    )(*operands)
    return outs[:4], outs[4]


def adamw_replicated(w, m, v, g, *, name):
    def body(w_ref, m_ref, v_ref, g_ref, d_ref, nm_ref, nv_ref):
        d, mm, vv = _adamw_math(w_ref[...], g_ref[...], m_ref[...], v_ref[...])
        d_ref[...] = d
        nm_ref[...] = mm
        nv_ref[...] = vv

    shp = jax.ShapeDtypeStruct(w.shape, F32)
    return pl.pallas_call(body, out_shape=(shp, shp, shp), name=name, compiler_params=_cparams())(w, m, v, g)


WEIGHT_NAMES = ("norm_w", "ssd_in_w", "ssd_conv_w", "ssd_conv_b", "ssd_dt_bias", "ssd_a_log", "ssd_d",
                "ssd_gnorm_w", "ssd_out_w", "sb_in_w", "sb_qn_w", "sb_kn_w", "sb_out_w", "ple_norm_w",
                "ple_gate_w", "ple_proj_w")
REPLICATED = ("norm_w", "ssd_conv_b", "ssd_dt_bias", "ssd_a_log", "ssd_d", "ssd_gnorm_w", "sb_qn_w", "sb_kn_w",
              "ple_norm_w")
PACK_ROWS = 8


def _pack(parts):
    flat = jnp.concatenate([t.reshape(-1) for t in parts])
    pad = (-flat.shape[0]) % (PACK_ROWS * LANES)
    return jnp.pad(flat, (0, pad)).reshape(-1, LANES)


def _unpack(packed, like):
    flat = packed.reshape(-1)
    out, off = [], 0
    for t in like:
        out.append(flat[off:off + t.size].reshape(t.shape))
        off += t.size
    return out


def _to_group_lanes(v, r):
    t = v.reshape(v.shape[0], SSD_N_GROUPS, r).transpose(1, 0, 2)
    return jnp.pad(t, ((0, 0), (0, 0), (0, LANES - r)))


def _from_group_lanes(t, r):
    return t[:, :, :r].transpose(1, 0, 2).reshape(t.shape[1], SSD_N_GROUPS * r)


def _head_vec(v, r):
    return jnp.pad(v.reshape(SSD_N_GROUPS, 1, r), ((0, 0), (0, 0), (0, LANES - r)))


def _col_blocks(full):
    rows = full.shape[0]
    return full.reshape(rows, N_DEV, -1).transpose(1, 0, 2)


def _from_col_blocks(blocks):
    return blocks.transpose(1, 0, 2).reshape(blocks.shape[1], -1)


def _split_cols(full, widths):
    out, off = [], 0
    for w in widths:
        out.append(full[:, off:off + w])
        off += w
    return out


def kernel(x, p, norm_w, ssd_in_w, ssd_conv_w, ssd_conv_b, ssd_dt_bias, ssd_a_log, ssd_d, ssd_gnorm_w, ssd_out_w, sb_in_w, sb_qn_w, sb_kn_w, sb_out_w, ple_norm_w, ple_gate_w, ple_proj_w, loss_target, m_norm_w, m_ssd_in_w, m_ssd_conv_w, m_ssd_conv_b, m_ssd_dt_bias, m_ssd_a_log, m_ssd_d, m_ssd_gnorm_w, m_ssd_out_w, m_sb_in_w, m_sb_qn_w, m_sb_kn_w, m_sb_out_w, m_ple_norm_w, m_ple_gate_w, m_ple_proj_w, v_norm_w, v_ssd_in_w, v_ssd_conv_w, v_ssd_conv_b, v_ssd_dt_bias, v_ssd_a_log, v_ssd_d, v_ssd_gnorm_w, v_ssd_out_w, v_sb_in_w, v_sb_qn_w, v_sb_kn_w, v_sb_out_w, v_ple_norm_w, v_ple_gate_w, v_ple_proj_w):
    env = dict(locals())
    wts = {n: env[n] for n in WEIGHT_NAMES}
    mom1 = {n: env["m_" + n] for n in WEIGHT_NAMES}
    mom2 = {n: env["v_" + n] for n in WEIGHT_NAMES}

    s, d = x.shape[1], x.shape[2]
    depth = norm_w.shape[0]
    n_ssd, n_sb = ssd_in_w.shape[0], sb_in_w.shape[0]
    di = ssd_out_w.shape[1] * N_DEV
    n_heads = ssd_dt_bias.shape[1]
    hpg = n_heads // SSD_N_GROUPS
    nbc = SSD_N_GROUPS * SSD_D_STATE
    in_segs = (di, di, nbc, nbc, n_heads)
    conv_segs = (di, nbc, nbc)
    sb_w = sb_out_w.shape[1] * N_DEV
    selectors = ssd_selectors(hpg)
    xi, yi, ci = _mesh_pos()
    core = ci.astype(jnp.int32).reshape(1)
    chip = (2 * xi + yi).astype(jnp.int32).reshape(1)

    def layer_keys(i):
        j = i // 2
        mixer = [("ssd_in_w", j), ("ssd_conv_w", j), ("ssd_out_w", j)] if i % 2 == 0 else [("sb_in_w", j), ("sb_out_w", j)]
        return mixer + [("ple_gate_w", i), ("ple_proj_w", i)]

    me_block = 4 * xi + 2 * yi + ci

    def landing_zone(t):
        return lax.dynamic_update_index_in_dim(lax.empty((N_DEV,) + t.shape, t.dtype), t, me_block, 0)

    def groups(i):
        keys = layer_keys(i)
        return [keys[:2], keys[2:]] if i == 0 else [keys]

    gathers, prev = {}, []
    for i in range(depth):
        for q, keys in enumerate(groups(i)):
            shards = [wts[n][idx] for n, idx in keys]
            if prev:
                shards = lax.optimization_barrier((prev[0], shards))[1]
            lands = [landing_zone(t if n == "ssd_conv_w" else t.astype(BF16)) for (n, _), t in zip(keys, shards)]
            gathers[i, q] = ici_start("gather", [], lands, after=prev, name=f"ag{i}{'ab'[q]}_start")
            prev = [gathers[i, q][4]]
    all_started = prev[0]
    full, ssd_full, passing = {}, {}, {}

    def hand_over(i, q, after):
        _, lands = ici_wait("gather", gathers[i, q], after, name=f"ag{i}{'ab'[q]}_wait")
        passing[i, q] = ici_start("pass", [], [t.reshape(N_CHIP, 2, *t.shape[1:]) for t in lands],
                                  name=f"ag{i}{'ab'[q]}_pass_start")

    def arrive(i, q, after):
        _, lands = ici_wait("pass", passing[i, q], after, name=f"ag{i}{'ab'[q]}_pass_wait")
        for k, t in zip(groups(i)[q], lands):
            full[k] = t.reshape(N_DEV, *t.shape[2:])

    def w_out_of(i):
        return full["ssd_out_w", i // 2].reshape(di, d) if i % 2 == 0 else full["sb_out_w", i // 2].reshape(sb_w, d)

    h = x.reshape(s, d)
    saved = []
    hand_over(0, 0, [all_started])
    arrive(0, 0, [all_started])
    for i in range(depth):
        j = i // 2
        if i > 0:
            arrive(i, 0, [h])
        sv = dict(h_in=h)
        u = rmsnorm_fwd(h, norm_w[i], name=f"l{i}_norm")
        sv["u"] = u
        if i % 2 == 0:
            fw = ssd_full[j] = dict(
                w_in=_split_cols(_from_col_blocks(full["ssd_in_w", j]), in_segs),
                conv_w=_split_cols(_from_col_blocks(full["ssd_conv_w", j]), conv_segs),
                conv_b=_split_cols(ssd_conv_b[j].reshape(1, -1), conv_segs))
            raw = [matmul(u, wseg, name=f"l{i}_in{q}") for q, wseg in enumerate(fw["w_in"])]
            if i == 0:
                hand_over(0, 1, [raw[4]])
            z, dt_raw = raw[0], raw[4]
            act = [ssd_conv_fwd(raw[1 + q], fw["conv_w"][q], fw["conv_b"][q], name=f"l{i}_conv{q}") for q in range(3)]
            dt, a_neg = ssd_dt_fwd(dt_raw, ssd_dt_bias[j], ssd_a_log[j], name=f"l{i}_dt")
            dtp = _to_group_lanes(dt, hpg)
            a_g = _head_vec(a_neg.reshape(-1), hpg)
            d_x = jnp.repeat(ssd_d[j].reshape(SSD_N_GROUPS, 1, hpg), SSD_HEAD_DIM, axis=2)
            y, states = ssd_scan_fwd(act[0], act[1], act[2], dtp, a_g, d_x, selectors, heads_per_group=hpg,
                                     name=f"l{i}_scan")
            yn = ssd_gate_fwd(y, z, ssd_gnorm_w[j], name=f"l{i}_gate")
            if i == 0:
                arrive(0, 1, [yn])
            h1 = matmul(yn, w_out_of(i), res=h, name=f"l{i}_out")
            sv.update(raw=raw, act=act, dt=dt, dtp=dtp, a_g=a_g, d_x=d_x, y=y, states=states, yn=yn)
        else:
            proj = matmul(u, full["sb_in_w", j], name=f"l{i}_in")
            qn, kn = sb_qk_fwd(proj, sb_qn_w[j], sb_kn_w[j], name=f"l{i}_qknorm")
            v_off = 2 * sb_w // SB_HEAD_DIM
            o, tot = sb_attn_fwd(qn, kn, proj, v_off=v_off, name=f"l{i}_attn")
            og = sb_gate_fwd(o, proj, name=f"l{i}_gate")
            h1 = matmul(og, w_out_of(i), res=h, name=f"l{i}_out")
            sv.update(proj=proj, qn=qn, kn=kn, o=o, tot=tot, og=og, v_off=v_off)
        if i + 1 < depth:
            hand_over(i + 1, 0, [h1])
        t = rmsnorm_fwd(h1, ple_norm_w[i], passing[i + 1, 0][4] if i + 1 < depth else None, name=f"l{i}_plenorm")
        gate_pre = matmul(t, full["ple_gate_w", i].reshape(d, d), name=f"l{i}_plegate")
        pp = matmul(p[i, 0], full["ple_proj_w", i], name=f"l{i}_pleproj")
        h = ple_fwd(h1, gate_pre, pp, name=f"l{i}_ple")
        sv.update(h1=h1, t=t, gate_pre=gate_pre, pp=pp)
        saved.append(sv)

    loss_part, dh = loss_head(h, loss_target.reshape(s, d), name="loss_head")
    loss = lax.psum(loss_part[0, 0], ("x", "y", "c"))

    big = {}
    small = {n: [None] * wts[n].shape[0] for n in REPLICATED}
    swaps, scatters = {}, {}
    order_after = jnp.zeros(TOKEN_SHAPE, F32)
    pending = None

    def send_to_sibling(i, q):
        blocks = [big[k].reshape(N_CHIP, 2, *big[k].shape[1:]) for k in groups(i)[::-1][q]]
        swaps[i, q] = ici_start("swap", blocks, [lax.empty((N_CHIP,) + t.shape[2:], t.dtype) for t in blocks],
                                name=f"rs{i}{'ab'[q]}_swap_start")
        return swaps[i, q][4]

    def send_to_chips(i, q, after):
        blocks, from_sibling = ici_wait("swap", swaps[i, q], after, name=f"rs{i}{'ab'[q]}_swap_wait")
        sums = [pair_add(g, r1, core, name=f"rs{i}{'ab'[q]}_pair_add{a}")
                for a, (g, r1) in enumerate(zip(blocks, from_sibling))]
        scatters[i, q] = ici_start("scatter", sums, [lax.empty((N_PEER_CHIPS,) + t.shape[1:], t.dtype) for t in sums],
                                   name=f"rs{i}{'ab'[q]}_start")
        return scatters[i, q][4]

    for i in reversed(range(depth)):
        j = i // 2
        sv = saved[i]
        dpp, dgp = ple_bwd(dh, sv["gate_pre"], sv["pp"], order_after, name=f"b{i}_ple")
        big["ple_proj_w", i] = matmul(p[i, 0], dpp, mode="tn", out_dtype=BF16, out_blocks=ple_proj_w.shape[2],
                                      name=f"b{i}_pleproj_w")
        big["ple_gate_w", i] = matmul(sv["t"], dgp, mode="tn", out_dtype=BF16, name=f"b{i}_plegate_w").reshape(N_DEV, -1, d)
        dt_ = matmul(dgp, full["ple_gate_w", i].reshape(d, d), mode="nt", name=f"b{i}_plegate_x")
        dh1, g_pn = rmsnorm_bwd(sv["h1"], ple_norm_w[i], dt_, dh, name=f"b{i}_plenorm")
        small["ple_norm_w"][i] = g_pn
        behind = send_to_chips(*pending, [dh1]) if pending is not None else None
        pending = None
        u = sv["u"]
        if i % 2 == 0:
            fw = ssd_full[j]
            raw, act = sv["raw"], sv["act"]
            big["ssd_out_w", j] = matmul(sv["yn"], dh1, mode="tn", out_dtype=BF16, name=f"b{i}_out_w").reshape(N_DEV, -1, d)
            if i == 0:
                send_to_sibling(0, 0)
            dyn = matmul(dh1, w_out_of(i), mode="nt", after=behind, name=f"b{i}_out_x")
            dy, dz, g_gn = ssd_gate_bwd(sv["y"], raw[0], ssd_gnorm_w[j], dyn, name=f"b{i}_gate")
            dxs, dbm, dcm, ddtp, dadtp, dd_g = ssd_scan_bwd(act[0], act[1], act[2], sv["dtp"], sv["a_g"], sv["d_x"], selectors,
                                                          sv["states"], dy, heads_per_group=hpg, name=f"b{i}_scan")
            behind = send_to_chips(0, 0, [dxs]) if i == 0 else None
            ddt_raw, g_dtb, g_alog = ssd_dt_bwd(raw[4], ssd_dt_bias[j], ssd_a_log[j], sv["dt"],
                                                _from_group_lanes(ddtp, hpg), _from_group_lanes(dadtp, hpg),
                                                name=f"b{i}_dt")
            conv_back = [ssd_conv_bwd(raw[1 + q], fw["conv_w"][q], fw["conv_b"][q], dact, name=f"b{i}_conv{q}")
                         for q, dact in enumerate((dxs, dbm, dcm))]
            dsegs = [dz] + [cb[0] for cb in conv_back] + [ddt_raw]
            g_in = jnp.concatenate([matmul(u, ds, mode="tn", out_dtype=BF16, name=f"b{i}_in{q}_w")
                                    for q, ds in enumerate(dsegs)], axis=1)
            big["ssd_in_w", j] = _col_blocks(g_in)
            big["ssd_conv_w", j] = _col_blocks(jnp.concatenate([cb[1] for cb in conv_back], axis=1))
            du = None
            for q, (ds, wseg) in enumerate(zip(dsegs, fw["w_in"])):
                du = matmul(ds, wseg, mode="nt", res=du, after=behind if q == 0 else None, name=f"b{i}_in{q}_x")
            small["ssd_conv_b"][j] = jnp.concatenate([cb[2] for cb in conv_back], axis=1)
            small["ssd_dt_bias"][j] = g_dtb
            small["ssd_a_log"][j] = g_alog
            small["ssd_d"][j] = dd_g[:, 0, :hpg]
            small["ssd_gnorm_w"][j] = g_gn
        else:
            proj = sv["proj"]
            big["sb_out_w", j] = matmul(sv["og"], dh1, mode="tn", out_dtype=BF16, name=f"b{i}_out_w").reshape(N_DEV, -1, d)
            dog = matmul(dh1, w_out_of(i), mode="nt", after=behind, name=f"b{i}_out_x")
            do, dg = sb_gate_bwd(dog, sv["o"], proj, name=f"b{i}_gate")
            dqn, dkn, dv = sb_attn_bwd(sv["qn"], sv["kn"], proj, sv["tot"], do, v_off=sv["v_off"], name=f"b{i}_attn")
            dproj, g_qn, g_kn = sb_pack_bwd(proj, sb_qn_w[j], sb_kn_w[j], dqn, dkn, dv, dg, name=f"b{i}_qknorm")
            big["sb_in_w", j] = matmul(u, dproj, mode="tn", out_dtype=BF16, out_blocks=sb_in_w.shape[2], name=f"b{i}_in_w")
            du = matmul(dproj, full["sb_in_w", j], mode="nt", name=f"b{i}_in_x")
            small["sb_qn_w"][j] = g_qn
            small["sb_kn_w"][j] = g_kn
        dh, g_n = rmsnorm_bwd(sv["h_in"], norm_w[i], du, dh1, name=f"b{i}_norm")
        small["norm_w"][i] = g_n
        pending = (i, len(groups(i)) - 1)
        order_after = send_to_sibling(*pending)
    send_to_chips(*pending, [dh])
    grad_x = dh.reshape(x.shape)

    rep_like = [wts[n] for n in REPLICATED]
    g_packed = all_reduce_small(_pack([jnp.stack([t.reshape(-1) for t in small[n]]) for n in REPLICATED]),
                                name="all_reduce_small_grads")
    d_packed, m_packed, v_packed = adamw_replicated(
        _pack(rep_like), _pack([mom1[n] for n in REPLICATED]), _pack([mom2[n] for n in REPLICATED]), g_packed,
        name="adamw_replicated")
    grads = dict(zip(REPLICATED, _unpack(g_packed, rep_like)))
    deltas = dict(zip(REPLICATED, _unpack(d_packed, rep_like)))
    new_m = dict(zip(REPLICATED, _unpack(m_packed, rep_like)))
    new_v = dict(zip(REPLICATED, _unpack(v_packed, rep_like)))

    updated = {}
    after = [scatters[0, len(groups(0)) - 1][4]]
    for i in reversed(range(depth)):
        for q, keys in enumerate(groups(i)[::-1]):
            sums, received = ici_wait("scatter", scatters[i, q], after, name=f"rs{i}{'ab'[q]}_wait")
            after = []
            for (n, idx), t_sum, recv in zip(keys, sums, received):
                updated[n], done = adamw_sharded(wts[n], mom1[n], mom2[n], idx, t_sum, recv, chip, updated.get(n),
                                                 name=f"adamw_{n}{idx}")
                after.append(done)
    for n, (g_n, d_n, m_n, v_n) in updated.items():
        grads[n], deltas[n], new_m[n], new_v[n] = g_n, d_n, m_n, v_n

    return (loss, grad_x, *[grads[n] for n in WEIGHT_NAMES], *[deltas[n] for n in WEIGHT_NAMES],
            *[new_m[n] for n in WEIGHT_NAMES], *[new_v[n] for n in WEIGHT_NAMES])
```

```python
import functools
import math

import jax
import jax.numpy as jnp
from jax import lax
from jax.experimental import pallas as pl
from jax.experimental.pallas import tpu as pltpu

F32 = jnp.float32
BF16 = jnp.bfloat16
MESH = pl.DeviceIdType.MESH

N_DEV = 8
N_CHIP = 4
LANES = 128
VMEM_LIMIT_BYTES = 56 * 1024 * 1024
MATMUL_TILE_BYTES = 36 * 1024 * 1024

NORM_EPS = 1e-6
GATED_NORM_EPS = 1e-5
SSD_HEAD_DIM = 64
SSD_N_GROUPS = 8
SSD_D_STATE = 128
SSD_D_CONV = 4
SSD_CHUNK = 128
SSD_FWD_CHUNKS_PER_STEP = 8
SSD_BWD_CHUNKS_PER_STEP = 8
SB_HEAD_DIM = 128

ADAM_LR = 0.001
ADAM_B1 = 0.9
ADAM_B2 = 0.999
ADAM_EPS = 1e-08
ADAM_WD = 0.01
ADAM_STEP = 10


def _cparams(sem=None, **kw):
    return pltpu.CompilerParams(dimension_semantics=sem, vmem_limit_bytes=VMEM_LIMIT_BYTES, **kw)


def _pick(dim, prefs):
    for t in prefs:
        if dim % t == 0:
            return t
    return dim


def _sigmoid(x):
    return 1.0 / (1.0 + jnp.exp(-x))


def _silu(x):
    return x * _sigmoid(x)


def _silu_grad(x):
    s = _sigmoid(x)
    return s * (1.0 + x * (1.0 - s))


def matmul(a, b, *, mode="nn", out_dtype=F32, res=None, out_blocks=None, after=None, name):
    b_blocked = b.ndim == 3
    if mode == "nn":
        m, kc = a.shape
        n = b.shape[-1] * (N_DEV if b_blocked else 1)
    elif mode == "nt":
        m, kc = a.shape
        n = b.shape[-2]
    else:
        kc, m = a.shape
        n = b.shape[-1]
    nb = b.shape[-1] if b_blocked else None
    tn = _pick(n if not out_blocks else out_blocks, (512, 256, 128))
    if b_blocked and mode == "nn":
        tn = _pick(nb, (512, 256, 128))
    k_unit = nb if (b_blocked and mode == "nt") else 1
    tm, tk = None, None
    for tm_try in (1024, 512, 256, 128):
        if m % tm_try:
            continue
        for tk_try in (kc, kc // 2, kc // 4, 2048, 1024, 512, 256, 128):
            if tk_try > kc or tk_try < k_unit or kc % tk_try or tk_try % k_unit:
                continue
            tiles = 2 * (tm_try * tk_try * a.dtype.itemsize + tk_try * tn * b.dtype.itemsize)
            tiles += tm_try * tn * (2 * jnp.dtype(out_dtype).itemsize + 4 + (8 if res is not None else 0))
            if tiles <= MATMUL_TILE_BYTES:
                tm, tk = tm_try, tk_try
                break
        if tm:
            break
    if tm is None:
        tm, tk = m, max(k_unit, LANES if kc % LANES == 0 else kc)
    nk = kc // tk
    grid = (m // tm, n // tn, nk)

    if mode == "tn":
        a_spec = pl.BlockSpec((tk, tm), lambda i, j, k: (k, i))
        dims = (((0,), (0,)), ((), ()))
    else:
        a_spec = pl.BlockSpec((tm, tk), lambda i, j, k: (i, k))
        dims = (((1,), (0,)), ((), ())) if mode == "nn" else (((1,), (1,)), ((), ()))
    if mode == "nt":
        if b_blocked:
            b_spec = pl.BlockSpec((tk // nb, tn, nb), lambda i, j, k: (k, j, 0))
        else:
            b_spec = pl.BlockSpec((tn, tk), lambda i, j, k: (j, k))
    else:
        if b_blocked:
            per = nb // tn
            b_spec = pl.BlockSpec((None, tk, tn), lambda i, j, k: (j // per, k, j % per))
        else:
            b_spec = pl.BlockSpec((tk, tn), lambda i, j, k: (k, j))
    if out_blocks:
        per_o = out_blocks // tn
        out_shape = jax.ShapeDtypeStruct((n // out_blocks, m, out_blocks), out_dtype)
        out_spec = pl.BlockSpec((None, tm, tn), lambda i, j, k: (j // per_o, i, j % per_o))
    else:
        out_shape = jax.ShapeDtypeStruct((m, n), out_dtype)
        out_spec = pl.BlockSpec((tm, tn), lambda i, j, k: (i, j))
    in_specs = [a_spec, b_spec]
    args = [a, b]
    if res is not None:
        in_specs.append(pl.BlockSpec((tm, tn), lambda i, j, k: (i, j)))
        args.append(res)
    if after is not None:
        in_specs.append(pl.BlockSpec(memory_space=pl.ANY))
        args.append(after)
    n_in = len(args)

    def body(*refs):
        a_ref, b_ref = refs[:2]
        r_ref = refs[2] if res is not None else None
        o_ref = refs[n_in]

        def finish(r):
            if res is not None:
                r = r + r_ref[...].astype(F32)
            o_ref[...] = r.astype(out_dtype)

        if b_blocked and mode == "nt":
            part = None
            for blk in range(tk // nb):
                term = lax.dot_general(a_ref[:, blk * nb:(blk + 1) * nb].astype(BF16), b_ref[blk].astype(BF16), dims,
                                       preferred_element_type=F32)
                part = term if part is None else part + term
        else:
            part = lax.dot_general(a_ref[...].astype(BF16), b_ref[...].astype(BF16), dims, preferred_element_type=F32)
        if nk == 1:
            finish(part)
            return
        acc_ref = refs[-1]
        k = pl.program_id(2)

        @pl.when(k == 0)
        def _():
            acc_ref[...] = part

        @pl.when(k > 0)
        def _():
            acc_ref[...] += part

        @pl.when(k == nk - 1)
        def _():
            finish(acc_ref[...])

    return pl.pallas_call(
        body, out_shape=out_shape, grid=grid, in_specs=in_specs, out_specs=out_spec,
        scratch_shapes=[] if nk == 1 else [pltpu.VMEM((tm, tn), F32)], name=name,
        compiler_params=_cparams(("parallel", "parallel", "arbitrary")),
    )(*args)


def _dot(a, b, dims, precision=None):
    return lax.dot_general(a, b, (dims, ((), ())), preferred_element_type=F32, precision=precision)


_NN = ((1,), (0,))
_NT = ((1,), (1,))
_TN = ((0,), (0,))
_EXACT = lax.Precision.HIGHEST


def _chunk_decay_terms(dt, a):
    ln = dt.shape[0]
    row = lax.broadcasted_iota(jnp.int32, (ln, ln), 0)
    col = lax.broadcasted_iota(jnp.int32, (ln, ln), 1)
    tri = (row >= col).astype(F32)
    a_col = _dot(tri, dt * a, _NN, _EXACT)
    return a_col, a_col.T, row >= col


def _exact_dot(x, sel, terms):
    t = x.shape[0]
    parts, rest = [], x
    for k in range(terms):
        piece = rest.astype(BF16)
        parts.append(piece)
        if k + 1 < terms:
            rest = rest - piece.astype(F32)
    r = _dot(jnp.concatenate(parts, axis=0), sel, _NN)
    out = r[:t]
    for k in range(1, terms):
        out = out + r[k * t:(k + 1) * t]
    return out


def ssd_selectors(r_n):
    lane = jnp.arange(LANES)
    spread64 = (lane[:, None] == jnp.arange(r_n * SSD_HEAD_DIM)[None, :] // SSD_HEAD_DIM).astype(BF16)
    pair_sum = jnp.stack([lane[None, :] == 2 * q + lane[:, None] // SSD_HEAD_DIM for q in range(r_n // 2)]).astype(BF16)
    row_sum = jnp.stack([jnp.broadcast_to(lane[None, :] == r, (LANES, LANES)) for r in range(r_n)]).astype(BF16)
    return spread64, pair_sum, row_sum


def _ssd_chunk_setup(dt, a, spread64):
    ln = dt.shape[0]
    a_col, a_row, causal = _chunk_decay_terms(dt, a)
    ea = jnp.exp(a_col)
    te = jnp.exp(a_col[ln - 1:ln, :] - a_col)
    return (a_row, a_col, _exact_dot(dt, spread64, 2), _exact_dot(ea, spread64, 2), _exact_dot(te, spread64, 2),
            ea, causal)


def ssd_scan_fwd(xs, bm, cm, dtp, a_g, d_x, selectors, *, heads_per_group, name):
    s, di = xs.shape
    g_n = SSD_N_GROUPS
    r_n, p_n, n_n, ln = heads_per_group, SSD_HEAD_DIM, SSD_D_STATE, SSD_CHUNK
    nc = s // ln
    cps = _pick(nc, (SSD_FWD_CHUNKS_PER_STEP, 4, 2, 1))
    pairs, pw = r_n // 2, 2 * p_n
    spread64 = selectors[0]

    def body(xs_ref, bm_ref, cm_ref, dt_ref, a_ref, d_ref, s64_ref, y_ref, st_ref, state):
        c = pl.program_id(1)

        @pl.when(c == 0)
        def _():
            state[...] = jnp.zeros_like(state)

        first_head = lax.broadcasted_iota(jnp.int32, (1, pw), 1) < p_n
        for sub in range(cps):
            rows = slice(sub * ln, (sub + 1) * ln)
            a_row, a_col, dt_x, ea_x, te_x, _, causal = _ssd_chunk_setup(dt_ref[rows, :], a_ref[...], s64_ref[...])
            bm_f = bm_ref[rows, :]
            bmb = bm_f.astype(BF16)
            bm_t = bm_f.T.astype(BF16)
            cmb = cm_ref[rows, :].astype(BF16)
            scores = _dot(cmb, bmb, _NT)
            for q in range(pairs):
                sl = slice(q * pw, (q + 1) * pw)
                x2 = xs_ref[rows, sl]
                xdt2 = x2 * dt_x[:, sl]
                xdt2b = xdt2.astype(BF16)
                y_heads = []
                for r in (2 * q, 2 * q + 1):
                    decay = jnp.exp(jnp.where(causal, a_col[:, r:r + 1] - a_row[r:r + 1, :], -jnp.inf))
                    y_heads.append(_dot((scores * decay).astype(BF16), xdt2b, _NN))
                s2t = state[q]
                st_ref[sub, q] = s2t
                y2 = jnp.where(first_head, y_heads[0], y_heads[1])
                y2 = y2 + ea_x[:, sl] * _dot(cmb, s2t.astype(BF16), _NN)
                y_ref[rows, sl] = y2 + d_ref[:, sl] * x2
                state[q] = s2t * ea_x[ln - 1:ln, sl] + _dot(bm_t, (xdt2 * te_x[:, sl]).astype(BF16), _NN)

    whole = lambda t: pl.BlockSpec(t.shape, lambda g, c: (0,) * t.ndim)
    step = cps * ln
    return pl.pallas_call(
        body,
        out_shape=(jax.ShapeDtypeStruct((s, di), F32),
                   jax.ShapeDtypeStruct((nc, g_n * pairs, n_n, pw), F32)),
        grid=(g_n, nc // cps),
        in_specs=[pl.BlockSpec((step, r_n * p_n), lambda g, c: (c, g)),
                  pl.BlockSpec((step, n_n), lambda g, c: (c, g)),
                  pl.BlockSpec((step, n_n), lambda g, c: (c, g)),
                  pl.BlockSpec((None, step, LANES), lambda g, c: (g, c, 0)),
                  pl.BlockSpec((None, 1, LANES), lambda g, c: (g, 0, 0)),
                  pl.BlockSpec((None, 1, r_n * p_n), lambda g, c: (g, 0, 0)),
                  whole(spread64)],
        out_specs=(pl.BlockSpec((step, r_n * p_n), lambda g, c: (c, g)),
                   pl.BlockSpec((cps, pairs, n_n, pw), lambda g, c: (c, g, 0, 0))),
        scratch_shapes=[pltpu.VMEM((pairs, n_n, pw), F32)],
        name=name, compiler_params=_cparams(("parallel", "arbitrary")),
    )(xs, bm, cm, dtp, a_g, d_x, spread64)


def _row8(v):
    return jnp.broadcast_to(v, (8, v.shape[1]))


def ssd_scan_bwd(xs, bm, cm, dtp, a_g, d_x, selectors, states, dy, *, heads_per_group, name):
    s, di = xs.shape
    g_n = SSD_N_GROUPS
    r_n, p_n, n_n, ln = heads_per_group, SSD_HEAD_DIM, SSD_D_STATE, SSD_CHUNK
    nc = s // ln
    cps = _pick(nc, (SSD_BWD_CHUNKS_PER_STEP, 2, 1))
    pairs, pw = r_n // 2, 2 * p_n
    spread64, pair_sum, row_sum = selectors

    def body(xs_ref, bm_ref, cm_ref, dt_ref, a_ref, d_ref, s64_ref, ps_ref, rs_ref, st_ref, dy_ref,
             dxs_ref, dbm_ref, dcm_ref, ddt_ref, dadt_ref, dd_ref, dstate, da_rows):
        c = pl.program_id(1)

        @pl.when(c == 0)
        def _():
            dstate[...] = jnp.zeros_like(dstate)
            dd_ref[...] = jnp.zeros_like(dd_ref)

        row = lax.broadcasted_iota(jnp.int32, (ln, ln), 0)
        col = lax.broadcasted_iota(jnp.int32, (ln, ln), 1)
        causal_t = col >= row
        upper = causal_t.astype(F32)
        first_head = lax.broadcasted_iota(jnp.int32, (1, pw), 1) < p_n
        for sub in reversed(range(cps)):
            rows = slice(sub * ln, (sub + 1) * ln)
            a_row, a_col, dt_x, ea_x, te_x, ea, causal = _ssd_chunk_setup(dt_ref[rows, :], a_ref[...], s64_ref[...])
            bmb = bm_ref[rows, :].astype(BF16)
            cm_f = cm_ref[rows, :]
            cmb = cm_f.astype(BF16)
            cm_t = cm_f.T.astype(BF16)
            scores = _dot(cmb, bmb, _NT)
            scores_t = _dot(bmb, cmb, _NT)
            e_last = ea[ln - 1:ln, :]
            da_rows[...] = jnp.zeros_like(da_rows)
            dscores = jnp.zeros((ln, ln), F32)
            dcm = jnp.zeros((ln, n_n), F32)
            dbm = jnp.zeros((ln, n_n), F32)
            da_cols = jnp.zeros((ln, LANES), F32)
            da_last = jnp.zeros((1, LANES), F32)
            ddt = jnp.zeros((ln, LANES), F32)
            dd = jnp.zeros((1, LANES), F32)
            for q in range(pairs):
                sl = slice(q * pw, (q + 1) * pw)
                sum2 = ps_ref[q]
                x2 = xs_ref[rows, sl]
                dt2 = dt_x[:, sl]
                xdt2 = x2 * dt2
                xdt2b = xdt2.astype(BF16)
                dy2 = dy_ref[rows, sl]
                dy2b = dy2.astype(BF16)
                dxdt_heads = []
                for h, r in enumerate((2 * q, 2 * q + 1)):
                    a_r = jnp.broadcast_to(a_col[:, r:r + 1], (ln, ln))
                    decay = jnp.exp(jnp.where(causal, a_r - a_row[r:r + 1, :], -jnp.inf))
                    decay_t = jnp.exp(jnp.where(causal_t, a_row[r:r + 1, :] - a_r, -jnp.inf))
                    dy_h = jnp.where(first_head if h == 0 else jnp.logical_not(first_head), dy2, 0.0).astype(BF16)
                    dm = _dot(dy_h, xdt2b, _NT)
                    dscores = dscores + dm * decay
                    e_mat = dm * (scores * decay)
                    da_cols = da_cols + _exact_dot(e_mat, rs_ref[r], 2)
                    da_rows[r:r + 1, :] = -jnp.sum(e_mat, axis=0, keepdims=True)
                    dxdt_heads.append(_dot((scores_t * decay_t).astype(BF16), dy2b, _NN))
                dxdt2 = jnp.where(first_head, dxdt_heads[0], dxdt_heads[1])
                s2t = st_ref[sub, q]
                s2tb = s2t.astype(BF16)
                ds2t = dstate[q]
                ds2tb = ds2t.astype(BF16)
                ea2, te2 = ea_x[:, sl], te_x[:, sl]
                y_off2 = ea2 * _dot(cmb, s2tb, _NN)
                dy_e2 = (dy2 * ea2).astype(BF16)
                dcm = dcm + _dot(dy_e2, s2tb, _NT)
                ds_in = _dot(cm_t, dy_e2, _NN)
                da_cols = da_cols + _exact_dot(dy2 * y_off2, sum2, 2)
                bds2 = _dot(bmb, ds2tb, _NN)
                dxdt2 = dxdt2 + te2 * bds2
                xdt_e2 = xdt2 * te2
                dbm = dbm + _dot(xdt_e2.astype(BF16), ds2tb, _NT)
                w_cols = _exact_dot(xdt_e2 * bds2, sum2, 2)
                da_cols = da_cols - w_cols
                state_dot = _exact_dot(_row8(jnp.sum(ds2t * s2t, axis=0, keepdims=True)), sum2, 2)[0:1]
                da_last = da_last + jnp.sum(w_cols, axis=0, keepdims=True) + e_last * state_dot
                dstate[q] = ds2t * ea_x[ln - 1:ln, sl] + ds_in
                dxs_ref[rows, sl] = dxdt2 * dt2 + d_ref[:, sl] * dy2
                ddt = ddt + _exact_dot(dxdt2 * x2, sum2, 2)
                dd = dd + _exact_dot(_row8(jnp.sum(dy2 * x2, axis=0, keepdims=True)), sum2, 2)[0:1]
            dcm_ref[rows, :] = dcm + _dot(dscores.astype(BF16), bmb, _NN)
            dbm_ref[rows, :] = dbm + _dot(dscores.T.astype(BF16), cmb, _NN)
            da_total = da_cols + da_rows[...].T
            dadt_ref[rows, :] = _dot(upper, da_total, _NN, _EXACT) + da_last
            ddt_ref[rows, :] = ddt
            dd_ref[...] += dd

    step, last_c = cps * ln, nc // cps - 1
    whole = lambda t: pl.BlockSpec(t.shape, lambda g, c: (0,) * t.ndim)
    return pl.pallas_call(
        body,
        out_shape=(jax.ShapeDtypeStruct((s, di), F32),
                   jax.ShapeDtypeStruct(bm.shape, F32),
                   jax.ShapeDtypeStruct(cm.shape, F32),
                   jax.ShapeDtypeStruct(dtp.shape, F32),
                   jax.ShapeDtypeStruct(dtp.shape, F32),
                   jax.ShapeDtypeStruct(a_g.shape, F32)),
        grid=(g_n, nc // cps),
        in_specs=[pl.BlockSpec((step, r_n * p_n), lambda g, c: (last_c - c, g)),
                  pl.BlockSpec((step, n_n), lambda g, c: (last_c - c, g)),
                  pl.BlockSpec((step, n_n), lambda g, c: (last_c - c, g)),
                  pl.BlockSpec((None, step, LANES), lambda g, c: (g, last_c - c, 0)),
                  pl.BlockSpec((None, 1, LANES), lambda g, c: (g, 0, 0)),
                  pl.BlockSpec((None, 1, r_n * p_n), lambda g, c: (g, 0, 0)),
                  whole(spread64), whole(pair_sum), whole(row_sum),
                  pl.BlockSpec((cps, pairs, n_n, pw), lambda g, c: (last_c - c, g, 0, 0)),
                  pl.BlockSpec((step, r_n * p_n), lambda g, c: (last_c - c, g))],
        out_specs=(pl.BlockSpec((step, r_n * p_n), lambda g, c: (last_c - c, g)),
                   pl.BlockSpec((step, n_n), lambda g, c: (last_c - c, g)),
                   pl.BlockSpec((step, n_n), lambda g, c: (last_c - c, g)),
                   pl.BlockSpec((None, step, LANES), lambda g, c: (g, last_c - c, 0)),
                   pl.BlockSpec((None, step, LANES), lambda g, c: (g, last_c - c, 0)),
                   pl.BlockSpec((None, 1, LANES), lambda g, c: (g, 0, 0))),
        scratch_shapes=[pltpu.VMEM((pairs, n_n, pw), F32), pltpu.VMEM((LANES, ln), F32)],
        name=name, compiler_params=_cparams(("parallel", "arbitrary")),
    )(xs, bm, cm, dtp, a_g, d_x, spread64, pair_sum, row_sum, states, dy)


SB_Q_TILE = 2048
SB_K_TILE = 256


def _tri_sum(x, tri):
    t = x.shape[0]
    hi = x.astype(BF16)
    r1 = x - hi.astype(F32)
    mid = r1.astype(BF16)
    lo = (r1 - mid.astype(F32)).astype(BF16)
    r = _dot(jnp.concatenate([hi, mid, lo], axis=0), tri, _NN)
    return r[:t] + r[t:2 * t] + r[2 * t:]


def _sb_logits(q, k_j, scale, strict):
    z = _dot(q, k_j, _NT) * scale
    sp = jnp.log(1.0 + jnp.exp(-jnp.abs(z)))
    log_b = jnp.minimum(z, 0.0) - sp
    log_1mb = log_b - z
    if strict is not None:
        log_1mb = jnp.where(strict, log_1mb, 0.0)
    return log_b, log_1mb


def _sb_tiles(s):
    tq = _pick(s, (SB_Q_TILE, 2 * SB_K_TILE, SB_K_TILE, LANES))
    return tq, min(tq, SB_K_TILE)


def _sb_diag_mask(rows, tk):
    return lax.broadcasted_iota(jnp.int32, (rows, tk), 1) < lax.broadcasted_iota(jnp.int32, (rows, tk), 0)


def _sb_iotas(t):
    row = lax.broadcasted_iota(jnp.int32, (t, t), 0)
    col = lax.broadcasted_iota(jnp.int32, (t, t), 1)
    return row, col


def sb_attn_fwd(qn, kn, v, *, v_off=0, name):
    s, w = qn.shape
    dh = SB_HEAD_DIM
    n_h = w // dh
    tq, tk = _sb_tiles(s)
    per = tq // tk
    scale = 1.0 / math.sqrt(dh)

    def body(q_ref, k_ref, v_ref, o_ref, tot_ref):
        i = pl.program_id(1)
        q = q_ref[...]
        row, col = _sb_iotas(tk)
        later = (row > col).astype(BF16)

        def tile(q_rows, j, acc, run, mask):
            s0 = pl.multiple_of(j * tk, tk)
            k_j = k_ref[pl.ds(s0, tk), :]
            v_j = v_ref[pl.ds(s0, tk), :].astype(BF16)
            log_b, log_1mb = _sb_logits(q_rows, k_j, scale, mask)
            att = jnp.exp(log_b + (_tri_sum(log_1mb, later) + run))
            if mask is not None:
                att = jnp.where(mask, att, 0.0)
            return acc + _dot(att.astype(BF16), v_j, _NN), run + jnp.sum(log_1mb, axis=1, keepdims=True)

        acc, run = jnp.zeros((tq, dh), F32), jnp.zeros((tq, 1), F32)
        for d in reversed(range(per)):
            r0 = d * tk
            a2, r2 = tile(q[r0:], i * per + d, acc[r0:], run[r0:], _sb_diag_mask(tq - r0, tk))
            acc = a2 if r0 == 0 else jnp.concatenate([acc[:r0], a2], axis=0)
            run = r2 if r0 == 0 else jnp.concatenate([run[:r0], r2], axis=0)

        def group(gg, c):
            for d in reversed(range(per)):
                c = tile(q, (i - 1 - gg) * per + d, c[0], c[1], None)
            return c

        acc, run = lax.fori_loop(0, i, group, (acc, run))
        o_ref[...] = acc
        tot_ref[...] = jnp.broadcast_to(run, (tq, dh))

    return pl.pallas_call(
        body,
        out_shape=(jax.ShapeDtypeStruct((s, w), F32), jax.ShapeDtypeStruct((s, w), F32)),
        grid=(n_h, s // tq),
        in_specs=[pl.BlockSpec((tq, dh), lambda h, i: (i, h)),
                  pl.BlockSpec((s, dh), lambda h, i: (0, h)),
                  pl.BlockSpec((s, dh), lambda h, i: (0, v_off + h))],
        out_specs=(pl.BlockSpec((tq, dh), lambda h, i: (i, h)),
                   pl.BlockSpec((tq, dh), lambda h, i: (i, h))),
        name=name, compiler_params=_cparams(("parallel", "parallel")),
    )(qn, kn, v)


def sb_attn_bwd(qn, kn, v, tot, do, *, v_off=0, name):
    s, w = qn.shape
    dh = SB_HEAD_DIM
    n_h = w // dh
    tq, tk = _sb_tiles(s)
    per = tq // tk
    scale = 1.0 / math.sqrt(dh)

    def body(q_ref, k_ref, v_ref, tot_ref, do_ref, dq_ref, dk_ref, dv_ref):
        dk_ref[...] = jnp.zeros_like(dk_ref)
        dv_ref[...] = jnp.zeros_like(dv_ref)
        row, col = _sb_iotas(tk)
        upto = (row <= col).astype(BF16)
        before = (row < col).astype(BF16)

        def q_block(i, _):
            t0 = pl.multiple_of(i * tq, tq)
            q = q_ref[pl.ds(t0, tq), :]
            do_i = do_ref[pl.ds(t0, tq), :].astype(BF16)
            total = tot_ref[pl.ds(t0, tq), 0:1]

            def tile(r0, j, dq, run_l, run_g, mask):
                s0 = pl.multiple_of(j * tk, tk)
                k_j = k_ref[pl.ds(s0, tk), :]
                v_j = v_ref[pl.ds(s0, tk), :].astype(BF16)
                q_r, do_r = q[r0:], do_i[r0:]
                log_b, log_1mb = _sb_logits(q_r, k_j, scale, mask)
                att = jnp.exp(log_b + ((total[r0:] - run_l) - _tri_sum(log_1mb, upto)))
                if mask is not None:
                    att = jnp.where(mask, att, 0.0)
                g = att * _dot(do_r, v_j, _NT)
                c = _tri_sum(g, before) + run_g
                dz = (g - (g + c) * jnp.exp(log_b)) * scale
                if mask is not None:
                    dz = jnp.where(mask, dz, 0.0)
                dz = dz.astype(BF16)
                dk_ref[pl.ds(s0, tk), :] += _dot(dz, q_r, _TN)
                dv_ref[pl.ds(s0, tk), :] += _dot(att.astype(BF16), do_r, _TN)
                return (dq + _dot(dz, k_j, _NN), run_l + jnp.sum(log_1mb, axis=1, keepdims=True),
                        run_g + jnp.sum(g, axis=1, keepdims=True))

            def group(gg, c):
                for d in range(per):
                    c = tile(0, gg * per + d, c[0], c[1], c[2], None)
                return c

            zero = jnp.zeros((tq, 1), F32)
            dq, run_l, run_g = lax.fori_loop(0, i, group, (jnp.zeros((tq, dh), F32), zero, zero))
            for d in range(per):
                r0 = d * tk
                p_dq, p_l, p_g = tile(r0, i * per + d, dq[r0:], run_l[r0:], run_g[r0:], _sb_diag_mask(tq - r0, tk))
                if r0 == 0:
                    dq, run_l, run_g = p_dq, p_l, p_g
                else:
                    dq = jnp.concatenate([dq[:r0], p_dq], axis=0)
                    run_l = jnp.concatenate([run_l[:r0], p_l], axis=0)
                    run_g = jnp.concatenate([run_g[:r0], p_g], axis=0)
            dq_ref[pl.ds(t0, tq), :] = dq
            return 0

        lax.fori_loop(0, s // tq, q_block, 0)

    head = pl.BlockSpec((s, dh), lambda h: (0, h))
    return pl.pallas_call(
        body,
        out_shape=tuple(jax.ShapeDtypeStruct((s, w), F32) for _ in range(3)),
        grid=(n_h,),
        in_specs=[head, head, pl.BlockSpec((s, dh), lambda h: (0, v_off + h)), head, head],
        out_specs=(head, head, head),
        name=name, compiler_params=_cparams(("parallel",)),
    )(qn, kn, v, tot, do)


ROW_TILE = 256
WIDE_ROW_TILE = 128


def _rows(width, col=0, tm=ROW_TILE):
    return pl.BlockSpec((tm, width), lambda i: (i, col))


_wide_rows = functools.partial(_rows, tm=WIDE_ROW_TILE)


def _whole(shape):
    return pl.BlockSpec(shape, lambda i: (0,) * len(shape))


def _ew_call(body, out_shape, in_specs, out_specs, args, n_rows, name, carried=False):
    return pl.pallas_call(
        body, out_shape=out_shape, grid=(n_rows // in_specs[0].block_shape[0],), in_specs=in_specs, out_specs=out_specs,
        name=name, compiler_params=_cparams(("arbitrary",) if carried else ("parallel",)),
    )(*args)


def _first_step(*refs):
    @pl.when(pl.program_id(0) == 0)
    def _():
        for r in refs:
            r[...] = jnp.zeros_like(r)


def rmsnorm_fwd(x, w, after=None, *, name):
    s, d = x.shape

    def body(x_ref, w_ref, *rest):
        o_ref = rest[-1]
        xv = x_ref[...]
        r = lax.rsqrt(jnp.mean(xv * xv, axis=-1, keepdims=True) + NORM_EPS)
        o_ref[...] = (xv * r * w_ref[...]).astype(BF16)

    extra = [] if after is None else [after]
    return _ew_call(body, jax.ShapeDtypeStruct((s, d), BF16),
                    [_rows(d), _whole((1, d))] + [_whole(TOKEN_SHAPE)] * len(extra), _rows(d),
                    (x, w.reshape(1, d), *extra), s, name)


def rmsnorm_bwd(x, w, dy, dres, *, name):
    s, d = x.shape

    def body(x_ref, w_ref, dy_ref, dr_ref, dx_ref, dw_ref):
        _first_step(dw_ref)
        xv = x_ref[...]
        r = lax.rsqrt(jnp.mean(xv * xv, axis=-1, keepdims=True) + NORM_EPS)
        xhat = xv * r
        dyv = dy_ref[...].astype(F32)
        dw_ref[...] += jnp.sum(dyv * xhat, axis=0, keepdims=True)
        g = dyv * w_ref[...]
        dx_ref[...] = dr_ref[...] + r * (g - xhat * jnp.mean(g * xhat, axis=-1, keepdims=True))

    return _ew_call(body, (jax.ShapeDtypeStruct((s, d), F32), jax.ShapeDtypeStruct((1, d), F32)),
                    [_rows(d), _whole((1, d)), _rows(d), _rows(d)], (_rows(d), _whole((1, d))),
                    (x, w.reshape(1, d), dy, dres), s, name, carried=True)


def ple_fwd(h1, gate_pre, pp, *, name):
    s, d = h1.shape

    def body(h_ref, g_ref, p_ref, o_ref):
        o_ref[...] = h_ref[...] + p_ref[...] * _sigmoid(g_ref[...])

    return _ew_call(body, jax.ShapeDtypeStruct((s, d), F32), [_rows(d)] * 3, _rows(d), (h1, gate_pre, pp), s, name)


def ple_bwd(dh2, gate_pre, pp, after, *, name):
    s, d = dh2.shape

    def body(dh_ref, g_ref, p_ref, after_ref, dp_ref, dg_ref):
        gate = _sigmoid(g_ref[...])
        dh = dh_ref[...]
        dp_ref[...] = (dh * gate).astype(BF16)
        dg_ref[...] = (dh * p_ref[...] * gate * (1.0 - gate)).astype(BF16)

    shp = jax.ShapeDtypeStruct((s, d), BF16)
    return _ew_call(body, (shp, shp), [_rows(d)] * 3 + [_whole(TOKEN_SHAPE)], (_rows(d), _rows(d)),
                    (dh2, gate_pre, pp, after), s, name)


def loss_head(y, target, *, name):
    s, d = y.shape

    def body(y_ref, t_ref, l_ref, dy_ref):
        _first_step(l_ref)
        err = y_ref[...] - t_ref[...]
        per_tok = jnp.mean(err * err, axis=-1, keepdims=True)
        l_ref[...] += 0.5 * jnp.sum(per_tok, axis=0, keepdims=True)
        dy_ref[...] = err * (1.0 / d)

    return _ew_call(body, (jax.ShapeDtypeStruct((1, 1), F32), jax.ShapeDtypeStruct((s, d), F32)),
                    [_rows(d), _rows(d)], (_whole((1, 1)), _rows(d)), (y, target), s, name, carried=True)


CONV_COL_TILE = 256


def _conv_taps(x, w_ref):
    row = lax.broadcasted_iota(jnp.int32, (x.shape[0], 1), 0)
    acc = x * w_ref[SSD_D_CONV - 1:SSD_D_CONV, :]
    shifted = []
    for d in range(1, SSD_D_CONV):
        xs = jnp.where(row >= d, pltpu.roll(x, d, 0), 0.0)
        shifted.append(xs)
        acc = acc + xs * w_ref[SSD_D_CONV - 1 - d:SSD_D_CONV - d, :]
    return acc, shifted


def ssd_conv_fwd(x, w, b, *, name):
    s, c = x.shape
    tc = _pick(c, (CONV_COL_TILE, LANES))

    def body(x_ref, w_ref, b_ref, o_ref):
        pre, _ = _conv_taps(x_ref[...], w_ref)
        o_ref[...] = _silu(pre + b_ref[...])

    col = pl.BlockSpec((s, tc), lambda j: (0, j))
    return pl.pallas_call(
        body, out_shape=jax.ShapeDtypeStruct((s, c), F32), grid=(c // tc,),
        in_specs=[col, pl.BlockSpec((SSD_D_CONV, tc), lambda j: (0, j)), pl.BlockSpec((1, tc), lambda j: (0, j))],
        out_specs=col, name=name, compiler_params=_cparams(("parallel",)),
    )(x, w, b)


def ssd_conv_bwd(x, w, b, dact, *, name):
    s, c = x.shape
    tc = _pick(c, (CONV_COL_TILE, LANES))

    def body(x_ref, w_ref, b_ref, da_ref, dx_ref, dw_ref, db_ref):
        xv = x_ref[...]
        pre, shifted = _conv_taps(xv, w_ref)
        dpre = da_ref[...] * _silu_grad(pre + b_ref[...])
        db_ref[...] = jnp.sum(dpre, axis=0, keepdims=True)
        row = lax.broadcasted_iota(jnp.int32, (s, 1), 0)
        dx = dpre * w_ref[SSD_D_CONV - 1:SSD_D_CONV, :]
        dw_ref[SSD_D_CONV - 1:SSD_D_CONV, :] = jnp.sum(dpre * xv, axis=0, keepdims=True)
        for d in range(1, SSD_D_CONV):
            k = SSD_D_CONV - 1 - d
            dw_ref[k:k + 1, :] = jnp.sum(dpre * shifted[d - 1], axis=0, keepdims=True)
            up = jnp.where(row < s - d, pltpu.roll(dpre, s - d, 0), 0.0)
            dx = dx + up * w_ref[k:k + 1, :]
        dx_ref[...] = dx.astype(BF16)

    col = pl.BlockSpec((s, tc), lambda j: (0, j))
    wspec = pl.BlockSpec((SSD_D_CONV, tc), lambda j: (0, j))
    bspec = pl.BlockSpec((1, tc), lambda j: (0, j))
    return pl.pallas_call(
        body,
        out_shape=(jax.ShapeDtypeStruct((s, c), BF16), jax.ShapeDtypeStruct((SSD_D_CONV, c), F32),
                   jax.ShapeDtypeStruct((1, c), F32)),
        grid=(c // tc,), in_specs=[col, wspec, bspec, col], out_specs=(col, wspec, bspec),
        name=name, compiler_params=_cparams(("parallel",)),
    )(x, w, b, dact)


def ssd_dt_fwd(dt_raw, bias, a_log, *, name):
    s, h = dt_raw.shape

    def body(r_ref, b_ref, al_ref, dt_ref, a_ref):
        zv = r_ref[...] + b_ref[...]
        dt_ref[...] = jnp.maximum(zv, 0.0) + jnp.log(1.0 + jnp.exp(-jnp.abs(zv)))
        a_ref[...] = -jnp.exp(al_ref[...])

    full = pl.BlockSpec((s, h), lambda: (0, 0))
    vec = pl.BlockSpec((1, h), lambda: (0, 0))
    return pl.pallas_call(
        body, out_shape=(jax.ShapeDtypeStruct((s, h), F32), jax.ShapeDtypeStruct((1, h), F32)),
        in_specs=[full, vec, vec], out_specs=(full, vec), name=name, compiler_params=_cparams(),
    )(dt_raw, bias.reshape(1, h), a_log.reshape(1, h))


def ssd_dt_bwd(dt_raw, bias, a_log, dt, ddt, dadt, *, name):
    s, h = dt_raw.shape

    def body(r_ref, b_ref, al_ref, dt_ref, ddt_ref, dadt_ref, dr_ref, db_ref, dal_ref):
        a = -jnp.exp(al_ref[...])
        dadt_v = dadt_ref[...]
        d_dt = ddt_ref[...] + a * dadt_v
        d_raw = d_dt * _sigmoid(r_ref[...] + b_ref[...])
        dr_ref[...] = d_raw
        db_ref[...] = jnp.sum(d_raw, axis=0, keepdims=True)
        dal_ref[...] = jnp.sum(dadt_v * dt_ref[...], axis=0, keepdims=True) * a

    full = pl.BlockSpec((s, h), lambda: (0, 0))
    vec = pl.BlockSpec((1, h), lambda: (0, 0))
    return pl.pallas_call(
        body, out_shape=(jax.ShapeDtypeStruct((s, h), F32), jax.ShapeDtypeStruct((1, h), F32),
                         jax.ShapeDtypeStruct((1, h), F32)),
        in_specs=[full, vec, vec, full, full, full], out_specs=(full, vec, vec), name=name,
        compiler_params=_cparams(),
    )(dt_raw, bias.reshape(1, h), a_log.reshape(1, h), dt, ddt, dadt)


def _group_mean(v, n_groups):
    gw = v.shape[-1] // n_groups
    parts = [jnp.broadcast_to(jnp.mean(v[:, k * gw:(k + 1) * gw], axis=-1, keepdims=True), (v.shape[0], gw))
             for k in range(n_groups)]
    return jnp.concatenate(parts, axis=-1)


def ssd_gate_fwd(y, z, gw, *, name):
    s, di = y.shape

    def body(y_ref, z_ref, w_ref, o_ref):
        yg = y_ref[...] * _silu(z_ref[...])
        r = lax.rsqrt(_group_mean(yg * yg, SSD_N_GROUPS) + GATED_NORM_EPS)
        o_ref[...] = (yg * r * w_ref[...]).astype(BF16)

    return _ew_call(body, jax.ShapeDtypeStruct((s, di), BF16), [_wide_rows(di), _wide_rows(di), _whole((1, di))],
                    _wide_rows(di), (y, z, gw.reshape(1, di)), s, name)


def ssd_gate_bwd(y, z, gw, dyn, *, name):
    s, di = y.shape

    def body(y_ref, z_ref, w_ref, dn_ref, dy_ref, dz_ref, dw_ref):
        _first_step(dw_ref)
        yv, zv = y_ref[...], z_ref[...]
        sz = _silu(zv)
        yg = yv * sz
        r = lax.rsqrt(_group_mean(yg * yg, SSD_N_GROUPS) + GATED_NORM_EPS)
        yhat = yg * r
        dn = dn_ref[...]
        dw_ref[...] += jnp.sum(dn * yhat, axis=0, keepdims=True)
        g = dn * w_ref[...]
        dyg = r * (g - yhat * _group_mean(g * yhat, SSD_N_GROUPS))
        dy_ref[...] = dyg * sz
        dz_ref[...] = (dyg * yv * _silu_grad(zv)).astype(BF16)

    return _ew_call(body, (jax.ShapeDtypeStruct((s, di), F32), jax.ShapeDtypeStruct((s, di), BF16),
                           jax.ShapeDtypeStruct((1, di), F32)),
                    [_wide_rows(di), _wide_rows(di), _whole((1, di)), _wide_rows(di)],
                    (_wide_rows(di), _wide_rows(di), _whole((1, di))),
                    (y, z, gw.reshape(1, di), dyn), s, name, carried=True)


def _head_mean(v):
    return _group_mean(v, v.shape[-1] // SB_HEAD_DIM)


def sb_qk_fwd(proj, qw, kw, *, name):
    s, w4 = proj.shape
    w = w4 // 4
    reps = w // SB_HEAD_DIM

    def body(q_ref, k_ref, qw_ref, kw_ref, qn_ref, kn_ref):
        for x_ref, w_ref, o_ref in ((q_ref, qw_ref, qn_ref), (k_ref, kw_ref, kn_ref)):
            xv = x_ref[...]
            r = lax.rsqrt(_head_mean(xv * xv) + NORM_EPS)
            o_ref[...] = (xv * r * jnp.tile(w_ref[...], (1, reps))).astype(BF16)

    shp = jax.ShapeDtypeStruct((s, w), BF16)
    return _ew_call(body, (shp, shp), [_rows(w, 0), _rows(w, 1), _whole((1, SB_HEAD_DIM)), _whole((1, SB_HEAD_DIM))],
                    (_rows(w), _rows(w)), (proj, proj, qw.reshape(1, -1), kw.reshape(1, -1)), s, name)


def sb_gate_fwd(o, proj, *, name):
    s, w = o.shape

    def body(o_ref, g_ref, og_ref):
        og_ref[...] = (o_ref[...] * _silu(g_ref[...])).astype(BF16)

    return _ew_call(body, jax.ShapeDtypeStruct((s, w), BF16), [_rows(w), _rows(w, 3)], _rows(w), (o, proj), s, name)


def sb_gate_bwd(dog, o, proj, *, name):
    s, w = o.shape

    def body(d_ref, o_ref, g_ref, do_ref, dg_ref):
        gv, dv = g_ref[...], d_ref[...]
        do_ref[...] = dv * _silu(gv)
        dg_ref[...] = (dv * o_ref[...] * _silu_grad(gv)).astype(BF16)

    return _ew_call(body, (jax.ShapeDtypeStruct((s, w), F32), jax.ShapeDtypeStruct((s, w), BF16)),
                    [_rows(w), _rows(w), _rows(w, 3)], (_rows(w), _rows(w)), (dog, o, proj), s, name)


def sb_pack_bwd(proj, qw, kw, dqn, dkn, dv, dg, *, name):
    s, w4 = proj.shape
    w = w4 // 4
    reps = w // SB_HEAD_DIM

    def body(q_ref, k_ref, qw_ref, kw_ref, dqn_ref, dkn_ref, dv_ref, dg_ref, dp_ref, dqw_ref, dkw_ref):
        _first_step(dqw_ref, dkw_ref)
        for idx, (x_ref, w_ref, d_ref, dw_ref) in enumerate(((q_ref, qw_ref, dqn_ref, dqw_ref),
                                                           (k_ref, kw_ref, dkn_ref, dkw_ref))):
            xv = x_ref[...]
            r = lax.rsqrt(_head_mean(xv * xv) + NORM_EPS)
            xhat = xv * r
            dn = d_ref[...]
            per_col = jnp.sum(dn * xhat, axis=0, keepdims=True)
            acc = per_col[:, 0:SB_HEAD_DIM]
            for hh in range(1, reps):
                acc = acc + per_col[:, hh * SB_HEAD_DIM:(hh + 1) * SB_HEAD_DIM]
            dw_ref[...] += acc
            g = dn * jnp.tile(w_ref[...], (1, reps))
            dp_ref[:, idx * w:(idx + 1) * w] = (r * (g - xhat * _head_mean(g * xhat))).astype(BF16)
        dp_ref[:, 2 * w:3 * w] = dv_ref[...].astype(BF16)
        dp_ref[:, 3 * w:4 * w] = dg_ref[...]

    vec = _whole((1, SB_HEAD_DIM))
    return _ew_call(body, (jax.ShapeDtypeStruct((s, w4), BF16), jax.ShapeDtypeStruct((1, SB_HEAD_DIM), F32),
                           jax.ShapeDtypeStruct((1, SB_HEAD_DIM), F32)),
                    [_wide_rows(w, 0), _wide_rows(w, 1), vec, vec, _wide_rows(w), _wide_rows(w), _wide_rows(w),
                     _wide_rows(w)],
                    (_wide_rows(w4), vec, vec),
                    (proj, proj, qw.reshape(1, -1), kw.reshape(1, -1), dqn, dkn, dv, dg), s, name, carried=True)


_HBM = pl.BlockSpec(memory_space=pltpu.HBM)


def _mesh_pos():
    return lax.axis_index("x"), lax.axis_index("y"), lax.axis_index("c")


def _other_chips(x, y):
    return [(1 - x, y), (x, 1 - y), (1 - x, 1 - y)]


_SEM = pl.BlockSpec(memory_space=pltpu.SEMAPHORE)
_ANY = pl.BlockSpec(memory_space=pl.ANY)
_DATAFLOW = pltpu.SideEffectType.DATAFLOW_SIDE_EFFECTING
N_PEER_CHIPS = N_CHIP - 1
TOKEN_SHAPE = (8, LANES)


def _in_hbm(t):
    return pltpu.with_memory_space_constraint(t, pltpu.HBM)


def _ici_copies(kind, src_refs, land_refs, send_sems, recv_sems, arrivals=False):
    x, y, c = _mesh_pos()
    out = []
    for a in range(len(land_refs)):
        if kind in ("pass", "swap"):
            if kind == "pass":
                src, dst = land_refs[a].at[:, c], land_refs[a].at[:, 1 - c if arrivals else c]
            else:
                src, dst = src_refs[a].at[:, 1 - c], land_refs[a]
            out.append(pltpu.make_async_remote_copy(
                src_ref=src, dst_ref=dst, send_sem=send_sems.at[a], recv_sem=recv_sems.at[a],
                device_id=(x, y, 1 - c), device_id_type=MESH))
            continue
        for j, chip in enumerate(_other_chips(x, y)):
            if kind == "gather":
                src = land_refs[a].at[4 * x + 2 * y + c]
                dst = land_refs[a].at[4 * chip[0] + 2 * chip[1] + c] if arrivals else src
            else:
                src, dst = src_refs[a].at[2 * chip[0] + chip[1]], land_refs[a].at[j]
            k = a * N_PEER_CHIPS + j
            out.append(pltpu.make_async_remote_copy(
                src_ref=src, dst_ref=dst, send_sem=send_sems.at[k], recv_sem=recv_sems.at[k],
                device_id=(*chip, c), device_id_type=MESH))
    return out


def _n_copies(kind, lands):
    return len(lands) * (1 if kind in ("pass", "swap") else N_PEER_CHIPS)


def ici_start(kind, srcs, lands, after=(), *, name):
    ns, nb = len(srcs), len(srcs) + len(lands)
    n_sem = _n_copies(kind, lands)

    def body(*refs):
        first_out = nb + len(after)
        for cp in _ici_copies(kind, refs[:ns], refs[ns:nb], refs[first_out], refs[first_out + 1]):
            cp.start()
        refs[-1][...] = jnp.zeros(TOKEN_SHAPE, F32)

    outs = pl.pallas_call(
        body, name=name,
        out_shape=(pltpu.SemaphoreType.DMA((n_sem,)), pltpu.SemaphoreType.DMA((n_sem,)),
                   *[pltpu.HBM(t.shape, t.dtype) for t in (*srcs, *lands)], jax.ShapeDtypeStruct(TOKEN_SHAPE, F32)),
        in_specs=[_HBM] * nb + [_ANY] * len(after),
        out_specs=(_SEM, _SEM, *([_HBM] * nb), pl.BlockSpec(memory_space=pltpu.VMEM)),
        input_output_aliases={k: 2 + k for k in range(nb)},
        compiler_params=pltpu.CompilerParams(has_side_effects=_DATAFLOW),
    )(*[_in_hbm(t) for t in (*srcs, *lands)], *after)
    return outs[0], outs[1], list(outs[2:2 + ns]), list(outs[2 + ns:2 + nb]), outs[-1]


def ici_wait(kind, started, after, *, name):
    send_sems, recv_sems, srcs, lands, _ = started
    ns, nb = len(srcs), len(srcs) + len(lands)

    def body(*refs):
        for cp in _ici_copies(kind, refs[:ns], refs[ns:nb], refs[nb], refs[nb + 1]):
            cp.wait_send()
        for cp in _ici_copies(kind, refs[:ns], refs[ns:nb], refs[nb], refs[nb + 1], arrivals=True):
            cp.wait_recv()

    outs = pl.pallas_call(
        body, name=name,
        out_shape=tuple(pltpu.HBM(t.shape, t.dtype) for t in (*srcs, *lands)),
        in_specs=[_HBM] * nb + [_SEM, _SEM] + [_ANY] * len(after),
        out_specs=tuple([_HBM] * nb),
        input_output_aliases={k: k for k in range(nb)},
        compiler_params=pltpu.CompilerParams(has_side_effects=_DATAFLOW),
    )(*srcs, *lands, send_sems, recv_sems, *after)
    return list(outs[:ns]), list(outs[ns:])


def all_reduce_small(v, *, name):
    r = v.shape[0]

    def body(v_ref, o_ref, buf, send_sems, recv_sems):
        x, y, c = _mesh_pos()
        me = 4 * x + 2 * y + c
        buf[me] = v_ref[...]
        copies = []
        for k in range(1, N_DEV):
            to = ((x + (k >> 2)) % 2, (y + ((k >> 1) & 1)) % 2, (c + (k & 1)) % 2)
            copies.append(pltpu.make_async_remote_copy(
                src_ref=v_ref, dst_ref=buf.at[me], send_sem=send_sems.at[k - 1], recv_sem=recv_sems.at[k - 1],
                device_id=to, device_id_type=MESH))
        for cp in copies:
            cp.start()
        for cp in copies:
            cp.wait()
        acc = buf[0]
        for d in range(1, N_DEV):
            acc = acc + buf[d]
        o_ref[...] = acc

    vm = pl.BlockSpec(memory_space=pltpu.VMEM)
    return pl.pallas_call(
        body, out_shape=jax.ShapeDtypeStruct(v.shape, F32), in_specs=[vm], out_specs=vm,
        scratch_shapes=[pltpu.VMEM((N_DEV, r, LANES), F32), pltpu.SemaphoreType.DMA((N_DEV - 1,)),
                        pltpu.SemaphoreType.DMA((N_DEV - 1,))],
        name=name,
    )(v)


def pair_add(g, r1, core, *, name):
    _, _, rows, cols = g.shape
    tm = _pick(rows, (256, 128))

    def body(c_ref, g_ref, r_ref, o_ref):
        o_ref[...] = (g_ref[...].astype(F32) + r_ref[...].astype(F32)).astype(o_ref.dtype)

    return pl.pallas_call(
        body, out_shape=jax.ShapeDtypeStruct(r1.shape, g.dtype),
        grid_spec=pltpu.PrefetchScalarGridSpec(
            num_scalar_prefetch=1, grid=(N_CHIP, rows // tm),
            in_specs=[pl.BlockSpec((None, None, tm, cols), lambda k, i, c_ref: (k, c_ref[0], i, 0)),
                      pl.BlockSpec((None, tm, cols), lambda k, i, c_ref: (k, i, 0))],
            out_specs=pl.BlockSpec((None, tm, cols), lambda k, i, c_ref: (k, i, 0))),
        name=name, compiler_params=_cparams(("parallel", "parallel")),
    )(core, g, r1)


def _adamw_math(w, g, m, v):
    m = ADAM_B1 * m + (1.0 - ADAM_B1) * g
    v = ADAM_B2 * v + (1.0 - ADAM_B2) * (g * g)
    m_hat = m / (1.0 - ADAM_B1 ** ADAM_STEP)
    v_hat = v / (1.0 - ADAM_B2 ** ADAM_STEP)
    delta = -ADAM_LR * (m_hat / (jnp.sqrt(v_hat) + ADAM_EPS) + ADAM_WD * w)
    return delta, m, v


def adamw_sharded(w, m, v, layer, chip_sums, received, chip, into, *, name):
    _, rows, cols = w.shape
    tm = _pick(rows, (256, 128))

    def body(k_ref, w_ref, m_ref, v_ref, t_ref, r_ref, *rest):
        g_ref, d_ref, nm_ref, nv_ref, token_ref = rest[-5:]
        g = t_ref[...].astype(F32)
        for j in range(N_CHIP - 1):
            g = g + r_ref[j].astype(F32)
        d, mm, vv = _adamw_math(w_ref[...], g, m_ref[...], v_ref[...])
        g_ref[...] = g
        d_ref[...] = d
        nm_ref[...] = mm
        nv_ref[...] = vv
        token_ref[...] = jnp.zeros(TOKEN_SHAPE, F32)

    blk = pl.BlockSpec((None, tm, cols), lambda i, k_ref: (layer, i, 0))
    shp = jax.ShapeDtypeStruct(w.shape, F32)
    in_specs = [blk, blk, blk,
                pl.BlockSpec((None, tm, cols), lambda i, k_ref: (k_ref[0], i, 0)),
                pl.BlockSpec((N_CHIP - 1, tm, cols), lambda i, k_ref: (0, i, 0))]
    operands = [chip, w, m, v, chip_sums, received]
    aliases = {}
    if into is not None:
        aliases = {len(operands) + q: q for q in range(4)}
        in_specs += [_ANY] * 4
        operands += list(into)
    outs = pl.pallas_call(
        body, out_shape=(shp, shp, shp, shp, jax.ShapeDtypeStruct(TOKEN_SHAPE, F32)),
        grid_spec=pltpu.PrefetchScalarGridSpec(
            num_scalar_prefetch=1, grid=(rows // tm,), in_specs=in_specs,
            out_specs=(blk, blk, blk, blk, pl.BlockSpec(TOKEN_SHAPE, lambda i, k_ref: (0, 0)))),
        input_output_aliases=aliases,
        name=name, compiler_params=_cparams(("arbitrary",)),
    )(*operands)
    return outs[:4], outs[4]


def adamw_replicated(w, m, v, g, *, name):
    def body(w_ref, m_ref, v_ref, g_ref, d_ref, nm_ref, nv_ref):
        d, mm, vv = _adamw_math(w_ref[...], g_ref[...], m_ref[...], v_ref[...])
        d_ref[...] = d
        nm_ref[...] = mm
        nv_ref[...] = vv

    shp = jax.ShapeDtypeStruct(w.shape, F32)
    return pl.pallas_call(body, out_shape=(shp, shp, shp), name=name, compiler_params=_cparams())(w, m, v, g)


WEIGHT_NAMES = ("norm_w", "ssd_in_w", "ssd_conv_w", "ssd_conv_b", "ssd_dt_bias", "ssd_a_log", "ssd_d",
                "ssd_gnorm_w", "ssd_out_w", "sb_in_w", "sb_qn_w", "sb_kn_w", "sb_out_w", "ple_norm_w",
                "ple_gate_w", "ple_proj_w")
REPLICATED = ("norm_w", "ssd_conv_b", "ssd_dt_bias", "ssd_a_log", "ssd_d", "ssd_gnorm_w", "sb_qn_w", "sb_kn_w",
              "ple_norm_w")
PACK_ROWS = 8


def _pack(parts):
    flat = jnp.concatenate([t.reshape(-1) for t in parts])
    pad = (-flat.shape[0]) % (PACK_ROWS * LANES)
    return jnp.pad(flat, (0, pad)).reshape(-1, LANES)


def _unpack(packed, like):
    flat = packed.reshape(-1)
    out, off = [], 0
    for t in like:
        out.append(flat[off:off + t.size].reshape(t.shape))
        off += t.size
    return out


def _to_group_lanes(v, r):
    t = v.reshape(v.shape[0], SSD_N_GROUPS, r).transpose(1, 0, 2)
    return jnp.pad(t, ((0, 0), (0, 0), (0, LANES - r)))


def _from_group_lanes(t, r):
    return t[:, :, :r].transpose(1, 0, 2).reshape(t.shape[1], SSD_N_GROUPS * r)


def _head_vec(v, r):
    return jnp.pad(v.reshape(SSD_N_GROUPS, 1, r), ((0, 0), (0, 0), (0, LANES - r)))


def _col_blocks(full):
    rows = full.shape[0]
    return full.reshape(rows, N_DEV, -1).transpose(1, 0, 2)


def _from_col_blocks(blocks):
    return blocks.transpose(1, 0, 2).reshape(blocks.shape[1], -1)


def _split_cols(full, widths):
    out, off = [], 0
    for w in widths:
        out.append(full[:, off:off + w])
        off += w
    return out


def kernel(x, p, norm_w, ssd_in_w, ssd_conv_w, ssd_conv_b, ssd_dt_bias, ssd_a_log, ssd_d, ssd_gnorm_w, ssd_out_w, sb_in_w, sb_qn_w, sb_kn_w, sb_out_w, ple_norm_w, ple_gate_w, ple_proj_w, loss_target, m_norm_w, m_ssd_in_w, m_ssd_conv_w, m_ssd_conv_b, m_ssd_dt_bias, m_ssd_a_log, m_ssd_d, m_ssd_gnorm_w, m_ssd_out_w, m_sb_in_w, m_sb_qn_w, m_sb_kn_w, m_sb_out_w, m_ple_norm_w, m_ple_gate_w, m_ple_proj_w, v_norm_w, v_ssd_in_w, v_ssd_conv_w, v_ssd_conv_b, v_ssd_dt_bias, v_ssd_a_log, v_ssd_d, v_ssd_gnorm_w, v_ssd_out_w, v_sb_in_w, v_sb_qn_w, v_sb_kn_w, v_sb_out_w, v_ple_norm_w, v_ple_gate_w, v_ple_proj_w):
    env = dict(locals())
    wts = {n: env[n] for n in WEIGHT_NAMES}
    mom1 = {n: env["m_" + n] for n in WEIGHT_NAMES}
    mom2 = {n: env["v_" + n] for n in WEIGHT_NAMES}

    s, d = x.shape[1], x.shape[2]
    depth = norm_w.shape[0]
    di = ssd_out_w.shape[1] * N_DEV
    n_heads = ssd_dt_bias.shape[1]
    hpg = n_heads // SSD_N_GROUPS
    nbc = SSD_N_GROUPS * SSD_D_STATE
    in_segs = (di, di, nbc, nbc, n_heads)
    conv_segs = (di, nbc, nbc)
    sb_w = sb_out_w.shape[1] * N_DEV
    selectors = ssd_selectors(hpg)
    xi, yi, ci = _mesh_pos()
    core = ci.astype(jnp.int32).reshape(1)
    chip = (2 * xi + yi).astype(jnp.int32).reshape(1)

    def layer_keys(i):
        j = i // 2
        mixer = [("ssd_in_w", j), ("ssd_conv_w", j), ("ssd_out_w", j)] if i % 2 == 0 else [("sb_in_w", j), ("sb_out_w", j)]
        return mixer + [("ple_gate_w", i), ("ple_proj_w", i)]

    me_block = 4 * xi + 2 * yi + ci

    def landing_zone(t):
        return lax.dynamic_update_index_in_dim(lax.empty((N_DEV,) + t.shape, t.dtype), t, me_block, 0)

    def groups(i):
        keys = layer_keys(i)
        return [keys[:2], keys[2:]] if i == 0 else [keys]

    gathers, prev = {}, []
    for i in range(depth):
        for q, keys in enumerate(groups(i)):
            shards = [wts[n][idx] for n, idx in keys]
            if prev:
                shards = lax.optimization_barrier((prev[0], shards))[1]
            lands = [landing_zone(t if n == "ssd_conv_w" else t.astype(BF16)) for (n, _), t in zip(keys, shards)]
            gathers[i, q] = ici_start("gather", [], lands, after=prev, name=f"ag{i}{'ab'[q]}_start")
            prev = [gathers[i, q][4]]
    all_started = prev[0]
    full, ssd_full, passing = {}, {}, {}

    def hand_over(i, q, after):
        _, lands = ici_wait("gather", gathers[i, q], after, name=f"ag{i}{'ab'[q]}_wait")
        passing[i, q] = ici_start("pass", [], [t.reshape(N_CHIP, 2, *t.shape[1:]) for t in lands],
                                  name=f"ag{i}{'ab'[q]}_pass_start")

    def arrive(i, q, after):
        _, lands = ici_wait("pass", passing[i, q], after, name=f"ag{i}{'ab'[q]}_pass_wait")
        for k, t in zip(groups(i)[q], lands):
            full[k] = t.reshape(N_DEV, *t.shape[2:])

    def w_out_of(i):
        return full["ssd_out_w", i // 2].reshape(di, d) if i % 2 == 0 else full["sb_out_w", i // 2].reshape(sb_w, d)

    h = x.reshape(s, d)
    saved = []
    hand_over(0, 0, [all_started])
    arrive(0, 0, [all_started])
    for i in range(depth):
        j = i // 2
        if i > 0:
            arrive(i, 0, [h])
        sv = dict(h_in=h)
        u = rmsnorm_fwd(h, norm_w[i], name=f"l{i}_norm")
        sv["u"] = u
        if i % 2 == 0:
            fw = ssd_full[j] = dict(
                w_in=_split_cols(_from_col_blocks(full["ssd_in_w", j]), in_segs),
                conv_w=_split_cols(_from_col_blocks(full["ssd_conv_w", j]), conv_segs),
                conv_b=_split_cols(ssd_conv_b[j].reshape(1, -1), conv_segs))
            raw = [matmul(u, wseg, name=f"l{i}_in{q}") for q, wseg in enumerate(fw["w_in"])]
            if i == 0:
                hand_over(0, 1, [raw[4]])
            z, dt_raw = raw[0], raw[4]
            act = [ssd_conv_fwd(raw[1 + q], fw["conv_w"][q], fw["conv_b"][q], name=f"l{i}_conv{q}") for q in range(3)]
            dt, a_neg = ssd_dt_fwd(dt_raw, ssd_dt_bias[j], ssd_a_log[j], name=f"l{i}_dt")
            dtp = _to_group_lanes(dt, hpg)
            a_g = _head_vec(a_neg.reshape(-1), hpg)
            d_x = jnp.repeat(ssd_d[j].reshape(SSD_N_GROUPS, 1, hpg), SSD_HEAD_DIM, axis=2)
            y, states = ssd_scan_fwd(act[0], act[1], act[2], dtp, a_g, d_x, selectors, heads_per_group=hpg,
                                     name=f"l{i}_scan")
            yn = ssd_gate_fwd(y, z, ssd_gnorm_w[j], name=f"l{i}_gate")
            if i == 0:
                arrive(0, 1, [yn])
            h1 = matmul(yn, w_out_of(i), res=h, name=f"l{i}_out")
            sv.update(raw=raw, act=act, dt=dt, dtp=dtp, a_g=a_g, d_x=d_x, y=y, states=states, yn=yn)
        else:
            proj = matmul(u, full["sb_in_w", j], name=f"l{i}_in")
            qn, kn = sb_qk_fwd(proj, sb_qn_w[j], sb_kn_w[j], name=f"l{i}_qknorm")
            v_off = 2 * sb_w // SB_HEAD_DIM
            o, tot = sb_attn_fwd(qn, kn, proj, v_off=v_off, name=f"l{i}_attn")
            og = sb_gate_fwd(o, proj, name=f"l{i}_gate")
            h1 = matmul(og, w_out_of(i), res=h, name=f"l{i}_out")
            sv.update(proj=proj, qn=qn, kn=kn, o=o, tot=tot, og=og, v_off=v_off)
        if i + 1 < depth:
            hand_over(i + 1, 0, [h1])
        t = rmsnorm_fwd(h1, ple_norm_w[i], passing[i + 1, 0][4] if i + 1 < depth else None, name=f"l{i}_plenorm")
        gate_pre = matmul(t, full["ple_gate_w", i].reshape(d, d), name=f"l{i}_plegate")
        pp = matmul(p[i, 0], full["ple_proj_w", i], name=f"l{i}_pleproj")
        h = ple_fwd(h1, gate_pre, pp, name=f"l{i}_ple")
        sv.update(h1=h1, t=t, gate_pre=gate_pre, pp=pp)
        saved.append(sv)

    loss_part, dh = loss_head(h, loss_target.reshape(s, d), name="loss_head")
    loss = lax.psum(loss_part[0, 0], ("x", "y", "c"))

    big = {}
    small = {n: [None] * wts[n].shape[0] for n in REPLICATED}
    swaps, scatters = {}, {}
    order_after = jnp.zeros(TOKEN_SHAPE, F32)
    pending = None

    def send_to_sibling(i, q):
        blocks = [big[k].reshape(N_CHIP, 2, *big[k].shape[1:]) for k in groups(i)[::-1][q]]
        swaps[i, q] = ici_start("swap", blocks, [lax.empty((N_CHIP,) + t.shape[2:], t.dtype) for t in blocks],
                                name=f"rs{i}{'ab'[q]}_swap_start")
        return swaps[i, q][4]

    def send_to_chips(i, q, after):
        blocks, from_sibling = ici_wait("swap", swaps[i, q], after, name=f"rs{i}{'ab'[q]}_swap_wait")
        sums = [pair_add(g, r1, core, name=f"rs{i}{'ab'[q]}_pair_add{a}")
                for a, (g, r1) in enumerate(zip(blocks, from_sibling))]
        scatters[i, q] = ici_start("scatter", sums, [lax.empty((N_PEER_CHIPS,) + t.shape[1:], t.dtype) for t in sums],
                                   name=f"rs{i}{'ab'[q]}_start")
        return scatters[i, q][4]

    for i in reversed(range(depth)):
        j = i // 2
        sv = saved[i]
        dpp, dgp = ple_bwd(dh, sv["gate_pre"], sv["pp"], order_after, name=f"b{i}_ple")
        big["ple_proj_w", i] = matmul(p[i, 0], dpp, mode="tn", out_dtype=BF16, out_blocks=ple_proj_w.shape[2],
                                      name=f"b{i}_pleproj_w")
        big["ple_gate_w", i] = matmul(sv["t"], dgp, mode="tn", out_dtype=BF16, name=f"b{i}_plegate_w").reshape(N_DEV, -1, d)
        dt_ = matmul(dgp, full["ple_gate_w", i].reshape(d, d), mode="nt", name=f"b{i}_plegate_x")
        dh1, g_pn = rmsnorm_bwd(sv["h1"], ple_norm_w[i], dt_, dh, name=f"b{i}_plenorm")
        small["ple_norm_w"][i] = g_pn
        behind = send_to_chips(*pending, [dh1]) if pending is not None else None
        pending = None
        u = sv["u"]
        if i % 2 == 0:
            fw = ssd_full[j]
            raw, act = sv["raw"], sv["act"]
            big["ssd_out_w", j] = matmul(sv["yn"], dh1, mode="tn", out_dtype=BF16, name=f"b{i}_out_w").reshape(N_DEV, -1, d)
            if i == 0:
                send_to_sibling(0, 0)
            dyn = matmul(dh1, w_out_of(i), mode="nt", after=behind, name=f"b{i}_out_x")
            dy, dz, g_gn = ssd_gate_bwd(sv["y"], raw[0], ssd_gnorm_w[j], dyn, name=f"b{i}_gate")
            dxs, dbm, dcm, ddtp, dadtp, dd_g = ssd_scan_bwd(act[0], act[1], act[2], sv["dtp"], sv["a_g"], sv["d_x"], selectors,
                                                          sv["states"], dy, heads_per_group=hpg, name=f"b{i}_scan")
            behind = send_to_chips(0, 0, [dxs]) if i == 0 else None
            ddt_raw, g_dtb, g_alog = ssd_dt_bwd(raw[4], ssd_dt_bias[j], ssd_a_log[j], sv["dt"],
                                                _from_group_lanes(ddtp, hpg), _from_group_lanes(dadtp, hpg),
                                                name=f"b{i}_dt")
            conv_back = [ssd_conv_bwd(raw[1 + q], fw["conv_w"][q], fw["conv_b"][q], dact, name=f"b{i}_conv{q}")
                         for q, dact in enumerate((dxs, dbm, dcm))]
            dsegs = [dz] + [cb[0] for cb in conv_back] + [ddt_raw]
            g_in = jnp.concatenate([matmul(u, ds, mode="tn", out_dtype=BF16, name=f"b{i}_in{q}_w")
                                    for q, ds in enumerate(dsegs)], axis=1)
            big["ssd_in_w", j] = _col_blocks(g_in)
            big["ssd_conv_w", j] = _col_blocks(jnp.concatenate([cb[1] for cb in conv_back], axis=1))
            du = None
            for q, (ds, wseg) in enumerate(zip(dsegs, fw["w_in"])):
                du = matmul(ds, wseg, mode="nt", res=du, after=behind if q == 0 else None, name=f"b{i}_in{q}_x")
            small["ssd_conv_b"][j] = jnp.concatenate([cb[2] for cb in conv_back], axis=1)
            small["ssd_dt_bias"][j] = g_dtb
            small["ssd_a_log"][j] = g_alog
            small["ssd_d"][j] = dd_g[:, 0, :hpg]
            small["ssd_gnorm_w"][j] = g_gn
        else:
            proj = sv["proj"]
            big["sb_out_w", j] = matmul(sv["og"], dh1, mode="tn", out_dtype=BF16, name=f"b{i}_out_w").reshape(N_DEV, -1, d)
            dog = matmul(dh1, w_out_of(i), mode="nt", after=behind, name=f"b{i}_out_x")
            do, dg = sb_gate_bwd(dog, sv["o"], proj, name=f"b{i}_gate")
            dqn, dkn, dv = sb_attn_bwd(sv["qn"], sv["kn"], proj, sv["tot"], do, v_off=sv["v_off"], name=f"b{i}_attn")
            dproj, g_qn, g_kn = sb_pack_bwd(proj, sb_qn_w[j], sb_kn_w[j], dqn, dkn, dv, dg, name=f"b{i}_qknorm")
            big["sb_in_w", j] = matmul(u, dproj, mode="tn", out_dtype=BF16, out_blocks=sb_in_w.shape[2], name=f"b{i}_in_w")
            du = matmul(dproj, full["sb_in_w", j], mode="nt", name=f"b{i}_in_x")
            small["sb_qn_w"][j] = g_qn
            small["sb_kn_w"][j] = g_kn
        dh, g_n = rmsnorm_bwd(sv["h_in"], norm_w[i], du, dh1, name=f"b{i}_norm")
        small["norm_w"][i] = g_n
        pending = (i, len(groups(i)) - 1)
        order_after = send_to_sibling(*pending)
    send_to_chips(*pending, [dh])
    grad_x = dh.reshape(x.shape)

    rep_like = [wts[n] for n in REPLICATED]
    g_packed = all_reduce_small(_pack([jnp.stack([t.reshape(-1) for t in small[n]]) for n in REPLICATED]),
                                name="all_reduce_small_grads")
    d_packed, m_packed, v_packed = adamw_replicated(
        _pack(rep_like), _pack([mom1[n] for n in REPLICATED]), _pack([mom2[n] for n in REPLICATED]), g_packed,
        name="adamw_replicated")
    grads = dict(zip(REPLICATED, _unpack(g_packed, rep_like)))
    deltas = dict(zip(REPLICATED, _unpack(d_packed, rep_like)))
    new_m = dict(zip(REPLICATED, _unpack(m_packed, rep_like)))
    new_v = dict(zip(REPLICATED, _unpack(v_packed, rep_like)))

    updated = {}
    after = [scatters[0, len(groups(0)) - 1][4]]
    for i in reversed(range(depth)):
        for q, keys in enumerate(groups(i)[::-1]):
            sums, received = ici_wait("scatter", scatters[i, q], after, name=f"rs{i}{'ab'[q]}_wait")
            after = []
            for (n, idx), t_sum, recv in zip(keys, sums, received):
                updated[n], done = adamw_sharded(wts[n], mom1[n], mom2[n], idx, t_sum, recv, chip, updated.get(n),
                                                 name=f"adamw_{n}{idx}")
                after.append(done)
    for n, (g_n, d_n, m_n, v_n) in updated.items():
        grads[n], deltas[n], new_m[n], new_v[n] = g_n, d_n, m_n, v_n

    return (loss, grad_x, *[grads[n] for n in WEIGHT_NAMES], *[deltas[n] for n in WEIGHT_NAMES],
            *[new_m[n] for n in WEIGHT_NAMES], *[new_v[n] for n in WEIGHT_NAMES])
```

```python
import functools
import math

import jax
import jax.numpy as jnp
from jax import lax
from jax.experimental import pallas as pl
from jax.experimental.pallas import tpu as pltpu

F32 = jnp.float32
BF16 = jnp.bfloat16
MESH = pl.DeviceIdType.MESH

N_DEV = 8
N_CHIP = 4
LANES = 128
VMEM_LIMIT_BYTES = 56 * 1024 * 1024
MATMUL_TILE_BYTES = 36 * 1024 * 1024

NORM_EPS = 1e-6
GATED_NORM_EPS = 1e-5
SSD_HEAD_DIM = 64
SSD_N_GROUPS = 8
SSD_D_STATE = 128
SSD_D_CONV = 4
SSD_CHUNK = 128
SSD_FWD_CHUNKS_PER_STEP = 8
SSD_BWD_CHUNKS_PER_STEP = 8
SB_HEAD_DIM = 128

ADAM_LR = 0.001
ADAM_B1 = 0.9
ADAM_B2 = 0.999
ADAM_EPS = 1e-08
ADAM_WD = 0.01
ADAM_STEP = 10


def _cparams(sem=None, **kw):
    return pltpu.CompilerParams(dimension_semantics=sem, vmem_limit_bytes=VMEM_LIMIT_BYTES, **kw)


def _pick(dim, prefs):
    for t in prefs:
        if dim % t == 0:
            return t
    return dim


def _sigmoid(x):
    return 1.0 / (1.0 + jnp.exp(-x))


def _silu(x):
    return x * _sigmoid(x)


def _silu_grad(x):
    s = _sigmoid(x)
    return s * (1.0 + x * (1.0 - s))


def matmul(a, b, *, mode="nn", out_dtype=F32, res=None, out_blocks=None, after=None, name):
    b_blocked = b.ndim == 3
    if mode == "nn":
        m, kc = a.shape
        n = b.shape[-1] * (N_DEV if b_blocked else 1)
    elif mode == "nt":
        m, kc = a.shape
        n = b.shape[-2]
    else:
        kc, m = a.shape
        n = b.shape[-1]
    nb = b.shape[-1] if b_blocked else None
    tn = _pick(n if not out_blocks else out_blocks, (512, 256, 128))
    if b_blocked and mode == "nn":
        tn = _pick(nb, (512, 256, 128))
    k_unit = nb if (b_blocked and mode == "nt") else 1
    tm, tk = None, None
    for tm_try in (1024, 512, 256, 128):
        if m % tm_try:
            continue
        for tk_try in (kc, kc // 2, kc // 4, 2048, 1024, 512, 256, 128):
            if tk_try > kc or tk_try < k_unit or kc % tk_try or tk_try % k_unit:
                continue
            tiles = 2 * (tm_try * tk_try * a.dtype.itemsize + tk_try * tn * b.dtype.itemsize)
            tiles += tm_try * tn * (2 * jnp.dtype(out_dtype).itemsize + 4 + (8 if res is not None else 0))
            if tiles <= MATMUL_TILE_BYTES:
                tm, tk = tm_try, tk_try
                break
        if tm:
            break
    if tm is None:
        tm, tk = m, max(k_unit, LANES if kc % LANES == 0 else kc)
    nk = kc // tk
    grid = (m // tm, n // tn, nk)

    if mode == "tn":
        a_spec = pl.BlockSpec((tk, tm), lambda i, j, k: (k, i))
        dims = (((0,), (0,)), ((), ()))
    else:
        a_spec = pl.BlockSpec((tm, tk), lambda i, j, k: (i, k))
        dims = (((1,), (0,)), ((), ())) if mode == "nn" else (((1,), (1,)), ((), ()))
    if mode == "nt":
        if b_blocked:
            b_spec = pl.BlockSpec((tk // nb, tn, nb), lambda i, j, k: (k, j, 0))
        else:
            b_spec = pl.BlockSpec((tn, tk), lambda i, j, k: (j, k))
    else:
        if b_blocked:
            per = nb // tn
            b_spec = pl.BlockSpec((None, tk, tn), lambda i, j, k: (j // per, k, j % per))
        else:
            b_spec = pl.BlockSpec((tk, tn), lambda i, j, k: (k, j))
    if out_blocks:
        per_o = out_blocks // tn
        out_shape = jax.ShapeDtypeStruct((n // out_blocks, m, out_blocks), out_dtype)
        out_spec = pl.BlockSpec((None, tm, tn), lambda i, j, k: (j // per_o, i, j % per_o))
    else:
        out_shape = jax.ShapeDtypeStruct((m, n), out_dtype)
        out_spec = pl.BlockSpec((tm, tn), lambda i, j, k: (i, j))
    in_specs = [a_spec, b_spec]
    args = [a, b]
    if res is not None:
        in_specs.append(pl.BlockSpec((tm, tn), lambda i, j, k: (i, j)))
        args.append(res)
    if after is not None:
        in_specs.append(pl.BlockSpec(memory_space=pl.ANY))
        args.append(after)
    n_in = len(args)

    def body(*refs):
        a_ref, b_ref = refs[:2]
        r_ref = refs[2] if res is not None else None
        o_ref = refs[n_in]

        def finish(r):
            if res is not None:
                r = r + r_ref[...].astype(F32)
            o_ref[...] = r.astype(out_dtype)

        if b_blocked and mode == "nt":
            part = None
            for blk in range(tk // nb):
                term = lax.dot_general(a_ref[:, blk * nb:(blk + 1) * nb].astype(BF16), b_ref[blk].astype(BF16), dims,
                                       preferred_element_type=F32)
                part = term if part is None else part + term
        else:
            part = lax.dot_general(a_ref[...].astype(BF16), b_ref[...].astype(BF16), dims, preferred_element_type=F32)
        if nk == 1:
            finish(part)
            return
        acc_ref = refs[-1]
        k = pl.program_id(2)

        @pl.when(k == 0)
        def _():
            acc_ref[...] = part

        @pl.when(k > 0)
        def _():
            acc_ref[...] += part

        @pl.when(k == nk - 1)
        def _():
            finish(acc_ref[...])

    return pl.pallas_call(
        body, out_shape=out_shape, grid=grid, in_specs=in_specs, out_specs=out_spec,
        scratch_shapes=[] if nk == 1 else [pltpu.VMEM((tm, tn), F32)], name=name,
        compiler_params=_cparams(("parallel", "parallel", "arbitrary")),
    )(*args)


def _dot(a, b, dims, precision=None):
    return lax.dot_general(a, b, (dims, ((), ())), preferred_element_type=F32, precision=precision)


_NN = ((1,), (0,))
_NT = ((1,), (1,))
_TN = ((0,), (0,))
_EXACT = lax.Precision.HIGHEST


def _chunk_decay_terms(dt, a):
    ln = dt.shape[0]
    row = lax.broadcasted_iota(jnp.int32, (ln, ln), 0)
    col = lax.broadcasted_iota(jnp.int32, (ln, ln), 1)
    tri = (row >= col).astype(F32)
    a_col = _dot(tri, dt * a, _NN, _EXACT)
    return a_col, a_col.T, row >= col


def _exact_dot(x, sel, terms):
    t = x.shape[0]
    parts, rest = [], x
    for k in range(terms):
        piece = rest.astype(BF16)
        parts.append(piece)
        if k + 1 < terms:
            rest = rest - piece.astype(F32)
    r = _dot(jnp.concatenate(parts, axis=0), sel, _NN)
    out = r[:t]
    for k in range(1, terms):
        out = out + r[k * t:(k + 1) * t]
    return out


def ssd_selectors(r_n):
    lane = jnp.arange(LANES)
    spread64 = (lane[:, None] == jnp.arange(r_n * SSD_HEAD_DIM)[None, :] // SSD_HEAD_DIM).astype(BF16)
    pair_sum = jnp.stack([lane[None, :] == 2 * q + lane[:, None] // SSD_HEAD_DIM for q in range(r_n // 2)]).astype(BF16)
    row_sum = jnp.stack([jnp.broadcast_to(lane[None, :] == r, (LANES, LANES)) for r in range(r_n)]).astype(BF16)
    return spread64, pair_sum, row_sum


def _ssd_chunk_setup(dt, a, spread64):
    ln = dt.shape[0]
    a_col, a_row, causal = _chunk_decay_terms(dt, a)
    ea = jnp.exp(a_col)
    te = jnp.exp(a_col[ln - 1:ln, :] - a_col)
    return (a_row, a_col, _exact_dot(dt, spread64, 2), _exact_dot(ea, spread64, 2), _exact_dot(te, spread64, 2),
            ea, causal)


def ssd_scan_fwd(xs, bm, cm, dtp, a_g, d_x, selectors, *, heads_per_group, name):
    s, di = xs.shape
    g_n = SSD_N_GROUPS
    r_n, p_n, n_n, ln = heads_per_group, SSD_HEAD_DIM, SSD_D_STATE, SSD_CHUNK
    nc = s // ln
    cps = _pick(nc, (SSD_FWD_CHUNKS_PER_STEP, 4, 2, 1))
    pairs, pw = r_n // 2, 2 * p_n
    spread64 = selectors[0]

    def body(xs_ref, bm_ref, cm_ref, dt_ref, a_ref, d_ref, s64_ref, y_ref, st_ref, state):
        c = pl.program_id(1)

        @pl.when(c == 0)
        def _():
            state[...] = jnp.zeros_like(state)

        first_head = lax.broadcasted_iota(jnp.int32, (1, pw), 1) < p_n
        for sub in range(cps):
            rows = slice(sub * ln, (sub + 1) * ln)
            a_row, a_col, dt_x, ea_x, te_x, _, causal = _ssd_chunk_setup(dt_ref[rows, :], a_ref[...], s64_ref[...])
            bm_f = bm_ref[rows, :]
            bmb = bm_f.astype(BF16)
            bm_t = bm_f.T.astype(BF16)
            cmb = cm_ref[rows, :].astype(BF16)
            scores = _dot(cmb, bmb, _NT)
            for q in range(pairs):
                sl = slice(q * pw, (q + 1) * pw)
                x2 = xs_ref[rows, sl]
                xdt2 = x2 * dt_x[:, sl]
                xdt2b = xdt2.astype(BF16)
                y_heads = []
                for r in (2 * q, 2 * q + 1):
                    decay = jnp.exp(jnp.where(causal, a_col[:, r:r + 1] - a_row[r:r + 1, :], -jnp.inf))
                    y_heads.append(_dot((scores * decay).astype(BF16), xdt2b, _NN))
                s2t = state[q]
                st_ref[sub, q] = s2t
                y2 = jnp.where(first_head, y_heads[0], y_heads[1])
                y2 = y2 + ea_x[:, sl] * _dot(cmb, s2t.astype(BF16), _NN)
                y_ref[rows, sl] = y2 + d_ref[:, sl] * x2
                state[q] = s2t * ea_x[ln - 1:ln, sl] + _dot(bm_t, (xdt2 * te_x[:, sl]).astype(BF16), _NN)

    whole = lambda t: pl.BlockSpec(t.shape, lambda g, c: (0,) * t.ndim)
    step = cps * ln
    return pl.pallas_call(
        body,
        out_shape=(jax.ShapeDtypeStruct((s, di), F32),
                   jax.ShapeDtypeStruct((nc, g_n * pairs, n_n, pw), F32)),
        grid=(g_n, nc // cps),
        in_specs=[pl.BlockSpec((step, r_n * p_n), lambda g, c: (c, g)),
                  pl.BlockSpec((step, n_n), lambda g, c: (c, g)),
                  pl.BlockSpec((step, n_n), lambda g, c: (c, g)),
                  pl.BlockSpec((None, step, LANES), lambda g, c: (g, c, 0)),
                  pl.BlockSpec((None, 1, LANES), lambda g, c: (g, 0, 0)),
                  pl.BlockSpec((None, 1, r_n * p_n), lambda g, c: (g, 0, 0)),
                  whole(spread64)],
        out_specs=(pl.BlockSpec((step, r_n * p_n), lambda g, c: (c, g)),
                   pl.BlockSpec((cps, pairs, n_n, pw), lambda g, c: (c, g, 0, 0))),
        scratch_shapes=[pltpu.VMEM((pairs, n_n, pw), F32)],
        name=name, compiler_params=_cparams(("parallel", "arbitrary")),
    )(xs, bm, cm, dtp, a_g, d_x, spread64)


def _row8(v):
    return jnp.broadcast_to(v, (8, v.shape[1]))


def ssd_scan_bwd(xs, bm, cm, dtp, a_g, d_x, selectors, states, dy, *, heads_per_group, name):
    s, di = xs.shape
    g_n = SSD_N_GROUPS
    r_n, p_n, n_n, ln = heads_per_group, SSD_HEAD_DIM, SSD_D_STATE, SSD_CHUNK
    nc = s // ln
    cps = _pick(nc, (SSD_BWD_CHUNKS_PER_STEP, 2, 1))
    pairs, pw = r_n // 2, 2 * p_n
    spread64, pair_sum, row_sum = selectors

    def body(xs_ref, bm_ref, cm_ref, dt_ref, a_ref, d_ref, s64_ref, ps_ref, rs_ref, st_ref, dy_ref,
             dxs_ref, dbm_ref, dcm_ref, ddt_ref, dadt_ref, dd_ref, dstate, da_rows):
        c = pl.program_id(1)

        @pl.when(c == 0)
        def _():
            dstate[...] = jnp.zeros_like(dstate)
            dd_ref[...] = jnp.zeros_like(dd_ref)

        row = lax.broadcasted_iota(jnp.int32, (ln, ln), 0)
        col = lax.broadcasted_iota(jnp.int32, (ln, ln), 1)
        causal_t = col >= row
        upper = causal_t.astype(F32)
        first_head = lax.broadcasted_iota(jnp.int32, (1, pw), 1) < p_n
        for sub in reversed(range(cps)):
            rows = slice(sub * ln, (sub + 1) * ln)
            a_row, a_col, dt_x, ea_x, te_x, ea, causal = _ssd_chunk_setup(dt_ref[rows, :], a_ref[...], s64_ref[...])
            bmb = bm_ref[rows, :].astype(BF16)
            cm_f = cm_ref[rows, :]
            cmb = cm_f.astype(BF16)
            cm_t = cm_f.T.astype(BF16)
            scores = _dot(cmb, bmb, _NT)
            scores_t = _dot(bmb, cmb, _NT)
            e_last = ea[ln - 1:ln, :]
            da_rows[...] = jnp.zeros_like(da_rows)
            dscores = jnp.zeros((ln, ln), F32)
            dcm = jnp.zeros((ln, n_n), F32)
            dbm = jnp.zeros((ln, n_n), F32)
            da_cols = jnp.zeros((ln, LANES), F32)
            da_last = jnp.zeros((1, LANES), F32)
            ddt = jnp.zeros((ln, LANES), F32)
            dd = jnp.zeros((1, LANES), F32)
            for q in range(pairs):
                sl = slice(q * pw, (q + 1) * pw)
                sum2 = ps_ref[q]
                x2 = xs_ref[rows, sl]
                dt2 = dt_x[:, sl]
                xdt2 = x2 * dt2
                xdt2b = xdt2.astype(BF16)
                dy2 = dy_ref[rows, sl]
                dy2b = dy2.astype(BF16)
                dxdt_heads = []
                for h, r in enumerate((2 * q, 2 * q + 1)):
                    a_r = jnp.broadcast_to(a_col[:, r:r + 1], (ln, ln))
                    decay = jnp.exp(jnp.where(causal, a_r - a_row[r:r + 1, :], -jnp.inf))
                    decay_t = jnp.exp(jnp.where(causal_t, a_row[r:r + 1, :] - a_r, -jnp.inf))
                    dy_h = jnp.where(first_head if h == 0 else jnp.logical_not(first_head), dy2, 0.0).astype(BF16)
                    dm = _dot(dy_h, xdt2b, _NT)
                    dscores = dscores + dm * decay
                    e_mat = dm * (scores * decay)
                    da_cols = da_cols + _exact_dot(e_mat, rs_ref[r], 2)
                    da_rows[r:r + 1, :] = -jnp.sum(e_mat, axis=0, keepdims=True)
                    dxdt_heads.append(_dot((scores_t * decay_t).astype(BF16), dy2b, _NN))
                dxdt2 = jnp.where(first_head, dxdt_heads[0], dxdt_heads[1])
                s2t = st_ref[sub, q]
                s2tb = s2t.astype(BF16)
                ds2t = dstate[q]
                ds2tb = ds2t.astype(BF16)
                ea2, te2 = ea_x[:, sl], te_x[:, sl]
                y_off2 = ea2 * _dot(cmb, s2tb, _NN)
                dy_e2 = (dy2 * ea2).astype(BF16)
                dcm = dcm + _dot(dy_e2, s2tb, _NT)
                ds_in = _dot(cm_t, dy_e2, _NN)
                da_cols = da_cols + _exact_dot(dy2 * y_off2, sum2, 2)
                bds2 = _dot(bmb, ds2tb, _NN)
                dxdt2 = dxdt2 + te2 * bds2
                xdt_e2 = xdt2 * te2
                dbm = dbm + _dot(xdt_e2.astype(BF16), ds2tb, _NT)
                w_cols = _exact_dot(xdt_e2 * bds2, sum2, 2)
                da_cols = da_cols - w_cols
                state_dot = _exact_dot(_row8(jnp.sum(ds2t * s2t, axis=0, keepdims=True)), sum2, 2)[0:1]
                da_last = da_last + jnp.sum(w_cols, axis=0, keepdims=True) + e_last * state_dot
                dstate[q] = ds2t * ea_x[ln - 1:ln, sl] + ds_in
                dxs_ref[rows, sl] = dxdt2 * dt2 + d_ref[:, sl] * dy2
                ddt = ddt + _exact_dot(dxdt2 * x2, sum2, 2)
                dd = dd + _exact_dot(_row8(jnp.sum(dy2 * x2, axis=0, keepdims=True)), sum2, 2)[0:1]
            dcm_ref[rows, :] = dcm + _dot(dscores.astype(BF16), bmb, _NN)
            dbm_ref[rows, :] = dbm + _dot(dscores.T.astype(BF16), cmb, _NN)
            da_total = da_cols + da_rows[...].T
            dadt_ref[rows, :] = _dot(upper, da_total, _NN, _EXACT) + da_last
            ddt_ref[rows, :] = ddt
            dd_ref[...] += dd

    step, last_c = cps * ln, nc // cps - 1
    whole = lambda t: pl.BlockSpec(t.shape, lambda g, c: (0,) * t.ndim)
    return pl.pallas_call(
        body,
        out_shape=(jax.ShapeDtypeStruct((s, di), F32),
                   jax.ShapeDtypeStruct(bm.shape, F32),
                   jax.ShapeDtypeStruct(cm.shape, F32),
                   jax.ShapeDtypeStruct(dtp.shape, F32),
                   jax.ShapeDtypeStruct(dtp.shape, F32),
                   jax.ShapeDtypeStruct(a_g.shape, F32)),
        grid=(g_n, nc // cps),
        in_specs=[pl.BlockSpec((step, r_n * p_n), lambda g, c: (last_c - c, g)),
                  pl.BlockSpec((step, n_n), lambda g, c: (last_c - c, g)),
                  pl.BlockSpec((step, n_n), lambda g, c: (last_c - c, g)),
                  pl.BlockSpec((None, step, LANES), lambda g, c: (g, last_c - c, 0)),
                  pl.BlockSpec((None, 1, LANES), lambda g, c: (g, 0, 0)),
                  pl.BlockSpec((None, 1, r_n * p_n), lambda g, c: (g, 0, 0)),
                  whole(spread64), whole(pair_sum), whole(row_sum),
                  pl.BlockSpec((cps, pairs, n_n, pw), lambda g, c: (last_c - c, g, 0, 0)),
                  pl.BlockSpec((step, r_n * p_n), lambda g, c: (last_c - c, g))],
        out_specs=(pl.BlockSpec((step, r_n * p_n), lambda g, c: (last_c - c, g)),
                   pl.BlockSpec((step, n_n), lambda g, c: (last_c - c, g)),
                   pl.BlockSpec((step, n_n), lambda g, c: (last_c - c, g)),
                   pl.BlockSpec((None, step, LANES), lambda g, c: (g, last_c - c, 0)),
                   pl.BlockSpec((None, step, LANES), lambda g, c: (g, last_c - c, 0)),
                   pl.BlockSpec((None, 1, LANES), lambda g, c: (g, 0, 0))),
        scratch_shapes=[pltpu.VMEM((pairs, n_n, pw), F32), pltpu.VMEM((LANES, ln), F32)],
        name=name, compiler_params=_cparams(("parallel", "arbitrary")),
    )(xs, bm, cm, dtp, a_g, d_x, spread64, pair_sum, row_sum, states, dy)


SB_Q_TILE = 2048
SB_K_TILE = 256


def _tri_sum(x, tri):
    t = x.shape[0]
    hi = x.astype(BF16)
    r1 = x - hi.astype(F32)
    mid = r1.astype(BF16)
    lo = (r1 - mid.astype(F32)).astype(BF16)
    r = _dot(jnp.concatenate([hi, mid, lo], axis=0), tri, _NN)
    return r[:t] + r[t:2 * t] + r[2 * t:]


def _sb_logits(q, k_j, scale, strict):
    z = _dot(q, k_j, _NT) * scale
    sp = jnp.log(1.0 + jnp.exp(-jnp.abs(z)))
    log_b = jnp.minimum(z, 0.0) - sp
    log_1mb = log_b - z
    if strict is not None:
        log_1mb = jnp.where(strict, log_1mb, 0.0)
    return log_b, log_1mb


def _sb_tiles(s):
    tq = _pick(s, (SB_Q_TILE, 2 * SB_K_TILE, SB_K_TILE, LANES))
    return tq, min(tq, SB_K_TILE)


def _sb_diag_mask(rows, tk):
    return lax.broadcasted_iota(jnp.int32, (rows, tk), 1) < lax.broadcasted_iota(jnp.int32, (rows, tk), 0)


def _sb_iotas(t):
    row = lax.broadcasted_iota(jnp.int32, (t, t), 0)
    col = lax.broadcasted_iota(jnp.int32, (t, t), 1)
    return row, col


def sb_attn_fwd(qn, kn, v, *, v_off=0, name):
    s, w = qn.shape
    dh = SB_HEAD_DIM
    n_h = w // dh
    tq, tk = _sb_tiles(s)
    per = tq // tk
    scale = 1.0 / math.sqrt(dh)

    def body(q_ref, k_ref, v_ref, o_ref, tot_ref):
        i = pl.program_id(1)
        q = q_ref[...]
        row, col = _sb_iotas(tk)
        later = (row > col).astype(BF16)

        def tile(q_rows, j, acc, run, mask):
            s0 = pl.multiple_of(j * tk, tk)
            k_j = k_ref[pl.ds(s0, tk), :]
            v_j = v_ref[pl.ds(s0, tk), :].astype(BF16)
            log_b, log_1mb = _sb_logits(q_rows, k_j, scale, mask)
            att = jnp.exp(log_b + (_tri_sum(log_1mb, later) + run))
            if mask is not None:
                att = jnp.where(mask, att, 0.0)
            return acc + _dot(att.astype(BF16), v_j, _NN), run + jnp.sum(log_1mb, axis=1, keepdims=True)

        acc, run = jnp.zeros((tq, dh), F32), jnp.zeros((tq, 1), F32)
        for d in reversed(range(per)):
            r0 = d * tk
            a2, r2 = tile(q[r0:], i * per + d, acc[r0:], run[r0:], _sb_diag_mask(tq - r0, tk))
            acc = a2 if r0 == 0 else jnp.concatenate([acc[:r0], a2], axis=0)
            run = r2 if r0 == 0 else jnp.concatenate([run[:r0], r2], axis=0)

        def group(gg, c):
            for d in reversed(range(per)):
                c = tile(q, (i - 1 - gg) * per + d, c[0], c[1], None)
            return c

        acc, run = lax.fori_loop(0, i, group, (acc, run))
        o_ref[...] = acc
        tot_ref[...] = jnp.broadcast_to(run, (tq, dh))

    return pl.pallas_call(
        body,
        out_shape=(jax.ShapeDtypeStruct((s, w), F32), jax.ShapeDtypeStruct((s, w), F32)),
        grid=(n_h, s // tq),
        in_specs=[pl.BlockSpec((tq, dh), lambda h, i: (i, h)),
                  pl.BlockSpec((s, dh), lambda h, i: (0, h)),
                  pl.BlockSpec((s, dh), lambda h, i: (0, v_off + h))],
        out_specs=(pl.BlockSpec((tq, dh), lambda h, i: (i, h)),
                   pl.BlockSpec((tq, dh), lambda h, i: (i, h))),
        name=name, compiler_params=_cparams(("parallel", "parallel")),
    )(qn, kn, v)


def sb_attn_bwd(qn, kn, v, tot, do, *, v_off=0, name):
    s, w = qn.shape
    dh = SB_HEAD_DIM
    n_h = w // dh
    tq, tk = _sb_tiles(s)
    per = tq // tk
    scale = 1.0 / math.sqrt(dh)

    def body(q_ref, k_ref, v_ref, tot_ref, do_ref, dq_ref, dk_ref, dv_ref):
        dk_ref[...] = jnp.zeros_like(dk_ref)
        dv_ref[...] = jnp.zeros_like(dv_ref)
        row, col = _sb_iotas(tk)
        upto = (row <= col).astype(BF16)
        before = (row < col).astype(BF16)

        def q_block(i, _):
            t0 = pl.multiple_of(i * tq, tq)
            q = q_ref[pl.ds(t0, tq), :]
            do_i = do_ref[pl.ds(t0, tq), :].astype(BF16)
            total = tot_ref[pl.ds(t0, tq), 0:1]

            def tile(r0, j, dq, run_l, run_g, mask):
                s0 = pl.multiple_of(j * tk, tk)
                k_j = k_ref[pl.ds(s0, tk), :]
                v_j = v_ref[pl.ds(s0, tk), :].astype(BF16)
                q_r, do_r = q[r0:], do_i[r0:]
                log_b, log_1mb = _sb_logits(q_r, k_j, scale, mask)
                att = jnp.exp(log_b + ((total[r0:] - run_l) - _tri_sum(log_1mb, upto)))
                if mask is not None:
                    att = jnp.where(mask, att, 0.0)
                g = att * _dot(do_r, v_j, _NT)
                c = _tri_sum(g, before) + run_g
                dz = (g - (g + c) * jnp.exp(log_b)) * scale
                if mask is not None:
                    dz = jnp.where(mask, dz, 0.0)
                dz = dz.astype(BF16)
                dk_ref[pl.ds(s0, tk), :] += _dot(dz, q_r, _TN)
                dv_ref[pl.ds(s0, tk), :] += _dot(att.astype(BF16), do_r, _TN)
                return (dq + _dot(dz, k_j, _NN), run_l + jnp.sum(log_1mb, axis=1, keepdims=True),
                        run_g + jnp.sum(g, axis=1, keepdims=True))

            def group(gg, c):
                for d in range(per):
                    c = tile(0, gg * per + d, c[0], c[1], c[2], None)
                return c

            zero = jnp.zeros((tq, 1), F32)
            dq, run_l, run_g = lax.fori_loop(0, i, group, (jnp.zeros((tq, dh), F32), zero, zero))
            for d in range(per):
                r0 = d * tk
                p_dq, p_l, p_g = tile(r0, i * per + d, dq[r0:], run_l[r0:], run_g[r0:], _sb_diag_mask(tq - r0, tk))
                if r0 == 0:
                    dq, run_l, run_g = p_dq, p_l, p_g
                else:
                    dq = jnp.concatenate([dq[:r0], p_dq], axis=0)
                    run_l = jnp.concatenate([run_l[:r0], p_l], axis=0)
                    run_g = jnp.concatenate([run_g[:r0], p_g], axis=0)
            dq_ref[pl.ds(t0, tq), :] = dq
            return 0

        lax.fori_loop(0, s // tq, q_block, 0)

    head = pl.BlockSpec((s, dh), lambda h: (0, h))
    return pl.pallas_call(
        body,
        out_shape=tuple(jax.ShapeDtypeStruct((s, w), F32) for _ in range(3)),
        grid=(n_h,),
        in_specs=[head, head, pl.BlockSpec((s, dh), lambda h: (0, v_off + h)), head, head],
        out_specs=(head, head, head),
        name=name, compiler_params=_cparams(("parallel",)),
    )(qn, kn, v, tot, do)


ROW_TILE = 512
WIDE_ROW_TILE = 128


def _rows(width, col=0, tm=ROW_TILE):
    return pl.BlockSpec((tm, width), lambda i: (i, col))


_wide_rows = functools.partial(_rows, tm=WIDE_ROW_TILE)


def _whole(shape):
    return pl.BlockSpec(shape, lambda i: (0,) * len(shape))


def _ew_call(body, out_shape, in_specs, out_specs, args, n_rows, name, carried=False):
    return pl.pallas_call(
        body, out_shape=out_shape, grid=(n_rows // in_specs[0].block_shape[0],), in_specs=in_specs, out_specs=out_specs,
        name=name, compiler_params=_cparams(("arbitrary",) if carried else ("parallel",)),
    )(*args)


def _first_step(*refs):
    @pl.when(pl.program_id(0) == 0)
    def _():
        for r in refs:
            r[...] = jnp.zeros_like(r)


def rmsnorm_fwd(x, w, after=None, *, name):
    s, d = x.shape

    def body(x_ref, w_ref, *rest):
        o_ref = rest[-1]
        xv = x_ref[...]
        r = lax.rsqrt(jnp.mean(xv * xv, axis=-1, keepdims=True) + NORM_EPS)
        o_ref[...] = (xv * r * w_ref[...]).astype(BF16)

    extra = [] if after is None else [after]
    return _ew_call(body, jax.ShapeDtypeStruct((s, d), BF16),
                    [_rows(d), _whole((1, d))] + [_whole(TOKEN_SHAPE)] * len(extra), _rows(d),
                    (x, w.reshape(1, d), *extra), s, name)


def rmsnorm_bwd(x, w, dy, dres, *, name):
    s, d = x.shape

    def body(x_ref, w_ref, dy_ref, dr_ref, dx_ref, dw_ref):
        _first_step(dw_ref)
        xv = x_ref[...]
        r = lax.rsqrt(jnp.mean(xv * xv, axis=-1, keepdims=True) + NORM_EPS)
        xhat = xv * r
        dyv = dy_ref[...].astype(F32)
        dw_ref[...] += jnp.sum(dyv * xhat, axis=0, keepdims=True)
        g = dyv * w_ref[...]
        dx_ref[...] = dr_ref[...] + r * (g - xhat * jnp.mean(g * xhat, axis=-1, keepdims=True))

    return _ew_call(body, (jax.ShapeDtypeStruct((s, d), F32), jax.ShapeDtypeStruct((1, d), F32)),
                    [_rows(d), _whole((1, d)), _rows(d), _rows(d)], (_rows(d), _whole((1, d))),
                    (x, w.reshape(1, d), dy, dres), s, name, carried=True)


def ple_fwd(h1, gate_pre, pp, *, name):
    s, d = h1.shape

    def body(h_ref, g_ref, p_ref, o_ref):
        o_ref[...] = h_ref[...] + p_ref[...] * _sigmoid(g_ref[...])

    return _ew_call(body, jax.ShapeDtypeStruct((s, d), F32), [_rows(d)] * 3, _rows(d), (h1, gate_pre, pp), s, name)


def ple_bwd(dh2, gate_pre, pp, after, *, name):
    s, d = dh2.shape

    def body(dh_ref, g_ref, p_ref, after_ref, dp_ref, dg_ref):
        gate = _sigmoid(g_ref[...])
        dh = dh_ref[...]
        dp_ref[...] = (dh * gate).astype(BF16)
        dg_ref[...] = (dh * p_ref[...] * gate * (1.0 - gate)).astype(BF16)

    shp = jax.ShapeDtypeStruct((s, d), BF16)
    return _ew_call(body, (shp, shp), [_rows(d)] * 3 + [_whole(TOKEN_SHAPE)], (_rows(d), _rows(d)),
                    (dh2, gate_pre, pp, after), s, name)


def loss_head(y, target, *, name):
    s, d = y.shape

    def body(y_ref, t_ref, l_ref, dy_ref):
        _first_step(l_ref)
        err = y_ref[...] - t_ref[...]
        per_tok = jnp.mean(err * err, axis=-1, keepdims=True)
        l_ref[...] += 0.5 * jnp.sum(per_tok, axis=0, keepdims=True)
        dy_ref[...] = err * (1.0 / d)

    return _ew_call(body, (jax.ShapeDtypeStruct((1, 1), F32), jax.ShapeDtypeStruct((s, d), F32)),
                    [_rows(d), _rows(d)], (_whole((1, 1)), _rows(d)), (y, target), s, name, carried=True)


CONV_COL_TILE = 256


def _conv_taps(x, w_ref):
    row = lax.broadcasted_iota(jnp.int32, (x.shape[0], 1), 0)
    acc = x * w_ref[SSD_D_CONV - 1:SSD_D_CONV, :]
    shifted = []
    for d in range(1, SSD_D_CONV):
        xs = jnp.where(row >= d, pltpu.roll(x, d, 0), 0.0)
        shifted.append(xs)
        acc = acc + xs * w_ref[SSD_D_CONV - 1 - d:SSD_D_CONV - d, :]
    return acc, shifted


def ssd_conv_fwd(x, w, b, *, name):
    s, c = x.shape
    tc = _pick(c, (CONV_COL_TILE, LANES))

    def body(x_ref, w_ref, b_ref, o_ref):
        pre, _ = _conv_taps(x_ref[...], w_ref)
        o_ref[...] = _silu(pre + b_ref[...])

    col = pl.BlockSpec((s, tc), lambda j: (0, j))
    return pl.pallas_call(
        body, out_shape=jax.ShapeDtypeStruct((s, c), F32), grid=(c // tc,),
        in_specs=[col, pl.BlockSpec((SSD_D_CONV, tc), lambda j: (0, j)), pl.BlockSpec((1, tc), lambda j: (0, j))],
        out_specs=col, name=name, compiler_params=_cparams(("parallel",)),
    )(x, w, b)


def ssd_conv_bwd(x, w, b, dact, *, name):
    s, c = x.shape
    tc = _pick(c, (CONV_COL_TILE, LANES))

    def body(x_ref, w_ref, b_ref, da_ref, dx_ref, dw_ref, db_ref):
        xv = x_ref[...]
        pre, shifted = _conv_taps(xv, w_ref)
        dpre = da_ref[...] * _silu_grad(pre + b_ref[...])
        db_ref[...] = jnp.sum(dpre, axis=0, keepdims=True)
        row = lax.broadcasted_iota(jnp.int32, (s, 1), 0)
        dx = dpre * w_ref[SSD_D_CONV - 1:SSD_D_CONV, :]
        dw_ref[SSD_D_CONV - 1:SSD_D_CONV, :] = jnp.sum(dpre * xv, axis=0, keepdims=True)
        for d in range(1, SSD_D_CONV):
            k = SSD_D_CONV - 1 - d
            dw_ref[k:k + 1, :] = jnp.sum(dpre * shifted[d - 1], axis=0, keepdims=True)
            up = jnp.where(row < s - d, pltpu.roll(dpre, s - d, 0), 0.0)
            dx = dx + up * w_ref[k:k + 1, :]
        dx_ref[...] = dx.astype(BF16)

    col = pl.BlockSpec((s, tc), lambda j: (0, j))
    wspec = pl.BlockSpec((SSD_D_CONV, tc), lambda j: (0, j))
    bspec = pl.BlockSpec((1, tc), lambda j: (0, j))
    return pl.pallas_call(
        body,
        out_shape=(jax.ShapeDtypeStruct((s, c), BF16), jax.ShapeDtypeStruct((SSD_D_CONV, c), F32),
                   jax.ShapeDtypeStruct((1, c), F32)),
        grid=(c // tc,), in_specs=[col, wspec, bspec, col], out_specs=(col, wspec, bspec),
        name=name, compiler_params=_cparams(("parallel",)),
    )(x, w, b, dact)


def ssd_dt_fwd(dt_raw, bias, a_log, *, name):
    s, h = dt_raw.shape

    def body(r_ref, b_ref, al_ref, dt_ref, a_ref):
        zv = r_ref[...] + b_ref[...]
        dt_ref[...] = jnp.maximum(zv, 0.0) + jnp.log(1.0 + jnp.exp(-jnp.abs(zv)))
        a_ref[...] = -jnp.exp(al_ref[...])

    full = pl.BlockSpec((s, h), lambda: (0, 0))
    vec = pl.BlockSpec((1, h), lambda: (0, 0))
    return pl.pallas_call(
        body, out_shape=(jax.ShapeDtypeStruct((s, h), F32), jax.ShapeDtypeStruct((1, h), F32)),
        in_specs=[full, vec, vec], out_specs=(full, vec), name=name, compiler_params=_cparams(),
    )(dt_raw, bias.reshape(1, h), a_log.reshape(1, h))


def ssd_dt_bwd(dt_raw, bias, a_log, dt, ddt, dadt, *, name):
    s, h = dt_raw.shape

    def body(r_ref, b_ref, al_ref, dt_ref, ddt_ref, dadt_ref, dr_ref, db_ref, dal_ref):
        a = -jnp.exp(al_ref[...])
        dadt_v = dadt_ref[...]
        d_dt = ddt_ref[...] + a * dadt_v
        d_raw = d_dt * _sigmoid(r_ref[...] + b_ref[...])
        dr_ref[...] = d_raw
        db_ref[...] = jnp.sum(d_raw, axis=0, keepdims=True)
        dal_ref[...] = jnp.sum(dadt_v * dt_ref[...], axis=0, keepdims=True) * a

    full = pl.BlockSpec((s, h), lambda: (0, 0))
    vec = pl.BlockSpec((1, h), lambda: (0, 0))
    return pl.pallas_call(
        body, out_shape=(jax.ShapeDtypeStruct((s, h), F32), jax.ShapeDtypeStruct((1, h), F32),
                         jax.ShapeDtypeStruct((1, h), F32)),
        in_specs=[full, vec, vec, full, full, full], out_specs=(full, vec, vec), name=name,
        compiler_params=_cparams(),
    )(dt_raw, bias.reshape(1, h), a_log.reshape(1, h), dt, ddt, dadt)


def _group_mean(v, n_groups):
    gw = v.shape[-1] // n_groups
    parts = [jnp.broadcast_to(jnp.mean(v[:, k * gw:(k + 1) * gw], axis=-1, keepdims=True), (v.shape[0], gw))
             for k in range(n_groups)]
    return jnp.concatenate(parts, axis=-1)


def ssd_gate_fwd(y, z, gw, *, name):
    s, di = y.shape

    def body(y_ref, z_ref, w_ref, o_ref):
        yg = y_ref[...] * _silu(z_ref[...])
        r = lax.rsqrt(_group_mean(yg * yg, SSD_N_GROUPS) + GATED_NORM_EPS)
        o_ref[...] = (yg * r * w_ref[...]).astype(BF16)

    return _ew_call(body, jax.ShapeDtypeStruct((s, di), BF16), [_wide_rows(di), _wide_rows(di), _whole((1, di))],
                    _wide_rows(di), (y, z, gw.reshape(1, di)), s, name)


def ssd_gate_bwd(y, z, gw, dyn, *, name):
    s, di = y.shape

    def body(y_ref, z_ref, w_ref, dn_ref, dy_ref, dz_ref, dw_ref):
        _first_step(dw_ref)
        yv, zv = y_ref[...], z_ref[...]
        sz = _silu(zv)
        yg = yv * sz
        r = lax.rsqrt(_group_mean(yg * yg, SSD_N_GROUPS) + GATED_NORM_EPS)
        yhat = yg * r
        dn = dn_ref[...]
        dw_ref[...] += jnp.sum(dn * yhat, axis=0, keepdims=True)
        g = dn * w_ref[...]
        dyg = r * (g - yhat * _group_mean(g * yhat, SSD_N_GROUPS))
        dy_ref[...] = dyg * sz
        dz_ref[...] = (dyg * yv * _silu_grad(zv)).astype(BF16)

    return _ew_call(body, (jax.ShapeDtypeStruct((s, di), F32), jax.ShapeDtypeStruct((s, di), BF16),
                           jax.ShapeDtypeStruct((1, di), F32)),
                    [_wide_rows(di), _wide_rows(di), _whole((1, di)), _wide_rows(di)],
                    (_wide_rows(di), _wide_rows(di), _whole((1, di))),
                    (y, z, gw.reshape(1, di), dyn), s, name, carried=True)


def _head_mean(v):
    return _group_mean(v, v.shape[-1] // SB_HEAD_DIM)


def sb_qk_fwd(proj, qw, kw, *, name):
    s, w4 = proj.shape
    w = w4 // 4
    reps = w // SB_HEAD_DIM

    def body(q_ref, k_ref, qw_ref, kw_ref, qn_ref, kn_ref):
        for x_ref, w_ref, o_ref in ((q_ref, qw_ref, qn_ref), (k_ref, kw_ref, kn_ref)):
            xv = x_ref[...]
            r = lax.rsqrt(_head_mean(xv * xv) + NORM_EPS)
            o_ref[...] = (xv * r * jnp.tile(w_ref[...], (1, reps))).astype(BF16)

    shp = jax.ShapeDtypeStruct((s, w), BF16)
    return _ew_call(body, (shp, shp), [_rows(w, 0), _rows(w, 1), _whole((1, SB_HEAD_DIM)), _whole((1, SB_HEAD_DIM))],
                    (_rows(w), _rows(w)), (proj, proj, qw.reshape(1, -1), kw.reshape(1, -1)), s, name)


def sb_gate_fwd(o, proj, *, name):
    s, w = o.shape

    def body(o_ref, g_ref, og_ref):
        og_ref[...] = (o_ref[...] * _silu(g_ref[...])).astype(BF16)

    return _ew_call(body, jax.ShapeDtypeStruct((s, w), BF16), [_rows(w), _rows(w, 3)], _rows(w), (o, proj), s, name)


def sb_gate_bwd(dog, o, proj, *, name):
    s, w = o.shape

    def body(d_ref, o_ref, g_ref, do_ref, dg_ref):
        gv, dv = g_ref[...], d_ref[...]
        do_ref[...] = dv * _silu(gv)
        dg_ref[...] = (dv * o_ref[...] * _silu_grad(gv)).astype(BF16)

    return _ew_call(body, (jax.ShapeDtypeStruct((s, w), F32), jax.ShapeDtypeStruct((s, w), BF16)),
                    [_rows(w), _rows(w), _rows(w, 3)], (_rows(w), _rows(w)), (dog, o, proj), s, name)


def sb_pack_bwd(proj, qw, kw, dqn, dkn, dv, dg, *, name):
    s, w4 = proj.shape
    w = w4 // 4
    reps = w // SB_HEAD_DIM

    def body(q_ref, k_ref, qw_ref, kw_ref, dqn_ref, dkn_ref, dv_ref, dg_ref, dp_ref, dqw_ref, dkw_ref):
        _first_step(dqw_ref, dkw_ref)
        for idx, (x_ref, w_ref, d_ref, dw_ref) in enumerate(((q_ref, qw_ref, dqn_ref, dqw_ref),
                                                           (k_ref, kw_ref, dkn_ref, dkw_ref))):
            xv = x_ref[...]
            r = lax.rsqrt(_head_mean(xv * xv) + NORM_EPS)
            xhat = xv * r
            dn = d_ref[...]
            per_col = jnp.sum(dn * xhat, axis=0, keepdims=True)
            acc = per_col[:, 0:SB_HEAD_DIM]
            for hh in range(1, reps):
                acc = acc + per_col[:, hh * SB_HEAD_DIM:(hh + 1) * SB_HEAD_DIM]
            dw_ref[...] += acc
            g = dn * jnp.tile(w_ref[...], (1, reps))
            dp_ref[:, idx * w:(idx + 1) * w] = (r * (g - xhat * _head_mean(g * xhat))).astype(BF16)
        dp_ref[:, 2 * w:3 * w] = dv_ref[...].astype(BF16)
        dp_ref[:, 3 * w:4 * w] = dg_ref[...]

    vec = _whole((1, SB_HEAD_DIM))
    return _ew_call(body, (jax.ShapeDtypeStruct((s, w4), BF16), jax.ShapeDtypeStruct((1, SB_HEAD_DIM), F32),
                           jax.ShapeDtypeStruct((1, SB_HEAD_DIM), F32)),
                    [_wide_rows(w, 0), _wide_rows(w, 1), vec, vec, _wide_rows(w), _wide_rows(w), _wide_rows(w),
                     _wide_rows(w)],
                    (_wide_rows(w4), vec, vec),
                    (proj, proj, qw.reshape(1, -1), kw.reshape(1, -1), dqn, dkn, dv, dg), s, name, carried=True)


_HBM = pl.BlockSpec(memory_space=pltpu.HBM)


def _mesh_pos():
    return lax.axis_index("x"), lax.axis_index("y"), lax.axis_index("c")


def _other_chips(x, y):
    return [(1 - x, y), (x, 1 - y), (1 - x, 1 - y)]


_SEM = pl.BlockSpec(memory_space=pltpu.SEMAPHORE)
_ANY = pl.BlockSpec(memory_space=pl.ANY)
_DATAFLOW = pltpu.SideEffectType.DATAFLOW_SIDE_EFFECTING
N_PEER_CHIPS = N_CHIP - 1
TOKEN_SHAPE = (8, LANES)


def _in_hbm(t):
    return pltpu.with_memory_space_constraint(t, pltpu.HBM)


def _ici_copies(kind, src_refs, land_refs, send_sems, recv_sems, arrivals=False):
    x, y, c = _mesh_pos()
    out = []
    for a in range(len(land_refs)):
        if kind in ("pass", "swap"):
            if kind == "pass":
                src, dst = land_refs[a].at[:, c], land_refs[a].at[:, 1 - c if arrivals else c]
            else:
                src, dst = src_refs[a].at[:, 1 - c], land_refs[a]
            out.append(pltpu.make_async_remote_copy(
                src_ref=src, dst_ref=dst, send_sem=send_sems.at[a], recv_sem=recv_sems.at[a],
                device_id=(x, y, 1 - c), device_id_type=MESH))
            continue
        for j, chip in enumerate(_other_chips(x, y)):
            if kind == "gather":
                src = land_refs[a].at[4 * x + 2 * y + c]
                dst = land_refs[a].at[4 * chip[0] + 2 * chip[1] + c] if arrivals else src
            else:
                src, dst = src_refs[a].at[2 * chip[0] + chip[1]], land_refs[a].at[j]
            k = a * N_PEER_CHIPS + j
            out.append(pltpu.make_async_remote_copy(
                src_ref=src, dst_ref=dst, send_sem=send_sems.at[k], recv_sem=recv_sems.at[k],
                device_id=(*chip, c), device_id_type=MESH))
    return out


def _n_copies(kind, lands):
    return len(lands) * (1 if kind in ("pass", "swap") else N_PEER_CHIPS)


def ici_start(kind, srcs, lands, after=(), *, name):
    ns, nb = len(srcs), len(srcs) + len(lands)
    n_sem = _n_copies(kind, lands)

    def body(*refs):
        first_out = nb + len(after)
        for cp in _ici_copies(kind, refs[:ns], refs[ns:nb], refs[first_out], refs[first_out + 1]):
            cp.start()
        refs[-1][...] = jnp.zeros(TOKEN_SHAPE, F32)

    outs = pl.pallas_call(
        body, name=name,
        out_shape=(pltpu.SemaphoreType.DMA((n_sem,)), pltpu.SemaphoreType.DMA((n_sem,)),
                   *[pltpu.HBM(t.shape, t.dtype) for t in (*srcs, *lands)], jax.ShapeDtypeStruct(TOKEN_SHAPE, F32)),
        in_specs=[_HBM] * nb + [_ANY] * len(after),
        out_specs=(_SEM, _SEM, *([_HBM] * nb), pl.BlockSpec(memory_space=pltpu.VMEM)),
        input_output_aliases={k: 2 + k for k in range(nb)},
        compiler_params=pltpu.CompilerParams(has_side_effects=_DATAFLOW),
    )(*[_in_hbm(t) for t in (*srcs, *lands)], *after)
    return outs[0], outs[1], list(outs[2:2 + ns]), list(outs[2 + ns:2 + nb]), outs[-1]


def ici_wait(kind, started, after, *, name):
    send_sems, recv_sems, srcs, lands, _ = started
    ns, nb = len(srcs), len(srcs) + len(lands)

    def body(*refs):
        for cp in _ici_copies(kind, refs[:ns], refs[ns:nb], refs[nb], refs[nb + 1]):
            cp.wait_send()
        for cp in _ici_copies(kind, refs[:ns], refs[ns:nb], refs[nb], refs[nb + 1], arrivals=True):
            cp.wait_recv()

    outs = pl.pallas_call(
        body, name=name,
        out_shape=tuple(pltpu.HBM(t.shape, t.dtype) for t in (*srcs, *lands)),
        in_specs=[_HBM] * nb + [_SEM, _SEM] + [_ANY] * len(after),
        out_specs=tuple([_HBM] * nb),
        input_output_aliases={k: k for k in range(nb)},
        compiler_params=pltpu.CompilerParams(has_side_effects=_DATAFLOW),
    )(*srcs, *lands, send_sems, recv_sems, *after)
    return list(outs[:ns]), list(outs[ns:])


def all_reduce_small(v, *, name):
    r = v.shape[0]

    def body(v_ref, o_ref, buf, send_sems, recv_sems):
        x, y, c = _mesh_pos()
        me = 4 * x + 2 * y + c
        buf[me] = v_ref[...]
        copies = []
        for k in range(1, N_DEV):
            to = ((x + (k >> 2)) % 2, (y + ((k >> 1) & 1)) % 2, (c + (k & 1)) % 2)
            copies.append(pltpu.make_async_remote_copy(
                src_ref=v_ref, dst_ref=buf.at[me], send_sem=send_sems.at[k - 1], recv_sem=recv_sems.at[k - 1],
                device_id=to, device_id_type=MESH))
        for cp in copies:
            cp.start()
        for cp in copies:
            cp.wait()
        acc = buf[0]
        for d in range(1, N_DEV):
            acc = acc + buf[d]
        o_ref[...] = acc

    vm = pl.BlockSpec(memory_space=pltpu.VMEM)
    return pl.pallas_call(
        body, out_shape=jax.ShapeDtypeStruct(v.shape, F32), in_specs=[vm], out_specs=vm,
        scratch_shapes=[pltpu.VMEM((N_DEV, r, LANES), F32), pltpu.SemaphoreType.DMA((N_DEV - 1,)),
                        pltpu.SemaphoreType.DMA((N_DEV - 1,))],
        name=name,
    )(v)


def pair_add(g, r1, core, *, name):
    _, _, rows, cols = g.shape
    tm = _pick(rows, (256, 128))

    def body(c_ref, g_ref, r_ref, o_ref):
        o_ref[...] = (g_ref[...].astype(F32) + r_ref[...].astype(F32)).astype(o_ref.dtype)

    return pl.pallas_call(
        body, out_shape=jax.ShapeDtypeStruct(r1.shape, g.dtype),
        grid_spec=pltpu.PrefetchScalarGridSpec(
            num_scalar_prefetch=1, grid=(N_CHIP, rows // tm),
            in_specs=[pl.BlockSpec((None, None, tm, cols), lambda k, i, c_ref: (k, c_ref[0], i, 0)),
                      pl.BlockSpec((None, tm, cols), lambda k, i, c_ref: (k, i, 0))],
            out_specs=pl.BlockSpec((None, tm, cols), lambda k, i, c_ref: (k, i, 0))),
        name=name, compiler_params=_cparams(("parallel", "parallel")),
    )(core, g, r1)


def _adamw_math(w, g, m, v):
    m = ADAM_B1 * m + (1.0 - ADAM_B1) * g
    v = ADAM_B2 * v + (1.0 - ADAM_B2) * (g * g)
    m_hat = m / (1.0 - ADAM_B1 ** ADAM_STEP)
    v_hat = v / (1.0 - ADAM_B2 ** ADAM_STEP)
    delta = -ADAM_LR * (m_hat / (jnp.sqrt(v_hat) + ADAM_EPS) + ADAM_WD * w)
    return delta, m, v


def adamw_sharded(w, m, v, layer, chip_sums, received, chip, into, *, name):
    _, rows, cols = w.shape
    tm = _pick(rows, (256, 128))

    def body(k_ref, w_ref, m_ref, v_ref, t_ref, r_ref, *rest):
        g_ref, d_ref, nm_ref, nv_ref, token_ref = rest[-5:]
        g = t_ref[...].astype(F32)
        for j in range(N_CHIP - 1):
            g = g + r_ref[j].astype(F32)
        d, mm, vv = _adamw_math(w_ref[...], g, m_ref[...], v_ref[...])
        g_ref[...] = g
        d_ref[...] = d
        nm_ref[...] = mm
        nv_ref[...] = vv
        token_ref[...] = jnp.zeros(TOKEN_SHAPE, F32)

    blk = pl.BlockSpec((None, tm, cols), lambda i, k_ref: (layer, i, 0))
    shp = jax.ShapeDtypeStruct(w.shape, F32)
    in_specs = [blk, blk, blk,
                pl.BlockSpec((None, tm, cols), lambda i, k_ref: (k_ref[0], i, 0)),
                pl.BlockSpec((N_CHIP - 1, tm, cols), lambda i, k_ref: (0, i, 0))]
    operands = [chip, w, m, v, chip_sums, received]
    aliases = {}
    if into is not None:
        aliases = {len(operands) + q: q for q in range(4)}
        in_specs += [_ANY] * 4
        operands += list(into)
    outs = pl.pallas_call(
        body, out_shape=(shp, shp, shp, shp, jax.ShapeDtypeStruct(TOKEN_SHAPE, F32)),
        grid_spec=pltpu.PrefetchScalarGridSpec(
            num_scalar_prefetch=1, grid=(rows // tm,), in_specs=in_specs,
            out_specs=(blk, blk, blk, blk, pl.BlockSpec(TOKEN_SHAPE, lambda i, k_ref: (0, 0)))),
        input_output_aliases=aliases,
        name=name, compiler_params=_cparams(("arbitrary",)),
    )(*operands)
    return outs[:4], outs[4]


def adamw_replicated(w, m, v, g, *, name):
    def body(w_ref, m_ref, v_ref, g_ref, d_ref, nm_ref, nv_ref):
        d, mm, vv = _adamw_math(w_ref[...], g_ref[...], m_ref[...], v_ref[...])
        d_ref[...] = d
        nm_ref[...] = mm
        nv_ref[...] = vv

    shp = jax.ShapeDtypeStruct(w.shape, F32)
    return pl.pallas_call(body, out_shape=(shp, shp, shp), name=name, compiler_params=_cparams())(w, m, v, g)


WEIGHT_NAMES = ("norm_w", "ssd_in_w", "ssd_conv_w", "ssd_conv_b", "ssd_dt_bias", "ssd_a_log", "ssd_d",
                "ssd_gnorm_w", "ssd_out_w", "sb_in_w", "sb_qn_w", "sb_kn_w", "sb_out_w", "ple_norm_w",
                "ple_gate_w", "ple_proj_w")
REPLICATED = ("norm_w", "ssd_conv_b", "ssd_dt_bias", "ssd_a_log", "ssd_d", "ssd_gnorm_w", "sb_qn_w", "sb_kn_w",
              "ple_norm_w")
PACK_ROWS = 8


def _pack(parts):
    flat = jnp.concatenate([t.reshape(-1) for t in parts])
    pad = (-flat.shape[0]) % (PACK_ROWS * LANES)
    return jnp.pad(flat, (0, pad)).reshape(-1, LANES)


def _unpack(packed, like):
    flat = packed.reshape(-1)
    out, off = [], 0
    for t in like:
        out.append(flat[off:off + t.size].reshape(t.shape))
        off += t.size
    return out


def _to_group_lanes(v, r):
    t = v.reshape(v.shape[0], SSD_N_GROUPS, r).transpose(1, 0, 2)
    return jnp.pad(t, ((0, 0), (0, 0), (0, LANES - r)))


def _from_group_lanes(t, r):
    return t[:, :, :r].transpose(1, 0, 2).reshape(t.shape[1], SSD_N_GROUPS * r)


def _head_vec(v, r):
    return jnp.pad(v.reshape(SSD_N_GROUPS, 1, r), ((0, 0), (0, 0), (0, LANES - r)))


def _col_blocks(full):
    rows = full.shape[0]
    return full.reshape(rows, N_DEV, -1).transpose(1, 0, 2)


def _from_col_blocks(blocks):
    return blocks.transpose(1, 0, 2).reshape(blocks.shape[1], -1)


def _split_cols(full, widths):
    out, off = [], 0
    for w in widths:
        out.append(full[:, off:off + w])
        off += w
    return out


def kernel(x, p, norm_w, ssd_in_w, ssd_conv_w, ssd_conv_b, ssd_dt_bias, ssd_a_log, ssd_d, ssd_gnorm_w, ssd_out_w, sb_in_w, sb_qn_w, sb_kn_w, sb_out_w, ple_norm_w, ple_gate_w, ple_proj_w, loss_target, m_norm_w, m_ssd_in_w, m_ssd_conv_w, m_ssd_conv_b, m_ssd_dt_bias, m_ssd_a_log, m_ssd_d, m_ssd_gnorm_w, m_ssd_out_w, m_sb_in_w, m_sb_qn_w, m_sb_kn_w, m_sb_out_w, m_ple_norm_w, m_ple_gate_w, m_ple_proj_w, v_norm_w, v_ssd_in_w, v_ssd_conv_w, v_ssd_conv_b, v_ssd_dt_bias, v_ssd_a_log, v_ssd_d, v_ssd_gnorm_w, v_ssd_out_w, v_sb_in_w, v_sb_qn_w, v_sb_kn_w, v_sb_out_w, v_ple_norm_w, v_ple_gate_w, v_ple_proj_w):
    env = dict(locals())
    wts = {n: env[n] for n in WEIGHT_NAMES}
    mom1 = {n: env["m_" + n] for n in WEIGHT_NAMES}
    mom2 = {n: env["v_" + n] for n in WEIGHT_NAMES}

    s, d = x.shape[1], x.shape[2]
    depth = norm_w.shape[0]
    di = ssd_out_w.shape[1] * N_DEV
    n_heads = ssd_dt_bias.shape[1]
    hpg = n_heads // SSD_N_GROUPS
    nbc = SSD_N_GROUPS * SSD_D_STATE
    in_segs = (di, di, nbc, nbc, n_heads)
    conv_segs = (di, nbc, nbc)
    sb_w = sb_out_w.shape[1] * N_DEV
    selectors = ssd_selectors(hpg)
    xi, yi, ci = _mesh_pos()
    core = ci.astype(jnp.int32).reshape(1)
    chip = (2 * xi + yi).astype(jnp.int32).reshape(1)

    def layer_keys(i):
        j = i // 2
        mixer = [("ssd_in_w", j), ("ssd_conv_w", j), ("ssd_out_w", j)] if i % 2 == 0 else [("sb_in_w", j), ("sb_out_w", j)]
        return mixer + [("ple_gate_w", i), ("ple_proj_w", i)]

    me_block = 4 * xi + 2 * yi + ci

    def landing_zone(t):
        return lax.dynamic_update_index_in_dim(lax.empty((N_DEV,) + t.shape, t.dtype), t, me_block, 0)

    def groups(i):
        keys = layer_keys(i)
        return [keys[:2], keys[2:]] if i == 0 else [keys]

    gathers, prev = {}, []
    for i in range(depth):
        for q, keys in enumerate(groups(i)):
            shards = [wts[n][idx] for n, idx in keys]
            if prev:
                shards = lax.optimization_barrier((prev[0], shards))[1]
            lands = [landing_zone(t if n == "ssd_conv_w" else t.astype(BF16)) for (n, _), t in zip(keys, shards)]
            gathers[i, q] = ici_start("gather", [], lands, after=prev, name=f"ag{i}{'ab'[q]}_start")
            prev = [gathers[i, q][4]]
    all_started = prev[0]
    full, ssd_full, passing = {}, {}, {}

    def hand_over(i, q, after):
        _, lands = ici_wait("gather", gathers[i, q], after, name=f"ag{i}{'ab'[q]}_wait")
        passing[i, q] = ici_start("pass", [], [t.reshape(N_CHIP, 2, *t.shape[1:]) for t in lands],
                                  name=f"ag{i}{'ab'[q]}_pass_start")

    def arrive(i, q, after):
        _, lands = ici_wait("pass", passing[i, q], after, name=f"ag{i}{'ab'[q]}_pass_wait")
        for k, t in zip(groups(i)[q], lands):
            full[k] = t.reshape(N_DEV, *t.shape[2:])

    def w_out_of(i):
        return full["ssd_out_w", i // 2].reshape(di, d) if i % 2 == 0 else full["sb_out_w", i // 2].reshape(sb_w, d)

    h = x.reshape(s, d)
    saved = []
    hand_over(0, 0, [all_started])
    arrive(0, 0, [all_started])
    for i in range(depth):
        j = i // 2
        if i > 0:
            arrive(i, 0, [h])
        sv = dict(h_in=h)
        u = rmsnorm_fwd(h, norm_w[i], name=f"l{i}_norm")
        sv["u"] = u
        if i % 2 == 0:
            fw = ssd_full[j] = dict(
                w_in=_split_cols(_from_col_blocks(full["ssd_in_w", j]), in_segs),
                conv_w=_split_cols(_from_col_blocks(full["ssd_conv_w", j]), conv_segs),
                conv_b=_split_cols(ssd_conv_b[j].reshape(1, -1), conv_segs))
            raw = [matmul(u, wseg, name=f"l{i}_in{q}") for q, wseg in enumerate(fw["w_in"])]
            if i == 0:
                hand_over(0, 1, [raw[4]])
            z, dt_raw = raw[0], raw[4]
            act = [ssd_conv_fwd(raw[1 + q], fw["conv_w"][q], fw["conv_b"][q], name=f"l{i}_conv{q}") for q in range(3)]
            dt, a_neg = ssd_dt_fwd(dt_raw, ssd_dt_bias[j], ssd_a_log[j], name=f"l{i}_dt")
            dtp = _to_group_lanes(dt, hpg)
            a_g = _head_vec(a_neg.reshape(-1), hpg)
            d_x = jnp.repeat(ssd_d[j].reshape(SSD_N_GROUPS, 1, hpg), SSD_HEAD_DIM, axis=2)
            y, states = ssd_scan_fwd(act[0], act[1], act[2], dtp, a_g, d_x, selectors, heads_per_group=hpg,
                                     name=f"l{i}_scan")
            yn = ssd_gate_fwd(y, z, ssd_gnorm_w[j], name=f"l{i}_gate")
            if i == 0:
                arrive(0, 1, [yn])
            h1 = matmul(yn, w_out_of(i), res=h, name=f"l{i}_out")
            sv.update(raw=raw, act=act, dt=dt, dtp=dtp, a_g=a_g, d_x=d_x, y=y, states=states, yn=yn)
        else:
            proj = matmul(u, full["sb_in_w", j], name=f"l{i}_in")
            qn, kn = sb_qk_fwd(proj, sb_qn_w[j], sb_kn_w[j], name=f"l{i}_qknorm")
            v_off = 2 * sb_w // SB_HEAD_DIM
            o, tot = sb_attn_fwd(qn, kn, proj, v_off=v_off, name=f"l{i}_attn")
            og = sb_gate_fwd(o, proj, name=f"l{i}_gate")
            h1 = matmul(og, w_out_of(i), res=h, name=f"l{i}_out")
            sv.update(proj=proj, qn=qn, kn=kn, o=o, tot=tot, og=og, v_off=v_off)
        if i + 1 < depth:
            hand_over(i + 1, 0, [h1])
        t = rmsnorm_fwd(h1, ple_norm_w[i], passing[i + 1, 0][4] if i + 1 < depth else None, name=f"l{i}_plenorm")
        gate_pre = matmul(t, full["ple_gate_w", i].reshape(d, d), name=f"l{i}_plegate")
        pp = matmul(p[i, 0], full["ple_proj_w", i], name=f"l{i}_pleproj")
        h = ple_fwd(h1, gate_pre, pp, name=f"l{i}_ple")
        sv.update(h1=h1, t=t, gate_pre=gate_pre, pp=pp)
        saved.append(sv)

    loss_part, dh = loss_head(h, loss_target.reshape(s, d), name="loss_head")
    loss = lax.psum(loss_part[0, 0], ("x", "y", "c"))

    big = {}
    small = {n: [None] * wts[n].shape[0] for n in REPLICATED}
    swaps, scatters = {}, {}
    order_after = jnp.zeros(TOKEN_SHAPE, F32)
    pending = None

    def send_to_sibling(i, q):
        blocks = [big[k].reshape(N_CHIP, 2, *big[k].shape[1:]) for k in groups(i)[::-1][q]]
        swaps[i, q] = ici_start("swap", blocks, [lax.empty((N_CHIP,) + t.shape[2:], t.dtype) for t in blocks],
                                name=f"rs{i}{'ab'[q]}_swap_start")
        return swaps[i, q][4]

    def send_to_chips(i, q, after):
        blocks, from_sibling = ici_wait("swap", swaps[i, q], after, name=f"rs{i}{'ab'[q]}_swap_wait")
        sums = [pair_add(g, r1, core, name=f"rs{i}{'ab'[q]}_pair_add{a}")
                for a, (g, r1) in enumerate(zip(blocks, from_sibling))]
        scatters[i, q] = ici_start("scatter", sums, [lax.empty((N_PEER_CHIPS,) + t.shape[1:], t.dtype) for t in sums],
                                   name=f"rs{i}{'ab'[q]}_start")
        return scatters[i, q][4]

    for i in reversed(range(depth)):
        j = i // 2
        sv = saved[i]
        dpp, dgp = ple_bwd(dh, sv["gate_pre"], sv["pp"], order_after, name=f"b{i}_ple")
        big["ple_proj_w", i] = matmul(p[i, 0], dpp, mode="tn", out_dtype=BF16, out_blocks=ple_proj_w.shape[2],
                                      name=f"b{i}_pleproj_w")
        big["ple_gate_w", i] = matmul(sv["t"], dgp, mode="tn", out_dtype=BF16, name=f"b{i}_plegate_w").reshape(N_DEV, -1, d)
        dt_ = matmul(dgp, full["ple_gate_w", i].reshape(d, d), mode="nt", name=f"b{i}_plegate_x")
        dh1, g_pn = rmsnorm_bwd(sv["h1"], ple_norm_w[i], dt_, dh, name=f"b{i}_plenorm")
        small["ple_norm_w"][i] = g_pn
        behind = send_to_chips(*pending, [dh1]) if pending is not None else None
        pending = None
        u = sv["u"]
        if i % 2 == 0:
            fw = ssd_full[j]
            raw, act = sv["raw"], sv["act"]
            big["ssd_out_w", j] = matmul(sv["yn"], dh1, mode="tn", out_dtype=BF16, name=f"b{i}_out_w").reshape(N_DEV, -1, d)
            if i == 0:
                send_to_sibling(0, 0)
            dyn = matmul(dh1, w_out_of(i), mode="nt", after=behind, name=f"b{i}_out_x")
            dy, dz, g_gn = ssd_gate_bwd(sv["y"], raw[0], ssd_gnorm_w[j], dyn, name=f"b{i}_gate")
            dxs, dbm, dcm, ddtp, dadtp, dd_g = ssd_scan_bwd(act[0], act[1], act[2], sv["dtp"], sv["a_g"], sv["d_x"], selectors,
                                                          sv["states"], dy, heads_per_group=hpg, name=f"b{i}_scan")
            behind = send_to_chips(0, 0, [dxs]) if i == 0 else None
            ddt_raw, g_dtb, g_alog = ssd_dt_bwd(raw[4], ssd_dt_bias[j], ssd_a_log[j], sv["dt"],
                                                _from_group_lanes(ddtp, hpg), _from_group_lanes(dadtp, hpg),
                                                name=f"b{i}_dt")
            conv_back = [ssd_conv_bwd(raw[1 + q], fw["conv_w"][q], fw["conv_b"][q], dact, name=f"b{i}_conv{q}")
                         for q, dact in enumerate((dxs, dbm, dcm))]
            dsegs = [dz] + [cb[0] for cb in conv_back] + [ddt_raw]
            g_in = jnp.concatenate([matmul(u, ds, mode="tn", out_dtype=BF16, name=f"b{i}_in{q}_w")
                                    for q, ds in enumerate(dsegs)], axis=1)
            big["ssd_in_w", j] = _col_blocks(g_in)
            big["ssd_conv_w", j] = _col_blocks(jnp.concatenate([cb[1] for cb in conv_back], axis=1))
            du = None
            for q, (ds, wseg) in enumerate(zip(dsegs, fw["w_in"])):
                du = matmul(ds, wseg, mode="nt", res=du, after=behind if q == 0 else None, name=f"b{i}_in{q}_x")
            small["ssd_conv_b"][j] = jnp.concatenate([cb[2] for cb in conv_back], axis=1)
            small["ssd_dt_bias"][j] = g_dtb
            small["ssd_a_log"][j] = g_alog
            small["ssd_d"][j] = dd_g[:, 0, :hpg]
            small["ssd_gnorm_w"][j] = g_gn
        else:
            proj = sv["proj"]
            big["sb_out_w", j] = matmul(sv["og"], dh1, mode="tn", out_dtype=BF16, name=f"b{i}_out_w").reshape(N_DEV, -1, d)
            dog = matmul(dh1, w_out_of(i), mode="nt", after=behind, name=f"b{i}_out_x")
            do, dg = sb_gate_bwd(dog, sv["o"], proj, name=f"b{i}_gate")
            dqn, dkn, dv = sb_attn_bwd(sv["qn"], sv["kn"], proj, sv["tot"], do, v_off=sv["v_off"], name=f"b{i}_attn")
            dproj, g_qn, g_kn = sb_pack_bwd(proj, sb_qn_w[j], sb_kn_w[j], dqn, dkn, dv, dg, name=f"b{i}_qknorm")
            big["sb_in_w", j] = matmul(u, dproj, mode="tn", out_dtype=BF16, out_blocks=sb_in_w.shape[2], name=f"b{i}_in_w")
            du = matmul(dproj, full["sb_in_w", j], mode="nt", name=f"b{i}_in_x")
            small["sb_qn_w"][j] = g_qn
            small["sb_kn_w"][j] = g_kn
        dh, g_n = rmsnorm_bwd(sv["h_in"], norm_w[i], du, dh1, name=f"b{i}_norm")
        small["norm_w"][i] = g_n
        pending = (i, len(groups(i)) - 1)
        order_after = send_to_sibling(*pending)
    send_to_chips(*pending, [dh])
    grad_x = dh.reshape(x.shape)

    rep_like = [wts[n] for n in REPLICATED]
    g_packed = all_reduce_small(_pack([jnp.stack([t.reshape(-1) for t in small[n]]) for n in REPLICATED]),
                                name="all_reduce_small_grads")
    d_packed, m_packed, v_packed = adamw_replicated(
        _pack(rep_like), _pack([mom1[n] for n in REPLICATED]), _pack([mom2[n] for n in REPLICATED]), g_packed,
        name="adamw_replicated")
    grads = dict(zip(REPLICATED, _unpack(g_packed, rep_like)))
    deltas = dict(zip(REPLICATED, _unpack(d_packed, rep_like)))
    new_m = dict(zip(REPLICATED, _unpack(m_packed, rep_like)))
    new_v = dict(zip(REPLICATED, _unpack(v_packed, rep_like)))

    updated = {}
    after = [scatters[0, len(groups(0)) - 1][4]]
    for i in reversed(range(depth)):
        for q, keys in enumerate(groups(i)[::-1]):
            sums, received = ici_wait("scatter", scatters[i, q], after, name=f"rs{i}{'ab'[q]}_wait")
            after = []
            for (n, idx), t_sum, recv in zip(keys, sums, received):
                updated[n], done = adamw_sharded(wts[n], mom1[n], mom2[n], idx, t_sum, recv, chip, updated.get(n),
                                                 name=f"adamw_{n}{idx}")
                after.append(done)
    for n, (g_n, d_n, m_n, v_n) in updated.items():
        grads[n], deltas[n], new_m[n], new_v[n] = g_n, d_n, m_n, v_n

    return (loss, grad_x, *[grads[n] for n in WEIGHT_NAMES], *[deltas[n] for n in WEIGHT_NAMES],
            *[new_m[n] for n in WEIGHT_NAMES], *[new_v[n] for n in WEIGHT_NAMES])
```

```python
import functools
import math

import jax
import jax.numpy as jnp
from jax import lax
from jax.experimental import pallas as pl
from jax.experimental.pallas import tpu as pltpu

F32 = jnp.float32
BF16 = jnp.bfloat16
MESH = pl.DeviceIdType.MESH

N_DEV = 8
N_CHIP = 4
LANES = 128
VMEM_LIMIT_BYTES = 56 * 1024 * 1024
MATMUL_TILE_BYTES = 36 * 1024 * 1024

NORM_EPS = 1e-6
GATED_NORM_EPS = 1e-5
SSD_HEAD_DIM = 64
SSD_N_GROUPS = 8
SSD_D_STATE = 128
SSD_D_CONV = 4
SSD_CHUNK = 128
SSD_FWD_CHUNKS_PER_STEP = 8
SSD_BWD_CHUNKS_PER_STEP = 8
SB_HEAD_DIM = 128

ADAM_LR = 0.001
ADAM_B1 = 0.9
ADAM_B2 = 0.999
ADAM_EPS = 1e-08
ADAM_WD = 0.01
ADAM_STEP = 10


def _cparams(sem=None, **kw):
    return pltpu.CompilerParams(dimension_semantics=sem, vmem_limit_bytes=VMEM_LIMIT_BYTES, **kw)


def _pick(dim, prefs):
    for t in prefs:
        if dim % t == 0:
            return t
    return dim


def _sigmoid(x):
    return 1.0 / (1.0 + jnp.exp(-x))


def _silu(x):
    return x * _sigmoid(x)


def _silu_grad(x):
    s = _sigmoid(x)
    return s * (1.0 + x * (1.0 - s))


def matmul(a, b, *, mode="nn", out_dtype=F32, res=None, out_blocks=None, after=None, name):
    b_blocked = b.ndim == 3
    if mode == "nn":
        m, kc = a.shape
        n = b.shape[-1] * (N_DEV if b_blocked else 1)
    elif mode == "nt":
        m, kc = a.shape
        n = b.shape[-2]
    else:
        kc, m = a.shape
        n = b.shape[-1]
    nb = b.shape[-1] if b_blocked else None
    tn = _pick(n if not out_blocks else out_blocks, (512, 256, 128))
    if b_blocked and mode == "nn":
        tn = _pick(nb, (512, 256, 128))
    k_unit = nb if (b_blocked and mode == "nt") else 1
    tm, tk = None, None
    for tm_try in (1024, 512, 256, 128):
        if m % tm_try:
            continue
        for tk_try in (kc, kc // 2, kc // 4, 2048, 1024, 512, 256, 128):
            if tk_try > kc or tk_try < k_unit or kc % tk_try or tk_try % k_unit:
                continue
            tiles = 2 * (tm_try * tk_try * a.dtype.itemsize + tk_try * tn * b.dtype.itemsize)
            tiles += tm_try * tn * (2 * jnp.dtype(out_dtype).itemsize + 4 + (8 if res is not None else 0))
            if tiles <= MATMUL_TILE_BYTES:
                tm, tk = tm_try, tk_try
                break
        if tm:
            break
    if tm is None:
        tm, tk = m, max(k_unit, LANES if kc % LANES == 0 else kc)
    nk = kc // tk
    grid = (m // tm, n // tn, nk)

    if mode == "tn":
        a_spec = pl.BlockSpec((tk, tm), lambda i, j, k: (k, i))
        dims = (((0,), (0,)), ((), ()))
    else:
        a_spec = pl.BlockSpec((tm, tk), lambda i, j, k: (i, k))
        dims = (((1,), (0,)), ((), ())) if mode == "nn" else (((1,), (1,)), ((), ()))
    if mode == "nt":
        if b_blocked:
            b_spec = pl.BlockSpec((tk // nb, tn, nb), lambda i, j, k: (k, j, 0))
        else:
            b_spec = pl.BlockSpec((tn, tk), lambda i, j, k: (j, k))
    else:
        if b_blocked:
            per = nb // tn
            b_spec = pl.BlockSpec((None, tk, tn), lambda i, j, k: (j // per, k, j % per))
        else:
            b_spec = pl.BlockSpec((tk, tn), lambda i, j, k: (k, j))
    if out_blocks:
        per_o = out_blocks // tn
        out_shape = jax.ShapeDtypeStruct((n // out_blocks, m, out_blocks), out_dtype)
        out_spec = pl.BlockSpec((None, tm, tn), lambda i, j, k: (j // per_o, i, j % per_o))
    else:
        out_shape = jax.ShapeDtypeStruct((m, n), out_dtype)
        out_spec = pl.BlockSpec((tm, tn), lambda i, j, k: (i, j))
    in_specs = [a_spec, b_spec]
    args = [a, b]
    if res is not None:
        in_specs.append(pl.BlockSpec((tm, tn), lambda i, j, k: (i, j)))
        args.append(res)
    if after is not None:
        in_specs.append(pl.BlockSpec(memory_space=pl.ANY))
        args.append(after)
    n_in = len(args)

    def body(*refs):
        a_ref, b_ref = refs[:2]
        r_ref = refs[2] if res is not None else None
        o_ref = refs[n_in]

        def finish(r):
            if res is not None:
                r = r + r_ref[...].astype(F32)
            o_ref[...] = r.astype(out_dtype)

        if b_blocked and mode == "nt":
            part = None
            for blk in range(tk // nb):
                term = lax.dot_general(a_ref[:, blk * nb:(blk + 1) * nb].astype(BF16), b_ref[blk].astype(BF16), dims,
                                       preferred_element_type=F32)
                part = term if part is None else part + term
        else:
            part = lax.dot_general(a_ref[...].astype(BF16), b_ref[...].astype(BF16), dims, preferred_element_type=F32)
        if nk == 1:
            finish(part)
            return
        acc_ref = refs[-1]
        k = pl.program_id(2)

        @pl.when(k == 0)
        def _():
            acc_ref[...] = part

        @pl.when(k > 0)
        def _():
            acc_ref[...] += part

        @pl.when(k == nk - 1)
        def _():
            finish(acc_ref[...])

    return pl.pallas_call(
        body, out_shape=out_shape, grid=grid, in_specs=in_specs, out_specs=out_spec,
        scratch_shapes=[] if nk == 1 else [pltpu.VMEM((tm, tn), F32)], name=name,
        compiler_params=_cparams(("parallel", "parallel", "arbitrary")),
    )(*args)


def _dot(a, b, dims, precision=None):
    return lax.dot_general(a, b, (dims, ((), ())), preferred_element_type=F32, precision=precision)


_NN = ((1,), (0,))
_NT = ((1,), (1,))
_TN = ((0,), (0,))
_EXACT = lax.Precision.HIGHEST


def _chunk_decay_terms(dt, a):
    ln = dt.shape[0]
    row = lax.broadcasted_iota(jnp.int32, (ln, ln), 0)
    col = lax.broadcasted_iota(jnp.int32, (ln, ln), 1)
    tri = (row >= col).astype(F32)
    a_col = _dot(tri, dt * a, _NN, _EXACT)
    return a_col, a_col.T, row >= col


def _exact_dot(x, sel, terms):
    t = x.shape[0]
    parts, rest = [], x
    for k in range(terms):
        piece = rest.astype(BF16)
        parts.append(piece)
        if k + 1 < terms:
            rest = rest - piece.astype(F32)
    r = _dot(jnp.concatenate(parts, axis=0), sel, _NN)
    out = r[:t]
    for k in range(1, terms):
        out = out + r[k * t:(k + 1) * t]
    return out


def ssd_selectors(r_n):
    lane = jnp.arange(LANES)
    spread64 = (lane[:, None] == jnp.arange(r_n * SSD_HEAD_DIM)[None, :] // SSD_HEAD_DIM).astype(BF16)
    pair_sum = jnp.stack([lane[None, :] == 2 * q + lane[:, None] // SSD_HEAD_DIM for q in range(r_n // 2)]).astype(BF16)
    row_sum = jnp.stack([jnp.broadcast_to(lane[None, :] == r, (LANES, LANES)) for r in range(r_n)]).astype(BF16)
    return spread64, pair_sum, row_sum


def _ssd_chunk_setup(dt, a, spread64):
    ln = dt.shape[0]
    a_col, a_row, causal = _chunk_decay_terms(dt, a)
    ea = jnp.exp(a_col)
    te = jnp.exp(a_col[ln - 1:ln, :] - a_col)
    return (a_row, a_col, _exact_dot(dt, spread64, 2), _exact_dot(ea, spread64, 2), _exact_dot(te, spread64, 2),
            ea, causal)


def ssd_scan_fwd(xs, bm, cm, dtp, a_g, d_x, selectors, *, heads_per_group, name):
    s, di = xs.shape
    g_n = SSD_N_GROUPS
    r_n, p_n, n_n, ln = heads_per_group, SSD_HEAD_DIM, SSD_D_STATE, SSD_CHUNK
    nc = s // ln
    cps = _pick(nc, (SSD_FWD_CHUNKS_PER_STEP, 4, 2, 1))
    pairs, pw = r_n // 2, 2 * p_n
    spread64 = selectors[0]

    def body(xs_ref, bm_ref, cm_ref, dt_ref, a_ref, d_ref, s64_ref, y_ref, st_ref, state):
        c = pl.program_id(1)

        @pl.when(c == 0)
        def _():
            state[...] = jnp.zeros_like(state)

        first_head = lax.broadcasted_iota(jnp.int32, (1, pw), 1) < p_n
        for sub in range(cps):
            rows = slice(sub * ln, (sub + 1) * ln)
            a_row, a_col, dt_x, ea_x, te_x, _, causal = _ssd_chunk_setup(dt_ref[rows, :], a_ref[...], s64_ref[...])
            bm_f = bm_ref[rows, :]
            bmb = bm_f.astype(BF16)
            bm_t = bm_f.T.astype(BF16)
            cmb = cm_ref[rows, :].astype(BF16)
            scores = _dot(cmb, bmb, _NT)
            for q in range(pairs):
                sl = slice(q * pw, (q + 1) * pw)
                x2 = xs_ref[rows, sl]
                xdt2 = x2 * dt_x[:, sl]
                xdt2b = xdt2.astype(BF16)
                y_heads = []
                for r in (2 * q, 2 * q + 1):
                    decay = jnp.exp(jnp.where(causal, a_col[:, r:r + 1] - a_row[r:r + 1, :], -jnp.inf))
                    y_heads.append(_dot((scores * decay).astype(BF16), xdt2b, _NN))
                s2t = state[q]
                st_ref[sub, q] = s2t
                y2 = jnp.where(first_head, y_heads[0], y_heads[1])
                y2 = y2 + ea_x[:, sl] * _dot(cmb, s2t.astype(BF16), _NN)
                y_ref[rows, sl] = y2 + d_ref[:, sl] * x2
                state[q] = s2t * ea_x[ln - 1:ln, sl] + _dot(bm_t, (xdt2 * te_x[:, sl]).astype(BF16), _NN)

    whole = lambda t: pl.BlockSpec(t.shape, lambda g, c: (0,) * t.ndim)
    step = cps * ln
    return pl.pallas_call(
        body,
        out_shape=(jax.ShapeDtypeStruct((s, di), F32),
                   jax.ShapeDtypeStruct((nc, g_n * pairs, n_n, pw), F32)),
        grid=(g_n, nc // cps),
        in_specs=[pl.BlockSpec((step, r_n * p_n), lambda g, c: (c, g)),
                  pl.BlockSpec((step, n_n), lambda g, c: (c, g)),
                  pl.BlockSpec((step, n_n), lambda g, c: (c, g)),
                  pl.BlockSpec((None, step, LANES), lambda g, c: (g, c, 0)),
                  pl.BlockSpec((None, 1, LANES), lambda g, c: (g, 0, 0)),
                  pl.BlockSpec((None, 1, r_n * p_n), lambda g, c: (g, 0, 0)),
                  whole(spread64)],
        out_specs=(pl.BlockSpec((step, r_n * p_n), lambda g, c: (c, g)),
                   pl.BlockSpec((cps, pairs, n_n, pw), lambda g, c: (c, g, 0, 0))),
        scratch_shapes=[pltpu.VMEM((pairs, n_n, pw), F32)],
        name=name, compiler_params=_cparams(("parallel", "arbitrary")),
    )(xs, bm, cm, dtp, a_g, d_x, spread64)


def _row8(v):
    return jnp.broadcast_to(v, (8, v.shape[1]))


def ssd_scan_bwd(xs, bm, cm, dtp, a_g, d_x, selectors, states, dy, *, heads_per_group, name):
    s, di = xs.shape
    g_n = SSD_N_GROUPS
    r_n, p_n, n_n, ln = heads_per_group, SSD_HEAD_DIM, SSD_D_STATE, SSD_CHUNK
    nc = s // ln
    cps = _pick(nc, (SSD_BWD_CHUNKS_PER_STEP, 2, 1))
    pairs, pw = r_n // 2, 2 * p_n
    spread64, pair_sum, row_sum = selectors

    def body(xs_ref, bm_ref, cm_ref, dt_ref, a_ref, d_ref, s64_ref, ps_ref, rs_ref, st_ref, dy_ref,
             dxs_ref, dbm_ref, dcm_ref, ddt_ref, dadt_ref, dd_ref, dstate, da_rows):
        c = pl.program_id(1)

        @pl.when(c == 0)
        def _():
            dstate[...] = jnp.zeros_like(dstate)
            dd_ref[...] = jnp.zeros_like(dd_ref)

        row = lax.broadcasted_iota(jnp.int32, (ln, ln), 0)
        col = lax.broadcasted_iota(jnp.int32, (ln, ln), 1)
        causal_t = col >= row
        upper = causal_t.astype(F32)
        first_head = lax.broadcasted_iota(jnp.int32, (1, pw), 1) < p_n
        for sub in reversed(range(cps)):
            rows = slice(sub * ln, (sub + 1) * ln)
            a_row, a_col, dt_x, ea_x, te_x, ea, causal = _ssd_chunk_setup(dt_ref[rows, :], a_ref[...], s64_ref[...])
            bmb = bm_ref[rows, :].astype(BF16)
            cm_f = cm_ref[rows, :]
            cmb = cm_f.astype(BF16)
            cm_t = cm_f.T.astype(BF16)
            scores = _dot(cmb, bmb, _NT)
            scores_t = _dot(bmb, cmb, _NT)
            e_last = ea[ln - 1:ln, :]
            da_rows[...] = jnp.zeros_like(da_rows)
            dscores = jnp.zeros((ln, ln), F32)
            dcm = jnp.zeros((ln, n_n), F32)
            dbm = jnp.zeros((ln, n_n), F32)
            da_cols = jnp.zeros((ln, LANES), F32)
            da_last = jnp.zeros((1, LANES), F32)
            ddt = jnp.zeros((ln, LANES), F32)
            dd = jnp.zeros((1, LANES), F32)
            for q in range(pairs):
                sl = slice(q * pw, (q + 1) * pw)
                sum2 = ps_ref[q]
                x2 = xs_ref[rows, sl]
                dt2 = dt_x[:, sl]
                xdt2 = x2 * dt2
                xdt2b = xdt2.astype(BF16)
                dy2 = dy_ref[rows, sl]
                dy2b = dy2.astype(BF16)
                dxdt_heads = []
                for h, r in enumerate((2 * q, 2 * q + 1)):
                    a_r = jnp.broadcast_to(a_col[:, r:r + 1], (ln, ln))
                    decay = jnp.exp(jnp.where(causal, a_r - a_row[r:r + 1, :], -jnp.inf))
                    decay_t = jnp.exp(jnp.where(causal_t, a_row[r:r + 1, :] - a_r, -jnp.inf))
                    dy_h = jnp.where(first_head if h == 0 else jnp.logical_not(first_head), dy2, 0.0).astype(BF16)
                    dm = _dot(dy_h, xdt2b, _NT)
                    dscores = dscores + dm * decay
                    e_mat = dm * (scores * decay)
                    da_cols = da_cols + _exact_dot(e_mat, rs_ref[r], 2)
                    da_rows[r:r + 1, :] = -jnp.sum(e_mat, axis=0, keepdims=True)
                    dxdt_heads.append(_dot((scores_t * decay_t).astype(BF16), dy2b, _NN))
                dxdt2 = jnp.where(first_head, dxdt_heads[0], dxdt_heads[1])
                s2t = st_ref[sub, q]
                s2tb = s2t.astype(BF16)
                ds2t = dstate[q]
                ds2tb = ds2t.astype(BF16)
                ea2, te2 = ea_x[:, sl], te_x[:, sl]
                y_off2 = ea2 * _dot(cmb, s2tb, _NN)
                dy_e2 = (dy2 * ea2).astype(BF16)
                dcm = dcm + _dot(dy_e2, s2tb, _NT)
                ds_in = _dot(cm_t, dy_e2, _NN)
                da_cols = da_cols + _exact_dot(dy2 * y_off2, sum2, 2)
                bds2 = _dot(bmb, ds2tb, _NN)
                dxdt2 = dxdt2 + te2 * bds2
                xdt_e2 = xdt2 * te2
                dbm = dbm + _dot(xdt_e2.astype(BF16), ds2tb, _NT)
                w_cols = _exact_dot(xdt_e2 * bds2, sum2, 2)
                da_cols = da_cols - w_cols
                state_dot = _exact_dot(_row8(jnp.sum(ds2t * s2t, axis=0, keepdims=True)), sum2, 2)[0:1]
                da_last = da_last + jnp.sum(w_cols, axis=0, keepdims=True) + e_last * state_dot
                dstate[q] = ds2t * ea_x[ln - 1:ln, sl] + ds_in
                dxs_ref[rows, sl] = dxdt2 * dt2 + d_ref[:, sl] * dy2
                ddt = ddt + _exact_dot(dxdt2 * x2, sum2, 2)
                dd = dd + _exact_dot(_row8(jnp.sum(dy2 * x2, axis=0, keepdims=True)), sum2, 2)[0:1]
            dcm_ref[rows, :] = dcm + _dot(dscores.astype(BF16), bmb, _NN)
            dbm_ref[rows, :] = dbm + _dot(dscores.T.astype(BF16), cmb, _NN)
            da_total = da_cols + da_rows[...].T
            dadt_ref[rows, :] = _dot(upper, da_total, _NN, _EXACT) + da_last
            ddt_ref[rows, :] = ddt
            dd_ref[...] += dd

    step, last_c = cps * ln, nc // cps - 1
    whole = lambda t: pl.BlockSpec(t.shape, lambda g, c: (0,) * t.ndim)
    return pl.pallas_call(
        body,
        out_shape=(jax.ShapeDtypeStruct((s, di), F32),
                   jax.ShapeDtypeStruct(bm.shape, F32),
                   jax.ShapeDtypeStruct(cm.shape, F32),
                   jax.ShapeDtypeStruct(dtp.shape, F32),
                   jax.ShapeDtypeStruct(dtp.shape, F32),
                   jax.ShapeDtypeStruct(a_g.shape, F32)),
        grid=(g_n, nc // cps),
        in_specs=[pl.BlockSpec((step, r_n * p_n), lambda g, c: (last_c - c, g)),
                  pl.BlockSpec((step, n_n), lambda g, c: (last_c - c, g)),
                  pl.BlockSpec((step, n_n), lambda g, c: (last_c - c, g)),
                  pl.BlockSpec((None, step, LANES), lambda g, c: (g, last_c - c, 0)),
                  pl.BlockSpec((None, 1, LANES), lambda g, c: (g, 0, 0)),
                  pl.BlockSpec((None, 1, r_n * p_n), lambda g, c: (g, 0, 0)),
                  whole(spread64), whole(pair_sum), whole(row_sum),
                  pl.BlockSpec((cps, pairs, n_n, pw), lambda g, c: (last_c - c, g, 0, 0)),
                  pl.BlockSpec((step, r_n * p_n), lambda g, c: (last_c - c, g))],
        out_specs=(pl.BlockSpec((step, r_n * p_n), lambda g, c: (last_c - c, g)),
                   pl.BlockSpec((step, n_n), lambda g, c: (last_c - c, g)),
                   pl.BlockSpec((step, n_n), lambda g, c: (last_c - c, g)),
                   pl.BlockSpec((None, step, LANES), lambda g, c: (g, last_c - c, 0)),
                   pl.BlockSpec((None, step, LANES), lambda g, c: (g, last_c - c, 0)),
                   pl.BlockSpec((None, 1, LANES), lambda g, c: (g, 0, 0))),
        scratch_shapes=[pltpu.VMEM((pairs, n_n, pw), F32), pltpu.VMEM((LANES, ln), F32)],
        name=name, compiler_params=_cparams(("parallel", "arbitrary")),
    )(xs, bm, cm, dtp, a_g, d_x, spread64, pair_sum, row_sum, states, dy)


SB_Q_TILE = 2048
SB_K_TILE = 256


def _tri_sum(x, tri):
    t = x.shape[0]
    hi = x.astype(BF16)
    r1 = x - hi.astype(F32)
    mid = r1.astype(BF16)
    lo = (r1 - mid.astype(F32)).astype(BF16)
    r = _dot(jnp.concatenate([hi, mid, lo], axis=0), tri, _NN)
    return r[:t] + r[t:2 * t] + r[2 * t:]


def _sb_logits(q, k_j, scale, strict):
    z = _dot(q, k_j, _NT) * scale
    sp = jnp.log(1.0 + jnp.exp(-jnp.abs(z)))
    log_b = jnp.minimum(z, 0.0) - sp
    log_1mb = log_b - z
    if strict is not None:
        log_1mb = jnp.where(strict, log_1mb, 0.0)
    return log_b, log_1mb


def _sb_tiles(s):
    tq = _pick(s, (SB_Q_TILE, 2 * SB_K_TILE, SB_K_TILE, LANES))
    return tq, min(tq, SB_K_TILE)


def _sb_diag_mask(rows, tk):
    return lax.broadcasted_iota(jnp.int32, (rows, tk), 1) < lax.broadcasted_iota(jnp.int32, (rows, tk), 0)


def _sb_iotas(t):
    row = lax.broadcasted_iota(jnp.int32, (t, t), 0)
    col = lax.broadcasted_iota(jnp.int32, (t, t), 1)
    return row, col


def sb_attn_fwd(qn, kn, v, *, v_off=0, name):
    s, w = qn.shape
    dh = SB_HEAD_DIM
    n_h = w // dh
    tq, tk = _sb_tiles(s)
    per = tq // tk
    scale = 1.0 / math.sqrt(dh)

    def body(q_ref, k_ref, v_ref, o_ref, tot_ref):
        i = pl.program_id(1)
        q = q_ref[...]
        row, col = _sb_iotas(tk)
        later = (row > col).astype(BF16)

        def tile(q_rows, j, acc, run, mask):
            s0 = pl.multiple_of(j * tk, tk)
            k_j = k_ref[pl.ds(s0, tk), :]
            v_j = v_ref[pl.ds(s0, tk), :].astype(BF16)
            log_b, log_1mb = _sb_logits(q_rows, k_j, scale, mask)
            att = jnp.exp(log_b + (_tri_sum(log_1mb, later) + run))
            if mask is not None:
                att = jnp.where(mask, att, 0.0)
            return acc + _dot(att.astype(BF16), v_j, _NN), run + jnp.sum(log_1mb, axis=1, keepdims=True)

        acc, run = jnp.zeros((tq, dh), F32), jnp.zeros((tq, 1), F32)
        for d in reversed(range(per)):
            r0 = d * tk
            a2, r2 = tile(q[r0:], i * per + d, acc[r0:], run[r0:], _sb_diag_mask(tq - r0, tk))
            acc = a2 if r0 == 0 else jnp.concatenate([acc[:r0], a2], axis=0)
            run = r2 if r0 == 0 else jnp.concatenate([run[:r0], r2], axis=0)

        def group(gg, c):
            for d in reversed(range(per)):
                c = tile(q, (i - 1 - gg) * per + d, c[0], c[1], None)
            return c

        acc, run = lax.fori_loop(0, i, group, (acc, run))
        o_ref[...] = acc
        tot_ref[...] = jnp.broadcast_to(run, (tq, dh))

    return pl.pallas_call(
        body,
        out_shape=(jax.ShapeDtypeStruct((s, w), F32), jax.ShapeDtypeStruct((s, w), F32)),
        grid=(n_h, s // tq),
        in_specs=[pl.BlockSpec((tq, dh), lambda h, i: (i, h)),
                  pl.BlockSpec((s, dh), lambda h, i: (0, h)),
                  pl.BlockSpec((s, dh), lambda h, i: (0, v_off + h))],
        out_specs=(pl.BlockSpec((tq, dh), lambda h, i: (i, h)),
                   pl.BlockSpec((tq, dh), lambda h, i: (i, h))),
        name=name, compiler_params=_cparams(("parallel", "parallel")),
    )(qn, kn, v)


def sb_attn_bwd(qn, kn, v, tot, do, *, v_off=0, name):
    s, w = qn.shape
    dh = SB_HEAD_DIM
    n_h = w // dh
    tq, tk = _sb_tiles(s)
    per = tq // tk
    scale = 1.0 / math.sqrt(dh)

    def body(q_ref, k_ref, v_ref, tot_ref, do_ref, dq_ref, dk_ref, dv_ref):
        dk_ref[...] = jnp.zeros_like(dk_ref)
        dv_ref[...] = jnp.zeros_like(dv_ref)
        row, col = _sb_iotas(tk)
        upto = (row <= col).astype(BF16)
        before = (row < col).astype(BF16)

        def q_block(i, _):
            t0 = pl.multiple_of(i * tq, tq)
            q = q_ref[pl.ds(t0, tq), :]
            do_i = do_ref[pl.ds(t0, tq), :].astype(BF16)
            total = tot_ref[pl.ds(t0, tq), 0:1]

            def tile(r0, j, dq, run_l, run_g, mask):
                s0 = pl.multiple_of(j * tk, tk)
                k_j = k_ref[pl.ds(s0, tk), :]
                v_j = v_ref[pl.ds(s0, tk), :].astype(BF16)
                q_r, do_r = q[r0:], do_i[r0:]
                log_b, log_1mb = _sb_logits(q_r, k_j, scale, mask)
                att = jnp.exp(log_b + ((total[r0:] - run_l) - _tri_sum(log_1mb, upto)))
                if mask is not None:
                    att = jnp.where(mask, att, 0.0)
                g = att * _dot(do_r, v_j, _NT)
                c = _tri_sum(g, before) + run_g
                dz = (g - (g + c) * jnp.exp(log_b)) * scale
                if mask is not None:
                    dz = jnp.where(mask, dz, 0.0)
                dz = dz.astype(BF16)
                dk_ref[pl.ds(s0, tk), :] += _dot(dz, q_r, _TN)
                dv_ref[pl.ds(s0, tk), :] += _dot(att.astype(BF16), do_r, _TN)
                return (dq + _dot(dz, k_j, _NN), run_l + jnp.sum(log_1mb, axis=1, keepdims=True),
                        run_g + jnp.sum(g, axis=1, keepdims=True))

            def group(gg, c):
                for d in range(per):
                    c = tile(0, gg * per + d, c[0], c[1], c[2], None)
                return c

            zero = jnp.zeros((tq, 1), F32)
            dq, run_l, run_g = lax.fori_loop(0, i, group, (jnp.zeros((tq, dh), F32), zero, zero))
            for d in range(per):
                r0 = d * tk
                p_dq, p_l, p_g = tile(r0, i * per + d, dq[r0:], run_l[r0:], run_g[r0:], _sb_diag_mask(tq - r0, tk))
                if r0 == 0:
                    dq, run_l, run_g = p_dq, p_l, p_g
                else:
                    dq = jnp.concatenate([dq[:r0], p_dq], axis=0)
                    run_l = jnp.concatenate([run_l[:r0], p_l], axis=0)
                    run_g = jnp.concatenate([run_g[:r0], p_g], axis=0)
            dq_ref[pl.ds(t0, tq), :] = dq
            return 0

        lax.fori_loop(0, s // tq, q_block, 0)

    head = pl.BlockSpec((s, dh), lambda h: (0, h))
    return pl.pallas_call(
        body,
        out_shape=tuple(jax.ShapeDtypeStruct((s, w), F32) for _ in range(3)),
        grid=(n_h,),
        in_specs=[head, head, pl.BlockSpec((s, dh), lambda h: (0, v_off + h)), head, head],
        out_specs=(head, head, head),
        name=name, compiler_params=_cparams(("parallel",)),
    )(qn, kn, v, tot, do)


ROW_TILE = 512
WIDE_ROW_TILE = 128


def _rows(width, col=0, tm=ROW_TILE):
    return pl.BlockSpec((tm, width), lambda i: (i, col))


_wide_rows = functools.partial(_rows, tm=WIDE_ROW_TILE)


def _whole(shape):
    return pl.BlockSpec(shape, lambda i: (0,) * len(shape))


def _ew_call(body, out_shape, in_specs, out_specs, args, n_rows, name, carried=False):
    return pl.pallas_call(
        body, out_shape=out_shape, grid=(n_rows // in_specs[0].block_shape[0],), in_specs=in_specs, out_specs=out_specs,
        name=name, compiler_params=_cparams(("arbitrary",) if carried else ("parallel",)),
    )(*args)


def _first_step(*refs):
    @pl.when(pl.program_id(0) == 0)
    def _():
        for r in refs:
            r[...] = jnp.zeros_like(r)


def rmsnorm_fwd(x, w, after=None, *, name):
    s, d = x.shape

    def body(x_ref, w_ref, *rest):
        o_ref = rest[-1]
        xv = x_ref[...]
        r = lax.rsqrt(jnp.mean(xv * xv, axis=-1, keepdims=True) + NORM_EPS)
        o_ref[...] = (xv * r * w_ref[...]).astype(BF16)

    extra = [] if after is None else [after]
    return _ew_call(body, jax.ShapeDtypeStruct((s, d), BF16),
                    [_rows(d), _whole((1, d))] + [_whole(TOKEN_SHAPE)] * len(extra), _rows(d),
                    (x, w.reshape(1, d), *extra), s, name)


def rmsnorm_bwd(x, w, dy, dres, *, name):
    s, d = x.shape

    def body(x_ref, w_ref, dy_ref, dr_ref, dx_ref, dw_ref):
        _first_step(dw_ref)
        xv = x_ref[...]
        r = lax.rsqrt(jnp.mean(xv * xv, axis=-1, keepdims=True) + NORM_EPS)
        xhat = xv * r
        dyv = dy_ref[...].astype(F32)
        dw_ref[...] += jnp.sum(dyv * xhat, axis=0, keepdims=True)
        g = dyv * w_ref[...]
        dx_ref[...] = dr_ref[...] + r * (g - xhat * jnp.mean(g * xhat, axis=-1, keepdims=True))

    return _ew_call(body, (jax.ShapeDtypeStruct((s, d), F32), jax.ShapeDtypeStruct((1, d), F32)),
                    [_rows(d), _whole((1, d)), _rows(d), _rows(d)], (_rows(d), _whole((1, d))),
                    (x, w.reshape(1, d), dy, dres), s, name, carried=True)


def ple_fwd(h1, gate_pre, pp, *, name):
    s, d = h1.shape

    def body(h_ref, g_ref, p_ref, o_ref):
        o_ref[...] = h_ref[...] + p_ref[...] * _sigmoid(g_ref[...])

    return _ew_call(body, jax.ShapeDtypeStruct((s, d), F32), [_rows(d)] * 3, _rows(d), (h1, gate_pre, pp), s, name)


def ple_bwd(dh2, gate_pre, pp, after, *, name):
    s, d = dh2.shape

    def body(dh_ref, g_ref, p_ref, after_ref, dp_ref, dg_ref):
        gate = _sigmoid(g_ref[...])
        dh = dh_ref[...]
        dp_ref[...] = (dh * gate).astype(BF16)
        dg_ref[...] = (dh * p_ref[...] * gate * (1.0 - gate)).astype(BF16)

    shp = jax.ShapeDtypeStruct((s, d), BF16)
    return _ew_call(body, (shp, shp), [_rows(d)] * 3 + [_whole(TOKEN_SHAPE)], (_rows(d), _rows(d)),
                    (dh2, gate_pre, pp, after), s, name)


def loss_head(y, target, *, name):
    s, d = y.shape

    def body(y_ref, t_ref, l_ref, dy_ref):
        _first_step(l_ref)
        err = y_ref[...] - t_ref[...]
        per_tok = jnp.mean(err * err, axis=-1, keepdims=True)
        l_ref[...] += 0.5 * jnp.sum(per_tok, axis=0, keepdims=True)
        dy_ref[...] = err * (1.0 / d)

    return _ew_call(body, (jax.ShapeDtypeStruct((1, 1), F32), jax.ShapeDtypeStruct((s, d), F32)),
                    [_rows(d), _rows(d)], (_whole((1, 1)), _rows(d)), (y, target), s, name, carried=True)


CONV_COL_TILE = 256


def _conv_taps(x, w_ref):
    row = lax.broadcasted_iota(jnp.int32, (x.shape[0], 1), 0)
    acc = x * w_ref[SSD_D_CONV - 1:SSD_D_CONV, :]
    shifted = []
    for d in range(1, SSD_D_CONV):
        xs = jnp.where(row >= d, pltpu.roll(x, d, 0), 0.0)
        shifted.append(xs)
        acc = acc + xs * w_ref[SSD_D_CONV - 1 - d:SSD_D_CONV - d, :]
    return acc, shifted


def ssd_conv_fwd(x, w, b, *, name):
    s, c = x.shape
    tc = _pick(c, (CONV_COL_TILE, LANES))

    def body(x_ref, w_ref, b_ref, o_ref):
        pre, _ = _conv_taps(x_ref[...], w_ref)
        o_ref[...] = _silu(pre + b_ref[...])

    col = pl.BlockSpec((s, tc), lambda j: (0, j))
    return pl.pallas_call(
        body, out_shape=jax.ShapeDtypeStruct((s, c), F32), grid=(c // tc,),
        in_specs=[col, pl.BlockSpec((SSD_D_CONV, tc), lambda j: (0, j)), pl.BlockSpec((1, tc), lambda j: (0, j))],
        out_specs=col, name=name, compiler_params=_cparams(("parallel",)),
    )(x, w, b)


def ssd_conv_bwd(x, w, b, dact, *, name):
    s, c = x.shape
    tc = _pick(c, (CONV_COL_TILE, LANES))

    def body(x_ref, w_ref, b_ref, da_ref, dx_ref, dw_ref, db_ref):
        xv = x_ref[...]
        pre, shifted = _conv_taps(xv, w_ref)
        dpre = da_ref[...] * _silu_grad(pre + b_ref[...])
        db_ref[...] = jnp.sum(dpre, axis=0, keepdims=True)
        row = lax.broadcasted_iota(jnp.int32, (s, 1), 0)
        dx = dpre * w_ref[SSD_D_CONV - 1:SSD_D_CONV, :]
        dw_ref[SSD_D_CONV - 1:SSD_D_CONV, :] = jnp.sum(dpre * xv, axis=0, keepdims=True)
        for d in range(1, SSD_D_CONV):
            k = SSD_D_CONV - 1 - d
            dw_ref[k:k + 1, :] = jnp.sum(dpre * shifted[d - 1], axis=0, keepdims=True)
            up = jnp.where(row < s - d, pltpu.roll(dpre, s - d, 0), 0.0)
            dx = dx + up * w_ref[k:k + 1, :]
        dx_ref[...] = dx.astype(BF16)

    col = pl.BlockSpec((s, tc), lambda j: (0, j))
    wspec = pl.BlockSpec((SSD_D_CONV, tc), lambda j: (0, j))
    bspec = pl.BlockSpec((1, tc), lambda j: (0, j))
    return pl.pallas_call(
        body,
        out_shape=(jax.ShapeDtypeStruct((s, c), BF16), jax.ShapeDtypeStruct((SSD_D_CONV, c), F32),
                   jax.ShapeDtypeStruct((1, c), F32)),
        grid=(c // tc,), in_specs=[col, wspec, bspec, col], out_specs=(col, wspec, bspec),
        name=name, compiler_params=_cparams(("parallel",)),
    )(x, w, b, dact)


def ssd_dt_fwd(dt_raw, bias, a_log, *, name):
    s, h = dt_raw.shape

    def body(r_ref, b_ref, al_ref, dt_ref, a_ref):
        zv = r_ref[...] + b_ref[...]
        dt_ref[...] = jnp.maximum(zv, 0.0) + jnp.log(1.0 + jnp.exp(-jnp.abs(zv)))
        a_ref[...] = -jnp.exp(al_ref[...])

    full = pl.BlockSpec((s, h), lambda: (0, 0))
    vec = pl.BlockSpec((1, h), lambda: (0, 0))
    return pl.pallas_call(
        body, out_shape=(jax.ShapeDtypeStruct((s, h), F32), jax.ShapeDtypeStruct((1, h), F32)),
        in_specs=[full, vec, vec], out_specs=(full, vec), name=name, compiler_params=_cparams(),
    )(dt_raw, bias.reshape(1, h), a_log.reshape(1, h))


def ssd_dt_bwd(dt_raw, bias, a_log, dt, ddt, dadt, *, name):
    s, h = dt_raw.shape

    def body(r_ref, b_ref, al_ref, dt_ref, ddt_ref, dadt_ref, dr_ref, db_ref, dal_ref):
        a = -jnp.exp(al_ref[...])
        dadt_v = dadt_ref[...]
        d_dt = ddt_ref[...] + a * dadt_v
        d_raw = d_dt * _sigmoid(r_ref[...] + b_ref[...])
        dr_ref[...] = d_raw
        db_ref[...] = jnp.sum(d_raw, axis=0, keepdims=True)
        dal_ref[...] = jnp.sum(dadt_v * dt_ref[...], axis=0, keepdims=True) * a

    full = pl.BlockSpec((s, h), lambda: (0, 0))
    vec = pl.BlockSpec((1, h), lambda: (0, 0))
    return pl.pallas_call(
        body, out_shape=(jax.ShapeDtypeStruct((s, h), F32), jax.ShapeDtypeStruct((1, h), F32),
                         jax.ShapeDtypeStruct((1, h), F32)),
        in_specs=[full, vec, vec, full, full, full], out_specs=(full, vec, vec), name=name,
        compiler_params=_cparams(),
    )(dt_raw, bias.reshape(1, h), a_log.reshape(1, h), dt, ddt, dadt)


def _group_mean(v, n_groups):
    gw = v.shape[-1] // n_groups
    parts = [jnp.broadcast_to(jnp.mean(v[:, k * gw:(k + 1) * gw], axis=-1, keepdims=True), (v.shape[0], gw))
             for k in range(n_groups)]
    return jnp.concatenate(parts, axis=-1)


def ssd_gate_fwd(y, z, gw, *, name):
    s, di = y.shape

    def body(y_ref, z_ref, w_ref, o_ref):
        yg = y_ref[...] * _silu(z_ref[...])
        r = lax.rsqrt(_group_mean(yg * yg, SSD_N_GROUPS) + GATED_NORM_EPS)
        o_ref[...] = (yg * r * w_ref[...]).astype(BF16)

    return _ew_call(body, jax.ShapeDtypeStruct((s, di), BF16), [_wide_rows(di), _wide_rows(di), _whole((1, di))],
                    _wide_rows(di), (y, z, gw.reshape(1, di)), s, name)


def ssd_gate_bwd(y, z, gw, dyn, *, name):
    s, di = y.shape

    def body(y_ref, z_ref, w_ref, dn_ref, dy_ref, dz_ref, dw_ref):
        _first_step(dw_ref)
        yv, zv = y_ref[...], z_ref[...]
        sz = _silu(zv)
        yg = yv * sz
        r = lax.rsqrt(_group_mean(yg * yg, SSD_N_GROUPS) + GATED_NORM_EPS)
        yhat = yg * r
        dn = dn_ref[...]
        dw_ref[...] += jnp.sum(dn * yhat, axis=0, keepdims=True)
        g = dn * w_ref[...]
        dyg = r * (g - yhat * _group_mean(g * yhat, SSD_N_GROUPS))
        dy_ref[...] = dyg * sz
        dz_ref[...] = (dyg * yv * _silu_grad(zv)).astype(BF16)

    return _ew_call(body, (jax.ShapeDtypeStruct((s, di), F32), jax.ShapeDtypeStruct((s, di), BF16),
                           jax.ShapeDtypeStruct((1, di), F32)),
                    [_wide_rows(di), _wide_rows(di), _whole((1, di)), _wide_rows(di)],
                    (_wide_rows(di), _wide_rows(di), _whole((1, di))),
                    (y, z, gw.reshape(1, di), dyn), s, name, carried=True)


def _head_mean(v):
    return _group_mean(v, v.shape[-1] // SB_HEAD_DIM)


def sb_qk_fwd(proj, qw, kw, *, name):
    s, w4 = proj.shape
    w = w4 // 4
    reps = w // SB_HEAD_DIM

    def body(q_ref, k_ref, qw_ref, kw_ref, qn_ref, kn_ref):
        for x_ref, w_ref, o_ref in ((q_ref, qw_ref, qn_ref), (k_ref, kw_ref, kn_ref)):
            xv = x_ref[...]
            r = lax.rsqrt(_head_mean(xv * xv) + NORM_EPS)
            o_ref[...] = (xv * r * jnp.tile(w_ref[...], (1, reps))).astype(BF16)

    shp = jax.ShapeDtypeStruct((s, w), BF16)
    return _ew_call(body, (shp, shp), [_rows(w, 0), _rows(w, 1), _whole((1, SB_HEAD_DIM)), _whole((1, SB_HEAD_DIM))],
                    (_rows(w), _rows(w)), (proj, proj, qw.reshape(1, -1), kw.reshape(1, -1)), s, name)


def sb_gate_fwd(o, proj, *, name):
    s, w = o.shape

    def body(o_ref, g_ref, og_ref):
        og_ref[...] = (o_ref[...] * _silu(g_ref[...])).astype(BF16)

    return _ew_call(body, jax.ShapeDtypeStruct((s, w), BF16), [_rows(w), _rows(w, 3)], _rows(w), (o, proj), s, name)


def sb_gate_bwd(dog, o, proj, *, name):
    s, w = o.shape

    def body(d_ref, o_ref, g_ref, do_ref, dg_ref):
        gv, dv = g_ref[...], d_ref[...]
        do_ref[...] = dv * _silu(gv)
        dg_ref[...] = (dv * o_ref[...] * _silu_grad(gv)).astype(BF16)

    return _ew_call(body, (jax.ShapeDtypeStruct((s, w), F32), jax.ShapeDtypeStruct((s, w), BF16)),
                    [_rows(w), _rows(w), _rows(w, 3)], (_rows(w), _rows(w)), (dog, o, proj), s, name)


def sb_pack_bwd(proj, qw, kw, dqn, dkn, dv, dg, *, name):
    s, w4 = proj.shape
    w = w4 // 4
    reps = w // SB_HEAD_DIM

    def body(q_ref, k_ref, qw_ref, kw_ref, dqn_ref, dkn_ref, dv_ref, dg_ref, dp_ref, dqw_ref, dkw_ref):
        _first_step(dqw_ref, dkw_ref)
        for idx, (x_ref, w_ref, d_ref, dw_ref) in enumerate(((q_ref, qw_ref, dqn_ref, dqw_ref),
                                                           (k_ref, kw_ref, dkn_ref, dkw_ref))):
            xv = x_ref[...]
            r = lax.rsqrt(_head_mean(xv * xv) + NORM_EPS)
            xhat = xv * r
            dn = d_ref[...]
            per_col = jnp.sum(dn * xhat, axis=0, keepdims=True)
            acc = per_col[:, 0:SB_HEAD_DIM]
            for hh in range(1, reps):
                acc = acc + per_col[:, hh * SB_HEAD_DIM:(hh + 1) * SB_HEAD_DIM]
            dw_ref[...] += acc
            g = dn * jnp.tile(w_ref[...], (1, reps))
            dp_ref[:, idx * w:(idx + 1) * w] = (r * (g - xhat * _head_mean(g * xhat))).astype(BF16)
        dp_ref[:, 2 * w:3 * w] = dv_ref[...].astype(BF16)
        dp_ref[:, 3 * w:4 * w] = dg_ref[...]

    vec = _whole((1, SB_HEAD_DIM))
    return _ew_call(body, (jax.ShapeDtypeStruct((s, w4), BF16), jax.ShapeDtypeStruct((1, SB_HEAD_DIM), F32),
                           jax.ShapeDtypeStruct((1, SB_HEAD_DIM), F32)),
                    [_wide_rows(w, 0), _wide_rows(w, 1), vec, vec, _wide_rows(w), _wide_rows(w), _wide_rows(w),
                     _wide_rows(w)],
                    (_wide_rows(w4), vec, vec),
                    (proj, proj, qw.reshape(1, -1), kw.reshape(1, -1), dqn, dkn, dv, dg), s, name, carried=True)


_HBM = pl.BlockSpec(memory_space=pltpu.HBM)


def _mesh_pos():
    return lax.axis_index("x"), lax.axis_index("y"), lax.axis_index("c")


def _other_chips(x, y):
    return [(1 - x, y), (x, 1 - y), (1 - x, 1 - y)]


_SEM = pl.BlockSpec(memory_space=pltpu.SEMAPHORE)
_ANY = pl.BlockSpec(memory_space=pl.ANY)
_DATAFLOW = pltpu.SideEffectType.DATAFLOW_SIDE_EFFECTING
N_PEER_CHIPS = N_CHIP - 1
TOKEN_SHAPE = (8, LANES)


def _in_hbm(t):
    return pltpu.with_memory_space_constraint(t, pltpu.HBM)


def _ici_copies(kind, src_refs, land_refs, send_sems, recv_sems, arrivals=False):
    x, y, c = _mesh_pos()
    out = []
    for a in range(len(land_refs)):
        if kind in ("pass", "swap"):
            if kind == "pass":
                src, dst = land_refs[a].at[:, c], land_refs[a].at[:, 1 - c if arrivals else c]
            else:
                src, dst = src_refs[a].at[:, 1 - c], land_refs[a]
            out.append(pltpu.make_async_remote_copy(
                src_ref=src, dst_ref=dst, send_sem=send_sems.at[a], recv_sem=recv_sems.at[a],
                device_id=(x, y, 1 - c), device_id_type=MESH))
            continue
        for j, chip in enumerate(_other_chips(x, y)):
            if kind == "gather":
                src = land_refs[a].at[4 * x + 2 * y + c]
                dst = land_refs[a].at[4 * chip[0] + 2 * chip[1] + c] if arrivals else src
            else:
                src, dst = src_refs[a].at[2 * chip[0] + chip[1]], land_refs[a].at[j]
            k = a * N_PEER_CHIPS + j
            out.append(pltpu.make_async_remote_copy(
                src_ref=src, dst_ref=dst, send_sem=send_sems.at[k], recv_sem=recv_sems.at[k],
                device_id=(*chip, c), device_id_type=MESH))
    return out


def _n_copies(kind, lands):
    return len(lands) * (1 if kind in ("pass", "swap") else N_PEER_CHIPS)


def ici_start(kind, srcs, lands, after=(), *, name):
    ns, nb = len(srcs), len(srcs) + len(lands)
    n_sem = _n_copies(kind, lands)

    def body(*refs):
        first_out = nb + len(after)
        for cp in _ici_copies(kind, refs[:ns], refs[ns:nb], refs[first_out], refs[first_out + 1]):
            cp.start()
        refs[-1][...] = jnp.zeros(TOKEN_SHAPE, F32)

    outs = pl.pallas_call(
        body, name=name,
        out_shape=(pltpu.SemaphoreType.DMA((n_sem,)), pltpu.SemaphoreType.DMA((n_sem,)),
                   *[pltpu.HBM(t.shape, t.dtype) for t in (*srcs, *lands)], jax.ShapeDtypeStruct(TOKEN_SHAPE, F32)),
        in_specs=[_HBM] * nb + [_ANY] * len(after),
        out_specs=(_SEM, _SEM, *([_HBM] * nb), pl.BlockSpec(memory_space=pltpu.VMEM)),
        input_output_aliases={k: 2 + k for k in range(nb)},
        compiler_params=pltpu.CompilerParams(has_side_effects=_DATAFLOW),
    )(*[_in_hbm(t) for t in (*srcs, *lands)], *after)
    return outs[0], outs[1], list(outs[2:2 + ns]), list(outs[2 + ns:2 + nb]), outs[-1]


def ici_wait(kind, started, after, *, name):
    send_sems, recv_sems, srcs, lands, _ = started
    ns, nb = len(srcs), len(srcs) + len(lands)

    def body(*refs):
        for cp in _ici_copies(kind, refs[:ns], refs[ns:nb], refs[nb], refs[nb + 1]):
            cp.wait_send()
        for cp in _ici_copies(kind, refs[:ns], refs[ns:nb], refs[nb], refs[nb + 1], arrivals=True):
            cp.wait_recv()

    outs = pl.pallas_call(
        body, name=name,
        out_shape=tuple(pltpu.HBM(t.shape, t.dtype) for t in (*srcs, *lands)),
        in_specs=[_HBM] * nb + [_SEM, _SEM] + [_ANY] * len(after),
        out_specs=tuple([_HBM] * nb),
        input_output_aliases={k: k for k in range(nb)},
        compiler_params=pltpu.CompilerParams(has_side_effects=_DATAFLOW),
    )(*srcs, *lands, send_sems, recv_sems, *after)
    return list(outs[:ns]), list(outs[ns:])


def all_reduce_small(v, *, name):
    r = v.shape[0]

    def body(v_ref, o_ref, buf, send_sems, recv_sems):
        x, y, c = _mesh_pos()
        me = 4 * x + 2 * y + c
        buf[me] = v_ref[...]
        copies = []
        for k in range(1, N_DEV):
            to = ((x + (k >> 2)) % 2, (y + ((k >> 1) & 1)) % 2, (c + (k & 1)) % 2)
            copies.append(pltpu.make_async_remote_copy(
                src_ref=v_ref, dst_ref=buf.at[me], send_sem=send_sems.at[k - 1], recv_sem=recv_sems.at[k - 1],
                device_id=to, device_id_type=MESH))
        for cp in copies:
            cp.start()
        for cp in copies:
            cp.wait()
        acc = buf[0]
        for d in range(1, N_DEV):
            acc = acc + buf[d]
        o_ref[...] = acc

    vm = pl.BlockSpec(memory_space=pltpu.VMEM)
    return pl.pallas_call(
        body, out_shape=jax.ShapeDtypeStruct(v.shape, F32), in_specs=[vm], out_specs=vm,
        scratch_shapes=[pltpu.VMEM((N_DEV, r, LANES), F32), pltpu.SemaphoreType.DMA((N_DEV - 1,)),
                        pltpu.SemaphoreType.DMA((N_DEV - 1,))],
        name=name,
    )(v)


def pair_add(g, r1, core, *, name):
    _, _, rows, cols = g.shape
    tm = _pick(rows, (1024, 512, 256, 128))

    def body(c_ref, g_ref, r_ref, o_ref):
        o_ref[...] = (g_ref[...].astype(F32) + r_ref[...].astype(F32)).astype(o_ref.dtype)

    return pl.pallas_call(
        body, out_shape=jax.ShapeDtypeStruct(r1.shape, g.dtype),
        grid_spec=pltpu.PrefetchScalarGridSpec(
            num_scalar_prefetch=1, grid=(N_CHIP, rows // tm),
            in_specs=[pl.BlockSpec((None, None, tm, cols), lambda k, i, c_ref: (k, c_ref[0], i, 0)),
                      pl.BlockSpec((None, tm, cols), lambda k, i, c_ref: (k, i, 0))],
            out_specs=pl.BlockSpec((None, tm, cols), lambda k, i, c_ref: (k, i, 0))),
        name=name, compiler_params=_cparams(("parallel", "parallel")),
    )(core, g, r1)


def _adamw_math(w, g, m, v):
    m = ADAM_B1 * m + (1.0 - ADAM_B1) * g
    v = ADAM_B2 * v + (1.0 - ADAM_B2) * (g * g)
    m_hat = m / (1.0 - ADAM_B1 ** ADAM_STEP)
    v_hat = v / (1.0 - ADAM_B2 ** ADAM_STEP)
    delta = -ADAM_LR * (m_hat / (jnp.sqrt(v_hat) + ADAM_EPS) + ADAM_WD * w)
    return delta, m, v


def adamw_sharded(w, m, v, layer, chip_sums, received, chip, into, *, name):
    _, rows, cols = w.shape
    tm = _pick(rows, (256, 128))

    def body(k_ref, w_ref, m_ref, v_ref, t_ref, r_ref, *rest):
        g_ref, d_ref, nm_ref, nv_ref, token_ref = rest[-5:]
        g = t_ref[...].astype(F32)
        for j in range(N_CHIP - 1):
            g = g + r_ref[j].astype(F32)
        d, mm, vv = _adamw_math(w_ref[...], g, m_ref[...], v_ref[...])
        g_ref[...] = g
        d_ref[...] = d
        nm_ref[...] = mm
        nv_ref[...] = vv
        token_ref[...] = jnp.zeros(TOKEN_SHAPE, F32)

    blk = pl.BlockSpec((None, tm, cols), lambda i, k_ref: (layer, i, 0))
    shp = jax.ShapeDtypeStruct(w.shape, F32)
    in_specs = [blk, blk, blk,
                pl.BlockSpec((None, tm, cols), lambda i, k_ref: (k_ref[0], i, 0)),
                pl.BlockSpec((N_CHIP - 1, tm, cols), lambda i, k_ref: (0, i, 0))]
    operands = [chip, w, m, v, chip_sums, received]
    aliases = {}
    if into is not None:
        aliases = {len(operands) + q: q for q in range(4)}
        in_specs += [_ANY] * 4
        operands += list(into)
    outs = pl.pallas_call(
        body, out_shape=(shp, shp, shp, shp, jax.ShapeDtypeStruct(TOKEN_SHAPE, F32)),
        grid_spec=pltpu.PrefetchScalarGridSpec(
            num_scalar_prefetch=1, grid=(rows // tm,), in_specs=in_specs,
            out_specs=(blk, blk, blk, blk, pl.BlockSpec(TOKEN_SHAPE, lambda i, k_ref: (0, 0)))),
        input_output_aliases=aliases,
        name=name, compiler_params=_cparams(("arbitrary",)),
    )(*operands)
    return outs[:4], outs[4]


def adamw_replicated(w, m, v, g, *, name):
    def body(w_ref, m_ref, v_ref, g_ref, d_ref, nm_ref, nv_ref):
        d, mm, vv = _adamw_math(w_ref[...], g_ref[...], m_ref[...], v_ref[...])
        d_ref[...] = d
        nm_ref[...] = mm
        nv_ref[...] = vv

    shp = jax.ShapeDtypeStruct(w.shape, F32)
    return pl.pallas_call(body, out_shape=(shp, shp, shp), name=name, compiler_params=_cparams())(w, m, v, g)


WEIGHT_NAMES = ("norm_w", "ssd_in_w", "ssd_conv_w", "ssd_conv_b", "ssd_dt_bias", "ssd_a_log", "ssd_d",
                "ssd_gnorm_w", "ssd_out_w", "sb_in_w", "sb_qn_w", "sb_kn_w", "sb_out_w", "ple_norm_w",
                "ple_gate_w", "ple_proj_w")
REPLICATED = ("norm_w", "ssd_conv_b", "ssd_dt_bias", "ssd_a_log", "ssd_d", "ssd_gnorm_w", "sb_qn_w", "sb_kn_w",
              "ple_norm_w")
PACK_ROWS = 8


def _pack(parts):
    flat = jnp.concatenate([t.reshape(-1) for t in parts])
    pad = (-flat.shape[0]) % (PACK_ROWS * LANES)
    return jnp.pad(flat, (0, pad)).reshape(-1, LANES)


def _unpack(packed, like):
    flat = packed.reshape(-1)
    out, off = [], 0
    for t in like:
        out.append(flat[off:off + t.size].reshape(t.shape))
        off += t.size
    return out


def _to_group_lanes(v, r):
    t = v.reshape(v.shape[0], SSD_N_GROUPS, r).transpose(1, 0, 2)
    return jnp.pad(t, ((0, 0), (0, 0), (0, LANES - r)))


def _from_group_lanes(t, r):
    return t[:, :, :r].transpose(1, 0, 2).reshape(t.shape[1], SSD_N_GROUPS * r)


def _head_vec(v, r):
    return jnp.pad(v.reshape(SSD_N_GROUPS, 1, r), ((0, 0), (0, 0), (0, LANES - r)))


def _col_blocks(full):
    rows = full.shape[0]
    return full.reshape(rows, N_DEV, -1).transpose(1, 0, 2)


def _from_col_blocks(blocks):
    return blocks.transpose(1, 0, 2).reshape(blocks.shape[1], -1)


def _split_cols(full, widths):
    out, off = [], 0
    for w in widths:
        out.append(full[:, off:off + w])
        off += w
    return out


def kernel(x, p, norm_w, ssd_in_w, ssd_conv_w, ssd_conv_b, ssd_dt_bias, ssd_a_log, ssd_d, ssd_gnorm_w, ssd_out_w, sb_in_w, sb_qn_w, sb_kn_w, sb_out_w, ple_norm_w, ple_gate_w, ple_proj_w, loss_target, m_norm_w, m_ssd_in_w, m_ssd_conv_w, m_ssd_conv_b, m_ssd_dt_bias, m_ssd_a_log, m_ssd_d, m_ssd_gnorm_w, m_ssd_out_w, m_sb_in_w, m_sb_qn_w, m_sb_kn_w, m_sb_out_w, m_ple_norm_w, m_ple_gate_w, m_ple_proj_w, v_norm_w, v_ssd_in_w, v_ssd_conv_w, v_ssd_conv_b, v_ssd_dt_bias, v_ssd_a_log, v_ssd_d, v_ssd_gnorm_w, v_ssd_out_w, v_sb_in_w, v_sb_qn_w, v_sb_kn_w, v_sb_out_w, v_ple_norm_w, v_ple_gate_w, v_ple_proj_w):
    env = dict(locals())
    wts = {n: env[n] for n in WEIGHT_NAMES}
    mom1 = {n: env["m_" + n] for n in WEIGHT_NAMES}
    mom2 = {n: env["v_" + n] for n in WEIGHT_NAMES}

    s, d = x.shape[1], x.shape[2]
    depth = norm_w.shape[0]
    di = ssd_out_w.shape[1] * N_DEV
    n_heads = ssd_dt_bias.shape[1]
    hpg = n_heads // SSD_N_GROUPS
    nbc = SSD_N_GROUPS * SSD_D_STATE
    in_segs = (di, di, nbc, nbc, n_heads)
    conv_segs = (di, nbc, nbc)
    sb_w = sb_out_w.shape[1] * N_DEV
    selectors = ssd_selectors(hpg)
    xi, yi, ci = _mesh_pos()
    core = ci.astype(jnp.int32).reshape(1)
    chip = (2 * xi + yi).astype(jnp.int32).reshape(1)

    def layer_keys(i):
        j = i // 2
        mixer = [("ssd_in_w", j), ("ssd_conv_w", j), ("ssd_out_w", j)] if i % 2 == 0 else [("sb_in_w", j), ("sb_out_w", j)]
        return mixer + [("ple_gate_w", i), ("ple_proj_w", i)]

    me_block = 4 * xi + 2 * yi + ci

    def landing_zone(t):
        return lax.dynamic_update_index_in_dim(lax.empty((N_DEV,) + t.shape, t.dtype), t, me_block, 0)

    def groups(i):
        keys = layer_keys(i)
        return [keys[:2], keys[2:]] if i == 0 else [keys]

    gathers, prev = {}, []
    for i in range(depth):
        for q, keys in enumerate(groups(i)):
            shards = [wts[n][idx] for n, idx in keys]
            if prev:
                shards = lax.optimization_barrier((prev[0], shards))[1]
            lands = [landing_zone(t if n == "ssd_conv_w" else t.astype(BF16)) for (n, _), t in zip(keys, shards)]
            gathers[i, q] = ici_start("gather", [], lands, after=prev, name=f"ag{i}{'ab'[q]}_start")
            prev = [gathers[i, q][4]]
    all_started = prev[0]
    full, ssd_full, passing = {}, {}, {}

    def hand_over(i, q, after):
        _, lands = ici_wait("gather", gathers[i, q], after, name=f"ag{i}{'ab'[q]}_wait")
        passing[i, q] = ici_start("pass", [], [t.reshape(N_CHIP, 2, *t.shape[1:]) for t in lands],
                                  name=f"ag{i}{'ab'[q]}_pass_start")

    def arrive(i, q, after):
        _, lands = ici_wait("pass", passing[i, q], after, name=f"ag{i}{'ab'[q]}_pass_wait")
        for k, t in zip(groups(i)[q], lands):
            full[k] = t.reshape(N_DEV, *t.shape[2:])

    def w_out_of(i):
        return full["ssd_out_w", i // 2].reshape(di, d) if i % 2 == 0 else full["sb_out_w", i // 2].reshape(sb_w, d)

    h = x.reshape(s, d)
    saved = []
    hand_over(0, 0, [all_started])
    arrive(0, 0, [all_started])
    for i in range(depth):
        j = i // 2
        if i > 0:
            arrive(i, 0, [h])
        sv = dict(h_in=h)
        u = rmsnorm_fwd(h, norm_w[i], name=f"l{i}_norm")
        sv["u"] = u
        if i % 2 == 0:
            fw = ssd_full[j] = dict(
                w_in=_split_cols(_from_col_blocks(full["ssd_in_w", j]), in_segs),
                conv_w=_split_cols(_from_col_blocks(full["ssd_conv_w", j]), conv_segs),
                conv_b=_split_cols(ssd_conv_b[j].reshape(1, -1), conv_segs))
            raw = [matmul(u, wseg, name=f"l{i}_in{q}") for q, wseg in enumerate(fw["w_in"])]
            if i == 0:
                hand_over(0, 1, [raw[4]])
            z, dt_raw = raw[0], raw[4]
            act = [ssd_conv_fwd(raw[1 + q], fw["conv_w"][q], fw["conv_b"][q], name=f"l{i}_conv{q}") for q in range(3)]
            dt, a_neg = ssd_dt_fwd(dt_raw, ssd_dt_bias[j], ssd_a_log[j], name=f"l{i}_dt")
            dtp = _to_group_lanes(dt, hpg)
            a_g = _head_vec(a_neg.reshape(-1), hpg)
            d_x = jnp.repeat(ssd_d[j].reshape(SSD_N_GROUPS, 1, hpg), SSD_HEAD_DIM, axis=2)
            y, states = ssd_scan_fwd(act[0], act[1], act[2], dtp, a_g, d_x, selectors, heads_per_group=hpg,
                                     name=f"l{i}_scan")
            yn = ssd_gate_fwd(y, z, ssd_gnorm_w[j], name=f"l{i}_gate")
            if i == 0:
                arrive(0, 1, [yn])
            h1 = matmul(yn, w_out_of(i), res=h, name=f"l{i}_out")
            sv.update(raw=raw, act=act, dt=dt, dtp=dtp, a_g=a_g, d_x=d_x, y=y, states=states, yn=yn)
        else:
            proj = matmul(u, full["sb_in_w", j], name=f"l{i}_in")
            qn, kn = sb_qk_fwd(proj, sb_qn_w[j], sb_kn_w[j], name=f"l{i}_qknorm")
            v_off = 2 * sb_w // SB_HEAD_DIM
            o, tot = sb_attn_fwd(qn, kn, proj, v_off=v_off, name=f"l{i}_attn")
            og = sb_gate_fwd(o, proj, name=f"l{i}_gate")
            h1 = matmul(og, w_out_of(i), res=h, name=f"l{i}_out")
            sv.update(proj=proj, qn=qn, kn=kn, o=o, tot=tot, og=og, v_off=v_off)
        if i + 1 < depth:
            hand_over(i + 1, 0, [h1])
        t = rmsnorm_fwd(h1, ple_norm_w[i], passing[i + 1, 0][4] if i + 1 < depth else None, name=f"l{i}_plenorm")
        gate_pre = matmul(t, full["ple_gate_w", i].reshape(d, d), name=f"l{i}_plegate")
        pp = matmul(p[i, 0], full["ple_proj_w", i], name=f"l{i}_pleproj")
        h = ple_fwd(h1, gate_pre, pp, name=f"l{i}_ple")
        sv.update(h1=h1, t=t, gate_pre=gate_pre, pp=pp)
        saved.append(sv)

    loss_part, dh = loss_head(h, loss_target.reshape(s, d), name="loss_head")
    loss = lax.psum(loss_part[0, 0], ("x", "y", "c"))

    big = {}
    small = {n: [None] * wts[n].shape[0] for n in REPLICATED}
    swaps, scatters = {}, {}
    order_after = jnp.zeros(TOKEN_SHAPE, F32)
    pending = None

    def send_to_sibling(i, q):
        blocks = [big[k].reshape(N_CHIP, 2, *big[k].shape[1:]) for k in groups(i)[::-1][q]]
        swaps[i, q] = ici_start("swap", blocks, [lax.empty((N_CHIP,) + t.shape[2:], t.dtype) for t in blocks],
                                name=f"rs{i}{'ab'[q]}_swap_start")
        return swaps[i, q][4]

    def send_to_chips(i, q, after):
        blocks, from_sibling = ici_wait("swap", swaps[i, q], after, name=f"rs{i}{'ab'[q]}_swap_wait")
        sums = [pair_add(g, r1, core, name=f"rs{i}{'ab'[q]}_pair_add{a}")
                for a, (g, r1) in enumerate(zip(blocks, from_sibling))]
        scatters[i, q] = ici_start("scatter", sums, [lax.empty((N_PEER_CHIPS,) + t.shape[1:], t.dtype) for t in sums],
                                   name=f"rs{i}{'ab'[q]}_start")
        return scatters[i, q][4]

    for i in reversed(range(depth)):
        j = i // 2
        sv = saved[i]
        dpp, dgp = ple_bwd(dh, sv["gate_pre"], sv["pp"], order_after, name=f"b{i}_ple")
        big["ple_proj_w", i] = matmul(p[i, 0], dpp, mode="tn", out_dtype=BF16, out_blocks=ple_proj_w.shape[2],
                                      name=f"b{i}_pleproj_w")
        big["ple_gate_w", i] = matmul(sv["t"], dgp, mode="tn", out_dtype=BF16, name=f"b{i}_plegate_w").reshape(N_DEV, -1, d)
        dt_ = matmul(dgp, full["ple_gate_w", i].reshape(d, d), mode="nt", name=f"b{i}_plegate_x")
        dh1, g_pn = rmsnorm_bwd(sv["h1"], ple_norm_w[i], dt_, dh, name=f"b{i}_plenorm")
        small["ple_norm_w"][i] = g_pn
        behind = send_to_chips(*pending, [dh1]) if pending is not None else None
        pending = None
        u = sv["u"]
        if i % 2 == 0:
            fw = ssd_full[j]
            raw, act = sv["raw"], sv["act"]
            big["ssd_out_w", j] = matmul(sv["yn"], dh1, mode="tn", out_dtype=BF16, name=f"b{i}_out_w").reshape(N_DEV, -1, d)
            if i == 0:
                send_to_sibling(0, 0)
            dyn = matmul(dh1, w_out_of(i), mode="nt", after=behind, name=f"b{i}_out_x")
            dy, dz, g_gn = ssd_gate_bwd(sv["y"], raw[0], ssd_gnorm_w[j], dyn, name=f"b{i}_gate")
            dxs, dbm, dcm, ddtp, dadtp, dd_g = ssd_scan_bwd(act[0], act[1], act[2], sv["dtp"], sv["a_g"], sv["d_x"], selectors,
                                                          sv["states"], dy, heads_per_group=hpg, name=f"b{i}_scan")
            behind = send_to_chips(0, 0, [dxs]) if i == 0 else None
            ddt_raw, g_dtb, g_alog = ssd_dt_bwd(raw[4], ssd_dt_bias[j], ssd_a_log[j], sv["dt"],
                                                _from_group_lanes(ddtp, hpg), _from_group_lanes(dadtp, hpg),
                                                name=f"b{i}_dt")
            conv_back = [ssd_conv_bwd(raw[1 + q], fw["conv_w"][q], fw["conv_b"][q], dact, name=f"b{i}_conv{q}")
                         for q, dact in enumerate((dxs, dbm, dcm))]
            dsegs = [dz] + [cb[0] for cb in conv_back] + [ddt_raw]
            g_in = jnp.concatenate([matmul(u, ds, mode="tn", out_dtype=BF16, name=f"b{i}_in{q}_w")
                                    for q, ds in enumerate(dsegs)], axis=1)
            big["ssd_in_w", j] = _col_blocks(g_in)
            big["ssd_conv_w", j] = _col_blocks(jnp.concatenate([cb[1] for cb in conv_back], axis=1))
            du = None
            for q, (ds, wseg) in enumerate(zip(dsegs, fw["w_in"])):
                du = matmul(ds, wseg, mode="nt", res=du, after=behind if q == 0 else None, name=f"b{i}_in{q}_x")
            small["ssd_conv_b"][j] = jnp.concatenate([cb[2] for cb in conv_back], axis=1)
            small["ssd_dt_bias"][j] = g_dtb
            small["ssd_a_log"][j] = g_alog
            small["ssd_d"][j] = dd_g[:, 0, :hpg]
            small["ssd_gnorm_w"][j] = g_gn
        else:
            proj = sv["proj"]
            big["sb_out_w", j] = matmul(sv["og"], dh1, mode="tn", out_dtype=BF16, name=f"b{i}_out_w").reshape(N_DEV, -1, d)
            dog = matmul(dh1, w_out_of(i), mode="nt", after=behind, name=f"b{i}_out_x")
            do, dg = sb_gate_bwd(dog, sv["o"], proj, name=f"b{i}_gate")
            dqn, dkn, dv = sb_attn_bwd(sv["qn"], sv["kn"], proj, sv["tot"], do, v_off=sv["v_off"], name=f"b{i}_attn")
            dproj, g_qn, g_kn = sb_pack_bwd(proj, sb_qn_w[j], sb_kn_w[j], dqn, dkn, dv, dg, name=f"b{i}_qknorm")
            big["sb_in_w", j] = matmul(u, dproj, mode="tn", out_dtype=BF16, out_blocks=sb_in_w.shape[2], name=f"b{i}_in_w")
            du = matmul(dproj, full["sb_in_w", j], mode="nt", name=f"b{i}_in_x")
            small["sb_qn_w"][j] = g_qn
            small["sb_kn_w"][j] = g_kn
        dh, g_n = rmsnorm_bwd(sv["h_in"], norm_w[i], du, dh1, name=f"b{i}_norm")
        small["norm_w"][i] = g_n
        pending = (i, len(groups(i)) - 1)
        order_after = send_to_sibling(*pending)
    send_to_chips(*pending, [dh])
    grad_x = dh.reshape(x.shape)

    rep_like = [wts[n] for n in REPLICATED]
    g_packed = all_reduce_small(_pack([jnp.stack([t.reshape(-1) for t in small[n]]) for n in REPLICATED]),
                                name="all_reduce_small_grads")
    d_packed, m_packed, v_packed = adamw_replicated(
        _pack(rep_like), _pack([mom1[n] for n in REPLICATED]), _pack([mom2[n] for n in REPLICATED]), g_packed,
        name="adamw_replicated")
    grads = dict(zip(REPLICATED, _unpack(g_packed, rep_like)))
    deltas = dict(zip(REPLICATED, _unpack(d_packed, rep_like)))
    new_m = dict(zip(REPLICATED, _unpack(m_packed, rep_like)))
    new_v = dict(zip(REPLICATED, _unpack(v_packed, rep_like)))

    updated = {}
    after = [scatters[0, len(groups(0)) - 1][4]]
    for i in reversed(range(depth)):
        for q, keys in enumerate(groups(i)[::-1]):
            sums, received = ici_wait("scatter", scatters[i, q], after, name=f"rs{i}{'ab'[q]}_wait")
            after = []
            for (n, idx), t_sum, recv in zip(keys, sums, received):
                updated[n], done = adamw_sharded(wts[n], mom1[n], mom2[n], idx, t_sum, recv, chip, updated.get(n),
                                                 name=f"adamw_{n}{idx}")
                after.append(done)
    for n, (g_n, d_n, m_n, v_n) in updated.items():
        grads[n], deltas[n], new_m[n], new_v[n] = g_n, d_n, m_n, v_n

    return (loss, grad_x, *[grads[n] for n in WEIGHT_NAMES], *[deltas[n] for n in WEIGHT_NAMES],
            *[new_m[n] for n in WEIGHT_NAMES], *[new_v[n] for n in WEIGHT_NAMES])
```

```python
import functools
import math

import jax
import jax.numpy as jnp
from jax import lax
from jax.experimental import pallas as pl
from jax.experimental.pallas import tpu as pltpu

F32 = jnp.float32
BF16 = jnp.bfloat16
MESH = pl.DeviceIdType.MESH

N_DEV = 8
N_CHIP = 4
LANES = 128
VMEM_LIMIT_BYTES = 56 * 1024 * 1024
MATMUL_TILE_BYTES = 36 * 1024 * 1024

NORM_EPS = 1e-6
GATED_NORM_EPS = 1e-5
SSD_HEAD_DIM = 64
SSD_N_GROUPS = 8
SSD_D_STATE = 128
SSD_D_CONV = 4
SSD_CHUNK = 128
SSD_FWD_CHUNKS_PER_STEP = 8
SSD_BWD_CHUNKS_PER_STEP = 8
SB_HEAD_DIM = 128

ADAM_LR = 0.001
ADAM_B1 = 0.9
ADAM_B2 = 0.999
ADAM_EPS = 1e-08
ADAM_WD = 0.01
ADAM_STEP = 10


def _cparams(sem=None, **kw):
    return pltpu.CompilerParams(dimension_semantics=sem, vmem_limit_bytes=VMEM_LIMIT_BYTES, **kw)


def _pick(dim, prefs):
    for t in prefs:
        if dim % t == 0:
            return t
    return dim


def _sigmoid(x):
    return 1.0 / (1.0 + jnp.exp(-x))


def _silu(x):
    return x * _sigmoid(x)


def _silu_grad(x):
    s = _sigmoid(x)
    return s * (1.0 + x * (1.0 - s))


def matmul(a, b, *, mode="nn", out_dtype=F32, res=None, out_blocks=None, after=None, name):
    b_blocked = b.ndim == 3
    if mode == "nn":
        m, kc = a.shape
        n = b.shape[-1] * (N_DEV if b_blocked else 1)
    elif mode == "nt":
        m, kc = a.shape
        n = b.shape[-2]
    else:
        kc, m = a.shape
        n = b.shape[-1]
    nb = b.shape[-1] if b_blocked else None
    tn = _pick(n if not out_blocks else out_blocks, (512, 256, 128))
    if b_blocked and mode == "nn":
        tn = _pick(nb, (512, 256, 128))
    k_unit = nb if (b_blocked and mode == "nt") else 1
    tm, tk = None, None
    for tm_try in (1024, 512, 256, 128):
        if m % tm_try:
            continue
        for tk_try in (kc, kc // 2, kc // 4, 2048, 1024, 512, 256, 128):
            if tk_try > kc or tk_try < k_unit or kc % tk_try or tk_try % k_unit:
                continue
            tiles = 2 * (tm_try * tk_try * a.dtype.itemsize + tk_try * tn * b.dtype.itemsize)
            tiles += tm_try * tn * (2 * jnp.dtype(out_dtype).itemsize + 4 + (8 if res is not None else 0))
            if tiles <= MATMUL_TILE_BYTES:
                tm, tk = tm_try, tk_try
                break
        if tm:
            break
    if tm is None:
        tm, tk = m, max(k_unit, LANES if kc % LANES == 0 else kc)
    nk = kc // tk
    grid = (m // tm, n // tn, nk)

    if mode == "tn":
        a_spec = pl.BlockSpec((tk, tm), lambda i, j, k: (k, i))
        dims = (((0,), (0,)), ((), ()))
    else:
        a_spec = pl.BlockSpec((tm, tk), lambda i, j, k: (i, k))
        dims = (((1,), (0,)), ((), ())) if mode == "nn" else (((1,), (1,)), ((), ()))
    if mode == "nt":
        if b_blocked:
            b_spec = pl.BlockSpec((tk // nb, tn, nb), lambda i, j, k: (k, j, 0))
        else:
            b_spec = pl.BlockSpec((tn, tk), lambda i, j, k: (j, k))
    else:
        if b_blocked:
            per = nb // tn
            b_spec = pl.BlockSpec((None, tk, tn), lambda i, j, k: (j // per, k, j % per))
        else:
            b_spec = pl.BlockSpec((tk, tn), lambda i, j, k: (k, j))
    if out_blocks:
        per_o = out_blocks // tn
        out_shape = jax.ShapeDtypeStruct((n // out_blocks, m, out_blocks), out_dtype)
        out_spec = pl.BlockSpec((None, tm, tn), lambda i, j, k: (j // per_o, i, j % per_o))
    else:
        out_shape = jax.ShapeDtypeStruct((m, n), out_dtype)
        out_spec = pl.BlockSpec((tm, tn), lambda i, j, k: (i, j))
    in_specs = [a_spec, b_spec]
    args = [a, b]
    if res is not None:
        in_specs.append(pl.BlockSpec((tm, tn), lambda i, j, k: (i, j)))
        args.append(res)
    if after is not None:
        in_specs.append(pl.BlockSpec(memory_space=pl.ANY))
        args.append(after)
    n_in = len(args)

    def body(*refs):
        a_ref, b_ref = refs[:2]
        r_ref = refs[2] if res is not None else None
        o_ref = refs[n_in]

        def finish(r):
            if res is not None:
                r = r + r_ref[...].astype(F32)
            o_ref[...] = r.astype(out_dtype)

        if b_blocked and mode == "nt":
            part = None
            for blk in range(tk // nb):
                term = lax.dot_general(a_ref[:, blk * nb:(blk + 1) * nb].astype(BF16), b_ref[blk].astype(BF16), dims,
                                       preferred_element_type=F32)
                part = term if part is None else part + term
        else:
            part = lax.dot_general(a_ref[...].astype(BF16), b_ref[...].astype(BF16), dims, preferred_element_type=F32)
        if nk == 1:
            finish(part)
            return
        acc_ref = refs[-1]
        k = pl.program_id(2)

        @pl.when(k == 0)
        def _():
            acc_ref[...] = part

        @pl.when(k > 0)
        def _():
            acc_ref[...] += part

        @pl.when(k == nk - 1)
        def _():
            finish(acc_ref[...])

    return pl.pallas_call(
        body, out_shape=out_shape, grid=grid, in_specs=in_specs, out_specs=out_spec,
        scratch_shapes=[] if nk == 1 else [pltpu.VMEM((tm, tn), F32)], name=name,
        compiler_params=_cparams(("parallel", "parallel", "arbitrary")),
    )(*args)


def _dot(a, b, dims, precision=None):
    return lax.dot_general(a, b, (dims, ((), ())), preferred_element_type=F32, precision=precision)


_NN = ((1,), (0,))
_NT = ((1,), (1,))
_TN = ((0,), (0,))
_EXACT = lax.Precision.HIGHEST


def _chunk_decay_terms(dt, a):
    ln = dt.shape[0]
    row = lax.broadcasted_iota(jnp.int32, (ln, ln), 0)
    col = lax.broadcasted_iota(jnp.int32, (ln, ln), 1)
    tri = (row >= col).astype(F32)
    a_col = _dot(tri, dt * a, _NN, _EXACT)
    return a_col, a_col.T, row >= col


def _exact_dot(x, sel, terms):
    t = x.shape[0]
    parts, rest = [], x
    for k in range(terms):
        piece = rest.astype(BF16)
        parts.append(piece)
        if k + 1 < terms:
            rest = rest - piece.astype(F32)
    r = _dot(jnp.concatenate(parts, axis=0), sel, _NN)
    out = r[:t]
    for k in range(1, terms):
        out = out + r[k * t:(k + 1) * t]
    return out


def ssd_selectors(r_n):
    lane = jnp.arange(LANES)
    spread64 = (lane[:, None] == jnp.arange(r_n * SSD_HEAD_DIM)[None, :] // SSD_HEAD_DIM).astype(BF16)
    pair_sum = jnp.stack([lane[None, :] == 2 * q + lane[:, None] // SSD_HEAD_DIM for q in range(r_n // 2)]).astype(BF16)
    row_sum = jnp.stack([jnp.broadcast_to(lane[None, :] == r, (LANES, LANES)) for r in range(r_n)]).astype(BF16)
    return spread64, pair_sum, row_sum


def _ssd_chunk_setup(dt, a, spread64):
    ln = dt.shape[0]
    a_col, a_row, causal = _chunk_decay_terms(dt, a)
    ea = jnp.exp(a_col)
    te = jnp.exp(a_col[ln - 1:ln, :] - a_col)
    return (a_row, a_col, _exact_dot(dt, spread64, 2), _exact_dot(ea, spread64, 2), _exact_dot(te, spread64, 2),
            ea, causal)


def ssd_scan_fwd(xs, bm, cm, dtp, a_g, d_x, selectors, *, heads_per_group, name):
    s, di = xs.shape
    g_n = SSD_N_GROUPS
    r_n, p_n, n_n, ln = heads_per_group, SSD_HEAD_DIM, SSD_D_STATE, SSD_CHUNK
    nc = s // ln
    cps = _pick(nc, (SSD_FWD_CHUNKS_PER_STEP, 4, 2, 1))
    pairs, pw = r_n // 2, 2 * p_n
    spread64 = selectors[0]

    def body(xs_ref, bm_ref, cm_ref, dt_ref, a_ref, d_ref, s64_ref, y_ref, st_ref, state):
        c = pl.program_id(1)

        @pl.when(c == 0)
        def _():
            state[...] = jnp.zeros_like(state)

        first_head = lax.broadcasted_iota(jnp.int32, (1, pw), 1) < p_n
        for sub in range(cps):
            rows = slice(sub * ln, (sub + 1) * ln)
            a_row, a_col, dt_x, ea_x, te_x, _, causal = _ssd_chunk_setup(dt_ref[rows, :], a_ref[...], s64_ref[...])
            bm_f = bm_ref[rows, :]
            bmb = bm_f.astype(BF16)
            bm_t = bm_f.T.astype(BF16)
            cmb = cm_ref[rows, :].astype(BF16)
            scores = _dot(cmb, bmb, _NT)
            for q in range(pairs):
                sl = slice(q * pw, (q + 1) * pw)
                x2 = xs_ref[rows, sl]
                xdt2 = x2 * dt_x[:, sl]
                xdt2b = xdt2.astype(BF16)
                y_heads = []
                for r in (2 * q, 2 * q + 1):
                    decay = jnp.exp(jnp.where(causal, a_col[:, r:r + 1] - a_row[r:r + 1, :], -jnp.inf))
                    y_heads.append(_dot((scores * decay).astype(BF16), xdt2b, _NN))
                s2t = state[q]
                st_ref[sub, q] = s2t
                y2 = jnp.where(first_head, y_heads[0], y_heads[1])
                y2 = y2 + ea_x[:, sl] * _dot(cmb, s2t.astype(BF16), _NN)
                y_ref[rows, sl] = y2 + d_ref[:, sl] * x2
                state[q] = s2t * ea_x[ln - 1:ln, sl] + _dot(bm_t, (xdt2 * te_x[:, sl]).astype(BF16), _NN)

    whole = lambda t: pl.BlockSpec(t.shape, lambda g, c: (0,) * t.ndim)
    step = cps * ln
    return pl.pallas_call(
        body,
        out_shape=(jax.ShapeDtypeStruct((s, di), F32),
                   jax.ShapeDtypeStruct((nc, g_n * pairs, n_n, pw), F32)),
        grid=(g_n, nc // cps),
        in_specs=[pl.BlockSpec((step, r_n * p_n), lambda g, c: (c, g)),
                  pl.BlockSpec((step, n_n), lambda g, c: (c, g)),
                  pl.BlockSpec((step, n_n), lambda g, c: (c, g)),
                  pl.BlockSpec((None, step, LANES), lambda g, c: (g, c, 0)),
                  pl.BlockSpec((None, 1, LANES), lambda g, c: (g, 0, 0)),
                  pl.BlockSpec((None, 1, r_n * p_n), lambda g, c: (g, 0, 0)),
                  whole(spread64)],
        out_specs=(pl.BlockSpec((step, r_n * p_n), lambda g, c: (c, g)),
                   pl.BlockSpec((cps, pairs, n_n, pw), lambda g, c: (c, g, 0, 0))),
        scratch_shapes=[pltpu.VMEM((pairs, n_n, pw), F32)],
        name=name, compiler_params=_cparams(("parallel", "arbitrary")),
    )(xs, bm, cm, dtp, a_g, d_x, spread64)


def _row8(v):
    return jnp.broadcast_to(v, (8, v.shape[1]))


def ssd_scan_bwd(xs, bm, cm, dtp, a_g, d_x, selectors, states, dy, *, heads_per_group, name):
    s, di = xs.shape
    g_n = SSD_N_GROUPS
    r_n, p_n, n_n, ln = heads_per_group, SSD_HEAD_DIM, SSD_D_STATE, SSD_CHUNK
    nc = s // ln
    cps = _pick(nc, (SSD_BWD_CHUNKS_PER_STEP, 2, 1))
    pairs, pw = r_n // 2, 2 * p_n
    spread64, pair_sum, row_sum = selectors

    def body(xs_ref, bm_ref, cm_ref, dt_ref, a_ref, d_ref, s64_ref, ps_ref, rs_ref, st_ref, dy_ref,
             dxs_ref, dbm_ref, dcm_ref, ddt_ref, dadt_ref, dd_ref, dstate, da_rows):
        c = pl.program_id(1)

        @pl.when(c == 0)
        def _():
            dstate[...] = jnp.zeros_like(dstate)
            dd_ref[...] = jnp.zeros_like(dd_ref)

        row = lax.broadcasted_iota(jnp.int32, (ln, ln), 0)
        col = lax.broadcasted_iota(jnp.int32, (ln, ln), 1)
        causal_t = col >= row
        upper = causal_t.astype(F32)
        first_head = lax.broadcasted_iota(jnp.int32, (1, pw), 1) < p_n
        for sub in reversed(range(cps)):
            rows = slice(sub * ln, (sub + 1) * ln)
            a_row, a_col, dt_x, ea_x, te_x, ea, causal = _ssd_chunk_setup(dt_ref[rows, :], a_ref[...], s64_ref[...])
            bmb = bm_ref[rows, :].astype(BF16)
            cm_f = cm_ref[rows, :]
            cmb = cm_f.astype(BF16)
            cm_t = cm_f.T.astype(BF16)
            scores = _dot(cmb, bmb, _NT)
            scores_t = _dot(bmb, cmb, _NT)
            e_last = ea[ln - 1:ln, :]
            da_rows[...] = jnp.zeros_like(da_rows)
            dscores = jnp.zeros((ln, ln), F32)
            dcm = jnp.zeros((ln, n_n), F32)
            dbm = jnp.zeros((ln, n_n), F32)
            da_cols = jnp.zeros((ln, LANES), F32)
            da_last = jnp.zeros((1, LANES), F32)
            ddt = jnp.zeros((ln, LANES), F32)
            dd = jnp.zeros((1, LANES), F32)
            for q in range(pairs):
                sl = slice(q * pw, (q + 1) * pw)
                sum2 = ps_ref[q]
                x2 = xs_ref[rows, sl]
                dt2 = dt_x[:, sl]
                xdt2 = x2 * dt2
                xdt2b = xdt2.astype(BF16)
                dy2 = dy_ref[rows, sl]
                dy2b = dy2.astype(BF16)
                dxdt_heads = []
                for h, r in enumerate((2 * q, 2 * q + 1)):
                    a_r = jnp.broadcast_to(a_col[:, r:r + 1], (ln, ln))
                    decay = jnp.exp(jnp.where(causal, a_r - a_row[r:r + 1, :], -jnp.inf))
                    decay_t = jnp.exp(jnp.where(causal_t, a_row[r:r + 1, :] - a_r, -jnp.inf))
                    dy_h = jnp.where(first_head if h == 0 else jnp.logical_not(first_head), dy2, 0.0).astype(BF16)
                    dm = _dot(dy_h, xdt2b, _NT)
                    dscores = dscores + dm * decay
                    e_mat = dm * (scores * decay)
                    da_cols = da_cols + _exact_dot(e_mat, rs_ref[r], 2)
                    da_rows[r:r + 1, :] = -jnp.sum(e_mat, axis=0, keepdims=True)
                    dxdt_heads.append(_dot((scores_t * decay_t).astype(BF16), dy2b, _NN))
                dxdt2 = jnp.where(first_head, dxdt_heads[0], dxdt_heads[1])
                s2t = st_ref[sub, q]
                s2tb = s2t.astype(BF16)
                ds2t = dstate[q]
                ds2tb = ds2t.astype(BF16)
                ea2, te2 = ea_x[:, sl], te_x[:, sl]
                y_off2 = ea2 * _dot(cmb, s2tb, _NN)
                dy_e2 = (dy2 * ea2).astype(BF16)
                dcm = dcm + _dot(dy_e2, s2tb, _NT)
                ds_in = _dot(cm_t, dy_e2, _NN)
                da_cols = da_cols + _exact_dot(dy2 * y_off2, sum2, 2)
                bds2 = _dot(bmb, ds2tb, _NN)
                dxdt2 = dxdt2 + te2 * bds2
                xdt_e2 = xdt2 * te2
                dbm = dbm + _dot(xdt_e2.astype(BF16), ds2tb, _NT)
                w_cols = _exact_dot(xdt_e2 * bds2, sum2, 2)
                da_cols = da_cols - w_cols
                state_dot = _exact_dot(_row8(jnp.sum(ds2t * s2t, axis=0, keepdims=True)), sum2, 2)[0:1]
                da_last = da_last + jnp.sum(w_cols, axis=0, keepdims=True) + e_last * state_dot
                dstate[q] = ds2t * ea_x[ln - 1:ln, sl] + ds_in
                dxs_ref[rows, sl] = dxdt2 * dt2 + d_ref[:, sl] * dy2
                ddt = ddt + _exact_dot(dxdt2 * x2, sum2, 2)
                dd = dd + _exact_dot(_row8(jnp.sum(dy2 * x2, axis=0, keepdims=True)), sum2, 2)[0:1]
            dcm_ref[rows, :] = dcm + _dot(dscores.astype(BF16), bmb, _NN)
            dbm_ref[rows, :] = dbm + _dot(dscores.T.astype(BF16), cmb, _NN)
            da_total = da_cols + da_rows[...].T
            dadt_ref[rows, :] = _dot(upper, da_total, _NN, _EXACT) + da_last
            ddt_ref[rows, :] = ddt
            dd_ref[...] += dd

    step, last_c = cps * ln, nc // cps - 1
    whole = lambda t: pl.BlockSpec(t.shape, lambda g, c: (0,) * t.ndim)
    return pl.pallas_call(
        body,
        out_shape=(jax.ShapeDtypeStruct((s, di), F32),
                   jax.ShapeDtypeStruct(bm.shape, F32),
                   jax.ShapeDtypeStruct(cm.shape, F32),
                   jax.ShapeDtypeStruct(dtp.shape, F32),
                   jax.ShapeDtypeStruct(dtp.shape, F32),
                   jax.ShapeDtypeStruct(a_g.shape, F32)),
        grid=(g_n, nc // cps),
        in_specs=[pl.BlockSpec((step, r_n * p_n), lambda g, c: (last_c - c, g)),
                  pl.BlockSpec((step, n_n), lambda g, c: (last_c - c, g)),
                  pl.BlockSpec((step, n_n), lambda g, c: (last_c - c, g)),
                  pl.BlockSpec((None, step, LANES), lambda g, c: (g, last_c - c, 0)),
                  pl.BlockSpec((None, 1, LANES), lambda g, c: (g, 0, 0)),
                  pl.BlockSpec((None, 1, r_n * p_n), lambda g, c: (g, 0, 0)),
                  whole(spread64), whole(pair_sum), whole(row_sum),
                  pl.BlockSpec((cps, pairs, n_n, pw), lambda g, c: (last_c - c, g, 0, 0)),
                  pl.BlockSpec((step, r_n * p_n), lambda g, c: (last_c - c, g))],
        out_specs=(pl.BlockSpec((step, r_n * p_n), lambda g, c: (last_c - c, g)),
                   pl.BlockSpec((step, n_n), lambda g, c: (last_c - c, g)),
                   pl.BlockSpec((step, n_n), lambda g, c: (last_c - c, g)),
                   pl.BlockSpec((None, step, LANES), lambda g, c: (g, last_c - c, 0)),
                   pl.BlockSpec((None, step, LANES), lambda g, c: (g, last_c - c, 0)),
                   pl.BlockSpec((None, 1, LANES), lambda g, c: (g, 0, 0))),
        scratch_shapes=[pltpu.VMEM((pairs, n_n, pw), F32), pltpu.VMEM((LANES, ln), F32)],
        name=name, compiler_params=_cparams(("parallel", "arbitrary")),
    )(xs, bm, cm, dtp, a_g, d_x, spread64, pair_sum, row_sum, states, dy)


SB_Q_TILE = 2048
SB_K_TILE = 256


def _tri_sum(x, tri):
    t = x.shape[0]
    hi = x.astype(BF16)
    r1 = x - hi.astype(F32)
    mid = r1.astype(BF16)
    lo = (r1 - mid.astype(F32)).astype(BF16)
    r = _dot(jnp.concatenate([hi, mid, lo], axis=0), tri, _NN)
    return r[:t] + r[t:2 * t] + r[2 * t:]


def _sb_logits(q, k_j, scale, strict):
    z = _dot(q, k_j, _NT) * scale
    sp = jnp.log(1.0 + jnp.exp(-jnp.abs(z)))
    log_b = jnp.minimum(z, 0.0) - sp
    log_1mb = log_b - z
    if strict is not None:
        log_1mb = jnp.where(strict, log_1mb, 0.0)
    return log_b, log_1mb


def _sb_tiles(s):
    tq = _pick(s, (SB_Q_TILE, 2 * SB_K_TILE, SB_K_TILE, LANES))
    return tq, min(tq, SB_K_TILE)


def _sb_diag_mask(rows, tk):
    return lax.broadcasted_iota(jnp.int32, (rows, tk), 1) < lax.broadcasted_iota(jnp.int32, (rows, tk), 0)


def _sb_iotas(t):
    row = lax.broadcasted_iota(jnp.int32, (t, t), 0)
    col = lax.broadcasted_iota(jnp.int32, (t, t), 1)
    return row, col


def sb_attn_fwd(qn, kn, v, gate, *, v_off=0, g_off=0, name):
    s, w = qn.shape
    dh = SB_HEAD_DIM
    n_h = w // dh
    tq, tk = _sb_tiles(s)
    per = tq // tk
    scale = 1.0 / math.sqrt(dh)

    def body(q_ref, k_ref, v_ref, g_ref, o_ref, tot_ref, og_ref):
        i = pl.program_id(1)
        q = q_ref[...]
        row, col = _sb_iotas(tk)
        later = (row > col).astype(BF16)

        def tile(q_rows, j, acc, run, mask):
            s0 = pl.multiple_of(j * tk, tk)
            k_j = k_ref[pl.ds(s0, tk), :]
            v_j = v_ref[pl.ds(s0, tk), :].astype(BF16)
            log_b, log_1mb = _sb_logits(q_rows, k_j, scale, mask)
            att = jnp.exp(log_b + (_tri_sum(log_1mb, later) + run))
            if mask is not None:
                att = jnp.where(mask, att, 0.0)
            return acc + _dot(att.astype(BF16), v_j, _NN), run + jnp.sum(log_1mb, axis=1, keepdims=True)

        acc, run = jnp.zeros((tq, dh), F32), jnp.zeros((tq, 1), F32)
        for d in reversed(range(per)):
            r0 = d * tk
            a2, r2 = tile(q[r0:], i * per + d, acc[r0:], run[r0:], _sb_diag_mask(tq - r0, tk))
            acc = a2 if r0 == 0 else jnp.concatenate([acc[:r0], a2], axis=0)
            run = r2 if r0 == 0 else jnp.concatenate([run[:r0], r2], axis=0)

        def group(gg, c):
            for d in reversed(range(per)):
                c = tile(q, (i - 1 - gg) * per + d, c[0], c[1], None)
            return c

        acc, run = lax.fori_loop(0, i, group, (acc, run))
        o_ref[...] = acc
        tot_ref[...] = jnp.broadcast_to(run, (tq, dh))
        og_ref[...] = (acc * _silu(g_ref[...])).astype(BF16)

    rows = pl.BlockSpec((tq, dh), lambda h, i: (i, h))
    return pl.pallas_call(
        body,
        out_shape=(jax.ShapeDtypeStruct((s, w), F32), jax.ShapeDtypeStruct((s, w), F32),
                   jax.ShapeDtypeStruct((s, w), BF16)),
        grid=(n_h, s // tq),
        in_specs=[rows,
                  pl.BlockSpec((s, dh), lambda h, i: (0, h)),
                  pl.BlockSpec((s, dh), lambda h, i: (0, v_off + h)),
                  pl.BlockSpec((tq, dh), lambda h, i: (i, g_off + h))],
        out_specs=(rows, rows, rows),
        name=name, compiler_params=_cparams(("parallel", "parallel")),
    )(qn, kn, v, gate)


def sb_attn_bwd(qn, kn, v, tot, do, *, v_off=0, name):
    s, w = qn.shape
    dh = SB_HEAD_DIM
    n_h = w // dh
    tq, tk = _sb_tiles(s)
    per = tq // tk
    scale = 1.0 / math.sqrt(dh)

    def body(q_ref, k_ref, v_ref, tot_ref, do_ref, dq_ref, dk_ref, dv_ref):
        dk_ref[...] = jnp.zeros_like(dk_ref)
        dv_ref[...] = jnp.zeros_like(dv_ref)
        row, col = _sb_iotas(tk)
        upto = (row <= col).astype(BF16)
        before = (row < col).astype(BF16)

        def q_block(i, _):
            t0 = pl.multiple_of(i * tq, tq)
            q = q_ref[pl.ds(t0, tq), :]
            do_i = do_ref[pl.ds(t0, tq), :].astype(BF16)
            total = tot_ref[pl.ds(t0, tq), 0:1]

            def tile(r0, j, dq, run_l, run_g, mask):
                s0 = pl.multiple_of(j * tk, tk)
                k_j = k_ref[pl.ds(s0, tk), :]
                v_j = v_ref[pl.ds(s0, tk), :].astype(BF16)
                q_r, do_r = q[r0:], do_i[r0:]
                log_b, log_1mb = _sb_logits(q_r, k_j, scale, mask)
                att = jnp.exp(log_b + ((total[r0:] - run_l) - _tri_sum(log_1mb, upto)))
                if mask is not None:
                    att = jnp.where(mask, att, 0.0)
                g = att * _dot(do_r, v_j, _NT)
                c = _tri_sum(g, before) + run_g
                dz = (g - (g + c) * jnp.exp(log_b)) * scale
                if mask is not None:
                    dz = jnp.where(mask, dz, 0.0)
                dz = dz.astype(BF16)
                dk_ref[pl.ds(s0, tk), :] += _dot(dz, q_r, _TN)
                dv_ref[pl.ds(s0, tk), :] += _dot(att.astype(BF16), do_r, _TN)
                return (dq + _dot(dz, k_j, _NN), run_l + jnp.sum(log_1mb, axis=1, keepdims=True),
                        run_g + jnp.sum(g, axis=1, keepdims=True))

            def group(gg, c):
                for d in range(per):
                    c = tile(0, gg * per + d, c[0], c[1], c[2], None)
                return c

            zero = jnp.zeros((tq, 1), F32)
            dq, run_l, run_g = lax.fori_loop(0, i, group, (jnp.zeros((tq, dh), F32), zero, zero))
            for d in range(per):
                r0 = d * tk
                p_dq, p_l, p_g = tile(r0, i * per + d, dq[r0:], run_l[r0:], run_g[r0:], _sb_diag_mask(tq - r0, tk))
                if r0 == 0:
                    dq, run_l, run_g = p_dq, p_l, p_g
                else:
                    dq = jnp.concatenate([dq[:r0], p_dq], axis=0)
                    run_l = jnp.concatenate([run_l[:r0], p_l], axis=0)
                    run_g = jnp.concatenate([run_g[:r0], p_g], axis=0)
            dq_ref[pl.ds(t0, tq), :] = dq
            return 0

        lax.fori_loop(0, s // tq, q_block, 0)

    head = pl.BlockSpec((s, dh), lambda h: (0, h))
    return pl.pallas_call(
        body,
        out_shape=tuple(jax.ShapeDtypeStruct((s, w), F32) for _ in range(3)),
        grid=(n_h,),
        in_specs=[head, head, pl.BlockSpec((s, dh), lambda h: (0, v_off + h)), head, head],
        out_specs=(head, head, head),
        name=name, compiler_params=_cparams(("parallel",)),
    )(qn, kn, v, tot, do)


ROW_TILE = 512
WIDE_ROW_TILE = 128


def _rows(width, col=0, tm=ROW_TILE):
    return pl.BlockSpec((tm, width), lambda i: (i, col))


_wide_rows = functools.partial(_rows, tm=WIDE_ROW_TILE)


def _whole(shape):
    return pl.BlockSpec(shape, lambda i: (0,) * len(shape))


def _ew_call(body, out_shape, in_specs, out_specs, args, n_rows, name, carried=False):
    return pl.pallas_call(
        body, out_shape=out_shape, grid=(n_rows // in_specs[0].block_shape[0],), in_specs=in_specs, out_specs=out_specs,
        name=name, compiler_params=_cparams(("arbitrary",) if carried else ("parallel",)),
    )(*args)


def _first_step(*refs):
    @pl.when(pl.program_id(0) == 0)
    def _():
        for r in refs:
            r[...] = jnp.zeros_like(r)


def rmsnorm_fwd(x, w, after=None, *, name):
    s, d = x.shape

    def body(x_ref, w_ref, *rest):
        o_ref = rest[-1]
        xv = x_ref[...]
        r = lax.rsqrt(jnp.mean(xv * xv, axis=-1, keepdims=True) + NORM_EPS)
        o_ref[...] = (xv * r * w_ref[...]).astype(BF16)

    extra = [] if after is None else [after]
    return _ew_call(body, jax.ShapeDtypeStruct((s, d), BF16),
                    [_rows(d), _whole((1, d))] + [_whole(TOKEN_SHAPE)] * len(extra), _rows(d),
                    (x, w.reshape(1, d), *extra), s, name)


def rmsnorm_bwd(x, w, dy, dres, *, name):
    s, d = x.shape

    def body(x_ref, w_ref, dy_ref, dr_ref, dx_ref, dw_ref):
        _first_step(dw_ref)
        xv = x_ref[...]
        r = lax.rsqrt(jnp.mean(xv * xv, axis=-1, keepdims=True) + NORM_EPS)
        xhat = xv * r
        dyv = dy_ref[...].astype(F32)
        dw_ref[...] += jnp.sum(dyv * xhat, axis=0, keepdims=True)
        g = dyv * w_ref[...]
        dx_ref[...] = dr_ref[...] + r * (g - xhat * jnp.mean(g * xhat, axis=-1, keepdims=True))

    return _ew_call(body, (jax.ShapeDtypeStruct((s, d), F32), jax.ShapeDtypeStruct((1, d), F32)),
                    [_rows(d), _whole((1, d)), _rows(d), _rows(d)], (_rows(d), _whole((1, d))),
                    (x, w.reshape(1, d), dy, dres), s, name, carried=True)


def ple_fwd(h1, gate_pre, pp, *, name):
    s, d = h1.shape

    def body(h_ref, g_ref, p_ref, o_ref):
        o_ref[...] = h_ref[...] + p_ref[...] * _sigmoid(g_ref[...])

    return _ew_call(body, jax.ShapeDtypeStruct((s, d), F32), [_rows(d)] * 3, _rows(d), (h1, gate_pre, pp), s, name)


def ple_bwd(dh2, gate_pre, pp, after, *, name):
    s, d = dh2.shape

    def body(dh_ref, g_ref, p_ref, after_ref, dp_ref, dg_ref):
        gate = _sigmoid(g_ref[...])
        dh = dh_ref[...]
        dp_ref[...] = (dh * gate).astype(BF16)
        dg_ref[...] = (dh * p_ref[...] * gate * (1.0 - gate)).astype(BF16)

    shp = jax.ShapeDtypeStruct((s, d), BF16)
    return _ew_call(body, (shp, shp), [_rows(d)] * 3 + [_whole(TOKEN_SHAPE)], (_rows(d), _rows(d)),
                    (dh2, gate_pre, pp, after), s, name)


def loss_head(y, target, *, name):
    s, d = y.shape

    def body(y_ref, t_ref, l_ref, dy_ref):
        _first_step(l_ref)
        err = y_ref[...] - t_ref[...]
        per_tok = jnp.mean(err * err, axis=-1, keepdims=True)
        l_ref[...] += 0.5 * jnp.sum(per_tok, axis=0, keepdims=True)
        dy_ref[...] = err * (1.0 / d)

    return _ew_call(body, (jax.ShapeDtypeStruct((1, 1), F32), jax.ShapeDtypeStruct((s, d), F32)),
                    [_rows(d), _rows(d)], (_whole((1, 1)), _rows(d)), (y, target), s, name, carried=True)


CONV_COL_TILE = 256


def _conv_taps(x, w_ref):
    row = lax.broadcasted_iota(jnp.int32, (x.shape[0], 1), 0)
    acc = x * w_ref[SSD_D_CONV - 1:SSD_D_CONV, :]
    shifted = []
    for d in range(1, SSD_D_CONV):
        xs = jnp.where(row >= d, pltpu.roll(x, d, 0), 0.0)
        shifted.append(xs)
        acc = acc + xs * w_ref[SSD_D_CONV - 1 - d:SSD_D_CONV - d, :]
    return acc, shifted


def ssd_conv_fwd(x, w, b, *, name):
    s, c = x.shape
    tc = _pick(c, (CONV_COL_TILE, LANES))

    def body(x_ref, w_ref, b_ref, o_ref):
        pre, _ = _conv_taps(x_ref[...], w_ref)
        o_ref[...] = _silu(pre + b_ref[...])

    col = pl.BlockSpec((s, tc), lambda j: (0, j))
    return pl.pallas_call(
        body, out_shape=jax.ShapeDtypeStruct((s, c), F32), grid=(c // tc,),
        in_specs=[col, pl.BlockSpec((SSD_D_CONV, tc), lambda j: (0, j)), pl.BlockSpec((1, tc), lambda j: (0, j))],
        out_specs=col, name=name, compiler_params=_cparams(("parallel",)),
    )(x, w, b)


def ssd_conv_bwd(x, w, b, dact, *, name):
    s, c = x.shape
    tc = _pick(c, (CONV_COL_TILE, LANES))

    def body(x_ref, w_ref, b_ref, da_ref, dx_ref, dw_ref, db_ref):
        xv = x_ref[...]
        pre, shifted = _conv_taps(xv, w_ref)
        dpre = da_ref[...] * _silu_grad(pre + b_ref[...])
        db_ref[...] = jnp.sum(dpre, axis=0, keepdims=True)
        row = lax.broadcasted_iota(jnp.int32, (s, 1), 0)
        dx = dpre * w_ref[SSD_D_CONV - 1:SSD_D_CONV, :]
        dw_ref[SSD_D_CONV - 1:SSD_D_CONV, :] = jnp.sum(dpre * xv, axis=0, keepdims=True)
        for d in range(1, SSD_D_CONV):
            k = SSD_D_CONV - 1 - d
            dw_ref[k:k + 1, :] = jnp.sum(dpre * shifted[d - 1], axis=0, keepdims=True)
            up = jnp.where(row < s - d, pltpu.roll(dpre, s - d, 0), 0.0)
            dx = dx + up * w_ref[k:k + 1, :]
        dx_ref[...] = dx.astype(BF16)

    col = pl.BlockSpec((s, tc), lambda j: (0, j))
    wspec = pl.BlockSpec((SSD_D_CONV, tc), lambda j: (0, j))
    bspec = pl.BlockSpec((1, tc), lambda j: (0, j))
    return pl.pallas_call(
        body,
        out_shape=(jax.ShapeDtypeStruct((s, c), BF16), jax.ShapeDtypeStruct((SSD_D_CONV, c), F32),
                   jax.ShapeDtypeStruct((1, c), F32)),
        grid=(c // tc,), in_specs=[col, wspec, bspec, col], out_specs=(col, wspec, bspec),
        name=name, compiler_params=_cparams(("parallel",)),
    )(x, w, b, dact)


def ssd_dt_fwd(dt_raw, bias, a_log, *, name):
    s, h = dt_raw.shape

    def body(r_ref, b_ref, al_ref, dt_ref, a_ref):
        zv = r_ref[...] + b_ref[...]
        dt_ref[...] = jnp.maximum(zv, 0.0) + jnp.log(1.0 + jnp.exp(-jnp.abs(zv)))
        a_ref[...] = -jnp.exp(al_ref[...])

    full = pl.BlockSpec((s, h), lambda: (0, 0))
    vec = pl.BlockSpec((1, h), lambda: (0, 0))
    return pl.pallas_call(
        body, out_shape=(jax.ShapeDtypeStruct((s, h), F32), jax.ShapeDtypeStruct((1, h), F32)),
        in_specs=[full, vec, vec], out_specs=(full, vec), name=name, compiler_params=_cparams(),
    )(dt_raw, bias.reshape(1, h), a_log.reshape(1, h))


def ssd_dt_bwd(dt_raw, bias, a_log, dt, ddt, dadt, *, name):
    s, h = dt_raw.shape

    def body(r_ref, b_ref, al_ref, dt_ref, ddt_ref, dadt_ref, dr_ref, db_ref, dal_ref):
        a = -jnp.exp(al_ref[...])
        dadt_v = dadt_ref[...]
        d_dt = ddt_ref[...] + a * dadt_v
        d_raw = d_dt * _sigmoid(r_ref[...] + b_ref[...])
        dr_ref[...] = d_raw
        db_ref[...] = jnp.sum(d_raw, axis=0, keepdims=True)
        dal_ref[...] = jnp.sum(dadt_v * dt_ref[...], axis=0, keepdims=True) * a

    full = pl.BlockSpec((s, h), lambda: (0, 0))
    vec = pl.BlockSpec((1, h), lambda: (0, 0))
    return pl.pallas_call(
        body, out_shape=(jax.ShapeDtypeStruct((s, h), F32), jax.ShapeDtypeStruct((1, h), F32),
                         jax.ShapeDtypeStruct((1, h), F32)),
        in_specs=[full, vec, vec, full, full, full], out_specs=(full, vec, vec), name=name,
        compiler_params=_cparams(),
    )(dt_raw, bias.reshape(1, h), a_log.reshape(1, h), dt, ddt, dadt)


def _group_mean(v, n_groups):
    gw = v.shape[-1] // n_groups
    parts = [jnp.broadcast_to(jnp.mean(v[:, k * gw:(k + 1) * gw], axis=-1, keepdims=True), (v.shape[0], gw))
             for k in range(n_groups)]
    return jnp.concatenate(parts, axis=-1)


def ssd_gate_fwd(y, z, gw, *, name):
    s, di = y.shape

    def body(y_ref, z_ref, w_ref, o_ref):
        yg = y_ref[...] * _silu(z_ref[...])
        r = lax.rsqrt(_group_mean(yg * yg, SSD_N_GROUPS) + GATED_NORM_EPS)
        o_ref[...] = (yg * r * w_ref[...]).astype(BF16)

    return _ew_call(body, jax.ShapeDtypeStruct((s, di), BF16), [_wide_rows(di), _wide_rows(di), _whole((1, di))],
                    _wide_rows(di), (y, z, gw.reshape(1, di)), s, name)


def ssd_gate_bwd(y, z, gw, dyn, *, name):
    s, di = y.shape

    def body(y_ref, z_ref, w_ref, dn_ref, dy_ref, dz_ref, dw_ref):
        _first_step(dw_ref)
        yv, zv = y_ref[...], z_ref[...]
        sz = _silu(zv)
        yg = yv * sz
        r = lax.rsqrt(_group_mean(yg * yg, SSD_N_GROUPS) + GATED_NORM_EPS)
        yhat = yg * r
        dn = dn_ref[...]
        dw_ref[...] += jnp.sum(dn * yhat, axis=0, keepdims=True)
        g = dn * w_ref[...]
        dyg = r * (g - yhat * _group_mean(g * yhat, SSD_N_GROUPS))
        dy_ref[...] = dyg * sz
        dz_ref[...] = (dyg * yv * _silu_grad(zv)).astype(BF16)

    return _ew_call(body, (jax.ShapeDtypeStruct((s, di), F32), jax.ShapeDtypeStruct((s, di), BF16),
                           jax.ShapeDtypeStruct((1, di), F32)),
                    [_wide_rows(di), _wide_rows(di), _whole((1, di)), _wide_rows(di)],
                    (_wide_rows(di), _wide_rows(di), _whole((1, di))),
                    (y, z, gw.reshape(1, di), dyn), s, name, carried=True)


def _head_mean(v):
    return _group_mean(v, v.shape[-1] // SB_HEAD_DIM)


def sb_qk_fwd(proj, qw, kw, *, name):
    s, w4 = proj.shape
    w = w4 // 4
    reps = w // SB_HEAD_DIM

    def body(q_ref, k_ref, qw_ref, kw_ref, qn_ref, kn_ref):
        for x_ref, w_ref, o_ref in ((q_ref, qw_ref, qn_ref), (k_ref, kw_ref, kn_ref)):
            xv = x_ref[...]
            r = lax.rsqrt(_head_mean(xv * xv) + NORM_EPS)
            o_ref[...] = (xv * r * jnp.tile(w_ref[...], (1, reps))).astype(BF16)

    shp = jax.ShapeDtypeStruct((s, w), BF16)
    return _ew_call(body, (shp, shp), [_rows(w, 0), _rows(w, 1), _whole((1, SB_HEAD_DIM)), _whole((1, SB_HEAD_DIM))],
                    (_rows(w), _rows(w)), (proj, proj, qw.reshape(1, -1), kw.reshape(1, -1)), s, name)


def sb_gate_bwd(dog, o, proj, *, name):
    s, w = o.shape

    def body(d_ref, o_ref, g_ref, do_ref, dg_ref):
        gv, dv = g_ref[...], d_ref[...]
        do_ref[...] = dv * _silu(gv)
        dg_ref[...] = (dv * o_ref[...] * _silu_grad(gv)).astype(BF16)

    return _ew_call(body, (jax.ShapeDtypeStruct((s, w), F32), jax.ShapeDtypeStruct((s, w), BF16)),
                    [_rows(w), _rows(w), _rows(w, 3)], (_rows(w), _rows(w)), (dog, o, proj), s, name)


def sb_pack_bwd(proj, qw, kw, dqn, dkn, dv, dg, *, name):
    s, w4 = proj.shape
    w = w4 // 4
    reps = w // SB_HEAD_DIM

    def body(q_ref, k_ref, qw_ref, kw_ref, dqn_ref, dkn_ref, dv_ref, dg_ref, dp_ref, dqw_ref, dkw_ref):
        _first_step(dqw_ref, dkw_ref)
        for idx, (x_ref, w_ref, d_ref, dw_ref) in enumerate(((q_ref, qw_ref, dqn_ref, dqw_ref),
                                                           (k_ref, kw_ref, dkn_ref, dkw_ref))):
            xv = x_ref[...]
            r = lax.rsqrt(_head_mean(xv * xv) + NORM_EPS)
            xhat = xv * r
            dn = d_ref[...]
            per_col = jnp.sum(dn * xhat, axis=0, keepdims=True)
            acc = per_col[:, 0:SB_HEAD_DIM]
            for hh in range(1, reps):
                acc = acc + per_col[:, hh * SB_HEAD_DIM:(hh + 1) * SB_HEAD_DIM]
            dw_ref[...] += acc
            g = dn * jnp.tile(w_ref[...], (1, reps))
            dp_ref[:, idx * w:(idx + 1) * w] = (r * (g - xhat * _head_mean(g * xhat))).astype(BF16)
        dp_ref[:, 2 * w:3 * w] = dv_ref[...].astype(BF16)
        dp_ref[:, 3 * w:4 * w] = dg_ref[...]

    vec = _whole((1, SB_HEAD_DIM))
    return _ew_call(body, (jax.ShapeDtypeStruct((s, w4), BF16), jax.ShapeDtypeStruct((1, SB_HEAD_DIM), F32),
                           jax.ShapeDtypeStruct((1, SB_HEAD_DIM), F32)),
                    [_wide_rows(w, 0), _wide_rows(w, 1), vec, vec, _wide_rows(w), _wide_rows(w), _wide_rows(w),
                     _wide_rows(w)],
                    (_wide_rows(w4), vec, vec),
                    (proj, proj, qw.reshape(1, -1), kw.reshape(1, -1), dqn, dkn, dv, dg), s, name, carried=True)


_HBM = pl.BlockSpec(memory_space=pltpu.HBM)


def _mesh_pos():
    return lax.axis_index("x"), lax.axis_index("y"), lax.axis_index("c")


def _other_chips(x, y):
    return [(1 - x, y), (x, 1 - y), (1 - x, 1 - y)]


_SEM = pl.BlockSpec(memory_space=pltpu.SEMAPHORE)
_ANY = pl.BlockSpec(memory_space=pl.ANY)
_DATAFLOW = pltpu.SideEffectType.DATAFLOW_SIDE_EFFECTING
N_PEER_CHIPS = N_CHIP - 1
TOKEN_SHAPE = (8, LANES)


def _in_hbm(t):
    return pltpu.with_memory_space_constraint(t, pltpu.HBM)


def _ici_copies(kind, src_refs, land_refs, send_sems, recv_sems, arrivals=False):
    x, y, c = _mesh_pos()
    out = []
    for a in range(len(land_refs)):
        if kind in ("pass", "swap"):
            if kind == "pass":
                src, dst = land_refs[a].at[:, c], land_refs[a].at[:, 1 - c if arrivals else c]
            else:
                src, dst = src_refs[a].at[:, 1 - c], land_refs[a]
            out.append(pltpu.make_async_remote_copy(
                src_ref=src, dst_ref=dst, send_sem=send_sems.at[a], recv_sem=recv_sems.at[a],
                device_id=(x, y, 1 - c), device_id_type=MESH))
            continue
        for j, chip in enumerate(_other_chips(x, y)):
            if kind == "gather":
                src = land_refs[a].at[4 * x + 2 * y + c]
                dst = land_refs[a].at[4 * chip[0] + 2 * chip[1] + c] if arrivals else src
            else:
                src, dst = src_refs[a].at[2 * chip[0] + chip[1]], land_refs[a].at[j]
            k = a * N_PEER_CHIPS + j
            out.append(pltpu.make_async_remote_copy(
                src_ref=src, dst_ref=dst, send_sem=send_sems.at[k], recv_sem=recv_sems.at[k],
                device_id=(*chip, c), device_id_type=MESH))
    return out


def _n_copies(kind, lands):
    return len(lands) * (1 if kind in ("pass", "swap") else N_PEER_CHIPS)


def ici_start(kind, srcs, lands, after=(), *, name):
    ns, nb = len(srcs), len(srcs) + len(lands)
    n_sem = _n_copies(kind, lands)

    def body(*refs):
        first_out = nb + len(after)
        for cp in _ici_copies(kind, refs[:ns], refs[ns:nb], refs[first_out], refs[first_out + 1]):
            cp.start()
        refs[-1][...] = jnp.zeros(TOKEN_SHAPE, F32)

    outs = pl.pallas_call(
        body, name=name,
        out_shape=(pltpu.SemaphoreType.DMA((n_sem,)), pltpu.SemaphoreType.DMA((n_sem,)),
                   *[pltpu.HBM(t.shape, t.dtype) for t in (*srcs, *lands)], jax.ShapeDtypeStruct(TOKEN_SHAPE, F32)),
        in_specs=[_HBM] * nb + [_ANY] * len(after),
        out_specs=(_SEM, _SEM, *([_HBM] * nb), pl.BlockSpec(memory_space=pltpu.VMEM)),
        input_output_aliases={k: 2 + k for k in range(nb)},
        compiler_params=pltpu.CompilerParams(has_side_effects=_DATAFLOW),
    )(*[_in_hbm(t) for t in (*srcs, *lands)], *after)
    return outs[0], outs[1], list(outs[2:2 + ns]), list(outs[2 + ns:2 + nb]), outs[-1]


def ici_wait(kind, started, after, *, name):
    send_sems, recv_sems, srcs, lands, _ = started
    ns, nb = len(srcs), len(srcs) + len(lands)

    def body(*refs):
        for cp in _ici_copies(kind, refs[:ns], refs[ns:nb], refs[nb], refs[nb + 1]):
            cp.wait_send()
        for cp in _ici_copies(kind, refs[:ns], refs[ns:nb], refs[nb], refs[nb + 1], arrivals=True):
            cp.wait_recv()

    outs = pl.pallas_call(
        body, name=name,
        out_shape=tuple(pltpu.HBM(t.shape, t.dtype) for t in (*srcs, *lands)),
        in_specs=[_HBM] * nb + [_SEM, _SEM] + [_ANY] * len(after),
        out_specs=tuple([_HBM] * nb),
        input_output_aliases={k: k for k in range(nb)},
        compiler_params=pltpu.CompilerParams(has_side_effects=_DATAFLOW),
    )(*srcs, *lands, send_sems, recv_sems, *after)
    return list(outs[:ns]), list(outs[ns:])


def all_reduce_small(v, *, name):
    r = v.shape[0]

    def body(v_ref, o_ref, buf, send_sems, recv_sems):
        x, y, c = _mesh_pos()
        me = 4 * x + 2 * y + c
        buf[me] = v_ref[...]
        copies = []
        for k in range(1, N_DEV):
            to = ((x + (k >> 2)) % 2, (y + ((k >> 1) & 1)) % 2, (c + (k & 1)) % 2)
            copies.append(pltpu.make_async_remote_copy(
                src_ref=v_ref, dst_ref=buf.at[me], send_sem=send_sems.at[k - 1], recv_sem=recv_sems.at[k - 1],
                device_id=to, device_id_type=MESH))
        for cp in copies:
            cp.start()
        for cp in copies:
            cp.wait()
        acc = buf[0]
        for d in range(1, N_DEV):
            acc = acc + buf[d]
        o_ref[...] = acc

    vm = pl.BlockSpec(memory_space=pltpu.VMEM)
    return pl.pallas_call(
        body, out_shape=jax.ShapeDtypeStruct(v.shape, F32), in_specs=[vm], out_specs=vm,
        scratch_shapes=[pltpu.VMEM((N_DEV, r, LANES), F32), pltpu.SemaphoreType.DMA((N_DEV - 1,)),
                        pltpu.SemaphoreType.DMA((N_DEV - 1,))],
        name=name,
    )(v)


def pair_add(g, r1, core, *, name):
    _, _, rows, cols = g.shape
    tm = _pick(rows, (1024, 512, 256, 128))

    def body(c_ref, g_ref, r_ref, o_ref):
        o_ref[...] = (g_ref[...].astype(F32) + r_ref[...].astype(F32)).astype(o_ref.dtype)

    return pl.pallas_call(
        body, out_shape=jax.ShapeDtypeStruct(r1.shape, g.dtype),
        grid_spec=pltpu.PrefetchScalarGridSpec(
            num_scalar_prefetch=1, grid=(N_CHIP, rows // tm),
            in_specs=[pl.BlockSpec((None, None, tm, cols), lambda k, i, c_ref: (k, c_ref[0], i, 0)),
                      pl.BlockSpec((None, tm, cols), lambda k, i, c_ref: (k, i, 0))],
            out_specs=pl.BlockSpec((None, tm, cols), lambda k, i, c_ref: (k, i, 0))),
        name=name, compiler_params=_cparams(("parallel", "parallel")),
    )(core, g, r1)


def _adamw_math(w, g, m, v):
    m = ADAM_B1 * m + (1.0 - ADAM_B1) * g
    v = ADAM_B2 * v + (1.0 - ADAM_B2) * (g * g)
    m_hat = m / (1.0 - ADAM_B1 ** ADAM_STEP)
    v_hat = v / (1.0 - ADAM_B2 ** ADAM_STEP)
    delta = -ADAM_LR * (m_hat / (jnp.sqrt(v_hat) + ADAM_EPS) + ADAM_WD * w)
    return delta, m, v


def adamw_sharded(w, m, v, layer, chip_sums, received, chip, into, *, name):
    _, rows, cols = w.shape
    tm = _pick(rows, (256, 128))

    def body(k_ref, w_ref, m_ref, v_ref, t_ref, r_ref, *rest):
        g_ref, d_ref, nm_ref, nv_ref, token_ref = rest[-5:]
        g = t_ref[...].astype(F32)
        for j in range(N_CHIP - 1):
            g = g + r_ref[j].astype(F32)
        d, mm, vv = _adamw_math(w_ref[...], g, m_ref[...], v_ref[...])
        g_ref[...] = g
        d_ref[...] = d
        nm_ref[...] = mm
        nv_ref[...] = vv
        token_ref[...] = jnp.zeros(TOKEN_SHAPE, F32)

    blk = pl.BlockSpec((None, tm, cols), lambda i, k_ref: (layer, i, 0))
    shp = jax.ShapeDtypeStruct(w.shape, F32)
    in_specs = [blk, blk, blk,
                pl.BlockSpec((None, tm, cols), lambda i, k_ref: (k_ref[0], i, 0)),
                pl.BlockSpec((N_CHIP - 1, tm, cols), lambda i, k_ref: (0, i, 0))]
    operands = [chip, w, m, v, chip_sums, received]
    aliases = {}
    if into is not None:
        aliases = {len(operands) + q: q for q in range(4)}
        in_specs += [_ANY] * 4
        operands += list(into)
    outs = pl.pallas_call(
        body, out_shape=(shp, shp, shp, shp, jax.ShapeDtypeStruct(TOKEN_SHAPE, F32)),
        grid_spec=pltpu.PrefetchScalarGridSpec(
            num_scalar_prefetch=1, grid=(rows // tm,), in_specs=in_specs,
            out_specs=(blk, blk, blk, blk, pl.BlockSpec(TOKEN_SHAPE, lambda i, k_ref: (0, 0)))),
        input_output_aliases=aliases,
        name=name, compiler_params=_cparams(("arbitrary",)),
    )(*operands)
    return outs[:4], outs[4]


def adamw_replicated(w, m, v, g, *, name):
    def body(w_ref, m_ref, v_ref, g_ref, d_ref, nm_ref, nv_ref):
        d, mm, vv = _adamw_math(w_ref[...], g_ref[...], m_ref[...], v_ref[...])
        d_ref[...] = d
        nm_ref[...] = mm
        nv_ref[...] = vv

    shp = jax.ShapeDtypeStruct(w.shape, F32)
    return pl.pallas_call(body, out_shape=(shp, shp, shp), name=name, compiler_params=_cparams())(w, m, v, g)


WEIGHT_NAMES = ("norm_w", "ssd_in_w", "ssd_conv_w", "ssd_conv_b", "ssd_dt_bias", "ssd_a_log", "ssd_d",
                "ssd_gnorm_w", "ssd_out_w", "sb_in_w", "sb_qn_w", "sb_kn_w", "sb_out_w", "ple_norm_w",
                "ple_gate_w", "ple_proj_w")
REPLICATED = ("norm_w", "ssd_conv_b", "ssd_dt_bias", "ssd_a_log", "ssd_d", "ssd_gnorm_w", "sb_qn_w", "sb_kn_w",
              "ple_norm_w")
PACK_ROWS = 8


def _pack(parts):
    flat = jnp.concatenate([t.reshape(-1) for t in parts])
    pad = (-flat.shape[0]) % (PACK_ROWS * LANES)
    return jnp.pad(flat, (0, pad)).reshape(-1, LANES)


def _unpack(packed, like):
    flat = packed.reshape(-1)
    out, off = [], 0
    for t in like:
        out.append(flat[off:off + t.size].reshape(t.shape))
        off += t.size
    return out


def _to_group_lanes(v, r):
    t = v.reshape(v.shape[0], SSD_N_GROUPS, r).transpose(1, 0, 2)
    return jnp.pad(t, ((0, 0), (0, 0), (0, LANES - r)))


def _from_group_lanes(t, r):
    return t[:, :, :r].transpose(1, 0, 2).reshape(t.shape[1], SSD_N_GROUPS * r)


def _head_vec(v, r):
    return jnp.pad(v.reshape(SSD_N_GROUPS, 1, r), ((0, 0), (0, 0), (0, LANES - r)))


def _col_blocks(full):
    rows = full.shape[0]
    return full.reshape(rows, N_DEV, -1).transpose(1, 0, 2)


def _from_col_blocks(blocks):
    return blocks.transpose(1, 0, 2).reshape(blocks.shape[1], -1)


def _split_cols(full, widths):
    out, off = [], 0
    for w in widths:
        out.append(full[:, off:off + w])
        off += w
    return out


def kernel(x, p, norm_w, ssd_in_w, ssd_conv_w, ssd_conv_b, ssd_dt_bias, ssd_a_log, ssd_d, ssd_gnorm_w, ssd_out_w, sb_in_w, sb_qn_w, sb_kn_w, sb_out_w, ple_norm_w, ple_gate_w, ple_proj_w, loss_target, m_norm_w, m_ssd_in_w, m_ssd_conv_w, m_ssd_conv_b, m_ssd_dt_bias, m_ssd_a_log, m_ssd_d, m_ssd_gnorm_w, m_ssd_out_w, m_sb_in_w, m_sb_qn_w, m_sb_kn_w, m_sb_out_w, m_ple_norm_w, m_ple_gate_w, m_ple_proj_w, v_norm_w, v_ssd_in_w, v_ssd_conv_w, v_ssd_conv_b, v_ssd_dt_bias, v_ssd_a_log, v_ssd_d, v_ssd_gnorm_w, v_ssd_out_w, v_sb_in_w, v_sb_qn_w, v_sb_kn_w, v_sb_out_w, v_ple_norm_w, v_ple_gate_w, v_ple_proj_w):
    env = dict(locals())
    wts = {n: env[n] for n in WEIGHT_NAMES}
    mom1 = {n: env["m_" + n] for n in WEIGHT_NAMES}
    mom2 = {n: env["v_" + n] for n in WEIGHT_NAMES}

    s, d = x.shape[1], x.shape[2]
    depth = norm_w.shape[0]
    di = ssd_out_w.shape[1] * N_DEV
    n_heads = ssd_dt_bias.shape[1]
    hpg = n_heads // SSD_N_GROUPS
    nbc = SSD_N_GROUPS * SSD_D_STATE
    in_segs = (di, di, nbc, nbc, n_heads)
    conv_segs = (di, nbc, nbc)
    sb_w = sb_out_w.shape[1] * N_DEV
    selectors = ssd_selectors(hpg)
    xi, yi, ci = _mesh_pos()
    core = ci.astype(jnp.int32).reshape(1)
    chip = (2 * xi + yi).astype(jnp.int32).reshape(1)

    def layer_keys(i):
        j = i // 2
        mixer = [("ssd_in_w", j), ("ssd_conv_w", j), ("ssd_out_w", j)] if i % 2 == 0 else [("sb_in_w", j), ("sb_out_w", j)]
        return mixer + [("ple_gate_w", i), ("ple_proj_w", i)]

    me_block = 4 * xi + 2 * yi + ci

    def landing_zone(t):
        return lax.dynamic_update_index_in_dim(lax.empty((N_DEV,) + t.shape, t.dtype), t, me_block, 0)

    def groups(i):
        keys = layer_keys(i)
        return [keys[:2], keys[2:]] if i == 0 else [keys]

    gathers, prev = {}, []
    for i in range(depth):
        for q, keys in enumerate(groups(i)):
            shards = [wts[n][idx] for n, idx in keys]
            if prev:
                shards = lax.optimization_barrier((prev[0], shards))[1]
            lands = [landing_zone(t if n == "ssd_conv_w" else t.astype(BF16)) for (n, _), t in zip(keys, shards)]
            gathers[i, q] = ici_start("gather", [], lands, after=prev, name=f"ag{i}{'ab'[q]}_start")
            prev = [gathers[i, q][4]]
    all_started = prev[0]
    full, ssd_full, passing = {}, {}, {}

    def hand_over(i, q, after):
        _, lands = ici_wait("gather", gathers[i, q], after, name=f"ag{i}{'ab'[q]}_wait")
        passing[i, q] = ici_start("pass", [], [t.reshape(N_CHIP, 2, *t.shape[1:]) for t in lands],
                                  name=f"ag{i}{'ab'[q]}_pass_start")

    def arrive(i, q, after):
        _, lands = ici_wait("pass", passing[i, q], after, name=f"ag{i}{'ab'[q]}_pass_wait")
        for k, t in zip(groups(i)[q], lands):
            full[k] = t.reshape(N_DEV, *t.shape[2:])

    def w_out_of(i):
        return full["ssd_out_w", i // 2].reshape(di, d) if i % 2 == 0 else full["sb_out_w", i // 2].reshape(sb_w, d)

    h = x.reshape(s, d)
    saved = []
    hand_over(0, 0, [all_started])
    arrive(0, 0, [all_started])
    for i in range(depth):
        j = i // 2
        if i > 0:
            arrive(i, 0, [h])
        sv = dict(h_in=h)
        u = rmsnorm_fwd(h, norm_w[i], name=f"l{i}_norm")
        sv["u"] = u
        if i % 2 == 0:
            fw = ssd_full[j] = dict(
                w_in=_split_cols(_from_col_blocks(full["ssd_in_w", j]), in_segs),
                conv_w=_split_cols(_from_col_blocks(full["ssd_conv_w", j]), conv_segs),
                conv_b=_split_cols(ssd_conv_b[j].reshape(1, -1), conv_segs))
            raw = [matmul(u, wseg, name=f"l{i}_in{q}") for q, wseg in enumerate(fw["w_in"])]
            if i == 0:
                hand_over(0, 1, [raw[4]])
            z, dt_raw = raw[0], raw[4]
            act = [ssd_conv_fwd(raw[1 + q], fw["conv_w"][q], fw["conv_b"][q], name=f"l{i}_conv{q}") for q in range(3)]
            dt, a_neg = ssd_dt_fwd(dt_raw, ssd_dt_bias[j], ssd_a_log[j], name=f"l{i}_dt")
            dtp = _to_group_lanes(dt, hpg)
            a_g = _head_vec(a_neg.reshape(-1), hpg)
            d_x = jnp.repeat(ssd_d[j].reshape(SSD_N_GROUPS, 1, hpg), SSD_HEAD_DIM, axis=2)
            y, states = ssd_scan_fwd(act[0], act[1], act[2], dtp, a_g, d_x, selectors, heads_per_group=hpg,
                                     name=f"l{i}_scan")
            yn = ssd_gate_fwd(y, z, ssd_gnorm_w[j], name=f"l{i}_gate")
            if i == 0:
                arrive(0, 1, [yn])
            h1 = matmul(yn, w_out_of(i), res=h, name=f"l{i}_out")
            sv.update(raw=raw, act=act, dt=dt, dtp=dtp, a_g=a_g, d_x=d_x, y=y, states=states, yn=yn)
        else:
            proj = matmul(u, full["sb_in_w", j], name=f"l{i}_in")
            qn, kn = sb_qk_fwd(proj, sb_qn_w[j], sb_kn_w[j], name=f"l{i}_qknorm")
            v_off = 2 * sb_w // SB_HEAD_DIM
            o, tot, og = sb_attn_fwd(qn, kn, proj, proj, v_off=v_off, g_off=3 * sb_w // SB_HEAD_DIM, name=f"l{i}_attn")
            h1 = matmul(og, w_out_of(i), res=h, name=f"l{i}_out")
            sv.update(proj=proj, qn=qn, kn=kn, o=o, tot=tot, og=og, v_off=v_off)
        if i + 1 < depth:
            hand_over(i + 1, 0, [h1])
        t = rmsnorm_fwd(h1, ple_norm_w[i], passing[i + 1, 0][4] if i + 1 < depth else None, name=f"l{i}_plenorm")
        gate_pre = matmul(t, full["ple_gate_w", i].reshape(d, d), name=f"l{i}_plegate")
        pp = matmul(p[i, 0], full["ple_proj_w", i], name=f"l{i}_pleproj")
        h = ple_fwd(h1, gate_pre, pp, name=f"l{i}_ple")
        sv.update(h1=h1, t=t, gate_pre=gate_pre, pp=pp)
        saved.append(sv)

    loss_part, dh = loss_head(h, loss_target.reshape(s, d), name="loss_head")
    loss = lax.psum(loss_part[0, 0], ("x", "y", "c"))

    big = {}
    small = {n: [None] * wts[n].shape[0] for n in REPLICATED}
    swaps, scatters = {}, {}
    order_after = jnp.zeros(TOKEN_SHAPE, F32)
    pending = None

    def send_to_sibling(i, q):
        blocks = [big[k].reshape(N_CHIP, 2, *big[k].shape[1:]) for k in groups(i)[::-1][q]]
        swaps[i, q] = ici_start("swap", blocks, [lax.empty((N_CHIP,) + t.shape[2:], t.dtype) for t in blocks],
                                name=f"rs{i}{'ab'[q]}_swap_start")
        return swaps[i, q][4]

    def send_to_chips(i, q, after):
        blocks, from_sibling = ici_wait("swap", swaps[i, q], after, name=f"rs{i}{'ab'[q]}_swap_wait")
        sums = [pair_add(g, r1, core, name=f"rs{i}{'ab'[q]}_pair_add{a}")
                for a, (g, r1) in enumerate(zip(blocks, from_sibling))]
        scatters[i, q] = ici_start("scatter", sums, [lax.empty((N_PEER_CHIPS,) + t.shape[1:], t.dtype) for t in sums],
                                   name=f"rs{i}{'ab'[q]}_start")
        return scatters[i, q][4]

    for i in reversed(range(depth)):
        j = i // 2
        sv = saved[i]
        dpp, dgp = ple_bwd(dh, sv["gate_pre"], sv["pp"], order_after, name=f"b{i}_ple")
        big["ple_proj_w", i] = matmul(p[i, 0], dpp, mode="tn", out_dtype=BF16, out_blocks=ple_proj_w.shape[2],
                                      name=f"b{i}_pleproj_w")
        big["ple_gate_w", i] = matmul(sv["t"], dgp, mode="tn", out_dtype=BF16, name=f"b{i}_plegate_w").reshape(N_DEV, -1, d)
        dt_ = matmul(dgp, full["ple_gate_w", i].reshape(d, d), mode="nt", name=f"b{i}_plegate_x")
        dh1, g_pn = rmsnorm_bwd(sv["h1"], ple_norm_w[i], dt_, dh, name=f"b{i}_plenorm")
        small["ple_norm_w"][i] = g_pn
        behind = send_to_chips(*pending, [dh1]) if pending is not None else None
        pending = None
        u = sv["u"]
        if i % 2 == 0:
            fw = ssd_full[j]
            raw, act = sv["raw"], sv["act"]
            big["ssd_out_w", j] = matmul(sv["yn"], dh1, mode="tn", out_dtype=BF16, name=f"b{i}_out_w").reshape(N_DEV, -1, d)
            if i == 0:
                send_to_sibling(0, 0)
            dyn = matmul(dh1, w_out_of(i), mode="nt", after=behind, name=f"b{i}_out_x")
            dy, dz, g_gn = ssd_gate_bwd(sv["y"], raw[0], ssd_gnorm_w[j], dyn, name=f"b{i}_gate")
            dxs, dbm, dcm, ddtp, dadtp, dd_g = ssd_scan_bwd(act[0], act[1], act[2], sv["dtp"], sv["a_g"], sv["d_x"], selectors,
                                                          sv["states"], dy, heads_per_group=hpg, name=f"b{i}_scan")
            behind = send_to_chips(0, 0, [dxs]) if i == 0 else None
            ddt_raw, g_dtb, g_alog = ssd_dt_bwd(raw[4], ssd_dt_bias[j], ssd_a_log[j], sv["dt"],
                                                _from_group_lanes(ddtp, hpg), _from_group_lanes(dadtp, hpg),
                                                name=f"b{i}_dt")
            conv_back = [ssd_conv_bwd(raw[1 + q], fw["conv_w"][q], fw["conv_b"][q], dact, name=f"b{i}_conv{q}")
                         for q, dact in enumerate((dxs, dbm, dcm))]
            dsegs = [dz] + [cb[0] for cb in conv_back] + [ddt_raw]
            g_in = jnp.concatenate([matmul(u, ds, mode="tn", out_dtype=BF16, name=f"b{i}_in{q}_w")
                                    for q, ds in enumerate(dsegs)], axis=1)
            big["ssd_in_w", j] = _col_blocks(g_in)
            big["ssd_conv_w", j] = _col_blocks(jnp.concatenate([cb[1] for cb in conv_back], axis=1))
            du = None
            for q, (ds, wseg) in enumerate(zip(dsegs, fw["w_in"])):
                du = matmul(ds, wseg, mode="nt", res=du, after=behind if q == 0 else None, name=f"b{i}_in{q}_x")
            small["ssd_conv_b"][j] = jnp.concatenate([cb[2] for cb in conv_back], axis=1)
            small["ssd_dt_bias"][j] = g_dtb
            small["ssd_a_log"][j] = g_alog
            small["ssd_d"][j] = dd_g[:, 0, :hpg]
            small["ssd_gnorm_w"][j] = g_gn
        else:
            proj = sv["proj"]
            big["sb_out_w", j] = matmul(sv["og"], dh1, mode="tn", out_dtype=BF16, name=f"b{i}_out_w").reshape(N_DEV, -1, d)
            dog = matmul(dh1, w_out_of(i), mode="nt", after=behind, name=f"b{i}_out_x")
            do, dg = sb_gate_bwd(dog, sv["o"], proj, name=f"b{i}_gate")
            dqn, dkn, dv = sb_attn_bwd(sv["qn"], sv["kn"], proj, sv["tot"], do, v_off=sv["v_off"], name=f"b{i}_attn")
            dproj, g_qn, g_kn = sb_pack_bwd(proj, sb_qn_w[j], sb_kn_w[j], dqn, dkn, dv, dg, name=f"b{i}_qknorm")
            big["sb_in_w", j] = matmul(u, dproj, mode="tn", out_dtype=BF16, out_blocks=sb_in_w.shape[2], name=f"b{i}_in_w")
            du = matmul(dproj, full["sb_in_w", j], mode="nt", name=f"b{i}_in_x")
            small["sb_qn_w"][j] = g_qn
            small["sb_kn_w"][j] = g_kn
        dh, g_n = rmsnorm_bwd(sv["h_in"], norm_w[i], du, dh1, name=f"b{i}_norm")
        small["norm_w"][i] = g_n
        pending = (i, len(groups(i)) - 1)
        order_after = send_to_sibling(*pending)
    send_to_chips(*pending, [dh])
    grad_x = dh.reshape(x.shape)

    rep_like = [wts[n] for n in REPLICATED]
    g_packed = all_reduce_small(_pack([jnp.stack([t.reshape(-1) for t in small[n]]) for n in REPLICATED]),
                                name="all_reduce_small_grads")
    d_packed, m_packed, v_packed = adamw_replicated(
        _pack(rep_like), _pack([mom1[n] for n in REPLICATED]), _pack([mom2[n] for n in REPLICATED]), g_packed,
        name="adamw_replicated")
    grads = dict(zip(REPLICATED, _unpack(g_packed, rep_like)))
    deltas = dict(zip(REPLICATED, _unpack(d_packed, rep_like)))
    new_m = dict(zip(REPLICATED, _unpack(m_packed, rep_like)))
    new_v = dict(zip(REPLICATED, _unpack(v_packed, rep_like)))

    updated = {}
    after = [scatters[0, len(groups(0)) - 1][4]]
    for i in reversed(range(depth)):
        for q, keys in enumerate(groups(i)[::-1]):
            sums, received = ici_wait("scatter", scatters[i, q], after, name=f"rs{i}{'ab'[q]}_wait")
            after = []
            for (n, idx), t_sum, recv in zip(keys, sums, received):
                updated[n], done = adamw_sharded(wts[n], mom1[n], mom2[n], idx, t_sum, recv, chip, updated.get(n),
                                                 name=f"adamw_{n}{idx}")
                after.append(done)
    for n, (g_n, d_n, m_n, v_n) in updated.items():
        grads[n], deltas[n], new_m[n], new_v[n] = g_n, d_n, m_n, v_n

    return (loss, grad_x, *[grads[n] for n in WEIGHT_NAMES], *[deltas[n] for n in WEIGHT_NAMES],
            *[new_m[n] for n in WEIGHT_NAMES], *[new_v[n] for n in WEIGHT_NAMES])
```
